```python
import jax
import jax.numpy as jnp
from jax import lax
import numpy as np

D_MODEL = 2048
BATCH = 8
SEQ = 4096
DEPTH = 2

GRID_W = 64
CTX_LEN = 256
N_EVEN = (DEPTH + 1) // 2
N_ODD = DEPTH // 2
EPS = 1e-6
N_MOD = 6

CHUNK = 128
A_HEADS = 8
A_HEAD_DIM = 128
A_WIDTH = A_HEADS * A_HEAD_DIM
B_WIDTH = 1024
B_CONV = 31
MIX_IN = 2 * A_WIDTH + 2 * B_WIDTH
MIX_OUT = A_WIDTH + B_WIDTH

MLA_HEADS = 16
Q_LORA = 768
KV_LORA = 512
QK_NOPE = 128
QK_ROPE = 64
V_DIM = 128
ROPE_THETA = 10000.0
Q_BLOCK = 128
MLA_IN = Q_LORA + KV_LORA + QK_ROPE

D_FF = 5632
FFN_CONV = 3

kernel_name = 'hybrid_gmlp_conformer_mla_dit_block'


def rmsnorm(x, g):
    xf = x.astype(jnp.float32)
    y = xf * lax.rsqrt(jnp.mean(xf * xf, axis=-1, keepdims=True) + EPS)
    return (y * g.astype(jnp.float32)).astype(x.dtype)


def layernorm(x, g, b):
    xf = x.astype(jnp.float32)
    mu = jnp.mean(xf, axis=-1, keepdims=True)
    var = jnp.mean(jnp.square(xf - mu), axis=-1, keepdims=True)
    y = (xf - mu) * lax.rsqrt(var + EPS)
    return (y * g.astype(jnp.float32) + b.astype(jnp.float32)).astype(x.dtype)


def modulate(h, shift, scale):
    return h * (1 + scale) + shift


def adaln(cvec, w, b):
    m = jax.nn.silu(cvec) @ w + b
    return jnp.split(m, N_MOD, axis=-1)


def depthwise_conv(x, w, b):
    pad = (w.shape[0] - 1) // 2
    y = lax.conv_general_dilated(x, w[:, None, :].astype(x.dtype), window_strides=(1,), padding=[(pad, pad)], dimension_numbers=('NWC', 'WIO', 'NWC'), feature_group_count=x.shape[-1])
    return y + b.astype(x.dtype)


def chunk_gmlp(z, ln_g, ln_b, w_s, b_s):
    z = jax.nn.gelu(z)
    u, v = z[..., :A_WIDTH], z[..., A_WIDTH:]
    v = layernorm(v, ln_g, ln_b)
    bn, L, _ = v.shape
    v = v.reshape(bn, L // CHUNK, CHUNK, A_HEADS, A_HEAD_DIM)
    v = jnp.einsum('hij,bnjhd->bnihd', w_s.astype(v.dtype), v) + b_s.T.astype(v.dtype)[:, :, None]
    return u * v.reshape(bn, L, A_WIDTH)


def conformer_conv(z, conv_w, conv_b, ln_g, ln_b):
    a, g = z[..., :B_WIDTH], z[..., B_WIDTH:]
    h = a * jax.nn.sigmoid(g)
    h = depthwise_conv(h, conv_w, conv_b)
    h = layernorm(h, ln_g, ln_b)
    return jax.nn.silu(h)


def ab_mixer(h, w_in, b_in, a_ln_g, a_ln_b, a_w_s, a_b_s, b_conv_w, b_conv_b, b_ln_g, b_ln_b, w_out):
    z = h @ w_in + b_in
    ya = chunk_gmlp(z[..., :2 * A_WIDTH], a_ln_g, a_ln_b, a_w_s, a_b_s)
    yb = conformer_conv(z[..., 2 * A_WIDTH:], b_conv_w, b_conv_b, b_ln_g, b_ln_b)
    return jnp.concatenate([ya, yb], axis=-1) @ w_out


def conv_ffn(h, w_up, conv_w, conv_b, w_down):
    z = h @ w_up
    g, u = z[..., :D_FF], z[..., D_FF:]
    g = depthwise_conv(g, conv_w, conv_b)
    return (jax.nn.silu(g) * u) @ w_down


def axial_rope(L):
    rows = L // GRID_W
    row = jnp.repeat(jnp.arange(rows, dtype=jnp.float32), GRID_W)
    col = jnp.tile(jnp.arange(GRID_W, dtype=jnp.float32), rows)
    n_freq = QK_ROPE // 4
    inv = ROPE_THETA ** (-jnp.arange(n_freq, dtype=jnp.float32) / n_freq)
    ang = jnp.concatenate([row[:, None] * inv, col[:, None] * inv], axis=-1)
    return jnp.cos(ang), jnp.sin(ang)


def apply_rope(x, cos, sin):
    half = x.shape[-1] // 2
    x1, x2 = x[..., :half], x[..., half:]
    cos = cos.astype(x.dtype)
    sin = sin.astype(x.dtype)
    return jnp.concatenate([x1 * cos - x2 * sin, x2 * cos + x1 * sin], axis=-1)


def mla_queries(cq, q_norm_g, w_uq, cos, sin):
    bn, L, _ = cq.shape
    q = (rmsnorm(cq, q_norm_g) @ w_uq).reshape(bn, L, MLA_HEADS, QK_NOPE + QK_ROPE)
    q_nope, q_pe = q[..., :QK_NOPE], q[..., QK_NOPE:]
    if cos is not None:
        q_pe = apply_rope(q_pe, cos[None, :, None, :], sin[None, :, None, :])
    return q_nope, q_pe


def mla_keys_values(ckv, k_pe, kv_norm_g, w_ukv, cos, sin):
    bn, L, _ = ckv.shape
    kv = (rmsnorm(ckv, kv_norm_g) @ w_ukv).reshape(bn, L, MLA_HEADS, QK_NOPE + V_DIM)
    k_nope, v = kv[..., :QK_NOPE], kv[..., QK_NOPE:]
    if cos is not None:
        k_pe = apply_rope(k_pe, cos[None], sin[None])
    return k_nope, k_pe, v


def attend(q_nope, q_pe, k_nope, k_pe, v):
    bn, lq, _, _ = q_nope.shape
    nb = lq // Q_BLOCK
    qn = q_nope.reshape(bn, nb, Q_BLOCK, MLA_HEADS, QK_NOPE).transpose(1, 0, 2, 3, 4)
    qp = q_pe.reshape(bn, nb, Q_BLOCK, MLA_HEADS, QK_ROPE).transpose(1, 0, 2, 3, 4)
    scale = (QK_NOPE + QK_ROPE) ** -0.5

    def block(args):
        qn_b, qp_b = args
        s = jnp.einsum('bqhd,bkhd->bhqk', qn_b, k_nope) + jnp.einsum('bqhd,bkd->bhqk', qp_b, k_pe)
        p = jax.nn.softmax(s.astype(jnp.float32) * scale, axis=-1).astype(v.dtype)
        return jnp.einsum('bhqk,bkhd->bqhd', p, v)

    o = lax.map(block, (qn, qp))
    return o.transpose(1, 0, 2, 3, 4).reshape(bn, lq, MLA_HEADS * V_DIM)


def mla_mixer(h_lat, h_ctx, w_in, q_norm_g, w_uq, kv_norm_g, w_ukv, w_o, cos, sin, ctx_out):
    z = h_lat @ w_in
    cq, ckv, kpe = z[..., :Q_LORA], z[..., Q_LORA:Q_LORA + KV_LORA], z[..., Q_LORA + KV_LORA:]
    zc = h_ctx @ w_in[:, Q_LORA:]
    ckv_c, kpe_c = zc[..., :KV_LORA], zc[..., KV_LORA:]
    kn_l, kp_l, v_l = mla_keys_values(ckv, kpe, kv_norm_g, w_ukv, cos, sin)
    kn_c, kp_c, v_c = mla_keys_values(ckv_c, kpe_c, kv_norm_g, w_ukv, None, None)
    qn, qp = mla_queries(cq, q_norm_g, w_uq, cos, sin)
    o_lat = attend(qn, qp, jnp.concatenate([kn_l, kn_c], axis=1), jnp.concatenate([kp_l, kp_c], axis=1), jnp.concatenate([v_l, v_c], axis=1)) @ w_o
    o_ctx = None
    if ctx_out:
        qn_c, qp_c = mla_queries(h_ctx @ w_in[:, :Q_LORA], q_norm_g, w_uq, None, None)
        o_ctx = attend(qn_c, qp_c, kn_c, kp_c, v_c) @ w_o
    return o_lat, o_ctx


def _fwd_setup_inputs(seed: int = 0) -> dict:
    key = jax.random.key(seed)
    ks = iter(jax.random.split(key, 40))
    f32 = jnp.float32

    def nrm(shape, scale):
        return jax.random.normal(next(ks), shape, f32) * scale

    def gain(shape):
        return 1.0 + nrm(shape, 0.02)

    return {
        'x': nrm((BATCH, SEQ, D_MODEL), 1.0),
        'c': nrm((BATCH, D_MODEL), 1.0),
        'ctx': nrm((BATCH, CTX_LEN, D_MODEL), 1.0),
        'c_ctx': nrm((D_MODEL,), 1.0),
        'norm1_g': gain((DEPTH, D_MODEL)),
        'norm2_g': gain((DEPTH, D_MODEL)),
        'w_ada': nrm((DEPTH, D_MODEL, N_MOD * D_MODEL), D_MODEL ** -0.5),
        'b_ada': nrm((DEPTH, N_MOD * D_MODEL), 0.02),
        'ab_w_in': nrm((N_EVEN, D_MODEL, MIX_IN), D_MODEL ** -0.5),
        'ab_b_in': nrm((N_EVEN, MIX_IN), 0.02),
        'a_ln_g': gain((N_EVEN, A_WIDTH)),
        'a_ln_b': nrm((N_EVEN, A_WIDTH), 0.02),
        'a_w_s': nrm((N_EVEN, A_HEADS, CHUNK, CHUNK), CHUNK ** -0.5),
        'a_b_s': gain((N_EVEN, A_HEADS, CHUNK)),
        'b_conv_w': nrm((N_EVEN, B_CONV, B_WIDTH), B_CONV ** -0.5),
        'b_conv_b': nrm((N_EVEN, B_WIDTH), 0.02),
        'b_ln_g': gain((N_EVEN, B_WIDTH)),
        'b_ln_b': nrm((N_EVEN, B_WIDTH), 0.02),
        'ab_w_out': nrm((N_EVEN, MIX_OUT, D_MODEL), MIX_OUT ** -0.5),
        'mla_w_in': nrm((N_ODD, D_MODEL, MLA_IN), D_MODEL ** -0.5),
        'mla_q_norm_g': gain((N_ODD, Q_LORA)),
        'mla_w_uq': nrm((N_ODD, Q_LORA, MLA_HEADS * (QK_NOPE + QK_ROPE)), Q_LORA ** -0.5),
        'mla_kv_norm_g': gain((N_ODD, KV_LORA)),
        'mla_w_ukv': nrm((N_ODD, KV_LORA, MLA_HEADS * (QK_NOPE + V_DIM)), KV_LORA ** -0.5),
        'mla_w_o': nrm((N_ODD, MLA_HEADS * V_DIM, D_MODEL), (MLA_HEADS * V_DIM) ** -0.5),
        'ffn_w_up': nrm((DEPTH, D_MODEL, 2 * D_FF), D_MODEL ** -0.5),
        'ffn_conv_w': nrm((DEPTH, FFN_CONV, D_FF), FFN_CONV ** -0.5),
        'ffn_conv_b': nrm((DEPTH, D_FF), 0.02),
        'ffn_w_down': nrm((DEPTH, D_FF, D_MODEL), D_FF ** -0.5),
        'final_norm_g': gain((D_MODEL,)),
    }


def _fwd_reference(x, c, ctx, c_ctx, norm1_g, norm2_g, w_ada, b_ada, ab_w_in, ab_b_in, a_ln_g, a_ln_b, a_w_s, a_b_s, b_conv_w, b_conv_b, b_ln_g, b_ln_b, ab_w_out, mla_w_in, mla_q_norm_g, mla_w_uq, mla_kv_norm_g, mla_w_ukv, mla_w_o, ffn_w_up, ffn_conv_w, ffn_conv_b, ffn_w_down, final_norm_g):
    L = x.shape[1]
    cos, sin = axial_rope(L)
    xl, xc = x, ctx
    for i in range(DEPTH):
        last = i == DEPTH - 1
        even = i % 2 == 0
        j = i // 2
        ctx_in = (not last) or (not even)
        ctx_update = not last
        sh1, sc1, g1, sh2, sc2, g2 = [m[:, None, :] for m in adaln(c, w_ada[i], b_ada[i])]
        if ctx_in:
            csh1, csc1, cg1, csh2, csc2, cg2 = adaln(c_ctx, w_ada[i], b_ada[i])
        hl = modulate(rmsnorm(xl, norm1_g[i]), sh1, sc1)
        if even:
            ab_args = (ab_w_in[j], ab_b_in[j], a_ln_g[j], a_ln_b[j], a_w_s[j], a_b_s[j], b_conv_w[j], b_conv_b[j], b_ln_g[j], b_ln_b[j], ab_w_out[j])
            yl = ab_mixer(hl, *ab_args)
            if ctx_update:
                hc = modulate(rmsnorm(xc, norm1_g[i]), csh1, csc1)
                xc = xc + cg1 * ab_mixer(hc, *ab_args)
        else:
            hc = modulate(rmsnorm(xc, norm1_g[i]), csh1, csc1)
            yl, yc = mla_mixer(hl, hc, mla_w_in[j], mla_q_norm_g[j], mla_w_uq[j], mla_kv_norm_g[j], mla_w_ukv[j], mla_w_o[j], cos, sin, ctx_update)
            if ctx_update:
                xc = xc + cg1 * yc
        xl = xl + g1 * yl
        xl = xl + g2 * conv_ffn(modulate(rmsnorm(xl, norm2_g[i]), sh2, sc2), ffn_w_up[i], ffn_conv_w[i], ffn_conv_b[i], ffn_w_down[i])
        if ctx_update:
            xc = xc + cg2 * conv_ffn(modulate(rmsnorm(xc, norm2_g[i]), csh2, csc2), ffn_w_up[i], ffn_conv_w[i], ffn_conv_b[i], ffn_w_down[i])
    return rmsnorm(xl, final_norm_g)


import jax as _jax
import jax.numpy as _jnp

TWIN_FORMAT = 'train_step'
FWD_PARAMS = ['x', 'c', 'ctx', 'c_ctx', 'norm1_g', 'norm2_g', 'w_ada', 'b_ada', 'ab_w_in', 'ab_b_in', 'a_ln_g', 'a_ln_b', 'a_w_s', 'a_b_s', 'b_conv_w', 'b_conv_b', 'b_ln_g', 'b_ln_b', 'ab_w_out', 'mla_w_in', 'mla_q_norm_g', 'mla_w_uq', 'mla_kv_norm_g', 'mla_w_ukv', 'mla_w_o', 'ffn_w_up', 'ffn_conv_w', 'ffn_conv_b', 'ffn_w_down', 'final_norm_g']
TWIN_WEIGHTS = ['c_ctx', 'norm1_g', 'norm2_g', 'w_ada', 'b_ada', 'ab_w_in', 'ab_b_in', 'a_ln_g', 'a_ln_b', 'a_w_s', 'a_b_s', 'b_conv_w', 'b_conv_b', 'b_ln_g', 'b_ln_b', 'ab_w_out', 'mla_w_in', 'mla_q_norm_g', 'mla_w_uq', 'mla_kv_norm_g', 'mla_w_ukv', 'mla_w_o', 'ffn_w_up', 'ffn_conv_w', 'ffn_conv_b', 'ffn_w_down', 'final_norm_g']
TWIN_DIFF_INPUT = 'x'
TWIN_INPUTS = ['x', 'c', 'ctx', 'c_ctx', 'norm1_g', 'norm2_g', 'w_ada', 'b_ada', 'ab_w_in', 'ab_b_in', 'a_ln_g', 'a_ln_b', 'a_w_s', 'a_b_s', 'b_conv_w', 'b_conv_b', 'b_ln_g', 'b_ln_b', 'ab_w_out', 'mla_w_in', 'mla_q_norm_g', 'mla_w_uq', 'mla_kv_norm_g', 'mla_w_ukv', 'mla_w_o', 'ffn_w_up', 'ffn_conv_w', 'ffn_conv_b', 'ffn_w_down', 'final_norm_g', 'loss_target', 'm_c_ctx', 'm_norm1_g', 'm_norm2_g', 'm_w_ada', 'm_b_ada', 'm_ab_w_in', 'm_ab_b_in', 'm_a_ln_g', 'm_a_ln_b', 'm_a_w_s', 'm_a_b_s', 'm_b_conv_w', 'm_b_conv_b', 'm_b_ln_g', 'm_b_ln_b', 'm_ab_w_out', 'm_mla_w_in', 'm_mla_q_norm_g', 'm_mla_w_uq', 'm_mla_kv_norm_g', 'm_mla_w_ukv', 'm_mla_w_o', 'm_ffn_w_up', 'm_ffn_conv_w', 'm_ffn_conv_b', 'm_ffn_w_down', 'm_final_norm_g', 'v_c_ctx', 'v_norm1_g', 'v_norm2_g', 'v_w_ada', 'v_b_ada', 'v_ab_w_in', 'v_ab_b_in', 'v_a_ln_g', 'v_a_ln_b', 'v_a_w_s', 'v_a_b_s', 'v_b_conv_w', 'v_b_conv_b', 'v_b_ln_g', 'v_b_ln_b', 'v_ab_w_out', 'v_mla_w_in', 'v_mla_q_norm_g', 'v_mla_w_uq', 'v_mla_kv_norm_g', 'v_mla_w_ukv', 'v_mla_w_o', 'v_ffn_w_up', 'v_ffn_conv_w', 'v_ffn_conv_b', 'v_ffn_w_down', 'v_final_norm_g']
TWIN_OUTPUTS = ['loss', 'grad_x', 'grad_c_ctx', 'grad_norm1_g', 'grad_norm2_g', 'grad_w_ada', 'grad_b_ada', 'grad_ab_w_in', 'grad_ab_b_in', 'grad_a_ln_g', 'grad_a_ln_b', 'grad_a_w_s', 'grad_a_b_s', 'grad_b_conv_w', 'grad_b_conv_b', 'grad_b_ln_g', 'grad_b_ln_b', 'grad_ab_w_out', 'grad_mla_w_in', 'grad_mla_q_norm_g', 'grad_mla_w_uq', 'grad_mla_kv_norm_g', 'grad_mla_w_ukv', 'grad_mla_w_o', 'grad_ffn_w_up', 'grad_ffn_conv_w', 'grad_ffn_conv_b', 'grad_ffn_w_down', 'grad_final_norm_g', 'delta_c_ctx', 'delta_norm1_g', 'delta_norm2_g', 'delta_w_ada', 'delta_b_ada', 'delta_ab_w_in', 'delta_ab_b_in', 'delta_a_ln_g', 'delta_a_ln_b', 'delta_a_w_s', 'delta_a_b_s', 'delta_b_conv_w', 'delta_b_conv_b', 'delta_b_ln_g', 'delta_b_ln_b', 'delta_ab_w_out', 'delta_mla_w_in', 'delta_mla_q_norm_g', 'delta_mla_w_uq', 'delta_mla_kv_norm_g', 'delta_mla_w_ukv', 'delta_mla_w_o', 'delta_ffn_w_up', 'delta_ffn_conv_w', 'delta_ffn_conv_b', 'delta_ffn_w_down', 'delta_final_norm_g', 'new_m_c_ctx', 'new_m_norm1_g', 'new_m_norm2_g', 'new_m_w_ada', 'new_m_b_ada', 'new_m_ab_w_in', 'new_m_ab_b_in', 'new_m_a_ln_g', 'new_m_a_ln_b', 'new_m_a_w_s', 'new_m_a_b_s', 'new_m_b_conv_w', 'new_m_b_conv_b', 'new_m_b_ln_g', 'new_m_b_ln_b', 'new_m_ab_w_out', 'new_m_mla_w_in', 'new_m_mla_q_norm_g', 'new_m_mla_w_uq', 'new_m_mla_kv_norm_g', 'new_m_mla_w_ukv', 'new_m_mla_w_o', 'new_m_ffn_w_up', 'new_m_ffn_conv_w', 'new_m_ffn_conv_b', 'new_m_ffn_w_down', 'new_m_final_norm_g', 'new_v_c_ctx', 'new_v_norm1_g', 'new_v_norm2_g', 'new_v_w_ada', 'new_v_b_ada', 'new_v_ab_w_in', 'new_v_ab_b_in', 'new_v_a_ln_g', 'new_v_a_ln_b', 'new_v_a_w_s', 'new_v_a_b_s', 'new_v_b_conv_w', 'new_v_b_conv_b', 'new_v_b_ln_g', 'new_v_b_ln_b', 'new_v_ab_w_out', 'new_v_mla_w_in', 'new_v_mla_q_norm_g', 'new_v_mla_w_uq', 'new_v_mla_kv_norm_g', 'new_v_mla_w_ukv', 'new_v_mla_w_o', 'new_v_ffn_w_up', 'new_v_ffn_conv_w', 'new_v_ffn_conv_b', 'new_v_ffn_w_down', 'new_v_final_norm_g']
TWIN_LEAF_KINDS = {'loss': 'loss', 'grad_x': 'grad_x', 'grad_c_ctx': 'grad_w', 'grad_norm1_g': 'grad_w', 'grad_norm2_g': 'grad_w', 'grad_w_ada': 'grad_w', 'grad_b_ada': 'grad_w', 'grad_ab_w_in': 'grad_w', 'grad_ab_b_in': 'grad_w', 'grad_a_ln_g': 'grad_w', 'grad_a_ln_b': 'grad_w', 'grad_a_w_s': 'grad_w', 'grad_a_b_s': 'grad_w', 'grad_b_conv_w': 'grad_w', 'grad_b_conv_b': 'grad_w', 'grad_b_ln_g': 'grad_w', 'grad_b_ln_b': 'grad_w', 'grad_ab_w_out': 'grad_w', 'grad_mla_w_in': 'grad_w', 'grad_mla_q_norm_g': 'grad_w', 'grad_mla_w_uq': 'grad_w', 'grad_mla_kv_norm_g': 'grad_w', 'grad_mla_w_ukv': 'grad_w', 'grad_mla_w_o': 'grad_w', 'grad_ffn_w_up': 'grad_w', 'grad_ffn_conv_w': 'grad_w', 'grad_ffn_conv_b': 'grad_w', 'grad_ffn_w_down': 'grad_w', 'grad_final_norm_g': 'grad_w', 'delta_c_ctx': 'delta_w', 'delta_norm1_g': 'delta_w', 'delta_norm2_g': 'delta_w', 'delta_w_ada': 'delta_w', 'delta_b_ada': 'delta_w', 'delta_ab_w_in': 'delta_w', 'delta_ab_b_in': 'delta_w', 'delta_a_ln_g': 'delta_w', 'delta_a_ln_b': 'delta_w', 'delta_a_w_s': 'delta_w', 'delta_a_b_s': 'delta_w', 'delta_b_conv_w': 'delta_w', 'delta_b_conv_b': 'delta_w', 'delta_b_ln_g': 'delta_w', 'delta_b_ln_b': 'delta_w', 'delta_ab_w_out': 'delta_w', 'delta_mla_w_in': 'delta_w', 'delta_mla_q_norm_g': 'delta_w', 'delta_mla_w_uq': 'delta_w', 'delta_mla_kv_norm_g': 'delta_w', 'delta_mla_w_ukv': 'delta_w', 'delta_mla_w_o': 'delta_w', 'delta_ffn_w_up': 'delta_w', 'delta_ffn_conv_w': 'delta_w', 'delta_ffn_conv_b': 'delta_w', 'delta_ffn_w_down': 'delta_w', 'delta_final_norm_g': 'delta_w', 'new_m_c_ctx': 'new_m', 'new_m_norm1_g': 'new_m', 'new_m_norm2_g': 'new_m', 'new_m_w_ada': 'new_m', 'new_m_b_ada': 'new_m', 'new_m_ab_w_in': 'new_m', 'new_m_ab_b_in': 'new_m', 'new_m_a_ln_g': 'new_m', 'new_m_a_ln_b': 'new_m', 'new_m_a_w_s': 'new_m', 'new_m_a_b_s': 'new_m', 'new_m_b_conv_w': 'new_m', 'new_m_b_conv_b': 'new_m', 'new_m_b_ln_g': 'new_m', 'new_m_b_ln_b': 'new_m', 'new_m_ab_w_out': 'new_m', 'new_m_mla_w_in': 'new_m', 'new_m_mla_q_norm_g': 'new_m', 'new_m_mla_w_uq': 'new_m', 'new_m_mla_kv_norm_g': 'new_m', 'new_m_mla_w_ukv': 'new_m', 'new_m_mla_w_o': 'new_m', 'new_m_ffn_w_up': 'new_m', 'new_m_ffn_conv_w': 'new_m', 'new_m_ffn_conv_b': 'new_m', 'new_m_ffn_w_down': 'new_m', 'new_m_final_norm_g': 'new_m', 'new_v_c_ctx': 'new_v', 'new_v_norm1_g': 'new_v', 'new_v_norm2_g': 'new_v', 'new_v_w_ada': 'new_v', 'new_v_b_ada': 'new_v', 'new_v_ab_w_in': 'new_v', 'new_v_ab_b_in': 'new_v', 'new_v_a_ln_g': 'new_v', 'new_v_a_ln_b': 'new_v', 'new_v_a_w_s': 'new_v', 'new_v_a_b_s': 'new_v', 'new_v_b_conv_w': 'new_v', 'new_v_b_conv_b': 'new_v', 'new_v_b_ln_g': 'new_v', 'new_v_b_ln_b': 'new_v', 'new_v_ab_w_out': 'new_v', 'new_v_mla_w_in': 'new_v', 'new_v_mla_q_norm_g': 'new_v', 'new_v_mla_w_uq': 'new_v', 'new_v_mla_kv_norm_g': 'new_v', 'new_v_mla_w_ukv': 'new_v', 'new_v_mla_w_o': 'new_v', 'new_v_ffn_w_up': 'new_v', 'new_v_ffn_conv_w': 'new_v', 'new_v_ffn_conv_b': 'new_v', 'new_v_ffn_w_down': 'new_v', 'new_v_final_norm_g': 'new_v'}


def _forward(args):
    return _fwd_reference(*[args[k] for k in FWD_PARAMS])


def _output_shape():
    def fwd():
        inp = _fwd_setup_inputs(0)
        return _fwd_reference(*[inp[k] for k in FWD_PARAMS])
    out = _jax.eval_shape(fwd)
    return out.shape, out.dtype

N_MICROBATCH = 1
ADAM_LR = 0.001
ADAM_B1 = 0.9
ADAM_B2 = 0.999
ADAM_EPS = 1e-08
ADAM_WD = 0.01
ADAM_STEP = 10
PER_EXAMPLE_BATCH_AXIS = {'x': 0, 'c': 0, 'ctx': 0, 'loss_target': 0}
SHARED_INPUTS = []
_WEIGHT_DTYPES = {'c_ctx': _jnp.float32, 'norm1_g': _jnp.float32, 'norm2_g': _jnp.float32, 'w_ada': _jnp.float32, 'b_ada': _jnp.float32, 'ab_w_in': _jnp.float32, 'ab_b_in': _jnp.float32, 'a_ln_g': _jnp.float32, 'a_ln_b': _jnp.float32, 'a_w_s': _jnp.float32, 'a_b_s': _jnp.float32, 'b_conv_w': _jnp.float32, 'b_conv_b': _jnp.float32, 'b_ln_g': _jnp.float32, 'b_ln_b': _jnp.float32, 'ab_w_out': _jnp.float32, 'mla_w_in': _jnp.float32, 'mla_q_norm_g': _jnp.float32, 'mla_w_uq': _jnp.float32, 'mla_kv_norm_g': _jnp.float32, 'mla_w_ukv': _jnp.float32, 'mla_w_o': _jnp.float32, 'ffn_w_up': _jnp.float32, 'ffn_conv_w': _jnp.float32, 'ffn_conv_b': _jnp.float32, 'ffn_w_down': _jnp.float32, 'final_norm_g': _jnp.float32}
MOMENT_SCALE = {'c_ctx': 1.335586e-02, 'norm1_g': 4.204080e-02, 'norm2_g': 5.405673e-02, 'w_ada': 3.516035e-02, 'b_ada': 6.200987e-02, 'ab_w_in': 4.648887e-02, 'ab_b_in': 3.938792e-02, 'a_ln_g': 4.834300e-02, 'a_ln_b': 4.890526e-02, 'a_w_s': 4.510596e-02, 'a_b_s': 4.689438e-02, 'b_conv_w': 3.284951e-02, 'b_conv_b': 5.087329e-02, 'b_ln_g': 3.893687e-02, 'b_ln_b': 3.715593e-02, 'ab_w_out': 5.260459e-02, 'mla_w_in': 2.768492e-02, 'mla_q_norm_g': 9.198719e-03, 'mla_w_uq': 4.586116e-03, 'mla_kv_norm_g': 4.127557e-02, 'mla_w_ukv': 1.565006e-02, 'mla_w_o': 2.228833e-02, 'ffn_w_up': 2.620418e-02, 'ffn_conv_w': 2.684376e-02, 'ffn_conv_b': 1.936782e-02, 'ffn_w_down': 4.304702e-02, 'final_norm_g': 1.623946e+01}


def _to_microbatches(a, axis):
    t = _jnp.moveaxis(a, axis, 0)
    t = t.reshape((N_MICROBATCH, t.shape[0] // N_MICROBATCH) + t.shape[1:])
    return _jnp.moveaxis(t, 1, axis + 1)


def setup_inputs(seed: int = 0) -> dict:
    inp = _fwd_setup_inputs(seed)
    key = _jax.random.fold_in(_jax.random.key(seed), 7919)
    shape, _ = _output_shape()
    out = dict(inp)
    out["loss_target"] = _jax.random.normal(_jax.random.fold_in(key, 0), shape, _jnp.float32)
    for i, name in enumerate(TWIN_WEIGHTS):
        w = inp[name].astype(_jnp.float32)
        if MOMENT_SCALE is None:
            s = _jnp.sqrt(_jnp.mean(_jnp.square(w)) + 1e-30)
        else:
            s = MOMENT_SCALE[name]
        km, kv = _jax.random.split(_jax.random.fold_in(key, i + 1))
        out[name] = w
        out["m_" + name] = s * _jax.random.normal(km, w.shape, _jnp.float32)
        out["v_" + name] = (s * s) * _jax.random.uniform(kv, w.shape, _jnp.float32, 0.5, 1.5)
    if N_MICROBATCH > 1:
        for name, axis in PER_EXAMPLE_BATCH_AXIS.items():
            out[name] = _to_microbatches(out[name], axis)
    return {'x': out['x'], 'c': out['c'], 'ctx': out['ctx'], 'c_ctx': out['c_ctx'], 'norm1_g': out['norm1_g'], 'norm2_g': out['norm2_g'], 'w_ada': out['w_ada'], 'b_ada': out['b_ada'], 'ab_w_in': out['ab_w_in'], 'ab_b_in': out['ab_b_in'], 'a_ln_g': out['a_ln_g'], 'a_ln_b': out['a_ln_b'], 'a_w_s': out['a_w_s'], 'a_b_s': out['a_b_s'], 'b_conv_w': out['b_conv_w'], 'b_conv_b': out['b_conv_b'], 'b_ln_g': out['b_ln_g'], 'b_ln_b': out['b_ln_b'], 'ab_w_out': out['ab_w_out'], 'mla_w_in': out['mla_w_in'], 'mla_q_norm_g': out['mla_q_norm_g'], 'mla_w_uq': out['mla_w_uq'], 'mla_kv_norm_g': out['mla_kv_norm_g'], 'mla_w_ukv': out['mla_w_ukv'], 'mla_w_o': out['mla_w_o'], 'ffn_w_up': out['ffn_w_up'], 'ffn_conv_w': out['ffn_conv_w'], 'ffn_conv_b': out['ffn_conv_b'], 'ffn_w_down': out['ffn_w_down'], 'final_norm_g': out['final_norm_g'], 'loss_target': out['loss_target'], 'm_c_ctx': out['m_c_ctx'], 'm_norm1_g': out['m_norm1_g'], 'm_norm2_g': out['m_norm2_g'], 'm_w_ada': out['m_w_ada'], 'm_b_ada': out['m_b_ada'], 'm_ab_w_in': out['m_ab_w_in'], 'm_ab_b_in': out['m_ab_b_in'], 'm_a_ln_g': out['m_a_ln_g'], 'm_a_ln_b': out['m_a_ln_b'], 'm_a_w_s': out['m_a_w_s'], 'm_a_b_s': out['m_a_b_s'], 'm_b_conv_w': out['m_b_conv_w'], 'm_b_conv_b': out['m_b_conv_b'], 'm_b_ln_g': out['m_b_ln_g'], 'm_b_ln_b': out['m_b_ln_b'], 'm_ab_w_out': out['m_ab_w_out'], 'm_mla_w_in': out['m_mla_w_in'], 'm_mla_q_norm_g': out['m_mla_q_norm_g'], 'm_mla_w_uq': out['m_mla_w_uq'], 'm_mla_kv_norm_g': out['m_mla_kv_norm_g'], 'm_mla_w_ukv': out['m_mla_w_ukv'], 'm_mla_w_o': out['m_mla_w_o'], 'm_ffn_w_up': out['m_ffn_w_up'], 'm_ffn_conv_w': out['m_ffn_conv_w'], 'm_ffn_conv_b': out['m_ffn_conv_b'], 'm_ffn_w_down': out['m_ffn_w_down'], 'm_final_norm_g': out['m_final_norm_g'], 'v_c_ctx': out['v_c_ctx'], 'v_norm1_g': out['v_norm1_g'], 'v_norm2_g': out['v_norm2_g'], 'v_w_ada': out['v_w_ada'], 'v_b_ada': out['v_b_ada'], 'v_ab_w_in': out['v_ab_w_in'], 'v_ab_b_in': out['v_ab_b_in'], 'v_a_ln_g': out['v_a_ln_g'], 'v_a_ln_b': out['v_a_ln_b'], 'v_a_w_s': out['v_a_w_s'], 'v_a_b_s': out['v_a_b_s'], 'v_b_conv_w': out['v_b_conv_w'], 'v_b_conv_b': out['v_b_conv_b'], 'v_b_ln_g': out['v_b_ln_g'], 'v_b_ln_b': out['v_b_ln_b'], 'v_ab_w_out': out['v_ab_w_out'], 'v_mla_w_in': out['v_mla_w_in'], 'v_mla_q_norm_g': out['v_mla_q_norm_g'], 'v_mla_w_uq': out['v_mla_w_uq'], 'v_mla_kv_norm_g': out['v_mla_kv_norm_g'], 'v_mla_w_ukv': out['v_mla_w_ukv'], 'v_mla_w_o': out['v_mla_w_o'], 'v_ffn_w_up': out['v_ffn_w_up'], 'v_ffn_conv_w': out['v_ffn_conv_w'], 'v_ffn_conv_b': out['v_ffn_conv_b'], 'v_ffn_w_down': out['v_ffn_w_down'], 'v_final_norm_g': out['v_final_norm_g']}


def _loss(weights, diff, rest, loss_target):
    with _jax.named_scope("forward"):
        args = {**rest, TWIN_DIFF_INPUT: diff, **{k: w.astype(_WEIGHT_DTYPES[k]) for k, w in weights.items()}}
        y = _forward(args)
    with _jax.named_scope("loss_head"):
        err = _jnp.square(y.astype(_jnp.float32) - loss_target)
        return 0.5 * _jnp.sum(_jnp.mean(err, axis=-1)) if err.ndim else 0.5 * err


def _adamw(w, g, m, v):
    m = ADAM_B1 * m + (1.0 - ADAM_B1) * g
    v = ADAM_B2 * v + (1.0 - ADAM_B2) * _jnp.square(g)
    m_hat = m / (1.0 - ADAM_B1 ** ADAM_STEP)
    v_hat = v / (1.0 - ADAM_B2 ** ADAM_STEP)
    delta = -ADAM_LR * (m_hat / (_jnp.sqrt(v_hat) + ADAM_EPS) + ADAM_WD * w)
    return delta, m, v


def reference(x, c, ctx, c_ctx, norm1_g, norm2_g, w_ada, b_ada, ab_w_in, ab_b_in, a_ln_g, a_ln_b, a_w_s, a_b_s, b_conv_w, b_conv_b, b_ln_g, b_ln_b, ab_w_out, mla_w_in, mla_q_norm_g, mla_w_uq, mla_kv_norm_g, mla_w_ukv, mla_w_o, ffn_w_up, ffn_conv_w, ffn_conv_b, ffn_w_down, final_norm_g, loss_target, m_c_ctx, m_norm1_g, m_norm2_g, m_w_ada, m_b_ada, m_ab_w_in, m_ab_b_in, m_a_ln_g, m_a_ln_b, m_a_w_s, m_a_b_s, m_b_conv_w, m_b_conv_b, m_b_ln_g, m_b_ln_b, m_ab_w_out, m_mla_w_in, m_mla_q_norm_g, m_mla_w_uq, m_mla_kv_norm_g, m_mla_w_ukv, m_mla_w_o, m_ffn_w_up, m_ffn_conv_w, m_ffn_conv_b, m_ffn_w_down, m_final_norm_g, v_c_ctx, v_norm1_g, v_norm2_g, v_w_ada, v_b_ada, v_ab_w_in, v_ab_b_in, v_a_ln_g, v_a_ln_b, v_a_w_s, v_a_b_s, v_b_conv_w, v_b_conv_b, v_b_ln_g, v_b_ln_b, v_ab_w_out, v_mla_w_in, v_mla_q_norm_g, v_mla_w_uq, v_mla_kv_norm_g, v_mla_w_ukv, v_mla_w_o, v_ffn_w_up, v_ffn_conv_w, v_ffn_conv_b, v_ffn_w_down, v_final_norm_g):
    given = dict(x=x, c=c, ctx=ctx, c_ctx=c_ctx, norm1_g=norm1_g, norm2_g=norm2_g, w_ada=w_ada, b_ada=b_ada, ab_w_in=ab_w_in, ab_b_in=ab_b_in, a_ln_g=a_ln_g, a_ln_b=a_ln_b, a_w_s=a_w_s, a_b_s=a_b_s, b_conv_w=b_conv_w, b_conv_b=b_conv_b, b_ln_g=b_ln_g, b_ln_b=b_ln_b, ab_w_out=ab_w_out, mla_w_in=mla_w_in, mla_q_norm_g=mla_q_norm_g, mla_w_uq=mla_w_uq, mla_kv_norm_g=mla_kv_norm_g, mla_w_ukv=mla_w_ukv, mla_w_o=mla_w_o, ffn_w_up=ffn_w_up, ffn_conv_w=ffn_conv_w, ffn_conv_b=ffn_conv_b, ffn_w_down=ffn_w_down, final_norm_g=final_norm_g, loss_target=loss_target, m_c_ctx=m_c_ctx, m_norm1_g=m_norm1_g, m_norm2_g=m_norm2_g, m_w_ada=m_w_ada, m_b_ada=m_b_ada, m_ab_w_in=m_ab_w_in, m_ab_b_in=m_ab_b_in, m_a_ln_g=m_a_ln_g, m_a_ln_b=m_a_ln_b, m_a_w_s=m_a_w_s, m_a_b_s=m_a_b_s, m_b_conv_w=m_b_conv_w, m_b_conv_b=m_b_conv_b, m_b_ln_g=m_b_ln_g, m_b_ln_b=m_b_ln_b, m_ab_w_out=m_ab_w_out, m_mla_w_in=m_mla_w_in, m_mla_q_norm_g=m_mla_q_norm_g, m_mla_w_uq=m_mla_w_uq, m_mla_kv_norm_g=m_mla_kv_norm_g, m_mla_w_ukv=m_mla_w_ukv, m_mla_w_o=m_mla_w_o, m_ffn_w_up=m_ffn_w_up, m_ffn_conv_w=m_ffn_conv_w, m_ffn_conv_b=m_ffn_conv_b, m_ffn_w_down=m_ffn_w_down, m_final_norm_g=m_final_norm_g, v_c_ctx=v_c_ctx, v_norm1_g=v_norm1_g, v_norm2_g=v_norm2_g, v_w_ada=v_w_ada, v_b_ada=v_b_ada, v_ab_w_in=v_ab_w_in, v_ab_b_in=v_ab_b_in, v_a_ln_g=v_a_ln_g, v_a_ln_b=v_a_ln_b, v_a_w_s=v_a_w_s, v_a_b_s=v_a_b_s, v_b_conv_w=v_b_conv_w, v_b_conv_b=v_b_conv_b, v_b_ln_g=v_b_ln_g, v_b_ln_b=v_b_ln_b, v_ab_w_out=v_ab_w_out, v_mla_w_in=v_mla_w_in, v_mla_q_norm_g=v_mla_q_norm_g, v_mla_w_uq=v_mla_w_uq, v_mla_kv_norm_g=v_mla_kv_norm_g, v_mla_w_ukv=v_mla_w_ukv, v_mla_w_o=v_mla_w_o, v_ffn_w_up=v_ffn_w_up, v_ffn_conv_w=v_ffn_conv_w, v_ffn_conv_b=v_ffn_conv_b, v_ffn_w_down=v_ffn_w_down, v_final_norm_g=v_final_norm_g)
    weights = {n: given[n] for n in TWIN_WEIGHTS}
    shared = {n: given[n] for n in SHARED_INPUTS}
    per_example = {n: given[n] for n in ['x', 'c', 'ctx']}
    grad_fn = _jax.value_and_grad(_loss, argnums=(0, 1))

    def one_microbatch(ex, loss_target):
        ex = dict(ex)
        diff = ex.pop(TWIN_DIFF_INPUT)
        return grad_fn(weights, diff, {**shared, **ex}, loss_target)

    if N_MICROBATCH == 1:
        loss, (grad_w, grad_x) = one_microbatch(per_example, given["loss_target"])
    else:
        def body(carry, xs):
            loss_sum, grad_sum = carry
            l_k, (gw_k, gx_k) = one_microbatch(xs[0], xs[1])
            with _jax.named_scope("update"):
                return (loss_sum + l_k, _jax.tree.map(_jnp.add, grad_sum, gw_k)), gx_k

        init = (_jnp.zeros((), _jnp.float32), _jax.tree.map(_jnp.zeros_like, weights))
        (loss, grad_w), grad_x = _jax.lax.scan(body, init, (per_example, given["loss_target"]))
    with _jax.named_scope("update"):
        delta_w, new_m, new_v = {}, {}, {}
        for n in TWIN_WEIGHTS:
            delta_w[n], new_m[n], new_v[n] = _adamw(weights[n], grad_w[n], given["m_" + n], given["v_" + n])
    return (loss, grad_x, *[grad_w[n] for n in TWIN_WEIGHTS], *[delta_w[n] for n in TWIN_WEIGHTS],
            *[new_m[n] for n in TWIN_WEIGHTS], *[new_v[n] for n in TWIN_WEIGHTS])
```

```python
import functools
import math

import jax
import jax.numpy as jnp
from jax import lax
from jax.experimental import pallas as pl
from jax.experimental.pallas import tpu as pltpu

F32 = jnp.float32
BF16 = jnp.bfloat16
SDS = jax.ShapeDtypeStruct

N_DEV = 8
EPS = 1e-6
CHUNK = 128
NOPE = 128
ROPE = 64
VDIM = 128
GRID_W = 64
ROPE_THETA = 10000.0
HALO = 16
ADAM_LR, ADAM_B1, ADAM_B2, ADAM_EPS, ADAM_WD, ADAM_STEP = 0.001, 0.9, 0.999, 1e-08, 0.01, 10
VMEM_LIMIT = 56 * 1024 * 1024


def _tile(n, prefs):
    for p in prefs:
        if n % p == 0:
            return p
    return n


def _params(sem, vmem=VMEM_LIMIT):
    return pltpu.CompilerParams(dimension_semantics=sem, vmem_limit_bytes=vmem)


def _sigmoid(x):
    return 1.0 / (1.0 + jnp.exp(-x))


def _silu(x):
    return x * _sigmoid(x)


def _dsilu(x):
    s = _sigmoid(x)
    return s * (1.0 + x * (1.0 - s))


_GELU_C = math.sqrt(2.0 / math.pi)


def _gelu(x):
    return 0.5 * x * (1.0 + jnp.tanh(_GELU_C * (x + 0.044715 * x * x * x)))


def _dgelu(x):
    t = jnp.tanh(_GELU_C * (x + 0.044715 * x * x * x))
    return 0.5 * (1.0 + t) + 0.5 * x * (1.0 - t * t) * _GELU_C * (1.0 + 3.0 * 0.044715 * x * x)


def _colsum(v):
    return jnp.sum(v, axis=0, keepdims=True)


def _exchange(x, *, scatter, name):
    blk = x.shape[1:] if scatter else x.shape

    def body(x_ref, o_ref, send_sems, recv_sems, local_sem):
        ax, ay, ac = lax.axis_index("x"), lax.axis_index("y"), lax.axis_index("c")
        me = 4 * ax + 2 * ay + ac

        def peer(k):
            return ax ^ (k >> 2), ay ^ ((k >> 1) & 1), ac ^ (k & 1)

        def src(d):
            return x_ref.at[d] if scatter else x_ref

        own = pltpu.make_async_copy(src(me), o_ref.at[me], local_sem)
        own.start()
        sends = []
        for k in range(1, N_DEV):
            px, py, pc = peer(k)
            cp = pltpu.make_async_remote_copy(
                src_ref=src(4 * px + 2 * py + pc), dst_ref=o_ref.at[me], send_sem=send_sems.at[k - 1],
                recv_sem=recv_sems.at[k - 1], device_id=(px, py, pc), device_id_type=pl.DeviceIdType.MESH)
            cp.start()
            sends.append(cp)
        for k in range(1, N_DEV):
            px, py, pc = peer(k)
            d = 4 * px + 2 * py + pc
            pltpu.make_async_remote_copy(
                src_ref=src(d), dst_ref=o_ref.at[d], send_sem=send_sems.at[k - 1], recv_sem=recv_sems.at[k - 1],
                device_id=(px, py, pc), device_id_type=pl.DeviceIdType.MESH).wait_recv()
        for cp in sends:
            cp.wait_send()
        own.wait()

    return pl.pallas_call(
        body, out_shape=SDS((N_DEV,) + tuple(blk), x.dtype),
        in_specs=[pl.BlockSpec(memory_space=pl.ANY)], out_specs=pl.BlockSpec(memory_space=pl.ANY),
        scratch_shapes=[pltpu.SemaphoreType.DMA((N_DEV - 1,)), pltpu.SemaphoreType.DMA((N_DEV - 1,)),
                        pltpu.SemaphoreType.DMA],
        name=name)(x)


def _all_gather(x, name):
    return _exchange(x, scatter=False, name=name)


def _all_to_all(x, name):
    return _exchange(x, scatter=True, name=name)


def _sum_lead(x, name):
    n, R, C = x.shape
    tr = _tile(R, (512, 256, 128, 64, 32, 16, 8))

    def body(x_ref, o_ref):
        acc = x_ref[0]
        for d in range(1, n):
            acc = acc + x_ref[d]
        o_ref[...] = acc

    return pl.pallas_call(
        body, out_shape=SDS((R, C), F32), grid=(R // tr,),
        in_specs=[pl.BlockSpec((n, tr, C), lambda i: (0, i, 0))], out_specs=pl.BlockSpec((tr, C), lambda i: (i, 0)),
        compiler_params=_params(("parallel",)), name=name)(x)


_TP = (1408, 1088, 1024, 768, 512, 256, 128)
_TQ = (1408, 1024, 768, 512, 256, 128)
_TR = (1408, 1024, 512, 256, 128)


def _mm(a, b, *, mode, out_dtype, name, rows=None, bias=None, res=None, gate=None, seg_t=None, a_silu=False):
    if mode == "nn":
        P, R, Q = rows or a.shape[0], a.shape[1], b.shape[1]
    elif mode == "nt":
        P, R, Q = rows or a.shape[0], a.shape[1], b.shape[0]
    else:
        R, P, Q = rows or a.shape[0], a.shape[1], b.shape[1]
    tp, tq, tr = _tile(P, _TP), _tile(Q, _TQ), _tile(R, _TR if mode != "tn" else (512, 256, 128))
    nk = R // tr
    if mode == "nn":
        a_spec = pl.BlockSpec((tp, tr), lambda i, j, k: (i, k))
        b_spec = pl.BlockSpec((tr, tq), lambda i, j, k: (k, j))
        dims = (((1,), (0,)), ((), ()))
    elif mode == "nt":
        a_spec = pl.BlockSpec((tp, tr), lambda i, j, k: (i, k))
        b_spec = pl.BlockSpec((tq, tr), lambda i, j, k: (j, k))
        dims = (((1,), (1,)), ((), ()))
    else:
        a_spec = pl.BlockSpec((tr, tp), lambda i, j, k: (k, i))
        b_spec = pl.BlockSpec((tr, tq), lambda i, j, k: (k, j))
        dims = (((0,), (0,)), ((), ()))
    ins, in_specs = [a, b], [a_spec, b_spec]
    if bias is not None:
        ins.append(bias)
        in_specs.append(pl.BlockSpec((1, tq), lambda i, j, k: (0, j)))
    gated = res is not None
    if gated:
        n_seg = gate.shape[0]
        ins += [res, gate]
        in_specs += [pl.BlockSpec((tp, tq), lambda i, j, k: (i, j)),
                     pl.BlockSpec((n_seg, 1, tq), lambda i, j, k: (0, 0, j))]
    out_shape = [SDS((P, Q), out_dtype)]
    out_specs = [pl.BlockSpec((tp, tq), lambda i, j, k: (i, j))]
    if gated:
        out_shape.append(SDS((P, Q), BF16))
        out_specs.append(pl.BlockSpec((tp, tq), lambda i, j, k: (i, j)))

    def body(*refs):
        a_ref, b_ref = refs[0], refs[1]
        pos = 2
        bias_ref = res_ref = gate_ref = o2_ref = None
        if bias is not None:
            bias_ref = refs[pos]
            pos += 1
        if gated:
            res_ref, gate_ref = refs[pos], refs[pos + 1]
            pos += 2
        o_ref = refs[pos]
        pos += 1
        if gated:
            o2_ref = refs[pos]
            pos += 1
        acc_ref = refs[pos]
        k = pl.program_id(2)

        @pl.when(k == 0)
        def _():
            acc_ref[...] = jnp.zeros_like(acc_ref)

        av = a_ref[...]
        if a_silu:
            av = _silu(av.astype(F32))
        acc_ref[...] += lax.dot_general(av.astype(BF16), b_ref[...].astype(BF16), dims, preferred_element_type=F32)

        @pl.when(k == nk - 1)
        def _():
            acc = acc_ref[...]
            if bias_ref is not None:
                acc = acc + bias_ref[...]
            if gated:
                if n_seg == 1:
                    g = gate_ref[0]
                else:
                    row = pl.program_id(0) * tp + lax.broadcasted_iota(jnp.int32, (tp, 1), 0)
                    g = jnp.where(row < seg_t, gate_ref[0], gate_ref[1])
                o_ref[...] = (res_ref[...] + g * acc).astype(o_ref.dtype)
                o2_ref[...] = acc.astype(BF16)
            else:
                o_ref[...] = acc.astype(o_ref.dtype)

    out = pl.pallas_call(
        body, out_shape=out_shape, grid=(P // tp, Q // tq, nk), in_specs=in_specs, out_specs=out_specs,
        scratch_shapes=[pltpu.VMEM((tp, tq), F32)],
        compiler_params=_params(("parallel", "parallel", "arbitrary")), name=name)(*ins)
    return tuple(out) if gated else out[0]


def _row_tile(seg_t, m):
    return 256 if (seg_t % 256 == 0 and m % 256 == 0) else 128


def _normmod_fwd(x, g, sh, sc, seg_t, name):
    M, D = x.shape
    tm = _row_tile(seg_t, M)
    n_seg = sh.shape[0]
    nt = seg_t // tm

    def seg(i):
        return ((i >= nt).astype(jnp.int32) if n_seg == 2 else 0, 0, 0)

    def body(x_ref, g_ref, sh_ref, sc_ref, o_ref):
        xv = x_ref[...]
        r = lax.rsqrt(jnp.mean(xv * xv, axis=-1, keepdims=True) + EPS)
        y = xv * r * g_ref[...]
        o_ref[...] = (y * (1.0 + sc_ref[0]) + sh_ref[0]).astype(BF16)

    return pl.pallas_call(
        body, out_shape=SDS((M, D), BF16), grid=(M // tm,),
        in_specs=[pl.BlockSpec((tm, D), lambda i: (i, 0)), pl.BlockSpec((1, D), lambda i: (0, 0)),
                  pl.BlockSpec((1, 1, D), seg), pl.BlockSpec((1, 1, D), seg)],
        out_specs=pl.BlockSpec((tm, D), lambda i: (i, 0)),
        compiler_params=_params(("parallel",)), name=name)(x, g, sh, sc)


def _normmod_bwd(x, g, sc, dh, dx_in, seg_t, name, o_prev=None, gate_prev=None):
    M, D = x.shape
    tm = _row_tile(seg_t, M)
    n_seg = sc.shape[0]
    nt = seg_t // tm
    n_in = dx_in.shape[0] // tm
    with_prev = o_prev is not None
    n_segp = gate_prev.shape[0] if with_prev else 0

    def seg(i):
        return ((i >= nt).astype(jnp.int32) if n_seg == 2 else 0, 0, 0)

    def segp(i):
        return ((i >= nt).astype(jnp.int32) if n_segp == 2 else 0, 0, 0)

    def body(*refs):
        x_ref, g_ref, sc_ref, dh_ref, dxin_ref = refs[:5]
        pos = 5
        if with_prev:
            op_ref, gp_ref = refs[5], refs[6]
            pos = 7
        dx_ref, dg_ref, dsh_ref, dsc_ref = refs[pos:pos + 4]
        if with_prev:
            dop_ref, dgp_ref = refs[pos + 4], refs[pos + 5]
        i = pl.program_id(0)
        xv = x_ref[...]
        r = lax.rsqrt(jnp.mean(xv * xv, axis=-1, keepdims=True) + EPS)
        xh = xv * r
        gv = g_ref[...]
        dhv = dh_ref[...].astype(F32)
        dy = dhv * (1.0 + sc_ref[0])
        dxh = dy * gv
        dxv = r * (dxh - xh * jnp.mean(dxh * xh, axis=-1, keepdims=True))
        if n_in * tm < M:
            dxv = dxv + jnp.where(i < n_in, dxin_ref[...], 0.0)
        else:
            dxv = dxv + dxin_ref[...]
        dx_ref[...] = dxv

        @pl.when(i == 0)
        def _():
            dg_ref[...] = jnp.zeros_like(dg_ref)

        first_of_seg = (i == 0) | (i == nt) if n_seg == 2 else (i == 0)

        @pl.when(first_of_seg)
        def _():
            dsh_ref[...] = jnp.zeros_like(dsh_ref)
            dsc_ref[...] = jnp.zeros_like(dsc_ref)

        dg_ref[...] += _colsum(dy * xh)
        dsh_ref[0] += _colsum(dhv)
        dsc_ref[0] += _colsum(dhv * xh * gv)
        if with_prev:
            first_of_segp = (i == 0) | (i == nt) if n_segp == 2 else (i == 0)

            @pl.when(first_of_segp)
            def _():
                dgp_ref[...] = jnp.zeros_like(dgp_ref)

            dop_ref[...] = (gp_ref[0] * dxv).astype(BF16)
            dgp_ref[0] += _colsum(dxv * op_ref[...].astype(F32))

    row = pl.BlockSpec((tm, D), lambda i: (i, 0))
    ins = [x, g, sc, dh, dx_in]
    in_specs = [row, pl.BlockSpec((1, D), lambda i: (0, 0)), pl.BlockSpec((1, 1, D), seg), row,
                pl.BlockSpec((tm, D), lambda i: (jnp.minimum(i, n_in - 1), 0))]
    out_shape = [SDS((M, D), F32), SDS((1, D), F32), SDS((n_seg, 1, D), F32), SDS((n_seg, 1, D), F32)]
    out_specs = [row, pl.BlockSpec((1, D), lambda i: (0, 0)), pl.BlockSpec((1, 1, D), seg), pl.BlockSpec((1, 1, D), seg)]
    if with_prev:
        ins += [o_prev, gate_prev]
        in_specs += [row, pl.BlockSpec((1, 1, D), segp)]
        out_shape += [SDS((M, D), BF16), SDS((n_segp, 1, D), F32)]
        out_specs += [row, pl.BlockSpec((1, 1, D), segp)]
    return pl.pallas_call(
        body, out_shape=out_shape, grid=(M // tm,), in_specs=in_specs, out_specs=out_specs,
        compiler_params=_params(("arbitrary",)), name=name)(*ins)


def _final(x, g, target, o_prev, gate_prev, name):
    T, D = x.shape
    tm = _tile(T, (256, 128))

    def body(x_ref, g_ref, t_ref, op_ref, gp_ref, dx_ref, loss_ref, dg_ref, dop_ref, dgp_ref):
        i = pl.program_id(0)
        xv = x_ref[...]
        r = lax.rsqrt(jnp.mean(xv * xv, axis=-1, keepdims=True) + EPS)
        xh = xv * r
        gv = g_ref[...]
        e = xh * gv - t_ref[...]
        dout = e * (1.0 / D)
        dxh = dout * gv
        dxv = r * (dxh - xh * jnp.mean(dxh * xh, axis=-1, keepdims=True))
        dx_ref[...] = dxv
        dop_ref[...] = (gp_ref[0] * dxv).astype(BF16)

        @pl.when(i == 0)
        def _():
            loss_ref[...] = jnp.zeros_like(loss_ref)
            dg_ref[...] = jnp.zeros_like(dg_ref)
            dgp_ref[...] = jnp.zeros_like(dgp_ref)

        loss_ref[...] += _colsum(e * e) * (0.5 / D)
        dg_ref[...] += _colsum(dout * xh)
        dgp_ref[0] += _colsum(dxv * op_ref[...].astype(F32))

    row = pl.BlockSpec((tm, D), lambda i: (i, 0))
    vec = pl.BlockSpec((1, D), lambda i: (0, 0))
    vec3 = pl.BlockSpec((1, 1, D), lambda i: (0, 0, 0))
    return pl.pallas_call(
        body, out_shape=[SDS((T, D), F32), SDS((1, D), F32), SDS((1, D), F32), SDS((T, D), BF16), SDS((1, 1, D), F32)],
        grid=(T // tm,), in_specs=[row, vec, row, row, vec3], out_specs=[row, vec, vec, row, vec3],
        compiler_params=_params(("arbitrary",)), name=name)(x, g, target, o_prev, gate_prev)


def _gmlp_core(z, lg, lb, ws_ref, bst):
    W = z.shape[1] // 2
    t = _gelu(z)
    u, v = t[:, :W], t[:, W:]
    mu = jnp.mean(v, axis=-1, keepdims=True)
    vc = v - mu
    rstd = lax.rsqrt(jnp.mean(vc * vc, axis=-1, keepdims=True) + EPS)
    vhat = vc * rstd
    vn = vhat * lg + lb
    vp = []
    for h in range(W // CHUNK):
        blk = vn[:, h * CHUNK:(h + 1) * CHUNK].astype(BF16)
        vp.append(jnp.dot(ws_ref[h].astype(BF16), blk, preferred_element_type=F32) + bst[:, h:h + 1])
    return u, vhat, rstd, vp


def _gmlp_fwd(z, ln_g, ln_b, w_s, b_st, W, name):
    M = z.shape[0]
    H = W // CHUNK

    def body(z_ref, lg_ref, lb_ref, ws_ref, bst_ref, o_ref):
        u, _, _, vp = _gmlp_core(z_ref[...], lg_ref[...], lb_ref[...], ws_ref, bst_ref[...])
        for h in range(H):
            o_ref[:, h * CHUNK:(h + 1) * CHUNK] = (u[:, h * CHUNK:(h + 1) * CHUNK] * vp[h]).astype(BF16)

    vec = pl.BlockSpec((1, W), lambda i: (0, 0))
    return pl.pallas_call(
        body, out_shape=SDS((M, 2 * W), BF16), grid=(M // CHUNK,),
        in_specs=[pl.BlockSpec((CHUNK, 2 * W), lambda i: (i, 0)), vec, vec,
                  pl.BlockSpec((H, CHUNK, CHUNK), lambda i: (0, 0, 0)), pl.BlockSpec((CHUNK, H), lambda i: (0, 0))],
        out_specs=pl.BlockSpec((CHUNK, W), lambda i: (i, 0)),
        compiler_params=_params(("parallel",)), name=name)(z, ln_g, ln_b, w_s, b_st)


def _gmlp_bwd(z, dy, ln_g, ln_b, w_s, b_st, W, name):
    M = z.shape[0]
    H = W // CHUNK
    ZW = z.shape[1]

    def body(z_ref, dy_ref, lg_ref, lb_ref, ws_ref, bst_ref, dz_ref, dlg_ref, dlb_ref, dws_ref, dbs_ref, dbin_ref):
        i = pl.program_id(0)

        @pl.when(i == 0)
        def _():
            for r in (dlg_ref, dlb_ref, dws_ref, dbs_ref, dbin_ref):
                r[...] = jnp.zeros_like(r)

        zv = z_ref[...]
        lg = lg_ref[...]
        u, vhat, rstd, vp = _gmlp_core(zv, lg, lb_ref[...], ws_ref, bst_ref[...])
        vn = vhat * lg + lb_ref[...]
        dya = dy_ref[...]
        du_parts, dvn_parts = [], []
        for h in range(H):
            sl = slice(h * CHUNK, (h + 1) * CHUNK)
            dya_h = dya[:, sl]
            du_parts.append(dya_h * vp[h])
            dvp = dya_h * u[:, sl]
            dbs_ref[h] += dvp
            dvp16 = dvp.astype(BF16)
            dws_ref[h] += lax.dot_general(dvp16, vn[:, sl].astype(BF16), (((1,), (1,)), ((), ())),
                                          preferred_element_type=F32)
            dvn_parts.append(lax.dot_general(ws_ref[h].astype(BF16), dvp16, (((0,), (0,)), ((), ())),
                                             preferred_element_type=F32))
        du = jnp.concatenate(du_parts, axis=1)
        dvn = jnp.concatenate(dvn_parts, axis=1)
        dlg_ref[...] += _colsum(dvn * vhat)
        dlb_ref[...] += _colsum(dvn)
        dvh = dvn * lg
        dv = rstd * (dvh - jnp.mean(dvh, axis=-1, keepdims=True) - vhat * jnp.mean(dvh * vhat, axis=-1, keepdims=True))
        dz = jnp.concatenate([du, dv], axis=1) * _dgelu(zv)
        dbin_ref[...] += _colsum(dz)
        dz_ref[...] = dz.astype(BF16)

    vec = pl.BlockSpec((1, W), lambda i: (0, 0))
    mat = pl.BlockSpec((H, CHUNK, CHUNK), lambda i: (0, 0, 0))
    return pl.pallas_call(
        body,
        out_shape=[SDS((M, ZW), BF16), SDS((1, W), F32), SDS((1, W), F32), SDS((H, CHUNK, CHUNK), F32),
                   SDS((H, CHUNK, CHUNK), F32), SDS((1, 2 * W), F32)],
        grid=(M // CHUNK,),
        in_specs=[pl.BlockSpec((CHUNK, 2 * W), lambda i: (i, 0)), pl.BlockSpec((CHUNK, W), lambda i: (i, 0)), vec, vec,
                  mat, pl.BlockSpec((CHUNK, H), lambda i: (0, 0))],
        out_specs=[pl.BlockSpec((CHUNK, 2 * W), lambda i: (i, 0)), vec, vec, mat, mat,
                   pl.BlockSpec((1, 2 * W), lambda i: (0, 0))],
        compiler_params=_params(("arbitrary",)), name=name)(z, dy, ln_g, ln_b, w_s, b_st)


def _halo_specs(tm, width, col, n_rows):
    per = tm // HALO
    last = n_rows // HALO - 1
    prev = pl.BlockSpec((HALO, width), lambda i: (jnp.maximum(i * per - 1, 0), col))
    nxt = pl.BlockSpec((HALO, width), lambda i: (jnp.minimum((i + 1) * per, last), col))
    return prev, nxt


def _edge_flags(i, tm, seg_t, m):
    r0 = i * tm
    has_prev = jnp.where((r0 == 0) | (r0 == seg_t), 0.0, 1.0)
    has_next = jnp.where((r0 + tm == seg_t) | (r0 + tm == m), 0.0, 1.0)
    return has_prev, has_next


def _glu(zz, wb):
    return zz[:, :wb] * _sigmoid(zz[:, wb:])


def _conv_taps(src_ref, w_ref, first, tm, kw, flip=False):
    rc = 32
    parts = []
    for c in range(tm // rc):
        acc = None
        for k in range(kw):
            wk = w_ref[pl.ds(kw - 1 - k if flip else k, 1), :]
            term = src_ref[pl.ds(first + c * rc + k, rc), :] * wk
            acc = term if acc is None else acc + term
        parts.append(acc)
    return jnp.concatenate(parts, axis=0)


def _conf_fwd(z, y, conv_w, conv_b, ln_g, ln_b, W, Wb, seg_t, name):
    M = z.shape[0]
    tm = _row_tile(seg_t, M)
    kw = conv_w.shape[0]
    pad = (kw - 1) // 2
    col = (2 * W) // (2 * Wb)

    def body(zc_ref, zp_ref, zn_ref, y_hbm, cw_ref, cb_ref, lg_ref, lb_ref, o_ref, hs_ref):
        del y_hbm
        hp, hn = _edge_flags(pl.program_id(0), tm, seg_t, M)
        hs_ref[pl.ds(0, HALO), :] = _glu(zp_ref[...], Wb) * hp
        hs_ref[pl.ds(HALO, tm), :] = _glu(zc_ref[...], Wb)
        hs_ref[pl.ds(HALO + tm, HALO), :] = _glu(zn_ref[...], Wb) * hn
        hc = _conv_taps(hs_ref, cw_ref, HALO - pad, tm, kw) + cb_ref[...]
        mu = jnp.mean(hc, axis=-1, keepdims=True)
        c = hc - mu
        rstd = lax.rsqrt(jnp.mean(c * c, axis=-1, keepdims=True) + EPS)
        o_ref[...] = _silu(c * rstd * lg_ref[...] + lb_ref[...]).astype(BF16)

    prev, nxt = _halo_specs(tm, 2 * Wb, col, M)
    vec = pl.BlockSpec((1, Wb), lambda i: (0, 0))
    return pl.pallas_call(
        body, out_shape=SDS(y.shape, BF16), grid=(M // tm,),
        in_specs=[pl.BlockSpec((tm, 2 * Wb), lambda i: (i, col)), prev, nxt, pl.BlockSpec(memory_space=pl.ANY),
                  pl.BlockSpec((kw, Wb), lambda i: (0, 0)), vec, vec, vec],
        out_specs=pl.BlockSpec((tm, Wb), lambda i: (i, W // Wb)),
        scratch_shapes=[pltpu.VMEM((tm + 2 * HALO, Wb), F32)],
        input_output_aliases={3: 0}, compiler_params=_params(("parallel",)), name=name)(
            z, z, z, y, conv_w, conv_b, ln_g, ln_b)


def _conf_bwd1(z, dy, conv_w, conv_b, ln_g, ln_b, W, Wb, seg_t, name):
    M = z.shape[0]
    tm = _row_tile(seg_t, M)
    kw = conv_w.shape[0]
    pad = (kw - 1) // 2
    col = (2 * W) // (2 * Wb)

    def body(zc_ref, zp_ref, zn_ref, dy_ref, cw_ref, cb_ref, lg_ref, lb_ref, dhc_ref, dlg_ref, dlb_ref, dcb_ref, hs_ref):
        i = pl.program_id(0)

        @pl.when(i == 0)
        def _():
            for r in (dlg_ref, dlb_ref, dcb_ref):
                r[...] = jnp.zeros_like(r)

        hp, hn = _edge_flags(i, tm, seg_t, M)
        hs_ref[pl.ds(0, HALO), :] = _glu(zp_ref[...], Wb) * hp
        hs_ref[pl.ds(HALO, tm), :] = _glu(zc_ref[...], Wb)
        hs_ref[pl.ds(HALO + tm, HALO), :] = _glu(zn_ref[...], Wb) * hn
        hc = _conv_taps(hs_ref, cw_ref, HALO - pad, tm, kw) + cb_ref[...]
        mu = jnp.mean(hc, axis=-1, keepdims=True)
        c = hc - mu
        rstd = lax.rsqrt(jnp.mean(c * c, axis=-1, keepdims=True) + EPS)
        hh = c * rstd
        lg = lg_ref[...]
        dhn = dy_ref[...] * _dsilu(hh * lg + lb_ref[...])
        dlg_ref[...] += _colsum(dhn * hh)
        dlb_ref[...] += _colsum(dhn)
        dhh = dhn * lg
        dhc = rstd * (dhh - jnp.mean(dhh, axis=-1, keepdims=True) - hh * jnp.mean(dhh * hh, axis=-1, keepdims=True))
        dcb_ref[...] += _colsum(dhc)
        dhc_ref[...] = dhc

    prev, nxt = _halo_specs(tm, 2 * Wb, col, M)
    vec = pl.BlockSpec((1, Wb), lambda i: (0, 0))
    return pl.pallas_call(
        body, out_shape=[SDS((M, Wb), F32), SDS((1, Wb), F32), SDS((1, Wb), F32), SDS((1, Wb), F32)], grid=(M // tm,),
        in_specs=[pl.BlockSpec((tm, 2 * Wb), lambda i: (i, col)), prev, nxt, pl.BlockSpec((tm, Wb), lambda i: (i, W // Wb)),
                  pl.BlockSpec((kw, Wb), lambda i: (0, 0)), vec, vec, vec],
        out_specs=[pl.BlockSpec((tm, Wb), lambda i: (i, 0)), vec, vec, vec],
        scratch_shapes=[pltpu.VMEM((tm + 2 * HALO, Wb), F32)],
        compiler_params=_params(("arbitrary",)), name=name)(z, z, z, dy, conv_w, conv_b, ln_g, ln_b)


def _conf_bwd2(z, dhc, dz, conv_w, W, Wb, seg_t, name):
    M = z.shape[0]
    tm = _row_tile(seg_t, M)
    kw = conv_w.shape[0]
    pad = (kw - 1) // 2
    col = (2 * W) // (2 * Wb)

    def body(zc_ref, zp_ref, zn_ref, dc_ref, dp_ref, dn_ref, dz_hbm, cw_ref, dz_ref, dcw_ref, dbin_ref, hs_ref, ds_ref):
        del dz_hbm
        i = pl.program_id(0)

        @pl.when(i == 0)
        def _():
            dcw_ref[...] = jnp.zeros_like(dcw_ref)
            dbin_ref[...] = jnp.zeros_like(dbin_ref)

        hp, hn = _edge_flags(i, tm, seg_t, M)
        zc = zc_ref[...]
        hs_ref[pl.ds(0, HALO), :] = _glu(zp_ref[...], Wb) * hp
        hs_ref[pl.ds(HALO, tm), :] = _glu(zc, Wb)
        hs_ref[pl.ds(HALO + tm, HALO), :] = _glu(zn_ref[...], Wb) * hn
        dcur = dc_ref[...]
        ds_ref[pl.ds(0, HALO), :] = dp_ref[...] * hp
        ds_ref[pl.ds(HALO, tm), :] = dcur
        ds_ref[pl.ds(HALO + tm, HALO), :] = dn_ref[...] * hn
        dh = _conv_taps(ds_ref, cw_ref, HALO - pad, tm, kw, flip=True)
        for k in range(kw):
            dcw_ref[pl.ds(k, 1), :] += _colsum(dcur * hs_ref[pl.ds(HALO - pad + k, tm), :])
        a, gt = zc[:, :Wb], zc[:, Wb:]
        s = _sigmoid(gt)
        dz = jnp.concatenate([dh * s, dh * a * s * (1.0 - s)], axis=1)
        dbin_ref[...] += _colsum(dz)
        dz_ref[...] = dz.astype(BF16)

    prev, nxt = _halo_specs(tm, 2 * Wb, col, M)
    dprev, dnxt = _halo_specs(tm, Wb, 0, M)
    return pl.pallas_call(
        body, out_shape=[SDS(dz.shape, BF16), SDS((kw, Wb), F32), SDS((1, 2 * Wb), F32)], grid=(M // tm,),
        in_specs=[pl.BlockSpec((tm, 2 * Wb), lambda i: (i, col)), prev, nxt,
                  pl.BlockSpec((tm, Wb), lambda i: (i, 0)), dprev, dnxt, pl.BlockSpec(memory_space=pl.ANY),
                  pl.BlockSpec((kw, Wb), lambda i: (0, 0))],
        out_specs=[pl.BlockSpec((tm, 2 * Wb), lambda i: (i, col)), pl.BlockSpec((kw, Wb), lambda i: (0, 0)),
                   pl.BlockSpec((1, 2 * Wb), lambda i: (0, 0))],
        scratch_shapes=[pltpu.VMEM((tm + 2 * HALO, Wb), F32), pltpu.VMEM((tm + 2 * HALO, Wb), F32)],
        input_output_aliases={6: 0}, compiler_params=_params(("arbitrary",)), name=name)(
            z, z, z, dhc, dhc, dhc, dz, conv_w)


_TF = (512, 256, 128)


def _ffn_act_fwd(z, conv_w, conv_b, seg_t, name):
    M, F2 = z.shape
    Fd = F2 // 2
    tm = _row_tile(seg_t, M)
    tf = _tile(Fd, _TF)
    nf = Fd // tf
    per, last = tm // HALO, M // HALO - 1

    def body(g_ref, gp_ref, gn_ref, u_ref, cw_ref, cb_ref, o_ref, gs_ref):
        hp, hn = _edge_flags(pl.program_id(0), tm, seg_t, M)
        gs_ref[pl.ds(0, HALO), :] = gp_ref[...].astype(F32) * hp
        gs_ref[pl.ds(HALO, tm), :] = g_ref[...].astype(F32)
        gs_ref[pl.ds(HALO + tm, HALO), :] = gn_ref[...].astype(F32) * hn
        gc = (gs_ref[pl.ds(HALO - 1, tm), :] * cw_ref[pl.ds(0, 1), :] + gs_ref[pl.ds(HALO, tm), :] * cw_ref[pl.ds(1, 1), :]
              + gs_ref[pl.ds(HALO + 1, tm), :] * cw_ref[pl.ds(2, 1), :] + cb_ref[...])
        o_ref[...] = (_silu(gc) * u_ref[...].astype(F32)).astype(BF16)

    return pl.pallas_call(
        body, out_shape=SDS((M, Fd), BF16), grid=(M // tm, nf),
        in_specs=[pl.BlockSpec((tm, tf), lambda i, j: (i, j)),
                  pl.BlockSpec((HALO, tf), lambda i, j: (jnp.maximum(i * per - 1, 0), j)),
                  pl.BlockSpec((HALO, tf), lambda i, j: (jnp.minimum((i + 1) * per, last), j)),
                  pl.BlockSpec((tm, tf), lambda i, j: (i, nf + j)),
                  pl.BlockSpec((3, tf), lambda i, j: (0, j)), pl.BlockSpec((1, tf), lambda i, j: (0, j))],
        out_specs=pl.BlockSpec((tm, tf), lambda i, j: (i, j)),
        scratch_shapes=[pltpu.VMEM((tm + 2 * HALO, tf), F32)],
        compiler_params=_params(("parallel", "parallel")), name=name)(z, z, z, z, conv_w, conv_b)


def _ffn_act_bwd(z, da, conv_w, conv_b, seg_t, name):
    M, F2 = z.shape
    Fd = F2 // 2
    tm = _row_tile(seg_t, M)
    tf = _tile(Fd, _TF)
    nf = Fd // tf
    per, last = tm // HALO, M // HALO - 1
    E = tm + 2 * HALO
    Q = 8

    def body(g_ref, gp_ref, gn_ref, u_ref, up_ref, un_ref, a_ref, ap_ref, an_ref, cw_ref, cb_ref,
             dz_ref, dcw_ref, dcb_ref, gs_ref, ds_ref, du_ref):
        i, p = pl.program_id(1), pl.program_id(2)

        @pl.when(p == 0)
        def _():
            hp, hn = _edge_flags(i, tm, seg_t, M)

            def ext(c_ref, p_ref, n_ref):
                return jnp.concatenate([p_ref[...].astype(F32) * hp, c_ref[...].astype(F32),
                                        n_ref[...].astype(F32) * hn], axis=0)

            gs_ref[...] = ext(g_ref, gp_ref, gn_ref)
            lo, n = HALO - Q, tm + 2 * Q
            w0, w1, w2 = cw_ref[pl.ds(0, 1), :], cw_ref[pl.ds(1, 1), :], cw_ref[pl.ds(2, 1), :]
            gc = (gs_ref[pl.ds(lo - 1, n), :] * w0 + gs_ref[pl.ds(lo, n), :] * w1 + gs_ref[pl.ds(lo + 1, n), :] * w2
                  + cb_ref[...])
            ue = ext(u_ref, up_ref, un_ref)[lo:lo + n]
            ae = ext(a_ref, ap_ref, an_ref)[lo:lo + n]
            dgc = ae * ue * _dsilu(gc)
            ds_ref[...] = dgc
            dg = ds_ref[pl.ds(Q + 1, tm), :] * w0 + dgc[Q:Q + tm] * w1 + ds_ref[pl.ds(Q - 1, tm), :] * w2
            dz_ref[...] = dg.astype(BF16)
            du_ref[...] = (ae[Q:Q + tm] * _silu(gc[Q:Q + tm])).astype(BF16)
            own = dgc[Q:Q + tm]

            @pl.when(i == 0)
            def _():
                dcw_ref[...] = jnp.zeros_like(dcw_ref)
                dcb_ref[...] = jnp.zeros_like(dcb_ref)

            dcb_ref[...] += _colsum(own)
            for k in range(3):
                dcw_ref[pl.ds(k, 1), :] += _colsum(own * gs_ref[pl.ds(HALO - 1 + k, tm), :])

        @pl.when(p == 1)
        def _():
            dz_ref[...] = du_ref[...]

    def cur(off):
        return pl.BlockSpec((tm, tf), lambda j, i, p: (i, off + j))

    def prv(off):
        return pl.BlockSpec((HALO, tf), lambda j, i, p: (jnp.maximum(i * per - 1, 0), off + j))

    def nxt(off):
        return pl.BlockSpec((HALO, tf), lambda j, i, p: (jnp.minimum((i + 1) * per, last), off + j))

    return pl.pallas_call(
        body, out_shape=[SDS((M, F2), BF16), SDS((3, Fd), F32), SDS((1, Fd), F32)], grid=(nf, M // tm, 2),
        in_specs=[cur(0), prv(0), nxt(0), cur(nf), prv(nf), nxt(nf), cur(0), prv(0), nxt(0),
                  pl.BlockSpec((3, tf), lambda j, i, p: (0, j)), pl.BlockSpec((1, tf), lambda j, i, p: (0, j))],
        out_specs=[pl.BlockSpec((tm, tf), lambda j, i, p: (i, p * nf + j)),
                   pl.BlockSpec((3, tf), lambda j, i, p: (0, j)), pl.BlockSpec((1, tf), lambda j, i, p: (0, j))],
        scratch_shapes=[pltpu.VMEM((E, tf), F32), pltpu.VMEM((tm + 2 * Q, tf), F32), pltpu.VMEM((tm, tf), BF16)],
        compiler_params=_params(("parallel", "arbitrary", "arbitrary")), name=name)(
            z, z, z, z, z, z, da, da, da, conv_w, conv_b)


def _swap32(x):
    lane = lax.broadcasted_iota(jnp.int32, x.shape, 1)
    return jnp.where((lane % 64) < 32, pltpu.roll(x, 96, axis=1), pltpu.roll(x, 32, axis=1))


def _rms(x, g):
    r = lax.rsqrt(jnp.mean(x * x, axis=-1, keepdims=True) + EPS)
    return x * r * g


def _rms_bwd(x, g, dy):
    r = lax.rsqrt(jnp.mean(x * x, axis=-1, keepdims=True) + EPS)
    xh = x * r
    dxh = dy * g
    return r * (dxh - xh * jnp.mean(dxh * xh, axis=-1, keepdims=True)), _colsum(dy * xh)


def _mla_prep_fwd(z, gq, gkv, cos, sin, QL, KL, name):
    M, NZ = z.shape
    tm = _tile(M, (256, 128))

    def body(z_ref, gq_ref, gkv_ref, cos_ref, sin_ref, cq_ref, ckv_ref, kpe_ref):
        zv = z_ref[...]
        cq_ref[...] = _rms(zv[:, :QL], gq_ref[...]).astype(BF16)
        ckv_ref[...] = _rms(zv[:, QL:QL + KL], gkv_ref[...]).astype(BF16)
        kp = zv[:, QL + KL:]
        r = kp * cos_ref[...] + _swap32(kp) * sin_ref[...]
        lane = lax.broadcasted_iota(jnp.int32, r.shape, 1)
        kpe_ref[0] = jnp.where(lane < ROPE, r, 0.0).astype(BF16)
        kpe_ref[1] = jnp.where(lane >= ROPE, r, 0.0).astype(BF16)

    tab = pl.BlockSpec((tm, 128), lambda i: (i, 0))
    return pl.pallas_call(
        body, out_shape=[SDS((M, QL), BF16), SDS((M, KL), BF16), SDS((2, M, 128), BF16)], grid=(M // tm,),
        in_specs=[pl.BlockSpec((tm, NZ), lambda i: (i, 0)), pl.BlockSpec((1, QL), lambda i: (0, 0)),
                  pl.BlockSpec((1, KL), lambda i: (0, 0)), tab, tab],
        out_specs=[pl.BlockSpec((tm, QL), lambda i: (i, 0)), pl.BlockSpec((tm, KL), lambda i: (i, 0)),
                   pl.BlockSpec((2, tm, 128), lambda i: (0, i, 0))],
        compiler_params=_params(("parallel",)), name=name)(z, gq, gkv, cos, sin)


def _mla_prep_bwd(z, dcq, dckv, dkpe, gq, gkv, cos, sin, QL, KL, seg_t, name):
    M, NZ = z.shape
    H = dkpe.shape[0]
    tm = _row_tile(seg_t, M)
    nt = seg_t // tm

    def body(z_ref, dcq_ref, dckv_ref, dkpe_ref, gq_ref, gkv_ref, cos_ref, sin_ref, dz_ref, dgq_ref, dgkv_ref):
        i = pl.program_id(0)

        @pl.when(i == 0)
        def _():
            dgq_ref[...] = jnp.zeros_like(dgq_ref)
            dgkv_ref[...] = jnp.zeros_like(dgkv_ref)

        zv = z_ref[...]
        dyq = jnp.where(i < nt, dcq_ref[...], 0.0)
        dxq, dgq = _rms_bwd(zv[:, :QL], gq_ref[...], dyq)
        dxkv, dgkv = _rms_bwd(zv[:, QL:QL + KL], gkv_ref[...], dckv_ref[...])
        dgq_ref[...] += dgq
        dgkv_ref[...] += dgkv
        even = dkpe_ref[0]
        odd = dkpe_ref[1]
        for h in range(2, H, 2):
            even = even + dkpe_ref[h]
            odd = odd + dkpe_ref[h + 1]
        lane = lax.broadcasted_iota(jnp.int32, even.shape, 1)
        dr = jnp.where(lane < ROPE, even, odd)
        dkp = dr * cos_ref[...] - _swap32(dr) * sin_ref[...]
        dz_ref[...] = jnp.concatenate([dxq, dxkv, dkp], axis=1).astype(BF16)

    tab = pl.BlockSpec((tm, 128), lambda i: (i, 0))
    return pl.pallas_call(
        body, out_shape=[SDS((M, NZ), BF16), SDS((1, QL), F32), SDS((1, KL), F32)], grid=(M // tm,),
        in_specs=[pl.BlockSpec((tm, NZ), lambda i: (i, 0)),
                  pl.BlockSpec((tm, QL), lambda i: (jnp.minimum(i, nt - 1), 0)),
                  pl.BlockSpec((tm, KL), lambda i: (i, 0)), pl.BlockSpec((H, tm, 128), lambda i: (0, i, 0)),
                  pl.BlockSpec((1, QL), lambda i: (0, 0)), pl.BlockSpec((1, KL), lambda i: (0, 0)), tab, tab],
        out_specs=[pl.BlockSpec((tm, NZ), lambda i: (i, 0)), pl.BlockSpec((1, QL), lambda i: (0, 0)),
                   pl.BlockSpec((1, KL), lambda i: (0, 0))],
        compiler_params=_params(("arbitrary",)), name=name)(z, dcq, dckv, dkpe, gq, gkv, cos, sin)


def _qrope_fwd(q, cos, sin, HN, name):
    T, NQ = q.shape
    tm = _tile(T, (256, 128))

    def body(q_ref, cos_ref, sin_ref, o_ref):
        o_ref[:, :HN] = q_ref[:, :HN].astype(BF16)
        for cb in range((NQ - HN) // 128):
            sl = slice(HN + cb * 128, HN + (cb + 1) * 128)
            xv = q_ref[:, sl]
            o_ref[:, sl] = (xv * cos_ref[...] + _swap32(xv) * sin_ref[...]).astype(BF16)

    tab = pl.BlockSpec((tm, 128), lambda i: (i, 0))
    return pl.pallas_call(
        body, out_shape=SDS((T, NQ), BF16), grid=(T // tm,),
        in_specs=[pl.BlockSpec((tm, NQ), lambda i: (i, 0)), tab, tab],
        out_specs=pl.BlockSpec((tm, NQ), lambda i: (i, 0)),
        compiler_params=_params(("parallel",)), name=name)(q, cos, sin)


def _qrope_bwd(dqpe, dqa, cos, sin, HN, name):
    T, HW = dqpe.shape
    HR = HW // 2
    tm = _tile(T, (256, 128))

    def body(d_ref, dqa_hbm, cos_ref, sin_ref, o_ref):
        del dqa_hbm
        for pr in range(HR // 128):
            dr = d_ref[:, 2 * pr * 128:(2 * pr + 1) * 128] + d_ref[:, (2 * pr + 1) * 128:(2 * pr + 2) * 128]
            o_ref[:, pr * 128:(pr + 1) * 128] = (dr * cos_ref[...] - _swap32(dr) * sin_ref[...]).astype(BF16)

    tab = pl.BlockSpec((tm, 128), lambda i: (i, 0))
    return pl.pallas_call(
        body, out_shape=SDS(dqa.shape, BF16), grid=(T // tm,),
        in_specs=[pl.BlockSpec((tm, HW), lambda i: (i, 0)), pl.BlockSpec(memory_space=pl.ANY), tab, tab],
        out_specs=pl.BlockSpec((tm, HR), lambda i: (i, HN // HR)),
        input_output_aliases={1: 0}, compiler_params=_params(("parallel",)), name=name)(dqpe, dqa, cos, sin)


_NT = (((1,), (1,)), ((), ()))
_TN = (((0,), (0,)), ((), ()))


def _attn_fwd(qa, kv, kpe, T, H, name):
    M = kv.shape[0]
    tq = _tile(T, (512, 256, 128))
    scale = (NOPE + ROPE) ** -0.5

    def body(qn_ref, qp_ref, kv_ref, kpe_ref, o_ref, lse_ref):
        s = lax.dot_general(qn_ref[...], kv_ref[:, :NOPE], _NT, preferred_element_type=F32)
        s = (s + lax.dot_general(qp_ref[...], kpe_ref[0], _NT, preferred_element_type=F32)) * scale
        m = jnp.max(s, axis=-1, keepdims=True)
        p = jnp.exp(s - m)
        l = jnp.sum(p, axis=-1, keepdims=True)
        o = jnp.dot(p.astype(BF16), kv_ref[:, NOPE:], preferred_element_type=F32)
        o_ref[...] = (o / l).astype(BF16)
        lse_ref[...] = jnp.broadcast_to(m + jnp.log(l), lse_ref.shape)

    return pl.pallas_call(
        body, out_shape=[SDS((T, H * VDIM), BF16), SDS((T, H * 128), F32)], grid=(H, T // tq),
        in_specs=[pl.BlockSpec((tq, NOPE), lambda h, i: (i, h)), pl.BlockSpec((tq, 128), lambda h, i: (i, H + h // 2)),
                  pl.BlockSpec((M, NOPE + VDIM), lambda h, i: (0, h)), pl.BlockSpec((1, M, 128), lambda h, i: (h % 2, 0, 0))],
        out_specs=[pl.BlockSpec((tq, VDIM), lambda h, i: (i, h)), pl.BlockSpec((tq, 128), lambda h, i: (i, h))],
        compiler_params=_params(("parallel", "parallel")), name=name)(qa, qa, kv, kpe)


def _attn_bwd(qa, kv, kpe, do, lse, T, H, name):
    M = kv.shape[0]
    tq = _tile(T, (256, 128))
    nq = T // tq
    scale = (NOPE + ROPE) ** -0.5

    def body(qn_ref, qp_ref, kv_ref, kpe_ref, do_ref, lse_ref, dqa_ref, dqpe_ref, dkv_ref, dkpe_ref, dk_acc, dv_acc, dp_acc):
        i = pl.program_id(1)

        @pl.when(i == 0)
        def _():
            dk_acc[...] = jnp.zeros_like(dk_acc)
            dv_acc[...] = jnp.zeros_like(dv_acc)
            dp_acc[...] = jnp.zeros_like(dp_acc)

        qn, qp, dov = qn_ref[...], qp_ref[...], do_ref[...]
        kn, v, kp = kv_ref[:, :NOPE], kv_ref[:, NOPE:], kpe_ref[0]
        s = lax.dot_general(qn, kn, _NT, preferred_element_type=F32)
        s = (s + lax.dot_general(qp, kp, _NT, preferred_element_type=F32)) * scale
        p = jnp.exp(s - lse_ref[:, 0:1])
        dp = lax.dot_general(dov, v, _NT, preferred_element_type=F32)
        delta = jnp.sum(p * dp, axis=-1, keepdims=True)
        ds = (p * (dp - delta) * scale).astype(BF16)
        dqa_ref[...] = jnp.dot(ds, kn, preferred_element_type=F32).astype(BF16)
        dqpe_ref[...] = jnp.dot(ds, kp, preferred_element_type=F32)
        dv_acc[...] += lax.dot_general(p.astype(BF16), dov, _TN, preferred_element_type=F32)
        dk_acc[...] += lax.dot_general(ds, qn, _TN, preferred_element_type=F32)
        dp_acc[...] += lax.dot_general(ds, qp, _TN, preferred_element_type=F32)

        @pl.when(i == nq - 1)
        def _():
            dkv_ref[:, :NOPE] = dk_acc[...].astype(BF16)
            dkv_ref[:, NOPE:] = dv_acc[...].astype(BF16)
            dkpe_ref[0] = dp_acc[...]

    return pl.pallas_call(
        body,
        out_shape=[SDS((T, H * (NOPE + ROPE)), BF16), SDS((T, H * 128), F32), SDS((M, H * (NOPE + VDIM)), BF16),
                   SDS((H, M, 128), F32)],
        grid=(H, nq),
        in_specs=[pl.BlockSpec((tq, NOPE), lambda h, i: (i, h)), pl.BlockSpec((tq, 128), lambda h, i: (i, H + h // 2)),
                  pl.BlockSpec((M, NOPE + VDIM), lambda h, i: (0, h)), pl.BlockSpec((1, M, 128), lambda h, i: (h % 2, 0, 0)),
                  pl.BlockSpec((tq, VDIM), lambda h, i: (i, h)), pl.BlockSpec((tq, 128), lambda h, i: (i, h))],
        out_specs=[pl.BlockSpec((tq, NOPE), lambda h, i: (i, h)), pl.BlockSpec((tq, 128), lambda h, i: (i, h)),
                   pl.BlockSpec((M, NOPE + VDIM), lambda h, i: (0, h)), pl.BlockSpec((1, M, 128), lambda h, i: (h, 0, 0))],
        scratch_shapes=[pltpu.VMEM((M, NOPE), F32), pltpu.VMEM((M, VDIM), F32), pltpu.VMEM((M, 128), F32)],
        compiler_params=_params(("parallel", "arbitrary")), name=name)(qa, qa, kv, kpe, do, lse)


def _adamw(w, m, v, name, g=None, recv=None):
    R, C = w.shape
    tr = R
    for cand in (1024, 512, 256, 128, 64, 32, 16, 8):
        if R % cand == 0 and cand * C <= 131072:
            tr = cand
            break
    c1 = 1.0 - ADAM_B1 ** ADAM_STEP
    c2 = 1.0 - ADAM_B2 ** ADAM_STEP
    summed = recv is not None

    def body(*refs):
        w_ref, m_ref, v_ref, g_ref = refs[:4]
        outs = refs[4:]
        if summed:
            gv = g_ref[0].astype(F32)
            for d in range(1, N_DEV):
                gv = gv + g_ref[d].astype(F32)
            outs[0][...] = gv
            outs = outs[1:]
        else:
            gv = g_ref[...]
        d_ref, nm_ref, nv_ref = outs
        mn = ADAM_B1 * m_ref[...] + (1.0 - ADAM_B1) * gv
        vn = ADAM_B2 * v_ref[...] + (1.0 - ADAM_B2) * (gv * gv)
        nm_ref[...] = mn
        nv_ref[...] = vn
        d_ref[...] = -ADAM_LR * ((mn / c1) / (jnp.sqrt(vn / c2) + ADAM_EPS) + ADAM_WD * w_ref[...])

    blk = pl.BlockSpec((tr, C), lambda i: (i, 0))
    g_spec = pl.BlockSpec((N_DEV, tr, C), lambda i: (0, i, 0)) if summed else blk
    n_out = 4 if summed else 3
    return pl.pallas_call(
        body, out_shape=[SDS((R, C), F32)] * n_out, grid=(R // tr,), in_specs=[blk, blk, blk, g_spec],
        out_specs=[blk] * n_out, compiler_params=_params(("parallel",)), name=name)(w, m, v, recv if summed else g)


WEIGHTS = ['c_ctx', 'norm1_g', 'norm2_g', 'w_ada', 'b_ada', 'ab_w_in', 'ab_b_in', 'a_ln_g', 'a_ln_b', 'a_w_s', 'a_b_s',
           'b_conv_w', 'b_conv_b', 'b_ln_g', 'b_ln_b', 'ab_w_out', 'mla_w_in', 'mla_q_norm_g', 'mla_w_uq',
           'mla_kv_norm_g', 'mla_w_ukv', 'mla_w_o', 'ffn_w_up', 'ffn_conv_w', 'ffn_conv_b', 'ffn_w_down', 'final_norm_g']


def _pack(parts):
    flat = jnp.concatenate([p.reshape(-1).astype(F32) for p in parts])
    n = flat.shape[0]
    n_pad = -(-n // 1024) * 1024
    return jnp.pad(flat, (0, n_pad - n)).reshape(n_pad // 128, 128)


def _unpack(flat, like):
    out, off = [], 0
    for shp in like:
        n = math.prod(shp)
        out.append(flat[..., off:off + n].reshape(flat.shape[:-1] + tuple(shp)))
        off += n
    return out


def _rope_tables(T, Tc):
    rows = T // GRID_W
    row = jnp.repeat(jnp.arange(rows, dtype=F32), GRID_W)
    col = jnp.tile(jnp.arange(GRID_W, dtype=F32), rows)
    n_freq = ROPE // 4
    inv = ROPE_THETA ** (-jnp.arange(n_freq, dtype=F32) / n_freq)
    ang = jnp.concatenate([row[:, None] * inv, col[:, None] * inv], axis=-1)
    cos, sin = jnp.cos(ang), jnp.sin(ang)
    cos = jnp.tile(cos, (1, 128 // (ROPE // 2)))
    sin = jnp.tile(jnp.concatenate([-sin, sin], axis=1), (1, 128 // ROPE))
    return (jnp.concatenate([cos, jnp.ones((Tc, 128), F32)], axis=0),
            jnp.concatenate([sin, jnp.zeros((Tc, 128), F32)], axis=0))


def _step(a):
    ax, ay, ac = lax.axis_index("x"), lax.axis_index("y"), lax.axis_index("c")
    me = 4 * ax + 2 * ay + ac
    T, D = a['x'].shape[1:]
    Tc = a['ctx'].shape[1]
    M = T + Tc
    W, Wb = a['a_ln_g'].shape[1], a['b_ln_g'].shape[1]
    assert W == Wb and T % Tc == 0
    Fd = a['ffn_conv_b'].shape[1]
    QL, KL = a['mla_q_norm_g'].shape[1] * N_DEV, a['mla_kv_norm_g'].shape[1] * N_DEV
    H = a['mla_w_ukv'].shape[2] * N_DEV // (NOPE + VDIM)
    HN, HR = H * NOPE, H * ROPE
    kw = a['b_conv_w'].shape[1]
    NA = a['w_ada'].shape[2]
    bf = lambda t: t.astype(BF16)

    small_shapes = [(D,), (kw, Wb // N_DEV), (2, 3, Fd // N_DEV), (QL // N_DEV,), (KL // N_DEV,)]
    g_small = _all_gather(_pack([a['c'][0], a['b_conv_w'][0], a['ffn_conv_w'], a['mla_q_norm_g'][0], a['mla_kv_norm_g'][0]]),
                          "ag_small")
    c_all, bcw, fcw, gq, gkv = _unpack(g_small.reshape(N_DEV, -1), small_shapes)
    bcw = jnp.transpose(bcw, (1, 0, 2)).reshape(kw, Wb)
    fcw = jnp.transpose(fcw, (1, 2, 0, 3)).reshape(2, 3, Fd)
    gq, gkv = gq.reshape(1, QL), gkv.reshape(1, KL)

    a16 = jnp.concatenate([c_all, a['c_ctx'][None], jnp.zeros((N_DEV - 1, D), F32)], axis=0)
    b_loc = lax.dynamic_slice(a['b_ada'], (0, me * NA), (2, NA))
    mods = [_mm(a16, a['w_ada'][l], mode="nn", out_dtype=F32, name=f"ada_fwd{l}", bias=b_loc[l:l + 1], a_silu=True)
            for l in range(2)]
    gm = _all_gather(jnp.concatenate(mods, axis=0), "ag_mod").reshape(N_DEV, 2, 2 * N_DEV, NA)
    gm = jnp.transpose(gm, (1, 2, 0, 3)).reshape(2, 2 * N_DEV, 6 * D)
    mod_lat = [lax.dynamic_slice(gm[l], (me, 0), (1, 6 * D)).reshape(6, 1, 1, D) for l in range(2)]
    mod_ctx = [gm[l][N_DEV].reshape(6, 1, 1, D) for l in range(2)]

    def mod(l, k, both):
        return jnp.concatenate([mod_lat[l][k], mod_ctx[l][k]], axis=0) if both else mod_lat[l][k]

    def gather_cols(wl, name):
        g = _all_gather(bf(wl), name)
        return jnp.transpose(g, (1, 0, 2)).reshape(wl.shape[0], -1)

    def gather_rows(wl, name):
        g = _all_gather(bf(wl), name)
        return g.reshape(-1, wl.shape[1])

    w_abin = gather_cols(a['ab_w_in'][0], "ag_ab_w_in")
    w_about = gather_rows(a['ab_w_out'][0], "ag_ab_w_out")
    w_in = gather_rows(a['mla_w_in'][0], "ag_mla_w_in")
    w_in = jnp.concatenate([w_in, w_in[:, QL + KL:]], axis=1)
    w_uq = gather_cols(a['mla_w_uq'][0], "ag_mla_w_uq").reshape(QL, H, NOPE + ROPE)
    w_uq = jnp.concatenate([w_uq[:, :, :NOPE].reshape(QL, HN), w_uq[:, :, NOPE:].reshape(QL, HR)], axis=1)
    w_ukv = gather_cols(a['mla_w_ukv'][0], "ag_mla_w_ukv")
    w_o = gather_rows(a['mla_w_o'][0], "ag_mla_w_o")
    g_up = _all_gather(bf(a['ffn_w_up']).reshape(2 * D, -1), "ag_ffn_w_up")
    w_up = [jnp.transpose(g_up[:, l * D:(l + 1) * D], (1, 0, 2)).reshape(D, 2 * Fd) for l in range(2)]
    g_dn = _all_gather(bf(a['ffn_w_down']).reshape(2 * Fd // N_DEV, D), "ag_ffn_w_down")
    w_dn = [g_dn[:, l * Fd // N_DEV:(l + 1) * Fd // N_DEV].reshape(Fd, D) for l in range(2)]

    cos, sin = _rope_tables(T, Tc)
    n1g, n2g = a['norm1_g'], a['norm2_g']
    a_bst = a['a_b_s'][0].T
    mm = functools.partial(_mm)

    x0 = jnp.concatenate([a['x'][0], a['ctx'][0]], axis=0)
    h1 = _normmod_fwd(x0, n1g[0:1], mod(0, 0, True), mod(0, 1, True), T, "l0_norm1")
    z = mm(h1, w_abin, mode="nn", out_dtype=F32, name="l0_ab_in", bias=a['ab_b_in'])
    y = _gmlp_fwd(z, a['a_ln_g'], a['a_ln_b'], a['a_w_s'][0], a_bst, W, "l0_gmlp")
    y = _conf_fwd(z, y, bcw, a['b_conv_b'], a['b_ln_g'], a['b_ln_b'], W, Wb, T, "l0_conf")
    x1, o1 = mm(y, w_about, mode="nn", out_dtype=F32, name="l0_ab_out", res=x0, gate=mod(0, 2, True), seg_t=T)
    h2 = _normmod_fwd(x1, n2g[0:1], mod(0, 3, True), mod(0, 4, True), T, "l0_norm2")
    z2 = mm(h2, w_up[0], mode="nn", out_dtype=BF16, name="l0_up")
    a2 = _ffn_act_fwd(z2, fcw[0], a['ffn_conv_b'][0:1], T, "l0_act")
    x2, o2 = mm(a2, w_dn[0], mode="nn", out_dtype=F32, name="l0_down", res=x1, gate=mod(0, 5, True), seg_t=T)

    h3 = _normmod_fwd(x2, n1g[1:2], mod(1, 0, True), mod(1, 1, True), T, "l1_norm1")
    z3 = mm(h3, w_in, mode="nn", out_dtype=F32, name="l1_mla_in")
    cqn, ckvn, kpe = _mla_prep_fwd(z3, gq, gkv, cos, sin, QL, KL, "l1_prep")
    q = mm(cqn, w_uq, mode="nn", out_dtype=F32, name="l1_uq", rows=T)
    kv = mm(ckvn, w_ukv, mode="nn", out_dtype=BF16, name="l1_ukv")
    qa = _qrope_fwd(q, cos, sin, HN, "l1_qrope")
    o_att, lse = _attn_fwd(qa, kv, kpe, T, H, "l1_attn")
    x3, o3 = mm(o_att, w_o, mode="nn", out_dtype=F32, name="l1_wo", res=x2, gate=mod(1, 2, False), seg_t=T)
    h4 = _normmod_fwd(x3, n2g[1:2], mod(1, 3, False), mod(1, 4, False), T, "l1_norm2")
    z4 = mm(h4, w_up[1], mode="nn", out_dtype=BF16, name="l1_up")
    a4 = _ffn_act_fwd(z4, fcw[1], a['ffn_conv_b'][1:2], T, "l1_act")
    x4, o4 = mm(a4, w_dn[1], mode="nn", out_dtype=F32, name="l1_down", res=x3, gate=mod(1, 5, False), seg_t=T)

    dx4, loss_cols, d_fng, do4, dg2_1 = _final(x4, a['final_norm_g'][None], a['loss_target'][0], o4, mod(1, 5, False),
                                               "final")
    loss = lax.psum(jnp.sum(loss_cols), ("x", "y", "c"))

    da4 = mm(do4, w_dn[1], mode="nt", out_dtype=BF16, name="l1_down_dx")
    dw_dn1 = mm(a4, do4, mode="tn", out_dtype=BF16, name="l1_down_dw")
    dz4, dfcw1, dfcb1 = _ffn_act_bwd(z4, da4, fcw[1], a['ffn_conv_b'][1:2], T, "l1_act_bwd")
    dh4 = mm(dz4, w_up[1], mode="nt", out_dtype=F32, name="l1_up_dx")
    dw_up1 = mm(h4, dz4, mode="tn", out_dtype=BF16, name="l1_up_dw")
    dx3, dn2g1, dsh2_1, dsc2_1, do3, dg1_1 = _normmod_bwd(x3, n2g[1:2], mod(1, 4, False), dh4, dx4, T, "l1_norm2_bwd",
                                                         o_prev=o3, gate_prev=mod(1, 2, False))
    d_oatt = mm(do3, w_o, mode="nt", out_dtype=BF16, name="l1_wo_dx")
    dw_o = mm(o_att, do3, mode="tn", out_dtype=BF16, name="l1_wo_dw")
    dqa, dqpe, dkv, dkpe = _attn_bwd(qa, kv, kpe, d_oatt, lse, T, H, "l1_attn_bwd")
    dqa = _qrope_bwd(dqpe, dqa, cos, sin, HN, "l1_qrope_bwd")
    dcq = mm(dqa, w_uq, mode="nt", out_dtype=F32, name="l1_uq_dx")
    dw_uq = mm(cqn, dqa, mode="tn", out_dtype=BF16, name="l1_uq_dw", rows=T)
    dckv = mm(dkv, w_ukv, mode="nt", out_dtype=F32, name="l1_ukv_dx")
    dw_ukv = mm(ckvn, dkv, mode="tn", out_dtype=BF16, name="l1_ukv_dw")
    dz3, dgq, dgkv = _mla_prep_bwd(z3, dcq, dckv, dkpe, gq, gkv, cos, sin, QL, KL, T, "l1_prep_bwd")
    dh3 = mm(dz3, w_in, mode="nt", out_dtype=F32, name="l1_mla_in_dx")
    dw_in = mm(h3, dz3, mode="tn", out_dtype=BF16, name="l1_mla_in_dw")
    dx2, dn1g1, dsh1_1, dsc1_1, do2, dg2_0 = _normmod_bwd(x2, n1g[1:2], mod(1, 1, True), dh3, dx3, T, "l1_norm1_bwd",
                                                         o_prev=o2, gate_prev=mod(0, 5, True))
    da2 = mm(do2, w_dn[0], mode="nt", out_dtype=BF16, name="l0_down_dx")
    dw_dn0 = mm(a2, do2, mode="tn", out_dtype=BF16, name="l0_down_dw")
    dz2, dfcw0, dfcb0 = _ffn_act_bwd(z2, da2, fcw[0], a['ffn_conv_b'][0:1], T, "l0_act_bwd")
    dh2 = mm(dz2, w_up[0], mode="nt", out_dtype=F32, name="l0_up_dx")
    dw_up0 = mm(h2, dz2, mode="tn", out_dtype=BF16, name="l0_up_dw")
    dx1, dn2g0, dsh2_0, dsc2_0, do1, dg1_0 = _normmod_bwd(x1, n2g[0:1], mod(0, 4, True), dh2, dx2, T, "l0_norm2_bwd",
                                                         o_prev=o1, gate_prev=mod(0, 2, True))
    dy = mm(do1, w_about, mode="nt", out_dtype=F32, name="l0_ab_out_dx")
    dw_about = mm(y, do1, mode="tn", out_dtype=BF16, name="l0_ab_out_dw")
    dz, dlag, dlab, dws, dbs, dbin_a = _gmlp_bwd(z, dy, a['a_ln_g'], a['a_ln_b'], a['a_w_s'][0], a_bst, W, "l0_gmlp_bwd")
    dhc, dlbg, dlbb, dbcb = _conf_bwd1(z, dy, bcw, a['b_conv_b'], a['b_ln_g'], a['b_ln_b'], W, Wb, T, "l0_conf_bwd1")
    dz, dbcw, dbin_b = _conf_bwd2(z, dhc, dz, bcw, W, Wb, T, "l0_conf_bwd2")
    dh1 = mm(dz, w_abin, mode="nt", out_dtype=F32, name="l0_ab_in_dx")
    dw_abin = mm(h1, dz, mode="tn", out_dtype=BF16, name="l0_ab_in_dw")
    dx0, dn1g0, dsh1_0, dsc1_0 = _normmod_bwd(x0, n1g[0:1], mod(0, 1, True), dh1, dx1, T, "l0_norm1_bwd")

    zero = jnp.zeros((D,), F32)
    dmod = jnp.stack([
        jnp.stack([jnp.stack([dsh1_0[0, 0], dsc1_0[0, 0], dg1_0[0, 0], dsh2_0[0, 0], dsc2_0[0, 0], dg2_0[0, 0]]),
                   jnp.stack([dsh1_0[1, 0], dsc1_0[1, 0], dg1_0[1, 0], dsh2_0[1, 0], dsc2_0[1, 0], dg2_0[1, 0]])]),
        jnp.stack([jnp.stack([dsh1_1[0, 0], dsc1_1[0, 0], dg1_1[0, 0], dsh2_1[0, 0], dsc2_1[0, 0], dg2_1[0, 0]]),
                   jnp.stack([dsh1_1[1, 0], dsc1_1[1, 0], zero, zero, zero, zero])])])
    small = {
        'norm1_g': jnp.concatenate([dn1g0, dn1g1], axis=0), 'norm2_g': jnp.concatenate([dn2g0, dn2g1], axis=0),
        'ab_b_in': jnp.concatenate([dbin_a, dbin_b], axis=1), 'a_ln_g': dlag, 'a_ln_b': dlab, 'a_w_s': dws[None],
        'a_b_s': jnp.sum(dbs, axis=-1)[None], 'b_conv_w': dbcw, 'b_conv_b': dbcb, 'b_ln_g': dlbg, 'b_ln_b': dlbb,
        'mla_q_norm_g': dgq, 'mla_kv_norm_g': dgkv, 'ffn_conv_w': jnp.stack([dfcw0, dfcw1]),
        'ffn_conv_b': jnp.concatenate([dfcb0, dfcb1], axis=0), 'final_norm_g': d_fng[0],
    }
    names = list(small)
    g2 = _all_gather(_pack([dmod] + [small[n] for n in names]), "ag_small_grads")
    red = _sum_lead(g2, "sum_small_grads").reshape(-1)
    red = dict(zip(names, _unpack(red, [(2, 2, 6, D)] + [small[n].shape for n in names])[1:]))
    dmod_all = g2.reshape(N_DEV, -1)[:, :2 * 2 * 6 * D].reshape(N_DEV, 2, 2, 6 * D)

    a16g = jnp.concatenate([c_all, jnp.tile(a['c_ctx'][None], (N_DEV, 1))], axis=0)
    dm_loc = lax.dynamic_slice(dmod_all, (0, 0, 0, me * NA), (N_DEV, 2, 2, NA))
    g_wada, cpart = [], []
    for l in range(2):
        dm16 = jnp.concatenate([dm_loc[:, l, 0], dm_loc[:, l, 1]], axis=0)
        g_wada.append(mm(a16g, dm16, mode="tn", out_dtype=F32, name=f"ada_dw{l}", a_silu=True))
        cpart.append(mm(dm_loc[:, l, 1], a['w_ada'][l], mode="nt", out_dtype=F32, name=f"ada_dc{l}"))
    g_bada = _sum_lead(jnp.transpose(dmod_all, (0, 2, 1, 3)).reshape(2 * N_DEV, 2 * 6 * D // 128, 128), "sum_b_ada")
    g_cc = _all_gather(jnp.concatenate(cpart, axis=0), "ag_c_ctx")
    g_cc = _sum_lead(g_cc.reshape(2 * N_DEV * N_DEV, D // 128, 128), "sum_c_ctx").reshape(D)
    grads = {
        'c_ctx': g_cc * _dsilu(a['c_ctx']), 'w_ada': jnp.stack(g_wada), 'b_ada': g_bada.reshape(2, 6 * D),
        'b_conv_w': lax.dynamic_slice(red['b_conv_w'], (0, me * (Wb // N_DEV)), (kw, Wb // N_DEV))[None],
        'ffn_conv_w': lax.dynamic_slice(red['ffn_conv_w'], (0, 0, me * (Fd // N_DEV)), (2, 3, Fd // N_DEV)),
        'mla_q_norm_g': lax.dynamic_slice(red['mla_q_norm_g'], (0, me * (QL // N_DEV)), (1, QL // N_DEV)),
        'mla_kv_norm_g': lax.dynamic_slice(red['mla_kv_norm_g'], (0, me * (KL // N_DEV)), (1, KL // N_DEV)),
    }
    for n in names:
        if n not in grads:
            grads[n] = red[n].reshape(a[n].shape)

    def cols(dw):
        k, n = dw.shape
        return jnp.transpose(dw.reshape(k, N_DEV, n // N_DEV), (1, 0, 2))

    def rows(dw):
        return dw.reshape(N_DEV, dw.shape[0] // N_DEV, dw.shape[1])

    dw_in = dw_in.astype(F32)
    dw_in = jnp.concatenate([dw_in[:, :QL + KL], dw_in[:, QL + KL:QL + KL + ROPE] + dw_in[:, QL + KL + ROPE:QL + KL + 2 * ROPE]],
                            axis=1).astype(BF16)
    dw_uq = jnp.concatenate([dw_uq[:, :HN].reshape(QL, H, NOPE), dw_uq[:, HN:].reshape(QL, H, ROPE)], axis=2)
    parts = {
        'ab_w_in': cols(dw_abin), 'ab_w_out': rows(dw_about), 'mla_w_in': rows(dw_in),
        'mla_w_uq': cols(dw_uq.reshape(QL, H * (NOPE + ROPE))), 'mla_w_ukv': cols(dw_ukv), 'mla_w_o': rows(dw_o),
        'ffn_w_up': jnp.concatenate([cols(dw_up0), cols(dw_up1)], axis=1),
        'ffn_w_down': jnp.concatenate([rows(dw_dn0), rows(dw_dn1)], axis=1),
    }
    out = {}
    for n in WEIGHTS:
        shp = a[n].shape
        w2 = a[n].reshape(-1, shp[-1])
        m2, v2 = a['m_' + n].reshape(w2.shape), a['v_' + n].reshape(w2.shape)
        if n in parts:
            recv = _all_to_all(parts[n], "a2a_" + n)
            res = _adamw(w2, m2, v2, "adamw_" + n, recv=recv)
        else:
            g2d = grads[n].reshape(w2.shape)
            res = (g2d,) + tuple(_adamw(w2, m2, v2, "adamw_" + n, g=g2d))
        out[n] = [r.reshape(shp) for r in res]
    return (loss, dx0[:T][None], *[out[n][0] for n in WEIGHTS], *[out[n][1] for n in WEIGHTS],
            *[out[n][2] for n in WEIGHTS], *[out[n][3] for n in WEIGHTS])


def kernel(x, c, ctx, c_ctx, norm1_g, norm2_g, w_ada, b_ada, ab_w_in, ab_b_in, a_ln_g, a_ln_b, a_w_s, a_b_s, b_conv_w, b_conv_b, b_ln_g, b_ln_b, ab_w_out, mla_w_in, mla_q_norm_g, mla_w_uq, mla_kv_norm_g, mla_w_ukv, mla_w_o, ffn_w_up, ffn_conv_w, ffn_conv_b, ffn_w_down, final_norm_g, loss_target, m_c_ctx, m_norm1_g, m_norm2_g, m_w_ada, m_b_ada, m_ab_w_in, m_ab_b_in, m_a_ln_g, m_a_ln_b, m_a_w_s, m_a_b_s, m_b_conv_w, m_b_conv_b, m_b_ln_g, m_b_ln_b, m_ab_w_out, m_mla_w_in, m_mla_q_norm_g, m_mla_w_uq, m_mla_kv_norm_g, m_mla_w_ukv, m_mla_w_o, m_ffn_w_up, m_ffn_conv_w, m_ffn_conv_b, m_ffn_w_down, m_final_norm_g, v_c_ctx, v_norm1_g, v_norm2_g, v_w_ada, v_b_ada, v_ab_w_in, v_ab_b_in, v_a_ln_g, v_a_ln_b, v_a_w_s, v_a_b_s, v_b_conv_w, v_b_conv_b, v_b_ln_g, v_b_ln_b, v_ab_w_out, v_mla_w_in, v_mla_q_norm_g, v_mla_w_uq, v_mla_kv_norm_g, v_mla_w_ukv, v_mla_w_o, v_ffn_w_up, v_ffn_conv_w, v_ffn_conv_b, v_ffn_w_down, v_final_norm_g):
    return _step(dict(locals()))
```

```python
import functools
import math

import jax
import jax.numpy as jnp
from jax import lax
from jax.experimental import pallas as pl
from jax.experimental.pallas import tpu as pltpu

F32 = jnp.float32
BF16 = jnp.bfloat16
SDS = jax.ShapeDtypeStruct

N_DEV = 8
EPS = 1e-6
CHUNK = 128
NOPE = 128
ROPE = 64
VDIM = 128
GRID_W = 64
ROPE_THETA = 10000.0
HALO = 16
ADAM_LR, ADAM_B1, ADAM_B2, ADAM_EPS, ADAM_WD, ADAM_STEP = 0.001, 0.9, 0.999, 1e-08, 0.01, 10
VMEM_LIMIT = 56 * 1024 * 1024


def _tile(n, prefs):
    for p in prefs:
        if n % p == 0:
            return p
    return n


def _params(sem, vmem=VMEM_LIMIT):
    return pltpu.CompilerParams(dimension_semantics=sem, vmem_limit_bytes=vmem)


def _sigmoid(x):
    return 1.0 / (1.0 + jnp.exp(-x))


def _silu(x):
    return x * _sigmoid(x)


def _dsilu(x):
    s = _sigmoid(x)
    return s * (1.0 + x * (1.0 - s))


_GELU_C = math.sqrt(2.0 / math.pi)


def _gelu(x):
    return 0.5 * x * (1.0 + jnp.tanh(_GELU_C * (x + 0.044715 * x * x * x)))


def _dgelu(x):
    t = jnp.tanh(_GELU_C * (x + 0.044715 * x * x * x))
    return 0.5 * (1.0 + t) + 0.5 * x * (1.0 - t * t) * _GELU_C * (1.0 + 3.0 * 0.044715 * x * x)


def _colsum(v):
    return jnp.sum(v, axis=0, keepdims=True)


def _xchg_copies(x_ref, o_ref, send_sems, recv_sems, local_sem, scatter, with_recvs):
    ax, ay, ac = lax.axis_index("x"), lax.axis_index("y"), lax.axis_index("c")
    me = 4 * ax + 2 * ay + ac

    def src(d):
        return x_ref.at[d] if scatter else x_ref

    own = pltpu.make_async_copy(src(me), o_ref.at[me], local_sem)
    sends, recvs = [], []
    for k in range(1, N_DEV):
        px, py, pc = ax ^ (k >> 2), ay ^ ((k >> 1) & 1), ac ^ (k & 1)
        d = 4 * px + 2 * py + pc
        for dst, lst in ((o_ref.at[me], sends), (o_ref.at[d], recvs)):
            if lst is sends or with_recvs:
                lst.append(pltpu.make_async_remote_copy(
                    src_ref=src(d), dst_ref=dst, send_sem=send_sems.at[k - 1], recv_sem=recv_sems.at[k - 1],
                    device_id=(px, py, pc), device_id_type=pl.DeviceIdType.MESH))
    return own, sends, recvs


def _xchg_start(*refs, scatter):
    own, sends, _ = _xchg_copies(*refs, scatter, False)
    own.start()
    for cp in sends:
        cp.start()


def _xchg_wait(*refs, scatter):
    own, sends, recvs = _xchg_copies(*refs, scatter, True)
    for cp in recvs:
        cp.wait_recv()
    for cp in sends:
        cp.wait_send()
    own.wait()


_XCHG_SEMS = [pltpu.SemaphoreType.DMA((N_DEV - 1,)), pltpu.SemaphoreType.DMA((N_DEV - 1,)), pltpu.SemaphoreType.DMA]


def _xchg_shape(x, scatter):
    return SDS((N_DEV,) + tuple(x.shape[1:] if scatter else x.shape), x.dtype)


def _exchange(x, *, scatter, name):
    def body(*refs):
        _xchg_start(*refs, scatter=scatter)
        _xchg_wait(*refs, scatter=scatter)

    return pl.pallas_call(
        body, out_shape=_xchg_shape(x, scatter),
        in_specs=[pl.BlockSpec(memory_space=pl.ANY)], out_specs=pl.BlockSpec(memory_space=pl.ANY),
        scratch_shapes=list(_XCHG_SEMS), name=name)(x)


def _carried(body, carry, n_in, n_out, n_scratch, first, last):
    nc = len(carry)

    def wrapped(*refs):
        ins, cin = refs[:n_in], refs[n_in:n_in + nc]
        o0 = n_in + nc
        outs, cout = refs[o0:o0 + n_out], refs[o0 + n_out:o0 + n_out + nc]
        scr = refs[o0 + n_out + nc:]
        sems = scr[n_scratch:]

        @pl.when(first())
        def _():
            for c in range(nc):
                _xchg_start(cin[c], cout[c], *sems[3 * c:3 * c + 3], scatter=carry[c][1])

        body(*ins, *outs, *scr[:n_scratch])

        @pl.when(last())
        def _():
            for c in range(nc):
                _xchg_wait(cin[c], cout[c], *sems[3 * c:3 * c + 3], scatter=carry[c][1])

    return wrapped


def _carry_call(body, carry, *, grid, out_shape, in_specs, out_specs, scratch_shapes, sem, name, ins):
    carry = carry or []
    nc = len(carry)
    if nc:
        def first():
            f = pl.program_id(0) == 0
            for ax in range(1, len(grid)):
                f = f & (pl.program_id(ax) == 0)
            return f

        def last():
            f = pl.program_id(0) == grid[0] - 1
            for ax in range(1, len(grid)):
                f = f & (pl.program_id(ax) == grid[ax] - 1)
            return f

        body = _carried(body, carry, len(in_specs), len(out_shape), len(scratch_shapes), first, last)
        anyspec = pl.BlockSpec(memory_space=pl.ANY)
        in_specs = list(in_specs) + [anyspec] * nc
        out_specs = list(out_specs) + [anyspec] * nc
        out_shape = list(out_shape) + [_xchg_shape(x, sc) for x, sc in carry]
        scratch_shapes = list(scratch_shapes) + list(_XCHG_SEMS) * nc
        ins = list(ins) + [x for x, _ in carry]
        sem = ("arbitrary",) * len(grid)
    out = pl.pallas_call(body, out_shape=out_shape, grid=grid, in_specs=in_specs, out_specs=out_specs,
                         scratch_shapes=scratch_shapes, compiler_params=_params(sem), name=name)(*ins)
    n_main = len(out) - nc
    return list(out[:n_main]), list(out[n_main:])


def _all_gather(x, name):
    return _exchange(x, scatter=False, name=name)


def _all_to_all(x, name):
    return _exchange(x, scatter=True, name=name)


def _sum_lead(x, name):
    n, R, C = x.shape
    tr = _tile(R, (512, 256, 128, 64, 32, 16, 8))

    def body(x_ref, o_ref):
        acc = x_ref[0]
        for d in range(1, n):
            acc = acc + x_ref[d]
        o_ref[...] = acc

    return pl.pallas_call(
        body, out_shape=SDS((R, C), F32), grid=(R // tr,),
        in_specs=[pl.BlockSpec((n, tr, C), lambda i: (0, i, 0))], out_specs=pl.BlockSpec((tr, C), lambda i: (i, 0)),
        compiler_params=_params(("parallel",)), name=name)(x)


_TP = (1408, 1088, 1024, 768, 512, 256, 128)
_TQ = (1408, 1024, 768, 512, 256, 128)
_TR = (1408, 1024, 512, 256, 128)


def _mm(a, b, *, mode, out_dtype, name, rows=None, bias=None, res=None, gate=None, seg_t=None, a_silu=False, carry=None):
    if mode == "nn":
        P, R, Q = rows or a.shape[0], a.shape[1], b.shape[1]
    elif mode == "nt":
        P, R, Q = rows or a.shape[0], a.shape[1], b.shape[0]
    else:
        R, P, Q = rows or a.shape[0], a.shape[1], b.shape[1]
    tp, tq, tr = _tile(P, _TP), _tile(Q, _TQ), _tile(R, _TR if mode != "tn" else (512, 256, 128))
    nk = R // tr
    if mode == "nn":
        a_spec = pl.BlockSpec((tp, tr), lambda i, j, k: (i, k))
        b_spec = pl.BlockSpec((tr, tq), lambda i, j, k: (k, j))
        dims = (((1,), (0,)), ((), ()))
    elif mode == "nt":
        a_spec = pl.BlockSpec((tp, tr), lambda i, j, k: (i, k))
        b_spec = pl.BlockSpec((tq, tr), lambda i, j, k: (j, k))
        dims = (((1,), (1,)), ((), ()))
    else:
        a_spec = pl.BlockSpec((tr, tp), lambda i, j, k: (k, i))
        b_spec = pl.BlockSpec((tr, tq), lambda i, j, k: (k, j))
        dims = (((0,), (0,)), ((), ()))
    ins, in_specs = [a, b], [a_spec, b_spec]
    if bias is not None:
        ins.append(bias)
        in_specs.append(pl.BlockSpec((1, tq), lambda i, j, k: (0, j)))
    gated = res is not None
    if gated:
        n_seg = gate.shape[0]
        ins += [res, gate]
        in_specs += [pl.BlockSpec((tp, tq), lambda i, j, k: (i, j)),
                     pl.BlockSpec((n_seg, 1, tq), lambda i, j, k: (0, 0, j))]
    out_shape = [SDS((P, Q), out_dtype)]
    out_specs = [pl.BlockSpec((tp, tq), lambda i, j, k: (i, j))]
    if gated:
        out_shape.append(SDS((P, Q), BF16))
        out_specs.append(pl.BlockSpec((tp, tq), lambda i, j, k: (i, j)))

    def body(*refs):
        a_ref, b_ref = refs[0], refs[1]
        pos = 2
        bias_ref = res_ref = gate_ref = o2_ref = None
        if bias is not None:
            bias_ref = refs[pos]
            pos += 1
        if gated:
            res_ref, gate_ref = refs[pos], refs[pos + 1]
            pos += 2
        o_ref = refs[pos]
        pos += 1
        if gated:
            o2_ref = refs[pos]
            pos += 1
        acc_ref = refs[pos]
        k = pl.program_id(2)

        @pl.when(k == 0)
        def _():
            acc_ref[...] = jnp.zeros_like(acc_ref)

        av = a_ref[...]
        if a_silu:
            av = _silu(av.astype(F32))
        acc_ref[...] += lax.dot_general(av.astype(BF16), b_ref[...].astype(BF16), dims, preferred_element_type=F32)

        @pl.when(k == nk - 1)
        def _():
            acc = acc_ref[...]
            if bias_ref is not None:
                acc = acc + bias_ref[...]
            if gated:
                if n_seg == 1:
                    g = gate_ref[0]
                else:
                    row = pl.program_id(0) * tp + lax.broadcasted_iota(jnp.int32, (tp, 1), 0)
                    g = jnp.where(row < seg_t, gate_ref[0], gate_ref[1])
                o_ref[...] = (res_ref[...] + g * acc).astype(o_ref.dtype)
                o2_ref[...] = acc.astype(BF16)
            else:
                o_ref[...] = acc.astype(o_ref.dtype)

    out, carried = _carry_call(
        body, carry, grid=(P // tp, Q // tq, nk), out_shape=out_shape, in_specs=in_specs, out_specs=out_specs,
        scratch_shapes=[pltpu.VMEM((tp, tq), F32)], sem=("parallel", "parallel", "arbitrary"), name=name, ins=ins)
    res_out = tuple(out) if gated else out[0]
    return (res_out, carried) if carry else res_out


def _row_tile(seg_t, m):
    return 256 if (seg_t % 256 == 0 and m % 256 == 0) else 128


def _normmod_fwd(x, g, sh, sc, seg_t, name):
    M, D = x.shape
    tm = _row_tile(seg_t, M)
    n_seg = sh.shape[0]
    nt = seg_t // tm

    def seg(i):
        return ((i >= nt).astype(jnp.int32) if n_seg == 2 else 0, 0, 0)

    def body(x_ref, g_ref, sh_ref, sc_ref, o_ref):
        xv = x_ref[...]
        r = lax.rsqrt(jnp.mean(xv * xv, axis=-1, keepdims=True) + EPS)
        y = xv * r * g_ref[...]
        o_ref[...] = (y * (1.0 + sc_ref[0]) + sh_ref[0]).astype(BF16)

    return pl.pallas_call(
        body, out_shape=SDS((M, D), BF16), grid=(M // tm,),
        in_specs=[pl.BlockSpec((tm, D), lambda i: (i, 0)), pl.BlockSpec((1, D), lambda i: (0, 0)),
                  pl.BlockSpec((1, 1, D), seg), pl.BlockSpec((1, 1, D), seg)],
        out_specs=pl.BlockSpec((tm, D), lambda i: (i, 0)),
        compiler_params=_params(("parallel",)), name=name)(x, g, sh, sc)


def _normmod_bwd(x, g, sc, dh, dx_in, seg_t, name, o_prev=None, gate_prev=None):
    M, D = x.shape
    tm = _row_tile(seg_t, M)
    n_seg = sc.shape[0]
    nt = seg_t // tm
    n_in = dx_in.shape[0] // tm
    with_prev = o_prev is not None
    n_segp = gate_prev.shape[0] if with_prev else 0

    def seg(i):
        return ((i >= nt).astype(jnp.int32) if n_seg == 2 else 0, 0, 0)

    def segp(i):
        return ((i >= nt).astype(jnp.int32) if n_segp == 2 else 0, 0, 0)

    def body(*refs):
        x_ref, g_ref, sc_ref, dh_ref, dxin_ref = refs[:5]
        pos = 5
        if with_prev:
            op_ref, gp_ref = refs[5], refs[6]
            pos = 7
        dx_ref, dg_ref, dsh_ref, dsc_ref = refs[pos:pos + 4]
        if with_prev:
            dop_ref, dgp_ref = refs[pos + 4], refs[pos + 5]
        i = pl.program_id(0)
        xv = x_ref[...]
        r = lax.rsqrt(jnp.mean(xv * xv, axis=-1, keepdims=True) + EPS)
        xh = xv * r
        gv = g_ref[...]
        dhv = dh_ref[...].astype(F32)
        dy = dhv * (1.0 + sc_ref[0])
        dxh = dy * gv
        dxv = r * (dxh - xh * jnp.mean(dxh * xh, axis=-1, keepdims=True))
        if n_in * tm < M:
            dxv = dxv + jnp.where(i < n_in, dxin_ref[...], 0.0)
        else:
            dxv = dxv + dxin_ref[...]
        dx_ref[...] = dxv

        @pl.when(i == 0)
        def _():
            dg_ref[...] = jnp.zeros_like(dg_ref)

        first_of_seg = (i == 0) | (i == nt) if n_seg == 2 else (i == 0)

        @pl.when(first_of_seg)
        def _():
            dsh_ref[...] = jnp.zeros_like(dsh_ref)
            dsc_ref[...] = jnp.zeros_like(dsc_ref)

        dg_ref[...] += _colsum(dy * xh)
        dsh_ref[0] += _colsum(dhv)
        dsc_ref[0] += _colsum(dhv * xh * gv)
        if with_prev:
            first_of_segp = (i == 0) | (i == nt) if n_segp == 2 else (i == 0)

            @pl.when(first_of_segp)
            def _():
                dgp_ref[...] = jnp.zeros_like(dgp_ref)

            dop_ref[...] = (gp_ref[0] * dxv).astype(BF16)
            dgp_ref[0] += _colsum(dxv * op_ref[...].astype(F32))

    row = pl.BlockSpec((tm, D), lambda i: (i, 0))
    ins = [x, g, sc, dh, dx_in]
    in_specs = [row, pl.BlockSpec((1, D), lambda i: (0, 0)), pl.BlockSpec((1, 1, D), seg), row,
                pl.BlockSpec((tm, D), lambda i: (jnp.minimum(i, n_in - 1), 0))]
    out_shape = [SDS((M, D), F32), SDS((1, D), F32), SDS((n_seg, 1, D), F32), SDS((n_seg, 1, D), F32)]
    out_specs = [row, pl.BlockSpec((1, D), lambda i: (0, 0)), pl.BlockSpec((1, 1, D), seg), pl.BlockSpec((1, 1, D), seg)]
    if with_prev:
        ins += [o_prev, gate_prev]
        in_specs += [row, pl.BlockSpec((1, 1, D), segp)]
        out_shape += [SDS((M, D), BF16), SDS((n_segp, 1, D), F32)]
        out_specs += [row, pl.BlockSpec((1, 1, D), segp)]
    return pl.pallas_call(
        body, out_shape=out_shape, grid=(M // tm,), in_specs=in_specs, out_specs=out_specs,
        compiler_params=_params(("arbitrary",)), name=name)(*ins)


def _final(x, g, target, o_prev, gate_prev, name):
    T, D = x.shape
    tm = _tile(T, (256, 128))

    def body(x_ref, g_ref, t_ref, op_ref, gp_ref, dx_ref, loss_ref, dg_ref, dop_ref, dgp_ref):
        i = pl.program_id(0)
        xv = x_ref[...]
        r = lax.rsqrt(jnp.mean(xv * xv, axis=-1, keepdims=True) + EPS)
        xh = xv * r
        gv = g_ref[...]
        e = xh * gv - t_ref[...]
        dout = e * (1.0 / D)
        dxh = dout * gv
        dxv = r * (dxh - xh * jnp.mean(dxh * xh, axis=-1, keepdims=True))
        dx_ref[...] = dxv
        dop_ref[...] = (gp_ref[0] * dxv).astype(BF16)

        @pl.when(i == 0)
        def _():
            loss_ref[...] = jnp.zeros_like(loss_ref)
            dg_ref[...] = jnp.zeros_like(dg_ref)
            dgp_ref[...] = jnp.zeros_like(dgp_ref)

        loss_ref[...] += _colsum(e * e) * (0.5 / D)
        dg_ref[...] += _colsum(dout * xh)
        dgp_ref[0] += _colsum(dxv * op_ref[...].astype(F32))

    row = pl.BlockSpec((tm, D), lambda i: (i, 0))
    vec = pl.BlockSpec((1, D), lambda i: (0, 0))
    vec3 = pl.BlockSpec((1, 1, D), lambda i: (0, 0, 0))
    return pl.pallas_call(
        body, out_shape=[SDS((T, D), F32), SDS((1, D), F32), SDS((1, D), F32), SDS((T, D), BF16), SDS((1, 1, D), F32)],
        grid=(T // tm,), in_specs=[row, vec, row, row, vec3], out_specs=[row, vec, vec, row, vec3],
        compiler_params=_params(("arbitrary",)), name=name)(x, g, target, o_prev, gate_prev)


def _gmlp_core(z, lg, lb, ws_ref, bst):
    W = z.shape[1] // 2
    t = _gelu(z)
    u, v = t[:, :W], t[:, W:]
    mu = jnp.mean(v, axis=-1, keepdims=True)
    vc = v - mu
    rstd = lax.rsqrt(jnp.mean(vc * vc, axis=-1, keepdims=True) + EPS)
    vhat = vc * rstd
    vn = vhat * lg + lb
    vp = []
    for h in range(W // CHUNK):
        blk = vn[:, h * CHUNK:(h + 1) * CHUNK].astype(BF16)
        vp.append(jnp.dot(ws_ref[h].astype(BF16), blk, preferred_element_type=F32) + bst[:, h:h + 1])
    return u, vhat, rstd, vp


def _gmlp_fwd(z, ln_g, ln_b, w_s, b_st, W, name):
    M = z.shape[0]
    H = W // CHUNK

    def body(z_ref, lg_ref, lb_ref, ws_ref, bst_ref, o_ref):
        u, _, _, vp = _gmlp_core(z_ref[...], lg_ref[...], lb_ref[...], ws_ref, bst_ref[...])
        for h in range(H):
            o_ref[:, h * CHUNK:(h + 1) * CHUNK] = (u[:, h * CHUNK:(h + 1) * CHUNK] * vp[h]).astype(BF16)

    vec = pl.BlockSpec((1, W), lambda i: (0, 0))
    return pl.pallas_call(
        body, out_shape=SDS((M, 2 * W), BF16), grid=(M // CHUNK,),
        in_specs=[pl.BlockSpec((CHUNK, 2 * W), lambda i: (i, 0)), vec, vec,
                  pl.BlockSpec((H, CHUNK, CHUNK), lambda i: (0, 0, 0)), pl.BlockSpec((CHUNK, H), lambda i: (0, 0))],
        out_specs=pl.BlockSpec((CHUNK, W), lambda i: (i, 0)),
        compiler_params=_params(("parallel",)), name=name)(z, ln_g, ln_b, w_s, b_st)


def _gmlp_bwd(z, dy, ln_g, ln_b, w_s, b_st, W, name):
    M = z.shape[0]
    H = W // CHUNK
    ZW = z.shape[1]

    def body(z_ref, dy_ref, lg_ref, lb_ref, ws_ref, bst_ref, dz_ref, dlg_ref, dlb_ref, dws_ref, dbs_ref, dbin_ref):
        i = pl.program_id(0)

        @pl.when(i == 0)
        def _():
            for r in (dlg_ref, dlb_ref, dws_ref, dbs_ref, dbin_ref):
                r[...] = jnp.zeros_like(r)

        zv = z_ref[...]
        lg = lg_ref[...]
        u, vhat, rstd, vp = _gmlp_core(zv, lg, lb_ref[...], ws_ref, bst_ref[...])
        vn = vhat * lg + lb_ref[...]
        dya = dy_ref[...]
        du_parts, dvn_parts = [], []
        for h in range(H):
            sl = slice(h * CHUNK, (h + 1) * CHUNK)
            dya_h = dya[:, sl]
            du_parts.append(dya_h * vp[h])
            dvp = dya_h * u[:, sl]
            dbs_ref[h] += dvp
            dvp16 = dvp.astype(BF16)
            dws_ref[h] += lax.dot_general(dvp16, vn[:, sl].astype(BF16), (((1,), (1,)), ((), ())),
                                          preferred_element_type=F32)
            dvn_parts.append(lax.dot_general(ws_ref[h].astype(BF16), dvp16, (((0,), (0,)), ((), ())),
                                             preferred_element_type=F32))
        du = jnp.concatenate(du_parts, axis=1)
        dvn = jnp.concatenate(dvn_parts, axis=1)
        dlg_ref[...] += _colsum(dvn * vhat)
        dlb_ref[...] += _colsum(dvn)
        dvh = dvn * lg
        dv = rstd * (dvh - jnp.mean(dvh, axis=-1, keepdims=True) - vhat * jnp.mean(dvh * vhat, axis=-1, keepdims=True))
        dz = jnp.concatenate([du, dv], axis=1) * _dgelu(zv)
        dbin_ref[...] += _colsum(dz)
        dz_ref[...] = dz.astype(BF16)

    vec = pl.BlockSpec((1, W), lambda i: (0, 0))
    mat = pl.BlockSpec((H, CHUNK, CHUNK), lambda i: (0, 0, 0))
    return pl.pallas_call(
        body,
        out_shape=[SDS((M, ZW), BF16), SDS((1, W), F32), SDS((1, W), F32), SDS((H, CHUNK, CHUNK), F32),
                   SDS((H, CHUNK, CHUNK), F32), SDS((1, 2 * W), F32)],
        grid=(M // CHUNK,),
        in_specs=[pl.BlockSpec((CHUNK, 2 * W), lambda i: (i, 0)), pl.BlockSpec((CHUNK, W), lambda i: (i, 0)), vec, vec,
                  mat, pl.BlockSpec((CHUNK, H), lambda i: (0, 0))],
        out_specs=[pl.BlockSpec((CHUNK, 2 * W), lambda i: (i, 0)), vec, vec, mat, mat,
                   pl.BlockSpec((1, 2 * W), lambda i: (0, 0))],
        compiler_params=_params(("arbitrary",)), name=name)(z, dy, ln_g, ln_b, w_s, b_st)


def _halo_specs(tm, width, col, n_rows):
    per = tm // HALO
    last = n_rows // HALO - 1
    prev = pl.BlockSpec((HALO, width), lambda i: (jnp.maximum(i * per - 1, 0), col))
    nxt = pl.BlockSpec((HALO, width), lambda i: (jnp.minimum((i + 1) * per, last), col))
    return prev, nxt


def _edge_flags(i, tm, seg_t, m):
    r0 = i * tm
    has_prev = jnp.where((r0 == 0) | (r0 == seg_t), 0.0, 1.0)
    has_next = jnp.where((r0 + tm == seg_t) | (r0 + tm == m), 0.0, 1.0)
    return has_prev, has_next


def _glu(zz, wb):
    return zz[:, :wb] * _sigmoid(zz[:, wb:])


def _conv_taps(src_ref, w_ref, first, tm, kw, flip=False):
    rc = 32
    parts = []
    for c in range(tm // rc):
        acc = None
        for k in range(kw):
            wk = w_ref[pl.ds(kw - 1 - k if flip else k, 1), :]
            term = src_ref[pl.ds(first + c * rc + k, rc), :] * wk
            acc = term if acc is None else acc + term
        parts.append(acc)
    return jnp.concatenate(parts, axis=0)


def _conf_fwd(z, y, conv_w, conv_b, ln_g, ln_b, W, Wb, seg_t, name):
    M = z.shape[0]
    tm = _row_tile(seg_t, M)
    kw = conv_w.shape[0]
    pad = (kw - 1) // 2
    col = (2 * W) // (2 * Wb)

    def body(zc_ref, zp_ref, zn_ref, y_hbm, cw_ref, cb_ref, lg_ref, lb_ref, o_ref, hs_ref):
        del y_hbm
        hp, hn = _edge_flags(pl.program_id(0), tm, seg_t, M)
        hs_ref[pl.ds(0, HALO), :] = _glu(zp_ref[...], Wb) * hp
        hs_ref[pl.ds(HALO, tm), :] = _glu(zc_ref[...], Wb)
        hs_ref[pl.ds(HALO + tm, HALO), :] = _glu(zn_ref[...], Wb) * hn
        hc = _conv_taps(hs_ref, cw_ref, HALO - pad, tm, kw) + cb_ref[...]
        mu = jnp.mean(hc, axis=-1, keepdims=True)
        c = hc - mu
        rstd = lax.rsqrt(jnp.mean(c * c, axis=-1, keepdims=True) + EPS)
        o_ref[...] = _silu(c * rstd * lg_ref[...] + lb_ref[...]).astype(BF16)

    prev, nxt = _halo_specs(tm, 2 * Wb, col, M)
    vec = pl.BlockSpec((1, Wb), lambda i: (0, 0))
    return pl.pallas_call(
        body, out_shape=SDS(y.shape, BF16), grid=(M // tm,),
        in_specs=[pl.BlockSpec((tm, 2 * Wb), lambda i: (i, col)), prev, nxt, pl.BlockSpec(memory_space=pl.ANY),
                  pl.BlockSpec((kw, Wb), lambda i: (0, 0)), vec, vec, vec],
        out_specs=pl.BlockSpec((tm, Wb), lambda i: (i, W // Wb)),
        scratch_shapes=[pltpu.VMEM((tm + 2 * HALO, Wb), F32)],
        input_output_aliases={3: 0}, compiler_params=_params(("parallel",)), name=name)(
            z, z, z, y, conv_w, conv_b, ln_g, ln_b)


def _conf_bwd1(z, dy, conv_w, conv_b, ln_g, ln_b, W, Wb, seg_t, name):
    M = z.shape[0]
    tm = _row_tile(seg_t, M)
    kw = conv_w.shape[0]
    pad = (kw - 1) // 2
    col = (2 * W) // (2 * Wb)

    def body(zc_ref, zp_ref, zn_ref, dy_ref, cw_ref, cb_ref, lg_ref, lb_ref, dhc_ref, dlg_ref, dlb_ref, dcb_ref, hs_ref):
        i = pl.program_id(0)

        @pl.when(i == 0)
        def _():
            for r in (dlg_ref, dlb_ref, dcb_ref):
                r[...] = jnp.zeros_like(r)

        hp, hn = _edge_flags(i, tm, seg_t, M)
        hs_ref[pl.ds(0, HALO), :] = _glu(zp_ref[...], Wb) * hp
        hs_ref[pl.ds(HALO, tm), :] = _glu(zc_ref[...], Wb)
        hs_ref[pl.ds(HALO + tm, HALO), :] = _glu(zn_ref[...], Wb) * hn
        hc = _conv_taps(hs_ref, cw_ref, HALO - pad, tm, kw) + cb_ref[...]
        mu = jnp.mean(hc, axis=-1, keepdims=True)
        c = hc - mu
        rstd = lax.rsqrt(jnp.mean(c * c, axis=-1, keepdims=True) + EPS)
        hh = c * rstd
        lg = lg_ref[...]
        dhn = dy_ref[...] * _dsilu(hh * lg + lb_ref[...])
        dlg_ref[...] += _colsum(dhn * hh)
        dlb_ref[...] += _colsum(dhn)
        dhh = dhn * lg
        dhc = rstd * (dhh - jnp.mean(dhh, axis=-1, keepdims=True) - hh * jnp.mean(dhh * hh, axis=-1, keepdims=True))
        dcb_ref[...] += _colsum(dhc)
        dhc_ref[...] = dhc

    prev, nxt = _halo_specs(tm, 2 * Wb, col, M)
    vec = pl.BlockSpec((1, Wb), lambda i: (0, 0))
    return pl.pallas_call(
        body, out_shape=[SDS((M, Wb), F32), SDS((1, Wb), F32), SDS((1, Wb), F32), SDS((1, Wb), F32)], grid=(M // tm,),
        in_specs=[pl.BlockSpec((tm, 2 * Wb), lambda i: (i, col)), prev, nxt, pl.BlockSpec((tm, Wb), lambda i: (i, W // Wb)),
                  pl.BlockSpec((kw, Wb), lambda i: (0, 0)), vec, vec, vec],
        out_specs=[pl.BlockSpec((tm, Wb), lambda i: (i, 0)), vec, vec, vec],
        scratch_shapes=[pltpu.VMEM((tm + 2 * HALO, Wb), F32)],
        compiler_params=_params(("arbitrary",)), name=name)(z, z, z, dy, conv_w, conv_b, ln_g, ln_b)


def _conf_bwd2(z, dhc, dz, conv_w, W, Wb, seg_t, name):
    M = z.shape[0]
    tm = _row_tile(seg_t, M)
    kw = conv_w.shape[0]
    pad = (kw - 1) // 2
    col = (2 * W) // (2 * Wb)

    def body(zc_ref, zp_ref, zn_ref, dc_ref, dp_ref, dn_ref, dz_hbm, cw_ref, dz_ref, dcw_ref, dbin_ref, hs_ref, ds_ref):
        del dz_hbm
        i = pl.program_id(0)

        @pl.when(i == 0)
        def _():
            dcw_ref[...] = jnp.zeros_like(dcw_ref)
            dbin_ref[...] = jnp.zeros_like(dbin_ref)

        hp, hn = _edge_flags(i, tm, seg_t, M)
        zc = zc_ref[...]
        hs_ref[pl.ds(0, HALO), :] = _glu(zp_ref[...], Wb) * hp
        hs_ref[pl.ds(HALO, tm), :] = _glu(zc, Wb)
        hs_ref[pl.ds(HALO + tm, HALO), :] = _glu(zn_ref[...], Wb) * hn
        dcur = dc_ref[...]
        ds_ref[pl.ds(0, HALO), :] = dp_ref[...] * hp
        ds_ref[pl.ds(HALO, tm), :] = dcur
        ds_ref[pl.ds(HALO + tm, HALO), :] = dn_ref[...] * hn
        dh = _conv_taps(ds_ref, cw_ref, HALO - pad, tm, kw, flip=True)
        for k in range(kw):
            dcw_ref[pl.ds(k, 1), :] += _colsum(dcur * hs_ref[pl.ds(HALO - pad + k, tm), :])
        a, gt = zc[:, :Wb], zc[:, Wb:]
        s = _sigmoid(gt)
        dz = jnp.concatenate([dh * s, dh * a * s * (1.0 - s)], axis=1)
        dbin_ref[...] += _colsum(dz)
        dz_ref[...] = dz.astype(BF16)

    prev, nxt = _halo_specs(tm, 2 * Wb, col, M)
    dprev, dnxt = _halo_specs(tm, Wb, 0, M)
    return pl.pallas_call(
        body, out_shape=[SDS(dz.shape, BF16), SDS((kw, Wb), F32), SDS((1, 2 * Wb), F32)], grid=(M // tm,),
        in_specs=[pl.BlockSpec((tm, 2 * Wb), lambda i: (i, col)), prev, nxt,
                  pl.BlockSpec((tm, Wb), lambda i: (i, 0)), dprev, dnxt, pl.BlockSpec(memory_space=pl.ANY),
                  pl.BlockSpec((kw, Wb), lambda i: (0, 0))],
        out_specs=[pl.BlockSpec((tm, 2 * Wb), lambda i: (i, col)), pl.BlockSpec((kw, Wb), lambda i: (0, 0)),
                   pl.BlockSpec((1, 2 * Wb), lambda i: (0, 0))],
        scratch_shapes=[pltpu.VMEM((tm + 2 * HALO, Wb), F32), pltpu.VMEM((tm + 2 * HALO, Wb), F32)],
        input_output_aliases={6: 0}, compiler_params=_params(("arbitrary",)), name=name)(
            z, z, z, dhc, dhc, dhc, dz, conv_w)


_TF = (1408, 512, 256, 128)


def _ffn_act_fwd(z, conv_w, conv_b, seg_t, name):
    M, F2 = z.shape
    Fd = F2 // 2
    tm = _row_tile(seg_t, M)
    tf = _tile(Fd, _TF)
    nf = Fd // tf
    per, last = tm // HALO, M // HALO - 1

    def body(g_ref, gp_ref, gn_ref, u_ref, cw_ref, cb_ref, o_ref, gs_ref):
        hp, hn = _edge_flags(pl.program_id(0), tm, seg_t, M)
        gs_ref[pl.ds(0, HALO), :] = gp_ref[...].astype(F32) * hp
        gs_ref[pl.ds(HALO, tm), :] = g_ref[...].astype(F32)
        gs_ref[pl.ds(HALO + tm, HALO), :] = gn_ref[...].astype(F32) * hn
        gc = (gs_ref[pl.ds(HALO - 1, tm), :] * cw_ref[pl.ds(0, 1), :] + gs_ref[pl.ds(HALO, tm), :] * cw_ref[pl.ds(1, 1), :]
              + gs_ref[pl.ds(HALO + 1, tm), :] * cw_ref[pl.ds(2, 1), :] + cb_ref[...])
        o_ref[...] = (_silu(gc) * u_ref[...].astype(F32)).astype(BF16)

    return pl.pallas_call(
        body, out_shape=SDS((M, Fd), BF16), grid=(M // tm, nf),
        in_specs=[pl.BlockSpec((tm, tf), lambda i, j: (i, j)),
                  pl.BlockSpec((HALO, tf), lambda i, j: (jnp.maximum(i * per - 1, 0), j)),
                  pl.BlockSpec((HALO, tf), lambda i, j: (jnp.minimum((i + 1) * per, last), j)),
                  pl.BlockSpec((tm, tf), lambda i, j: (i, nf + j)),
                  pl.BlockSpec((3, tf), lambda i, j: (0, j)), pl.BlockSpec((1, tf), lambda i, j: (0, j))],
        out_specs=pl.BlockSpec((tm, tf), lambda i, j: (i, j)),
        scratch_shapes=[pltpu.VMEM((tm + 2 * HALO, tf), F32)],
        compiler_params=_params(("parallel", "parallel")), name=name)(z, z, z, z, conv_w, conv_b)


def _ffn_act_bwd(z, da, conv_w, conv_b, seg_t, name):
    M, F2 = z.shape
    Fd = F2 // 2
    tm = _row_tile(seg_t, M)
    tf = _tile(Fd, _TF)
    nf = Fd // tf
    per, last = tm // HALO, M // HALO - 1
    E = tm + 2 * HALO
    Q = 8

    def body(g_ref, gp_ref, gn_ref, u_ref, up_ref, un_ref, a_ref, ap_ref, an_ref, cw_ref, cb_ref,
             dz_ref, dcw_ref, dcb_ref, gs_ref, ds_ref, du_ref):
        i, p = pl.program_id(1), pl.program_id(2)

        @pl.when(p == 0)
        def _():
            hp, hn = _edge_flags(i, tm, seg_t, M)

            def ext(c_ref, p_ref, n_ref):
                return jnp.concatenate([p_ref[...].astype(F32) * hp, c_ref[...].astype(F32),
                                        n_ref[...].astype(F32) * hn], axis=0)

            gs_ref[...] = ext(g_ref, gp_ref, gn_ref)
            lo, n = HALO - Q, tm + 2 * Q
            w0, w1, w2 = cw_ref[pl.ds(0, 1), :], cw_ref[pl.ds(1, 1), :], cw_ref[pl.ds(2, 1), :]
            gc = (gs_ref[pl.ds(lo - 1, n), :] * w0 + gs_ref[pl.ds(lo, n), :] * w1 + gs_ref[pl.ds(lo + 1, n), :] * w2
                  + cb_ref[...])
            ue = ext(u_ref, up_ref, un_ref)[lo:lo + n]
            ae = ext(a_ref, ap_ref, an_ref)[lo:lo + n]
            dgc = ae * ue * _dsilu(gc)
            ds_ref[...] = dgc
            dg = ds_ref[pl.ds(Q + 1, tm), :] * w0 + dgc[Q:Q + tm] * w1 + ds_ref[pl.ds(Q - 1, tm), :] * w2
            dz_ref[...] = dg.astype(BF16)
            du_ref[...] = (ae[Q:Q + tm] * _silu(gc[Q:Q + tm])).astype(BF16)
            own = dgc[Q:Q + tm]

            @pl.when(i == 0)
            def _():
                dcw_ref[...] = jnp.zeros_like(dcw_ref)
                dcb_ref[...] = jnp.zeros_like(dcb_ref)

            dcb_ref[...] += _colsum(own)
            for k in range(3):
                dcw_ref[pl.ds(k, 1), :] += _colsum(own * gs_ref[pl.ds(HALO - 1 + k, tm), :])

        @pl.when(p == 1)
        def _():
            dz_ref[...] = du_ref[...]

    def cur(off):
        return pl.BlockSpec((tm, tf), lambda j, i, p: (i, off + j))

    def prv(off):
        return pl.BlockSpec((HALO, tf), lambda j, i, p: (jnp.maximum(i * per - 1, 0), off + j))

    def nxt(off):
        return pl.BlockSpec((HALO, tf), lambda j, i, p: (jnp.minimum((i + 1) * per, last), off + j))

    return pl.pallas_call(
        body, out_shape=[SDS((M, F2), BF16), SDS((3, Fd), F32), SDS((1, Fd), F32)], grid=(nf, M // tm, 2),
        in_specs=[cur(0), prv(0), nxt(0), cur(nf), prv(nf), nxt(nf), cur(0), prv(0), nxt(0),
                  pl.BlockSpec((3, tf), lambda j, i, p: (0, j)), pl.BlockSpec((1, tf), lambda j, i, p: (0, j))],
        out_specs=[pl.BlockSpec((tm, tf), lambda j, i, p: (i, p * nf + j)),
                   pl.BlockSpec((3, tf), lambda j, i, p: (0, j)), pl.BlockSpec((1, tf), lambda j, i, p: (0, j))],
        scratch_shapes=[pltpu.VMEM((E, tf), F32), pltpu.VMEM((tm + 2 * Q, tf), F32), pltpu.VMEM((tm, tf), BF16)],
        compiler_params=_params(("parallel", "arbitrary", "arbitrary")), name=name)(
            z, z, z, z, z, z, da, da, da, conv_w, conv_b)


def _swap32(x):
    lane = lax.broadcasted_iota(jnp.int32, x.shape, 1)
    return jnp.where((lane % 64) < 32, pltpu.roll(x, 96, axis=1), pltpu.roll(x, 32, axis=1))


def _rms(x, g):
    r = lax.rsqrt(jnp.mean(x * x, axis=-1, keepdims=True) + EPS)
    return x * r * g


def _rms_bwd(x, g, dy):
    r = lax.rsqrt(jnp.mean(x * x, axis=-1, keepdims=True) + EPS)
    xh = x * r
    dxh = dy * g
    return r * (dxh - xh * jnp.mean(dxh * xh, axis=-1, keepdims=True)), _colsum(dy * xh)


def _mla_prep_fwd(z, gq, gkv, cos, sin, QL, KL, name):
    M, NZ = z.shape
    tm = _tile(M, (256, 128))

    def body(z_ref, gq_ref, gkv_ref, cos_ref, sin_ref, cq_ref, ckv_ref, kpe_ref):
        zv = z_ref[...]
        cq_ref[...] = _rms(zv[:, :QL], gq_ref[...]).astype(BF16)
        ckv_ref[...] = _rms(zv[:, QL:QL + KL], gkv_ref[...]).astype(BF16)
        kp = zv[:, QL + KL:]
        r = kp * cos_ref[...] + _swap32(kp) * sin_ref[...]
        lane = lax.broadcasted_iota(jnp.int32, r.shape, 1)
        kpe_ref[0] = jnp.where(lane < ROPE, r, 0.0).astype(BF16)
        kpe_ref[1] = jnp.where(lane >= ROPE, r, 0.0).astype(BF16)

    tab = pl.BlockSpec((tm, 128), lambda i: (i, 0))
    return pl.pallas_call(
        body, out_shape=[SDS((M, QL), BF16), SDS((M, KL), BF16), SDS((2, M, 128), BF16)], grid=(M // tm,),
        in_specs=[pl.BlockSpec((tm, NZ), lambda i: (i, 0)), pl.BlockSpec((1, QL), lambda i: (0, 0)),
                  pl.BlockSpec((1, KL), lambda i: (0, 0)), tab, tab],
        out_specs=[pl.BlockSpec((tm, QL), lambda i: (i, 0)), pl.BlockSpec((tm, KL), lambda i: (i, 0)),
                   pl.BlockSpec((2, tm, 128), lambda i: (0, i, 0))],
        compiler_params=_params(("parallel",)), name=name)(z, gq, gkv, cos, sin)


def _mla_prep_bwd(z, dcq, dckv, dkpe, gq, gkv, cos, sin, QL, KL, seg_t, name):
    M, NZ = z.shape
    H = dkpe.shape[0]
    tm = _row_tile(seg_t, M)
    nt = seg_t // tm

    def body(z_ref, dcq_ref, dckv_ref, dkpe_ref, gq_ref, gkv_ref, cos_ref, sin_ref, dz_ref, dgq_ref, dgkv_ref):
        i = pl.program_id(0)

        @pl.when(i == 0)
        def _():
            dgq_ref[...] = jnp.zeros_like(dgq_ref)
            dgkv_ref[...] = jnp.zeros_like(dgkv_ref)

        zv = z_ref[...]
        dyq = jnp.where(i < nt, dcq_ref[...], 0.0)
        dxq, dgq = _rms_bwd(zv[:, :QL], gq_ref[...], dyq)
        dxkv, dgkv = _rms_bwd(zv[:, QL:QL + KL], gkv_ref[...], dckv_ref[...])
        dgq_ref[...] += dgq
        dgkv_ref[...] += dgkv
        even = dkpe_ref[0]
        odd = dkpe_ref[1]
        for h in range(2, H, 2):
            even = even + dkpe_ref[h]
            odd = odd + dkpe_ref[h + 1]
        lane = lax.broadcasted_iota(jnp.int32, even.shape, 1)
        dr = jnp.where(lane < ROPE, even, odd)
        dkp = dr * cos_ref[...] - _swap32(dr) * sin_ref[...]
        dz_ref[...] = jnp.concatenate([dxq, dxkv, dkp], axis=1).astype(BF16)

    tab = pl.BlockSpec((tm, 128), lambda i: (i, 0))
    return pl.pallas_call(
        body, out_shape=[SDS((M, NZ), BF16), SDS((1, QL), F32), SDS((1, KL), F32)], grid=(M // tm,),
        in_specs=[pl.BlockSpec((tm, NZ), lambda i: (i, 0)),
                  pl.BlockSpec((tm, QL), lambda i: (jnp.minimum(i, nt - 1), 0)),
                  pl.BlockSpec((tm, KL), lambda i: (i, 0)), pl.BlockSpec((H, tm, 128), lambda i: (0, i, 0)),
                  pl.BlockSpec((1, QL), lambda i: (0, 0)), pl.BlockSpec((1, KL), lambda i: (0, 0)), tab, tab],
        out_specs=[pl.BlockSpec((tm, NZ), lambda i: (i, 0)), pl.BlockSpec((1, QL), lambda i: (0, 0)),
                   pl.BlockSpec((1, KL), lambda i: (0, 0))],
        compiler_params=_params(("arbitrary",)), name=name)(z, dcq, dckv, dkpe, gq, gkv, cos, sin)


def _qrope_fwd(q, cos, sin, HN, name):
    T, NQ = q.shape
    tm = _tile(T, (256, 128))

    def body(q_ref, cos_ref, sin_ref, o_ref):
        o_ref[:, :HN] = q_ref[:, :HN].astype(BF16)
        for cb in range((NQ - HN) // 128):
            sl = slice(HN + cb * 128, HN + (cb + 1) * 128)
            xv = q_ref[:, sl]
            o_ref[:, sl] = (xv * cos_ref[...] + _swap32(xv) * sin_ref[...]).astype(BF16)

    tab = pl.BlockSpec((tm, 128), lambda i: (i, 0))
    return pl.pallas_call(
        body, out_shape=SDS((T, NQ), BF16), grid=(T // tm,),
        in_specs=[pl.BlockSpec((tm, NQ), lambda i: (i, 0)), tab, tab],
        out_specs=pl.BlockSpec((tm, NQ), lambda i: (i, 0)),
        compiler_params=_params(("parallel",)), name=name)(q, cos, sin)


def _qrope_bwd(dqpe, dqa, cos, sin, HN, name):
    T, HW = dqpe.shape
    HR = HW // 2
    tm = _tile(T, (256, 128))

    def body(d_ref, dqa_hbm, cos_ref, sin_ref, o_ref):
        del dqa_hbm
        for pr in range(HR // 128):
            dr = d_ref[:, 2 * pr * 128:(2 * pr + 1) * 128] + d_ref[:, (2 * pr + 1) * 128:(2 * pr + 2) * 128]
            o_ref[:, pr * 128:(pr + 1) * 128] = (dr * cos_ref[...] - _swap32(dr) * sin_ref[...]).astype(BF16)

    tab = pl.BlockSpec((tm, 128), lambda i: (i, 0))
    return pl.pallas_call(
        body, out_shape=SDS(dqa.shape, BF16), grid=(T // tm,),
        in_specs=[pl.BlockSpec((tm, HW), lambda i: (i, 0)), pl.BlockSpec(memory_space=pl.ANY), tab, tab],
        out_specs=pl.BlockSpec((tm, HR), lambda i: (i, HN // HR)),
        input_output_aliases={1: 0}, compiler_params=_params(("parallel",)), name=name)(dqpe, dqa, cos, sin)


_NT = (((1,), (1,)), ((), ()))
_TN = (((0,), (0,)), ((), ()))


def _attn_fwd(qa, kv, kpe, T, H, name, carry=None):
    M = kv.shape[0]
    tq = _tile(T, (512, 256, 128))
    scale = (NOPE + ROPE) ** -0.5

    def body(qn_ref, qp_ref, kv_ref, kpe_ref, o_ref, lse_ref, kc_ref):
        @pl.when(pl.program_id(1) == 0)
        def _():
            kc_ref[:, :NOPE] = kv_ref[:, :NOPE]
            kc_ref[:, NOPE:] = kpe_ref[0]

        qc = jnp.concatenate([qn_ref[...], qp_ref[...]], axis=1)
        s = lax.dot_general(qc, kc_ref[...], _NT, preferred_element_type=F32) * scale
        m = jnp.max(s, axis=-1, keepdims=True)
        p = jnp.exp(s - m)
        l = jnp.sum(p, axis=-1, keepdims=True)
        o = jnp.dot(p.astype(BF16), kv_ref[:, NOPE:], preferred_element_type=F32)
        o_ref[...] = (o / l).astype(BF16)
        lse_ref[...] = jnp.broadcast_to(m + jnp.log(l), lse_ref.shape)

    return _carry_call(
        body, carry, out_shape=[SDS((T, H * VDIM), BF16), SDS((T, H * 128), F32)], grid=(H, T // tq),
        in_specs=[pl.BlockSpec((tq, NOPE), lambda h, i: (i, h)), pl.BlockSpec((tq, 128), lambda h, i: (i, H + h // 2)),
                  pl.BlockSpec((M, NOPE + VDIM), lambda h, i: (0, h)), pl.BlockSpec((1, M, 128), lambda h, i: (h % 2, 0, 0))],
        out_specs=[pl.BlockSpec((tq, VDIM), lambda h, i: (i, h)), pl.BlockSpec((tq, 128), lambda h, i: (i, h))],
        scratch_shapes=[pltpu.VMEM((M, NOPE + 128), BF16)],
        sem=("parallel", "arbitrary"), name=name, ins=(qa, qa, kv, kpe))


def _attn_bwd(qa, kv, kpe, do, lse, T, H, name, carry=None):
    M = kv.shape[0]
    tq = _tile(T, (256, 128))
    nq = T // tq
    scale = (NOPE + ROPE) ** -0.5

    def body(qn_ref, qp_ref, kv_ref, kpe_ref, do_ref, lse_ref, dqa_ref, dqpe_ref, dkv_ref, dkpe_ref, kc_ref, dk_acc, dv_acc):
        i = pl.program_id(1)

        @pl.when(i == 0)
        def _():
            kc_ref[:, :NOPE] = kv_ref[:, :NOPE]
            kc_ref[:, NOPE:] = kpe_ref[0]
            dk_acc[...] = jnp.zeros_like(dk_acc)
            dv_acc[...] = jnp.zeros_like(dv_acc)

        qc = jnp.concatenate([qn_ref[...], qp_ref[...]], axis=1)
        dov = do_ref[...]
        kc = kc_ref[...]
        s = lax.dot_general(qc, kc, _NT, preferred_element_type=F32) * scale
        p = jnp.exp(s - lse_ref[:, 0:1])
        dp = lax.dot_general(dov, kv_ref[:, NOPE:], _NT, preferred_element_type=F32)
        delta = jnp.sum(p * dp, axis=-1, keepdims=True)
        ds = (p * (dp - delta) * scale).astype(BF16)
        dq = jnp.dot(ds, kc, preferred_element_type=F32)
        dqa_ref[...] = dq[:, :NOPE].astype(BF16)
        dqpe_ref[...] = dq[:, NOPE:]
        dv_acc[...] += lax.dot_general(p.astype(BF16), dov, _TN, preferred_element_type=F32)
        dk_acc[...] += lax.dot_general(ds, qc, _TN, preferred_element_type=F32)

        @pl.when(i == nq - 1)
        def _():
            dkv_ref[:, :NOPE] = dk_acc[:, :NOPE].astype(BF16)
            dkv_ref[:, NOPE:] = dv_acc[...].astype(BF16)
            dkpe_ref[0] = dk_acc[:, NOPE:]

    return _carry_call(
        body, carry,
        out_shape=[SDS((T, H * (NOPE + ROPE)), BF16), SDS((T, H * 128), F32), SDS((M, H * (NOPE + VDIM)), BF16),
                   SDS((H, M, 128), F32)],
        grid=(H, nq),
        in_specs=[pl.BlockSpec((tq, NOPE), lambda h, i: (i, h)), pl.BlockSpec((tq, 128), lambda h, i: (i, H + h // 2)),
                  pl.BlockSpec((M, NOPE + VDIM), lambda h, i: (0, h)), pl.BlockSpec((1, M, 128), lambda h, i: (h % 2, 0, 0)),
                  pl.BlockSpec((tq, VDIM), lambda h, i: (i, h)), pl.BlockSpec((tq, 128), lambda h, i: (i, h))],
        out_specs=[pl.BlockSpec((tq, NOPE), lambda h, i: (i, h)), pl.BlockSpec((tq, 128), lambda h, i: (i, h)),
                   pl.BlockSpec((M, NOPE + VDIM), lambda h, i: (0, h)), pl.BlockSpec((1, M, 128), lambda h, i: (h, 0, 0))],
        scratch_shapes=[pltpu.VMEM((M, NOPE + 128), BF16), pltpu.VMEM((M, NOPE + 128), F32), pltpu.VMEM((M, VDIM), F32)],
        sem=("parallel", "arbitrary"), name=name, ins=(qa, qa, kv, kpe, do, lse))


def _adamw(w, m, v, name, g=None, recv=None):
    R, C = w.shape
    summed = recv is not None
    n_recv = len(recv) if summed else 1
    rows_each = R // n_recv
    tr = rows_each
    for cand in (1024, 512, 256, 128, 64, 32, 16, 8):
        if rows_each % cand == 0 and cand * C <= 131072:
            tr = cand
            break
    per = rows_each // tr
    c1 = 1.0 - ADAM_B1 ** ADAM_STEP
    c2 = 1.0 - ADAM_B2 ** ADAM_STEP

    def update(gv, w_ref, m_ref, v_ref, d_ref, nm_ref, nv_ref):
        mn = ADAM_B1 * m_ref[...] + (1.0 - ADAM_B1) * gv
        vn = ADAM_B2 * v_ref[...] + (1.0 - ADAM_B2) * (gv * gv)
        nm_ref[...] = mn
        nv_ref[...] = vn
        d_ref[...] = -ADAM_LR * ((mn / c1) / (jnp.sqrt(vn / c2) + ADAM_EPS) + ADAM_WD * w_ref[...])

    def body(*refs):
        w_ref, m_ref, v_ref = refs[:3]
        g_refs = refs[3:3 + n_recv]
        outs = refs[3 + n_recv:]
        if not summed:
            update(g_refs[0][...], w_ref, m_ref, v_ref, *outs)
            return
        for r in range(n_recv):
            @pl.when(pl.program_id(0) // per == r)
            def _():
                gv = g_refs[r][0].astype(F32)
                for d in range(1, N_DEV):
                    gv = gv + g_refs[r][d].astype(F32)
                outs[0][...] = gv
                update(gv, w_ref, m_ref, v_ref, *outs[1:])

    blk = pl.BlockSpec((tr, C), lambda i: (i, 0))
    if summed:
        g_specs = [pl.BlockSpec((N_DEV, tr, C), functools.partial(lambda i, r: (0, jnp.clip(i - r * per, 0, per - 1), 0), r=r))
                   for r in range(n_recv)]
    else:
        g_specs = [blk]
    n_out = 4 if summed else 3
    return pl.pallas_call(
        body, out_shape=[SDS((R, C), F32)] * n_out, grid=(R // tr,), in_specs=[blk, blk, blk] + g_specs,
        out_specs=[blk] * n_out, compiler_params=_params(("parallel",)), name=name)(w, m, v, *(recv if summed else [g]))


WEIGHTS = ['c_ctx', 'norm1_g', 'norm2_g', 'w_ada', 'b_ada', 'ab_w_in', 'ab_b_in', 'a_ln_g', 'a_ln_b', 'a_w_s', 'a_b_s',
           'b_conv_w', 'b_conv_b', 'b_ln_g', 'b_ln_b', 'ab_w_out', 'mla_w_in', 'mla_q_norm_g', 'mla_w_uq',
           'mla_kv_norm_g', 'mla_w_ukv', 'mla_w_o', 'ffn_w_up', 'ffn_conv_w', 'ffn_conv_b', 'ffn_w_down', 'final_norm_g']


def _pack(parts):
    flat = jnp.concatenate([p.reshape(-1).astype(F32) for p in parts])
    n = flat.shape[0]
    unit = 65536 if n > 65536 else 1024
    n_pad = -(-n // unit) * unit
    return jnp.pad(flat, (0, n_pad - n)).reshape(n_pad // 128, 128)


def _unpack(flat, like):
    out, off = [], 0
    for shp in like:
        n = math.prod(shp)
        out.append(flat[..., off:off + n].reshape(flat.shape[:-1] + tuple(shp)))
        off += n
    return out


def _rope_tables(T, Tc):
    rows = T // GRID_W
    row = jnp.repeat(jnp.arange(rows, dtype=F32), GRID_W)
    col = jnp.tile(jnp.arange(GRID_W, dtype=F32), rows)
    n_freq = ROPE // 4
    inv = ROPE_THETA ** (-jnp.arange(n_freq, dtype=F32) / n_freq)
    ang = jnp.concatenate([row[:, None] * inv, col[:, None] * inv], axis=-1)
    cos, sin = jnp.cos(ang), jnp.sin(ang)
    cos = jnp.tile(cos, (1, 128 // (ROPE // 2)))
    sin = jnp.tile(jnp.concatenate([-sin, sin], axis=1), (1, 128 // ROPE))
    return (jnp.concatenate([cos, jnp.ones((Tc, 128), F32)], axis=0),
            jnp.concatenate([sin, jnp.zeros((Tc, 128), F32)], axis=0))


def _step(a):
    ax, ay, ac = lax.axis_index("x"), lax.axis_index("y"), lax.axis_index("c")
    me = 4 * ax + 2 * ay + ac
    T, D = a['x'].shape[1:]
    Tc = a['ctx'].shape[1]
    M = T + Tc
    W, Wb = a['a_ln_g'].shape[1], a['b_ln_g'].shape[1]
    assert W == Wb and T % Tc == 0
    Fd = a['ffn_conv_b'].shape[1]
    QL, KL = a['mla_q_norm_g'].shape[1] * N_DEV, a['mla_kv_norm_g'].shape[1] * N_DEV
    H = a['mla_w_ukv'].shape[2] * N_DEV // (NOPE + VDIM)
    HN, HR = H * NOPE, H * ROPE
    kw = a['b_conv_w'].shape[1]
    NA = a['w_ada'].shape[2]
    bf = lambda t: t.astype(BF16)

    small_shapes = [(D,), (kw, Wb // N_DEV), (2, 3, Fd // N_DEV), (QL // N_DEV,), (KL // N_DEV,)]
    g_small = _all_gather(_pack([a['c'][0], a['b_conv_w'][0], a['ffn_conv_w'], a['mla_q_norm_g'][0], a['mla_kv_norm_g'][0]]),
                          "ag_small")
    c_all, bcw, fcw, gq, gkv = _unpack(g_small.reshape(N_DEV, -1), small_shapes)
    bcw = jnp.transpose(bcw, (1, 0, 2)).reshape(kw, Wb)
    fcw = jnp.transpose(fcw, (1, 2, 0, 3)).reshape(2, 3, Fd)
    gq, gkv = gq.reshape(1, QL), gkv.reshape(1, KL)

    a16 = jnp.concatenate([c_all, a['c_ctx'][None], jnp.zeros((N_DEV - 1, D), F32)], axis=0)
    b_loc = lax.dynamic_slice(a['b_ada'], (0, me * NA), (2, NA))
    mods = [_mm(a16, a['w_ada'][l], mode="nn", out_dtype=F32, name=f"ada_fwd{l}", bias=b_loc[l:l + 1], a_silu=True)
            for l in range(2)]
    gm = _all_gather(jnp.concatenate(mods, axis=0), "ag_mod").reshape(N_DEV, 2, 2 * N_DEV, NA)
    gm = jnp.transpose(gm, (1, 2, 0, 3)).reshape(2, 2 * N_DEV, 6 * D)
    mod_lat = [lax.dynamic_slice(gm[l], (me, 0), (1, 6 * D)).reshape(6, 1, 1, D) for l in range(2)]
    mod_ctx = [gm[l][N_DEV].reshape(6, 1, 1, D) for l in range(2)]

    def mod(l, k, both):
        return jnp.concatenate([mod_lat[l][k], mod_ctx[l][k]], axis=0) if both else mod_lat[l][k]

    def from_cols(g):
        return jnp.transpose(g, (1, 0, 2)).reshape(g.shape[1], -1)

    def from_rows(g):
        return g.reshape(-1, g.shape[2])

    def ag(x):
        return (x, False)

    def a2a(x):
        return (x, True)

    cos, sin = _rope_tables(T, Tc)
    n1g, n2g = a['norm1_g'], a['norm2_g']
    a_bst = a['a_b_s'][0].T
    mm = functools.partial(_mm)
    up_sh, dn_sh = bf(a['ffn_w_up']), bf(a['ffn_w_down'])

    w_abin = from_cols(_all_gather(bf(a['ab_w_in'][0]), "ag_ab_w_in"))
    x0 = jnp.concatenate([a['x'][0], a['ctx'][0]], axis=0)
    h1 = _normmod_fwd(x0, n1g[0:1], mod(0, 0, True), mod(0, 1, True), T, "l0_norm1")
    z, (g_about,) = mm(h1, w_abin, mode="nn", out_dtype=F32, name="l0_ab_in", bias=a['ab_b_in'],
                       carry=[ag(bf(a['ab_w_out'][0]))])
    w_about = from_rows(g_about)
    y = _gmlp_fwd(z, a['a_ln_g'], a['a_ln_b'], a['a_w_s'][0], a_bst, W, "l0_gmlp")
    y = _conf_fwd(z, y, bcw, a['b_conv_b'], a['b_ln_g'], a['b_ln_b'], W, Wb, T, "l0_conf")
    (x1, o1), (g_up0,) = mm(y, w_about, mode="nn", out_dtype=F32, name="l0_ab_out", res=x0, gate=mod(0, 2, True),
                            seg_t=T, carry=[ag(up_sh[0])])
    w_up = [from_cols(g_up0), None]
    h2 = _normmod_fwd(x1, n2g[0:1], mod(0, 3, True), mod(0, 4, True), T, "l0_norm2")
    z2, (g_dn0,) = mm(h2, w_up[0], mode="nn", out_dtype=BF16, name="l0_up", carry=[ag(dn_sh[0])])
    w_dn = [from_rows(g_dn0), None]
    a2 = _ffn_act_fwd(z2, fcw[0], a['ffn_conv_b'][0:1], T, "l0_act")
    (x2, o2), (g_in, g_uq) = mm(a2, w_dn[0], mode="nn", out_dtype=F32, name="l0_down", res=x1, gate=mod(0, 5, True),
                                seg_t=T, carry=[ag(bf(a['mla_w_in'][0])), ag(bf(a['mla_w_uq'][0]))])
    w_in = from_rows(g_in)
    w_in = jnp.concatenate([w_in, w_in[:, QL + KL:]], axis=1)
    w_uq = from_cols(g_uq).reshape(QL, H, NOPE + ROPE)
    w_uq = jnp.concatenate([w_uq[:, :, :NOPE].reshape(QL, HN), w_uq[:, :, NOPE:].reshape(QL, HR)], axis=1)

    h3 = _normmod_fwd(x2, n1g[1:2], mod(1, 0, True), mod(1, 1, True), T, "l1_norm1")
    z3, (g_ukv,) = mm(h3, w_in, mode="nn", out_dtype=F32, name="l1_mla_in", carry=[ag(bf(a['mla_w_ukv'][0]))])
    w_ukv = from_cols(g_ukv)
    cqn, ckvn, kpe = _mla_prep_fwd(z3, gq, gkv, cos, sin, QL, KL, "l1_prep")
    q, (g_wo,) = mm(cqn, w_uq, mode="nn", out_dtype=F32, name="l1_uq", rows=T, carry=[ag(bf(a['mla_w_o'][0]))])
    w_o = from_rows(g_wo)
    kv = mm(ckvn, w_ukv, mode="nn", out_dtype=BF16, name="l1_ukv")
    qa = _qrope_fwd(q, cos, sin, HN, "l1_qrope")
    (o_att, lse), (g_up1,) = _attn_fwd(qa, kv, kpe, T, H, "l1_attn", carry=[ag(up_sh[1])])
    w_up[1] = from_cols(g_up1)
    x3, o3 = mm(o_att, w_o, mode="nn", out_dtype=F32, name="l1_wo", res=x2, gate=mod(1, 2, False), seg_t=T)
    h4 = _normmod_fwd(x3, n2g[1:2], mod(1, 3, False), mod(1, 4, False), T, "l1_norm2")
    z4, (g_dn1,) = mm(h4, w_up[1], mode="nn", out_dtype=BF16, name="l1_up", carry=[ag(dn_sh[1])])
    w_dn[1] = from_rows(g_dn1)
    a4 = _ffn_act_fwd(z4, fcw[1], a['ffn_conv_b'][1:2], T, "l1_act")
    x4, o4 = mm(a4, w_dn[1], mode="nn", out_dtype=F32, name="l1_down", res=x3, gate=mod(1, 5, False), seg_t=T)

    dx4, loss_cols, d_fng, do4, dg2_1 = _final(x4, a['final_norm_g'][None], a['loss_target'][0], o4, mod(1, 5, False),
                                               "final")
    loss = lax.psum(jnp.sum(loss_cols), ("x", "y", "c"))

    def cols(dw):
        k, n = dw.shape
        return jnp.transpose(dw.reshape(k, N_DEV, n // N_DEV), (1, 0, 2))

    def rows(dw):
        return dw.reshape(N_DEV, dw.shape[0] // N_DEV, dw.shape[1])

    da4 = mm(do4, w_dn[1], mode="nt", out_dtype=BF16, name="l1_down_dx")
    dw_dn1 = mm(a4, do4, mode="tn", out_dtype=BF16, name="l1_down_dw")
    dz4, dfcw1, dfcb1 = _ffn_act_bwd(z4, da4, fcw[1], a['ffn_conv_b'][1:2], T, "l1_act_bwd")
    dw_up1, (r_dn1,) = mm(h4, dz4, mode="tn", out_dtype=BF16, name="l1_up_dw", carry=[a2a(rows(dw_dn1))])
    dh4 = mm(dz4, w_up[1], mode="nt", out_dtype=F32, name="l1_up_dx")
    dx3, dn2g1, dsh2_1, dsc2_1, do3, dg1_1 = _normmod_bwd(x3, n2g[1:2], mod(1, 4, False), dh4, dx4, T, "l1_norm2_bwd",
                                                         o_prev=o3, gate_prev=mod(1, 2, False))
    d_oatt = mm(do3, w_o, mode="nt", out_dtype=BF16, name="l1_wo_dx")
    dw_o = mm(o_att, do3, mode="tn", out_dtype=BF16, name="l1_wo_dw")
    (dqa, dqpe, dkv, dkpe), (r_up1, r_wo) = _attn_bwd(qa, kv, kpe, d_oatt, lse, T, H, "l1_attn_bwd",
                                                      carry=[a2a(cols(dw_up1)), a2a(rows(dw_o))])
    dqa = _qrope_bwd(dqpe, dqa, cos, sin, HN, "l1_qrope_bwd")
    dcq = mm(dqa, w_uq, mode="nt", out_dtype=F32, name="l1_uq_dx")
    dw_uq = mm(cqn, dqa, mode="tn", out_dtype=BF16, name="l1_uq_dw", rows=T)
    dw_uq = jnp.concatenate([dw_uq[:, :HN].reshape(QL, H, NOPE), dw_uq[:, HN:].reshape(QL, H, ROPE)], axis=2)
    dw_uq = dw_uq.reshape(QL, H * (NOPE + ROPE))
    dckv = mm(dkv, w_ukv, mode="nt", out_dtype=F32, name="l1_ukv_dx")
    dw_ukv = mm(ckvn, dkv, mode="tn", out_dtype=BF16, name="l1_ukv_dw")
    dz3, dgq, dgkv = _mla_prep_bwd(z3, dcq, dckv, dkpe, gq, gkv, cos, sin, QL, KL, T, "l1_prep_bwd")
    dh3 = mm(dz3, w_in, mode="nt", out_dtype=F32, name="l1_mla_in_dx")
    dw_in = mm(h3, dz3, mode="tn", out_dtype=BF16, name="l1_mla_in_dw").astype(F32)
    dw_in = jnp.concatenate([dw_in[:, :QL + KL], dw_in[:, QL + KL:QL + KL + ROPE] + dw_in[:, QL + KL + ROPE:QL + KL + 2 * ROPE]],
                            axis=1).astype(BF16)
    dx2, dn1g1, dsh1_1, dsc1_1, do2, dg2_0 = _normmod_bwd(x2, n1g[1:2], mod(1, 1, True), dh3, dx3, T, "l1_norm1_bwd",
                                                         o_prev=o2, gate_prev=mod(0, 5, True))
    da2, (r_uq, r_ukv) = mm(do2, w_dn[0], mode="nt", out_dtype=BF16, name="l0_down_dx",
                            carry=[a2a(cols(dw_uq)), a2a(cols(dw_ukv))])
    dw_dn0, (r_in,) = mm(a2, do2, mode="tn", out_dtype=BF16, name="l0_down_dw", carry=[a2a(rows(dw_in))])
    dz2, dfcw0, dfcb0 = _ffn_act_bwd(z2, da2, fcw[0], a['ffn_conv_b'][0:1], T, "l0_act_bwd")
    dw_up0, (r_dn0,) = mm(h2, dz2, mode="tn", out_dtype=BF16, name="l0_up_dw", carry=[a2a(rows(dw_dn0))])
    dh2, (r_up0,) = mm(dz2, w_up[0], mode="nt", out_dtype=F32, name="l0_up_dx", carry=[a2a(cols(dw_up0))])
    dx1, dn2g0, dsh2_0, dsc2_0, do1, dg1_0 = _normmod_bwd(x1, n2g[0:1], mod(0, 4, True), dh2, dx2, T, "l0_norm2_bwd",
                                                         o_prev=o1, gate_prev=mod(0, 2, True))
    dy = mm(do1, w_about, mode="nt", out_dtype=F32, name="l0_ab_out_dx")
    dw_about = mm(y, do1, mode="tn", out_dtype=BF16, name="l0_ab_out_dw")
    dz, dlag, dlab, dws, dbs, dbin_a = _gmlp_bwd(z, dy, a['a_ln_g'], a['a_ln_b'], a['a_w_s'][0], a_bst, W, "l0_gmlp_bwd")
    dhc, dlbg, dlbb, dbcb = _conf_bwd1(z, dy, bcw, a['b_conv_b'], a['b_ln_g'], a['b_ln_b'], W, Wb, T, "l0_conf_bwd1")
    dz, dbcw, dbin_b = _conf_bwd2(z, dhc, dz, bcw, W, Wb, T, "l0_conf_bwd2")
    dh1, (r_about,) = mm(dz, w_abin, mode="nt", out_dtype=F32, name="l0_ab_in_dx", carry=[a2a(rows(dw_about))])
    dw_abin = mm(h1, dz, mode="tn", out_dtype=BF16, name="l0_ab_in_dw")
    dx0, dn1g0, dsh1_0, dsc1_0 = _normmod_bwd(x0, n1g[0:1], mod(0, 1, True), dh1, dx1, T, "l0_norm1_bwd")
    r_abin = _all_to_all(cols(dw_abin), "a2a_ab_w_in")

    zero = jnp.zeros((D,), F32)
    dmod = jnp.stack([
        jnp.stack([jnp.stack([dsh1_0[0, 0], dsc1_0[0, 0], dg1_0[0, 0], dsh2_0[0, 0], dsc2_0[0, 0], dg2_0[0, 0]]),
                   jnp.stack([dsh1_0[1, 0], dsc1_0[1, 0], dg1_0[1, 0], dsh2_0[1, 0], dsc2_0[1, 0], dg2_0[1, 0]])]),
        jnp.stack([jnp.stack([dsh1_1[0, 0], dsc1_1[0, 0], dg1_1[0, 0], dsh2_1[0, 0], dsc2_1[0, 0], dg2_1[0, 0]]),
                   jnp.stack([dsh1_1[1, 0], dsc1_1[1, 0], zero, zero, zero, zero])])])
    small = {
        'norm1_g': jnp.concatenate([dn1g0, dn1g1], axis=0), 'norm2_g': jnp.concatenate([dn2g0, dn2g1], axis=0),
        'ab_b_in': jnp.concatenate([dbin_a, dbin_b], axis=1), 'a_ln_g': dlag, 'a_ln_b': dlab, 'a_w_s': dws[None],
        'a_b_s': jnp.sum(dbs, axis=-1)[None], 'b_conv_w': dbcw, 'b_conv_b': dbcb, 'b_ln_g': dlbg, 'b_ln_b': dlbb,
        'mla_q_norm_g': dgq, 'mla_kv_norm_g': dgkv, 'ffn_conv_w': jnp.stack([dfcw0, dfcw1]),
        'ffn_conv_b': jnp.concatenate([dfcb0, dfcb1], axis=0), 'final_norm_g': d_fng[0],
    }
    names = list(small)
    g2 = _all_gather(_pack([dmod] + [small[n] for n in names]), "ag_small_grads")
    red = _sum_lead(g2, "sum_small_grads").reshape(-1)
    red = dict(zip(names, _unpack(red, [(2, 2, 6, D)] + [small[n].shape for n in names])[1:]))
    dmod_all = g2.reshape(N_DEV, -1)[:, :2 * 2 * 6 * D].reshape(N_DEV, 2, 2, 6 * D)

    a16g = jnp.concatenate([c_all, jnp.tile(a['c_ctx'][None], (N_DEV, 1))], axis=0)
    dm_loc = lax.dynamic_slice(dmod_all, (0, 0, 0, me * NA), (N_DEV, 2, 2, NA))
    g_wada, cpart = [], []
    for l in range(2):
        dm16 = jnp.concatenate([dm_loc[:, l, 0], dm_loc[:, l, 1]], axis=0)
        g_wada.append(mm(a16g, dm16, mode="tn", out_dtype=F32, name=f"ada_dw{l}", a_silu=True))
        cpart.append(mm(dm_loc[:, l, 1], a['w_ada'][l], mode="nt", out_dtype=F32, name=f"ada_dc{l}"))
    g_bada = _sum_lead(jnp.transpose(dmod_all, (0, 2, 1, 3)).reshape(2 * N_DEV, 2 * 6 * D // 128, 128), "sum_b_ada")
    g_cc = _all_gather(jnp.concatenate(cpart, axis=0), "ag_c_ctx")
    g_cc = _sum_lead(g_cc.reshape(2 * N_DEV * N_DEV, D // 128, 128), "sum_c_ctx").reshape(D)
    grads = {
        'c_ctx': g_cc * _dsilu(a['c_ctx']), 'w_ada': jnp.stack(g_wada), 'b_ada': g_bada.reshape(2, 6 * D),
        'b_conv_w': lax.dynamic_slice(red['b_conv_w'], (0, me * (Wb // N_DEV)), (kw, Wb // N_DEV))[None],
        'ffn_conv_w': lax.dynamic_slice(red['ffn_conv_w'], (0, 0, me * (Fd // N_DEV)), (2, 3, Fd // N_DEV)),
        'mla_q_norm_g': lax.dynamic_slice(red['mla_q_norm_g'], (0, me * (QL // N_DEV)), (1, QL // N_DEV)),
        'mla_kv_norm_g': lax.dynamic_slice(red['mla_kv_norm_g'], (0, me * (KL // N_DEV)), (1, KL // N_DEV)),
    }
    for n in names:
        if n not in grads:
            grads[n] = red[n].reshape(a[n].shape)

    recvs = {'ab_w_in': [r_abin], 'ab_w_out': [r_about], 'mla_w_in': [r_in], 'mla_w_uq': [r_uq], 'mla_w_ukv': [r_ukv],
             'mla_w_o': [r_wo], 'ffn_w_up': [r_up0, r_up1], 'ffn_w_down': [r_dn0, r_dn1]}
    out = {}
    for n in WEIGHTS:
        shp = a[n].shape
        w2 = a[n].reshape(-1, shp[-1])
        m2, v2 = a['m_' + n].reshape(w2.shape), a['v_' + n].reshape(w2.shape)
        if n in recvs:
            res = _adamw(w2, m2, v2, "adamw_" + n, recv=recvs[n])
        else:
            g2d = grads[n].reshape(w2.shape)
            res = (g2d,) + tuple(_adamw(w2, m2, v2, "adamw_" + n, g=g2d))
        out[n] = [r.reshape(shp) for r in res]
    return (loss, dx0[:T][None], *[out[n][0] for n in WEIGHTS], *[out[n][1] for n in WEIGHTS],
            *[out[n][2] for n in WEIGHTS], *[out[n][3] for n in WEIGHTS])


def kernel(x, c, ctx, c_ctx, norm1_g, norm2_g, w_ada, b_ada, ab_w_in, ab_b_in, a_ln_g, a_ln_b, a_w_s, a_b_s, b_conv_w, b_conv_b, b_ln_g, b_ln_b, ab_w_out, mla_w_in, mla_q_norm_g, mla_w_uq, mla_kv_norm_g, mla_w_ukv, mla_w_o, ffn_w_up, ffn_conv_w, ffn_conv_b, ffn_w_down, final_norm_g, loss_target, m_c_ctx, m_norm1_g, m_norm2_g, m_w_ada, m_b_ada, m_ab_w_in, m_ab_b_in, m_a_ln_g, m_a_ln_b, m_a_w_s, m_a_b_s, m_b_conv_w, m_b_conv_b, m_b_ln_g, m_b_ln_b, m_ab_w_out, m_mla_w_in, m_mla_q_norm_g, m_mla_w_uq, m_mla_kv_norm_g, m_mla_w_ukv, m_mla_w_o, m_ffn_w_up, m_ffn_conv_w, m_ffn_conv_b, m_ffn_w_down, m_final_norm_g, v_c_ctx, v_norm1_g, v_norm2_g, v_w_ada, v_b_ada, v_ab_w_in, v_ab_b_in, v_a_ln_g, v_a_ln_b, v_a_w_s, v_a_b_s, v_b_conv_w, v_b_conv_b, v_b_ln_g, v_b_ln_b, v_ab_w_out, v_mla_w_in, v_mla_q_norm_g, v_mla_w_uq, v_mla_kv_norm_g, v_mla_w_ukv, v_mla_w_o, v_ffn_w_up, v_ffn_conv_w, v_ffn_conv_b, v_ffn_w_down, v_final_norm_g):
    return _step(dict(locals()))
```

```python
import functools
import math

import jax
import jax.numpy as jnp
from jax import lax
from jax.experimental import pallas as pl
from jax.experimental.pallas import tpu as pltpu

F32 = jnp.float32
BF16 = jnp.bfloat16
SDS = jax.ShapeDtypeStruct

N_DEV = 8
EPS = 1e-6
CHUNK = 128
NOPE = 128
ROPE = 64
VDIM = 128
GRID_W = 64
ROPE_THETA = 10000.0
HALO = 16
ADAM_LR, ADAM_B1, ADAM_B2, ADAM_EPS, ADAM_WD, ADAM_STEP = 0.001, 0.9, 0.999, 1e-08, 0.01, 10
VMEM_LIMIT = 56 * 1024 * 1024


def _tile(n, prefs):
    for p in prefs:
        if n % p == 0:
            return p
    return n


def _params(sem, vmem=VMEM_LIMIT):
    return pltpu.CompilerParams(dimension_semantics=sem, vmem_limit_bytes=vmem)


def _sigmoid(x):
    return 1.0 / (1.0 + jnp.exp(-x))


def _silu(x):
    return x * _sigmoid(x)


def _dsilu(x):
    s = _sigmoid(x)
    return s * (1.0 + x * (1.0 - s))


_GELU_C = math.sqrt(2.0 / math.pi)


def _gelu(x):
    return 0.5 * x * (1.0 + jnp.tanh(_GELU_C * (x + 0.044715 * x * x * x)))


def _dgelu(x):
    t = jnp.tanh(_GELU_C * (x + 0.044715 * x * x * x))
    return 0.5 * (1.0 + t) + 0.5 * x * (1.0 - t * t) * _GELU_C * (1.0 + 3.0 * 0.044715 * x * x)


def _colsum(v):
    return jnp.sum(v, axis=0, keepdims=True)


_SIBLING = 1
_CHIPS = (2, 4, 6)


def _xchg(x_ref, o_ref, send_sems, recv_sems, local_sem, scatter):
    ax, ay, ac = lax.axis_index("x"), lax.axis_index("y"), lax.axis_index("c")
    me = 4 * ax + 2 * ay + ac

    def dev(k):
        return ax ^ (k >> 2), ay ^ ((k >> 1) & 1), ac ^ (k & 1)

    def idx(k):
        px, py, pc = dev(k)
        return 4 * px + 2 * py + pc

    def copy(k, src, dst, to):
        return pltpu.make_async_remote_copy(src_ref=src, dst_ref=dst, send_sem=send_sems.at[k - 1],
                                            recv_sem=recv_sems.at[k - 1], device_id=dev(to),
                                            device_id_type=pl.DeviceIdType.MESH)

    def own():
        return pltpu.make_async_copy(x_ref.at[me] if scatter else x_ref, o_ref.at[me], local_sem)

    def sends():
        if scatter:
            return [copy(k, x_ref.at[idx(k)], o_ref.at[me], k) for k in range(1, N_DEV)]
        return [copy(k, x_ref, o_ref.at[me], k) for k in (_SIBLING,) + _CHIPS]

    def forwards():
        return [] if scatter else [copy(j + 1, o_ref.at[idx(j)], o_ref.at[idx(j)], _SIBLING) for j in _CHIPS]

    def arrival(k):
        return copy(k, o_ref.at[idx(k)], o_ref.at[idx(k)], k)

    return own, sends, forwards, arrival


def _xchg_start(*refs, scatter):
    own, sends, _, _ = _xchg(*refs, scatter)
    own().start()
    for cp in sends():
        cp.start()


def _xchg_forward(*refs, scatter):
    _, _, forwards, arrival = _xchg(*refs, scatter)
    if not scatter:
        for j, fw in zip(_CHIPS, forwards()):
            arrival(j).wait_recv()
            fw.start()


def _xchg_finish(*refs, scatter):
    own, sends, forwards, arrival = _xchg(*refs, scatter)
    for k in range(1, N_DEV):
        if scatter or k not in _CHIPS:
            arrival(k).wait_recv()
    for cp in sends() + forwards():
        cp.wait_send()
    own().wait()


_XCHG_SEMS = [pltpu.SemaphoreType.DMA((N_DEV - 1,)), pltpu.SemaphoreType.DMA((N_DEV - 1,)), pltpu.SemaphoreType.DMA]


def _xchg_shape(x, scatter):
    return SDS((N_DEV,) + tuple(x.shape[1:] if scatter else x.shape), x.dtype)


def _exchange(x, *, scatter, name):
    def body(*refs):
        _xchg_start(*refs, scatter=scatter)
        _xchg_forward(*refs, scatter=scatter)
        _xchg_finish(*refs, scatter=scatter)

    return pl.pallas_call(
        body, out_shape=_xchg_shape(x, scatter),
        in_specs=[pl.BlockSpec(memory_space=pl.ANY)], out_specs=pl.BlockSpec(memory_space=pl.ANY),
        scratch_shapes=list(_XCHG_SEMS), name=name)(x)


def _carried(body, carry, n_in, n_out, n_scratch, grid):
    nc = len(carry)
    total = math.prod(grid)
    mid = (3 * total) // 4

    def wrapped(*refs):
        ins, cin = refs[:n_in], refs[n_in:n_in + nc]
        o0 = n_in + nc
        outs, cout = refs[o0:o0 + n_out], refs[o0 + n_out:o0 + n_out + nc]
        scr = refs[o0 + n_out + nc:]
        sems = scr[n_scratch:]
        step = pl.program_id(0)
        for ax in range(1, len(grid)):
            step = step * grid[ax] + pl.program_id(ax)

        def each(fn):
            for c in range(nc):
                fn(cin[c], cout[c], *sems[3 * c:3 * c + 3], scatter=carry[c][1])

        @pl.when(step == 0)
        def _():
            each(_xchg_start)

        body(*ins, *outs, *scr[:n_scratch])

        if mid < total - 1:
            @pl.when(step == mid)
            def _():
                each(_xchg_forward)

        @pl.when(step == total - 1)
        def _():
            if mid >= total - 1:
                each(_xchg_forward)
            each(_xchg_finish)

    return wrapped


def _carry_call(body, carry, *, grid, out_shape, in_specs, out_specs, scratch_shapes, sem, name, ins):
    carry = carry or []
    nc = len(carry)
    if nc:
        body = _carried(body, carry, len(in_specs), len(out_shape), len(scratch_shapes), grid)
        anyspec = pl.BlockSpec(memory_space=pl.ANY)
        in_specs = list(in_specs) + [anyspec] * nc
        out_specs = list(out_specs) + [anyspec] * nc
        out_shape = list(out_shape) + [_xchg_shape(x, sc) for x, sc in carry]
        scratch_shapes = list(scratch_shapes) + list(_XCHG_SEMS) * nc
        ins = list(ins) + [x for x, _ in carry]
        sem = ("arbitrary",) * len(grid)
    out = pl.pallas_call(body, out_shape=out_shape, grid=grid, in_specs=in_specs, out_specs=out_specs,
                         scratch_shapes=scratch_shapes, compiler_params=_params(sem), name=name)(*ins)
    n_main = len(out) - nc
    return list(out[:n_main]), list(out[n_main:])


def _all_gather(x, name):
    return _exchange(x, scatter=False, name=name)


def _all_to_all(x, name):
    return _exchange(x, scatter=True, name=name)


def _sum_lead(x, name):
    n, R, C = x.shape
    tr = _tile(R, (512, 256, 128, 64, 32, 16, 8))

    def body(x_ref, o_ref):
        acc = x_ref[0]
        for d in range(1, n):
            acc = acc + x_ref[d]
        o_ref[...] = acc

    return pl.pallas_call(
        body, out_shape=SDS((R, C), F32), grid=(R // tr,),
        in_specs=[pl.BlockSpec((n, tr, C), lambda i: (0, i, 0))], out_specs=pl.BlockSpec((tr, C), lambda i: (i, 0)),
        compiler_params=_params(("parallel",)), name=name)(x)


_TP = (1408, 1088, 1024, 768, 512, 256, 128)
_TQ = (1408, 1024, 768, 512, 256, 128)
_TR = (1408, 1024, 512, 256, 128)


def _mm(a, b, *, mode, out_dtype, name, rows=None, bias=None, res=None, gate=None, seg_t=None, a_silu=False, carry=None):
    if mode == "nn":
        P, R, Q = rows or a.shape[0], a.shape[1], b.shape[1]
    elif mode == "nt":
        P, R, Q = rows or a.shape[0], a.shape[1], b.shape[0]
    else:
        R, P, Q = rows or a.shape[0], a.shape[1], b.shape[1]
    tp, tq, tr = _tile(P, _TP), _tile(Q, _TQ), _tile(R, _TR if mode != "tn" else (512, 256, 128))
    nk = R // tr
    if mode == "nn":
        a_spec = pl.BlockSpec((tp, tr), lambda i, j, k: (i, k))
        b_spec = pl.BlockSpec((tr, tq), lambda i, j, k: (k, j))
        dims = (((1,), (0,)), ((), ()))
    elif mode == "nt":
        a_spec = pl.BlockSpec((tp, tr), lambda i, j, k: (i, k))
        b_spec = pl.BlockSpec((tq, tr), lambda i, j, k: (j, k))
        dims = (((1,), (1,)), ((), ()))
    else:
        a_spec = pl.BlockSpec((tr, tp), lambda i, j, k: (k, i))
        b_spec = pl.BlockSpec((tr, tq), lambda i, j, k: (k, j))
        dims = (((0,), (0,)), ((), ()))
    ins, in_specs = [a, b], [a_spec, b_spec]
    if bias is not None:
        ins.append(bias)
        in_specs.append(pl.BlockSpec((1, tq), lambda i, j, k: (0, j)))
    gated = res is not None
    if gated:
        n_seg = gate.shape[0]
        ins += [res, gate]
        in_specs += [pl.BlockSpec((tp, tq), lambda i, j, k: (i, j)),
                     pl.BlockSpec((n_seg, 1, tq), lambda i, j, k: (0, 0, j))]
    out_shape = [SDS((P, Q), out_dtype)]
    out_specs = [pl.BlockSpec((tp, tq), lambda i, j, k: (i, j))]
    if gated:
        out_shape.append(SDS((P, Q), BF16))
        out_specs.append(pl.BlockSpec((tp, tq), lambda i, j, k: (i, j)))

    def body(*refs):
        a_ref, b_ref = refs[0], refs[1]
        pos = 2
        bias_ref = res_ref = gate_ref = o2_ref = None
        if bias is not None:
            bias_ref = refs[pos]
            pos += 1
        if gated:
            res_ref, gate_ref = refs[pos], refs[pos + 1]
            pos += 2
        o_ref = refs[pos]
        pos += 1
        if gated:
            o2_ref = refs[pos]
            pos += 1
        acc_ref = refs[pos]
        k = pl.program_id(2)

        @pl.when(k == 0)
        def _():
            acc_ref[...] = jnp.zeros_like(acc_ref)

        av = a_ref[...]
        if a_silu:
            av = _silu(av.astype(F32))
        acc_ref[...] += lax.dot_general(av.astype(BF16), b_ref[...].astype(BF16), dims, preferred_element_type=F32)

        @pl.when(k == nk - 1)
        def _():
            acc = acc_ref[...]
            if bias_ref is not None:
                acc = acc + bias_ref[...]
            if gated:
                if n_seg == 1:
                    g = gate_ref[0]
                else:
                    row = pl.program_id(0) * tp + lax.broadcasted_iota(jnp.int32, (tp, 1), 0)
                    g = jnp.where(row < seg_t, gate_ref[0], gate_ref[1])
                o_ref[...] = (res_ref[...] + g * acc).astype(o_ref.dtype)
                o2_ref[...] = acc.astype(BF16)
            else:
                o_ref[...] = acc.astype(o_ref.dtype)

    out, carried = _carry_call(
        body, carry, grid=(P // tp, Q // tq, nk), out_shape=out_shape, in_specs=in_specs, out_specs=out_specs,
        scratch_shapes=[pltpu.VMEM((tp, tq), F32)], sem=("parallel", "parallel", "arbitrary"), name=name, ins=ins)
    res_out = tuple(out) if gated else out[0]
    return (res_out, carried) if carry else res_out


def _row_tile(seg_t, m):
    return 256 if (seg_t % 256 == 0 and m % 256 == 0) else 128


def _normmod_fwd(x, g, sh, sc, seg_t, name):
    M, D = x.shape
    tm = _row_tile(seg_t, M)
    n_seg = sh.shape[0]
    nt = seg_t // tm

    def seg(i):
        return ((i >= nt).astype(jnp.int32) if n_seg == 2 else 0, 0, 0)

    def body(x_ref, g_ref, sh_ref, sc_ref, o_ref):
        xv = x_ref[...]
        r = lax.rsqrt(jnp.mean(xv * xv, axis=-1, keepdims=True) + EPS)
        y = xv * r * g_ref[...]
        o_ref[...] = (y * (1.0 + sc_ref[0]) + sh_ref[0]).astype(BF16)

    return pl.pallas_call(
        body, out_shape=SDS((M, D), BF16), grid=(M // tm,),
        in_specs=[pl.BlockSpec((tm, D), lambda i: (i, 0)), pl.BlockSpec((1, D), lambda i: (0, 0)),
                  pl.BlockSpec((1, 1, D), seg), pl.BlockSpec((1, 1, D), seg)],
        out_specs=pl.BlockSpec((tm, D), lambda i: (i, 0)),
        compiler_params=_params(("parallel",)), name=name)(x, g, sh, sc)


def _normmod_bwd(x, g, sc, dh, dx_in, seg_t, name, o_prev=None, gate_prev=None):
    M, D = x.shape
    tm = _row_tile(seg_t, M)
    n_seg = sc.shape[0]
    nt = seg_t // tm
    n_in = dx_in.shape[0] // tm
    with_prev = o_prev is not None
    n_segp = gate_prev.shape[0] if with_prev else 0

    def seg(i):
        return ((i >= nt).astype(jnp.int32) if n_seg == 2 else 0, 0, 0)

    def segp(i):
        return ((i >= nt).astype(jnp.int32) if n_segp == 2 else 0, 0, 0)

    def body(*refs):
        x_ref, g_ref, sc_ref, dh_ref, dxin_ref = refs[:5]
        pos = 5
        if with_prev:
            op_ref, gp_ref = refs[5], refs[6]
            pos = 7
        dx_ref, dg_ref, dsh_ref, dsc_ref = refs[pos:pos + 4]
        if with_prev:
            dop_ref, dgp_ref = refs[pos + 4], refs[pos + 5]
        i = pl.program_id(0)
        xv = x_ref[...]
        r = lax.rsqrt(jnp.mean(xv * xv, axis=-1, keepdims=True) + EPS)
        xh = xv * r
        gv = g_ref[...]
        dhv = dh_ref[...].astype(F32)
        dy = dhv * (1.0 + sc_ref[0])
        dxh = dy * gv
        dxv = r * (dxh - xh * jnp.mean(dxh * xh, axis=-1, keepdims=True))
        if n_in * tm < M:
            dxv = dxv + jnp.where(i < n_in, dxin_ref[...], 0.0)
        else:
            dxv = dxv + dxin_ref[...]
        dx_ref[...] = dxv

        @pl.when(i == 0)
        def _():
            dg_ref[...] = jnp.zeros_like(dg_ref)

        first_of_seg = (i == 0) | (i == nt) if n_seg == 2 else (i == 0)

        @pl.when(first_of_seg)
        def _():
            dsh_ref[...] = jnp.zeros_like(dsh_ref)
            dsc_ref[...] = jnp.zeros_like(dsc_ref)

        dg_ref[...] += _colsum(dy * xh)
        dsh_ref[0] += _colsum(dhv)
        dsc_ref[0] += _colsum(dhv * xh * gv)
        if with_prev:
            first_of_segp = (i == 0) | (i == nt) if n_segp == 2 else (i == 0)

            @pl.when(first_of_segp)
            def _():
                dgp_ref[...] = jnp.zeros_like(dgp_ref)

            dop_ref[...] = (gp_ref[0] * dxv).astype(BF16)
            dgp_ref[0] += _colsum(dxv * op_ref[...].astype(F32))

    row = pl.BlockSpec((tm, D), lambda i: (i, 0))
    ins = [x, g, sc, dh, dx_in]
    in_specs = [row, pl.BlockSpec((1, D), lambda i: (0, 0)), pl.BlockSpec((1, 1, D), seg), row,
                pl.BlockSpec((tm, D), lambda i: (jnp.minimum(i, n_in - 1), 0))]
    out_shape = [SDS((M, D), F32), SDS((1, D), F32), SDS((n_seg, 1, D), F32), SDS((n_seg, 1, D), F32)]
    out_specs = [row, pl.BlockSpec((1, D), lambda i: (0, 0)), pl.BlockSpec((1, 1, D), seg), pl.BlockSpec((1, 1, D), seg)]
    if with_prev:
        ins += [o_prev, gate_prev]
        in_specs += [row, pl.BlockSpec((1, 1, D), segp)]
        out_shape += [SDS((M, D), BF16), SDS((n_segp, 1, D), F32)]
        out_specs += [row, pl.BlockSpec((1, 1, D), segp)]
    return pl.pallas_call(
        body, out_shape=out_shape, grid=(M // tm,), in_specs=in_specs, out_specs=out_specs,
        compiler_params=_params(("arbitrary",)), name=name)(*ins)


def _final(x, g, target, o_prev, gate_prev, name):
    T, D = x.shape
    tm = _tile(T, (256, 128))

    def body(x_ref, g_ref, t_ref, op_ref, gp_ref, dx_ref, loss_ref, dg_ref, dop_ref, dgp_ref):
        i = pl.program_id(0)
        xv = x_ref[...]
        r = lax.rsqrt(jnp.mean(xv * xv, axis=-1, keepdims=True) + EPS)
        xh = xv * r
        gv = g_ref[...]
        e = xh * gv - t_ref[...]
        dout = e * (1.0 / D)
        dxh = dout * gv
        dxv = r * (dxh - xh * jnp.mean(dxh * xh, axis=-1, keepdims=True))
        dx_ref[...] = dxv
        dop_ref[...] = (gp_ref[0] * dxv).astype(BF16)

        @pl.when(i == 0)
        def _():
            loss_ref[...] = jnp.zeros_like(loss_ref)
            dg_ref[...] = jnp.zeros_like(dg_ref)
            dgp_ref[...] = jnp.zeros_like(dgp_ref)

        loss_ref[...] += _colsum(e * e) * (0.5 / D)
        dg_ref[...] += _colsum(dout * xh)
        dgp_ref[0] += _colsum(dxv * op_ref[...].astype(F32))

    row = pl.BlockSpec((tm, D), lambda i: (i, 0))
    vec = pl.BlockSpec((1, D), lambda i: (0, 0))
    vec3 = pl.BlockSpec((1, 1, D), lambda i: (0, 0, 0))
    return pl.pallas_call(
        body, out_shape=[SDS((T, D), F32), SDS((1, D), F32), SDS((1, D), F32), SDS((T, D), BF16), SDS((1, 1, D), F32)],
        grid=(T // tm,), in_specs=[row, vec, row, row, vec3], out_specs=[row, vec, vec, row, vec3],
        compiler_params=_params(("arbitrary",)), name=name)(x, g, target, o_prev, gate_prev)


def _gmlp_core(z, lg, lb, ws_ref, bst):
    W = z.shape[1] // 2
    t = _gelu(z)
    u, v = t[:, :W], t[:, W:]
    mu = jnp.mean(v, axis=-1, keepdims=True)
    vc = v - mu
    rstd = lax.rsqrt(jnp.mean(vc * vc, axis=-1, keepdims=True) + EPS)
    vhat = vc * rstd
    vn = vhat * lg + lb
    vp = []
    for h in range(W // CHUNK):
        blk = vn[:, h * CHUNK:(h + 1) * CHUNK].astype(BF16)
        vp.append(jnp.dot(ws_ref[h].astype(BF16), blk, preferred_element_type=F32) + bst[:, h:h + 1])
    return u, vhat, rstd, vp


def _gmlp_fwd(z, ln_g, ln_b, w_s, b_st, W, name):
    M = z.shape[0]
    H = W // CHUNK

    def body(z_ref, lg_ref, lb_ref, ws_ref, bst_ref, o_ref):
        u, _, _, vp = _gmlp_core(z_ref[...], lg_ref[...], lb_ref[...], ws_ref, bst_ref[...])
        for h in range(H):
            o_ref[:, h * CHUNK:(h + 1) * CHUNK] = (u[:, h * CHUNK:(h + 1) * CHUNK] * vp[h]).astype(BF16)

    vec = pl.BlockSpec((1, W), lambda i: (0, 0))
    return pl.pallas_call(
        body, out_shape=SDS((M, 2 * W), BF16), grid=(M // CHUNK,),
        in_specs=[pl.BlockSpec((CHUNK, 2 * W), lambda i: (i, 0)), vec, vec,
                  pl.BlockSpec((H, CHUNK, CHUNK), lambda i: (0, 0, 0)), pl.BlockSpec((CHUNK, H), lambda i: (0, 0))],
        out_specs=pl.BlockSpec((CHUNK, W), lambda i: (i, 0)),
        compiler_params=_params(("parallel",)), name=name)(z, ln_g, ln_b, w_s, b_st)


def _gmlp_bwd(z, dy, ln_g, ln_b, w_s, b_st, W, name):
    M = z.shape[0]
    H = W // CHUNK
    ZW = z.shape[1]

    def body(z_ref, dy_ref, lg_ref, lb_ref, ws_ref, bst_ref, dz_ref, dlg_ref, dlb_ref, dws_ref, dbs_ref, dbin_ref):
        i = pl.program_id(0)

        @pl.when(i == 0)
        def _():
            for r in (dlg_ref, dlb_ref, dws_ref, dbs_ref, dbin_ref):
                r[...] = jnp.zeros_like(r)

        zv = z_ref[...]
        lg = lg_ref[...]
        u, vhat, rstd, vp = _gmlp_core(zv, lg, lb_ref[...], ws_ref, bst_ref[...])
        vn = vhat * lg + lb_ref[...]
        dya = dy_ref[...]
        du_parts, dvn_parts = [], []
        for h in range(H):
            sl = slice(h * CHUNK, (h + 1) * CHUNK)
            dya_h = dya[:, sl]
            du_parts.append(dya_h * vp[h])
            dvp = dya_h * u[:, sl]
            dbs_ref[h] += dvp
            dvp16 = dvp.astype(BF16)
            dws_ref[h] += lax.dot_general(dvp16, vn[:, sl].astype(BF16), (((1,), (1,)), ((), ())),
                                          preferred_element_type=F32)
            dvn_parts.append(lax.dot_general(ws_ref[h].astype(BF16), dvp16, (((0,), (0,)), ((), ())),
                                             preferred_element_type=F32))
        du = jnp.concatenate(du_parts, axis=1)
        dvn = jnp.concatenate(dvn_parts, axis=1)
        dlg_ref[...] += _colsum(dvn * vhat)
        dlb_ref[...] += _colsum(dvn)
        dvh = dvn * lg
        dv = rstd * (dvh - jnp.mean(dvh, axis=-1, keepdims=True) - vhat * jnp.mean(dvh * vhat, axis=-1, keepdims=True))
        dz = jnp.concatenate([du, dv], axis=1) * _dgelu(zv)
        dbin_ref[...] += _colsum(dz)
        dz_ref[...] = dz.astype(BF16)

    vec = pl.BlockSpec((1, W), lambda i: (0, 0))
    mat = pl.BlockSpec((H, CHUNK, CHUNK), lambda i: (0, 0, 0))
    return pl.pallas_call(
        body,
        out_shape=[SDS((M, ZW), BF16), SDS((1, W), F32), SDS((1, W), F32), SDS((H, CHUNK, CHUNK), F32),
                   SDS((H, CHUNK, CHUNK), F32), SDS((1, 2 * W), F32)],
        grid=(M // CHUNK,),
        in_specs=[pl.BlockSpec((CHUNK, 2 * W), lambda i: (i, 0)), pl.BlockSpec((CHUNK, W), lambda i: (i, 0)), vec, vec,
                  mat, pl.BlockSpec((CHUNK, H), lambda i: (0, 0))],
        out_specs=[pl.BlockSpec((CHUNK, 2 * W), lambda i: (i, 0)), vec, vec, mat, mat,
                   pl.BlockSpec((1, 2 * W), lambda i: (0, 0))],
        compiler_params=_params(("arbitrary",)), name=name)(z, dy, ln_g, ln_b, w_s, b_st)


def _halo_specs(tm, width, col, n_rows):
    per = tm // HALO
    last = n_rows // HALO - 1
    prev = pl.BlockSpec((HALO, width), lambda i: (jnp.maximum(i * per - 1, 0), col))
    nxt = pl.BlockSpec((HALO, width), lambda i: (jnp.minimum((i + 1) * per, last), col))
    return prev, nxt


def _edge_flags(i, tm, seg_t, m):
    r0 = i * tm
    has_prev = jnp.where((r0 == 0) | (r0 == seg_t), 0.0, 1.0)
    has_next = jnp.where((r0 + tm == seg_t) | (r0 + tm == m), 0.0, 1.0)
    return has_prev, has_next


def _glu(zz, wb):
    return zz[:, :wb] * _sigmoid(zz[:, wb:])


def _conv_taps(src_ref, w_ref, first, tm, kw, flip=False):
    rc = 32
    parts = []
    for c in range(tm // rc):
        acc = None
        for k in range(kw):
            wk = w_ref[pl.ds(kw - 1 - k if flip else k, 1), :]
            term = src_ref[pl.ds(first + c * rc + k, rc), :] * wk
            acc = term if acc is None else acc + term
        parts.append(acc)
    return jnp.concatenate(parts, axis=0)


def _conf_fwd(z, y, conv_w, conv_b, ln_g, ln_b, W, Wb, seg_t, name):
    M = z.shape[0]
    tm = _row_tile(seg_t, M)
    kw = conv_w.shape[0]
    pad = (kw - 1) // 2
    col = (2 * W) // (2 * Wb)

    def body(zc_ref, zp_ref, zn_ref, y_hbm, cw_ref, cb_ref, lg_ref, lb_ref, o_ref, hs_ref):
        del y_hbm
        hp, hn = _edge_flags(pl.program_id(0), tm, seg_t, M)
        hs_ref[pl.ds(0, HALO), :] = _glu(zp_ref[...], Wb) * hp
        hs_ref[pl.ds(HALO, tm), :] = _glu(zc_ref[...], Wb)
        hs_ref[pl.ds(HALO + tm, HALO), :] = _glu(zn_ref[...], Wb) * hn
        hc = _conv_taps(hs_ref, cw_ref, HALO - pad, tm, kw) + cb_ref[...]
        mu = jnp.mean(hc, axis=-1, keepdims=True)
        c = hc - mu
        rstd = lax.rsqrt(jnp.mean(c * c, axis=-1, keepdims=True) + EPS)
        o_ref[...] = _silu(c * rstd * lg_ref[...] + lb_ref[...]).astype(BF16)

    prev, nxt = _halo_specs(tm, 2 * Wb, col, M)
    vec = pl.BlockSpec((1, Wb), lambda i: (0, 0))
    return pl.pallas_call(
        body, out_shape=SDS(y.shape, BF16), grid=(M // tm,),
        in_specs=[pl.BlockSpec((tm, 2 * Wb), lambda i: (i, col)), prev, nxt, pl.BlockSpec(memory_space=pl.ANY),
                  pl.BlockSpec((kw, Wb), lambda i: (0, 0)), vec, vec, vec],
        out_specs=pl.BlockSpec((tm, Wb), lambda i: (i, W // Wb)),
        scratch_shapes=[pltpu.VMEM((tm + 2 * HALO, Wb), F32)],
        input_output_aliases={3: 0}, compiler_params=_params(("parallel",)), name=name)(
            z, z, z, y, conv_w, conv_b, ln_g, ln_b)


def _conf_bwd1(z, dy, conv_w, conv_b, ln_g, ln_b, W, Wb, seg_t, name):
    M = z.shape[0]
    tm = _row_tile(seg_t, M)
    kw = conv_w.shape[0]
    pad = (kw - 1) // 2
    col = (2 * W) // (2 * Wb)

    def body(zc_ref, zp_ref, zn_ref, dy_ref, cw_ref, cb_ref, lg_ref, lb_ref, dhc_ref, dlg_ref, dlb_ref, dcb_ref, hs_ref):
        i = pl.program_id(0)

        @pl.when(i == 0)
        def _():
            for r in (dlg_ref, dlb_ref, dcb_ref):
                r[...] = jnp.zeros_like(r)

        hp, hn = _edge_flags(i, tm, seg_t, M)
        hs_ref[pl.ds(0, HALO), :] = _glu(zp_ref[...], Wb) * hp
        hs_ref[pl.ds(HALO, tm), :] = _glu(zc_ref[...], Wb)
        hs_ref[pl.ds(HALO + tm, HALO), :] = _glu(zn_ref[...], Wb) * hn
        hc = _conv_taps(hs_ref, cw_ref, HALO - pad, tm, kw) + cb_ref[...]
        mu = jnp.mean(hc, axis=-1, keepdims=True)
        c = hc - mu
        rstd = lax.rsqrt(jnp.mean(c * c, axis=-1, keepdims=True) + EPS)
        hh = c * rstd
        lg = lg_ref[...]
        dhn = dy_ref[...] * _dsilu(hh * lg + lb_ref[...])
        dlg_ref[...] += _colsum(dhn * hh)
        dlb_ref[...] += _colsum(dhn)
        dhh = dhn * lg
        dhc = rstd * (dhh - jnp.mean(dhh, axis=-1, keepdims=True) - hh * jnp.mean(dhh * hh, axis=-1, keepdims=True))
        dcb_ref[...] += _colsum(dhc)
        dhc_ref[...] = dhc

    prev, nxt = _halo_specs(tm, 2 * Wb, col, M)
    vec = pl.BlockSpec((1, Wb), lambda i: (0, 0))
    return pl.pallas_call(
        body, out_shape=[SDS((M, Wb), F32), SDS((1, Wb), F32), SDS((1, Wb), F32), SDS((1, Wb), F32)], grid=(M // tm,),
        in_specs=[pl.BlockSpec((tm, 2 * Wb), lambda i: (i, col)), prev, nxt, pl.BlockSpec((tm, Wb), lambda i: (i, W // Wb)),
                  pl.BlockSpec((kw, Wb), lambda i: (0, 0)), vec, vec, vec],
        out_specs=[pl.BlockSpec((tm, Wb), lambda i: (i, 0)), vec, vec, vec],
        scratch_shapes=[pltpu.VMEM((tm + 2 * HALO, Wb), F32)],
        compiler_params=_params(("arbitrary",)), name=name)(z, z, z, dy, conv_w, conv_b, ln_g, ln_b)


def _conf_bwd2(z, dhc, dz, conv_w, W, Wb, seg_t, name):
    M = z.shape[0]
    tm = _row_tile(seg_t, M)
    kw = conv_w.shape[0]
    pad = (kw - 1) // 2
    col = (2 * W) // (2 * Wb)

    def body(zc_ref, zp_ref, zn_ref, dc_ref, dp_ref, dn_ref, dz_hbm, cw_ref, dz_ref, dcw_ref, dbin_ref, hs_ref, ds_ref):
        del dz_hbm
        i = pl.program_id(0)

        @pl.when(i == 0)
        def _():
            dcw_ref[...] = jnp.zeros_like(dcw_ref)
            dbin_ref[...] = jnp.zeros_like(dbin_ref)

        hp, hn = _edge_flags(i, tm, seg_t, M)
        zc = zc_ref[...]
        hs_ref[pl.ds(0, HALO), :] = _glu(zp_ref[...], Wb) * hp
        hs_ref[pl.ds(HALO, tm), :] = _glu(zc, Wb)
        hs_ref[pl.ds(HALO + tm, HALO), :] = _glu(zn_ref[...], Wb) * hn
        dcur = dc_ref[...]
        ds_ref[pl.ds(0, HALO), :] = dp_ref[...] * hp
        ds_ref[pl.ds(HALO, tm), :] = dcur
        ds_ref[pl.ds(HALO + tm, HALO), :] = dn_ref[...] * hn
        dh = _conv_taps(ds_ref, cw_ref, HALO - pad, tm, kw, flip=True)
        for k in range(kw):
            dcw_ref[pl.ds(k, 1), :] += _colsum(dcur * hs_ref[pl.ds(HALO - pad + k, tm), :])
        a, gt = zc[:, :Wb], zc[:, Wb:]
        s = _sigmoid(gt)
        dz = jnp.concatenate([dh * s, dh * a * s * (1.0 - s)], axis=1)
        dbin_ref[...] += _colsum(dz)
        dz_ref[...] = dz.astype(BF16)

    prev, nxt = _halo_specs(tm, 2 * Wb, col, M)
    dprev, dnxt = _halo_specs(tm, Wb, 0, M)
    return pl.pallas_call(
        body, out_shape=[SDS(dz.shape, BF16), SDS((kw, Wb), F32), SDS((1, 2 * Wb), F32)], grid=(M // tm,),
        in_specs=[pl.BlockSpec((tm, 2 * Wb), lambda i: (i, col)), prev, nxt,
                  pl.BlockSpec((tm, Wb), lambda i: (i, 0)), dprev, dnxt, pl.BlockSpec(memory_space=pl.ANY),
                  pl.BlockSpec((kw, Wb), lambda i: (0, 0))],
        out_specs=[pl.BlockSpec((tm, 2 * Wb), lambda i: (i, col)), pl.BlockSpec((kw, Wb), lambda i: (0, 0)),
                   pl.BlockSpec((1, 2 * Wb), lambda i: (0, 0))],
        scratch_shapes=[pltpu.VMEM((tm + 2 * HALO, Wb), F32), pltpu.VMEM((tm + 2 * HALO, Wb), F32)],
        input_output_aliases={6: 0}, compiler_params=_params(("arbitrary",)), name=name)(
            z, z, z, dhc, dhc, dhc, dz, conv_w)


_TF = (1408, 512, 256, 128)
_RC = 16
_CG = 256


def _col_groups(width):
    return [(c0, min(_CG, width - c0)) for c0 in range(0, width, _CG)]


def _ffn_act_fwd(z, conv_w, conv_b, seg_t, name):
    M, F2 = z.shape
    Fd = F2 // 2
    tm = _row_tile(seg_t, M)
    tf = _tile(Fd, _TF)
    nf = Fd // tf
    per, last = tm // HALO, M // HALO - 1

    def body(g_ref, gp_ref, gn_ref, u_ref, cw_ref, cb_ref, o_ref, gs_ref):
        hp, hn = _edge_flags(pl.program_id(0), tm, seg_t, M)
        gs_ref[pl.ds(0, HALO), :] = gp_ref[...].astype(F32) * hp
        gs_ref[pl.ds(HALO, tm), :] = g_ref[...].astype(F32)
        gs_ref[pl.ds(HALO + tm, HALO), :] = gn_ref[...].astype(F32) * hn
        for c0, cw in _col_groups(tf):
            cs = pl.ds(c0, cw)
            w0, w1, w2, cb = cw_ref[pl.ds(0, 1), cs], cw_ref[pl.ds(1, 1), cs], cw_ref[pl.ds(2, 1), cs], cb_ref[:, cs]
            for r0 in range(0, tm, _RC):
                gc = (gs_ref[pl.ds(HALO - 1 + r0, _RC), cs] * w0 + gs_ref[pl.ds(HALO + r0, _RC), cs] * w1
                      + gs_ref[pl.ds(HALO + 1 + r0, _RC), cs] * w2 + cb)
                o_ref[pl.ds(r0, _RC), cs] = (_silu(gc) * u_ref[pl.ds(r0, _RC), cs].astype(F32)).astype(BF16)

    return pl.pallas_call(
        body, out_shape=SDS((M, Fd), BF16), grid=(M // tm, nf),
        in_specs=[pl.BlockSpec((tm, tf), lambda i, j: (i, j)),
                  pl.BlockSpec((HALO, tf), lambda i, j: (jnp.maximum(i * per - 1, 0), j)),
                  pl.BlockSpec((HALO, tf), lambda i, j: (jnp.minimum((i + 1) * per, last), j)),
                  pl.BlockSpec((tm, tf), lambda i, j: (i, nf + j)),
                  pl.BlockSpec((3, tf), lambda i, j: (0, j)), pl.BlockSpec((1, tf), lambda i, j: (0, j))],
        out_specs=pl.BlockSpec((tm, tf), lambda i, j: (i, j)),
        scratch_shapes=[pltpu.VMEM((tm + 2 * HALO, tf), F32)],
        compiler_params=_params(("parallel", "parallel")), name=name)(z, z, z, z, conv_w, conv_b)


def _ffn_act_bwd(z, da, conv_w, conv_b, seg_t, name):
    M, F2 = z.shape
    Fd = F2 // 2
    tm = _row_tile(seg_t, M)
    tf = _tile(Fd, _TF)
    nf = Fd // tf
    per, last = tm // HALO, M // HALO - 1
    PAD = 8
    n_piece = tm // _RC

    def body(g_ref, gp_ref, gn_ref, u_ref, up_ref, un_ref, a_ref, ap_ref, an_ref, cw_ref, cb_ref,
             dz_ref, dcw_ref, dcb_ref, gs_ref, ds_ref, du_ref):
        i, p = pl.program_id(1), pl.program_id(2)

        @pl.when(p == 0)
        def _():
            hp, hn = _edge_flags(i, tm, seg_t, M)
            gs_ref[pl.ds(0, PAD), :] = jnp.zeros((PAD, tf), F32)
            gs_ref[pl.ds(PAD, HALO), :] = gp_ref[...].astype(F32) * hp
            gs_ref[pl.ds(PAD + HALO, tm), :] = g_ref[...].astype(F32)
            gs_ref[pl.ds(PAD + HALO + tm, HALO), :] = gn_ref[...].astype(F32) * hn
            gs_ref[pl.ds(PAD + 2 * HALO + tm, PAD), :] = jnp.zeros((PAD, tf), F32)

            @pl.when(i == 0)
            def _():
                dcw_ref[...] = jnp.zeros_like(dcw_ref)
                dcb_ref[...] = jnp.zeros_like(dcb_ref)

            def fold(v):
                return v[:8] + v[8:]

            for c0, cw in _col_groups(tf):
                cs = pl.ds(c0, cw)
                w0, w1, w2, cb = cw_ref[pl.ds(0, 1), cs], cw_ref[pl.ds(1, 1), cs], cw_ref[pl.ds(2, 1), cs], cb_ref[:, cs]
                acc = [jnp.zeros((8, cw), F32) for _ in range(4)]
                for ci in range(-1, n_piece + 1):
                    r0 = ci * _RC
                    taps = [gs_ref[pl.ds(PAD + HALO + r0 - 1 + k, _RC), cs] for k in range(3)]
                    gc = taps[0] * w0 + taps[1] * w1 + taps[2] * w2 + cb
                    if ci < 0:
                        ue, ae = up_ref[:, cs].astype(F32), ap_ref[:, cs].astype(F32) * hp
                    elif ci == n_piece:
                        ue, ae = un_ref[:, cs].astype(F32), an_ref[:, cs].astype(F32) * hn
                    else:
                        ue, ae = u_ref[pl.ds(r0, _RC), cs].astype(F32), a_ref[pl.ds(r0, _RC), cs].astype(F32)
                    sg = _sigmoid(gc)
                    dgc = ae * ue * (sg * (1.0 + gc * (1.0 - sg)))
                    ds_ref[pl.ds(HALO + r0, _RC), cs] = dgc
                    if 0 <= ci < n_piece:
                        du_ref[pl.ds(r0, _RC), cs] = (ae * gc * sg).astype(BF16)
                        for k in range(3):
                            acc[k] = acc[k] + fold(dgc * taps[k])
                        acc[3] = acc[3] + fold(dgc)
                for r0 in range(0, tm, _RC):
                    b = HALO + r0
                    dg = (ds_ref[pl.ds(b + 1, _RC), cs] * w0 + ds_ref[pl.ds(b, _RC), cs] * w1
                          + ds_ref[pl.ds(b - 1, _RC), cs] * w2)
                    dz_ref[pl.ds(r0, _RC), cs] = dg.astype(BF16)
                for k in range(3):
                    dcw_ref[pl.ds(k, 1), cs] += _colsum(acc[k])
                dcb_ref[:, cs] += _colsum(acc[3])

        @pl.when(p == 1)
        def _():
            dz_ref[...] = du_ref[...]

    def cur(off):
        return pl.BlockSpec((tm, tf), lambda j, i, p: (i, off + j))

    def prv(off):
        return pl.BlockSpec((HALO, tf), lambda j, i, p: (jnp.maximum(i * per - 1, 0), off + j))

    def nxt(off):
        return pl.BlockSpec((HALO, tf), lambda j, i, p: (jnp.minimum((i + 1) * per, last), off + j))

    return pl.pallas_call(
        body, out_shape=[SDS((M, F2), BF16), SDS((3, Fd), F32), SDS((1, Fd), F32)], grid=(nf, M // tm, 2),
        in_specs=[cur(0), prv(0), nxt(0), cur(nf), prv(nf), nxt(nf), cur(0), prv(0), nxt(0),
                  pl.BlockSpec((3, tf), lambda j, i, p: (0, j)), pl.BlockSpec((1, tf), lambda j, i, p: (0, j))],
        out_specs=[pl.BlockSpec((tm, tf), lambda j, i, p: (i, p * nf + j)),
                   pl.BlockSpec((3, tf), lambda j, i, p: (0, j)), pl.BlockSpec((1, tf), lambda j, i, p: (0, j))],
        scratch_shapes=[pltpu.VMEM((tm + 2 * HALO + 2 * PAD, tf), F32), pltpu.VMEM((tm + 2 * HALO, tf), F32),
                        pltpu.VMEM((tm, tf), BF16)],
        compiler_params=_params(("parallel", "arbitrary", "arbitrary")), name=name)(
            z, z, z, z, z, z, da, da, da, conv_w, conv_b)


_LN2 = math.log(2.0)
_QSCALE = (NOPE + ROPE) ** -0.5 / _LN2


def _swap32(x):
    lane = lax.broadcasted_iota(jnp.int32, x.shape, 1)
    return jnp.where((lane % 64) < 32, pltpu.roll(x, 96, axis=1), pltpu.roll(x, 32, axis=1))


def _rms(x, g):
    r = lax.rsqrt(jnp.mean(x * x, axis=-1, keepdims=True) + EPS)
    return x * r * g


def _rms_bwd(x, g, dy):
    r = lax.rsqrt(jnp.mean(x * x, axis=-1, keepdims=True) + EPS)
    xh = x * r
    dxh = dy * g
    return r * (dxh - xh * jnp.mean(dxh * xh, axis=-1, keepdims=True)), _colsum(dy * xh)


def _mla_prep_fwd(z, gq, gkv, cos, sin, QL, KL, name):
    M, NZ = z.shape
    tm = _tile(M, (256, 128))

    def body(z_ref, gq_ref, gkv_ref, cos_ref, sin_ref, cq_ref, ckv_ref, kpe_ref):
        zv = z_ref[...]
        cq_ref[...] = _rms(zv[:, :QL], gq_ref[...]).astype(BF16)
        ckv_ref[...] = _rms(zv[:, QL:QL + KL], gkv_ref[...]).astype(BF16)
        kp = zv[:, QL + KL:]
        r = kp * cos_ref[...] + _swap32(kp) * sin_ref[...]
        lane = lax.broadcasted_iota(jnp.int32, r.shape, 1)
        kpe_ref[0] = jnp.where(lane < ROPE, r, 0.0).astype(BF16)
        kpe_ref[1] = jnp.where(lane >= ROPE, r, 0.0).astype(BF16)

    tab = pl.BlockSpec((tm, 128), lambda i: (i, 0))
    return pl.pallas_call(
        body, out_shape=[SDS((M, QL), BF16), SDS((M, KL), BF16), SDS((2, M, 128), BF16)], grid=(M // tm,),
        in_specs=[pl.BlockSpec((tm, NZ), lambda i: (i, 0)), pl.BlockSpec((1, QL), lambda i: (0, 0)),
                  pl.BlockSpec((1, KL), lambda i: (0, 0)), tab, tab],
        out_specs=[pl.BlockSpec((tm, QL), lambda i: (i, 0)), pl.BlockSpec((tm, KL), lambda i: (i, 0)),
                   pl.BlockSpec((2, tm, 128), lambda i: (0, i, 0))],
        compiler_params=_params(("parallel",)), name=name)(z, gq, gkv, cos, sin)


def _mla_prep_bwd(z, dcq, dckv, dkpe, gq, gkv, cos, sin, QL, KL, seg_t, name):
    M, NZ = z.shape
    H = dkpe.shape[0]
    tm = _row_tile(seg_t, M)
    nt = seg_t // tm

    def body(z_ref, dcq_ref, dckv_ref, dkpe_ref, gq_ref, gkv_ref, cos_ref, sin_ref, dz_ref, dgq_ref, dgkv_ref):
        i = pl.program_id(0)

        @pl.when(i == 0)
        def _():
            dgq_ref[...] = jnp.zeros_like(dgq_ref)
            dgkv_ref[...] = jnp.zeros_like(dgkv_ref)

        zv = z_ref[...]
        dyq = jnp.where(i < nt, dcq_ref[...], 0.0)
        dxq, dgq = _rms_bwd(zv[:, :QL], gq_ref[...], dyq)
        dxkv, dgkv = _rms_bwd(zv[:, QL:QL + KL], gkv_ref[...], dckv_ref[...])
        dgq_ref[...] += dgq
        dgkv_ref[...] += dgkv
        even = dkpe_ref[0]
        odd = dkpe_ref[1]
        for h in range(2, H, 2):
            even = even + dkpe_ref[h]
            odd = odd + dkpe_ref[h + 1]
        lane = lax.broadcasted_iota(jnp.int32, even.shape, 1)
        dr = jnp.where(lane < ROPE, even, odd)
        dkp = dr * cos_ref[...] - _swap32(dr) * sin_ref[...]
        dz_ref[...] = jnp.concatenate([dxq, dxkv, dkp], axis=1).astype(BF16)

    tab = pl.BlockSpec((tm, 128), lambda i: (i, 0))
    return pl.pallas_call(
        body, out_shape=[SDS((M, NZ), BF16), SDS((1, QL), F32), SDS((1, KL), F32)], grid=(M // tm,),
        in_specs=[pl.BlockSpec((tm, NZ), lambda i: (i, 0)),
                  pl.BlockSpec((tm, QL), lambda i: (jnp.minimum(i, nt - 1), 0)),
                  pl.BlockSpec((tm, KL), lambda i: (i, 0)), pl.BlockSpec((H, tm, 128), lambda i: (0, i, 0)),
                  pl.BlockSpec((1, QL), lambda i: (0, 0)), pl.BlockSpec((1, KL), lambda i: (0, 0)), tab, tab],
        out_specs=[pl.BlockSpec((tm, NZ), lambda i: (i, 0)), pl.BlockSpec((1, QL), lambda i: (0, 0)),
                   pl.BlockSpec((1, KL), lambda i: (0, 0))],
        compiler_params=_params(("arbitrary",)), name=name)(z, dcq, dckv, dkpe, gq, gkv, cos, sin)


def _qrope_fwd(q, cos, sin, HN, name):
    T, NQ = q.shape
    tm = _tile(T, (256, 128))

    def body(q_ref, cos_ref, sin_ref, o_ref):
        o_ref[:, :HN] = (q_ref[:, :HN] * _QSCALE).astype(BF16)
        for cb in range((NQ - HN) // 128):
            sl = slice(HN + cb * 128, HN + (cb + 1) * 128)
            xv = q_ref[:, sl]
            o_ref[:, sl] = ((xv * cos_ref[...] + _swap32(xv) * sin_ref[...]) * _QSCALE).astype(BF16)

    tab = pl.BlockSpec((tm, 128), lambda i: (i, 0))
    return pl.pallas_call(
        body, out_shape=SDS((T, NQ), BF16), grid=(T // tm,),
        in_specs=[pl.BlockSpec((tm, NQ), lambda i: (i, 0)), tab, tab],
        out_specs=pl.BlockSpec((tm, NQ), lambda i: (i, 0)),
        compiler_params=_params(("parallel",)), name=name)(q, cos, sin)


def _qrope_bwd(dqpe, dqa, cos, sin, HN, name):
    T, HW = dqpe.shape
    HR = HW // 2
    tm = _tile(T, (256, 128))

    def body(d_ref, dqa_hbm, cos_ref, sin_ref, o_ref):
        del dqa_hbm
        for pr in range(HR // 128):
            dr = d_ref[:, 2 * pr * 128:(2 * pr + 1) * 128] + d_ref[:, (2 * pr + 1) * 128:(2 * pr + 2) * 128]
            o_ref[:, pr * 128:(pr + 1) * 128] = (dr * cos_ref[...] - _swap32(dr) * sin_ref[...]).astype(BF16)

    tab = pl.BlockSpec((tm, 128), lambda i: (i, 0))
    return pl.pallas_call(
        body, out_shape=SDS(dqa.shape, BF16), grid=(T // tm,),
        in_specs=[pl.BlockSpec((tm, HW), lambda i: (i, 0)), pl.BlockSpec(memory_space=pl.ANY), tab, tab],
        out_specs=pl.BlockSpec((tm, HR), lambda i: (i, HN // HR)),
        input_output_aliases={1: 0}, compiler_params=_params(("parallel",)), name=name)(dqpe, dqa, cos, sin)


_NT = (((1,), (1,)), ((), ()))
_TN = (((0,), (0,)), ((), ()))


def _attn_fwd(qa, kv, kpe, T, H, name, carry=None):
    M = kv.shape[0]
    tq = _tile(T, (512, 256, 128))
    scale = (NOPE + ROPE) ** -0.5

    def body(qn_ref, qp_ref, kv_ref, kpe_ref, o_ref, lse_ref, kc_ref):
        @pl.when(pl.program_id(1) == 0)
        def _():
            kc_ref[:, :NOPE] = kv_ref[:, :NOPE]
            kc_ref[:, NOPE:] = kpe_ref[0]

        qc = jnp.concatenate([qn_ref[...], qp_ref[...]], axis=1)
        s = lax.dot_general(qc, kc_ref[...], _NT, preferred_element_type=F32)
        m = jnp.max(s, axis=-1, keepdims=True)
        p = jnp.exp2(s - m)
        l = jnp.sum(p, axis=-1, keepdims=True)
        o = jnp.dot(p.astype(BF16), kv_ref[:, NOPE:], preferred_element_type=F32)
        o_ref[...] = (o / l).astype(BF16)
        lse_ref[...] = jnp.broadcast_to(m + jnp.log2(l), lse_ref.shape)

    return _carry_call(
        body, carry, out_shape=[SDS((T, H * VDIM), BF16), SDS((T, H * 128), F32)], grid=(H, T // tq),
        in_specs=[pl.BlockSpec((tq, NOPE), lambda h, i: (i, h)), pl.BlockSpec((tq, 128), lambda h, i: (i, H + h // 2)),
                  pl.BlockSpec((M, NOPE + VDIM), lambda h, i: (0, h)), pl.BlockSpec((1, M, 128), lambda h, i: (h % 2, 0, 0))],
        out_specs=[pl.BlockSpec((tq, VDIM), lambda h, i: (i, h)), pl.BlockSpec((tq, 128), lambda h, i: (i, h))],
        scratch_shapes=[pltpu.VMEM((M, NOPE + 128), BF16)],
        sem=("parallel", "arbitrary"), name=name, ins=(qa, qa, kv, kpe))


def _attn_bwd(qa, kv, kpe, do, lse, T, H, name, carry=None):
    M = kv.shape[0]
    tq = _tile(T, (256, 128))
    nq = T // tq
    scale = (NOPE + ROPE) ** -0.5

    def body(qn_ref, qp_ref, kv_ref, kpe_ref, do_ref, lse_ref, dqa_ref, dqpe_ref, dkv_ref, dkpe_ref, kc_ref, dk_acc, dv_acc):
        i = pl.program_id(1)

        @pl.when(i == 0)
        def _():
            kc_ref[:, :NOPE] = kv_ref[:, :NOPE]
            kc_ref[:, NOPE:] = kpe_ref[0]
            dk_acc[...] = jnp.zeros_like(dk_acc)
            dv_acc[...] = jnp.zeros_like(dv_acc)

        qc = jnp.concatenate([qn_ref[...], qp_ref[...]], axis=1)
        dov = do_ref[...]
        kc = kc_ref[...]
        s = lax.dot_general(qc, kc, _NT, preferred_element_type=F32)
        p = jnp.exp2(s - lse_ref[:, 0:1])
        dp = lax.dot_general(dov, kv_ref[:, NOPE:], _NT, preferred_element_type=F32)
        delta = jnp.sum(p * dp, axis=-1, keepdims=True)
        ds = (p * (dp - delta)).astype(BF16)
        dq = jnp.dot(ds, kc, preferred_element_type=F32) * scale
        dqa_ref[...] = dq[:, :NOPE].astype(BF16)
        dqpe_ref[...] = dq[:, NOPE:]
        dv_acc[...] += lax.dot_general(p.astype(BF16), dov, _TN, preferred_element_type=F32)
        dk_acc[...] += lax.dot_general(ds, qc, _TN, preferred_element_type=F32)

        @pl.when(i == nq - 1)
        def _():
            dkv_ref[:, :NOPE] = (dk_acc[:, :NOPE] * _LN2).astype(BF16)
            dkv_ref[:, NOPE:] = dv_acc[...].astype(BF16)
            dkpe_ref[0] = dk_acc[:, NOPE:] * _LN2

    return _carry_call(
        body, carry,
        out_shape=[SDS((T, H * (NOPE + ROPE)), BF16), SDS((T, H * 128), F32), SDS((M, H * (NOPE + VDIM)), BF16),
                   SDS((H, M, 128), F32)],
        grid=(H, nq),
        in_specs=[pl.BlockSpec((tq, NOPE), lambda h, i: (i, h)), pl.BlockSpec((tq, 128), lambda h, i: (i, H + h // 2)),
                  pl.BlockSpec((M, NOPE + VDIM), lambda h, i: (0, h)), pl.BlockSpec((1, M, 128), lambda h, i: (h % 2, 0, 0)),
                  pl.BlockSpec((tq, VDIM), lambda h, i: (i, h)), pl.BlockSpec((tq, 128), lambda h, i: (i, h))],
        out_specs=[pl.BlockSpec((tq, NOPE), lambda h, i: (i, h)), pl.BlockSpec((tq, 128), lambda h, i: (i, h)),
                   pl.BlockSpec((M, NOPE + VDIM), lambda h, i: (0, h)), pl.BlockSpec((1, M, 128), lambda h, i: (h, 0, 0))],
        scratch_shapes=[pltpu.VMEM((M, NOPE + 128), BF16), pltpu.VMEM((M, NOPE + 128), F32), pltpu.VMEM((M, VDIM), F32)],
        sem=("parallel", "arbitrary"), name=name, ins=(qa, qa, kv, kpe, do, lse))


def _adamw(w, m, v, name, g=None, recv=None):
    R, C = w.shape
    summed = recv is not None
    n_recv = len(recv) if summed else 1
    rows_each = R // n_recv
    tr = rows_each
    for cand in (1024, 512, 256, 128, 64, 32, 16, 8):
        if rows_each % cand == 0 and cand * C <= 131072:
            tr = cand
            break
    per = rows_each // tr
    c1 = 1.0 - ADAM_B1 ** ADAM_STEP
    c2 = 1.0 - ADAM_B2 ** ADAM_STEP

    def update(gv, w_ref, m_ref, v_ref, d_ref, nm_ref, nv_ref):
        mn = ADAM_B1 * m_ref[...] + (1.0 - ADAM_B1) * gv
        vn = ADAM_B2 * v_ref[...] + (1.0 - ADAM_B2) * (gv * gv)
        nm_ref[...] = mn
        nv_ref[...] = vn
        d_ref[...] = -ADAM_LR * ((mn / c1) / (jnp.sqrt(vn / c2) + ADAM_EPS) + ADAM_WD * w_ref[...])

    def body(*refs):
        w_ref, m_ref, v_ref = refs[:3]
        g_refs = refs[3:3 + n_recv]
        outs = refs[3 + n_recv:]
        if not summed:
            update(g_refs[0][...], w_ref, m_ref, v_ref, *outs)
            return
        for r in range(n_recv):
            @pl.when(pl.program_id(0) // per == r)
            def _():
                gv = g_refs[r][0].astype(F32)
                for d in range(1, N_DEV):
                    gv = gv + g_refs[r][d].astype(F32)
                outs[0][...] = gv
                update(gv, w_ref, m_ref, v_ref, *outs[1:])

    blk = pl.BlockSpec((tr, C), lambda i: (i, 0))
    if summed:
        g_specs = [pl.BlockSpec((N_DEV, tr, C), functools.partial(lambda i, r: (0, jnp.clip(i - r * per, 0, per - 1), 0), r=r))
                   for r in range(n_recv)]
    else:
        g_specs = [blk]
    n_out = 4 if summed else 3
    return pl.pallas_call(
        body, out_shape=[SDS((R, C), F32)] * n_out, grid=(R // tr,), in_specs=[blk, blk, blk] + g_specs,
        out_specs=[blk] * n_out, compiler_params=_params(("parallel",)), name=name)(w, m, v, *(recv if summed else [g]))


WEIGHTS = ['c_ctx', 'norm1_g', 'norm2_g', 'w_ada', 'b_ada', 'ab_w_in', 'ab_b_in', 'a_ln_g', 'a_ln_b', 'a_w_s', 'a_b_s',
           'b_conv_w', 'b_conv_b', 'b_ln_g', 'b_ln_b', 'ab_w_out', 'mla_w_in', 'mla_q_norm_g', 'mla_w_uq',
           'mla_kv_norm_g', 'mla_w_ukv', 'mla_w_o', 'ffn_w_up', 'ffn_conv_w', 'ffn_conv_b', 'ffn_w_down', 'final_norm_g']


def _pack(parts):
    flat = jnp.concatenate([p.reshape(-1).astype(F32) for p in parts])
    n = flat.shape[0]
    unit = 65536 if n > 65536 else 1024
    n_pad = -(-n // unit) * unit
    return jnp.pad(flat, (0, n_pad - n)).reshape(n_pad // 128, 128)


def _unpack(flat, like):
    out, off = [], 0
    for shp in like:
        n = math.prod(shp)
        out.append(flat[..., off:off + n].reshape(flat.shape[:-1] + tuple(shp)))
        off += n
    return out


def _rope_tables(T, Tc):
    rows = T // GRID_W
    row = jnp.repeat(jnp.arange(rows, dtype=F32), GRID_W)
    col = jnp.tile(jnp.arange(GRID_W, dtype=F32), rows)
    n_freq = ROPE // 4
    inv = ROPE_THETA ** (-jnp.arange(n_freq, dtype=F32) / n_freq)
    ang = jnp.concatenate([row[:, None] * inv, col[:, None] * inv], axis=-1)
    cos, sin = jnp.cos(ang), jnp.sin(ang)
    cos = jnp.tile(cos, (1, 128 // (ROPE // 2)))
    sin = jnp.tile(jnp.concatenate([-sin, sin], axis=1), (1, 128 // ROPE))
    return (jnp.concatenate([cos, jnp.ones((Tc, 128), F32)], axis=0),
            jnp.concatenate([sin, jnp.zeros((Tc, 128), F32)], axis=0))


def _step(a):
    ax, ay, ac = lax.axis_index("x"), lax.axis_index("y"), lax.axis_index("c")
    me = 4 * ax + 2 * ay + ac
    T, D = a['x'].shape[1:]
    Tc = a['ctx'].shape[1]
    M = T + Tc
    W, Wb = a['a_ln_g'].shape[1], a['b_ln_g'].shape[1]
    assert W == Wb and T % Tc == 0
    Fd = a['ffn_conv_b'].shape[1]
    QL, KL = a['mla_q_norm_g'].shape[1] * N_DEV, a['mla_kv_norm_g'].shape[1] * N_DEV
    H = a['mla_w_ukv'].shape[2] * N_DEV // (NOPE + VDIM)
    HN, HR = H * NOPE, H * ROPE
    kw = a['b_conv_w'].shape[1]
    NA = a['w_ada'].shape[2]
    bf = lambda t: t.astype(BF16)

    small_shapes = [(D,), (kw, Wb // N_DEV), (2, 3, Fd // N_DEV), (QL // N_DEV,), (KL // N_DEV,)]
    g_small = _all_gather(_pack([a['c'][0], a['b_conv_w'][0], a['ffn_conv_w'], a['mla_q_norm_g'][0], a['mla_kv_norm_g'][0]]),
                          "ag_small")
    c_all, bcw, fcw, gq, gkv = _unpack(g_small.reshape(N_DEV, -1), small_shapes)
    bcw = jnp.transpose(bcw, (1, 0, 2)).reshape(kw, Wb)
    fcw = jnp.transpose(fcw, (1, 2, 0, 3)).reshape(2, 3, Fd)
    gq, gkv = gq.reshape(1, QL), gkv.reshape(1, KL)

    a16 = jnp.concatenate([c_all, a['c_ctx'][None], jnp.zeros((N_DEV - 1, D), F32)], axis=0)
    b_loc = lax.dynamic_slice(a['b_ada'], (0, me * NA), (2, NA))
    mods = [_mm(a16, a['w_ada'][l], mode="nn", out_dtype=F32, name=f"ada_fwd{l}", bias=b_loc[l:l + 1], a_silu=True)
            for l in range(2)]
    gm = _all_gather(jnp.concatenate(mods, axis=0), "ag_mod").reshape(N_DEV, 2, 2 * N_DEV, NA)
    gm = jnp.transpose(gm, (1, 2, 0, 3)).reshape(2, 2 * N_DEV, 6 * D)
    mod_lat = [lax.dynamic_slice(gm[l], (me, 0), (1, 6 * D)).reshape(6, 1, 1, D) for l in range(2)]
    mod_ctx = [gm[l][N_DEV].reshape(6, 1, 1, D) for l in range(2)]

    def mod(l, k, both):
        return jnp.concatenate([mod_lat[l][k], mod_ctx[l][k]], axis=0) if both else mod_lat[l][k]

    def from_cols(g):
        return jnp.transpose(g, (1, 0, 2)).reshape(g.shape[1], -1)

    def from_rows(g):
        return g.reshape(-1, g.shape[2])

    def ag(x):
        return (x, False)

    def a2a(x):
        return (x, True)

    cos, sin = _rope_tables(T, Tc)
    n1g, n2g = a['norm1_g'], a['norm2_g']
    a_bst = a['a_b_s'][0].T
    mm = functools.partial(_mm)
    up_sh, dn_sh = bf(a['ffn_w_up']), bf(a['ffn_w_down'])

    w_abin = from_cols(_all_gather(bf(a['ab_w_in'][0]), "ag_ab_w_in"))
    x0 = jnp.concatenate([a['x'][0], a['ctx'][0]], axis=0)
    h1 = _normmod_fwd(x0, n1g[0:1], mod(0, 0, True), mod(0, 1, True), T, "l0_norm1")
    z, (g_about,) = mm(h1, w_abin, mode="nn", out_dtype=F32, name="l0_ab_in", bias=a['ab_b_in'],
                       carry=[ag(bf(a['ab_w_out'][0]))])
    w_about = from_rows(g_about)
    y = _gmlp_fwd(z, a['a_ln_g'], a['a_ln_b'], a['a_w_s'][0], a_bst, W, "l0_gmlp")
    y = _conf_fwd(z, y, bcw, a['b_conv_b'], a['b_ln_g'], a['b_ln_b'], W, Wb, T, "l0_conf")
    (x1, o1), (g_up0,) = mm(y, w_about, mode="nn", out_dtype=F32, name="l0_ab_out", res=x0, gate=mod(0, 2, True),
                            seg_t=T, carry=[ag(up_sh[0])])
    w_up = [from_cols(g_up0), None]
    h2 = _normmod_fwd(x1, n2g[0:1], mod(0, 3, True), mod(0, 4, True), T, "l0_norm2")
    z2, (g_dn0,) = mm(h2, w_up[0], mode="nn", out_dtype=BF16, name="l0_up", carry=[ag(dn_sh[0])])
    w_dn = [from_rows(g_dn0), None]
    a2 = _ffn_act_fwd(z2, fcw[0], a['ffn_conv_b'][0:1], T, "l0_act")
    (x2, o2), (g_in, g_uq) = mm(a2, w_dn[0], mode="nn", out_dtype=F32, name="l0_down", res=x1, gate=mod(0, 5, True),
                                seg_t=T, carry=[ag(bf(a['mla_w_in'][0])), ag(bf(a['mla_w_uq'][0]))])
    w_in = from_rows(g_in)
    w_in = jnp.concatenate([w_in, w_in[:, QL + KL:]], axis=1)
    w_uq = from_cols(g_uq).reshape(QL, H, NOPE + ROPE)
    w_uq = jnp.concatenate([w_uq[:, :, :NOPE].reshape(QL, HN), w_uq[:, :, NOPE:].reshape(QL, HR)], axis=1)

    h3 = _normmod_fwd(x2, n1g[1:2], mod(1, 0, True), mod(1, 1, True), T, "l1_norm1")
    z3, (g_ukv,) = mm(h3, w_in, mode="nn", out_dtype=F32, name="l1_mla_in", carry=[ag(bf(a['mla_w_ukv'][0]))])
    w_ukv = from_cols(g_ukv)
    cqn, ckvn, kpe = _mla_prep_fwd(z3, gq, gkv, cos, sin, QL, KL, "l1_prep")
    q, (g_wo,) = mm(cqn, w_uq, mode="nn", out_dtype=F32, name="l1_uq", rows=T, carry=[ag(bf(a['mla_w_o'][0]))])
    w_o = from_rows(g_wo)
    kv = mm(ckvn, w_ukv, mode="nn", out_dtype=BF16, name="l1_ukv")
    qa = _qrope_fwd(q, cos, sin, HN, "l1_qrope")
    (o_att, lse), (g_up1,) = _attn_fwd(qa, kv, kpe, T, H, "l1_attn", carry=[ag(up_sh[1])])
    w_up[1] = from_cols(g_up1)
    x3, o3 = mm(o_att, w_o, mode="nn", out_dtype=F32, name="l1_wo", res=x2, gate=mod(1, 2, False), seg_t=T)
    h4 = _normmod_fwd(x3, n2g[1:2], mod(1, 3, False), mod(1, 4, False), T, "l1_norm2")
    z4, (g_dn1,) = mm(h4, w_up[1], mode="nn", out_dtype=BF16, name="l1_up", carry=[ag(dn_sh[1])])
    w_dn[1] = from_rows(g_dn1)
    a4 = _ffn_act_fwd(z4, fcw[1], a['ffn_conv_b'][1:2], T, "l1_act")
    x4, o4 = mm(a4, w_dn[1], mode="nn", out_dtype=F32, name="l1_down", res=x3, gate=mod(1, 5, False), seg_t=T)

    dx4, loss_cols, d_fng, do4, dg2_1 = _final(x4, a['final_norm_g'][None], a['loss_target'][0], o4, mod(1, 5, False),
                                               "final")
    loss = lax.psum(jnp.sum(loss_cols), ("x", "y", "c"))

    def cols(dw):
        k, n = dw.shape
        return jnp.transpose(dw.reshape(k, N_DEV, n // N_DEV), (1, 0, 2))

    def rows(dw):
        return dw.reshape(N_DEV, dw.shape[0] // N_DEV, dw.shape[1])

    da4 = mm(do4, w_dn[1], mode="nt", out_dtype=BF16, name="l1_down_dx")
    dw_dn1 = mm(a4, do4, mode="tn", out_dtype=BF16, name="l1_down_dw")
    dz4, dfcw1, dfcb1 = _ffn_act_bwd(z4, da4, fcw[1], a['ffn_conv_b'][1:2], T, "l1_act_bwd")
    dw_up1, (r_dn1,) = mm(h4, dz4, mode="tn", out_dtype=BF16, name="l1_up_dw", carry=[a2a(rows(dw_dn1))])
    dh4 = mm(dz4, w_up[1], mode="nt", out_dtype=F32, name="l1_up_dx")
    dx3, dn2g1, dsh2_1, dsc2_1, do3, dg1_1 = _normmod_bwd(x3, n2g[1:2], mod(1, 4, False), dh4, dx4, T, "l1_norm2_bwd",
                                                         o_prev=o3, gate_prev=mod(1, 2, False))
    d_oatt = mm(do3, w_o, mode="nt", out_dtype=BF16, name="l1_wo_dx")
    dw_o = mm(o_att, do3, mode="tn", out_dtype=BF16, name="l1_wo_dw")
    (dqa, dqpe, dkv, dkpe), (r_up1, r_wo) = _attn_bwd(qa, kv, kpe, d_oatt, lse, T, H, "l1_attn_bwd",
                                                      carry=[a2a(cols(dw_up1)), a2a(rows(dw_o))])
    dqa = _qrope_bwd(dqpe, dqa, cos, sin, HN, "l1_qrope_bwd")
    dcq = mm(dqa, w_uq, mode="nt", out_dtype=F32, name="l1_uq_dx")
    dw_uq = mm(cqn, dqa, mode="tn", out_dtype=BF16, name="l1_uq_dw", rows=T)
    dw_uq = jnp.concatenate([dw_uq[:, :HN].reshape(QL, H, NOPE), dw_uq[:, HN:].reshape(QL, H, ROPE)], axis=2)
    dw_uq = dw_uq.reshape(QL, H * (NOPE + ROPE))
    dckv = mm(dkv, w_ukv, mode="nt", out_dtype=F32, name="l1_ukv_dx")
    dw_ukv = mm(ckvn, dkv, mode="tn", out_dtype=BF16, name="l1_ukv_dw")
    dz3, dgq, dgkv = _mla_prep_bwd(z3, dcq, dckv, dkpe, gq, gkv, cos, sin, QL, KL, T, "l1_prep_bwd")
    dh3 = mm(dz3, w_in, mode="nt", out_dtype=F32, name="l1_mla_in_dx")
    dw_in = mm(h3, dz3, mode="tn", out_dtype=BF16, name="l1_mla_in_dw").astype(F32)
    dw_in = jnp.concatenate([dw_in[:, :QL + KL], dw_in[:, QL + KL:QL + KL + ROPE] + dw_in[:, QL + KL + ROPE:QL + KL + 2 * ROPE]],
                            axis=1).astype(BF16)
    dx2, dn1g1, dsh1_1, dsc1_1, do2, dg2_0 = _normmod_bwd(x2, n1g[1:2], mod(1, 1, True), dh3, dx3, T, "l1_norm1_bwd",
                                                         o_prev=o2, gate_prev=mod(0, 5, True))
    da2, (r_uq, r_ukv) = mm(do2, w_dn[0], mode="nt", out_dtype=BF16, name="l0_down_dx",
                            carry=[a2a(cols(dw_uq)), a2a(cols(dw_ukv))])
    dw_dn0, (r_in,) = mm(a2, do2, mode="tn", out_dtype=BF16, name="l0_down_dw", carry=[a2a(rows(dw_in))])
    dz2, dfcw0, dfcb0 = _ffn_act_bwd(z2, da2, fcw[0], a['ffn_conv_b'][0:1], T, "l0_act_bwd")
    dw_up0, (r_dn0,) = mm(h2, dz2, mode="tn", out_dtype=BF16, name="l0_up_dw", carry=[a2a(rows(dw_dn0))])
    dh2, (r_up0,) = mm(dz2, w_up[0], mode="nt", out_dtype=F32, name="l0_up_dx", carry=[a2a(cols(dw_up0))])
    dx1, dn2g0, dsh2_0, dsc2_0, do1, dg1_0 = _normmod_bwd(x1, n2g[0:1], mod(0, 4, True), dh2, dx2, T, "l0_norm2_bwd",
                                                         o_prev=o1, gate_prev=mod(0, 2, True))
    dy = mm(do1, w_about, mode="nt", out_dtype=F32, name="l0_ab_out_dx")
    dw_about = mm(y, do1, mode="tn", out_dtype=BF16, name="l0_ab_out_dw")
    dz, dlag, dlab, dws, dbs, dbin_a = _gmlp_bwd(z, dy, a['a_ln_g'], a['a_ln_b'], a['a_w_s'][0], a_bst, W, "l0_gmlp_bwd")
    dhc, dlbg, dlbb, dbcb = _conf_bwd1(z, dy, bcw, a['b_conv_b'], a['b_ln_g'], a['b_ln_b'], W, Wb, T, "l0_conf_bwd1")
    dz, dbcw, dbin_b = _conf_bwd2(z, dhc, dz, bcw, W, Wb, T, "l0_conf_bwd2")
    dh1, (r_about,) = mm(dz, w_abin, mode="nt", out_dtype=F32, name="l0_ab_in_dx", carry=[a2a(rows(dw_about))])
    dw_abin = mm(h1, dz, mode="tn", out_dtype=BF16, name="l0_ab_in_dw")
    dx0, dn1g0, dsh1_0, dsc1_0 = _normmod_bwd(x0, n1g[0:1], mod(0, 1, True), dh1, dx1, T, "l0_norm1_bwd")
    r_abin = _all_to_all(cols(dw_abin), "a2a_ab_w_in")

    zero = jnp.zeros((D,), F32)
    dmod = jnp.stack([
        jnp.stack([jnp.stack([dsh1_0[0, 0], dsc1_0[0, 0], dg1_0[0, 0], dsh2_0[0, 0], dsc2_0[0, 0], dg2_0[0, 0]]),
                   jnp.stack([dsh1_0[1, 0], dsc1_0[1, 0], dg1_0[1, 0], dsh2_0[1, 0], dsc2_0[1, 0], dg2_0[1, 0]])]),
        jnp.stack([jnp.stack([dsh1_1[0, 0], dsc1_1[0, 0], dg1_1[0, 0], dsh2_1[0, 0], dsc2_1[0, 0], dg2_1[0, 0]]),
                   jnp.stack([dsh1_1[1, 0], dsc1_1[1, 0], zero, zero, zero, zero])])])
    small = {
        'norm1_g': jnp.concatenate([dn1g0, dn1g1], axis=0), 'norm2_g': jnp.concatenate([dn2g0, dn2g1], axis=0),
        'ab_b_in': jnp.concatenate([dbin_a, dbin_b], axis=1), 'a_ln_g': dlag, 'a_ln_b': dlab, 'a_w_s': dws[None],
        'a_b_s': jnp.sum(dbs, axis=-1)[None], 'b_conv_w': dbcw, 'b_conv_b': dbcb, 'b_ln_g': dlbg, 'b_ln_b': dlbb,
        'mla_q_norm_g': dgq, 'mla_kv_norm_g': dgkv, 'ffn_conv_w': jnp.stack([dfcw0, dfcw1]),
        'ffn_conv_b': jnp.concatenate([dfcb0, dfcb1], axis=0), 'final_norm_g': d_fng[0],
    }
    names = list(small)
    g2 = _all_gather(_pack([dmod] + [small[n] for n in names]), "ag_small_grads")
    red = _sum_lead(g2, "sum_small_grads").reshape(-1)
    red = dict(zip(names, _unpack(red, [(2, 2, 6, D)] + [small[n].shape for n in names])[1:]))
    dmod_all = g2.reshape(N_DEV, -1)[:, :2 * 2 * 6 * D].reshape(N_DEV, 2, 2, 6 * D)

    a16g = jnp.concatenate([c_all, jnp.tile(a['c_ctx'][None], (N_DEV, 1))], axis=0)
    dm_loc = lax.dynamic_slice(dmod_all, (0, 0, 0, me * NA), (N_DEV, 2, 2, NA))
    g_wada, cpart = [], []
    for l in range(2):
        dm16 = jnp.concatenate([dm_loc[:, l, 0], dm_loc[:, l, 1]], axis=0)
        g_wada.append(mm(a16g, dm16, mode="tn", out_dtype=F32, name=f"ada_dw{l}", a_silu=True))
        cpart.append(mm(dm_loc[:, l, 1], a['w_ada'][l], mode="nt", out_dtype=F32, name=f"ada_dc{l}"))
    g_bada = _sum_lead(jnp.transpose(dmod_all, (0, 2, 1, 3)).reshape(2 * N_DEV, 2 * 6 * D // 128, 128), "sum_b_ada")
    g_cc = _all_gather(jnp.concatenate(cpart, axis=0), "ag_c_ctx")
    g_cc = _sum_lead(g_cc.reshape(2 * N_DEV * N_DEV, D // 128, 128), "sum_c_ctx").reshape(D)
    grads = {
        'c_ctx': g_cc * _dsilu(a['c_ctx']), 'w_ada': jnp.stack(g_wada), 'b_ada': g_bada.reshape(2, 6 * D),
        'b_conv_w': lax.dynamic_slice(red['b_conv_w'], (0, me * (Wb // N_DEV)), (kw, Wb // N_DEV))[None],
        'ffn_conv_w': lax.dynamic_slice(red['ffn_conv_w'], (0, 0, me * (Fd // N_DEV)), (2, 3, Fd // N_DEV)),
        'mla_q_norm_g': lax.dynamic_slice(red['mla_q_norm_g'], (0, me * (QL // N_DEV)), (1, QL // N_DEV)),
        'mla_kv_norm_g': lax.dynamic_slice(red['mla_kv_norm_g'], (0, me * (KL // N_DEV)), (1, KL // N_DEV)),
    }
    for n in names:
        if n not in grads:
            grads[n] = red[n].reshape(a[n].shape)

    recvs = {'ab_w_in': [r_abin], 'ab_w_out': [r_about], 'mla_w_in': [r_in], 'mla_w_uq': [r_uq], 'mla_w_ukv': [r_ukv],
             'mla_w_o': [r_wo], 'ffn_w_up': [r_up0, r_up1], 'ffn_w_down': [r_dn0, r_dn1]}
    out = {}
    for n in WEIGHTS:
        shp = a[n].shape
        w2 = a[n].reshape(-1, shp[-1])
        m2, v2 = a['m_' + n].reshape(w2.shape), a['v_' + n].reshape(w2.shape)
        if n in recvs:
            res = _adamw(w2, m2, v2, "adamw_" + n, recv=recvs[n])
        else:
            g2d = grads[n].reshape(w2.shape)
            res = (g2d,) + tuple(_adamw(w2, m2, v2, "adamw_" + n, g=g2d))
        out[n] = [r.reshape(shp) for r in res]
    return (loss, dx0[:T][None], *[out[n][0] for n in WEIGHTS], *[out[n][1] for n in WEIGHTS],
            *[out[n][2] for n in WEIGHTS], *[out[n][3] for n in WEIGHTS])


def kernel(x, c, ctx, c_ctx, norm1_g, norm2_g, w_ada, b_ada, ab_w_in, ab_b_in, a_ln_g, a_ln_b, a_w_s, a_b_s, b_conv_w, b_conv_b, b_ln_g, b_ln_b, ab_w_out, mla_w_in, mla_q_norm_g, mla_w_uq, mla_kv_norm_g, mla_w_ukv, mla_w_o, ffn_w_up, ffn_conv_w, ffn_conv_b, ffn_w_down, final_norm_g, loss_target, m_c_ctx, m_norm1_g, m_norm2_g, m_w_ada, m_b_ada, m_ab_w_in, m_ab_b_in, m_a_ln_g, m_a_ln_b, m_a_w_s, m_a_b_s, m_b_conv_w, m_b_conv_b, m_b_ln_g, m_b_ln_b, m_ab_w_out, m_mla_w_in, m_mla_q_norm_g, m_mla_w_uq, m_mla_kv_norm_g, m_mla_w_ukv, m_mla_w_o, m_ffn_w_up, m_ffn_conv_w, m_ffn_conv_b, m_ffn_w_down, m_final_norm_g, v_c_ctx, v_norm1_g, v_norm2_g, v_w_ada, v_b_ada, v_ab_w_in, v_ab_b_in, v_a_ln_g, v_a_ln_b, v_a_w_s, v_a_b_s, v_b_conv_w, v_b_conv_b, v_b_ln_g, v_b_ln_b, v_ab_w_out, v_mla_w_in, v_mla_q_norm_g, v_mla_w_uq, v_mla_kv_norm_g, v_mla_w_ukv, v_mla_w_o, v_ffn_w_up, v_ffn_conv_w, v_ffn_conv_b, v_ffn_w_down, v_final_norm_g):
    return _step(dict(locals()))
```

```python
import functools
import math

import jax
import jax.numpy as jnp
from jax import lax
from jax.experimental import pallas as pl
from jax.experimental.pallas import tpu as pltpu

F32 = jnp.float32
BF16 = jnp.bfloat16
SDS = jax.ShapeDtypeStruct

N_DEV = 8
EPS = 1e-6
CHUNK = 128
NOPE = 128
ROPE = 64
VDIM = 128
GRID_W = 64
ROPE_THETA = 10000.0
HALO = 16
ADAM_LR, ADAM_B1, ADAM_B2, ADAM_EPS, ADAM_WD, ADAM_STEP = 0.001, 0.9, 0.999, 1e-08, 0.01, 10
VMEM_LIMIT = 56 * 1024 * 1024


def _tile(n, prefs):
    for p in prefs:
        if n % p == 0:
            return p
    return n


def _params(sem, vmem=VMEM_LIMIT):
    return pltpu.CompilerParams(dimension_semantics=sem, vmem_limit_bytes=vmem)


def _sigmoid(x):
    return 0.5 * jnp.tanh(0.5 * x) + 0.5


def _silu(x):
    return x * _sigmoid(x)


def _dsilu(x):
    s = _sigmoid(x)
    return s * (1.0 + x * (1.0 - s))


_GELU_C = math.sqrt(2.0 / math.pi)


def _gelu(x):
    return 0.5 * x * (1.0 + jnp.tanh(_GELU_C * (x + 0.044715 * x * x * x)))


def _dgelu(x):
    t = jnp.tanh(_GELU_C * (x + 0.044715 * x * x * x))
    return 0.5 * (1.0 + t) + 0.5 * x * (1.0 - t * t) * _GELU_C * (1.0 + 3.0 * 0.044715 * x * x)


def _colsum(v):
    return jnp.sum(v, axis=0, keepdims=True)


_SIBLING = 1
_CHIPS = (2, 4, 6)


def _xchg(x_ref, o_ref, send_sems, recv_sems, local_sem, scatter):
    ax, ay, ac = lax.axis_index("x"), lax.axis_index("y"), lax.axis_index("c")
    me = 4 * ax + 2 * ay + ac

    def dev(k):
        return ax ^ (k >> 2), ay ^ ((k >> 1) & 1), ac ^ (k & 1)

    def idx(k):
        px, py, pc = dev(k)
        return 4 * px + 2 * py + pc

    def copy(k, src, dst, to):
        return pltpu.make_async_remote_copy(src_ref=src, dst_ref=dst, send_sem=send_sems.at[k - 1],
                                            recv_sem=recv_sems.at[k - 1], device_id=dev(to),
                                            device_id_type=pl.DeviceIdType.MESH)

    def own():
        return pltpu.make_async_copy(x_ref.at[me] if scatter else x_ref, o_ref.at[me], local_sem)

    def sends():
        if scatter:
            return [copy(k, x_ref.at[idx(k)], o_ref.at[me], k) for k in range(1, N_DEV)]
        return [copy(k, x_ref, o_ref.at[me], k) for k in (_SIBLING,) + _CHIPS]

    def forwards():
        return [] if scatter else [copy(j + 1, o_ref.at[idx(j)], o_ref.at[idx(j)], _SIBLING) for j in _CHIPS]

    def arrival(k):
        return copy(k, o_ref.at[idx(k)], o_ref.at[idx(k)], k)

    return own, sends, forwards, arrival


def _xchg_start(*refs, scatter):
    own, sends, _, _ = _xchg(*refs, scatter)
    own().start()
    for cp in sends():
        cp.start()


def _xchg_forward(*refs, scatter):
    _, _, forwards, arrival = _xchg(*refs, scatter)
    if not scatter:
        for j, fw in zip(_CHIPS, forwards()):
            arrival(j).wait_recv()
            fw.start()


def _xchg_finish(*refs, scatter):
    own, sends, forwards, arrival = _xchg(*refs, scatter)
    for k in range(1, N_DEV):
        if scatter or k not in _CHIPS:
            arrival(k).wait_recv()
    for cp in sends() + forwards():
        cp.wait_send()
    own().wait()


_XCHG_SEMS = [pltpu.SemaphoreType.DMA((N_DEV - 1,)), pltpu.SemaphoreType.DMA((N_DEV - 1,)), pltpu.SemaphoreType.DMA]


def _xchg_shape(x, scatter):
    return SDS((N_DEV,) + tuple(x.shape[1:] if scatter else x.shape), x.dtype)


def _exchange(x, *, scatter, name):
    def body(*refs):
        _xchg_start(*refs, scatter=scatter)
        _xchg_forward(*refs, scatter=scatter)
        _xchg_finish(*refs, scatter=scatter)

    return pl.pallas_call(
        body, out_shape=_xchg_shape(x, scatter),
        in_specs=[pl.BlockSpec(memory_space=pl.ANY)], out_specs=pl.BlockSpec(memory_space=pl.ANY),
        scratch_shapes=list(_XCHG_SEMS), name=name)(x)


def _carried(body, carry, n_in, n_out, n_scratch, grid):
    nc = len(carry)
    total = math.prod(grid)
    mid = (3 * total) // 4

    def wrapped(*refs):
        ins, cin = refs[:n_in], refs[n_in:n_in + nc]
        o0 = n_in + nc
        outs, cout = refs[o0:o0 + n_out], refs[o0 + n_out:o0 + n_out + nc]
        scr = refs[o0 + n_out + nc:]
        sems = scr[n_scratch:]
        step = pl.program_id(0)
        for ax in range(1, len(grid)):
            step = step * grid[ax] + pl.program_id(ax)

        def each(fn):
            for c in range(nc):
                fn(cin[c], cout[c], *sems[3 * c:3 * c + 3], scatter=carry[c][1])

        @pl.when(step == 0)
        def _():
            each(_xchg_start)

        body(*ins, *outs, *scr[:n_scratch])

        if mid < total - 1:
            @pl.when(step == mid)
            def _():
                each(_xchg_forward)

        @pl.when(step == total - 1)
        def _():
            if mid >= total - 1:
                each(_xchg_forward)
            each(_xchg_finish)

    return wrapped


def _carry_call(body, carry, *, grid, out_shape, in_specs, out_specs, scratch_shapes, sem, name, ins, aliases=None):
    carry = carry or []
    nc = len(carry)
    if nc:
        body = _carried(body, carry, len(in_specs), len(out_shape), len(scratch_shapes), grid)
        anyspec = pl.BlockSpec(memory_space=pl.ANY)
        in_specs = list(in_specs) + [anyspec] * nc
        out_specs = list(out_specs) + [anyspec] * nc
        out_shape = list(out_shape) + [_xchg_shape(x, sc) for x, sc in carry]
        scratch_shapes = list(scratch_shapes) + list(_XCHG_SEMS) * nc
        ins = list(ins) + [x for x, _ in carry]
        sem = ("arbitrary",) * len(grid)
    out = pl.pallas_call(body, out_shape=out_shape, grid=grid, in_specs=in_specs, out_specs=out_specs,
                         scratch_shapes=scratch_shapes, compiler_params=_params(sem), name=name,
                         input_output_aliases=aliases or {})(*ins)
    n_main = len(out) - nc
    return list(out[:n_main]), list(out[n_main:])


def _all_gather(x, name):
    return _exchange(x, scatter=False, name=name)


def _all_to_all(x, name):
    return _exchange(x, scatter=True, name=name)


def _sum_lead(x, name):
    n, R, C = x.shape
    tr = _tile(R, (512, 256, 128, 64, 32, 16, 8))

    def body(x_ref, o_ref):
        acc = x_ref[0]
        for d in range(1, n):
            acc = acc + x_ref[d]
        o_ref[...] = acc

    return pl.pallas_call(
        body, out_shape=SDS((R, C), F32), grid=(R // tr,),
        in_specs=[pl.BlockSpec((n, tr, C), lambda i: (0, i, 0))], out_specs=pl.BlockSpec((tr, C), lambda i: (i, 0)),
        compiler_params=_params(("parallel",)), name=name)(x)


_TP = (1408, 1088, 1024, 768, 512, 256, 128)
_TQ = (1408, 1024, 768, 512, 256, 128)
_TR = (1408, 1024, 512, 256, 128)


def _mm(a, b, *, mode, out_dtype, name, rows=None, bias=None, res=None, gate=None, seg_t=None, a_silu=False, carry=None,
        b_dev=False, out_dev=False, p_range=None):
    if mode == "nn":
        P, R, Q = rows or a.shape[0], a.shape[1], (b.shape[0] * b.shape[2] if b_dev else b.shape[1])
    elif mode == "nt":
        P, R, Q = rows or a.shape[0], a.shape[1], (b.shape[1] if b_dev else b.shape[0])
    else:
        R, P, Q = rows or a.shape[0], a.shape[1], b.shape[1]
    p0 = 0
    if p_range is not None:
        p0, P = p_range
    tp = _tile(P, _TP)
    tq = _tile(Q // N_DEV if (out_dev or (b_dev and mode == "nn")) else Q, _TQ)
    tr = _tile(R // N_DEV if (b_dev and mode == "nt") else R, _TR if mode != "tn" else (512, 256, 128))
    nk = R // tr
    qd = (Q // N_DEV) // tq
    rd = (R // N_DEV) // tr
    if mode == "nn":
        a_spec = pl.BlockSpec((tp, tr), lambda i, j, k: (i, k))
        b_spec = (pl.BlockSpec((None, tr, tq), lambda i, j, k: (j // qd, k, j % qd)) if b_dev
                  else pl.BlockSpec((tr, tq), lambda i, j, k: (k, j)))
        dims = (((1,), (0,)), ((), ()))
    elif mode == "nt":
        a_spec = pl.BlockSpec((tp, tr), lambda i, j, k: (i, k))
        b_spec = (pl.BlockSpec((None, tq, tr), lambda i, j, k: (k // rd, j, k % rd)) if b_dev
                  else pl.BlockSpec((tq, tr), lambda i, j, k: (j, k)))
        dims = (((1,), (1,)), ((), ()))
    else:
        pb = p0 // tp
        a_spec = pl.BlockSpec((tr, tp), lambda i, j, k: (k, i + pb))
        b_spec = pl.BlockSpec((tr, tq), lambda i, j, k: (k, j))
        dims = (((0,), (0,)), ((), ()))
    ins, in_specs = [a, b], [a_spec, b_spec]
    if bias is not None:
        ins.append(bias)
        in_specs.append(pl.BlockSpec((1, tq), lambda i, j, k: (0, j)))
    gated = res is not None
    if gated:
        n_seg = gate.shape[0]
        ins += [res, gate]
        in_specs += [pl.BlockSpec((tp, tq), lambda i, j, k: (i, j)),
                     pl.BlockSpec((n_seg, 1, tq), lambda i, j, k: (0, 0, j))]
    if out_dev:
        out_shape = [SDS((N_DEV, P, Q // N_DEV), out_dtype)]
        out_specs = [pl.BlockSpec((None, tp, tq), lambda i, j, k: (j // qd, i, j % qd))]
    else:
        out_shape = [SDS((P, Q), out_dtype)]
        out_specs = [pl.BlockSpec((tp, tq), lambda i, j, k: (i, j))]
    if gated:
        out_shape.append(SDS((P, Q), BF16))
        out_specs.append(pl.BlockSpec((tp, tq), lambda i, j, k: (i, j)))

    def body(*refs):
        a_ref, b_ref = refs[0], refs[1]
        pos = 2
        bias_ref = res_ref = gate_ref = o2_ref = None
        if bias is not None:
            bias_ref = refs[pos]
            pos += 1
        if gated:
            res_ref, gate_ref = refs[pos], refs[pos + 1]
            pos += 2
        o_ref = refs[pos]
        pos += 1
        if gated:
            o2_ref = refs[pos]
            pos += 1
        acc_ref = refs[pos]
        k = pl.program_id(2)

        @pl.when(k == 0)
        def _():
            acc_ref[...] = jnp.zeros_like(acc_ref)

        av = a_ref[...]
        if a_silu:
            av = _silu(av.astype(F32))
        acc_ref[...] += lax.dot_general(av.astype(BF16), b_ref[...].astype(BF16), dims, preferred_element_type=F32)

        @pl.when(k == nk - 1)
        def _():
            acc = acc_ref[...]
            if bias_ref is not None:
                acc = acc + bias_ref[...]
            if gated:
                if n_seg == 1:
                    g = gate_ref[0]
                else:
                    row = pl.program_id(0) * tp + lax.broadcasted_iota(jnp.int32, (tp, 1), 0)
                    g = jnp.where(row < seg_t, gate_ref[0], gate_ref[1])
                o_ref[...] = (res_ref[...] + g * acc).astype(o_ref.dtype)
                o2_ref[...] = acc.astype(BF16)
            else:
                o_ref[...] = acc.astype(o_ref.dtype)

    out, carried = _carry_call(
        body, carry, grid=(P // tp, Q // tq, nk), out_shape=out_shape, in_specs=in_specs, out_specs=out_specs,
        scratch_shapes=[pltpu.VMEM((tp, tq), F32)], sem=("parallel", "parallel", "arbitrary"), name=name, ins=ins)
    res_out = tuple(out) if gated else out[0]
    return (res_out, carried) if carry else res_out


def _row_tile(seg_t, m):
    return 256 if (seg_t % 256 == 0 and m % 256 == 0) else 128


def _normmod_fwd(x, g, sh, sc, seg_t, name):
    M, D = x.shape
    tm = _row_tile(seg_t, M)
    n_seg = sh.shape[0]
    nt = seg_t // tm

    def seg(i):
        return ((i >= nt).astype(jnp.int32) if n_seg == 2 else 0, 0, 0)

    def body(x_ref, g_ref, sh_ref, sc_ref, o_ref):
        xv = x_ref[...]
        r = lax.rsqrt(jnp.mean(xv * xv, axis=-1, keepdims=True) + EPS)
        y = xv * r * g_ref[...]
        o_ref[...] = (y * (1.0 + sc_ref[0]) + sh_ref[0]).astype(BF16)

    return pl.pallas_call(
        body, out_shape=SDS((M, D), BF16), grid=(M // tm,),
        in_specs=[pl.BlockSpec((tm, D), lambda i: (i, 0)), pl.BlockSpec((1, D), lambda i: (0, 0)),
                  pl.BlockSpec((1, 1, D), seg), pl.BlockSpec((1, 1, D), seg)],
        out_specs=pl.BlockSpec((tm, D), lambda i: (i, 0)),
        compiler_params=_params(("parallel",)), name=name)(x, g, sh, sc)


def _normmod_bwd(x, g, sc, dh, dx_in, seg_t, name, o_prev=None, gate_prev=None):
    M, D = x.shape
    tm = _row_tile(seg_t, M)
    n_seg = sc.shape[0]
    nt = seg_t // tm
    n_in = dx_in.shape[0] // tm
    with_prev = o_prev is not None
    n_segp = gate_prev.shape[0] if with_prev else 0

    def seg(i):
        return ((i >= nt).astype(jnp.int32) if n_seg == 2 else 0, 0, 0)

    def segp(i):
        return ((i >= nt).astype(jnp.int32) if n_segp == 2 else 0, 0, 0)

    def body(*refs):
        x_ref, g_ref, sc_ref, dh_ref, dxin_ref = refs[:5]
        pos = 5
        if with_prev:
            op_ref, gp_ref = refs[5], refs[6]
            pos = 7
        dx_ref, dg_ref, dsh_ref, dsc_ref = refs[pos:pos + 4]
        if with_prev:
            dop_ref, dgp_ref = refs[pos + 4], refs[pos + 5]
        i = pl.program_id(0)
        xv = x_ref[...]
        r = lax.rsqrt(jnp.mean(xv * xv, axis=-1, keepdims=True) + EPS)
        xh = xv * r
        gv = g_ref[...]
        dhv = dh_ref[...].astype(F32)
        dy = dhv * (1.0 + sc_ref[0])
        dxh = dy * gv
        dxv = r * (dxh - xh * jnp.mean(dxh * xh, axis=-1, keepdims=True))
        if n_in * tm < M:
            dxv = dxv + jnp.where(i < n_in, dxin_ref[...], 0.0)
        else:
            dxv = dxv + dxin_ref[...]
        dx_ref[...] = dxv

        @pl.when(i == 0)
        def _():
            dg_ref[...] = jnp.zeros_like(dg_ref)

        first_of_seg = (i == 0) | (i == nt) if n_seg == 2 else (i == 0)

        @pl.when(first_of_seg)
        def _():
            dsh_ref[...] = jnp.zeros_like(dsh_ref)
            dsc_ref[...] = jnp.zeros_like(dsc_ref)

        dg_ref[...] += _colsum(dy * xh)
        dsh_ref[0] += _colsum(dhv)
        dsc_ref[0] += _colsum(dhv * xh * gv)
        if with_prev:
            first_of_segp = (i == 0) | (i == nt) if n_segp == 2 else (i == 0)

            @pl.when(first_of_segp)
            def _():
                dgp_ref[...] = jnp.zeros_like(dgp_ref)

            dop_ref[...] = (gp_ref[0] * dxv).astype(BF16)
            dgp_ref[0] += _colsum(dxv * op_ref[...].astype(F32))

    row = pl.BlockSpec((tm, D), lambda i: (i, 0))
    ins = [x, g, sc, dh, dx_in]
    in_specs = [row, pl.BlockSpec((1, D), lambda i: (0, 0)), pl.BlockSpec((1, 1, D), seg), row,
                pl.BlockSpec((tm, D), lambda i: (jnp.minimum(i, n_in - 1), 0))]
    out_shape = [SDS((M, D), F32), SDS((1, D), F32), SDS((n_seg, 1, D), F32), SDS((n_seg, 1, D), F32)]
    out_specs = [row, pl.BlockSpec((1, D), lambda i: (0, 0)), pl.BlockSpec((1, 1, D), seg), pl.BlockSpec((1, 1, D), seg)]
    if with_prev:
        ins += [o_prev, gate_prev]
        in_specs += [row, pl.BlockSpec((1, 1, D), segp)]
        out_shape += [SDS((M, D), BF16), SDS((n_segp, 1, D), F32)]
        out_specs += [row, pl.BlockSpec((1, 1, D), segp)]
    return pl.pallas_call(
        body, out_shape=out_shape, grid=(M // tm,), in_specs=in_specs, out_specs=out_specs,
        compiler_params=_params(("arbitrary",)), name=name)(*ins)


def _final(x, g, target, o_prev, gate_prev, name):
    T, D = x.shape
    tm = _tile(T, (256, 128))

    def body(x_ref, g_ref, t_ref, op_ref, gp_ref, dx_ref, loss_ref, dg_ref, dop_ref, dgp_ref):
        i = pl.program_id(0)
        xv = x_ref[...]
        r = lax.rsqrt(jnp.mean(xv * xv, axis=-1, keepdims=True) + EPS)
        xh = xv * r
        gv = g_ref[...]
        e = xh * gv - t_ref[...]
        dout = e * (1.0 / D)
        dxh = dout * gv
        dxv = r * (dxh - xh * jnp.mean(dxh * xh, axis=-1, keepdims=True))
        dx_ref[...] = dxv
        dop_ref[...] = (gp_ref[0] * dxv).astype(BF16)

        @pl.when(i == 0)
        def _():
            loss_ref[...] = jnp.zeros_like(loss_ref)
            dg_ref[...] = jnp.zeros_like(dg_ref)
            dgp_ref[...] = jnp.zeros_like(dgp_ref)

        loss_ref[...] += _colsum(e * e) * (0.5 / D)
        dg_ref[...] += _colsum(dout * xh)
        dgp_ref[0] += _colsum(dxv * op_ref[...].astype(F32))

    row = pl.BlockSpec((tm, D), lambda i: (i, 0))
    vec = pl.BlockSpec((1, D), lambda i: (0, 0))
    vec3 = pl.BlockSpec((1, 1, D), lambda i: (0, 0, 0))
    return pl.pallas_call(
        body, out_shape=[SDS((T, D), F32), SDS((1, D), F32), SDS((1, D), F32), SDS((T, D), BF16), SDS((1, 1, D), F32)],
        grid=(T // tm,), in_specs=[row, vec, row, row, vec3], out_specs=[row, vec, vec, row, vec3],
        compiler_params=_params(("arbitrary",)), name=name)(x, g, target, o_prev, gate_prev)


def _gmlp_core(z, lg, lb, ws_ref, bst):
    W = z.shape[1] // 2
    t = _gelu(z)
    u, v = t[:, :W], t[:, W:]
    mu = jnp.mean(v, axis=-1, keepdims=True)
    vc = v - mu
    rstd = lax.rsqrt(jnp.mean(vc * vc, axis=-1, keepdims=True) + EPS)
    vhat = vc * rstd
    vn = vhat * lg + lb
    vp = []
    for h in range(W // CHUNK):
        blk = vn[:, h * CHUNK:(h + 1) * CHUNK].astype(BF16)
        vp.append(jnp.dot(ws_ref[h].astype(BF16), blk, preferred_element_type=F32) + bst[:, h:h + 1])
    return u, vhat, rstd, vp


def _gmlp_fwd(z, ln_g, ln_b, w_s, b_st, W, name):
    M = z.shape[0]
    H = W // CHUNK

    def body(z_ref, lg_ref, lb_ref, ws_ref, bst_ref, o_ref):
        u, _, _, vp = _gmlp_core(z_ref[...], lg_ref[...], lb_ref[...], ws_ref, bst_ref[...])
        for h in range(H):
            o_ref[:, h * CHUNK:(h + 1) * CHUNK] = (u[:, h * CHUNK:(h + 1) * CHUNK] * vp[h]).astype(BF16)

    vec = pl.BlockSpec((1, W), lambda i: (0, 0))
    return pl.pallas_call(
        body, out_shape=SDS((M, 2 * W), BF16), grid=(M // CHUNK,),
        in_specs=[pl.BlockSpec((CHUNK, 2 * W), lambda i: (i, 0)), vec, vec,
                  pl.BlockSpec((H, CHUNK, CHUNK), lambda i: (0, 0, 0)), pl.BlockSpec((CHUNK, H), lambda i: (0, 0))],
        out_specs=pl.BlockSpec((CHUNK, W), lambda i: (i, 0)),
        compiler_params=_params(("parallel",)), name=name)(z, ln_g, ln_b, w_s, b_st)


def _gmlp_bwd(z, dy, ln_g, ln_b, w_s, b_st, W, name):
    M = z.shape[0]
    H = W // CHUNK
    ZW = z.shape[1]

    def body(z_ref, dy_ref, lg_ref, lb_ref, ws_ref, bst_ref, dz_ref, dlg_ref, dlb_ref, dws_ref, dbs_ref, dbin_ref):
        i = pl.program_id(0)

        @pl.when(i == 0)
        def _():
            for r in (dlg_ref, dlb_ref, dws_ref, dbs_ref, dbin_ref):
                r[...] = jnp.zeros_like(r)

        zv = z_ref[...]
        lg = lg_ref[...]
        u, vhat, rstd, vp = _gmlp_core(zv, lg, lb_ref[...], ws_ref, bst_ref[...])
        vn = vhat * lg + lb_ref[...]
        dya = dy_ref[...]
        du_parts, dvn_parts = [], []
        for h in range(H):
            sl = slice(h * CHUNK, (h + 1) * CHUNK)
            dya_h = dya[:, sl]
            du_parts.append(dya_h * vp[h])
            dvp = dya_h * u[:, sl]
            dbs_ref[h] += dvp
            dvp16 = dvp.astype(BF16)
            dws_ref[h] += lax.dot_general(dvp16, vn[:, sl].astype(BF16), (((1,), (1,)), ((), ())),
                                          preferred_element_type=F32)
            dvn_parts.append(lax.dot_general(ws_ref[h].astype(BF16), dvp16, (((0,), (0,)), ((), ())),
                                             preferred_element_type=F32))
        du = jnp.concatenate(du_parts, axis=1)
        dvn = jnp.concatenate(dvn_parts, axis=1)
        dlg_ref[...] += _colsum(dvn * vhat)
        dlb_ref[...] += _colsum(dvn)
        dvh = dvn * lg
        dv = rstd * (dvh - jnp.mean(dvh, axis=-1, keepdims=True) - vhat * jnp.mean(dvh * vhat, axis=-1, keepdims=True))
        dz = jnp.concatenate([du, dv], axis=1) * _dgelu(zv)
        dbin_ref[...] += _colsum(dz)
        dz_ref[...] = dz.astype(BF16)

    vec = pl.BlockSpec((1, W), lambda i: (0, 0))
    mat = pl.BlockSpec((H, CHUNK, CHUNK), lambda i: (0, 0, 0))
    return pl.pallas_call(
        body,
        out_shape=[SDS((M, ZW), BF16), SDS((1, W), F32), SDS((1, W), F32), SDS((H, CHUNK, CHUNK), F32),
                   SDS((H, CHUNK, CHUNK), F32), SDS((1, 2 * W), F32)],
        grid=(M // CHUNK,),
        in_specs=[pl.BlockSpec((CHUNK, 2 * W), lambda i: (i, 0)), pl.BlockSpec((CHUNK, W), lambda i: (i, 0)), vec, vec,
                  mat, pl.BlockSpec((CHUNK, H), lambda i: (0, 0))],
        out_specs=[pl.BlockSpec((CHUNK, 2 * W), lambda i: (i, 0)), vec, vec, mat, mat,
                   pl.BlockSpec((1, 2 * W), lambda i: (0, 0))],
        compiler_params=_params(("arbitrary",)), name=name)(z, dy, ln_g, ln_b, w_s, b_st)


def _halo_specs(tm, width, col, n_rows):
    per = tm // HALO
    last = n_rows // HALO - 1
    prev = pl.BlockSpec((HALO, width), lambda i: (jnp.maximum(i * per - 1, 0), col))
    nxt = pl.BlockSpec((HALO, width), lambda i: (jnp.minimum((i + 1) * per, last), col))
    return prev, nxt


def _edge_flags(i, tm, seg_t, m):
    r0 = i * tm
    has_prev = jnp.where((r0 == 0) | (r0 == seg_t), 0.0, 1.0)
    has_next = jnp.where((r0 + tm == seg_t) | (r0 + tm == m), 0.0, 1.0)
    return has_prev, has_next


def _glu(zz, wb):
    return zz[:, :wb] * _sigmoid(zz[:, wb:])


def _conv_taps(src_ref, w_ref, first, tm, kw, flip=False):
    rc = 32
    parts = []
    for c in range(tm // rc):
        acc = None
        for k in range(kw):
            wk = w_ref[pl.ds(kw - 1 - k if flip else k, 1), :]
            term = src_ref[pl.ds(first + c * rc + k, rc), :] * wk
            acc = term if acc is None else acc + term
        parts.append(acc)
    return jnp.concatenate(parts, axis=0)


def _conf_fwd(z, y, conv_w, conv_b, ln_g, ln_b, W, Wb, seg_t, name):
    M = z.shape[0]
    tm = _row_tile(seg_t, M)
    kw = conv_w.shape[0]
    pad = (kw - 1) // 2
    col = (2 * W) // (2 * Wb)

    def body(zc_ref, zp_ref, zn_ref, y_hbm, cw_ref, cb_ref, lg_ref, lb_ref, o_ref, hs_ref):
        del y_hbm
        hp, hn = _edge_flags(pl.program_id(0), tm, seg_t, M)
        hs_ref[pl.ds(0, HALO), :] = _glu(zp_ref[...], Wb) * hp
        hs_ref[pl.ds(HALO, tm), :] = _glu(zc_ref[...], Wb)
        hs_ref[pl.ds(HALO + tm, HALO), :] = _glu(zn_ref[...], Wb) * hn
        hc = _conv_taps(hs_ref, cw_ref, HALO - pad, tm, kw) + cb_ref[...]
        mu = jnp.mean(hc, axis=-1, keepdims=True)
        c = hc - mu
        rstd = lax.rsqrt(jnp.mean(c * c, axis=-1, keepdims=True) + EPS)
        o_ref[...] = _silu(c * rstd * lg_ref[...] + lb_ref[...]).astype(BF16)

    prev, nxt = _halo_specs(tm, 2 * Wb, col, M)
    vec = pl.BlockSpec((1, Wb), lambda i: (0, 0))
    return pl.pallas_call(
        body, out_shape=SDS(y.shape, BF16), grid=(M // tm,),
        in_specs=[pl.BlockSpec((tm, 2 * Wb), lambda i: (i, col)), prev, nxt, pl.BlockSpec(memory_space=pl.ANY),
                  pl.BlockSpec((kw, Wb), lambda i: (0, 0)), vec, vec, vec],
        out_specs=pl.BlockSpec((tm, Wb), lambda i: (i, W // Wb)),
        scratch_shapes=[pltpu.VMEM((tm + 2 * HALO, Wb), F32)],
        input_output_aliases={3: 0}, compiler_params=_params(("parallel",)), name=name)(
            z, z, z, y, conv_w, conv_b, ln_g, ln_b)


def _conf_bwd1(z, dy, conv_w, conv_b, ln_g, ln_b, W, Wb, seg_t, name):
    M = z.shape[0]
    tm = _row_tile(seg_t, M)
    kw = conv_w.shape[0]
    pad = (kw - 1) // 2
    col = (2 * W) // (2 * Wb)

    def body(zc_ref, zp_ref, zn_ref, dy_ref, cw_ref, cb_ref, lg_ref, lb_ref, dhc_ref, dlg_ref, dlb_ref, dcb_ref, hs_ref):
        i = pl.program_id(0)

        @pl.when(i == 0)
        def _():
            for r in (dlg_ref, dlb_ref, dcb_ref):
                r[...] = jnp.zeros_like(r)

        hp, hn = _edge_flags(i, tm, seg_t, M)
        hs_ref[pl.ds(0, HALO), :] = _glu(zp_ref[...], Wb) * hp
        hs_ref[pl.ds(HALO, tm), :] = _glu(zc_ref[...], Wb)
        hs_ref[pl.ds(HALO + tm, HALO), :] = _glu(zn_ref[...], Wb) * hn
        hc = _conv_taps(hs_ref, cw_ref, HALO - pad, tm, kw) + cb_ref[...]
        mu = jnp.mean(hc, axis=-1, keepdims=True)
        c = hc - mu
        rstd = lax.rsqrt(jnp.mean(c * c, axis=-1, keepdims=True) + EPS)
        hh = c * rstd
        lg = lg_ref[...]
        dhn = dy_ref[...] * _dsilu(hh * lg + lb_ref[...])
        dlg_ref[...] += _colsum(dhn * hh)
        dlb_ref[...] += _colsum(dhn)
        dhh = dhn * lg
        dhc = rstd * (dhh - jnp.mean(dhh, axis=-1, keepdims=True) - hh * jnp.mean(dhh * hh, axis=-1, keepdims=True))
        dcb_ref[...] += _colsum(dhc)
        dhc_ref[...] = dhc

    prev, nxt = _halo_specs(tm, 2 * Wb, col, M)
    vec = pl.BlockSpec((1, Wb), lambda i: (0, 0))
    return pl.pallas_call(
        body, out_shape=[SDS((M, Wb), F32), SDS((1, Wb), F32), SDS((1, Wb), F32), SDS((1, Wb), F32)], grid=(M // tm,),
        in_specs=[pl.BlockSpec((tm, 2 * Wb), lambda i: (i, col)), prev, nxt, pl.BlockSpec((tm, Wb), lambda i: (i, W // Wb)),
                  pl.BlockSpec((kw, Wb), lambda i: (0, 0)), vec, vec, vec],
        out_specs=[pl.BlockSpec((tm, Wb), lambda i: (i, 0)), vec, vec, vec],
        scratch_shapes=[pltpu.VMEM((tm + 2 * HALO, Wb), F32)],
        compiler_params=_params(("arbitrary",)), name=name)(z, z, z, dy, conv_w, conv_b, ln_g, ln_b)


def _conf_bwd2(z, dhc, dz, conv_w, W, Wb, seg_t, name, carry=None):
    M = z.shape[0]
    tm = _row_tile(seg_t, M)
    kw = conv_w.shape[0]
    pad = (kw - 1) // 2
    col = (2 * W) // (2 * Wb)

    def body(zc_ref, zp_ref, zn_ref, dc_ref, dp_ref, dn_ref, dz_hbm, cw_ref, dz_ref, dcw_ref, dbin_ref, hs_ref, ds_ref):
        del dz_hbm
        i = pl.program_id(0)

        @pl.when(i == 0)
        def _():
            dcw_ref[...] = jnp.zeros_like(dcw_ref)
            dbin_ref[...] = jnp.zeros_like(dbin_ref)

        hp, hn = _edge_flags(i, tm, seg_t, M)
        zc = zc_ref[...]
        hs_ref[pl.ds(0, HALO), :] = _glu(zp_ref[...], Wb) * hp
        hs_ref[pl.ds(HALO, tm), :] = _glu(zc, Wb)
        hs_ref[pl.ds(HALO + tm, HALO), :] = _glu(zn_ref[...], Wb) * hn
        dcur = dc_ref[...]
        ds_ref[pl.ds(0, HALO), :] = dp_ref[...] * hp
        ds_ref[pl.ds(HALO, tm), :] = dcur
        ds_ref[pl.ds(HALO + tm, HALO), :] = dn_ref[...] * hn
        dh = _conv_taps(ds_ref, cw_ref, HALO - pad, tm, kw, flip=True)
        for k in range(kw):
            dcw_ref[pl.ds(k, 1), :] += _colsum(dcur * hs_ref[pl.ds(HALO - pad + k, tm), :])
        a, gt = zc[:, :Wb], zc[:, Wb:]
        s = _sigmoid(gt)
        dz = jnp.concatenate([dh * s, dh * a * s * (1.0 - s)], axis=1)
        dbin_ref[...] += _colsum(dz)
        dz_ref[...] = dz.astype(BF16)

    prev, nxt = _halo_specs(tm, 2 * Wb, col, M)
    dprev, dnxt = _halo_specs(tm, Wb, 0, M)
    out, carried = _carry_call(
        body, carry, out_shape=[SDS(dz.shape, BF16), SDS((kw, Wb), F32), SDS((1, 2 * Wb), F32)], grid=(M // tm,),
        in_specs=[pl.BlockSpec((tm, 2 * Wb), lambda i: (i, col)), prev, nxt,
                  pl.BlockSpec((tm, Wb), lambda i: (i, 0)), dprev, dnxt, pl.BlockSpec(memory_space=pl.ANY),
                  pl.BlockSpec((kw, Wb), lambda i: (0, 0))],
        out_specs=[pl.BlockSpec((tm, 2 * Wb), lambda i: (i, col)), pl.BlockSpec((kw, Wb), lambda i: (0, 0)),
                   pl.BlockSpec((1, 2 * Wb), lambda i: (0, 0))],
        scratch_shapes=[pltpu.VMEM((tm + 2 * HALO, Wb), F32), pltpu.VMEM((tm + 2 * HALO, Wb), F32)],
        aliases={6: 0}, sem=("arbitrary",), name=name, ins=(z, z, z, dhc, dhc, dhc, dz, conv_w))
    return (out, carried) if carry else out


_TF = (1408, 512, 256, 128)
_RC = 16
_CG = 256


def _col_groups(width):
    return [(c0, min(_CG, width - c0)) for c0 in range(0, width, _CG)]


def _ffn_act_fwd(z, conv_w, conv_b, seg_t, name):
    M, F2 = z.shape
    Fd = F2 // 2
    tm = _row_tile(seg_t, M)
    tf = _tile(Fd, _TF)
    nf = Fd // tf
    per, last = tm // HALO, M // HALO - 1

    def body(g_ref, gp_ref, gn_ref, u_ref, cw_ref, cb_ref, o_ref, gs_ref):
        hp, hn = _edge_flags(pl.program_id(0), tm, seg_t, M)
        gs_ref[pl.ds(0, HALO), :] = gp_ref[...].astype(F32) * hp
        gs_ref[pl.ds(HALO, tm), :] = g_ref[...].astype(F32)
        gs_ref[pl.ds(HALO + tm, HALO), :] = gn_ref[...].astype(F32) * hn
        for c0, cw in _col_groups(tf):
            cs = pl.ds(c0, cw)
            w0, w1, w2, cb = cw_ref[pl.ds(0, 1), cs], cw_ref[pl.ds(1, 1), cs], cw_ref[pl.ds(2, 1), cs], cb_ref[:, cs]
            for r0 in range(0, tm, _RC):
                gc = (gs_ref[pl.ds(HALO - 1 + r0, _RC), cs] * w0 + gs_ref[pl.ds(HALO + r0, _RC), cs] * w1
                      + gs_ref[pl.ds(HALO + 1 + r0, _RC), cs] * w2 + cb)
                o_ref[pl.ds(r0, _RC), cs] = (_silu(gc) * u_ref[pl.ds(r0, _RC), cs].astype(F32)).astype(BF16)

    return pl.pallas_call(
        body, out_shape=SDS((M, Fd), BF16), grid=(M // tm, nf),
        in_specs=[pl.BlockSpec((tm, tf), lambda i, j: (i, j)),
                  pl.BlockSpec((HALO, tf), lambda i, j: (jnp.maximum(i * per - 1, 0), j)),
                  pl.BlockSpec((HALO, tf), lambda i, j: (jnp.minimum((i + 1) * per, last), j)),
                  pl.BlockSpec((tm, tf), lambda i, j: (i, nf + j)),
                  pl.BlockSpec((3, tf), lambda i, j: (0, j)), pl.BlockSpec((1, tf), lambda i, j: (0, j))],
        out_specs=pl.BlockSpec((tm, tf), lambda i, j: (i, j)),
        scratch_shapes=[pltpu.VMEM((tm + 2 * HALO, tf), F32)],
        compiler_params=_params(("parallel", "parallel")), name=name)(z, z, z, z, conv_w, conv_b)


def _ffn_act_bwd(z, da, conv_w, conv_b, seg_t, name):
    M, F2 = z.shape
    Fd = F2 // 2
    tm = _row_tile(seg_t, M)
    tf = _tile(Fd, _TF)
    nf = Fd // tf
    per, last = tm // HALO, M // HALO - 1
    PAD = 8
    n_piece = tm // _RC

    def body(g_ref, gp_ref, gn_ref, u_ref, up_ref, un_ref, a_ref, ap_ref, an_ref, cw_ref, cb_ref,
             dz_ref, dcw_ref, dcb_ref, gs_ref, ds_ref, du_ref):
        i, p = pl.program_id(1), pl.program_id(2)

        @pl.when(p == 0)
        def _():
            hp, hn = _edge_flags(i, tm, seg_t, M)
            gs_ref[pl.ds(0, PAD), :] = jnp.zeros((PAD, tf), F32)
            gs_ref[pl.ds(PAD, HALO), :] = gp_ref[...].astype(F32) * hp
            gs_ref[pl.ds(PAD + HALO, tm), :] = g_ref[...].astype(F32)
            gs_ref[pl.ds(PAD + HALO + tm, HALO), :] = gn_ref[...].astype(F32) * hn
            gs_ref[pl.ds(PAD + 2 * HALO + tm, PAD), :] = jnp.zeros((PAD, tf), F32)

            @pl.when(i == 0)
            def _():
                dcw_ref[...] = jnp.zeros_like(dcw_ref)
                dcb_ref[...] = jnp.zeros_like(dcb_ref)

            def fold(v):
                return v[:8] + v[8:]

            for c0, cw in _col_groups(tf):
                cs = pl.ds(c0, cw)
                w0, w1, w2, cb = cw_ref[pl.ds(0, 1), cs], cw_ref[pl.ds(1, 1), cs], cw_ref[pl.ds(2, 1), cs], cb_ref[:, cs]
                acc = [jnp.zeros((8, cw), F32) for _ in range(4)]
                for ci in range(-1, n_piece + 1):
                    r0 = ci * _RC
                    taps = [gs_ref[pl.ds(PAD + HALO + r0 - 1 + k, _RC), cs] for k in range(3)]
                    gc = taps[0] * w0 + taps[1] * w1 + taps[2] * w2 + cb
                    if ci < 0:
                        ue, ae = up_ref[:, cs].astype(F32), ap_ref[:, cs].astype(F32) * hp
                    elif ci == n_piece:
                        ue, ae = un_ref[:, cs].astype(F32), an_ref[:, cs].astype(F32) * hn
                    else:
                        ue, ae = u_ref[pl.ds(r0, _RC), cs].astype(F32), a_ref[pl.ds(r0, _RC), cs].astype(F32)
                    sg = _sigmoid(gc)
                    dgc = ae * ue * (sg * (1.0 + gc * (1.0 - sg)))
                    ds_ref[pl.ds(HALO + r0, _RC), cs] = dgc
                    if 0 <= ci < n_piece:
                        du_ref[pl.ds(r0, _RC), cs] = (ae * gc * sg).astype(BF16)
                        for k in range(3):
                            acc[k] = acc[k] + fold(dgc * taps[k])
                        acc[3] = acc[3] + fold(dgc)
                for r0 in range(0, tm, _RC):
                    b = HALO + r0
                    dg = (ds_ref[pl.ds(b + 1, _RC), cs] * w0 + ds_ref[pl.ds(b, _RC), cs] * w1
                          + ds_ref[pl.ds(b - 1, _RC), cs] * w2)
                    dz_ref[pl.ds(r0, _RC), cs] = dg.astype(BF16)
                for k in range(3):
                    dcw_ref[pl.ds(k, 1), cs] += _colsum(acc[k])
                dcb_ref[:, cs] += _colsum(acc[3])

        @pl.when(p == 1)
        def _():
            dz_ref[...] = du_ref[...]

    def cur(off):
        return pl.BlockSpec((tm, tf), lambda j, i, p: (i, off + j))

    def prv(off):
        return pl.BlockSpec((HALO, tf), lambda j, i, p: (jnp.maximum(i * per - 1, 0), off + j))

    def nxt(off):
        return pl.BlockSpec((HALO, tf), lambda j, i, p: (jnp.minimum((i + 1) * per, last), off + j))

    return pl.pallas_call(
        body, out_shape=[SDS((M, F2), BF16), SDS((3, Fd), F32), SDS((1, Fd), F32)], grid=(nf, M // tm, 2),
        in_specs=[cur(0), prv(0), nxt(0), cur(nf), prv(nf), nxt(nf), cur(0), prv(0), nxt(0),
                  pl.BlockSpec((3, tf), lambda j, i, p: (0, j)), pl.BlockSpec((1, tf), lambda j, i, p: (0, j))],
        out_specs=[pl.BlockSpec((tm, tf), lambda j, i, p: (i, p * nf + j)),
                   pl.BlockSpec((3, tf), lambda j, i, p: (0, j)), pl.BlockSpec((1, tf), lambda j, i, p: (0, j))],
        scratch_shapes=[pltpu.VMEM((tm + 2 * HALO + 2 * PAD, tf), F32), pltpu.VMEM((tm + 2 * HALO, tf), F32),
                        pltpu.VMEM((tm, tf), BF16)],
        compiler_params=_params(("parallel", "arbitrary", "arbitrary")), name=name)(
            z, z, z, z, z, z, da, da, da, conv_w, conv_b)


_LN2 = math.log(2.0)
_QSCALE = (NOPE + ROPE) ** -0.5 / _LN2


def _swap32(x):
    lane = lax.broadcasted_iota(jnp.int32, x.shape, 1)
    return jnp.where((lane % 64) < 32, pltpu.roll(x, 96, axis=1), pltpu.roll(x, 32, axis=1))


def _rms(x, g):
    r = lax.rsqrt(jnp.mean(x * x, axis=-1, keepdims=True) + EPS)
    return x * r * g


def _rms_bwd(x, g, dy):
    r = lax.rsqrt(jnp.mean(x * x, axis=-1, keepdims=True) + EPS)
    xh = x * r
    dxh = dy * g
    return r * (dxh - xh * jnp.mean(dxh * xh, axis=-1, keepdims=True)), _colsum(dy * xh)


def _mla_prep_fwd(z, gq, gkv, cos, sin, QL, KL, name):
    M, NZ = z.shape
    tm = _tile(M, (256, 128))

    def body(z_ref, gq_ref, gkv_ref, cos_ref, sin_ref, cq_ref, ckv_ref, kpe_ref):
        zv = z_ref[...]
        cq_ref[...] = _rms(zv[:, :QL], gq_ref[...]).astype(BF16)
        ckv_ref[...] = _rms(zv[:, QL:QL + KL], gkv_ref[...]).astype(BF16)
        kp = zv[:, QL + KL:]
        r = kp * cos_ref[...] + _swap32(kp) * sin_ref[...]
        lane = lax.broadcasted_iota(jnp.int32, r.shape, 1)
        kpe_ref[0] = jnp.where(lane < ROPE, r, 0.0).astype(BF16)
        kpe_ref[1] = jnp.where(lane >= ROPE, r, 0.0).astype(BF16)

    tab = pl.BlockSpec((tm, 128), lambda i: (i, 0))
    return pl.pallas_call(
        body, out_shape=[SDS((M, QL), BF16), SDS((M, KL), BF16), SDS((2, M, 128), BF16)], grid=(M // tm,),
        in_specs=[pl.BlockSpec((tm, NZ), lambda i: (i, 0)), pl.BlockSpec((1, QL), lambda i: (0, 0)),
                  pl.BlockSpec((1, KL), lambda i: (0, 0)), tab, tab],
        out_specs=[pl.BlockSpec((tm, QL), lambda i: (i, 0)), pl.BlockSpec((tm, KL), lambda i: (i, 0)),
                   pl.BlockSpec((2, tm, 128), lambda i: (0, i, 0))],
        compiler_params=_params(("parallel",)), name=name)(z, gq, gkv, cos, sin)


def _mla_prep_bwd(z, dcq, dckv, dkpe, gq, gkv, cos, sin, QL, KL, seg_t, name):
    M, NZ = z.shape
    H = dkpe.shape[0]
    tm = _row_tile(seg_t, M)
    nt = seg_t // tm

    def body(z_ref, dcq_ref, dckv_ref, dkpe_ref, gq_ref, gkv_ref, cos_ref, sin_ref, dz_ref, dgq_ref, dgkv_ref):
        i = pl.program_id(0)

        @pl.when(i == 0)
        def _():
            dgq_ref[...] = jnp.zeros_like(dgq_ref)
            dgkv_ref[...] = jnp.zeros_like(dgkv_ref)

        zv = z_ref[...]
        dyq = jnp.where(i < nt, dcq_ref[...], 0.0)
        dxq, dgq = _rms_bwd(zv[:, :QL], gq_ref[...], dyq)
        dxkv, dgkv = _rms_bwd(zv[:, QL:QL + KL], gkv_ref[...], dckv_ref[...])
        dgq_ref[...] += dgq
        dgkv_ref[...] += dgkv
        even = dkpe_ref[0]
        odd = dkpe_ref[1]
        for h in range(2, H, 2):
            even = even + dkpe_ref[h]
            odd = odd + dkpe_ref[h + 1]
        lane = lax.broadcasted_iota(jnp.int32, even.shape, 1)
        dr = jnp.where(lane < ROPE, even, odd)
        dkp = dr * cos_ref[...] - _swap32(dr) * sin_ref[...]
        dz_ref[...] = jnp.concatenate([dxq, dxkv, dkp], axis=1).astype(BF16)

    tab = pl.BlockSpec((tm, 128), lambda i: (i, 0))
    return pl.pallas_call(
        body, out_shape=[SDS((M, NZ), BF16), SDS((1, QL), F32), SDS((1, KL), F32)], grid=(M // tm,),
        in_specs=[pl.BlockSpec((tm, NZ), lambda i: (i, 0)),
                  pl.BlockSpec((tm, QL), lambda i: (jnp.minimum(i, nt - 1), 0)),
                  pl.BlockSpec((tm, KL), lambda i: (i, 0)), pl.BlockSpec((H, tm, 128), lambda i: (0, i, 0)),
                  pl.BlockSpec((1, QL), lambda i: (0, 0)), pl.BlockSpec((1, KL), lambda i: (0, 0)), tab, tab],
        out_specs=[pl.BlockSpec((tm, NZ), lambda i: (i, 0)), pl.BlockSpec((1, QL), lambda i: (0, 0)),
                   pl.BlockSpec((1, KL), lambda i: (0, 0))],
        compiler_params=_params(("arbitrary",)), name=name)(z, dcq, dckv, dkpe, gq, gkv, cos, sin)


def _qrope_fwd(q, cos, sin, HN, name):
    T, NQ = q.shape
    tm = _tile(T, (256, 128))

    def body(q_ref, cos_ref, sin_ref, o_ref):
        o_ref[:, :HN] = (q_ref[:, :HN] * _QSCALE).astype(BF16)
        for cb in range((NQ - HN) // 128):
            sl = slice(HN + cb * 128, HN + (cb + 1) * 128)
            xv = q_ref[:, sl]
            o_ref[:, sl] = ((xv * cos_ref[...] + _swap32(xv) * sin_ref[...]) * _QSCALE).astype(BF16)

    tab = pl.BlockSpec((tm, 128), lambda i: (i, 0))
    return pl.pallas_call(
        body, out_shape=SDS((T, NQ), BF16), grid=(T // tm,),
        in_specs=[pl.BlockSpec((tm, NQ), lambda i: (i, 0)), tab, tab],
        out_specs=pl.BlockSpec((tm, NQ), lambda i: (i, 0)),
        compiler_params=_params(("parallel",)), name=name)(q, cos, sin)


def _qrope_bwd(dqpe, dqa, cos, sin, HN, name):
    T, HW = dqpe.shape
    HR = HW // 2
    tm = _tile(T, (256, 128))

    def body(d_ref, dqa_hbm, cos_ref, sin_ref, o_ref):
        del dqa_hbm
        for pr in range(HR // 128):
            dr = d_ref[:, 2 * pr * 128:(2 * pr + 1) * 128] + d_ref[:, (2 * pr + 1) * 128:(2 * pr + 2) * 128]
            o_ref[:, pr * 128:(pr + 1) * 128] = (dr * cos_ref[...] - _swap32(dr) * sin_ref[...]).astype(BF16)

    tab = pl.BlockSpec((tm, 128), lambda i: (i, 0))
    return pl.pallas_call(
        body, out_shape=SDS(dqa.shape, BF16), grid=(T // tm,),
        in_specs=[pl.BlockSpec((tm, HW), lambda i: (i, 0)), pl.BlockSpec(memory_space=pl.ANY), tab, tab],
        out_specs=pl.BlockSpec((tm, HR), lambda i: (i, HN // HR)),
        input_output_aliases={1: 0}, compiler_params=_params(("parallel",)), name=name)(dqpe, dqa, cos, sin)


_ATT_SUB = 4
_ATT_SUB_B = 2
_NT = (((1,), (1,)), ((), ()))
_TN = (((0,), (0,)), ((), ()))


def _attn_fwd(qa, kv, kpe, T, H, name, carry=None):
    M = kv.shape[0]
    tq = _tile(T, (1024, 512, 256, 128))
    scale = (NOPE + ROPE) ** -0.5

    def body(qn_ref, qp_ref, kv_ref, kpe_ref, o_ref, lse_ref, kc_ref):
        @pl.when(pl.program_id(1) == 0)
        def _():
            kc_ref[:, :NOPE] = kv_ref[:, :NOPE]
            kc_ref[:, NOPE:] = kpe_ref[0]

        rs = tq // _ATT_SUB
        outs, lses = [], []
        for u in range(_ATT_SUB):
            rows = pl.ds(u * rs, rs)
            qc = jnp.concatenate([qn_ref[rows, :], qp_ref[rows, :]], axis=1)
            s = lax.dot_general(qc, kc_ref[...], _NT, preferred_element_type=F32)
            m = jnp.max(s, axis=-1, keepdims=True)
            p = jnp.exp2(s - m)
            l = jnp.sum(p, axis=-1, keepdims=True)
            o = jnp.dot(p.astype(BF16), kv_ref[:, NOPE:], preferred_element_type=F32)
            outs.append((o / l).astype(BF16))
            lses.append(jnp.broadcast_to(m + jnp.log2(l), (rs, 128)))
        o_ref[...] = jnp.concatenate(outs, axis=0)
        lse_ref[...] = jnp.concatenate(lses, axis=0)

    return _carry_call(
        body, carry, out_shape=[SDS((T, H * VDIM), BF16), SDS((T, H * 128), F32)], grid=(H, T // tq),
        in_specs=[pl.BlockSpec((tq, NOPE), lambda h, i: (i, h)), pl.BlockSpec((tq, 128), lambda h, i: (i, H + h // 2)),
                  pl.BlockSpec((M, NOPE + VDIM), lambda h, i: (0, h)), pl.BlockSpec((1, M, 128), lambda h, i: (h % 2, 0, 0))],
        out_specs=[pl.BlockSpec((tq, VDIM), lambda h, i: (i, h)), pl.BlockSpec((tq, 128), lambda h, i: (i, h))],
        scratch_shapes=[pltpu.VMEM((M, NOPE + 128), BF16)],
        sem=("parallel", "arbitrary"), name=name, ins=(qa, qa, kv, kpe))


def _attn_bwd(qa, kv, kpe, do, lse, T, H, name, carry=None):
    M = kv.shape[0]
    tq = _tile(T, (512, 256, 128))
    nq = T // tq
    scale = (NOPE + ROPE) ** -0.5

    def body(qn_ref, qp_ref, kv_ref, kpe_ref, do_ref, lse_ref, dqa_ref, dqpe_ref, dkv_ref, dkpe_ref, kc_ref, dk_acc, dv_acc):
        i = pl.program_id(1)

        @pl.when(i == 0)
        def _():
            kc_ref[:, :NOPE] = kv_ref[:, :NOPE]
            kc_ref[:, NOPE:] = kpe_ref[0]
            dk_acc[...] = jnp.zeros_like(dk_acc)
            dv_acc[...] = jnp.zeros_like(dv_acc)

        rs = tq // _ATT_SUB_B
        dqs, dvs, dks = [], None, None
        for u in range(_ATT_SUB_B):
            rows = pl.ds(u * rs, rs)
            qc = jnp.concatenate([qn_ref[rows, :], qp_ref[rows, :]], axis=1)
            dov = do_ref[rows, :]
            s = lax.dot_general(qc, kc_ref[...], _NT, preferred_element_type=F32)
            p = jnp.exp2(s - lse_ref[rows, 0:1])
            dp = lax.dot_general(dov, kv_ref[:, NOPE:], _NT, preferred_element_type=F32)
            delta = jnp.sum(p * dp, axis=-1, keepdims=True)
            ds = (p * (dp - delta)).astype(BF16)
            dqs.append(jnp.dot(ds, kc_ref[...], preferred_element_type=F32) * scale)
            dv = lax.dot_general(p.astype(BF16), dov, _TN, preferred_element_type=F32)
            dk = lax.dot_general(ds, qc, _TN, preferred_element_type=F32)
            dvs = dv if dvs is None else dvs + dv
            dks = dk if dks is None else dks + dk
        dq = jnp.concatenate(dqs, axis=0)
        dqa_ref[...] = dq[:, :NOPE].astype(BF16)
        dqpe_ref[...] = dq[:, NOPE:]
        dv_acc[...] += dvs
        dk_acc[...] += dks

        @pl.when(i == nq - 1)
        def _():
            dkv_ref[:, :NOPE] = (dk_acc[:, :NOPE] * _LN2).astype(BF16)
            dkv_ref[:, NOPE:] = dv_acc[...].astype(BF16)
            dkpe_ref[0] = dk_acc[:, NOPE:] * _LN2

    return _carry_call(
        body, carry,
        out_shape=[SDS((T, H * (NOPE + ROPE)), BF16), SDS((T, H * 128), F32), SDS((M, H * (NOPE + VDIM)), BF16),
                   SDS((H, M, 128), F32)],
        grid=(H, nq),
        in_specs=[pl.BlockSpec((tq, NOPE), lambda h, i: (i, h)), pl.BlockSpec((tq, 128), lambda h, i: (i, H + h // 2)),
                  pl.BlockSpec((M, NOPE + VDIM), lambda h, i: (0, h)), pl.BlockSpec((1, M, 128), lambda h, i: (h % 2, 0, 0)),
                  pl.BlockSpec((tq, VDIM), lambda h, i: (i, h)), pl.BlockSpec((tq, 128), lambda h, i: (i, h))],
        out_specs=[pl.BlockSpec((tq, NOPE), lambda h, i: (i, h)), pl.BlockSpec((tq, 128), lambda h, i: (i, h)),
                   pl.BlockSpec((M, NOPE + VDIM), lambda h, i: (0, h)), pl.BlockSpec((1, M, 128), lambda h, i: (h, 0, 0))],
        scratch_shapes=[pltpu.VMEM((M, NOPE + 128), BF16), pltpu.VMEM((M, NOPE + 128), F32), pltpu.VMEM((M, VDIM), F32)],
        sem=("parallel", "arbitrary"), name=name, ins=(qa, qa, kv, kpe, do, lse))


def _adamw(w, m, v, name, g=None, recv=None, carry=None):
    R, C = w.shape
    summed = recv is not None
    n_recv = len(recv) if summed else 1
    runs = [r.shape[1] for r in recv] if summed else [R]
    tr = math.gcd(*runs)
    for cand in (1024, 512, 256, 128, 64, 32, 16, 8):
        if tr % cand == 0 and cand * C <= 131072:
            tr = cand
            break
    first = [sum(runs[:r]) // tr for r in range(n_recv + 1)]
    c1 = 1.0 - ADAM_B1 ** ADAM_STEP
    c2 = 1.0 - ADAM_B2 ** ADAM_STEP

    def update(gv, w_ref, m_ref, v_ref, d_ref, nm_ref, nv_ref):
        mn = ADAM_B1 * m_ref[...] + (1.0 - ADAM_B1) * gv
        vn = ADAM_B2 * v_ref[...] + (1.0 - ADAM_B2) * (gv * gv)
        nm_ref[...] = mn
        nv_ref[...] = vn
        d_ref[...] = -ADAM_LR * ((mn / c1) / (jnp.sqrt(vn / c2) + ADAM_EPS) + ADAM_WD * w_ref[...])

    def body(*refs):
        w_ref, m_ref, v_ref = refs[:3]
        g_refs = refs[3:3 + n_recv]
        outs = refs[3 + n_recv:]
        if not summed:
            update(g_refs[0][...], w_ref, m_ref, v_ref, *outs)
            return
        i = pl.program_id(0)
        for r in range(n_recv):
            @pl.when((i >= first[r]) & (i < first[r + 1]))
            def _():
                gv = g_refs[r][0].astype(F32)
                for d in range(1, N_DEV):
                    gv = gv + g_refs[r][d].astype(F32)
                outs[0][...] = gv
                update(gv, w_ref, m_ref, v_ref, *outs[1:])

    blk = pl.BlockSpec((tr, C), lambda i: (i, 0))
    if summed:
        g_specs = [pl.BlockSpec((N_DEV, tr, C), functools.partial(
            lambda i, lo, n: (0, jnp.clip(i - lo, 0, n - 1), 0), lo=first[r], n=first[r + 1] - first[r]))
                   for r in range(n_recv)]
    else:
        g_specs = [blk]
    n_out = 4 if summed else 3
    out, carried = _carry_call(
        body, carry, out_shape=[SDS((R, C), F32)] * n_out, grid=(R // tr,), in_specs=[blk, blk, blk] + g_specs,
        out_specs=[blk] * n_out, scratch_shapes=[], sem=("parallel",), name=name,
        ins=(w, m, v, *(recv if summed else [g])))
    return (out, carried) if carry else out


WEIGHTS = ['c_ctx', 'norm1_g', 'norm2_g', 'w_ada', 'b_ada', 'ab_w_in', 'ab_b_in', 'a_ln_g', 'a_ln_b', 'a_w_s', 'a_b_s',
           'b_conv_w', 'b_conv_b', 'b_ln_g', 'b_ln_b', 'ab_w_out', 'mla_w_in', 'mla_q_norm_g', 'mla_w_uq',
           'mla_kv_norm_g', 'mla_w_ukv', 'mla_w_o', 'ffn_w_up', 'ffn_conv_w', 'ffn_conv_b', 'ffn_w_down', 'final_norm_g']


def _pack(parts):
    flat = jnp.concatenate([p.reshape(-1).astype(F32) for p in parts])
    n = flat.shape[0]
    unit = 65536 if n > 65536 else 1024
    n_pad = -(-n // unit) * unit
    return jnp.pad(flat, (0, n_pad - n)).reshape(n_pad // 128, 128)


def _unpack(flat, like):
    out, off = [], 0
    for shp in like:
        n = math.prod(shp)
        out.append(flat[..., off:off + n].reshape(flat.shape[:-1] + tuple(shp)))
        off += n
    return out


def _rope_tables(T, Tc):
    rows = T // GRID_W
    row = jnp.repeat(jnp.arange(rows, dtype=F32), GRID_W)
    col = jnp.tile(jnp.arange(GRID_W, dtype=F32), rows)
    n_freq = ROPE // 4
    inv = ROPE_THETA ** (-jnp.arange(n_freq, dtype=F32) / n_freq)
    ang = jnp.concatenate([row[:, None] * inv, col[:, None] * inv], axis=-1)
    cos, sin = jnp.cos(ang), jnp.sin(ang)
    cos = jnp.tile(cos, (1, 128 // (ROPE // 2)))
    sin = jnp.tile(jnp.concatenate([-sin, sin], axis=1), (1, 128 // ROPE))
    return (jnp.concatenate([cos, jnp.ones((Tc, 128), F32)], axis=0),
            jnp.concatenate([sin, jnp.zeros((Tc, 128), F32)], axis=0))


def _step(a):
    ax, ay, ac = lax.axis_index("x"), lax.axis_index("y"), lax.axis_index("c")
    me = 4 * ax + 2 * ay + ac
    T, D = a['x'].shape[1:]
    Tc = a['ctx'].shape[1]
    M = T + Tc
    W, Wb = a['a_ln_g'].shape[1], a['b_ln_g'].shape[1]
    assert W == Wb and T % Tc == 0
    Fd = a['ffn_conv_b'].shape[1]
    QL, KL = a['mla_q_norm_g'].shape[1] * N_DEV, a['mla_kv_norm_g'].shape[1] * N_DEV
    H = a['mla_w_ukv'].shape[2] * N_DEV // (NOPE + VDIM)
    HN, HR = H * NOPE, H * ROPE
    kw = a['b_conv_w'].shape[1]
    NA = a['w_ada'].shape[2]
    bf = lambda t: t.astype(BF16)

    small_shapes = [(D,), (kw, Wb // N_DEV), (2, 3, Fd // N_DEV), (QL // N_DEV,), (KL // N_DEV,)]
    g_small = _all_gather(_pack([a['c'][0], a['b_conv_w'][0], a['ffn_conv_w'], a['mla_q_norm_g'][0], a['mla_kv_norm_g'][0]]),
                          "ag_small")
    c_all, bcw, fcw, gq, gkv = _unpack(g_small.reshape(N_DEV, -1), small_shapes)
    bcw = jnp.transpose(bcw, (1, 0, 2)).reshape(kw, Wb)
    fcw = jnp.transpose(fcw, (1, 2, 0, 3)).reshape(2, 3, Fd)
    gq, gkv = gq.reshape(1, QL), gkv.reshape(1, KL)

    a16 = jnp.concatenate([c_all, a['c_ctx'][None], jnp.zeros((N_DEV - 1, D), F32)], axis=0)
    b_loc = lax.dynamic_slice(a['b_ada'], (0, me * NA), (2, NA))
    mods = [_mm(a16, a['w_ada'][l], mode="nn", out_dtype=F32, name=f"ada_fwd{l}", bias=b_loc[l:l + 1], a_silu=True)
            for l in range(2)]
    gm = _all_gather(jnp.concatenate(mods, axis=0), "ag_mod").reshape(N_DEV, 2, 2 * N_DEV, NA)
    gm = jnp.transpose(gm, (1, 2, 0, 3)).reshape(2, 2 * N_DEV, 6 * D)
    mod_lat = [lax.dynamic_slice(gm[l], (me, 0), (1, 6 * D)).reshape(6, 1, 1, D) for l in range(2)]
    mod_ctx = [gm[l][N_DEV].reshape(6, 1, 1, D) for l in range(2)]

    def mod(l, k, both):
        return jnp.concatenate([mod_lat[l][k], mod_ctx[l][k]], axis=0) if both else mod_lat[l][k]

    def from_cols(g):
        return jnp.transpose(g, (1, 0, 2)).reshape(g.shape[1], -1)

    def from_rows(g):
        return g.reshape(-1, g.shape[2])

    def ag(x):
        return (x, False)

    def a2a(x):
        return (x, True)

    cos, sin = _rope_tables(T, Tc)
    n1g, n2g = a['norm1_g'], a['norm2_g']
    a_bst = a['a_b_s'][0].T
    mm = functools.partial(_mm)
    up_sh, dn_sh = bf(a['ffn_w_up']), bf(a['ffn_w_down'])

    w_abin = _all_gather(bf(a['ab_w_in'][0]), "ag_ab_w_in")
    x0 = jnp.concatenate([a['x'][0], a['ctx'][0]], axis=0)
    h1 = _normmod_fwd(x0, n1g[0:1], mod(0, 0, True), mod(0, 1, True), T, "l0_norm1")
    z, (g_about,) = mm(h1, w_abin, mode="nn", out_dtype=F32, name="l0_ab_in", bias=a['ab_b_in'], b_dev=True,
                       carry=[ag(bf(a['ab_w_out'][0]))])
    w_about = from_rows(g_about)
    y = _gmlp_fwd(z, a['a_ln_g'], a['a_ln_b'], a['a_w_s'][0], a_bst, W, "l0_gmlp")
    y = _conf_fwd(z, y, bcw, a['b_conv_b'], a['b_ln_g'], a['b_ln_b'], W, Wb, T, "l0_conf")
    (x1, o1), (g_up0,) = mm(y, w_about, mode="nn", out_dtype=F32, name="l0_ab_out", res=x0, gate=mod(0, 2, True),
                            seg_t=T, carry=[ag(up_sh[0])])
    w_up = [g_up0, None]
    h2 = _normmod_fwd(x1, n2g[0:1], mod(0, 3, True), mod(0, 4, True), T, "l0_norm2")
    z2, (g_dn0,) = mm(h2, w_up[0], mode="nn", out_dtype=BF16, name="l0_up", b_dev=True, carry=[ag(dn_sh[0])])
    w_dn = [from_rows(g_dn0), None]
    a2 = _ffn_act_fwd(z2, fcw[0], a['ffn_conv_b'][0:1], T, "l0_act")
    (x2, o2), (g_in, g_uq) = mm(a2, w_dn[0], mode="nn", out_dtype=F32, name="l0_down", res=x1, gate=mod(0, 5, True),
                                seg_t=T, carry=[ag(bf(a['mla_w_in'][0])), ag(bf(a['mla_w_uq'][0]))])
    w_in = from_rows(g_in)
    w_in = jnp.concatenate([w_in, w_in[:, QL + KL:]], axis=1)
    w_uq = from_cols(g_uq).reshape(QL, H, NOPE + ROPE)
    w_uq = jnp.concatenate([w_uq[:, :, :NOPE].reshape(QL, HN), w_uq[:, :, NOPE:].reshape(QL, HR)], axis=1)

    h3 = _normmod_fwd(x2, n1g[1:2], mod(1, 0, True), mod(1, 1, True), T, "l1_norm1")
    z3, (g_ukv,) = mm(h3, w_in, mode="nn", out_dtype=F32, name="l1_mla_in", carry=[ag(bf(a['mla_w_ukv'][0]))])
    w_ukv = g_ukv
    cqn, ckvn, kpe = _mla_prep_fwd(z3, gq, gkv, cos, sin, QL, KL, "l1_prep")
    q, (g_wo,) = mm(cqn, w_uq, mode="nn", out_dtype=F32, name="l1_uq", rows=T, carry=[ag(bf(a['mla_w_o'][0]))])
    w_o = from_rows(g_wo)
    kv = mm(ckvn, w_ukv, mode="nn", out_dtype=BF16, name="l1_ukv", b_dev=True)
    qa = _qrope_fwd(q, cos, sin, HN, "l1_qrope")
    (o_att, lse), (g_up1,) = _attn_fwd(qa, kv, kpe, T, H, "l1_attn", carry=[ag(up_sh[1])])
    w_up[1] = g_up1
    x3, o3 = mm(o_att, w_o, mode="nn", out_dtype=F32, name="l1_wo", res=x2, gate=mod(1, 2, False), seg_t=T)
    h4 = _normmod_fwd(x3, n2g[1:2], mod(1, 3, False), mod(1, 4, False), T, "l1_norm2")
    z4, (g_dn1,) = mm(h4, w_up[1], mode="nn", out_dtype=BF16, name="l1_up", b_dev=True, carry=[ag(dn_sh[1])])
    w_dn[1] = from_rows(g_dn1)
    a4 = _ffn_act_fwd(z4, fcw[1], a['ffn_conv_b'][1:2], T, "l1_act")
    x4, o4 = mm(a4, w_dn[1], mode="nn", out_dtype=F32, name="l1_down", res=x3, gate=mod(1, 5, False), seg_t=T)

    dx4, loss_cols, d_fng, do4, dg2_1 = _final(x4, a['final_norm_g'][None], a['loss_target'][0], o4, mod(1, 5, False),
                                               "final")
    loss = lax.psum(jnp.sum(loss_cols), ("x", "y", "c"))

    def cols(dw):
        k, n = dw.shape
        return jnp.transpose(dw.reshape(k, N_DEV, n // N_DEV), (1, 0, 2))

    def rows(dw):
        return dw.reshape(N_DEV, dw.shape[0] // N_DEV, dw.shape[1])

    da4 = mm(do4, w_dn[1], mode="nt", out_dtype=BF16, name="l1_down_dx")
    dw_dn1 = mm(a4, do4, mode="tn", out_dtype=BF16, name="l1_down_dw")
    dz4, dfcw1, dfcb1 = _ffn_act_bwd(z4, da4, fcw[1], a['ffn_conv_b'][1:2], T, "l1_act_bwd")
    dw_up1, (r_dn1,) = mm(h4, dz4, mode="tn", out_dtype=BF16, name="l1_up_dw", out_dev=True,
                          carry=[a2a(rows(dw_dn1))])
    dh4 = mm(dz4, w_up[1], mode="nt", out_dtype=F32, name="l1_up_dx", b_dev=True)
    dx3, dn2g1, dsh2_1, dsc2_1, do3, dg1_1 = _normmod_bwd(x3, n2g[1:2], mod(1, 4, False), dh4, dx4, T, "l1_norm2_bwd",
                                                         o_prev=o3, gate_prev=mod(1, 2, False))
    d_oatt = mm(do3, w_o, mode="nt", out_dtype=BF16, name="l1_wo_dx")
    dw_o = mm(o_att, do3, mode="tn", out_dtype=BF16, name="l1_wo_dw")
    (dqa, dqpe, dkv, dkpe), (r_up1, r_wo) = _attn_bwd(qa, kv, kpe, d_oatt, lse, T, H, "l1_attn_bwd",
                                                      carry=[a2a(dw_up1), a2a(rows(dw_o))])
    dqa = _qrope_bwd(dqpe, dqa, cos, sin, HN, "l1_qrope_bwd")
    dcq = mm(dqa, w_uq, mode="nt", out_dtype=F32, name="l1_uq_dx")
    dw_uq = mm(cqn, dqa, mode="tn", out_dtype=BF16, name="l1_uq_dw", rows=T)
    dw_uq = jnp.concatenate([dw_uq[:, :HN].reshape(QL, H, NOPE), dw_uq[:, HN:].reshape(QL, H, ROPE)], axis=2)
    dw_uq = dw_uq.reshape(QL, H * (NOPE + ROPE))
    dckv = mm(dkv, w_ukv, mode="nt", out_dtype=F32, name="l1_ukv_dx", b_dev=True)
    dw_ukv = mm(ckvn, dkv, mode="tn", out_dtype=BF16, name="l1_ukv_dw", out_dev=True)
    dz3, dgq, dgkv = _mla_prep_bwd(z3, dcq, dckv, dkpe, gq, gkv, cos, sin, QL, KL, T, "l1_prep_bwd")
    dh3 = mm(dz3, w_in, mode="nt", out_dtype=F32, name="l1_mla_in_dx")
    dw_in = mm(h3, dz3, mode="tn", out_dtype=BF16, name="l1_mla_in_dw").astype(F32)
    dw_in = jnp.concatenate([dw_in[:, :QL + KL], dw_in[:, QL + KL:QL + KL + ROPE] + dw_in[:, QL + KL + ROPE:QL + KL + 2 * ROPE]],
                            axis=1).astype(BF16)
    dx2, dn1g1, dsh1_1, dsc1_1, do2, dg2_0 = _normmod_bwd(x2, n1g[1:2], mod(1, 1, True), dh3, dx3, T, "l1_norm1_bwd",
                                                         o_prev=o2, gate_prev=mod(0, 5, True))
    da2, (r_uq, r_ukv) = mm(do2, w_dn[0], mode="nt", out_dtype=BF16, name="l0_down_dx",
                            carry=[a2a(cols(dw_uq)), a2a(dw_ukv)])
    dw_dn0, (r_in,) = mm(a2, do2, mode="tn", out_dtype=BF16, name="l0_down_dw", carry=[a2a(rows(dw_in))])
    dz2, dfcw0, dfcb0 = _ffn_act_bwd(z2, da2, fcw[0], a['ffn_conv_b'][0:1], T, "l0_act_bwd")
    dw_up0, (r_dn0,) = mm(h2, dz2, mode="tn", out_dtype=BF16, name="l0_up_dw", out_dev=True,
                          carry=[a2a(rows(dw_dn0))])
    dh2, (r_up0a,) = mm(dz2, w_up[0], mode="nt", out_dtype=F32, name="l0_up_dx", b_dev=True,
                        carry=[a2a(dw_up0[:, :D // 2])])
    dx1, dn2g0, dsh2_0, dsc2_0, do1, dg1_0 = _normmod_bwd(x1, n2g[0:1], mod(0, 4, True), dh2, dx2, T, "l0_norm2_bwd",
                                                         o_prev=o1, gate_prev=mod(0, 2, True))
    dy = mm(do1, w_about, mode="nt", out_dtype=F32, name="l0_ab_out_dx")
    dw_about = mm(y, do1, mode="tn", out_dtype=BF16, name="l0_ab_out_dw")
    dz, dlag, dlab, dws, dbs, dbin_a = _gmlp_bwd(z, dy, a['a_ln_g'], a['a_ln_b'], a['a_w_s'][0], a_bst, W, "l0_gmlp_bwd")
    dhc, dlbg, dlbb, dbcb = _conf_bwd1(z, dy, bcw, a['b_conv_b'], a['b_ln_g'], a['b_ln_b'], W, Wb, T, "l0_conf_bwd1")
    (dz, dbcw, dbin_b), (r_up0b,) = _conf_bwd2(z, dhc, dz, bcw, W, Wb, T, "l0_conf_bwd2",
                                               carry=[a2a(dw_up0[:, D // 2:])])
    dh1, (r_about,) = mm(dz, w_abin, mode="nt", out_dtype=F32, name="l0_ab_in_dx", b_dev=True,
                         carry=[a2a(rows(dw_about))])
    dw_abin_a = mm(h1, dz, mode="tn", out_dtype=BF16, name="l0_ab_in_dw_a", out_dev=True, p_range=(0, D // 2))
    dw_abin_b, (r_abin_a,) = mm(h1, dz, mode="tn", out_dtype=BF16, name="l0_ab_in_dw_b", out_dev=True,
                                p_range=(D // 2, D // 2), carry=[a2a(dw_abin_a)])
    dx0, dn1g0, dsh1_0, dsc1_0 = _normmod_bwd(x0, n1g[0:1], mod(0, 1, True), dh1, dx1, T, "l0_norm1_bwd")

    zero = jnp.zeros((D,), F32)
    dmod = jnp.stack([
        jnp.stack([jnp.stack([dsh1_0[0, 0], dsc1_0[0, 0], dg1_0[0, 0], dsh2_0[0, 0], dsc2_0[0, 0], dg2_0[0, 0]]),
                   jnp.stack([dsh1_0[1, 0], dsc1_0[1, 0], dg1_0[1, 0], dsh2_0[1, 0], dsc2_0[1, 0], dg2_0[1, 0]])]),
        jnp.stack([jnp.stack([dsh1_1[0, 0], dsc1_1[0, 0], dg1_1[0, 0], dsh2_1[0, 0], dsc2_1[0, 0], dg2_1[0, 0]]),
                   jnp.stack([dsh1_1[1, 0], dsc1_1[1, 0], zero, zero, zero, zero])])])
    small = {
        'norm1_g': jnp.concatenate([dn1g0, dn1g1], axis=0), 'norm2_g': jnp.concatenate([dn2g0, dn2g1], axis=0),
        'ab_b_in': jnp.concatenate([dbin_a, dbin_b], axis=1), 'a_ln_g': dlag, 'a_ln_b': dlab, 'a_w_s': dws[None],
        'a_b_s': jnp.sum(dbs, axis=-1)[None], 'b_conv_w': dbcw, 'b_conv_b': dbcb, 'b_ln_g': dlbg, 'b_ln_b': dlbb,
        'mla_q_norm_g': dgq, 'mla_kv_norm_g': dgkv, 'ffn_conv_w': jnp.stack([dfcw0, dfcw1]),
        'ffn_conv_b': jnp.concatenate([dfcb0, dfcb1], axis=0), 'final_norm_g': d_fng[0],
    }
    names = list(small)
    g2 = _all_gather(_pack([dmod] + [small[n] for n in names]), "ag_small_grads")
    red = _sum_lead(g2, "sum_small_grads").reshape(-1)
    red = dict(zip(names, _unpack(red, [(2, 2, 6, D)] + [small[n].shape for n in names])[1:]))
    dmod_all = g2.reshape(N_DEV, -1)[:, :2 * 2 * 6 * D].reshape(N_DEV, 2, 2, 6 * D)

    a16g = jnp.concatenate([c_all, jnp.tile(a['c_ctx'][None], (N_DEV, 1))], axis=0)
    dm_loc = lax.dynamic_slice(dmod_all, (0, 0, 0, me * NA), (N_DEV, 2, 2, NA))
    g_wada, cpart = [], []
    for l in range(2):
        dm16 = jnp.concatenate([dm_loc[:, l, 0], dm_loc[:, l, 1]], axis=0)
        g_wada.append(mm(a16g, dm16, mode="tn", out_dtype=F32, name=f"ada_dw{l}", a_silu=True))
        cpart.append(mm(dm_loc[:, l, 1], a['w_ada'][l], mode="nt", out_dtype=F32, name=f"ada_dc{l}"))
    g_bada = _sum_lead(jnp.transpose(dmod_all, (0, 2, 1, 3)).reshape(2 * N_DEV, 2 * 6 * D // 128, 128), "sum_b_ada")
    g_cc = _all_gather(jnp.concatenate(cpart, axis=0), "ag_c_ctx")
    g_cc = _sum_lead(g_cc.reshape(2 * N_DEV * N_DEV, D // 128, 128), "sum_c_ctx").reshape(D)
    grads = {
        'c_ctx': g_cc * _dsilu(a['c_ctx']), 'w_ada': jnp.stack(g_wada), 'b_ada': g_bada.reshape(2, 6 * D),
        'b_conv_w': lax.dynamic_slice(red['b_conv_w'], (0, me * (Wb // N_DEV)), (kw, Wb // N_DEV))[None],
        'ffn_conv_w': lax.dynamic_slice(red['ffn_conv_w'], (0, 0, me * (Fd // N_DEV)), (2, 3, Fd // N_DEV)),
        'mla_q_norm_g': lax.dynamic_slice(red['mla_q_norm_g'], (0, me * (QL // N_DEV)), (1, QL // N_DEV)),
        'mla_kv_norm_g': lax.dynamic_slice(red['mla_kv_norm_g'], (0, me * (KL // N_DEV)), (1, KL // N_DEV)),
    }
    for n in names:
        if n not in grads:
            grads[n] = red[n].reshape(a[n].shape)

    recvs = {'ab_w_out': [r_about], 'mla_w_in': [r_in], 'mla_w_uq': [r_uq], 'mla_w_ukv': [r_ukv],
             'mla_w_o': [r_wo], 'ffn_w_up': [r_up0a, r_up0b, r_up1], 'ffn_w_down': [r_dn0, r_dn1]}
    out = {}
    for n in WEIGHTS:
        shp = a[n].shape
        w2 = a[n].reshape(-1, shp[-1])
        m2, v2 = a['m_' + n].reshape(w2.shape), a['v_' + n].reshape(w2.shape)
        if n in recvs:
            res = _adamw(w2, m2, v2, "adamw_" + n, recv=recvs[n])
        elif n == 'w_ada':
            g2d = grads[n].reshape(w2.shape)
            res, (r_abin_b,) = _adamw(w2, m2, v2, "adamw_" + n, g=g2d, carry=[a2a(dw_abin_b)])
            res = (g2d,) + tuple(res)
            recvs['ab_w_in'] = [r_abin_a, r_abin_b]
        else:
            g2d = grads[n].reshape(w2.shape)
            res = (g2d,) + tuple(_adamw(w2, m2, v2, "adamw_" + n, g=g2d))
        out[n] = [r.reshape(shp) for r in res]
    return (loss, dx0[:T][None], *[out[n][0] for n in WEIGHTS], *[out[n][1] for n in WEIGHTS],
            *[out[n][2] for n in WEIGHTS], *[out[n][3] for n in WEIGHTS])


def kernel(x, c, ctx, c_ctx, norm1_g, norm2_g, w_ada, b_ada, ab_w_in, ab_b_in, a_ln_g, a_ln_b, a_w_s, a_b_s, b_conv_w, b_conv_b, b_ln_g, b_ln_b, ab_w_out, mla_w_in, mla_q_norm_g, mla_w_uq, mla_kv_norm_g, mla_w_ukv, mla_w_o, ffn_w_up, ffn_conv_w, ffn_conv_b, ffn_w_down, final_norm_g, loss_target, m_c_ctx, m_norm1_g, m_norm2_g, m_w_ada, m_b_ada, m_ab_w_in, m_ab_b_in, m_a_ln_g, m_a_ln_b, m_a_w_s, m_a_b_s, m_b_conv_w, m_b_conv_b, m_b_ln_g, m_b_ln_b, m_ab_w_out, m_mla_w_in, m_mla_q_norm_g, m_mla_w_uq, m_mla_kv_norm_g, m_mla_w_ukv, m_mla_w_o, m_ffn_w_up, m_ffn_conv_w, m_ffn_conv_b, m_ffn_w_down, m_final_norm_g, v_c_ctx, v_norm1_g, v_norm2_g, v_w_ada, v_b_ada, v_ab_w_in, v_ab_b_in, v_a_ln_g, v_a_ln_b, v_a_w_s, v_a_b_s, v_b_conv_w, v_b_conv_b, v_b_ln_g, v_b_ln_b, v_ab_w_out, v_mla_w_in, v_mla_q_norm_g, v_mla_w_uq, v_mla_kv_norm_g, v_mla_w_ukv, v_mla_w_o, v_ffn_w_up, v_ffn_conv_w, v_ffn_conv_b, v_ffn_w_down, v_final_norm_g):
    return _step(dict(locals()))
```

```python
import functools
import math

import jax
import jax.numpy as jnp
from jax import lax
from jax.experimental import pallas as pl
from jax.experimental.pallas import tpu as pltpu

F32 = jnp.float32
BF16 = jnp.bfloat16
SDS = jax.ShapeDtypeStruct

N_DEV = 8
EPS = 1e-6
CHUNK = 128
NOPE = 128
ROPE = 64
VDIM = 128
GRID_W = 64
ROPE_THETA = 10000.0
HALO = 16
ADAM_LR, ADAM_B1, ADAM_B2, ADAM_EPS, ADAM_WD, ADAM_STEP = 0.001, 0.9, 0.999, 1e-08, 0.01, 10
VMEM_LIMIT = 56 * 1024 * 1024


def _tile(n, prefs):
    for p in prefs:
        if n % p == 0:
            return p
    return n


def _params(sem, vmem=VMEM_LIMIT):
    return pltpu.CompilerParams(dimension_semantics=sem, vmem_limit_bytes=vmem)


def _sigmoid(x):
    return 0.5 * jnp.tanh(0.5 * x) + 0.5


def _silu(x):
    return x * _sigmoid(x)


def _dsilu(x):
    s = _sigmoid(x)
    return s * (1.0 + x * (1.0 - s))


_GELU_C = math.sqrt(2.0 / math.pi)


def _gelu(x):
    return 0.5 * x * (1.0 + jnp.tanh(_GELU_C * (x + 0.044715 * x * x * x)))


def _dgelu(x):
    t = jnp.tanh(_GELU_C * (x + 0.044715 * x * x * x))
    return 0.5 * (1.0 + t) + 0.5 * x * (1.0 - t * t) * _GELU_C * (1.0 + 3.0 * 0.044715 * x * x)


def _colsum(v):
    return jnp.sum(v, axis=0, keepdims=True)


_SIBLING = 1
_CHIPS = (2, 4, 6)


def _xchg(x_ref, o_ref, send_sems, recv_sems, local_sem, scatter):
    ax, ay, ac = lax.axis_index("x"), lax.axis_index("y"), lax.axis_index("c")
    me = 4 * ax + 2 * ay + ac

    def dev(k):
        return ax ^ (k >> 2), ay ^ ((k >> 1) & 1), ac ^ (k & 1)

    def idx(k):
        px, py, pc = dev(k)
        return 4 * px + 2 * py + pc

    def copy(k, src, dst, to):
        return pltpu.make_async_remote_copy(src_ref=src, dst_ref=dst, send_sem=send_sems.at[k - 1],
                                            recv_sem=recv_sems.at[k - 1], device_id=dev(to),
                                            device_id_type=pl.DeviceIdType.MESH)

    def own():
        return pltpu.make_async_copy(x_ref.at[me] if scatter else x_ref, o_ref.at[me], local_sem)

    def sends():
        if scatter:
            return [copy(k, x_ref.at[idx(k)], o_ref.at[me], k) for k in range(1, N_DEV)]
        return [copy(k, x_ref, o_ref.at[me], k) for k in (_SIBLING,) + _CHIPS]

    def forwards():
        return [] if scatter else [copy(j + 1, o_ref.at[idx(j)], o_ref.at[idx(j)], _SIBLING) for j in _CHIPS]

    def arrival(k):
        return copy(k, o_ref.at[idx(k)], o_ref.at[idx(k)], k)

    return own, sends, forwards, arrival


def _xchg_start(*refs, scatter):
    own, sends, _, _ = _xchg(*refs, scatter)
    own().start()
    for cp in sends():
        cp.start()


def _xchg_forward(*refs, scatter):
    _, _, forwards, arrival = _xchg(*refs, scatter)
    if not scatter:
        for j, fw in zip(_CHIPS, forwards()):
            arrival(j).wait_recv()
            fw.start()


def _xchg_finish(*refs, scatter):
    own, sends, forwards, arrival = _xchg(*refs, scatter)
    for k in range(1, N_DEV):
        if scatter or k not in _CHIPS:
            arrival(k).wait_recv()
    for cp in sends() + forwards():
        cp.wait_send()
    own().wait()


_XCHG_SEMS = [pltpu.SemaphoreType.DMA((N_DEV - 1,)), pltpu.SemaphoreType.DMA((N_DEV - 1,)), pltpu.SemaphoreType.DMA]


def _xchg_shape(x, scatter):
    return SDS((N_DEV,) + tuple(x.shape[1:] if scatter else x.shape), x.dtype)


def _exchange(x, *, scatter, name):
    def body(*refs):
        _xchg_start(*refs, scatter=scatter)
        _xchg_forward(*refs, scatter=scatter)
        _xchg_finish(*refs, scatter=scatter)

    return pl.pallas_call(
        body, out_shape=_xchg_shape(x, scatter),
        in_specs=[pl.BlockSpec(memory_space=pl.ANY)], out_specs=pl.BlockSpec(memory_space=pl.ANY),
        scratch_shapes=list(_XCHG_SEMS), name=name)(x)


def _carried(body, carry, n_in, n_out, n_scratch, grid):
    nc = len(carry)
    total = math.prod(grid)
    mid = (3 * total) // 4

    def wrapped(*refs):
        ins, cin = refs[:n_in], refs[n_in:n_in + nc]
        o0 = n_in + nc
        outs, cout = refs[o0:o0 + n_out], refs[o0 + n_out:o0 + n_out + nc]
        scr = refs[o0 + n_out + nc:]
        sems = scr[n_scratch:]
        step = pl.program_id(0)
        for ax in range(1, len(grid)):
            step = step * grid[ax] + pl.program_id(ax)

        def each(fn):
            for c in range(nc):
                fn(cin[c], cout[c], *sems[3 * c:3 * c + 3], scatter=carry[c][1])

        @pl.when(step == 0)
        def _():
            each(_xchg_start)

        body(*ins, *outs, *scr[:n_scratch])

        if mid < total - 1:
            @pl.when(step == mid)
            def _():
                each(_xchg_forward)

        @pl.when(step == total - 1)
        def _():
            if mid >= total - 1:
                each(_xchg_forward)
            each(_xchg_finish)

    return wrapped


def _carry_call(body, carry, *, grid, out_shape, in_specs, out_specs, scratch_shapes, sem, name, ins, aliases=None):
    carry = carry or []
    nc = len(carry)
    if nc:
        body = _carried(body, carry, len(in_specs), len(out_shape), len(scratch_shapes), grid)
        anyspec = pl.BlockSpec(memory_space=pl.ANY)
        in_specs = list(in_specs) + [anyspec] * nc
        out_specs = list(out_specs) + [anyspec] * nc
        out_shape = list(out_shape) + [_xchg_shape(x, sc) for x, sc in carry]
        scratch_shapes = list(scratch_shapes) + list(_XCHG_SEMS) * nc
        ins = list(ins) + [x for x, _ in carry]
        sem = ("arbitrary",) * len(grid)
    out = pl.pallas_call(body, out_shape=out_shape, grid=grid, in_specs=in_specs, out_specs=out_specs,
                         scratch_shapes=scratch_shapes, compiler_params=_params(sem), name=name,
                         input_output_aliases=aliases or {})(*ins)
    n_main = len(out) - nc
    return list(out[:n_main]), list(out[n_main:])


def _all_gather(x, name):
    return _exchange(x, scatter=False, name=name)


def _all_to_all(x, name):
    return _exchange(x, scatter=True, name=name)


def _sum_lead(x, name):
    n, R, C = x.shape
    tr = _tile(R, (512, 256, 128, 64, 32, 16, 8))

    def body(x_ref, o_ref):
        acc = x_ref[0]
        for d in range(1, n):
            acc = acc + x_ref[d]
        o_ref[...] = acc

    return pl.pallas_call(
        body, out_shape=SDS((R, C), F32), grid=(R // tr,),
        in_specs=[pl.BlockSpec((n, tr, C), lambda i: (0, i, 0))], out_specs=pl.BlockSpec((tr, C), lambda i: (i, 0)),
        compiler_params=_params(("parallel",)), name=name)(x)


_TP = (1408, 1088, 1024, 768, 512, 256, 128)
_TQ = (1408, 1024, 768, 512, 256, 128)
_TR = (2048, 1408, 1024, 768, 512, 256, 128)
_TR_TN = (1088, 1024, 512, 256, 128)


def _mm(a, b, *, mode, out_dtype, name, rows=None, bias=None, res=None, gate=None, seg_t=None, a_silu=False, carry=None,
        b_dev=False, out_dev=False, p_range=None):
    if mode == "nn":
        P, R, Q = rows or a.shape[0], a.shape[1], (b.shape[0] * b.shape[2] if b_dev else b.shape[1])
    elif mode == "nt":
        P, R, Q = rows or a.shape[0], a.shape[1], (b.shape[1] if b_dev else b.shape[0])
    else:
        R, P, Q = rows or a.shape[0], a.shape[1], b.shape[1]
    p0 = 0
    if p_range is not None:
        p0, P = p_range
    tp = _tile(P, _TP)
    tq = _tile(Q // N_DEV if (out_dev or (b_dev and mode == "nn")) else Q, _TQ)
    tr = _tile(R // N_DEV if (b_dev and mode == "nt") else R, _TR if mode != "tn" else _TR_TN)
    nk = R // tr
    qd = (Q // N_DEV) // tq
    rd = (R // N_DEV) // tr
    if mode == "nn":
        a_spec = pl.BlockSpec((tp, tr), lambda i, j, k: (i, k))
        b_spec = (pl.BlockSpec((None, tr, tq), lambda i, j, k: (j // qd, k, j % qd)) if b_dev
                  else pl.BlockSpec((tr, tq), lambda i, j, k: (k, j)))
        dims = (((1,), (0,)), ((), ()))
    elif mode == "nt":
        a_spec = pl.BlockSpec((tp, tr), lambda i, j, k: (i, k))
        b_spec = (pl.BlockSpec((None, tq, tr), lambda i, j, k: (k // rd, j, k % rd)) if b_dev
                  else pl.BlockSpec((tq, tr), lambda i, j, k: (j, k)))
        dims = (((1,), (1,)), ((), ()))
    else:
        pb = p0 // tp
        a_spec = pl.BlockSpec((tr, tp), lambda i, j, k: (k, i + pb))
        b_spec = pl.BlockSpec((tr, tq), lambda i, j, k: (k, j))
        dims = (((0,), (0,)), ((), ()))
    ins, in_specs = [a, b], [a_spec, b_spec]
    if bias is not None:
        ins.append(bias)
        in_specs.append(pl.BlockSpec((1, tq), lambda i, j, k: (0, j)))
    gated = res is not None
    if gated:
        n_seg = gate.shape[0]
        ins += [res, gate]
        in_specs += [pl.BlockSpec((tp, tq), lambda i, j, k: (i, j)),
                     pl.BlockSpec((n_seg, 1, tq), lambda i, j, k: (0, 0, j))]
    if out_dev:
        out_shape = [SDS((N_DEV, P, Q // N_DEV), out_dtype)]
        out_specs = [pl.BlockSpec((None, tp, tq), lambda i, j, k: (j // qd, i, j % qd))]
    else:
        out_shape = [SDS((P, Q), out_dtype)]
        out_specs = [pl.BlockSpec((tp, tq), lambda i, j, k: (i, j))]
    if gated:
        out_shape.append(SDS((P, Q), BF16))
        out_specs.append(pl.BlockSpec((tp, tq), lambda i, j, k: (i, j)))

    def body(*refs):
        a_ref, b_ref = refs[0], refs[1]
        pos = 2
        bias_ref = res_ref = gate_ref = o2_ref = None
        if bias is not None:
            bias_ref = refs[pos]
            pos += 1
        if gated:
            res_ref, gate_ref = refs[pos], refs[pos + 1]
            pos += 2
        o_ref = refs[pos]
        pos += 1
        if gated:
            o2_ref = refs[pos]
            pos += 1
        acc_ref = refs[pos] if nk > 1 else None
        k = pl.program_id(2)
        av = a_ref[...]
        if a_silu:
            av = _silu(av.astype(F32))
        part = lax.dot_general(av.astype(BF16), b_ref[...].astype(BF16), dims, preferred_element_type=F32)
        if nk > 1:
            @pl.when(k == 0)
            def _():
                acc_ref[...] = part

            @pl.when(k > 0)
            def _():
                acc_ref[...] += part

        @pl.when(k == nk - 1)
        def _():
            acc = acc_ref[...] if nk > 1 else part
            if bias_ref is not None:
                acc = acc + bias_ref[...]
            if gated:
                if n_seg == 1:
                    g = gate_ref[0]
                else:
                    row = pl.program_id(0) * tp + lax.broadcasted_iota(jnp.int32, (tp, 1), 0)
                    g = jnp.where(row < seg_t, gate_ref[0], gate_ref[1])
                o_ref[...] = (res_ref[...] + g * acc).astype(o_ref.dtype)
                o2_ref[...] = acc.astype(BF16)
            else:
                o_ref[...] = acc.astype(o_ref.dtype)

    out, carried = _carry_call(
        body, carry, grid=(P // tp, Q // tq, nk), out_shape=out_shape, in_specs=in_specs, out_specs=out_specs,
        scratch_shapes=[pltpu.VMEM((tp, tq), F32)] if nk > 1 else [], sem=("parallel", "parallel", "arbitrary"),
        name=name, ins=ins)
    res_out = tuple(out) if gated else out[0]
    return (res_out, carried) if carry else res_out


def _row_tile(seg_t, m):
    return 256 if (seg_t % 256 == 0 and m % 256 == 0) else 128


def _normmod_fwd(x, g, sh, sc, seg_t, name):
    M, D = x.shape
    tm = _row_tile(seg_t, M)
    n_seg = sh.shape[0]
    nt = seg_t // tm

    def seg(i):
        return ((i >= nt).astype(jnp.int32) if n_seg == 2 else 0, 0, 0)

    def body(x_ref, g_ref, sh_ref, sc_ref, o_ref):
        xv = x_ref[...]
        r = lax.rsqrt(jnp.mean(xv * xv, axis=-1, keepdims=True) + EPS)
        y = xv * r * g_ref[...]
        o_ref[...] = (y * (1.0 + sc_ref[0]) + sh_ref[0]).astype(BF16)

    return pl.pallas_call(
        body, out_shape=SDS((M, D), BF16), grid=(M // tm,),
        in_specs=[pl.BlockSpec((tm, D), lambda i: (i, 0)), pl.BlockSpec((1, D), lambda i: (0, 0)),
                  pl.BlockSpec((1, 1, D), seg), pl.BlockSpec((1, 1, D), seg)],
        out_specs=pl.BlockSpec((tm, D), lambda i: (i, 0)),
        compiler_params=_params(("parallel",)), name=name)(x, g, sh, sc)


def _normmod_bwd(x, g, sc, dh, dx_in, seg_t, name, o_prev=None, gate_prev=None):
    M, D = x.shape
    tm = _row_tile(seg_t, M)
    n_seg = sc.shape[0]
    nt = seg_t // tm
    n_in = dx_in.shape[0] // tm
    with_prev = o_prev is not None
    n_segp = gate_prev.shape[0] if with_prev else 0

    def seg(i):
        return ((i >= nt).astype(jnp.int32) if n_seg == 2 else 0, 0, 0)

    def segp(i):
        return ((i >= nt).astype(jnp.int32) if n_segp == 2 else 0, 0, 0)

    def body(*refs):
        x_ref, g_ref, sc_ref, dh_ref, dxin_ref = refs[:5]
        pos = 5
        if with_prev:
            op_ref, gp_ref = refs[5], refs[6]
            pos = 7
        dx_ref, dg_ref, dsh_ref, dsc_ref = refs[pos:pos + 4]
        if with_prev:
            dop_ref, dgp_ref = refs[pos + 4], refs[pos + 5]
        i = pl.program_id(0)
        xv = x_ref[...]
        r = lax.rsqrt(jnp.mean(xv * xv, axis=-1, keepdims=True) + EPS)
        xh = xv * r
        gv = g_ref[...]
        dhv = dh_ref[...].astype(F32)
        dy = dhv * (1.0 + sc_ref[0])
        dxh = dy * gv
        dxv = r * (dxh - xh * jnp.mean(dxh * xh, axis=-1, keepdims=True))
        if n_in * tm < M:
            dxv = dxv + jnp.where(i < n_in, dxin_ref[...], 0.0)
        else:
            dxv = dxv + dxin_ref[...]
        dx_ref[...] = dxv

        @pl.when(i == 0)
        def _():
            dg_ref[...] = jnp.zeros_like(dg_ref)

        first_of_seg = (i == 0) | (i == nt) if n_seg == 2 else (i == 0)

        @pl.when(first_of_seg)
        def _():
            dsh_ref[...] = jnp.zeros_like(dsh_ref)
            dsc_ref[...] = jnp.zeros_like(dsc_ref)

        dg_ref[...] += _colsum(dy * xh)
        dsh_ref[0] += _colsum(dhv)
        dsc_ref[0] += _colsum(dhv * xh * gv)
        if with_prev:
            first_of_segp = (i == 0) | (i == nt) if n_segp == 2 else (i == 0)

            @pl.when(first_of_segp)
            def _():
                dgp_ref[...] = jnp.zeros_like(dgp_ref)

            dop_ref[...] = (gp_ref[0] * dxv).astype(BF16)
            dgp_ref[0] += _colsum(dxv * op_ref[...].astype(F32))

    row = pl.BlockSpec((tm, D), lambda i: (i, 0))
    ins = [x, g, sc, dh, dx_in]
    in_specs = [row, pl.BlockSpec((1, D), lambda i: (0, 0)), pl.BlockSpec((1, 1, D), seg), row,
                pl.BlockSpec((tm, D), lambda i: (jnp.minimum(i, n_in - 1), 0))]
    out_shape = [SDS((M, D), F32), SDS((1, D), F32), SDS((n_seg, 1, D), F32), SDS((n_seg, 1, D), F32)]
    out_specs = [row, pl.BlockSpec((1, D), lambda i: (0, 0)), pl.BlockSpec((1, 1, D), seg), pl.BlockSpec((1, 1, D), seg)]
    if with_prev:
        ins += [o_prev, gate_prev]
        in_specs += [row, pl.BlockSpec((1, 1, D), segp)]
        out_shape += [SDS((M, D), BF16), SDS((n_segp, 1, D), F32)]
        out_specs += [row, pl.BlockSpec((1, 1, D), segp)]
    return pl.pallas_call(
        body, out_shape=out_shape, grid=(M // tm,), in_specs=in_specs, out_specs=out_specs,
        compiler_params=_params(("arbitrary",)), name=name)(*ins)


def _final(x, g, target, o_prev, gate_prev, name):
    T, D = x.shape
    tm = _tile(T, (256, 128))

    def body(x_ref, g_ref, t_ref, op_ref, gp_ref, dx_ref, loss_ref, dg_ref, dop_ref, dgp_ref):
        i = pl.program_id(0)
        xv = x_ref[...]
        r = lax.rsqrt(jnp.mean(xv * xv, axis=-1, keepdims=True) + EPS)
        xh = xv * r
        gv = g_ref[...]
        e = xh * gv - t_ref[...]
        dout = e * (1.0 / D)
        dxh = dout * gv
        dxv = r * (dxh - xh * jnp.mean(dxh * xh, axis=-1, keepdims=True))
        dx_ref[...] = dxv
        dop_ref[...] = (gp_ref[0] * dxv).astype(BF16)

        @pl.when(i == 0)
        def _():
            loss_ref[...] = jnp.zeros_like(loss_ref)
            dg_ref[...] = jnp.zeros_like(dg_ref)
            dgp_ref[...] = jnp.zeros_like(dgp_ref)

        loss_ref[...] += _colsum(e * e) * (0.5 / D)
        dg_ref[...] += _colsum(dout * xh)
        dgp_ref[0] += _colsum(dxv * op_ref[...].astype(F32))

    row = pl.BlockSpec((tm, D), lambda i: (i, 0))
    vec = pl.BlockSpec((1, D), lambda i: (0, 0))
    vec3 = pl.BlockSpec((1, 1, D), lambda i: (0, 0, 0))
    return pl.pallas_call(
        body, out_shape=[SDS((T, D), F32), SDS((1, D), F32), SDS((1, D), F32), SDS((T, D), BF16), SDS((1, 1, D), F32)],
        grid=(T // tm,), in_specs=[row, vec, row, row, vec3], out_specs=[row, vec, vec, row, vec3],
        compiler_params=_params(("arbitrary",)), name=name)(x, g, target, o_prev, gate_prev)


def _gmlp_core(z, lg, lb, ws_ref, bst):
    W = z.shape[1] // 2
    t = _gelu(z)
    u, v = t[:, :W], t[:, W:]
    mu = jnp.mean(v, axis=-1, keepdims=True)
    vc = v - mu
    rstd = lax.rsqrt(jnp.mean(vc * vc, axis=-1, keepdims=True) + EPS)
    vhat = vc * rstd
    vn = vhat * lg + lb
    vp = []
    for h in range(W // CHUNK):
        blk = vn[:, h * CHUNK:(h + 1) * CHUNK].astype(BF16)
        vp.append(jnp.dot(ws_ref[h].astype(BF16), blk, preferred_element_type=F32) + bst[:, h:h + 1])
    return u, vhat, rstd, vp


def _gmlp_fwd(z, ln_g, ln_b, w_s, b_st, W, name):
    M = z.shape[0]
    H = W // CHUNK

    def body(z_ref, lg_ref, lb_ref, ws_ref, bst_ref, o_ref):
        u, _, _, vp = _gmlp_core(z_ref[...], lg_ref[...], lb_ref[...], ws_ref, bst_ref[...])
        for h in range(H):
            o_ref[:, h * CHUNK:(h + 1) * CHUNK] = (u[:, h * CHUNK:(h + 1) * CHUNK] * vp[h]).astype(BF16)

    vec = pl.BlockSpec((1, W), lambda i: (0, 0))
    return pl.pallas_call(
        body, out_shape=SDS((M, 2 * W), BF16), grid=(M // CHUNK,),
        in_specs=[pl.BlockSpec((CHUNK, 2 * W), lambda i: (i, 0)), vec, vec,
                  pl.BlockSpec((H, CHUNK, CHUNK), lambda i: (0, 0, 0)), pl.BlockSpec((CHUNK, H), lambda i: (0, 0))],
        out_specs=pl.BlockSpec((CHUNK, W), lambda i: (i, 0)),
        compiler_params=_params(("parallel",)), name=name)(z, ln_g, ln_b, w_s, b_st)


def _gmlp_bwd(z, dy, ln_g, ln_b, w_s, b_st, W, name):
    M = z.shape[0]
    H = W // CHUNK
    ZW = z.shape[1]

    def body(z_ref, dy_ref, lg_ref, lb_ref, ws_ref, bst_ref, dz_ref, dlg_ref, dlb_ref, dws_ref, dbs_ref, dbin_ref):
        i = pl.program_id(0)

        @pl.when(i == 0)
        def _():
            for r in (dlg_ref, dlb_ref, dws_ref, dbs_ref, dbin_ref):
                r[...] = jnp.zeros_like(r)

        zv = z_ref[...]
        lg = lg_ref[...]
        u, vhat, rstd, vp = _gmlp_core(zv, lg, lb_ref[...], ws_ref, bst_ref[...])
        vn = vhat * lg + lb_ref[...]
        dya = dy_ref[...]
        du_parts, dvn_parts = [], []
        for h in range(H):
            sl = slice(h * CHUNK, (h + 1) * CHUNK)
            dya_h = dya[:, sl]
            du_parts.append(dya_h * vp[h])
            dvp = dya_h * u[:, sl]
            dbs_ref[h] += dvp
            dvp16 = dvp.astype(BF16)
            dws_ref[h] += lax.dot_general(dvp16, vn[:, sl].astype(BF16), (((1,), (1,)), ((), ())),
                                          preferred_element_type=F32)
            dvn_parts.append(lax.dot_general(ws_ref[h].astype(BF16), dvp16, (((0,), (0,)), ((), ())),
                                             preferred_element_type=F32))
        du = jnp.concatenate(du_parts, axis=1)
        dvn = jnp.concatenate(dvn_parts, axis=1)
        dlg_ref[...] += _colsum(dvn * vhat)
        dlb_ref[...] += _colsum(dvn)
        dvh = dvn * lg
        dv = rstd * (dvh - jnp.mean(dvh, axis=-1, keepdims=True) - vhat * jnp.mean(dvh * vhat, axis=-1, keepdims=True))
        dz = jnp.concatenate([du, dv], axis=1) * _dgelu(zv)
        dbin_ref[...] += _colsum(dz)
        dz_ref[...] = dz.astype(BF16)

    vec = pl.BlockSpec((1, W), lambda i: (0, 0))
    mat = pl.BlockSpec((H, CHUNK, CHUNK), lambda i: (0, 0, 0))
    return pl.pallas_call(
        body,
        out_shape=[SDS((M, ZW), BF16), SDS((1, W), F32), SDS((1, W), F32), SDS((H, CHUNK, CHUNK), F32),
                   SDS((H, CHUNK, CHUNK), F32), SDS((1, 2 * W), F32)],
        grid=(M // CHUNK,),
        in_specs=[pl.BlockSpec((CHUNK, 2 * W), lambda i: (i, 0)), pl.BlockSpec((CHUNK, W), lambda i: (i, 0)), vec, vec,
                  mat, pl.BlockSpec((CHUNK, H), lambda i: (0, 0))],
        out_specs=[pl.BlockSpec((CHUNK, 2 * W), lambda i: (i, 0)), vec, vec, mat, mat,
                   pl.BlockSpec((1, 2 * W), lambda i: (0, 0))],
        compiler_params=_params(("arbitrary",)), name=name)(z, dy, ln_g, ln_b, w_s, b_st)


def _halo_specs(tm, width, col, n_rows):
    per = tm // HALO
    last = n_rows // HALO - 1
    prev = pl.BlockSpec((HALO, width), lambda i: (jnp.maximum(i * per - 1, 0), col))
    nxt = pl.BlockSpec((HALO, width), lambda i: (jnp.minimum((i + 1) * per, last), col))
    return prev, nxt


def _edge_flags(i, tm, seg_t, m):
    r0 = i * tm
    has_prev = jnp.where((r0 == 0) | (r0 == seg_t), 0.0, 1.0)
    has_next = jnp.where((r0 + tm == seg_t) | (r0 + tm == m), 0.0, 1.0)
    return has_prev, has_next


def _glu(zz, wb):
    return zz[:, :wb] * _sigmoid(zz[:, wb:])


def _conv_taps(src_ref, w_ref, first, tm, kw, flip=False):
    rc = 32
    parts = []
    for c in range(tm // rc):
        acc = None
        for k in range(kw):
            wk = w_ref[pl.ds(kw - 1 - k if flip else k, 1), :]
            term = src_ref[pl.ds(first + c * rc + k, rc), :] * wk
            acc = term if acc is None else acc + term
        parts.append(acc)
    return jnp.concatenate(parts, axis=0)


def _conf_fwd(z, y, conv_w, conv_b, ln_g, ln_b, W, Wb, seg_t, name, carry=None):
    M = z.shape[0]
    tm = _row_tile(seg_t, M)
    kw = conv_w.shape[0]
    pad = (kw - 1) // 2
    col = (2 * W) // (2 * Wb)

    def body(zc_ref, zp_ref, zn_ref, y_hbm, cw_ref, cb_ref, lg_ref, lb_ref, o_ref, hs_ref):
        del y_hbm
        hp, hn = _edge_flags(pl.program_id(0), tm, seg_t, M)
        hs_ref[pl.ds(0, HALO), :] = _glu(zp_ref[...], Wb) * hp
        hs_ref[pl.ds(HALO, tm), :] = _glu(zc_ref[...], Wb)
        hs_ref[pl.ds(HALO + tm, HALO), :] = _glu(zn_ref[...], Wb) * hn
        hc = _conv_taps(hs_ref, cw_ref, HALO - pad, tm, kw) + cb_ref[...]
        mu = jnp.mean(hc, axis=-1, keepdims=True)
        c = hc - mu
        rstd = lax.rsqrt(jnp.mean(c * c, axis=-1, keepdims=True) + EPS)
        o_ref[...] = _silu(c * rstd * lg_ref[...] + lb_ref[...]).astype(BF16)

    prev, nxt = _halo_specs(tm, 2 * Wb, col, M)
    vec = pl.BlockSpec((1, Wb), lambda i: (0, 0))
    out, carried = _carry_call(
        body, carry, out_shape=[SDS(y.shape, BF16)], grid=(M // tm,),
        in_specs=[pl.BlockSpec((tm, 2 * Wb), lambda i: (i, col)), prev, nxt, pl.BlockSpec(memory_space=pl.ANY),
                  pl.BlockSpec((kw, Wb), lambda i: (0, 0)), vec, vec, vec],
        out_specs=[pl.BlockSpec((tm, Wb), lambda i: (i, W // Wb))],
        scratch_shapes=[pltpu.VMEM((tm + 2 * HALO, Wb), F32)],
        aliases={3: 0}, sem=("parallel",), name=name, ins=(z, z, z, y, conv_w, conv_b, ln_g, ln_b))
    return (out[0], carried) if carry else out[0]


def _conf_bwd1(z, dy, conv_w, conv_b, ln_g, ln_b, W, Wb, seg_t, name):
    M = z.shape[0]
    tm = _row_tile(seg_t, M)
    kw = conv_w.shape[0]
    pad = (kw - 1) // 2
    col = (2 * W) // (2 * Wb)

    def body(zc_ref, zp_ref, zn_ref, dy_ref, cw_ref, cb_ref, lg_ref, lb_ref, dhc_ref, dlg_ref, dlb_ref, dcb_ref, hs_ref):
        i = pl.program_id(0)

        @pl.when(i == 0)
        def _():
            for r in (dlg_ref, dlb_ref, dcb_ref):
                r[...] = jnp.zeros_like(r)

        hp, hn = _edge_flags(i, tm, seg_t, M)
        hs_ref[pl.ds(0, HALO), :] = _glu(zp_ref[...], Wb) * hp
        hs_ref[pl.ds(HALO, tm), :] = _glu(zc_ref[...], Wb)
        hs_ref[pl.ds(HALO + tm, HALO), :] = _glu(zn_ref[...], Wb) * hn
        hc = _conv_taps(hs_ref, cw_ref, HALO - pad, tm, kw) + cb_ref[...]
        mu = jnp.mean(hc, axis=-1, keepdims=True)
        c = hc - mu
        rstd = lax.rsqrt(jnp.mean(c * c, axis=-1, keepdims=True) + EPS)
        hh = c * rstd
        lg = lg_ref[...]
        dhn = dy_ref[...] * _dsilu(hh * lg + lb_ref[...])
        dlg_ref[...] += _colsum(dhn * hh)
        dlb_ref[...] += _colsum(dhn)
        dhh = dhn * lg
        dhc = rstd * (dhh - jnp.mean(dhh, axis=-1, keepdims=True) - hh * jnp.mean(dhh * hh, axis=-1, keepdims=True))
        dcb_ref[...] += _colsum(dhc)
        dhc_ref[...] = dhc

    prev, nxt = _halo_specs(tm, 2 * Wb, col, M)
    vec = pl.BlockSpec((1, Wb), lambda i: (0, 0))
    return pl.pallas_call(
        body, out_shape=[SDS((M, Wb), F32), SDS((1, Wb), F32), SDS((1, Wb), F32), SDS((1, Wb), F32)], grid=(M // tm,),
        in_specs=[pl.BlockSpec((tm, 2 * Wb), lambda i: (i, col)), prev, nxt, pl.BlockSpec((tm, Wb), lambda i: (i, W // Wb)),
                  pl.BlockSpec((kw, Wb), lambda i: (0, 0)), vec, vec, vec],
        out_specs=[pl.BlockSpec((tm, Wb), lambda i: (i, 0)), vec, vec, vec],
        scratch_shapes=[pltpu.VMEM((tm + 2 * HALO, Wb), F32)],
        compiler_params=_params(("arbitrary",)), name=name)(z, z, z, dy, conv_w, conv_b, ln_g, ln_b)


def _conf_bwd2(z, dhc, dz, conv_w, W, Wb, seg_t, name, carry=None):
    M = z.shape[0]
    tm = _row_tile(seg_t, M)
    kw = conv_w.shape[0]
    pad = (kw - 1) // 2
    col = (2 * W) // (2 * Wb)

    def body(zc_ref, zp_ref, zn_ref, dc_ref, dp_ref, dn_ref, dz_hbm, cw_ref, dz_ref, dcw_ref, dbin_ref, hs_ref, ds_ref):
        del dz_hbm
        i = pl.program_id(0)

        @pl.when(i == 0)
        def _():
            dcw_ref[...] = jnp.zeros_like(dcw_ref)
            dbin_ref[...] = jnp.zeros_like(dbin_ref)

        hp, hn = _edge_flags(i, tm, seg_t, M)
        zc = zc_ref[...]
        hs_ref[pl.ds(0, HALO), :] = _glu(zp_ref[...], Wb) * hp
        hs_ref[pl.ds(HALO, tm), :] = _glu(zc, Wb)
        hs_ref[pl.ds(HALO + tm, HALO), :] = _glu(zn_ref[...], Wb) * hn
        dcur = dc_ref[...]
        ds_ref[pl.ds(0, HALO), :] = dp_ref[...] * hp
        ds_ref[pl.ds(HALO, tm), :] = dcur
        ds_ref[pl.ds(HALO + tm, HALO), :] = dn_ref[...] * hn
        dh = _conv_taps(ds_ref, cw_ref, HALO - pad, tm, kw, flip=True)
        for k in range(kw):
            dcw_ref[pl.ds(k, 1), :] += _colsum(dcur * hs_ref[pl.ds(HALO - pad + k, tm), :])
        a, gt = zc[:, :Wb], zc[:, Wb:]
        s = _sigmoid(gt)
        dz = jnp.concatenate([dh * s, dh * a * s * (1.0 - s)], axis=1)
        dbin_ref[...] += _colsum(dz)
        dz_ref[...] = dz.astype(BF16)

    prev, nxt = _halo_specs(tm, 2 * Wb, col, M)
    dprev, dnxt = _halo_specs(tm, Wb, 0, M)
    out, carried = _carry_call(
        body, carry, out_shape=[SDS(dz.shape, BF16), SDS((kw, Wb), F32), SDS((1, 2 * Wb), F32)], grid=(M // tm,),
        in_specs=[pl.BlockSpec((tm, 2 * Wb), lambda i: (i, col)), prev, nxt,
                  pl.BlockSpec((tm, Wb), lambda i: (i, 0)), dprev, dnxt, pl.BlockSpec(memory_space=pl.ANY),
                  pl.BlockSpec((kw, Wb), lambda i: (0, 0))],
        out_specs=[pl.BlockSpec((tm, 2 * Wb), lambda i: (i, col)), pl.BlockSpec((kw, Wb), lambda i: (0, 0)),
                   pl.BlockSpec((1, 2 * Wb), lambda i: (0, 0))],
        scratch_shapes=[pltpu.VMEM((tm + 2 * HALO, Wb), F32), pltpu.VMEM((tm + 2 * HALO, Wb), F32)],
        aliases={6: 0}, sem=("arbitrary",), name=name, ins=(z, z, z, dhc, dhc, dhc, dz, conv_w))
    return (out, carried) if carry else out


_TF = (1408, 512, 256, 128)
_RC = 16
_CG = 256


def _col_groups(width):
    return [(c0, min(_CG, width - c0)) for c0 in range(0, width, _CG)]


def _ffn_act_fwd(z, conv_w, conv_b, seg_t, name):
    M, F2 = z.shape
    Fd = F2 // 2
    tm = _row_tile(seg_t, M)
    tf = _tile(Fd, _TF)
    nf = Fd // tf
    per, last = tm // HALO, M // HALO - 1

    def body(g_ref, gp_ref, gn_ref, u_ref, cw_ref, cb_ref, o_ref, gs_ref):
        hp, hn = _edge_flags(pl.program_id(0), tm, seg_t, M)
        gs_ref[pl.ds(0, HALO), :] = gp_ref[...].astype(F32) * hp
        gs_ref[pl.ds(HALO, tm), :] = g_ref[...].astype(F32)
        gs_ref[pl.ds(HALO + tm, HALO), :] = gn_ref[...].astype(F32) * hn
        for c0, cw in _col_groups(tf):
            cs = pl.ds(c0, cw)
            w0, w1, w2, cb = cw_ref[pl.ds(0, 1), cs], cw_ref[pl.ds(1, 1), cs], cw_ref[pl.ds(2, 1), cs], cb_ref[:, cs]
            for r0 in range(0, tm, _RC):
                gc = (gs_ref[pl.ds(HALO - 1 + r0, _RC), cs] * w0 + gs_ref[pl.ds(HALO + r0, _RC), cs] * w1
                      + gs_ref[pl.ds(HALO + 1 + r0, _RC), cs] * w2 + cb)
                o_ref[pl.ds(r0, _RC), cs] = (_silu(gc) * u_ref[pl.ds(r0, _RC), cs].astype(F32)).astype(BF16)

    return pl.pallas_call(
        body, out_shape=SDS((M, Fd), BF16), grid=(M // tm, nf),
        in_specs=[pl.BlockSpec((tm, tf), lambda i, j: (i, j)),
                  pl.BlockSpec((HALO, tf), lambda i, j: (jnp.maximum(i * per - 1, 0), j)),
                  pl.BlockSpec((HALO, tf), lambda i, j: (jnp.minimum((i + 1) * per, last), j)),
                  pl.BlockSpec((tm, tf), lambda i, j: (i, nf + j)),
                  pl.BlockSpec((3, tf), lambda i, j: (0, j)), pl.BlockSpec((1, tf), lambda i, j: (0, j))],
        out_specs=pl.BlockSpec((tm, tf), lambda i, j: (i, j)),
        scratch_shapes=[pltpu.VMEM((tm + 2 * HALO, tf), F32)],
        compiler_params=_params(("parallel", "parallel")), name=name)(z, z, z, z, conv_w, conv_b)


def _ffn_act_bwd(z, da, conv_w, conv_b, seg_t, name):
    M, F2 = z.shape
    Fd = F2 // 2
    tm = _row_tile(seg_t, M)
    tf = _tile(Fd, _TF)
    nf = Fd // tf
    per, last = tm // HALO, M // HALO - 1
    PAD = 8
    n_piece = tm // _RC

    def body(g_ref, gp_ref, gn_ref, u_ref, up_ref, un_ref, a_ref, ap_ref, an_ref, cw_ref, cb_ref,
             dz_ref, dcw_ref, dcb_ref, gs_ref, ds_ref, du_ref):
        i, p = pl.program_id(1), pl.program_id(2)

        @pl.when(p == 0)
        def _():
            hp, hn = _edge_flags(i, tm, seg_t, M)
            gs_ref[pl.ds(0, PAD), :] = jnp.zeros((PAD, tf), F32)
            gs_ref[pl.ds(PAD, HALO), :] = gp_ref[...].astype(F32) * hp
            gs_ref[pl.ds(PAD + HALO, tm), :] = g_ref[...].astype(F32)
            gs_ref[pl.ds(PAD + HALO + tm, HALO), :] = gn_ref[...].astype(F32) * hn
            gs_ref[pl.ds(PAD + 2 * HALO + tm, PAD), :] = jnp.zeros((PAD, tf), F32)

            @pl.when(i == 0)
            def _():
                dcw_ref[...] = jnp.zeros_like(dcw_ref)
                dcb_ref[...] = jnp.zeros_like(dcb_ref)

            def fold(v):
                return v[:8] + v[8:]

            for c0, cw in _col_groups(tf):
                cs = pl.ds(c0, cw)
                w0, w1, w2, cb = cw_ref[pl.ds(0, 1), cs], cw_ref[pl.ds(1, 1), cs], cw_ref[pl.ds(2, 1), cs], cb_ref[:, cs]
                acc = [jnp.zeros((8, cw), F32) for _ in range(4)]
                for ci in range(-1, n_piece + 1):
                    r0 = ci * _RC
                    taps = [gs_ref[pl.ds(PAD + HALO + r0 - 1 + k, _RC), cs] for k in range(3)]
                    gc = taps[0] * w0 + taps[1] * w1 + taps[2] * w2 + cb
                    if ci < 0:
                        ue, ae = up_ref[:, cs].astype(F32), ap_ref[:, cs].astype(F32) * hp
                    elif ci == n_piece:
                        ue, ae = un_ref[:, cs].astype(F32), an_ref[:, cs].astype(F32) * hn
                    else:
                        ue, ae = u_ref[pl.ds(r0, _RC), cs].astype(F32), a_ref[pl.ds(r0, _RC), cs].astype(F32)
                    sg = _sigmoid(gc)
                    dgc = ae * ue * (sg * (1.0 + gc * (1.0 - sg)))
                    ds_ref[pl.ds(HALO + r0, _RC), cs] = dgc
                    if 0 <= ci < n_piece:
                        du_ref[pl.ds(r0, _RC), cs] = (ae * gc * sg).astype(BF16)
                        for k in range(3):
                            acc[k] = acc[k] + fold(dgc * taps[k])
                        acc[3] = acc[3] + fold(dgc)
                for r0 in range(0, tm, _RC):
                    b = HALO + r0
                    dg = (ds_ref[pl.ds(b + 1, _RC), cs] * w0 + ds_ref[pl.ds(b, _RC), cs] * w1
                          + ds_ref[pl.ds(b - 1, _RC), cs] * w2)
                    dz_ref[pl.ds(r0, _RC), cs] = dg.astype(BF16)
                for k in range(3):
                    dcw_ref[pl.ds(k, 1), cs] += _colsum(acc[k])
                dcb_ref[:, cs] += _colsum(acc[3])

        @pl.when(p == 1)
        def _():
            dz_ref[...] = du_ref[...]

    def cur(off):
        return pl.BlockSpec((tm, tf), lambda j, i, p: (i, off + j))

    def prv(off):
        return pl.BlockSpec((HALO, tf), lambda j, i, p: (jnp.maximum(i * per - 1, 0), off + j))

    def nxt(off):
        return pl.BlockSpec((HALO, tf), lambda j, i, p: (jnp.minimum((i + 1) * per, last), off + j))

    return pl.pallas_call(
        body, out_shape=[SDS((M, F2), BF16), SDS((3, Fd), F32), SDS((1, Fd), F32)], grid=(nf, M // tm, 2),
        in_specs=[cur(0), prv(0), nxt(0), cur(nf), prv(nf), nxt(nf), cur(0), prv(0), nxt(0),
                  pl.BlockSpec((3, tf), lambda j, i, p: (0, j)), pl.BlockSpec((1, tf), lambda j, i, p: (0, j))],
        out_specs=[pl.BlockSpec((tm, tf), lambda j, i, p: (i, p * nf + j)),
                   pl.BlockSpec((3, tf), lambda j, i, p: (0, j)), pl.BlockSpec((1, tf), lambda j, i, p: (0, j))],
        scratch_shapes=[pltpu.VMEM((tm + 2 * HALO + 2 * PAD, tf), F32), pltpu.VMEM((tm + 2 * HALO, tf), F32),
                        pltpu.VMEM((tm, tf), BF16)],
        compiler_params=_params(("parallel", "arbitrary", "arbitrary")), name=name)(
            z, z, z, z, z, z, da, da, da, conv_w, conv_b)


_LN2 = math.log(2.0)
_QSCALE = (NOPE + ROPE) ** -0.5 / _LN2


def _swap32(x):
    lane = lax.broadcasted_iota(jnp.int32, x.shape, 1)
    return jnp.where((lane % 64) < 32, pltpu.roll(x, 96, axis=1), pltpu.roll(x, 32, axis=1))


def _rms(x, g):
    r = lax.rsqrt(jnp.mean(x * x, axis=-1, keepdims=True) + EPS)
    return x * r * g


def _rms_bwd(x, g, dy):
    r = lax.rsqrt(jnp.mean(x * x, axis=-1, keepdims=True) + EPS)
    xh = x * r
    dxh = dy * g
    return r * (dxh - xh * jnp.mean(dxh * xh, axis=-1, keepdims=True)), _colsum(dy * xh)


def _mla_prep_fwd(z, gq, gkv, cos, sin, QL, KL, name):
    M, NZ = z.shape
    tm = _tile(M, (256, 128))

    def body(z_ref, gq_ref, gkv_ref, cos_ref, sin_ref, cq_ref, ckv_ref, kpe_ref):
        zv = z_ref[...]
        cq_ref[...] = _rms(zv[:, :QL], gq_ref[...]).astype(BF16)
        ckv_ref[...] = _rms(zv[:, QL:QL + KL], gkv_ref[...]).astype(BF16)
        kp = zv[:, QL + KL:]
        r = kp * cos_ref[...] + _swap32(kp) * sin_ref[...]
        lane = lax.broadcasted_iota(jnp.int32, r.shape, 1)
        kpe_ref[0] = jnp.where(lane < ROPE, r, 0.0).astype(BF16)
        kpe_ref[1] = jnp.where(lane >= ROPE, r, 0.0).astype(BF16)

    tab = pl.BlockSpec((tm, 128), lambda i: (i, 0))
    return pl.pallas_call(
        body, out_shape=[SDS((M, QL), BF16), SDS((M, KL), BF16), SDS((2, M, 128), BF16)], grid=(M // tm,),
        in_specs=[pl.BlockSpec((tm, NZ), lambda i: (i, 0)), pl.BlockSpec((1, QL), lambda i: (0, 0)),
                  pl.BlockSpec((1, KL), lambda i: (0, 0)), tab, tab],
        out_specs=[pl.BlockSpec((tm, QL), lambda i: (i, 0)), pl.BlockSpec((tm, KL), lambda i: (i, 0)),
                   pl.BlockSpec((2, tm, 128), lambda i: (0, i, 0))],
        compiler_params=_params(("parallel",)), name=name)(z, gq, gkv, cos, sin)


def _mla_prep_bwd(z, dcq, dckv, dkpe, gq, gkv, cos, sin, QL, KL, seg_t, name):
    M, NZ = z.shape
    H = dkpe.shape[0]
    tm = _row_tile(seg_t, M)
    nt = seg_t // tm

    def body(z_ref, dcq_ref, dckv_ref, dkpe_ref, gq_ref, gkv_ref, cos_ref, sin_ref, dz_ref, dgq_ref, dgkv_ref):
        i = pl.program_id(0)

        @pl.when(i == 0)
        def _():
            dgq_ref[...] = jnp.zeros_like(dgq_ref)
            dgkv_ref[...] = jnp.zeros_like(dgkv_ref)

        zv = z_ref[...]
        dyq = jnp.where(i < nt, dcq_ref[...], 0.0)
        dxq, dgq = _rms_bwd(zv[:, :QL], gq_ref[...], dyq)
        dxkv, dgkv = _rms_bwd(zv[:, QL:QL + KL], gkv_ref[...], dckv_ref[...])
        dgq_ref[...] += dgq
        dgkv_ref[...] += dgkv
        even = dkpe_ref[0]
        odd = dkpe_ref[1]
        for h in range(2, H, 2):
            even = even + dkpe_ref[h]
            odd = odd + dkpe_ref[h + 1]
        lane = lax.broadcasted_iota(jnp.int32, even.shape, 1)
        dr = jnp.where(lane < ROPE, even, odd)
        dkp = dr * cos_ref[...] - _swap32(dr) * sin_ref[...]
        dz_ref[...] = jnp.concatenate([dxq, dxkv, dkp], axis=1).astype(BF16)

    tab = pl.BlockSpec((tm, 128), lambda i: (i, 0))
    return pl.pallas_call(
        body, out_shape=[SDS((M, NZ), BF16), SDS((1, QL), F32), SDS((1, KL), F32)], grid=(M // tm,),
        in_specs=[pl.BlockSpec((tm, NZ), lambda i: (i, 0)),
                  pl.BlockSpec((tm, QL), lambda i: (jnp.minimum(i, nt - 1), 0)),
                  pl.BlockSpec((tm, KL), lambda i: (i, 0)), pl.BlockSpec((H, tm, 128), lambda i: (0, i, 0)),
                  pl.BlockSpec((1, QL), lambda i: (0, 0)), pl.BlockSpec((1, KL), lambda i: (0, 0)), tab, tab],
        out_specs=[pl.BlockSpec((tm, NZ), lambda i: (i, 0)), pl.BlockSpec((1, QL), lambda i: (0, 0)),
                   pl.BlockSpec((1, KL), lambda i: (0, 0))],
        compiler_params=_params(("arbitrary",)), name=name)(z, dcq, dckv, dkpe, gq, gkv, cos, sin)


def _qrope_fwd(q, cos, sin, HN, name):
    T, NQ = q.shape
    tm = _tile(T, (256, 128))

    def body(q_ref, cos_ref, sin_ref, o_ref):
        o_ref[:, :HN] = (q_ref[:, :HN] * _QSCALE).astype(BF16)
        for cb in range((NQ - HN) // 128):
            sl = slice(HN + cb * 128, HN + (cb + 1) * 128)
            xv = q_ref[:, sl]
            o_ref[:, sl] = ((xv * cos_ref[...] + _swap32(xv) * sin_ref[...]) * _QSCALE).astype(BF16)

    tab = pl.BlockSpec((tm, 128), lambda i: (i, 0))
    return pl.pallas_call(
        body, out_shape=SDS((T, NQ), BF16), grid=(T // tm,),
        in_specs=[pl.BlockSpec((tm, NQ), lambda i: (i, 0)), tab, tab],
        out_specs=pl.BlockSpec((tm, NQ), lambda i: (i, 0)),
        compiler_params=_params(("parallel",)), name=name)(q, cos, sin)


def _qrope_bwd(dqpe, dqa, cos, sin, HN, name):
    T, HW = dqpe.shape
    HR = HW // 2
    tm = _tile(T, (256, 128))

    def body(d_ref, dqa_hbm, cos_ref, sin_ref, o_ref):
        del dqa_hbm
        for pr in range(HR // 128):
            dr = d_ref[:, 2 * pr * 128:(2 * pr + 1) * 128] + d_ref[:, (2 * pr + 1) * 128:(2 * pr + 2) * 128]
            o_ref[:, pr * 128:(pr + 1) * 128] = (dr * cos_ref[...] - _swap32(dr) * sin_ref[...]).astype(BF16)

    tab = pl.BlockSpec((tm, 128), lambda i: (i, 0))
    return pl.pallas_call(
        body, out_shape=SDS(dqa.shape, BF16), grid=(T // tm,),
        in_specs=[pl.BlockSpec((tm, HW), lambda i: (i, 0)), pl.BlockSpec(memory_space=pl.ANY), tab, tab],
        out_specs=pl.BlockSpec((tm, HR), lambda i: (i, HN // HR)),
        input_output_aliases={1: 0}, compiler_params=_params(("parallel",)), name=name)(dqpe, dqa, cos, sin)


_ATT_SUB = 4
_ATT_SUB_B = 2
_NT = (((1,), (1,)), ((), ()))
_TN = (((0,), (0,)), ((), ()))


def _attn_fwd(qa, kv, kpe, T, H, name, carry=None):
    M = kv.shape[0]
    tq = _tile(T, (1024, 512, 256, 128))
    scale = (NOPE + ROPE) ** -0.5

    def body(qn_ref, qp_ref, kv_ref, kpe_ref, o_ref, lse_ref, kc_ref):
        @pl.when(pl.program_id(1) == 0)
        def _():
            kc_ref[:, :NOPE] = kv_ref[:, :NOPE]
            kc_ref[:, NOPE:] = kpe_ref[0]

        rs = tq // _ATT_SUB
        outs, lses = [], []
        for u in range(_ATT_SUB):
            rows = pl.ds(u * rs, rs)
            qc = jnp.concatenate([qn_ref[rows, :], qp_ref[rows, :]], axis=1)
            s = lax.dot_general(qc, kc_ref[...], _NT, preferred_element_type=F32)
            m = jnp.max(s, axis=-1, keepdims=True)
            p = jnp.exp2(s - m)
            l = jnp.sum(p, axis=-1, keepdims=True)
            o = jnp.dot(p.astype(BF16), kv_ref[:, NOPE:], preferred_element_type=F32)
            outs.append((o / l).astype(BF16))
            lses.append(jnp.broadcast_to(m + jnp.log2(l), (rs, 128)))
        o_ref[...] = jnp.concatenate(outs, axis=0)
        lse_ref[...] = jnp.concatenate(lses, axis=0)

    return _carry_call(
        body, carry, out_shape=[SDS((T, H * VDIM), BF16), SDS((T, H * 128), F32)], grid=(H, T // tq),
        in_specs=[pl.BlockSpec((tq, NOPE), lambda h, i: (i, h)), pl.BlockSpec((tq, 128), lambda h, i: (i, H + h // 2)),
                  pl.BlockSpec((M, NOPE + VDIM), lambda h, i: (0, h)), pl.BlockSpec((1, M, 128), lambda h, i: (h % 2, 0, 0))],
        out_specs=[pl.BlockSpec((tq, VDIM), lambda h, i: (i, h)), pl.BlockSpec((tq, 128), lambda h, i: (i, h))],
        scratch_shapes=[pltpu.VMEM((M, NOPE + 128), BF16)],
        sem=("parallel", "arbitrary"), name=name, ins=(qa, qa, kv, kpe))


def _attn_bwd(qa, kv, kpe, do, lse, T, H, name, carry=None):
    M = kv.shape[0]
    tq = _tile(T, (512, 256, 128))
    nq = T // tq
    scale = (NOPE + ROPE) ** -0.5

    def body(qn_ref, qp_ref, kv_ref, kpe_ref, do_ref, lse_ref, dqa_ref, dqpe_ref, dkv_ref, dkpe_ref, kc_ref, dk_acc, dv_acc):
        i = pl.program_id(1)

        @pl.when(i == 0)
        def _():
            kc_ref[:, :NOPE] = kv_ref[:, :NOPE]
            kc_ref[:, NOPE:] = kpe_ref[0]
            dk_acc[...] = jnp.zeros_like(dk_acc)
            dv_acc[...] = jnp.zeros_like(dv_acc)

        rs = tq // _ATT_SUB_B
        dqs, dvs, dks = [], None, None
        for u in range(_ATT_SUB_B):
            rows = pl.ds(u * rs, rs)
            qc = jnp.concatenate([qn_ref[rows, :], qp_ref[rows, :]], axis=1)
            dov = do_ref[rows, :]
            s = lax.dot_general(qc, kc_ref[...], _NT, preferred_element_type=F32)
            p = jnp.exp2(s - lse_ref[rows, 0:1])
            dp = lax.dot_general(dov, kv_ref[:, NOPE:], _NT, preferred_element_type=F32)
            delta = jnp.sum(p * dp, axis=-1, keepdims=True)
            ds = (p * (dp - delta)).astype(BF16)
            dqs.append(jnp.dot(ds, kc_ref[...], preferred_element_type=F32) * scale)
            dv = lax.dot_general(p.astype(BF16), dov, _TN, preferred_element_type=F32)
            dk = lax.dot_general(ds, qc, _TN, preferred_element_type=F32)
            dvs = dv if dvs is None else dvs + dv
            dks = dk if dks is None else dks + dk
        dq = jnp.concatenate(dqs, axis=0)
        dqa_ref[...] = dq[:, :NOPE].astype(BF16)
        dqpe_ref[...] = dq[:, NOPE:]
        dv_acc[...] += dvs
        dk_acc[...] += dks

        @pl.when(i == nq - 1)
        def _():
            dkv_ref[:, :NOPE] = (dk_acc[:, :NOPE] * _LN2).astype(BF16)
            dkv_ref[:, NOPE:] = dv_acc[...].astype(BF16)
            dkpe_ref[0] = dk_acc[:, NOPE:] * _LN2

    return _carry_call(
        body, carry,
        out_shape=[SDS((T, H * (NOPE + ROPE)), BF16), SDS((T, H * 128), F32), SDS((M, H * (NOPE + VDIM)), BF16),
                   SDS((H, M, 128), F32)],
        grid=(H, nq),
        in_specs=[pl.BlockSpec((tq, NOPE), lambda h, i: (i, h)), pl.BlockSpec((tq, 128), lambda h, i: (i, H + h // 2)),
                  pl.BlockSpec((M, NOPE + VDIM), lambda h, i: (0, h)), pl.BlockSpec((1, M, 128), lambda h, i: (h % 2, 0, 0)),
                  pl.BlockSpec((tq, VDIM), lambda h, i: (i, h)), pl.BlockSpec((tq, 128), lambda h, i: (i, h))],
        out_specs=[pl.BlockSpec((tq, NOPE), lambda h, i: (i, h)), pl.BlockSpec((tq, 128), lambda h, i: (i, h)),
                   pl.BlockSpec((M, NOPE + VDIM), lambda h, i: (0, h)), pl.BlockSpec((1, M, 128), lambda h, i: (h, 0, 0))],
        scratch_shapes=[pltpu.VMEM((M, NOPE + 128), BF16), pltpu.VMEM((M, NOPE + 128), F32), pltpu.VMEM((M, VDIM), F32)],
        sem=("parallel", "arbitrary"), name=name, ins=(qa, qa, kv, kpe, do, lse))


def _adamw(w, m, v, name, g=None, recv=None, carry=None):
    R, C = w.shape
    summed = recv is not None
    n_recv = len(recv) if summed else 1
    runs = [r.shape[1] for r in recv] if summed else [R]
    tr = math.gcd(*runs)
    for cand in (1024, 512, 256, 128, 64, 32, 16, 8):
        if tr % cand == 0 and cand * C <= 131072:
            tr = cand
            break
    first = [sum(runs[:r]) // tr for r in range(n_recv + 1)]
    c1 = 1.0 - ADAM_B1 ** ADAM_STEP
    c2 = 1.0 - ADAM_B2 ** ADAM_STEP

    def update(gv, w_ref, m_ref, v_ref, d_ref, nm_ref, nv_ref):
        mn = ADAM_B1 * m_ref[...] + (1.0 - ADAM_B1) * gv
        vn = ADAM_B2 * v_ref[...] + (1.0 - ADAM_B2) * (gv * gv)
        nm_ref[...] = mn
        nv_ref[...] = vn
        d_ref[...] = -ADAM_LR * ((mn / c1) / (jnp.sqrt(vn / c2) + ADAM_EPS) + ADAM_WD * w_ref[...])

    def body(*refs):
        w_ref, m_ref, v_ref = refs[:3]
        g_refs = refs[3:3 + n_recv]
        outs = refs[3 + n_recv:]
        if not summed:
            update(g_refs[0][...], w_ref, m_ref, v_ref, *outs)
            return
        i = pl.program_id(0)
        for r in range(n_recv):
            @pl.when((i >= first[r]) & (i < first[r + 1]))
            def _():
                gv = g_refs[r][0].astype(F32)
                for d in range(1, N_DEV):
                    gv = gv + g_refs[r][d].astype(F32)
                outs[0][...] = gv
                update(gv, w_ref, m_ref, v_ref, *outs[1:])

    blk = pl.BlockSpec((tr, C), lambda i: (i, 0))
    if summed:
        g_specs = [pl.BlockSpec((N_DEV, tr, C), functools.partial(
            lambda i, lo, n: (0, jnp.clip(i - lo, 0, n - 1), 0), lo=first[r], n=first[r + 1] - first[r]))
                   for r in range(n_recv)]
    else:
        g_specs = [blk]
    n_out = 4 if summed else 3
    out, carried = _carry_call(
        body, carry, out_shape=[SDS((R, C), F32)] * n_out, grid=(R // tr,), in_specs=[blk, blk, blk] + g_specs,
        out_specs=[blk] * n_out, scratch_shapes=[], sem=("parallel",), name=name,
        ins=(w, m, v, *(recv if summed else [g])))
    return (out, carried) if carry else out


WEIGHTS = ['c_ctx', 'norm1_g', 'norm2_g', 'w_ada', 'b_ada', 'ab_w_in', 'ab_b_in', 'a_ln_g', 'a_ln_b', 'a_w_s', 'a_b_s',
           'b_conv_w', 'b_conv_b', 'b_ln_g', 'b_ln_b', 'ab_w_out', 'mla_w_in', 'mla_q_norm_g', 'mla_w_uq',
           'mla_kv_norm_g', 'mla_w_ukv', 'mla_w_o', 'ffn_w_up', 'ffn_conv_w', 'ffn_conv_b', 'ffn_w_down', 'final_norm_g']


def _pack(parts):
    flat = jnp.concatenate([p.reshape(-1).astype(F32) for p in parts])
    n = flat.shape[0]
    unit = 65536 if n > 65536 else 1024
    n_pad = -(-n // unit) * unit
    return jnp.pad(flat, (0, n_pad - n)).reshape(n_pad // 128, 128)


def _unpack(flat, like):
    out, off = [], 0
    for shp in like:
        n = math.prod(shp)
        out.append(flat[..., off:off + n].reshape(flat.shape[:-1] + tuple(shp)))
        off += n
    return out


def _rope_tables(T, Tc):
    rows = T // GRID_W
    row = jnp.repeat(jnp.arange(rows, dtype=F32), GRID_W)
    col = jnp.tile(jnp.arange(GRID_W, dtype=F32), rows)
    n_freq = ROPE // 4
    inv = ROPE_THETA ** (-jnp.arange(n_freq, dtype=F32) / n_freq)
    ang = jnp.concatenate([row[:, None] * inv, col[:, None] * inv], axis=-1)
    cos, sin = jnp.cos(ang), jnp.sin(ang)
    cos = jnp.tile(cos, (1, 128 // (ROPE // 2)))
    sin = jnp.tile(jnp.concatenate([-sin, sin], axis=1), (1, 128 // ROPE))
    return (jnp.concatenate([cos, jnp.ones((Tc, 128), F32)], axis=0),
            jnp.concatenate([sin, jnp.zeros((Tc, 128), F32)], axis=0))


def _step(a):
    ax, ay, ac = lax.axis_index("x"), lax.axis_index("y"), lax.axis_index("c")
    me = 4 * ax + 2 * ay + ac
    T, D = a['x'].shape[1:]
    Tc = a['ctx'].shape[1]
    M = T + Tc
    W, Wb = a['a_ln_g'].shape[1], a['b_ln_g'].shape[1]
    assert W == Wb and T % Tc == 0
    Fd = a['ffn_conv_b'].shape[1]
    QL, KL = a['mla_q_norm_g'].shape[1] * N_DEV, a['mla_kv_norm_g'].shape[1] * N_DEV
    H = a['mla_w_ukv'].shape[2] * N_DEV // (NOPE + VDIM)
    HN, HR = H * NOPE, H * ROPE
    kw = a['b_conv_w'].shape[1]
    NA = a['w_ada'].shape[2]
    bf = lambda t: t.astype(BF16)

    small_shapes = [(D,), (kw, Wb // N_DEV), (2, 3, Fd // N_DEV), (QL // N_DEV,), (KL // N_DEV,)]
    g_small = _all_gather(_pack([a['c'][0], a['b_conv_w'][0], a['ffn_conv_w'], a['mla_q_norm_g'][0], a['mla_kv_norm_g'][0]]),
                          "ag_small")
    c_all, bcw, fcw, gq, gkv = _unpack(g_small.reshape(N_DEV, -1), small_shapes)
    bcw = jnp.transpose(bcw, (1, 0, 2)).reshape(kw, Wb)
    fcw = jnp.transpose(fcw, (1, 2, 0, 3)).reshape(2, 3, Fd)
    gq, gkv = gq.reshape(1, QL), gkv.reshape(1, KL)

    a16 = jnp.concatenate([c_all, a['c_ctx'][None], jnp.zeros((N_DEV - 1, D), F32)], axis=0)
    b_loc = lax.dynamic_slice(a['b_ada'], (0, me * NA), (2, NA))
    mods = [_mm(a16, a['w_ada'][l], mode="nn", out_dtype=F32, name=f"ada_fwd{l}", bias=b_loc[l:l + 1], a_silu=True)
            for l in range(2)]
    gm = _all_gather(jnp.concatenate(mods, axis=0), "ag_mod").reshape(N_DEV, 2, 2 * N_DEV, NA)
    gm = jnp.transpose(gm, (1, 2, 0, 3)).reshape(2, 2 * N_DEV, 6 * D)
    mod_lat = [lax.dynamic_slice(gm[l], (me, 0), (1, 6 * D)).reshape(6, 1, 1, D) for l in range(2)]
    mod_ctx = [gm[l][N_DEV].reshape(6, 1, 1, D) for l in range(2)]

    def mod(l, k, both):
        return jnp.concatenate([mod_lat[l][k], mod_ctx[l][k]], axis=0) if both else mod_lat[l][k]

    def from_cols(g):
        return jnp.transpose(g, (1, 0, 2)).reshape(g.shape[1], -1)

    def from_rows(g):
        return g.reshape(-1, g.shape[2])

    def ag(x):
        return (x, False)

    def a2a(x):
        return (x, True)

    cos, sin = _rope_tables(T, Tc)
    n1g, n2g = a['norm1_g'], a['norm2_g']
    a_bst = a['a_b_s'][0].T
    mm = functools.partial(_mm)
    up_sh, dn_sh = bf(a['ffn_w_up']), bf(a['ffn_w_down'])

    w_abin = _all_gather(bf(a['ab_w_in'][0]), "ag_ab_w_in")
    x0 = jnp.concatenate([a['x'][0], a['ctx'][0]], axis=0)
    h1 = _normmod_fwd(x0, n1g[0:1], mod(0, 0, True), mod(0, 1, True), T, "l0_norm1")
    z, (g_about,) = mm(h1, w_abin, mode="nn", out_dtype=F32, name="l0_ab_in", bias=a['ab_b_in'], b_dev=True,
                       carry=[ag(bf(a['ab_w_out'][0]))])
    w_about = from_rows(g_about)
    y = _gmlp_fwd(z, a['a_ln_g'], a['a_ln_b'], a['a_w_s'][0], a_bst, W, "l0_gmlp")
    y, (g_up0a,) = _conf_fwd(z, y, bcw, a['b_conv_b'], a['b_ln_g'], a['b_ln_b'], W, Wb, T, "l0_conf",
                             carry=[ag(up_sh[0][:D // 2])])
    (x1, o1), (g_up0b,) = mm(y, w_about, mode="nn", out_dtype=F32, name="l0_ab_out", res=x0, gate=mod(0, 2, True),
                             seg_t=T, carry=[ag(up_sh[0][D // 2:])])
    w_up = [jnp.concatenate([g_up0a, g_up0b], axis=1), None]
    h2 = _normmod_fwd(x1, n2g[0:1], mod(0, 3, True), mod(0, 4, True), T, "l0_norm2")
    z2, (g_dn0,) = mm(h2, w_up[0], mode="nn", out_dtype=BF16, name="l0_up", b_dev=True, carry=[ag(dn_sh[0])])
    w_dn = [from_rows(g_dn0), None]
    a2 = _ffn_act_fwd(z2, fcw[0], a['ffn_conv_b'][0:1], T, "l0_act")
    (x2, o2), (g_in, g_uq) = mm(a2, w_dn[0], mode="nn", out_dtype=F32, name="l0_down", res=x1, gate=mod(0, 5, True),
                                seg_t=T, carry=[ag(bf(a['mla_w_in'][0])), ag(bf(a['mla_w_uq'][0]))])
    w_in = from_rows(g_in)
    w_in = jnp.concatenate([w_in, w_in[:, QL + KL:]], axis=1)
    w_uq = from_cols(g_uq).reshape(QL, H, NOPE + ROPE)
    w_uq = jnp.concatenate([w_uq[:, :, :NOPE].reshape(QL, HN), w_uq[:, :, NOPE:].reshape(QL, HR)], axis=1)

    h3 = _normmod_fwd(x2, n1g[1:2], mod(1, 0, True), mod(1, 1, True), T, "l1_norm1")
    z3, (g_ukv,) = mm(h3, w_in, mode="nn", out_dtype=F32, name="l1_mla_in", carry=[ag(bf(a['mla_w_ukv'][0]))])
    w_ukv = g_ukv
    cqn, ckvn, kpe = _mla_prep_fwd(z3, gq, gkv, cos, sin, QL, KL, "l1_prep")
    q, (g_wo,) = mm(cqn, w_uq, mode="nn", out_dtype=F32, name="l1_uq", rows=T, carry=[ag(bf(a['mla_w_o'][0]))])
    w_o = from_rows(g_wo)
    kv = mm(ckvn, w_ukv, mode="nn", out_dtype=BF16, name="l1_ukv", b_dev=True)
    qa = _qrope_fwd(q, cos, sin, HN, "l1_qrope")
    (o_att, lse), (g_up1,) = _attn_fwd(qa, kv, kpe, T, H, "l1_attn", carry=[ag(up_sh[1])])
    w_up[1] = g_up1
    x3, o3 = mm(o_att, w_o, mode="nn", out_dtype=F32, name="l1_wo", res=x2, gate=mod(1, 2, False), seg_t=T)
    h4 = _normmod_fwd(x3, n2g[1:2], mod(1, 3, False), mod(1, 4, False), T, "l1_norm2")
    z4, (g_dn1,) = mm(h4, w_up[1], mode="nn", out_dtype=BF16, name="l1_up", b_dev=True, carry=[ag(dn_sh[1])])
    w_dn[1] = from_rows(g_dn1)
    a4 = _ffn_act_fwd(z4, fcw[1], a['ffn_conv_b'][1:2], T, "l1_act")
    x4, o4 = mm(a4, w_dn[1], mode="nn", out_dtype=F32, name="l1_down", res=x3, gate=mod(1, 5, False), seg_t=T)

    dx4, loss_cols, d_fng, do4, dg2_1 = _final(x4, a['final_norm_g'][None], a['loss_target'][0], o4, mod(1, 5, False),
                                               "final")
    loss = lax.psum(jnp.sum(loss_cols), ("x", "y", "c"))

    def cols(dw):
        k, n = dw.shape
        return jnp.transpose(dw.reshape(k, N_DEV, n // N_DEV), (1, 0, 2))

    def rows(dw):
        return dw.reshape(N_DEV, dw.shape[0] // N_DEV, dw.shape[1])

    da4 = mm(do4, w_dn[1], mode="nt", out_dtype=BF16, name="l1_down_dx")
    dw_dn1 = mm(a4, do4, mode="tn", out_dtype=BF16, name="l1_down_dw")
    dz4, dfcw1, dfcb1 = _ffn_act_bwd(z4, da4, fcw[1], a['ffn_conv_b'][1:2], T, "l1_act_bwd")
    dw_up1, (r_dn1,) = mm(h4, dz4, mode="tn", out_dtype=BF16, name="l1_up_dw", out_dev=True,
                          carry=[a2a(rows(dw_dn1))])
    dh4 = mm(dz4, w_up[1], mode="nt", out_dtype=F32, name="l1_up_dx", b_dev=True)
    dx3, dn2g1, dsh2_1, dsc2_1, do3, dg1_1 = _normmod_bwd(x3, n2g[1:2], mod(1, 4, False), dh4, dx4, T, "l1_norm2_bwd",
                                                         o_prev=o3, gate_prev=mod(1, 2, False))
    d_oatt = mm(do3, w_o, mode="nt", out_dtype=BF16, name="l1_wo_dx")
    dw_o = mm(o_att, do3, mode="tn", out_dtype=BF16, name="l1_wo_dw")
    (dqa, dqpe, dkv, dkpe), (r_up1, r_wo) = _attn_bwd(qa, kv, kpe, d_oatt, lse, T, H, "l1_attn_bwd",
                                                      carry=[a2a(dw_up1), a2a(rows(dw_o))])
    dqa = _qrope_bwd(dqpe, dqa, cos, sin, HN, "l1_qrope_bwd")
    dcq = mm(dqa, w_uq, mode="nt", out_dtype=F32, name="l1_uq_dx")
    dw_uq = mm(cqn, dqa, mode="tn", out_dtype=BF16, name="l1_uq_dw", rows=T)
    dw_uq = jnp.concatenate([dw_uq[:, :HN].reshape(QL, H, NOPE), dw_uq[:, HN:].reshape(QL, H, ROPE)], axis=2)
    dw_uq = dw_uq.reshape(QL, H * (NOPE + ROPE))
    dckv = mm(dkv, w_ukv, mode="nt", out_dtype=F32, name="l1_ukv_dx", b_dev=True)
    dw_ukv = mm(ckvn, dkv, mode="tn", out_dtype=BF16, name="l1_ukv_dw", out_dev=True)
    dz3, dgq, dgkv = _mla_prep_bwd(z3, dcq, dckv, dkpe, gq, gkv, cos, sin, QL, KL, T, "l1_prep_bwd")
    dh3 = mm(dz3, w_in, mode="nt", out_dtype=F32, name="l1_mla_in_dx")
    dw_in = mm(h3, dz3, mode="tn", out_dtype=BF16, name="l1_mla_in_dw").astype(F32)
    dw_in = jnp.concatenate([dw_in[:, :QL + KL], dw_in[:, QL + KL:QL + KL + ROPE] + dw_in[:, QL + KL + ROPE:QL + KL + 2 * ROPE]],
                            axis=1).astype(BF16)
    dx2, dn1g1, dsh1_1, dsc1_1, do2, dg2_0 = _normmod_bwd(x2, n1g[1:2], mod(1, 1, True), dh3, dx3, T, "l1_norm1_bwd",
                                                         o_prev=o2, gate_prev=mod(0, 5, True))
    da2, (r_uq, r_ukv) = mm(do2, w_dn[0], mode="nt", out_dtype=BF16, name="l0_down_dx",
                            carry=[a2a(cols(dw_uq)), a2a(dw_ukv)])
    dw_dn0, (r_in,) = mm(a2, do2, mode="tn", out_dtype=BF16, name="l0_down_dw", carry=[a2a(rows(dw_in))])
    dz2, dfcw0, dfcb0 = _ffn_act_bwd(z2, da2, fcw[0], a['ffn_conv_b'][0:1], T, "l0_act_bwd")
    dw_up0, (r_dn0,) = mm(h2, dz2, mode="tn", out_dtype=BF16, name="l0_up_dw", out_dev=True,
                          carry=[a2a(rows(dw_dn0))])
    dh2, (r_up0a,) = mm(dz2, w_up[0], mode="nt", out_dtype=F32, name="l0_up_dx", b_dev=True,
                        carry=[a2a(dw_up0[:, :D // 2])])
    dx1, dn2g0, dsh2_0, dsc2_0, do1, dg1_0 = _normmod_bwd(x1, n2g[0:1], mod(0, 4, True), dh2, dx2, T, "l0_norm2_bwd",
                                                         o_prev=o1, gate_prev=mod(0, 2, True))
    dy = mm(do1, w_about, mode="nt", out_dtype=F32, name="l0_ab_out_dx")
    dw_about = mm(y, do1, mode="tn", out_dtype=BF16, name="l0_ab_out_dw")
    dz, dlag, dlab, dws, dbs, dbin_a = _gmlp_bwd(z, dy, a['a_ln_g'], a['a_ln_b'], a['a_w_s'][0], a_bst, W, "l0_gmlp_bwd")
    dhc, dlbg, dlbb, dbcb = _conf_bwd1(z, dy, bcw, a['b_conv_b'], a['b_ln_g'], a['b_ln_b'], W, Wb, T, "l0_conf_bwd1")
    (dz, dbcw, dbin_b), (r_up0b,) = _conf_bwd2(z, dhc, dz, bcw, W, Wb, T, "l0_conf_bwd2",
                                               carry=[a2a(dw_up0[:, D // 2:])])
    dh1, (r_about,) = mm(dz, w_abin, mode="nt", out_dtype=F32, name="l0_ab_in_dx", b_dev=True,
                         carry=[a2a(rows(dw_about))])
    dw_abin_a = mm(h1, dz, mode="tn", out_dtype=BF16, name="l0_ab_in_dw_a", out_dev=True, p_range=(0, D // 2))
    dw_abin_b, (r_abin_a,) = mm(h1, dz, mode="tn", out_dtype=BF16, name="l0_ab_in_dw_b", out_dev=True,
                                p_range=(D // 2, D // 2), carry=[a2a(dw_abin_a)])
    dx0, dn1g0, dsh1_0, dsc1_0 = _normmod_bwd(x0, n1g[0:1], mod(0, 1, True), dh1, dx1, T, "l0_norm1_bwd")

    zero = jnp.zeros((D,), F32)
    dmod = jnp.stack([
        jnp.stack([jnp.stack([dsh1_0[0, 0], dsc1_0[0, 0], dg1_0[0, 0], dsh2_0[0, 0], dsc2_0[0, 0], dg2_0[0, 0]]),
                   jnp.stack([dsh1_0[1, 0], dsc1_0[1, 0], dg1_0[1, 0], dsh2_0[1, 0], dsc2_0[1, 0], dg2_0[1, 0]])]),
        jnp.stack([jnp.stack([dsh1_1[0, 0], dsc1_1[0, 0], dg1_1[0, 0], dsh2_1[0, 0], dsc2_1[0, 0], dg2_1[0, 0]]),
                   jnp.stack([dsh1_1[1, 0], dsc1_1[1, 0], zero, zero, zero, zero])])])
    small = {
        'norm1_g': jnp.concatenate([dn1g0, dn1g1], axis=0), 'norm2_g': jnp.concatenate([dn2g0, dn2g1], axis=0),
        'ab_b_in': jnp.concatenate([dbin_a, dbin_b], axis=1), 'a_ln_g': dlag, 'a_ln_b': dlab, 'a_w_s': dws[None],
        'a_b_s': jnp.sum(dbs, axis=-1)[None], 'b_conv_w': dbcw, 'b_conv_b': dbcb, 'b_ln_g': dlbg, 'b_ln_b': dlbb,
        'mla_q_norm_g': dgq, 'mla_kv_norm_g': dgkv, 'ffn_conv_w': jnp.stack([dfcw0, dfcw1]),
        'ffn_conv_b': jnp.concatenate([dfcb0, dfcb1], axis=0), 'final_norm_g': d_fng[0],
    }
    names = list(small)
    g2 = _all_gather(_pack([dmod] + [small[n] for n in names]), "ag_small_grads")
    red = _sum_lead(g2, "sum_small_grads").reshape(-1)
    red = dict(zip(names, _unpack(red, [(2, 2, 6, D)] + [small[n].shape for n in names])[1:]))
    dmod_all = g2.reshape(N_DEV, -1)[:, :2 * 2 * 6 * D].reshape(N_DEV, 2, 2, 6 * D)

    a16g = jnp.concatenate([c_all, jnp.tile(a['c_ctx'][None], (N_DEV, 1))], axis=0)
    dm_loc = lax.dynamic_slice(dmod_all, (0, 0, 0, me * NA), (N_DEV, 2, 2, NA))
    g_wada, cpart = [], []
    for l in range(2):
        dm16 = jnp.concatenate([dm_loc[:, l, 0], dm_loc[:, l, 1]], axis=0)
        g_wada.append(mm(a16g, dm16, mode="tn", out_dtype=F32, name=f"ada_dw{l}", a_silu=True))
        cpart.append(mm(dm_loc[:, l, 1], a['w_ada'][l], mode="nt", out_dtype=F32, name=f"ada_dc{l}"))
    g_bada = _sum_lead(jnp.transpose(dmod_all, (0, 2, 1, 3)).reshape(2 * N_DEV, 2 * 6 * D // 128, 128), "sum_b_ada")
    g_cc = _all_gather(jnp.concatenate(cpart, axis=0), "ag_c_ctx")
    g_cc = _sum_lead(g_cc.reshape(2 * N_DEV * N_DEV, D // 128, 128), "sum_c_ctx").reshape(D)
    grads = {
        'c_ctx': g_cc * _dsilu(a['c_ctx']), 'w_ada': jnp.stack(g_wada), 'b_ada': g_bada.reshape(2, 6 * D),
        'b_conv_w': lax.dynamic_slice(red['b_conv_w'], (0, me * (Wb // N_DEV)), (kw, Wb // N_DEV))[None],
        'ffn_conv_w': lax.dynamic_slice(red['ffn_conv_w'], (0, 0, me * (Fd // N_DEV)), (2, 3, Fd // N_DEV)),
        'mla_q_norm_g': lax.dynamic_slice(red['mla_q_norm_g'], (0, me * (QL // N_DEV)), (1, QL // N_DEV)),
        'mla_kv_norm_g': lax.dynamic_slice(red['mla_kv_norm_g'], (0, me * (KL // N_DEV)), (1, KL // N_DEV)),
    }
    for n in names:
        if n not in grads:
            grads[n] = red[n].reshape(a[n].shape)

    recvs = {'ab_w_out': [r_about], 'mla_w_in': [r_in], 'mla_w_uq': [r_uq], 'mla_w_ukv': [r_ukv],
             'mla_w_o': [r_wo], 'ffn_w_up': [r_up0a, r_up0b, r_up1], 'ffn_w_down': [r_dn0, r_dn1]}
    out = {}
    for n in WEIGHTS:
        shp = a[n].shape
        w2 = a[n].reshape(-1, shp[-1])
        m2, v2 = a['m_' + n].reshape(w2.shape), a['v_' + n].reshape(w2.shape)
        if n in recvs:
            res = _adamw(w2, m2, v2, "adamw_" + n, recv=recvs[n])
        elif n == 'w_ada':
            g2d = grads[n].reshape(w2.shape)
            res, (r_abin_b,) = _adamw(w2, m2, v2, "adamw_" + n, g=g2d, carry=[a2a(dw_abin_b)])
            res = (g2d,) + tuple(res)
            recvs['ab_w_in'] = [r_abin_a, r_abin_b]
        else:
            g2d = grads[n].reshape(w2.shape)
            res = (g2d,) + tuple(_adamw(w2, m2, v2, "adamw_" + n, g=g2d))
        out[n] = [r.reshape(shp) for r in res]
    return (loss, dx0[:T][None], *[out[n][0] for n in WEIGHTS], *[out[n][1] for n in WEIGHTS],
            *[out[n][2] for n in WEIGHTS], *[out[n][3] for n in WEIGHTS])


def kernel(x, c, ctx, c_ctx, norm1_g, norm2_g, w_ada, b_ada, ab_w_in, ab_b_in, a_ln_g, a_ln_b, a_w_s, a_b_s, b_conv_w, b_conv_b, b_ln_g, b_ln_b, ab_w_out, mla_w_in, mla_q_norm_g, mla_w_uq, mla_kv_norm_g, mla_w_ukv, mla_w_o, ffn_w_up, ffn_conv_w, ffn_conv_b, ffn_w_down, final_norm_g, loss_target, m_c_ctx, m_norm1_g, m_norm2_g, m_w_ada, m_b_ada, m_ab_w_in, m_ab_b_in, m_a_ln_g, m_a_ln_b, m_a_w_s, m_a_b_s, m_b_conv_w, m_b_conv_b, m_b_ln_g, m_b_ln_b, m_ab_w_out, m_mla_w_in, m_mla_q_norm_g, m_mla_w_uq, m_mla_kv_norm_g, m_mla_w_ukv, m_mla_w_o, m_ffn_w_up, m_ffn_conv_w, m_ffn_conv_b, m_ffn_w_down, m_final_norm_g, v_c_ctx, v_norm1_g, v_norm2_g, v_w_ada, v_b_ada, v_ab_w_in, v_ab_b_in, v_a_ln_g, v_a_ln_b, v_a_w_s, v_a_b_s, v_b_conv_w, v_b_conv_b, v_b_ln_g, v_b_ln_b, v_ab_w_out, v_mla_w_in, v_mla_q_norm_g, v_mla_w_uq, v_mla_kv_norm_g, v_mla_w_ukv, v_mla_w_o, v_ffn_w_up, v_ffn_conv_w, v_ffn_conv_b, v_ffn_w_down, v_final_norm_g):
    return _step(dict(locals()))
```

```python
import functools
import math

import jax
import jax.numpy as jnp
from jax import lax
from jax.experimental import pallas as pl
from jax.experimental.pallas import tpu as pltpu

F32 = jnp.float32
BF16 = jnp.bfloat16
SDS = jax.ShapeDtypeStruct

N_DEV = 8
EPS = 1e-6
CHUNK = 128
NOPE = 128
ROPE = 64
VDIM = 128
GRID_W = 64
ROPE_THETA = 10000.0
HALO = 16
ADAM_LR, ADAM_B1, ADAM_B2, ADAM_EPS, ADAM_WD, ADAM_STEP = 0.001, 0.9, 0.999, 1e-08, 0.01, 10
VMEM_LIMIT = 56 * 1024 * 1024


def _tile(n, prefs):
    for p in prefs:
        if n % p == 0:
            return p
    return n


def _params(sem, vmem=VMEM_LIMIT):
    return pltpu.CompilerParams(dimension_semantics=sem, vmem_limit_bytes=vmem)


def _sigmoid(x):
    return 0.5 * jnp.tanh(0.5 * x) + 0.5


def _silu(x):
    return x * _sigmoid(x)


def _dsilu(x):
    s = _sigmoid(x)
    return s * (1.0 + x * (1.0 - s))


_GELU_C = math.sqrt(2.0 / math.pi)


def _gelu(x):
    return 0.5 * x * (1.0 + jnp.tanh(_GELU_C * (x + 0.044715 * x * x * x)))


def _dgelu(x):
    t = jnp.tanh(_GELU_C * (x + 0.044715 * x * x * x))
    return 0.5 * (1.0 + t) + 0.5 * x * (1.0 - t * t) * _GELU_C * (1.0 + 3.0 * 0.044715 * x * x)


def _colsum(v):
    return jnp.sum(v, axis=0, keepdims=True)


_SIBLING = 1
_CHIPS = (2, 4, 6)


def _xchg(x_ref, o_ref, send_sems, recv_sems, local_sem, scatter):
    ax, ay, ac = lax.axis_index("x"), lax.axis_index("y"), lax.axis_index("c")
    me = 4 * ax + 2 * ay + ac

    def dev(k):
        return ax ^ (k >> 2), ay ^ ((k >> 1) & 1), ac ^ (k & 1)

    def idx(k):
        px, py, pc = dev(k)
        return 4 * px + 2 * py + pc

    def copy(k, src, dst, to):
        return pltpu.make_async_remote_copy(src_ref=src, dst_ref=dst, send_sem=send_sems.at[k - 1],
                                            recv_sem=recv_sems.at[k - 1], device_id=dev(to),
                                            device_id_type=pl.DeviceIdType.MESH)

    def own():
        return pltpu.make_async_copy(x_ref.at[me] if scatter else x_ref, o_ref.at[me], local_sem)

    def sends():
        if scatter:
            return [copy(k, x_ref.at[idx(k)], o_ref.at[me], k) for k in range(1, N_DEV)]
        return [copy(k, x_ref, o_ref.at[me], k) for k in (_SIBLING,) + _CHIPS]

    def forwards():
        return [] if scatter else [copy(j + 1, o_ref.at[idx(j)], o_ref.at[idx(j)], _SIBLING) for j in _CHIPS]

    def arrival(k):
        return copy(k, o_ref.at[idx(k)], o_ref.at[idx(k)], k)

    return own, sends, forwards, arrival


def _xchg_start(*refs, scatter):
    own, sends, _, _ = _xchg(*refs, scatter)
    own().start()
    for cp in sends():
        cp.start()


def _xchg_forward(*refs, scatter):
    _, _, forwards, arrival = _xchg(*refs, scatter)
    if not scatter:
        for j, fw in zip(_CHIPS, forwards()):
            arrival(j).wait_recv()
            fw.start()


def _xchg_finish(*refs, scatter):
    own, sends, forwards, arrival = _xchg(*refs, scatter)
    for k in range(1, N_DEV):
        if scatter or k not in _CHIPS:
            arrival(k).wait_recv()
    for cp in sends() + forwards():
        cp.wait_send()
    own().wait()


_XCHG_SEMS = [pltpu.SemaphoreType.DMA((N_DEV - 1,)), pltpu.SemaphoreType.DMA((N_DEV - 1,)), pltpu.SemaphoreType.DMA]


def _xchg_shape(x, scatter):
    return SDS((N_DEV,) + tuple(x.shape[1:] if scatter else x.shape), x.dtype)


def _exchange(x, *, scatter, name):
    def body(*refs):
        _xchg_start(*refs, scatter=scatter)
        _xchg_forward(*refs, scatter=scatter)
        _xchg_finish(*refs, scatter=scatter)

    return pl.pallas_call(
        body, out_shape=_xchg_shape(x, scatter),
        in_specs=[pl.BlockSpec(memory_space=pl.ANY)], out_specs=pl.BlockSpec(memory_space=pl.ANY),
        scratch_shapes=list(_XCHG_SEMS), name=name)(x)


def _carried(body, carry, n_in, n_out, n_scratch, grid):
    nc = len(carry)
    total = math.prod(grid)
    mid = (3 * total) // 4

    def wrapped(*refs):
        ins, cin = refs[:n_in], refs[n_in:n_in + nc]
        o0 = n_in + nc
        outs, cout = refs[o0:o0 + n_out], refs[o0 + n_out:o0 + n_out + nc]
        scr = refs[o0 + n_out + nc:]
        sems = scr[n_scratch:]
        step = pl.program_id(0)
        for ax in range(1, len(grid)):
            step = step * grid[ax] + pl.program_id(ax)

        def each(fn):
            for c in range(nc):
                fn(cin[c], cout[c], *sems[3 * c:3 * c + 3], scatter=carry[c][1])

        @pl.when(step == 0)
        def _():
            each(_xchg_start)

        body(*ins, *outs, *scr[:n_scratch])

        if mid < total - 1:
            @pl.when(step == mid)
            def _():
                each(_xchg_forward)

        @pl.when(step == total - 1)
        def _():
            if mid >= total - 1:
                each(_xchg_forward)
            each(_xchg_finish)

    return wrapped


def _carry_call(body, carry, *, grid, out_shape, in_specs, out_specs, scratch_shapes, sem, name, ins, aliases=None):
    carry = carry or []
    nc = len(carry)
    if nc:
        body = _carried(body, carry, len(in_specs), len(out_shape), len(scratch_shapes), grid)
        anyspec = pl.BlockSpec(memory_space=pl.ANY)
        in_specs = list(in_specs) + [anyspec] * nc
        out_specs = list(out_specs) + [anyspec] * nc
        out_shape = list(out_shape) + [_xchg_shape(x, sc) for x, sc in carry]
        scratch_shapes = list(scratch_shapes) + list(_XCHG_SEMS) * nc
        ins = list(ins) + [x for x, _ in carry]
        sem = ("arbitrary",) * len(grid)
    out = pl.pallas_call(body, out_shape=out_shape, grid=grid, in_specs=in_specs, out_specs=out_specs,
                         scratch_shapes=scratch_shapes, compiler_params=_params(sem), name=name,
                         input_output_aliases=aliases or {})(*ins)
    n_main = len(out) - nc
    return list(out[:n_main]), list(out[n_main:])


def _all_gather(x, name):
    return _exchange(x, scatter=False, name=name)


def _all_to_all(x, name):
    return _exchange(x, scatter=True, name=name)


def _sum_lead(x, name):
    n, R, C = x.shape
    tr = _tile(R, (512, 256, 128, 64, 32, 16, 8))

    def body(x_ref, o_ref):
        acc = x_ref[0]
        for d in range(1, n):
            acc = acc + x_ref[d]
        o_ref[...] = acc

    return pl.pallas_call(
        body, out_shape=SDS((R, C), F32), grid=(R // tr,),
        in_specs=[pl.BlockSpec((n, tr, C), lambda i: (0, i, 0))], out_specs=pl.BlockSpec((tr, C), lambda i: (i, 0)),
        compiler_params=_params(("parallel",)), name=name)(x)


_TP = (1408, 1088, 1024, 768, 512, 256, 128)
_TQ = (1408, 1024, 768, 512, 256, 128)
_TR = (2048, 1408, 1024, 768, 512, 256, 128)
_TR_TN = (1088, 1024, 512, 256, 128)


def _mm(a, b, *, mode, out_dtype, name, rows=None, bias=None, res=None, gate=None, seg_t=None, a_silu=False, carry=None,
        b_dev=False, out_dev=False, p_range=None):
    if mode == "nn":
        P, R, Q = rows or a.shape[0], a.shape[1], (b.shape[0] * b.shape[2] if b_dev else b.shape[1])
    elif mode == "nt":
        P, R, Q = rows or a.shape[0], a.shape[1], (b.shape[1] if b_dev else b.shape[0])
    else:
        R, P, Q = rows or a.shape[0], a.shape[1], b.shape[1]
    p0 = 0
    if p_range is not None:
        p0, P = p_range
    tp = _tile(P, _TP)
    tq = _tile(Q // N_DEV if (out_dev or (b_dev and mode == "nn")) else Q, _TQ)
    tr = _tile(R // N_DEV if (b_dev and mode == "nt") else R, _TR if mode != "tn" else _TR_TN)
    nk = R // tr
    qd = (Q // N_DEV) // tq
    rd = (R // N_DEV) // tr
    if mode == "nn":
        a_spec = pl.BlockSpec((tp, tr), lambda i, j, k: (i, k))
        b_spec = (pl.BlockSpec((None, tr, tq), lambda i, j, k: (j // qd, k, j % qd)) if b_dev
                  else pl.BlockSpec((tr, tq), lambda i, j, k: (k, j)))
        dims = (((1,), (0,)), ((), ()))
    elif mode == "nt":
        a_spec = pl.BlockSpec((tp, tr), lambda i, j, k: (i, k))
        b_spec = (pl.BlockSpec((None, tq, tr), lambda i, j, k: (k // rd, j, k % rd)) if b_dev
                  else pl.BlockSpec((tq, tr), lambda i, j, k: (j, k)))
        dims = (((1,), (1,)), ((), ()))
    else:
        pb = p0 // tp
        a_spec = pl.BlockSpec((tr, tp), lambda i, j, k: (k, i + pb))
        b_spec = pl.BlockSpec((tr, tq), lambda i, j, k: (k, j))
        dims = (((0,), (0,)), ((), ()))
    ins, in_specs = [a, b], [a_spec, b_spec]
    if bias is not None:
        ins.append(bias)
        in_specs.append(pl.BlockSpec((1, tq), lambda i, j, k: (0, j)))
    gated = res is not None
    if gated:
        n_seg = gate.shape[0]
        ins += [res, gate]
        in_specs += [pl.BlockSpec((tp, tq), lambda i, j, k: (i, j)),
                     pl.BlockSpec((n_seg, 1, tq), lambda i, j, k: (0, 0, j))]
    if out_dev:
        out_shape = [SDS((N_DEV, P, Q // N_DEV), out_dtype)]
        out_specs = [pl.BlockSpec((None, tp, tq), lambda i, j, k: (j // qd, i, j % qd))]
    else:
        out_shape = [SDS((P, Q), out_dtype)]
        out_specs = [pl.BlockSpec((tp, tq), lambda i, j, k: (i, j))]
    if gated:
        out_shape.append(SDS((P, Q), BF16))
        out_specs.append(pl.BlockSpec((tp, tq), lambda i, j, k: (i, j)))

    def body(*refs):
        a_ref, b_ref = refs[0], refs[1]
        pos = 2
        bias_ref = res_ref = gate_ref = o2_ref = None
        if bias is not None:
            bias_ref = refs[pos]
            pos += 1
        if gated:
            res_ref, gate_ref = refs[pos], refs[pos + 1]
            pos += 2
        o_ref = refs[pos]
        pos += 1
        if gated:
            o2_ref = refs[pos]
            pos += 1
        acc_ref = refs[pos] if nk > 1 else None
        k = pl.program_id(2)
        av = a_ref[...]
        if a_silu:
            av = _silu(av.astype(F32))
        part = lax.dot_general(av.astype(BF16), b_ref[...].astype(BF16), dims, preferred_element_type=F32)
        if nk > 1:
            @pl.when(k == 0)
            def _():
                acc_ref[...] = part

            @pl.when(k > 0)
            def _():
                acc_ref[...] += part

        @pl.when(k == nk - 1)
        def _():
            acc = acc_ref[...] if nk > 1 else part
            if bias_ref is not None:
                acc = acc + bias_ref[...]
            if gated:
                if n_seg == 1:
                    g = gate_ref[0]
                else:
                    row = pl.program_id(0) * tp + lax.broadcasted_iota(jnp.int32, (tp, 1), 0)
                    g = jnp.where(row < seg_t, gate_ref[0], gate_ref[1])
                o_ref[...] = (res_ref[...] + g * acc).astype(o_ref.dtype)
                o2_ref[...] = acc.astype(BF16)
            else:
                o_ref[...] = acc.astype(o_ref.dtype)

    out, carried = _carry_call(
        body, carry, grid=(P // tp, Q // tq, nk), out_shape=out_shape, in_specs=in_specs, out_specs=out_specs,
        scratch_shapes=[pltpu.VMEM((tp, tq), F32)] if nk > 1 else [], sem=("parallel", "parallel", "arbitrary"),
        name=name, ins=ins)
    res_out = tuple(out) if gated else out[0]
    return (res_out, carried) if carry else res_out


def _row_tile(seg_t, m):
    return 256 if (seg_t % 256 == 0 and m % 256 == 0) else 128


def _normmod_fwd(x, g, sh, sc, seg_t, name):
    M, D = x.shape
    tm = _row_tile(seg_t, M)
    n_seg = sh.shape[0]
    nt = seg_t // tm

    def seg(i):
        return ((i >= nt).astype(jnp.int32) if n_seg == 2 else 0, 0, 0)

    def body(x_ref, g_ref, sh_ref, sc_ref, o_ref):
        xv = x_ref[...]
        r = lax.rsqrt(jnp.mean(xv * xv, axis=-1, keepdims=True) + EPS)
        y = xv * r * g_ref[...]
        o_ref[...] = (y * (1.0 + sc_ref[0]) + sh_ref[0]).astype(BF16)

    return pl.pallas_call(
        body, out_shape=SDS((M, D), BF16), grid=(M // tm,),
        in_specs=[pl.BlockSpec((tm, D), lambda i: (i, 0)), pl.BlockSpec((1, D), lambda i: (0, 0)),
                  pl.BlockSpec((1, 1, D), seg), pl.BlockSpec((1, 1, D), seg)],
        out_specs=pl.BlockSpec((tm, D), lambda i: (i, 0)),
        compiler_params=_params(("parallel",)), name=name)(x, g, sh, sc)


def _normmod_bwd(x, g, sc, dh, dx_in, seg_t, name, o_prev=None, gate_prev=None):
    M, D = x.shape
    tm = _row_tile(seg_t, M)
    n_seg = sc.shape[0]
    nt = seg_t // tm
    n_in = dx_in.shape[0] // tm
    with_prev = o_prev is not None
    n_segp = gate_prev.shape[0] if with_prev else 0

    def seg(i):
        return ((i >= nt).astype(jnp.int32) if n_seg == 2 else 0, 0, 0)

    def segp(i):
        return ((i >= nt).astype(jnp.int32) if n_segp == 2 else 0, 0, 0)

    def body(*refs):
        x_ref, g_ref, sc_ref, dh_ref, dxin_ref = refs[:5]
        pos = 5
        if with_prev:
            op_ref, gp_ref = refs[5], refs[6]
            pos = 7
        dx_ref, dg_ref, dsh_ref, dsc_ref = refs[pos:pos + 4]
        if with_prev:
            dop_ref, dgp_ref = refs[pos + 4], refs[pos + 5]
        i = pl.program_id(0)
        xv = x_ref[...]
        r = lax.rsqrt(jnp.mean(xv * xv, axis=-1, keepdims=True) + EPS)
        xh = xv * r
        gv = g_ref[...]
        dhv = dh_ref[...].astype(F32)
        dy = dhv * (1.0 + sc_ref[0])
        dxh = dy * gv
        dxv = r * (dxh - xh * jnp.mean(dxh * xh, axis=-1, keepdims=True))
        if n_in * tm < M:
            dxv = dxv + jnp.where(i < n_in, dxin_ref[...], 0.0)
        else:
            dxv = dxv + dxin_ref[...]
        dx_ref[...] = dxv

        @pl.when(i == 0)
        def _():
            dg_ref[...] = jnp.zeros_like(dg_ref)

        first_of_seg = (i == 0) | (i == nt) if n_seg == 2 else (i == 0)

        @pl.when(first_of_seg)
        def _():
            dsh_ref[...] = jnp.zeros_like(dsh_ref)
            dsc_ref[...] = jnp.zeros_like(dsc_ref)

        dg_ref[...] += _colsum(dy * xh)
        dsh_ref[0] += _colsum(dhv)
        dsc_ref[0] += _colsum(dhv * xh * gv)
        if with_prev:
            first_of_segp = (i == 0) | (i == nt) if n_segp == 2 else (i == 0)

            @pl.when(first_of_segp)
            def _():
                dgp_ref[...] = jnp.zeros_like(dgp_ref)

            dop_ref[...] = (gp_ref[0] * dxv).astype(BF16)
            dgp_ref[0] += _colsum(dxv * op_ref[...].astype(F32))

    row = pl.BlockSpec((tm, D), lambda i: (i, 0))
    ins = [x, g, sc, dh, dx_in]
    in_specs = [row, pl.BlockSpec((1, D), lambda i: (0, 0)), pl.BlockSpec((1, 1, D), seg), row,
                pl.BlockSpec((tm, D), lambda i: (jnp.minimum(i, n_in - 1), 0))]
    out_shape = [SDS((M, D), F32), SDS((1, D), F32), SDS((n_seg, 1, D), F32), SDS((n_seg, 1, D), F32)]
    out_specs = [row, pl.BlockSpec((1, D), lambda i: (0, 0)), pl.BlockSpec((1, 1, D), seg), pl.BlockSpec((1, 1, D), seg)]
    if with_prev:
        ins += [o_prev, gate_prev]
        in_specs += [row, pl.BlockSpec((1, 1, D), segp)]
        out_shape += [SDS((M, D), BF16), SDS((n_segp, 1, D), F32)]
        out_specs += [row, pl.BlockSpec((1, 1, D), segp)]
    return pl.pallas_call(
        body, out_shape=out_shape, grid=(M // tm,), in_specs=in_specs, out_specs=out_specs,
        compiler_params=_params(("arbitrary",)), name=name)(*ins)


def _final(x, g, target, o_prev, gate_prev, name):
    T, D = x.shape
    tm = _tile(T, (256, 128))

    def body(x_ref, g_ref, t_ref, op_ref, gp_ref, dx_ref, loss_ref, dg_ref, dop_ref, dgp_ref):
        i = pl.program_id(0)
        xv = x_ref[...]
        r = lax.rsqrt(jnp.mean(xv * xv, axis=-1, keepdims=True) + EPS)
        xh = xv * r
        gv = g_ref[...]
        e = xh * gv - t_ref[...]
        dout = e * (1.0 / D)
        dxh = dout * gv
        dxv = r * (dxh - xh * jnp.mean(dxh * xh, axis=-1, keepdims=True))
        dx_ref[...] = dxv
        dop_ref[...] = (gp_ref[0] * dxv).astype(BF16)

        @pl.when(i == 0)
        def _():
            loss_ref[...] = jnp.zeros_like(loss_ref)
            dg_ref[...] = jnp.zeros_like(dg_ref)
            dgp_ref[...] = jnp.zeros_like(dgp_ref)

        loss_ref[...] += _colsum(e * e) * (0.5 / D)
        dg_ref[...] += _colsum(dout * xh)
        dgp_ref[0] += _colsum(dxv * op_ref[...].astype(F32))

    row = pl.BlockSpec((tm, D), lambda i: (i, 0))
    vec = pl.BlockSpec((1, D), lambda i: (0, 0))
    vec3 = pl.BlockSpec((1, 1, D), lambda i: (0, 0, 0))
    return pl.pallas_call(
        body, out_shape=[SDS((T, D), F32), SDS((1, D), F32), SDS((1, D), F32), SDS((T, D), BF16), SDS((1, 1, D), F32)],
        grid=(T // tm,), in_specs=[row, vec, row, row, vec3], out_specs=[row, vec, vec, row, vec3],
        compiler_params=_params(("arbitrary",)), name=name)(x, g, target, o_prev, gate_prev)


def _gmlp_core(z, lg, lb, ws_ref, bst):
    W = z.shape[1] // 2
    t = _gelu(z)
    u, v = t[:, :W], t[:, W:]
    mu = jnp.mean(v, axis=-1, keepdims=True)
    vc = v - mu
    rstd = lax.rsqrt(jnp.mean(vc * vc, axis=-1, keepdims=True) + EPS)
    vhat = vc * rstd
    vn = vhat * lg + lb
    vp = []
    for h in range(W // CHUNK):
        blk = vn[:, h * CHUNK:(h + 1) * CHUNK].astype(BF16)
        vp.append(jnp.dot(ws_ref[h].astype(BF16), blk, preferred_element_type=F32) + bst[:, h:h + 1])
    return u, vhat, rstd, vp


def _gmlp_fwd(z, ln_g, ln_b, w_s, b_st, W, name):
    M = z.shape[0]
    H = W // CHUNK

    def body(z_ref, lg_ref, lb_ref, ws_ref, bst_ref, o_ref):
        u, _, _, vp = _gmlp_core(z_ref[...], lg_ref[...], lb_ref[...], ws_ref, bst_ref[...])
        for h in range(H):
            o_ref[:, h * CHUNK:(h + 1) * CHUNK] = (u[:, h * CHUNK:(h + 1) * CHUNK] * vp[h]).astype(BF16)

    vec = pl.BlockSpec((1, W), lambda i: (0, 0))
    return pl.pallas_call(
        body, out_shape=SDS((M, 2 * W), BF16), grid=(M // CHUNK,),
        in_specs=[pl.BlockSpec((CHUNK, 2 * W), lambda i: (i, 0)), vec, vec,
                  pl.BlockSpec((H, CHUNK, CHUNK), lambda i: (0, 0, 0)), pl.BlockSpec((CHUNK, H), lambda i: (0, 0))],
        out_specs=pl.BlockSpec((CHUNK, W), lambda i: (i, 0)),
        compiler_params=_params(("parallel",)), name=name)(z, ln_g, ln_b, w_s, b_st)


def _gmlp_bwd(z, dy, ln_g, ln_b, w_s, b_st, W, name):
    M = z.shape[0]
    H = W // CHUNK
    ZW = z.shape[1]

    def body(z_ref, dy_ref, lg_ref, lb_ref, ws_ref, bst_ref, dz_ref, dlg_ref, dlb_ref, dws_ref, dbs_ref, dbin_ref):
        i = pl.program_id(0)

        @pl.when(i == 0)
        def _():
            for r in (dlg_ref, dlb_ref, dws_ref, dbs_ref, dbin_ref):
                r[...] = jnp.zeros_like(r)

        zv = z_ref[...]
        lg = lg_ref[...]
        u, vhat, rstd, vp = _gmlp_core(zv, lg, lb_ref[...], ws_ref, bst_ref[...])
        vn = vhat * lg + lb_ref[...]
        dya = dy_ref[...]
        du_parts, dvn_parts = [], []
        for h in range(H):
            sl = slice(h * CHUNK, (h + 1) * CHUNK)
            dya_h = dya[:, sl]
            du_parts.append(dya_h * vp[h])
            dvp = dya_h * u[:, sl]
            dbs_ref[h] += dvp
            dvp16 = dvp.astype(BF16)
            dws_ref[h] += lax.dot_general(dvp16, vn[:, sl].astype(BF16), (((1,), (1,)), ((), ())),
                                          preferred_element_type=F32)
            dvn_parts.append(lax.dot_general(ws_ref[h].astype(BF16), dvp16, (((0,), (0,)), ((), ())),
                                             preferred_element_type=F32))
        du = jnp.concatenate(du_parts, axis=1)
        dvn = jnp.concatenate(dvn_parts, axis=1)
        dlg_ref[...] += _colsum(dvn * vhat)
        dlb_ref[...] += _colsum(dvn)
        dvh = dvn * lg
        dv = rstd * (dvh - jnp.mean(dvh, axis=-1, keepdims=True) - vhat * jnp.mean(dvh * vhat, axis=-1, keepdims=True))
        dz = jnp.concatenate([du, dv], axis=1) * _dgelu(zv)
        dbin_ref[...] += _colsum(dz)
        dz_ref[...] = dz.astype(BF16)

    vec = pl.BlockSpec((1, W), lambda i: (0, 0))
    mat = pl.BlockSpec((H, CHUNK, CHUNK), lambda i: (0, 0, 0))
    return pl.pallas_call(
        body,
        out_shape=[SDS((M, ZW), BF16), SDS((1, W), F32), SDS((1, W), F32), SDS((H, CHUNK, CHUNK), F32),
                   SDS((H, CHUNK, CHUNK), F32), SDS((1, 2 * W), F32)],
        grid=(M // CHUNK,),
        in_specs=[pl.BlockSpec((CHUNK, 2 * W), lambda i: (i, 0)), pl.BlockSpec((CHUNK, W), lambda i: (i, 0)), vec, vec,
                  mat, pl.BlockSpec((CHUNK, H), lambda i: (0, 0))],
        out_specs=[pl.BlockSpec((CHUNK, 2 * W), lambda i: (i, 0)), vec, vec, mat, mat,
                   pl.BlockSpec((1, 2 * W), lambda i: (0, 0))],
        compiler_params=_params(("arbitrary",)), name=name)(z, dy, ln_g, ln_b, w_s, b_st)


def _halo_specs(tm, width, col, n_rows):
    per = tm // HALO
    last = n_rows // HALO - 1
    prev = pl.BlockSpec((HALO, width), lambda i: (jnp.maximum(i * per - 1, 0), col))
    nxt = pl.BlockSpec((HALO, width), lambda i: (jnp.minimum((i + 1) * per, last), col))
    return prev, nxt


def _edge_flags(i, tm, seg_t, m):
    r0 = i * tm
    has_prev = jnp.where((r0 == 0) | (r0 == seg_t), 0.0, 1.0)
    has_next = jnp.where((r0 + tm == seg_t) | (r0 + tm == m), 0.0, 1.0)
    return has_prev, has_next


def _glu(zz, wb):
    return zz[:, :wb] * _sigmoid(zz[:, wb:])


def _build_shifts(src_ref, sh_ref):
    n = src_ref.shape[0] - 8
    for r in range(1, 8):
        sh_ref[r - 1, pl.ds(0, n), :] = src_ref[pl.ds(r, n), :]


def _shifted(src_ref, sh_ref, off, r0, rc):
    a, r = divmod(off, 8)
    if r == 0:
        return src_ref[pl.ds(8 * a + r0, rc), :]
    return sh_ref[r - 1, pl.ds(8 * a + r0, rc), :]


def _conv_taps(src_ref, sh_ref, w_ref, first, tm, kw, flip=False):
    rc = 32
    parts = []
    for c in range(tm // rc):
        acc = None
        for k in range(kw):
            wk = w_ref[pl.ds(kw - 1 - k if flip else k, 1), :]
            term = _shifted(src_ref, sh_ref, first + k, c * rc, rc) * wk
            acc = term if acc is None else acc + term
        parts.append(acc)
    return jnp.concatenate(parts, axis=0)


def _conf_fwd(z, y, conv_w, conv_b, ln_g, ln_b, W, Wb, seg_t, name, carry=None):
    M = z.shape[0]
    tm = _row_tile(seg_t, M)
    kw = conv_w.shape[0]
    pad = (kw - 1) // 2
    col = (2 * W) // (2 * Wb)

    def body(zc_ref, zp_ref, zn_ref, y_hbm, cw_ref, cb_ref, lg_ref, lb_ref, o_ref, hc_ref, hs_ref, sh_ref):
        del y_hbm
        hp, hn = _edge_flags(pl.program_id(0), tm, seg_t, M)
        hs_ref[pl.ds(0, HALO), :] = _glu(zp_ref[...], Wb) * hp
        hs_ref[pl.ds(HALO, tm), :] = _glu(zc_ref[...], Wb)
        hs_ref[pl.ds(HALO + tm, HALO), :] = _glu(zn_ref[...], Wb) * hn
        _build_shifts(hs_ref, sh_ref)
        hc = _conv_taps(hs_ref, sh_ref, cw_ref, HALO - pad, tm, kw) + cb_ref[...]
        hc_ref[...] = hc
        mu = jnp.mean(hc, axis=-1, keepdims=True)
        c = hc - mu
        rstd = lax.rsqrt(jnp.mean(c * c, axis=-1, keepdims=True) + EPS)
        o_ref[...] = _silu(c * rstd * lg_ref[...] + lb_ref[...]).astype(BF16)

    prev, nxt = _halo_specs(tm, 2 * Wb, col, M)
    vec = pl.BlockSpec((1, Wb), lambda i: (0, 0))
    out, carried = _carry_call(
        body, carry, out_shape=[SDS(y.shape, BF16), SDS((M, Wb), F32)], grid=(M // tm,),
        in_specs=[pl.BlockSpec((tm, 2 * Wb), lambda i: (i, col)), prev, nxt, pl.BlockSpec(memory_space=pl.ANY),
                  pl.BlockSpec((kw, Wb), lambda i: (0, 0)), vec, vec, vec],
        out_specs=[pl.BlockSpec((tm, Wb), lambda i: (i, W // Wb)), pl.BlockSpec((tm, Wb), lambda i: (i, 0))],
        scratch_shapes=[pltpu.VMEM((tm + 2 * HALO, Wb), F32), pltpu.VMEM((7, tm + 2 * HALO, Wb), F32)],
        aliases={3: 0}, sem=("parallel",), name=name, ins=(z, z, z, y, conv_w, conv_b, ln_g, ln_b))
    return (out, carried) if carry else out


def _conf_bwd1(hc, dy, ln_g, ln_b, W, Wb, seg_t, name):
    M = hc.shape[0]
    tm = _row_tile(seg_t, M)

    def body(hc_ref, dy_ref, lg_ref, lb_ref, dhc_ref, dlg_ref, dlb_ref, dcb_ref):
        i = pl.program_id(0)

        @pl.when(i == 0)
        def _():
            for r in (dlg_ref, dlb_ref, dcb_ref):
                r[...] = jnp.zeros_like(r)

        hc = hc_ref[...]
        mu = jnp.mean(hc, axis=-1, keepdims=True)
        c = hc - mu
        rstd = lax.rsqrt(jnp.mean(c * c, axis=-1, keepdims=True) + EPS)
        hh = c * rstd
        lg = lg_ref[...]
        dhn = dy_ref[...] * _dsilu(hh * lg + lb_ref[...])
        dlg_ref[...] += _colsum(dhn * hh)
        dlb_ref[...] += _colsum(dhn)
        dhh = dhn * lg
        dhc = rstd * (dhh - jnp.mean(dhh, axis=-1, keepdims=True) - hh * jnp.mean(dhh * hh, axis=-1, keepdims=True))
        dcb_ref[...] += _colsum(dhc)
        dhc_ref[...] = dhc

    vec = pl.BlockSpec((1, Wb), lambda i: (0, 0))
    return pl.pallas_call(
        body, out_shape=[SDS((M, Wb), F32), SDS((1, Wb), F32), SDS((1, Wb), F32), SDS((1, Wb), F32)], grid=(M // tm,),
        in_specs=[pl.BlockSpec((tm, Wb), lambda i: (i, 0)), pl.BlockSpec((tm, Wb), lambda i: (i, W // Wb)), vec, vec],
        out_specs=[pl.BlockSpec((tm, Wb), lambda i: (i, 0)), vec, vec, vec],
        compiler_params=_params(("arbitrary",)), name=name)(hc, dy, ln_g, ln_b)


def _conf_bwd2(z, dhc, dz, conv_w, W, Wb, seg_t, name, carry=None):
    M = z.shape[0]
    tm = _row_tile(seg_t, M)
    kw = conv_w.shape[0]
    pad = (kw - 1) // 2
    col = (2 * W) // (2 * Wb)

    def body(zc_ref, zp_ref, zn_ref, dc_ref, dp_ref, dn_ref, dz_hbm, cw_ref, dz_ref, dcw_ref, dbin_ref, hs_ref, ds_ref,
             hsh_ref, dsh_ref):
        del dz_hbm
        i = pl.program_id(0)

        @pl.when(i == 0)
        def _():
            dcw_ref[...] = jnp.zeros_like(dcw_ref)
            dbin_ref[...] = jnp.zeros_like(dbin_ref)

        hp, hn = _edge_flags(i, tm, seg_t, M)
        zc = zc_ref[...]
        hs_ref[pl.ds(0, HALO), :] = _glu(zp_ref[...], Wb) * hp
        hs_ref[pl.ds(HALO, tm), :] = _glu(zc, Wb)
        hs_ref[pl.ds(HALO + tm, HALO), :] = _glu(zn_ref[...], Wb) * hn
        dcur = dc_ref[...]
        ds_ref[pl.ds(0, HALO), :] = dp_ref[...] * hp
        ds_ref[pl.ds(HALO, tm), :] = dcur
        ds_ref[pl.ds(HALO + tm, HALO), :] = dn_ref[...] * hn
        _build_shifts(ds_ref, dsh_ref)
        _build_shifts(hs_ref, hsh_ref)
        dh = _conv_taps(ds_ref, dsh_ref, cw_ref, HALO - pad, tm, kw, flip=True)
        for k in range(kw):
            dcw_ref[pl.ds(k, 1), :] += _colsum(dcur * _shifted(hs_ref, hsh_ref, HALO - pad + k, 0, tm))
        a, gt = zc[:, :Wb], zc[:, Wb:]
        s = _sigmoid(gt)
        dz = jnp.concatenate([dh * s, dh * a * s * (1.0 - s)], axis=1)
        dbin_ref[...] += _colsum(dz)
        dz_ref[...] = dz.astype(BF16)

    prev, nxt = _halo_specs(tm, 2 * Wb, col, M)
    dprev, dnxt = _halo_specs(tm, Wb, 0, M)
    out, carried = _carry_call(
        body, carry, out_shape=[SDS(dz.shape, BF16), SDS((kw, Wb), F32), SDS((1, 2 * Wb), F32)], grid=(M // tm,),
        in_specs=[pl.BlockSpec((tm, 2 * Wb), lambda i: (i, col)), prev, nxt,
                  pl.BlockSpec((tm, Wb), lambda i: (i, 0)), dprev, dnxt, pl.BlockSpec(memory_space=pl.ANY),
                  pl.BlockSpec((kw, Wb), lambda i: (0, 0))],
        out_specs=[pl.BlockSpec((tm, 2 * Wb), lambda i: (i, col)), pl.BlockSpec((kw, Wb), lambda i: (0, 0)),
                   pl.BlockSpec((1, 2 * Wb), lambda i: (0, 0))],
        scratch_shapes=[pltpu.VMEM((tm + 2 * HALO, Wb), F32), pltpu.VMEM((tm + 2 * HALO, Wb), F32),
                        pltpu.VMEM((7, tm + 2 * HALO, Wb), F32), pltpu.VMEM((7, tm + 2 * HALO, Wb), F32)],
        aliases={6: 0}, sem=("arbitrary",), name=name, ins=(z, z, z, dhc, dhc, dhc, dz, conv_w))
    return (out, carried) if carry else out


_TF = (1408, 512, 256, 128)
_RC = 16
_CG = 256


def _col_groups(width):
    return [(c0, min(_CG, width - c0)) for c0 in range(0, width, _CG)]


def _ffn_act_fwd(z, conv_w, conv_b, seg_t, name):
    M, F2 = z.shape
    Fd = F2 // 2
    tm = _row_tile(seg_t, M)
    tf = _tile(Fd, _TF)
    nf = Fd // tf
    per, last = tm // HALO, M // HALO - 1

    def body(g_ref, gp_ref, gn_ref, u_ref, cw_ref, cb_ref, o_ref, gs_ref):
        hp, hn = _edge_flags(pl.program_id(0), tm, seg_t, M)
        gs_ref[pl.ds(0, HALO), :] = gp_ref[...].astype(F32) * hp
        gs_ref[pl.ds(HALO, tm), :] = g_ref[...].astype(F32)
        gs_ref[pl.ds(HALO + tm, HALO), :] = gn_ref[...].astype(F32) * hn
        for c0, cw in _col_groups(tf):
            cs = pl.ds(c0, cw)
            w0, w1, w2, cb = cw_ref[pl.ds(0, 1), cs], cw_ref[pl.ds(1, 1), cs], cw_ref[pl.ds(2, 1), cs], cb_ref[:, cs]
            for r0 in range(0, tm, _RC):
                gc = (gs_ref[pl.ds(HALO - 1 + r0, _RC), cs] * w0 + gs_ref[pl.ds(HALO + r0, _RC), cs] * w1
                      + gs_ref[pl.ds(HALO + 1 + r0, _RC), cs] * w2 + cb)
                o_ref[pl.ds(r0, _RC), cs] = (_silu(gc) * u_ref[pl.ds(r0, _RC), cs].astype(F32)).astype(BF16)

    return pl.pallas_call(
        body, out_shape=SDS((M, Fd), BF16), grid=(M // tm, nf),
        in_specs=[pl.BlockSpec((tm, tf), lambda i, j: (i, j)),
                  pl.BlockSpec((HALO, tf), lambda i, j: (jnp.maximum(i * per - 1, 0), j)),
                  pl.BlockSpec((HALO, tf), lambda i, j: (jnp.minimum((i + 1) * per, last), j)),
                  pl.BlockSpec((tm, tf), lambda i, j: (i, nf + j)),
                  pl.BlockSpec((3, tf), lambda i, j: (0, j)), pl.BlockSpec((1, tf), lambda i, j: (0, j))],
        out_specs=pl.BlockSpec((tm, tf), lambda i, j: (i, j)),
        scratch_shapes=[pltpu.VMEM((tm + 2 * HALO, tf), F32)],
        compiler_params=_params(("parallel", "parallel")), name=name)(z, z, z, z, conv_w, conv_b)


def _ffn_act_bwd(z, da, conv_w, conv_b, seg_t, name):
    M, F2 = z.shape
    Fd = F2 // 2
    tm = _row_tile(seg_t, M)
    tf = _tile(Fd, _TF)
    nf = Fd // tf
    per, last = tm // HALO, M // HALO - 1
    PAD = 8
    n_piece = tm // _RC

    def body(g_ref, gp_ref, gn_ref, u_ref, up_ref, un_ref, a_ref, ap_ref, an_ref, cw_ref, cb_ref,
             dz_ref, dcw_ref, dcb_ref, gs_ref, ds_ref, du_ref):
        i, p = pl.program_id(1), pl.program_id(2)

        @pl.when(p == 0)
        def _():
            hp, hn = _edge_flags(i, tm, seg_t, M)
            gs_ref[pl.ds(0, PAD), :] = jnp.zeros((PAD, tf), F32)
            gs_ref[pl.ds(PAD, HALO), :] = gp_ref[...].astype(F32) * hp
            gs_ref[pl.ds(PAD + HALO, tm), :] = g_ref[...].astype(F32)
            gs_ref[pl.ds(PAD + HALO + tm, HALO), :] = gn_ref[...].astype(F32) * hn
            gs_ref[pl.ds(PAD + 2 * HALO + tm, PAD), :] = jnp.zeros((PAD, tf), F32)

            @pl.when(i == 0)
            def _():
                dcw_ref[...] = jnp.zeros_like(dcw_ref)
                dcb_ref[...] = jnp.zeros_like(dcb_ref)

            def fold(v):
                return v[:8] + v[8:]

            for c0, cw in _col_groups(tf):
                cs = pl.ds(c0, cw)
                w0, w1, w2, cb = cw_ref[pl.ds(0, 1), cs], cw_ref[pl.ds(1, 1), cs], cw_ref[pl.ds(2, 1), cs], cb_ref[:, cs]
                acc = [jnp.zeros((8, cw), F32) for _ in range(4)]
                for ci in range(-1, n_piece + 1):
                    r0 = ci * _RC
                    taps = [gs_ref[pl.ds(PAD + HALO + r0 - 1 + k, _RC), cs] for k in range(3)]
                    gc = taps[0] * w0 + taps[1] * w1 + taps[2] * w2 + cb
                    if ci < 0:
                        ue, ae = up_ref[:, cs].astype(F32), ap_ref[:, cs].astype(F32) * hp
                    elif ci == n_piece:
                        ue, ae = un_ref[:, cs].astype(F32), an_ref[:, cs].astype(F32) * hn
                    else:
                        ue, ae = u_ref[pl.ds(r0, _RC), cs].astype(F32), a_ref[pl.ds(r0, _RC), cs].astype(F32)
                    sg = _sigmoid(gc)
                    dgc = ae * ue * (sg * (1.0 + gc * (1.0 - sg)))
                    ds_ref[pl.ds(HALO + r0, _RC), cs] = dgc
                    if 0 <= ci < n_piece:
                        du_ref[pl.ds(r0, _RC), cs] = (ae * gc * sg).astype(BF16)
                        for k in range(3):
                            acc[k] = acc[k] + fold(dgc * taps[k])
                        acc[3] = acc[3] + fold(dgc)
                for r0 in range(0, tm, _RC):
                    b = HALO + r0
                    dg = (ds_ref[pl.ds(b + 1, _RC), cs] * w0 + ds_ref[pl.ds(b, _RC), cs] * w1
                          + ds_ref[pl.ds(b - 1, _RC), cs] * w2)
                    dz_ref[pl.ds(r0, _RC), cs] = dg.astype(BF16)
                for k in range(3):
                    dcw_ref[pl.ds(k, 1), cs] += _colsum(acc[k])
                dcb_ref[:, cs] += _colsum(acc[3])

        @pl.when(p == 1)
        def _():
            dz_ref[...] = du_ref[...]

    def cur(off):
        return pl.BlockSpec((tm, tf), lambda j, i, p: (i, off + j))

    def prv(off):
        return pl.BlockSpec((HALO, tf), lambda j, i, p: (jnp.maximum(i * per - 1, 0), off + j))

    def nxt(off):
        return pl.BlockSpec((HALO, tf), lambda j, i, p: (jnp.minimum((i + 1) * per, last), off + j))

    return pl.pallas_call(
        body, out_shape=[SDS((M, F2), BF16), SDS((3, Fd), F32), SDS((1, Fd), F32)], grid=(nf, M // tm, 2),
        in_specs=[cur(0), prv(0), nxt(0), cur(nf), prv(nf), nxt(nf), cur(0), prv(0), nxt(0),
                  pl.BlockSpec((3, tf), lambda j, i, p: (0, j)), pl.BlockSpec((1, tf), lambda j, i, p: (0, j))],
        out_specs=[pl.BlockSpec((tm, tf), lambda j, i, p: (i, p * nf + j)),
                   pl.BlockSpec((3, tf), lambda j, i, p: (0, j)), pl.BlockSpec((1, tf), lambda j, i, p: (0, j))],
        scratch_shapes=[pltpu.VMEM((tm + 2 * HALO + 2 * PAD, tf), F32), pltpu.VMEM((tm + 2 * HALO, tf), F32),
                        pltpu.VMEM((tm, tf), BF16)],
        compiler_params=_params(("parallel", "arbitrary", "arbitrary")), name=name)(
            z, z, z, z, z, z, da, da, da, conv_w, conv_b)


_LN2 = math.log(2.0)
_QSCALE = (NOPE + ROPE) ** -0.5 / _LN2


def _swap32(x):
    lane = lax.broadcasted_iota(jnp.int32, x.shape, 1)
    return jnp.where((lane % 64) < 32, pltpu.roll(x, 96, axis=1), pltpu.roll(x, 32, axis=1))


def _rms(x, g):
    r = lax.rsqrt(jnp.mean(x * x, axis=-1, keepdims=True) + EPS)
    return x * r * g


def _rms_bwd(x, g, dy):
    r = lax.rsqrt(jnp.mean(x * x, axis=-1, keepdims=True) + EPS)
    xh = x * r
    dxh = dy * g
    return r * (dxh - xh * jnp.mean(dxh * xh, axis=-1, keepdims=True)), _colsum(dy * xh)


def _mla_prep_fwd(z, gq, gkv, cos, sin, QL, KL, name):
    M, NZ = z.shape
    tm = _tile(M, (256, 128))

    def body(z_ref, gq_ref, gkv_ref, cos_ref, sin_ref, cq_ref, ckv_ref, kpe_ref):
        zv = z_ref[...]
        cq_ref[...] = _rms(zv[:, :QL], gq_ref[...]).astype(BF16)
        ckv_ref[...] = _rms(zv[:, QL:QL + KL], gkv_ref[...]).astype(BF16)
        kp = zv[:, QL + KL:]
        r = kp * cos_ref[...] + _swap32(kp) * sin_ref[...]
        lane = lax.broadcasted_iota(jnp.int32, r.shape, 1)
        kpe_ref[0] = jnp.where(lane < ROPE, r, 0.0).astype(BF16)
        kpe_ref[1] = jnp.where(lane >= ROPE, r, 0.0).astype(BF16)

    tab = pl.BlockSpec((tm, 128), lambda i: (i, 0))
    return pl.pallas_call(
        body, out_shape=[SDS((M, QL), BF16), SDS((M, KL), BF16), SDS((2, M, 128), BF16)], grid=(M // tm,),
        in_specs=[pl.BlockSpec((tm, NZ), lambda i: (i, 0)), pl.BlockSpec((1, QL), lambda i: (0, 0)),
                  pl.BlockSpec((1, KL), lambda i: (0, 0)), tab, tab],
        out_specs=[pl.BlockSpec((tm, QL), lambda i: (i, 0)), pl.BlockSpec((tm, KL), lambda i: (i, 0)),
                   pl.BlockSpec((2, tm, 128), lambda i: (0, i, 0))],
        compiler_params=_params(("parallel",)), name=name)(z, gq, gkv, cos, sin)


def _mla_prep_bwd(z, dcq, dckv, dkpe, gq, gkv, cos, sin, QL, KL, seg_t, name):
    M, NZ = z.shape
    H = dkpe.shape[0]
    tm = _row_tile(seg_t, M)
    nt = seg_t // tm

    def body(z_ref, dcq_ref, dckv_ref, dkpe_ref, gq_ref, gkv_ref, cos_ref, sin_ref, dz_ref, dgq_ref, dgkv_ref):
        i = pl.program_id(0)

        @pl.when(i == 0)
        def _():
            dgq_ref[...] = jnp.zeros_like(dgq_ref)
            dgkv_ref[...] = jnp.zeros_like(dgkv_ref)

        zv = z_ref[...]
        dyq = jnp.where(i < nt, dcq_ref[...], 0.0)
        dxq, dgq = _rms_bwd(zv[:, :QL], gq_ref[...], dyq)
        dxkv, dgkv = _rms_bwd(zv[:, QL:QL + KL], gkv_ref[...], dckv_ref[...])
        dgq_ref[...] += dgq
        dgkv_ref[...] += dgkv
        even = dkpe_ref[0]
        odd = dkpe_ref[1]
        for h in range(2, H, 2):
            even = even + dkpe_ref[h]
            odd = odd + dkpe_ref[h + 1]
        lane = lax.broadcasted_iota(jnp.int32, even.shape, 1)
        dr = jnp.where(lane < ROPE, even, odd)
        dkp = dr * cos_ref[...] - _swap32(dr) * sin_ref[...]
        dz_ref[...] = jnp.concatenate([dxq, dxkv, dkp], axis=1).astype(BF16)

    tab = pl.BlockSpec((tm, 128), lambda i: (i, 0))
    return pl.pallas_call(
        body, out_shape=[SDS((M, NZ), BF16), SDS((1, QL), F32), SDS((1, KL), F32)], grid=(M // tm,),
        in_specs=[pl.BlockSpec((tm, NZ), lambda i: (i, 0)),
                  pl.BlockSpec((tm, QL), lambda i: (jnp.minimum(i, nt - 1), 0)),
                  pl.BlockSpec((tm, KL), lambda i: (i, 0)), pl.BlockSpec((H, tm, 128), lambda i: (0, i, 0)),
                  pl.BlockSpec((1, QL), lambda i: (0, 0)), pl.BlockSpec((1, KL), lambda i: (0, 0)), tab, tab],
        out_specs=[pl.BlockSpec((tm, NZ), lambda i: (i, 0)), pl.BlockSpec((1, QL), lambda i: (0, 0)),
                   pl.BlockSpec((1, KL), lambda i: (0, 0))],
        compiler_params=_params(("arbitrary",)), name=name)(z, dcq, dckv, dkpe, gq, gkv, cos, sin)


def _qrope_fwd(q, cos, sin, HN, name):
    T, NQ = q.shape
    tm = _tile(T, (256, 128))

    def body(q_ref, cos_ref, sin_ref, o_ref):
        o_ref[:, :HN] = (q_ref[:, :HN] * _QSCALE).astype(BF16)
        for cb in range((NQ - HN) // 128):
            sl = slice(HN + cb * 128, HN + (cb + 1) * 128)
            xv = q_ref[:, sl]
            o_ref[:, sl] = ((xv * cos_ref[...] + _swap32(xv) * sin_ref[...]) * _QSCALE).astype(BF16)

    tab = pl.BlockSpec((tm, 128), lambda i: (i, 0))
    return pl.pallas_call(
        body, out_shape=SDS((T, NQ), BF16), grid=(T // tm,),
        in_specs=[pl.BlockSpec((tm, NQ), lambda i: (i, 0)), tab, tab],
        out_specs=pl.BlockSpec((tm, NQ), lambda i: (i, 0)),
        compiler_params=_params(("parallel",)), name=name)(q, cos, sin)


def _qrope_bwd(dqpe, dqa, cos, sin, HN, name):
    T, HW = dqpe.shape
    HR = HW // 2
    tm = _tile(T, (256, 128))

    def body(d_ref, dqa_hbm, cos_ref, sin_ref, o_ref):
        del dqa_hbm
        for pr in range(HR // 128):
            dr = d_ref[:, 2 * pr * 128:(2 * pr + 1) * 128] + d_ref[:, (2 * pr + 1) * 128:(2 * pr + 2) * 128]
            o_ref[:, pr * 128:(pr + 1) * 128] = (dr * cos_ref[...] - _swap32(dr) * sin_ref[...]).astype(BF16)

    tab = pl.BlockSpec((tm, 128), lambda i: (i, 0))
    return pl.pallas_call(
        body, out_shape=SDS(dqa.shape, BF16), grid=(T // tm,),
        in_specs=[pl.BlockSpec((tm, HW), lambda i: (i, 0)), pl.BlockSpec(memory_space=pl.ANY), tab, tab],
        out_specs=pl.BlockSpec((tm, HR), lambda i: (i, HN // HR)),
        input_output_aliases={1: 0}, compiler_params=_params(("parallel",)), name=name)(dqpe, dqa, cos, sin)


_ATT_SUB = 4
_ATT_SUB_B = 2
_NT = (((1,), (1,)), ((), ()))
_TN = (((0,), (0,)), ((), ()))


def _attn_fwd(qa, kv, kpe, T, H, name, carry=None):
    M = kv.shape[0]
    tq = _tile(T, (1024, 512, 256, 128))
    scale = (NOPE + ROPE) ** -0.5

    def body(qn_ref, qp_ref, kv_ref, kpe_ref, o_ref, lse_ref, kc_ref):
        @pl.when(pl.program_id(1) == 0)
        def _():
            kc_ref[:, :NOPE] = kv_ref[:, :NOPE]
            kc_ref[:, NOPE:] = kpe_ref[0]

        rs = tq // _ATT_SUB
        outs, lses = [], []
        for u in range(_ATT_SUB):
            rows = pl.ds(u * rs, rs)
            qc = jnp.concatenate([qn_ref[rows, :], qp_ref[rows, :]], axis=1)
            s = lax.dot_general(qc, kc_ref[...], _NT, preferred_element_type=F32)
            m = jnp.max(s, axis=-1, keepdims=True)
            p = jnp.exp2(s - m)
            l = jnp.sum(p, axis=-1, keepdims=True)
            o = jnp.dot(p.astype(BF16), kv_ref[:, NOPE:], preferred_element_type=F32)
            outs.append((o / l).astype(BF16))
            lses.append(jnp.broadcast_to(m + jnp.log2(l), (rs, 128)))
        o_ref[...] = jnp.concatenate(outs, axis=0)
        lse_ref[...] = jnp.concatenate(lses, axis=0)

    return _carry_call(
        body, carry, out_shape=[SDS((T, H * VDIM), BF16), SDS((T, H * 128), F32)], grid=(H, T // tq),
        in_specs=[pl.BlockSpec((tq, NOPE), lambda h, i: (i, h)), pl.BlockSpec((tq, 128), lambda h, i: (i, H + h // 2)),
                  pl.BlockSpec((M, NOPE + VDIM), lambda h, i: (0, h)), pl.BlockSpec((1, M, 128), lambda h, i: (h % 2, 0, 0))],
        out_specs=[pl.BlockSpec((tq, VDIM), lambda h, i: (i, h)), pl.BlockSpec((tq, 128), lambda h, i: (i, h))],
        scratch_shapes=[pltpu.VMEM((M, NOPE + 128), BF16)],
        sem=("parallel", "arbitrary"), name=name, ins=(qa, qa, kv, kpe))


def _attn_bwd(qa, kv, kpe, do, o, lse, T, H, name, carry=None):
    M = kv.shape[0]
    tq = _tile(T, (512, 256, 128))
    nq = T // tq
    scale = (NOPE + ROPE) ** -0.5

    def body(qn_ref, qp_ref, kv_ref, kpe_ref, do_ref, o_ref, lse_ref, dqa_ref, dqpe_ref, dkv_ref, dkpe_ref, kc_ref,
             dk_acc, dv_acc):
        i = pl.program_id(1)

        @pl.when(i == 0)
        def _():
            kc_ref[:, :NOPE] = kv_ref[:, :NOPE]
            kc_ref[:, NOPE:] = kpe_ref[0]
            dk_acc[...] = jnp.zeros_like(dk_acc)
            dv_acc[...] = jnp.zeros_like(dv_acc)

        rs = tq // _ATT_SUB_B
        p16s, ds16s = [], []
        for u in range(_ATT_SUB_B):
            rows = pl.ds(u * rs, rs)
            qc = jnp.concatenate([qn_ref[rows, :], qp_ref[rows, :]], axis=1)
            dov = do_ref[rows, :]
            s = lax.dot_general(qc, kc_ref[...], _NT, preferred_element_type=F32)
            p = jnp.exp2(s - lse_ref[rows, 0:1])
            dp = lax.dot_general(dov, kv_ref[:, NOPE:], _NT, preferred_element_type=F32)
            delta = jnp.sum(dov.astype(F32) * o_ref[rows, :].astype(F32), axis=-1, keepdims=True)
            ds16s.append((p * (dp - delta)).astype(BF16))
            p16s.append(p.astype(BF16))
        p16 = jnp.concatenate(p16s, axis=0)
        ds16 = jnp.concatenate(ds16s, axis=0)
        qc = jnp.concatenate([qn_ref[...], qp_ref[...]], axis=1)
        dq = jnp.dot(ds16, kc_ref[...], preferred_element_type=F32) * scale
        dqa_ref[...] = dq[:, :NOPE].astype(BF16)
        dqpe_ref[...] = dq[:, NOPE:]
        dv_acc[...] += lax.dot_general(p16, do_ref[...], _TN, preferred_element_type=F32)
        dk_acc[...] += lax.dot_general(ds16, qc, _TN, preferred_element_type=F32)

        @pl.when(i == nq - 1)
        def _():
            dkv_ref[:, :NOPE] = (dk_acc[:, :NOPE] * _LN2).astype(BF16)
            dkv_ref[:, NOPE:] = dv_acc[...].astype(BF16)
            dkpe_ref[0] = dk_acc[:, NOPE:] * _LN2

    return _carry_call(
        body, carry,
        out_shape=[SDS((T, H * (NOPE + ROPE)), BF16), SDS((T, H * 128), F32), SDS((M, H * (NOPE + VDIM)), BF16),
                   SDS((H, M, 128), F32)],
        grid=(H, nq),
        in_specs=[pl.BlockSpec((tq, NOPE), lambda h, i: (i, h)), pl.BlockSpec((tq, 128), lambda h, i: (i, H + h // 2)),
                  pl.BlockSpec((M, NOPE + VDIM), lambda h, i: (0, h)), pl.BlockSpec((1, M, 128), lambda h, i: (h % 2, 0, 0)),
                  pl.BlockSpec((tq, VDIM), lambda h, i: (i, h)), pl.BlockSpec((tq, VDIM), lambda h, i: (i, h)),
                  pl.BlockSpec((tq, 128), lambda h, i: (i, h))],
        out_specs=[pl.BlockSpec((tq, NOPE), lambda h, i: (i, h)), pl.BlockSpec((tq, 128), lambda h, i: (i, h)),
                   pl.BlockSpec((M, NOPE + VDIM), lambda h, i: (0, h)), pl.BlockSpec((1, M, 128), lambda h, i: (h, 0, 0))],
        scratch_shapes=[pltpu.VMEM((M, NOPE + 128), BF16), pltpu.VMEM((M, NOPE + 128), F32), pltpu.VMEM((M, VDIM), F32)],
        sem=("parallel", "arbitrary"), name=name, ins=(qa, qa, kv, kpe, do, o, lse))


def _adamw(w, m, v, name, g=None, recv=None, carry=None):
    R, C = w.shape
    summed = recv is not None
    n_recv = len(recv) if summed else 1
    runs = [r.shape[1] for r in recv] if summed else [R]
    tr = math.gcd(*runs)
    for cand in (1024, 512, 256, 128, 64, 32, 16, 8):
        if tr % cand == 0 and cand * C <= 131072:
            tr = cand
            break
    first = [sum(runs[:r]) // tr for r in range(n_recv + 1)]
    c1 = 1.0 - ADAM_B1 ** ADAM_STEP
    c2 = 1.0 - ADAM_B2 ** ADAM_STEP

    def update(gv, w_ref, m_ref, v_ref, d_ref, nm_ref, nv_ref):
        mn = ADAM_B1 * m_ref[...] + (1.0 - ADAM_B1) * gv
        vn = ADAM_B2 * v_ref[...] + (1.0 - ADAM_B2) * (gv * gv)
        nm_ref[...] = mn
        nv_ref[...] = vn
        d_ref[...] = -ADAM_LR * ((mn / c1) / (jnp.sqrt(vn / c2) + ADAM_EPS) + ADAM_WD * w_ref[...])

    def body(*refs):
        w_ref, m_ref, v_ref = refs[:3]
        g_refs = refs[3:3 + n_recv]
        outs = refs[3 + n_recv:]
        if not summed:
            update(g_refs[0][...], w_ref, m_ref, v_ref, *outs)
            return
        i = pl.program_id(0)
        for r in range(n_recv):
            @pl.when((i >= first[r]) & (i < first[r + 1]))
            def _():
                gv = g_refs[r][0].astype(F32)
                for d in range(1, N_DEV):
                    gv = gv + g_refs[r][d].astype(F32)
                outs[0][...] = gv
                update(gv, w_ref, m_ref, v_ref, *outs[1:])

    blk = pl.BlockSpec((tr, C), lambda i: (i, 0))
    if summed:
        g_specs = [pl.BlockSpec((N_DEV, tr, C), functools.partial(
            lambda i, lo, n: (0, jnp.clip(i - lo, 0, n - 1), 0), lo=first[r], n=first[r + 1] - first[r]))
                   for r in range(n_recv)]
    else:
        g_specs = [blk]
    n_out = 4 if summed else 3
    out, carried = _carry_call(
        body, carry, out_shape=[SDS((R, C), F32)] * n_out, grid=(R // tr,), in_specs=[blk, blk, blk] + g_specs,
        out_specs=[blk] * n_out, scratch_shapes=[], sem=("parallel",), name=name,
        ins=(w, m, v, *(recv if summed else [g])))
    return (out, carried) if carry else out


WEIGHTS = ['c_ctx', 'norm1_g', 'norm2_g', 'w_ada', 'b_ada', 'ab_w_in', 'ab_b_in', 'a_ln_g', 'a_ln_b', 'a_w_s', 'a_b_s',
           'b_conv_w', 'b_conv_b', 'b_ln_g', 'b_ln_b', 'ab_w_out', 'mla_w_in', 'mla_q_norm_g', 'mla_w_uq',
           'mla_kv_norm_g', 'mla_w_ukv', 'mla_w_o', 'ffn_w_up', 'ffn_conv_w', 'ffn_conv_b', 'ffn_w_down', 'final_norm_g']


def _pack(parts):
    flat = jnp.concatenate([p.reshape(-1).astype(F32) for p in parts])
    n = flat.shape[0]
    unit = 65536 if n > 65536 else 1024
    n_pad = -(-n // unit) * unit
    return jnp.pad(flat, (0, n_pad - n)).reshape(n_pad // 128, 128)


def _unpack(flat, like):
    out, off = [], 0
    for shp in like:
        n = math.prod(shp)
        out.append(flat[..., off:off + n].reshape(flat.shape[:-1] + tuple(shp)))
        off += n
    return out


def _rope_tables(T, Tc):
    rows = T // GRID_W
    row = jnp.repeat(jnp.arange(rows, dtype=F32), GRID_W)
    col = jnp.tile(jnp.arange(GRID_W, dtype=F32), rows)
    n_freq = ROPE // 4
    inv = ROPE_THETA ** (-jnp.arange(n_freq, dtype=F32) / n_freq)
    ang = jnp.concatenate([row[:, None] * inv, col[:, None] * inv], axis=-1)
    cos, sin = jnp.cos(ang), jnp.sin(ang)
    cos = jnp.tile(cos, (1, 128 // (ROPE // 2)))
    sin = jnp.tile(jnp.concatenate([-sin, sin], axis=1), (1, 128 // ROPE))
    return (jnp.concatenate([cos, jnp.ones((Tc, 128), F32)], axis=0),
            jnp.concatenate([sin, jnp.zeros((Tc, 128), F32)], axis=0))


def _step(a):
    ax, ay, ac = lax.axis_index("x"), lax.axis_index("y"), lax.axis_index("c")
    me = 4 * ax + 2 * ay + ac
    T, D = a['x'].shape[1:]
    Tc = a['ctx'].shape[1]
    M = T + Tc
    W, Wb = a['a_ln_g'].shape[1], a['b_ln_g'].shape[1]
    assert W == Wb and T % Tc == 0
    Fd = a['ffn_conv_b'].shape[1]
    QL, KL = a['mla_q_norm_g'].shape[1] * N_DEV, a['mla_kv_norm_g'].shape[1] * N_DEV
    H = a['mla_w_ukv'].shape[2] * N_DEV // (NOPE + VDIM)
    HN, HR = H * NOPE, H * ROPE
    kw = a['b_conv_w'].shape[1]
    NA = a['w_ada'].shape[2]
    bf = lambda t: t.astype(BF16)

    small_shapes = [(D,), (kw, Wb // N_DEV), (2, 3, Fd // N_DEV), (QL // N_DEV,), (KL // N_DEV,)]
    g_small = _all_gather(_pack([a['c'][0], a['b_conv_w'][0], a['ffn_conv_w'], a['mla_q_norm_g'][0], a['mla_kv_norm_g'][0]]),
                          "ag_small")
    c_all, bcw, fcw, gq, gkv = _unpack(g_small.reshape(N_DEV, -1), small_shapes)
    bcw = jnp.transpose(bcw, (1, 0, 2)).reshape(kw, Wb)
    fcw = jnp.transpose(fcw, (1, 2, 0, 3)).reshape(2, 3, Fd)
    gq, gkv = gq.reshape(1, QL), gkv.reshape(1, KL)

    a16 = jnp.concatenate([c_all, a['c_ctx'][None], jnp.zeros((N_DEV - 1, D), F32)], axis=0)
    b_loc = lax.dynamic_slice(a['b_ada'], (0, me * NA), (2, NA))
    mods = [_mm(a16, a['w_ada'][l], mode="nn", out_dtype=F32, name=f"ada_fwd{l}", bias=b_loc[l:l + 1], a_silu=True)
            for l in range(2)]
    gm = _all_gather(jnp.concatenate(mods, axis=0), "ag_mod").reshape(N_DEV, 2, 2 * N_DEV, NA)
    gm = jnp.transpose(gm, (1, 2, 0, 3)).reshape(2, 2 * N_DEV, 6 * D)
    mod_lat = [lax.dynamic_slice(gm[l], (me, 0), (1, 6 * D)).reshape(6, 1, 1, D) for l in range(2)]
    mod_ctx = [gm[l][N_DEV].reshape(6, 1, 1, D) for l in range(2)]

    def mod(l, k, both):
        return jnp.concatenate([mod_lat[l][k], mod_ctx[l][k]], axis=0) if both else mod_lat[l][k]

    def from_cols(g):
        return jnp.transpose(g, (1, 0, 2)).reshape(g.shape[1], -1)

    def from_rows(g):
        return g.reshape(-1, g.shape[2])

    def ag(x):
        return (x, False)

    def a2a(x):
        return (x, True)

    cos, sin = _rope_tables(T, Tc)
    n1g, n2g = a['norm1_g'], a['norm2_g']
    a_bst = a['a_b_s'][0].T
    mm = functools.partial(_mm)
    up_sh, dn_sh = bf(a['ffn_w_up']), bf(a['ffn_w_down'])

    w_abin = _all_gather(bf(a['ab_w_in'][0]), "ag_ab_w_in")
    x0 = jnp.concatenate([a['x'][0], a['ctx'][0]], axis=0)
    h1 = _normmod_fwd(x0, n1g[0:1], mod(0, 0, True), mod(0, 1, True), T, "l0_norm1")
    z, (g_about,) = mm(h1, w_abin, mode="nn", out_dtype=F32, name="l0_ab_in", bias=a['ab_b_in'], b_dev=True,
                       carry=[ag(bf(a['ab_w_out'][0]))])
    w_about = from_rows(g_about)
    y = _gmlp_fwd(z, a['a_ln_g'], a['a_ln_b'], a['a_w_s'][0], a_bst, W, "l0_gmlp")
    (y, hc_b), (g_up0a,) = _conf_fwd(z, y, bcw, a['b_conv_b'], a['b_ln_g'], a['b_ln_b'], W, Wb, T, "l0_conf",
                                     carry=[ag(up_sh[0][:D // 2])])
    (x1, o1), (g_up0b,) = mm(y, w_about, mode="nn", out_dtype=F32, name="l0_ab_out", res=x0, gate=mod(0, 2, True),
                             seg_t=T, carry=[ag(up_sh[0][D // 2:])])
    w_up = [jnp.concatenate([g_up0a, g_up0b], axis=1), None]
    h2 = _normmod_fwd(x1, n2g[0:1], mod(0, 3, True), mod(0, 4, True), T, "l0_norm2")
    z2, (g_dn0,) = mm(h2, w_up[0], mode="nn", out_dtype=BF16, name="l0_up", b_dev=True, carry=[ag(dn_sh[0])])
    w_dn = [from_rows(g_dn0), None]
    a2 = _ffn_act_fwd(z2, fcw[0], a['ffn_conv_b'][0:1], T, "l0_act")
    (x2, o2), (g_in, g_uq) = mm(a2, w_dn[0], mode="nn", out_dtype=F32, name="l0_down", res=x1, gate=mod(0, 5, True),
                                seg_t=T, carry=[ag(bf(a['mla_w_in'][0])), ag(bf(a['mla_w_uq'][0]))])
    w_in = from_rows(g_in)
    w_in = jnp.concatenate([w_in, w_in[:, QL + KL:]], axis=1)
    w_uq = from_cols(g_uq).reshape(QL, H, NOPE + ROPE)
    w_uq = jnp.concatenate([w_uq[:, :, :NOPE].reshape(QL, HN), w_uq[:, :, NOPE:].reshape(QL, HR)], axis=1)

    h3 = _normmod_fwd(x2, n1g[1:2], mod(1, 0, True), mod(1, 1, True), T, "l1_norm1")
    z3, (g_ukv,) = mm(h3, w_in, mode="nn", out_dtype=F32, name="l1_mla_in", carry=[ag(bf(a['mla_w_ukv'][0]))])
    w_ukv = g_ukv
    cqn, ckvn, kpe = _mla_prep_fwd(z3, gq, gkv, cos, sin, QL, KL, "l1_prep")
    q, (g_wo,) = mm(cqn, w_uq, mode="nn", out_dtype=F32, name="l1_uq", rows=T, carry=[ag(bf(a['mla_w_o'][0]))])
    w_o = from_rows(g_wo)
    kv = mm(ckvn, w_ukv, mode="nn", out_dtype=BF16, name="l1_ukv", b_dev=True)
    qa = _qrope_fwd(q, cos, sin, HN, "l1_qrope")
    (o_att, lse), (g_up1,) = _attn_fwd(qa, kv, kpe, T, H, "l1_attn", carry=[ag(up_sh[1])])
    w_up[1] = g_up1
    x3, o3 = mm(o_att, w_o, mode="nn", out_dtype=F32, name="l1_wo", res=x2, gate=mod(1, 2, False), seg_t=T)
    h4 = _normmod_fwd(x3, n2g[1:2], mod(1, 3, False), mod(1, 4, False), T, "l1_norm2")
    z4, (g_dn1,) = mm(h4, w_up[1], mode="nn", out_dtype=BF16, name="l1_up", b_dev=True, carry=[ag(dn_sh[1])])
    w_dn[1] = from_rows(g_dn1)
    a4 = _ffn_act_fwd(z4, fcw[1], a['ffn_conv_b'][1:2], T, "l1_act")
    x4, o4 = mm(a4, w_dn[1], mode="nn", out_dtype=F32, name="l1_down", res=x3, gate=mod(1, 5, False), seg_t=T)

    dx4, loss_cols, d_fng, do4, dg2_1 = _final(x4, a['final_norm_g'][None], a['loss_target'][0], o4, mod(1, 5, False),
                                               "final")
    loss = lax.psum(jnp.sum(loss_cols), ("x", "y", "c"))

    def cols(dw):
        k, n = dw.shape
        return jnp.transpose(dw.reshape(k, N_DEV, n // N_DEV), (1, 0, 2))

    def rows(dw):
        return dw.reshape(N_DEV, dw.shape[0] // N_DEV, dw.shape[1])

    da4 = mm(do4, w_dn[1], mode="nt", out_dtype=BF16, name="l1_down_dx")
    dw_dn1 = mm(a4, do4, mode="tn", out_dtype=BF16, name="l1_down_dw")
    dz4, dfcw1, dfcb1 = _ffn_act_bwd(z4, da4, fcw[1], a['ffn_conv_b'][1:2], T, "l1_act_bwd")
    dw_up1, (r_dn1,) = mm(h4, dz4, mode="tn", out_dtype=BF16, name="l1_up_dw", out_dev=True,
                          carry=[a2a(rows(dw_dn1))])
    dh4 = mm(dz4, w_up[1], mode="nt", out_dtype=F32, name="l1_up_dx", b_dev=True)
    dx3, dn2g1, dsh2_1, dsc2_1, do3, dg1_1 = _normmod_bwd(x3, n2g[1:2], mod(1, 4, False), dh4, dx4, T, "l1_norm2_bwd",
                                                         o_prev=o3, gate_prev=mod(1, 2, False))
    d_oatt = mm(do3, w_o, mode="nt", out_dtype=BF16, name="l1_wo_dx")
    dw_o = mm(o_att, do3, mode="tn", out_dtype=BF16, name="l1_wo_dw")
    (dqa, dqpe, dkv, dkpe), (r_up1, r_wo) = _attn_bwd(qa, kv, kpe, d_oatt, o_att, lse, T, H, "l1_attn_bwd",
                                                      carry=[a2a(dw_up1), a2a(rows(dw_o))])
    dqa = _qrope_bwd(dqpe, dqa, cos, sin, HN, "l1_qrope_bwd")
    dcq = mm(dqa, w_uq, mode="nt", out_dtype=F32, name="l1_uq_dx")
    dw_uq = mm(cqn, dqa, mode="tn", out_dtype=BF16, name="l1_uq_dw", rows=T)
    dw_uq = jnp.concatenate([dw_uq[:, :HN].reshape(QL, H, NOPE), dw_uq[:, HN:].reshape(QL, H, ROPE)], axis=2)
    dw_uq = dw_uq.reshape(QL, H * (NOPE + ROPE))
    dckv = mm(dkv, w_ukv, mode="nt", out_dtype=F32, name="l1_ukv_dx", b_dev=True)
    dw_ukv = mm(ckvn, dkv, mode="tn", out_dtype=BF16, name="l1_ukv_dw", out_dev=True)
    dz3, dgq, dgkv = _mla_prep_bwd(z3, dcq, dckv, dkpe, gq, gkv, cos, sin, QL, KL, T, "l1_prep_bwd")
    dh3 = mm(dz3, w_in, mode="nt", out_dtype=F32, name="l1_mla_in_dx")
    dw_in = mm(h3, dz3, mode="tn", out_dtype=BF16, name="l1_mla_in_dw").astype(F32)
    dw_in = jnp.concatenate([dw_in[:, :QL + KL], dw_in[:, QL + KL:QL + KL + ROPE] + dw_in[:, QL + KL + ROPE:QL + KL + 2 * ROPE]],
                            axis=1).astype(BF16)
    dx2, dn1g1, dsh1_1, dsc1_1, do2, dg2_0 = _normmod_bwd(x2, n1g[1:2], mod(1, 1, True), dh3, dx3, T, "l1_norm1_bwd",
                                                         o_prev=o2, gate_prev=mod(0, 5, True))
    da2, (r_uq, r_ukv) = mm(do2, w_dn[0], mode="nt", out_dtype=BF16, name="l0_down_dx",
                            carry=[a2a(cols(dw_uq)), a2a(dw_ukv)])
    dw_dn0, (r_in,) = mm(a2, do2, mode="tn", out_dtype=BF16, name="l0_down_dw", carry=[a2a(rows(dw_in))])
    dz2, dfcw0, dfcb0 = _ffn_act_bwd(z2, da2, fcw[0], a['ffn_conv_b'][0:1], T, "l0_act_bwd")
    dw_up0, (r_dn0,) = mm(h2, dz2, mode="tn", out_dtype=BF16, name="l0_up_dw", out_dev=True,
                          carry=[a2a(rows(dw_dn0))])
    dh2, (r_up0a,) = mm(dz2, w_up[0], mode="nt", out_dtype=F32, name="l0_up_dx", b_dev=True,
                        carry=[a2a(dw_up0[:, :D // 2])])
    dx1, dn2g0, dsh2_0, dsc2_0, do1, dg1_0 = _normmod_bwd(x1, n2g[0:1], mod(0, 4, True), dh2, dx2, T, "l0_norm2_bwd",
                                                         o_prev=o1, gate_prev=mod(0, 2, True))
    dy = mm(do1, w_about, mode="nt", out_dtype=F32, name="l0_ab_out_dx")
    dw_about = mm(y, do1, mode="tn", out_dtype=BF16, name="l0_ab_out_dw")
    dz, dlag, dlab, dws, dbs, dbin_a = _gmlp_bwd(z, dy, a['a_ln_g'], a['a_ln_b'], a['a_w_s'][0], a_bst, W, "l0_gmlp_bwd")
    dhc, dlbg, dlbb, dbcb = _conf_bwd1(hc_b, dy, a['b_ln_g'], a['b_ln_b'], W, Wb, T, "l0_conf_bwd1")
    (dz, dbcw, dbin_b), (r_up0b,) = _conf_bwd2(z, dhc, dz, bcw, W, Wb, T, "l0_conf_bwd2",
                                               carry=[a2a(dw_up0[:, D // 2:])])
    dh1, (r_about,) = mm(dz, w_abin, mode="nt", out_dtype=F32, name="l0_ab_in_dx", b_dev=True,
                         carry=[a2a(rows(dw_about))])
    dw_abin_a = mm(h1, dz, mode="tn", out_dtype=BF16, name="l0_ab_in_dw_a", out_dev=True, p_range=(0, D // 2))
    dw_abin_b, (r_abin_a,) = mm(h1, dz, mode="tn", out_dtype=BF16, name="l0_ab_in_dw_b", out_dev=True,
                                p_range=(D // 2, D // 2), carry=[a2a(dw_abin_a)])
    dx0, dn1g0, dsh1_0, dsc1_0 = _normmod_bwd(x0, n1g[0:1], mod(0, 1, True), dh1, dx1, T, "l0_norm1_bwd")

    zero = jnp.zeros((D,), F32)
    dmod = jnp.stack([
        jnp.stack([jnp.stack([dsh1_0[0, 0], dsc1_0[0, 0], dg1_0[0, 0], dsh2_0[0, 0], dsc2_0[0, 0], dg2_0[0, 0]]),
                   jnp.stack([dsh1_0[1, 0], dsc1_0[1, 0], dg1_0[1, 0], dsh2_0[1, 0], dsc2_0[1, 0], dg2_0[1, 0]])]),
        jnp.stack([jnp.stack([dsh1_1[0, 0], dsc1_1[0, 0], dg1_1[0, 0], dsh2_1[0, 0], dsc2_1[0, 0], dg2_1[0, 0]]),
                   jnp.stack([dsh1_1[1, 0], dsc1_1[1, 0], zero, zero, zero, zero])])])
    small = {
        'norm1_g': jnp.concatenate([dn1g0, dn1g1], axis=0), 'norm2_g': jnp.concatenate([dn2g0, dn2g1], axis=0),
        'ab_b_in': jnp.concatenate([dbin_a, dbin_b], axis=1), 'a_ln_g': dlag, 'a_ln_b': dlab, 'a_w_s': dws[None],
        'a_b_s': jnp.sum(dbs, axis=-1)[None], 'b_conv_w': dbcw, 'b_conv_b': dbcb, 'b_ln_g': dlbg, 'b_ln_b': dlbb,
        'mla_q_norm_g': dgq, 'mla_kv_norm_g': dgkv, 'ffn_conv_w': jnp.stack([dfcw0, dfcw1]),
        'ffn_conv_b': jnp.concatenate([dfcb0, dfcb1], axis=0), 'final_norm_g': d_fng[0],
    }
    names = list(small)
    g2 = _all_gather(_pack([dmod] + [small[n] for n in names]), "ag_small_grads")
    red = _sum_lead(g2, "sum_small_grads").reshape(-1)
    red = dict(zip(names, _unpack(red, [(2, 2, 6, D)] + [small[n].shape for n in names])[1:]))
    dmod_all = g2.reshape(N_DEV, -1)[:, :2 * 2 * 6 * D].reshape(N_DEV, 2, 2, 6 * D)

    a16g = jnp.concatenate([c_all, jnp.tile(a['c_ctx'][None], (N_DEV, 1))], axis=0)
    dm_loc = lax.dynamic_slice(dmod_all, (0, 0, 0, me * NA), (N_DEV, 2, 2, NA))
    g_wada, cpart = [], []
    for l in range(2):
        dm16 = jnp.concatenate([dm_loc[:, l, 0], dm_loc[:, l, 1]], axis=0)
        g_wada.append(mm(a16g, dm16, mode="tn", out_dtype=F32, name=f"ada_dw{l}", a_silu=True))
        cpart.append(mm(dm_loc[:, l, 1], a['w_ada'][l], mode="nt", out_dtype=F32, name=f"ada_dc{l}"))
    g_bada = _sum_lead(jnp.transpose(dmod_all, (0, 2, 1, 3)).reshape(2 * N_DEV, 2 * 6 * D // 128, 128), "sum_b_ada")
    g_cc = _all_gather(jnp.concatenate(cpart, axis=0), "ag_c_ctx")
    g_cc = _sum_lead(g_cc.reshape(2 * N_DEV * N_DEV, D // 128, 128), "sum_c_ctx").reshape(D)
    grads = {
        'c_ctx': g_cc * _dsilu(a['c_ctx']), 'w_ada': jnp.stack(g_wada), 'b_ada': g_bada.reshape(2, 6 * D),
        'b_conv_w': lax.dynamic_slice(red['b_conv_w'], (0, me * (Wb // N_DEV)), (kw, Wb // N_DEV))[None],
        'ffn_conv_w': lax.dynamic_slice(red['ffn_conv_w'], (0, 0, me * (Fd // N_DEV)), (2, 3, Fd // N_DEV)),
        'mla_q_norm_g': lax.dynamic_slice(red['mla_q_norm_g'], (0, me * (QL // N_DEV)), (1, QL // N_DEV)),
        'mla_kv_norm_g': lax.dynamic_slice(red['mla_kv_norm_g'], (0, me * (KL // N_DEV)), (1, KL // N_DEV)),
    }
    for n in names:
        if n not in grads:
            grads[n] = red[n].reshape(a[n].shape)

    recvs = {'ab_w_out': [r_about], 'mla_w_in': [r_in], 'mla_w_uq': [r_uq], 'mla_w_ukv': [r_ukv],
             'mla_w_o': [r_wo], 'ffn_w_up': [r_up0a, r_up0b, r_up1], 'ffn_w_down': [r_dn0, r_dn1]}
    out = {}
    for n in WEIGHTS:
        shp = a[n].shape
        w2 = a[n].reshape(-1, shp[-1])
        m2, v2 = a['m_' + n].reshape(w2.shape), a['v_' + n].reshape(w2.shape)
        if n in recvs:
            res = _adamw(w2, m2, v2, "adamw_" + n, recv=recvs[n])
        elif n == 'w_ada':
            g2d = grads[n].reshape(w2.shape)
            res, (r_abin_b,) = _adamw(w2, m2, v2, "adamw_" + n, g=g2d, carry=[a2a(dw_abin_b)])
            res = (g2d,) + tuple(res)
            recvs['ab_w_in'] = [r_abin_a, r_abin_b]
        else:
            g2d = grads[n].reshape(w2.shape)
            res = (g2d,) + tuple(_adamw(w2, m2, v2, "adamw_" + n, g=g2d))
        out[n] = [r.reshape(shp) for r in res]
    return (loss, dx0[:T][None], *[out[n][0] for n in WEIGHTS], *[out[n][1] for n in WEIGHTS],
            *[out[n][2] for n in WEIGHTS], *[out[n][3] for n in WEIGHTS])


def kernel(x, c, ctx, c_ctx, norm1_g, norm2_g, w_ada, b_ada, ab_w_in, ab_b_in, a_ln_g, a_ln_b, a_w_s, a_b_s, b_conv_w, b_conv_b, b_ln_g, b_ln_b, ab_w_out, mla_w_in, mla_q_norm_g, mla_w_uq, mla_kv_norm_g, mla_w_ukv, mla_w_o, ffn_w_up, ffn_conv_w, ffn_conv_b, ffn_w_down, final_norm_g, loss_target, m_c_ctx, m_norm1_g, m_norm2_g, m_w_ada, m_b_ada, m_ab_w_in, m_ab_b_in, m_a_ln_g, m_a_ln_b, m_a_w_s, m_a_b_s, m_b_conv_w, m_b_conv_b, m_b_ln_g, m_b_ln_b, m_ab_w_out, m_mla_w_in, m_mla_q_norm_g, m_mla_w_uq, m_mla_kv_norm_g, m_mla_w_ukv, m_mla_w_o, m_ffn_w_up, m_ffn_conv_w, m_ffn_conv_b, m_ffn_w_down, m_final_norm_g, v_c_ctx, v_norm1_g, v_norm2_g, v_w_ada, v_b_ada, v_ab_w_in, v_ab_b_in, v_a_ln_g, v_a_ln_b, v_a_w_s, v_a_b_s, v_b_conv_w, v_b_conv_b, v_b_ln_g, v_b_ln_b, v_ab_w_out, v_mla_w_in, v_mla_q_norm_g, v_mla_w_uq, v_mla_kv_norm_g, v_mla_w_ukv, v_mla_w_o, v_ffn_w_up, v_ffn_conv_w, v_ffn_conv_b, v_ffn_w_down, v_final_norm_g):
    return _step(dict(locals()))
```

```python
import functools
import math

import jax
import jax.numpy as jnp
from jax import lax
from jax.experimental import pallas as pl
from jax.experimental.pallas import tpu as pltpu

F32 = jnp.float32
BF16 = jnp.bfloat16
SDS = jax.ShapeDtypeStruct

N_DEV = 8
EPS = 1e-6
CHUNK = 128
NOPE = 128
ROPE = 64
VDIM = 128
GRID_W = 64
ROPE_THETA = 10000.0
HALO = 16
ADAM_LR, ADAM_B1, ADAM_B2, ADAM_EPS, ADAM_WD, ADAM_STEP = 0.001, 0.9, 0.999, 1e-08, 0.01, 10
VMEM_LIMIT = 56 * 1024 * 1024


def _tile(n, prefs):
    for p in prefs:
        if n % p == 0:
            return p
    return n


def _params(sem, vmem=VMEM_LIMIT):
    return pltpu.CompilerParams(dimension_semantics=sem, vmem_limit_bytes=vmem)


def _sigmoid(x):
    return 0.5 * jnp.tanh(0.5 * x) + 0.5


def _silu(x):
    return x * _sigmoid(x)


def _dsilu(x):
    s = _sigmoid(x)
    return s * (1.0 + x * (1.0 - s))


_GELU_C = math.sqrt(2.0 / math.pi)


def _gelu(x):
    return 0.5 * x * (1.0 + jnp.tanh(_GELU_C * (x + 0.044715 * x * x * x)))


def _dgelu(x):
    t = jnp.tanh(_GELU_C * (x + 0.044715 * x * x * x))
    return 0.5 * (1.0 + t) + 0.5 * x * (1.0 - t * t) * _GELU_C * (1.0 + 3.0 * 0.044715 * x * x)


def _colsum(v):
    return jnp.sum(v, axis=0, keepdims=True)


_SIBLING = 1
_CHIPS = (2, 4, 6)


def _xchg(x_ref, o_ref, send_sems, recv_sems, local_sem, scatter):
    ax, ay, ac = lax.axis_index("x"), lax.axis_index("y"), lax.axis_index("c")
    me = 4 * ax + 2 * ay + ac

    def dev(k):
        return ax ^ (k >> 2), ay ^ ((k >> 1) & 1), ac ^ (k & 1)

    def idx(k):
        px, py, pc = dev(k)
        return 4 * px + 2 * py + pc

    def copy(k, src, dst, to):
        return pltpu.make_async_remote_copy(src_ref=src, dst_ref=dst, send_sem=send_sems.at[k - 1],
                                            recv_sem=recv_sems.at[k - 1], device_id=dev(to),
                                            device_id_type=pl.DeviceIdType.MESH)

    def own():
        return pltpu.make_async_copy(x_ref.at[me] if scatter else x_ref, o_ref.at[me], local_sem)

    def sends():
        if scatter:
            return [copy(k, x_ref.at[idx(k)], o_ref.at[me], k) for k in range(1, N_DEV)]
        return [copy(k, x_ref, o_ref.at[me], k) for k in (_SIBLING,) + _CHIPS]

    def forwards():
        return [] if scatter else [copy(j + 1, o_ref.at[idx(j)], o_ref.at[idx(j)], _SIBLING) for j in _CHIPS]

    def arrival(k):
        return copy(k, o_ref.at[idx(k)], o_ref.at[idx(k)], k)

    return own, sends, forwards, arrival


def _xchg_start(*refs, scatter):
    own, sends, _, _ = _xchg(*refs, scatter)
    own().start()
    for cp in sends():
        cp.start()


def _xchg_forward(*refs, scatter):
    _, _, forwards, arrival = _xchg(*refs, scatter)
    if not scatter:
        for j, fw in zip(_CHIPS, forwards()):
            arrival(j).wait_recv()
            fw.start()


def _xchg_finish(*refs, scatter):
    own, sends, forwards, arrival = _xchg(*refs, scatter)
    for k in range(1, N_DEV):
        if scatter or k not in _CHIPS:
            arrival(k).wait_recv()
    for cp in sends() + forwards():
        cp.wait_send()
    own().wait()


_XCHG_SEMS = [pltpu.SemaphoreType.DMA((N_DEV - 1,)), pltpu.SemaphoreType.DMA((N_DEV - 1,)), pltpu.SemaphoreType.DMA]


def _xchg_shape(x, scatter):
    return SDS((N_DEV,) + tuple(x.shape[1:] if scatter else x.shape), x.dtype)


def _exchange(x, *, scatter, name):
    def body(*refs):
        _xchg_start(*refs, scatter=scatter)
        _xchg_forward(*refs, scatter=scatter)
        _xchg_finish(*refs, scatter=scatter)

    return pl.pallas_call(
        body, out_shape=_xchg_shape(x, scatter),
        in_specs=[pl.BlockSpec(memory_space=pl.ANY)], out_specs=pl.BlockSpec(memory_space=pl.ANY),
        scratch_shapes=list(_XCHG_SEMS), name=name)(x)


def _carried(body, carry, n_in, n_out, n_scratch, grid):
    nc = len(carry)
    total = math.prod(grid)
    mid = (3 * total) // 4

    def wrapped(*refs):
        ins, cin = refs[:n_in], refs[n_in:n_in + nc]
        o0 = n_in + nc
        outs, cout = refs[o0:o0 + n_out], refs[o0 + n_out:o0 + n_out + nc]
        scr = refs[o0 + n_out + nc:]
        sems = scr[n_scratch:]
        step = pl.program_id(0)
        for ax in range(1, len(grid)):
            step = step * grid[ax] + pl.program_id(ax)

        def each(fn):
            for c in range(nc):
                fn(cin[c], cout[c], *sems[3 * c:3 * c + 3], scatter=carry[c][1])

        @pl.when(step == 0)
        def _():
            each(_xchg_start)

        body(*ins, *outs, *scr[:n_scratch])

        if mid < total - 1:
            @pl.when(step == mid)
            def _():
                each(_xchg_forward)

        @pl.when(step == total - 1)
        def _():
            if mid >= total - 1:
                each(_xchg_forward)
            each(_xchg_finish)

    return wrapped


def _carry_call(body, carry, *, grid, out_shape, in_specs, out_specs, scratch_shapes, sem, name, ins, aliases=None):
    carry = carry or []
    nc = len(carry)
    if nc:
        body = _carried(body, carry, len(in_specs), len(out_shape), len(scratch_shapes), grid)
        anyspec = pl.BlockSpec(memory_space=pl.ANY)
        in_specs = list(in_specs) + [anyspec] * nc
        out_specs = list(out_specs) + [anyspec] * nc
        out_shape = list(out_shape) + [_xchg_shape(x, sc) for x, sc in carry]
        scratch_shapes = list(scratch_shapes) + list(_XCHG_SEMS) * nc
        ins = list(ins) + [x for x, _ in carry]
        sem = ("arbitrary",) * len(grid)
    out = pl.pallas_call(body, out_shape=out_shape, grid=grid, in_specs=in_specs, out_specs=out_specs,
                         scratch_shapes=scratch_shapes, compiler_params=_params(sem), name=name,
                         input_output_aliases=aliases or {})(*ins)
    n_main = len(out) - nc
    return list(out[:n_main]), list(out[n_main:])


def _all_gather(x, name):
    return _exchange(x, scatter=False, name=name)


def _all_to_all(x, name):
    return _exchange(x, scatter=True, name=name)


def _sum_lead(x, name):
    n, R, C = x.shape
    tr = _tile(R, (512, 256, 128, 64, 32, 16, 8))

    def body(x_ref, o_ref):
        acc = x_ref[0]
        for d in range(1, n):
            acc = acc + x_ref[d]
        o_ref[...] = acc

    return pl.pallas_call(
        body, out_shape=SDS((R, C), F32), grid=(R // tr,),
        in_specs=[pl.BlockSpec((n, tr, C), lambda i: (0, i, 0))], out_specs=pl.BlockSpec((tr, C), lambda i: (i, 0)),
        compiler_params=_params(("parallel",)), name=name)(x)


_TP = (1408, 1088, 1024, 768, 512, 256, 128)
_TQ = (1408, 1024, 768, 512, 256, 128)
_TR = (2048, 1408, 1024, 768, 512, 256, 128)
_TR_TN = (1088, 1024, 512, 256, 128)


def _mm(a, b, *, mode, out_dtype, name, rows=None, bias=None, res=None, gate=None, seg_t=None, a_silu=False, carry=None,
        b_dev=False, out_dev=False, p_range=None):
    if mode == "nn":
        P, R, Q = rows or a.shape[0], a.shape[1], (b.shape[0] * b.shape[2] if b_dev else b.shape[1])
    elif mode == "nt":
        P, R, Q = rows or a.shape[0], a.shape[1], (b.shape[1] if b_dev else b.shape[0])
    else:
        R, P, Q = rows or a.shape[0], a.shape[1], b.shape[1]
    p0 = 0
    if p_range is not None:
        p0, P = p_range
    tp = _tile(P, _TP)
    tq = _tile(Q // N_DEV if (out_dev or (b_dev and mode == "nn")) else Q, _TQ)
    tr = _tile(R // N_DEV if (b_dev and mode == "nt") else R, _TR if mode != "tn" else _TR_TN)
    nk = R // tr
    qd = (Q // N_DEV) // tq
    rd = (R // N_DEV) // tr
    if mode == "nn":
        a_spec = pl.BlockSpec((tp, tr), lambda i, j, k: (i, k))
        b_spec = (pl.BlockSpec((None, tr, tq), lambda i, j, k: (j // qd, k, j % qd)) if b_dev
                  else pl.BlockSpec((tr, tq), lambda i, j, k: (k, j)))
        dims = (((1,), (0,)), ((), ()))
    elif mode == "nt":
        a_spec = pl.BlockSpec((tp, tr), lambda i, j, k: (i, k))
        b_spec = (pl.BlockSpec((None, tq, tr), lambda i, j, k: (k // rd, j, k % rd)) if b_dev
                  else pl.BlockSpec((tq, tr), lambda i, j, k: (j, k)))
        dims = (((1,), (1,)), ((), ()))
    else:
        pb = p0 // tp
        a_spec = pl.BlockSpec((tr, tp), lambda i, j, k: (k, i + pb))
        b_spec = pl.BlockSpec((tr, tq), lambda i, j, k: (k, j))
        dims = (((0,), (0,)), ((), ()))
    ins, in_specs = [a, b], [a_spec, b_spec]
    if bias is not None:
        ins.append(bias)
        in_specs.append(pl.BlockSpec((1, tq), lambda i, j, k: (0, j)))
    gated = res is not None
    if gated:
        n_seg = gate.shape[0]
        ins += [res, gate]
        in_specs += [pl.BlockSpec((tp, tq), lambda i, j, k: (i, j)),
                     pl.BlockSpec((n_seg, 1, tq), lambda i, j, k: (0, 0, j))]
    if out_dev:
        out_shape = [SDS((N_DEV, P, Q // N_DEV), out_dtype)]
        out_specs = [pl.BlockSpec((None, tp, tq), lambda i, j, k: (j // qd, i, j % qd))]
    else:
        out_shape = [SDS((P, Q), out_dtype)]
        out_specs = [pl.BlockSpec((tp, tq), lambda i, j, k: (i, j))]
    if gated:
        out_shape.append(SDS((P, Q), BF16))
        out_specs.append(pl.BlockSpec((tp, tq), lambda i, j, k: (i, j)))

    def body(*refs):
        a_ref, b_ref = refs[0], refs[1]
        pos = 2
        bias_ref = res_ref = gate_ref = o2_ref = None
        if bias is not None:
            bias_ref = refs[pos]
            pos += 1
        if gated:
            res_ref, gate_ref = refs[pos], refs[pos + 1]
            pos += 2
        o_ref = refs[pos]
        pos += 1
        if gated:
            o2_ref = refs[pos]
            pos += 1
        acc_ref = refs[pos] if nk > 1 else None
        k = pl.program_id(2)
        av = a_ref[...]
        if a_silu:
            av = _silu(av.astype(F32))
        part = lax.dot_general(av.astype(BF16), b_ref[...].astype(BF16), dims, preferred_element_type=F32)
        if nk > 1:
            @pl.when(k == 0)
            def _():
                acc_ref[...] = part

            @pl.when(k > 0)
            def _():
                acc_ref[...] += part

        @pl.when(k == nk - 1)
        def _():
            acc = acc_ref[...] if nk > 1 else part
            if bias_ref is not None:
                acc = acc + bias_ref[...]
            if gated:
                if n_seg == 1:
                    g = gate_ref[0]
                else:
                    row = pl.program_id(0) * tp + lax.broadcasted_iota(jnp.int32, (tp, 1), 0)
                    g = jnp.where(row < seg_t, gate_ref[0], gate_ref[1])
                o_ref[...] = (res_ref[...] + g * acc).astype(o_ref.dtype)
                o2_ref[...] = acc.astype(BF16)
            else:
                o_ref[...] = acc.astype(o_ref.dtype)

    out, carried = _carry_call(
        body, carry, grid=(P // tp, Q // tq, nk), out_shape=out_shape, in_specs=in_specs, out_specs=out_specs,
        scratch_shapes=[pltpu.VMEM((tp, tq), F32)] if nk > 1 else [], sem=("parallel", "parallel", "arbitrary"),
        name=name, ins=ins)
    res_out = tuple(out) if gated else out[0]
    return (res_out, carried) if carry else res_out


def _row_tile(seg_t, m):
    return 256 if (seg_t % 256 == 0 and m % 256 == 0) else 128


def _normmod_fwd(x, g, sh, sc, seg_t, name):
    M, D = x.shape
    tm = _row_tile(seg_t, M)
    n_seg = sh.shape[0]
    nt = seg_t // tm

    def seg(i):
        return ((i >= nt).astype(jnp.int32) if n_seg == 2 else 0, 0, 0)

    def body(x_ref, g_ref, sh_ref, sc_ref, o_ref):
        xv = x_ref[...]
        r = lax.rsqrt(jnp.mean(xv * xv, axis=-1, keepdims=True) + EPS)
        y = xv * r * g_ref[...]
        o_ref[...] = (y * (1.0 + sc_ref[0]) + sh_ref[0]).astype(BF16)

    return pl.pallas_call(
        body, out_shape=SDS((M, D), BF16), grid=(M // tm,),
        in_specs=[pl.BlockSpec((tm, D), lambda i: (i, 0)), pl.BlockSpec((1, D), lambda i: (0, 0)),
                  pl.BlockSpec((1, 1, D), seg), pl.BlockSpec((1, 1, D), seg)],
        out_specs=pl.BlockSpec((tm, D), lambda i: (i, 0)),
        compiler_params=_params(("parallel",)), name=name)(x, g, sh, sc)


def _normmod_bwd(x, g, sc, dh, dx_in, seg_t, name, o_prev=None, gate_prev=None):
    M, D = x.shape
    tm = _row_tile(seg_t, M)
    n_seg = sc.shape[0]
    nt = seg_t // tm
    n_in = dx_in.shape[0] // tm
    with_prev = o_prev is not None
    n_segp = gate_prev.shape[0] if with_prev else 0

    def seg(i):
        return ((i >= nt).astype(jnp.int32) if n_seg == 2 else 0, 0, 0)

    def segp(i):
        return ((i >= nt).astype(jnp.int32) if n_segp == 2 else 0, 0, 0)

    def body(*refs):
        x_ref, g_ref, sc_ref, dh_ref, dxin_ref = refs[:5]
        pos = 5
        if with_prev:
            op_ref, gp_ref = refs[5], refs[6]
            pos = 7
        dx_ref, dg_ref, dsh_ref, dsc_ref = refs[pos:pos + 4]
        if with_prev:
            dop_ref, dgp_ref = refs[pos + 4], refs[pos + 5]
        i = pl.program_id(0)
        xv = x_ref[...]
        r = lax.rsqrt(jnp.mean(xv * xv, axis=-1, keepdims=True) + EPS)
        xh = xv * r
        gv = g_ref[...]
        dhv = dh_ref[...].astype(F32)
        dy = dhv * (1.0 + sc_ref[0])
        dxh = dy * gv
        dxv = r * (dxh - xh * jnp.mean(dxh * xh, axis=-1, keepdims=True))
        if n_in * tm < M:
            dxv = dxv + jnp.where(i < n_in, dxin_ref[...], 0.0)
        else:
            dxv = dxv + dxin_ref[...]
        dx_ref[...] = dxv

        @pl.when(i == 0)
        def _():
            dg_ref[...] = jnp.zeros_like(dg_ref)

        first_of_seg = (i == 0) | (i == nt) if n_seg == 2 else (i == 0)

        @pl.when(first_of_seg)
        def _():
            dsh_ref[...] = jnp.zeros_like(dsh_ref)
            dsc_ref[...] = jnp.zeros_like(dsc_ref)

        dg_ref[...] += _colsum(dy * xh)
        dsh_ref[0] += _colsum(dhv)
        dsc_ref[0] += _colsum(dhv * xh * gv)
        if with_prev:
            first_of_segp = (i == 0) | (i == nt) if n_segp == 2 else (i == 0)

            @pl.when(first_of_segp)
            def _():
                dgp_ref[...] = jnp.zeros_like(dgp_ref)

            dop_ref[...] = (gp_ref[0] * dxv).astype(BF16)
            dgp_ref[0] += _colsum(dxv * op_ref[...].astype(F32))

    row = pl.BlockSpec((tm, D), lambda i: (i, 0))
    ins = [x, g, sc, dh, dx_in]
    in_specs = [row, pl.BlockSpec((1, D), lambda i: (0, 0)), pl.BlockSpec((1, 1, D), seg), row,
                pl.BlockSpec((tm, D), lambda i: (jnp.minimum(i, n_in - 1), 0))]
    out_shape = [SDS((M, D), F32), SDS((1, D), F32), SDS((n_seg, 1, D), F32), SDS((n_seg, 1, D), F32)]
    out_specs = [row, pl.BlockSpec((1, D), lambda i: (0, 0)), pl.BlockSpec((1, 1, D), seg), pl.BlockSpec((1, 1, D), seg)]
    if with_prev:
        ins += [o_prev, gate_prev]
        in_specs += [row, pl.BlockSpec((1, 1, D), segp)]
        out_shape += [SDS((M, D), BF16), SDS((n_segp, 1, D), F32)]
        out_specs += [row, pl.BlockSpec((1, 1, D), segp)]
    return pl.pallas_call(
        body, out_shape=out_shape, grid=(M // tm,), in_specs=in_specs, out_specs=out_specs,
        compiler_params=_params(("arbitrary",)), name=name)(*ins)


def _final(x, g, target, o_prev, gate_prev, name):
    T, D = x.shape
    tm = _tile(T, (256, 128))

    def body(x_ref, g_ref, t_ref, op_ref, gp_ref, dx_ref, loss_ref, dg_ref, dop_ref, dgp_ref):
        i = pl.program_id(0)
        xv = x_ref[...]
        r = lax.rsqrt(jnp.mean(xv * xv, axis=-1, keepdims=True) + EPS)
        xh = xv * r
        gv = g_ref[...]
        e = xh * gv - t_ref[...]
        dout = e * (1.0 / D)
        dxh = dout * gv
        dxv = r * (dxh - xh * jnp.mean(dxh * xh, axis=-1, keepdims=True))
        dx_ref[...] = dxv
        dop_ref[...] = (gp_ref[0] * dxv).astype(BF16)

        @pl.when(i == 0)
        def _():
            loss_ref[...] = jnp.zeros_like(loss_ref)
            dg_ref[...] = jnp.zeros_like(dg_ref)
            dgp_ref[...] = jnp.zeros_like(dgp_ref)

        loss_ref[...] += _colsum(e * e) * (0.5 / D)
        dg_ref[...] += _colsum(dout * xh)
        dgp_ref[0] += _colsum(dxv * op_ref[...].astype(F32))

    row = pl.BlockSpec((tm, D), lambda i: (i, 0))
    vec = pl.BlockSpec((1, D), lambda i: (0, 0))
    vec3 = pl.BlockSpec((1, 1, D), lambda i: (0, 0, 0))
    return pl.pallas_call(
        body, out_shape=[SDS((T, D), F32), SDS((1, D), F32), SDS((1, D), F32), SDS((T, D), BF16), SDS((1, 1, D), F32)],
        grid=(T // tm,), in_specs=[row, vec, row, row, vec3], out_specs=[row, vec, vec, row, vec3],
        compiler_params=_params(("arbitrary",)), name=name)(x, g, target, o_prev, gate_prev)


def _gmlp_core(z, lg, lb, ws_ref, bst):
    W = z.shape[1] // 2
    t = _gelu(z)
    u, v = t[:, :W], t[:, W:]
    mu = jnp.mean(v, axis=-1, keepdims=True)
    vc = v - mu
    rstd = lax.rsqrt(jnp.mean(vc * vc, axis=-1, keepdims=True) + EPS)
    vhat = vc * rstd
    vn = vhat * lg + lb
    vp = []
    for h in range(W // CHUNK):
        blk = vn[:, h * CHUNK:(h + 1) * CHUNK].astype(BF16)
        vp.append(jnp.dot(ws_ref[h].astype(BF16), blk, preferred_element_type=F32) + bst[:, h:h + 1])
    return u, vhat, rstd, vp


def _gmlp_fwd(z, ln_g, ln_b, w_s, b_st, W, name):
    M = z.shape[0]
    H = W // CHUNK

    def body(z_ref, lg_ref, lb_ref, ws_ref, bst_ref, o_ref):
        u, _, _, vp = _gmlp_core(z_ref[...], lg_ref[...], lb_ref[...], ws_ref, bst_ref[...])
        for h in range(H):
            o_ref[:, h * CHUNK:(h + 1) * CHUNK] = (u[:, h * CHUNK:(h + 1) * CHUNK] * vp[h]).astype(BF16)

    vec = pl.BlockSpec((1, W), lambda i: (0, 0))
    return pl.pallas_call(
        body, out_shape=SDS((M, 2 * W), BF16), grid=(M // CHUNK,),
        in_specs=[pl.BlockSpec((CHUNK, 2 * W), lambda i: (i, 0)), vec, vec,
                  pl.BlockSpec((H, CHUNK, CHUNK), lambda i: (0, 0, 0)), pl.BlockSpec((CHUNK, H), lambda i: (0, 0))],
        out_specs=pl.BlockSpec((CHUNK, W), lambda i: (i, 0)),
        compiler_params=_params(("parallel",)), name=name)(z, ln_g, ln_b, w_s, b_st)


def _gmlp_bwd(z, dy, ln_g, ln_b, w_s, b_st, W, name):
    M = z.shape[0]
    H = W // CHUNK
    ZW = z.shape[1]

    def body(z_ref, dy_ref, lg_ref, lb_ref, ws_ref, bst_ref, dz_ref, dlg_ref, dlb_ref, dws_ref, dbs_ref, dbin_ref):
        i = pl.program_id(0)

        @pl.when(i == 0)
        def _():
            for r in (dlg_ref, dlb_ref, dws_ref, dbs_ref, dbin_ref):
                r[...] = jnp.zeros_like(r)

        zv = z_ref[...]
        lg = lg_ref[...]
        u, vhat, rstd, vp = _gmlp_core(zv, lg, lb_ref[...], ws_ref, bst_ref[...])
        vn = vhat * lg + lb_ref[...]
        dya = dy_ref[...]
        du_parts, dvn_parts = [], []
        for h in range(H):
            sl = slice(h * CHUNK, (h + 1) * CHUNK)
            dya_h = dya[:, sl]
            du_parts.append(dya_h * vp[h])
            dvp = dya_h * u[:, sl]
            dbs_ref[h] += dvp
            dvp16 = dvp.astype(BF16)
            dws_ref[h] += lax.dot_general(dvp16, vn[:, sl].astype(BF16), (((1,), (1,)), ((), ())),
                                          preferred_element_type=F32)
            dvn_parts.append(lax.dot_general(ws_ref[h].astype(BF16), dvp16, (((0,), (0,)), ((), ())),
                                             preferred_element_type=F32))
        du = jnp.concatenate(du_parts, axis=1)
        dvn = jnp.concatenate(dvn_parts, axis=1)
        dlg_ref[...] += _colsum(dvn * vhat)
        dlb_ref[...] += _colsum(dvn)
        dvh = dvn * lg
        dv = rstd * (dvh - jnp.mean(dvh, axis=-1, keepdims=True) - vhat * jnp.mean(dvh * vhat, axis=-1, keepdims=True))
        dz = jnp.concatenate([du, dv], axis=1) * _dgelu(zv)
        dbin_ref[...] += _colsum(dz)
        dz_ref[...] = dz.astype(BF16)

    vec = pl.BlockSpec((1, W), lambda i: (0, 0))
    mat = pl.BlockSpec((H, CHUNK, CHUNK), lambda i: (0, 0, 0))
    return pl.pallas_call(
        body,
        out_shape=[SDS((M, ZW), BF16), SDS((1, W), F32), SDS((1, W), F32), SDS((H, CHUNK, CHUNK), F32),
                   SDS((H, CHUNK, CHUNK), F32), SDS((1, 2 * W), F32)],
        grid=(M // CHUNK,),
        in_specs=[pl.BlockSpec((CHUNK, 2 * W), lambda i: (i, 0)), pl.BlockSpec((CHUNK, W), lambda i: (i, 0)), vec, vec,
                  mat, pl.BlockSpec((CHUNK, H), lambda i: (0, 0))],
        out_specs=[pl.BlockSpec((CHUNK, 2 * W), lambda i: (i, 0)), vec, vec, mat, mat,
                   pl.BlockSpec((1, 2 * W), lambda i: (0, 0))],
        compiler_params=_params(("arbitrary",)), name=name)(z, dy, ln_g, ln_b, w_s, b_st)


def _halo_specs(tm, width, col, n_rows):
    per = tm // HALO
    last = n_rows // HALO - 1
    prev = pl.BlockSpec((HALO, width), lambda i: (jnp.maximum(i * per - 1, 0), col))
    nxt = pl.BlockSpec((HALO, width), lambda i: (jnp.minimum((i + 1) * per, last), col))
    return prev, nxt


def _edge_flags(i, tm, seg_t, m):
    r0 = i * tm
    has_prev = jnp.where((r0 == 0) | (r0 == seg_t), 0.0, 1.0)
    has_next = jnp.where((r0 + tm == seg_t) | (r0 + tm == m), 0.0, 1.0)
    return has_prev, has_next


def _glu(zz, wb):
    return zz[:, :wb] * _sigmoid(zz[:, wb:])


def _build_shifts(src_ref, sh_ref):
    n = src_ref.shape[0] - 8
    for r in range(1, 8):
        sh_ref[r - 1, pl.ds(0, n), :] = src_ref[pl.ds(r, n), :]


def _shifted(src_ref, sh_ref, off, r0, rc):
    a, r = divmod(off, 8)
    if r == 0:
        return src_ref[pl.ds(8 * a + r0, rc), :]
    return sh_ref[r - 1, pl.ds(8 * a + r0, rc), :]


def _conv_taps(src_ref, sh_ref, w_ref, first, tm, kw, flip=False):
    rc = 32
    parts = []
    for c in range(tm // rc):
        acc = None
        for k in range(kw):
            wk = w_ref[pl.ds(kw - 1 - k if flip else k, 1), :]
            term = _shifted(src_ref, sh_ref, first + k, c * rc, rc) * wk
            acc = term if acc is None else acc + term
        parts.append(acc)
    return jnp.concatenate(parts, axis=0)


def _conf_fwd(z, y, conv_w, conv_b, ln_g, ln_b, W, Wb, seg_t, name, carry=None):
    M = z.shape[0]
    tm = _row_tile(seg_t, M)
    kw = conv_w.shape[0]
    pad = (kw - 1) // 2
    col = (2 * W) // (2 * Wb)

    def body(zc_ref, zp_ref, zn_ref, y_hbm, cw_ref, cb_ref, lg_ref, lb_ref, o_ref, hc_ref, hs_ref, sh_ref):
        del y_hbm
        hp, hn = _edge_flags(pl.program_id(0), tm, seg_t, M)
        hs_ref[pl.ds(0, HALO), :] = _glu(zp_ref[...], Wb) * hp
        hs_ref[pl.ds(HALO, tm), :] = _glu(zc_ref[...], Wb)
        hs_ref[pl.ds(HALO + tm, HALO), :] = _glu(zn_ref[...], Wb) * hn
        _build_shifts(hs_ref, sh_ref)
        hc = _conv_taps(hs_ref, sh_ref, cw_ref, HALO - pad, tm, kw) + cb_ref[...]
        hc_ref[...] = hc
        mu = jnp.mean(hc, axis=-1, keepdims=True)
        c = hc - mu
        rstd = lax.rsqrt(jnp.mean(c * c, axis=-1, keepdims=True) + EPS)
        o_ref[...] = _silu(c * rstd * lg_ref[...] + lb_ref[...]).astype(BF16)

    prev, nxt = _halo_specs(tm, 2 * Wb, col, M)
    vec = pl.BlockSpec((1, Wb), lambda i: (0, 0))
    out, carried = _carry_call(
        body, carry, out_shape=[SDS(y.shape, BF16), SDS((M, Wb), F32)], grid=(M // tm,),
        in_specs=[pl.BlockSpec((tm, 2 * Wb), lambda i: (i, col)), prev, nxt, pl.BlockSpec(memory_space=pl.ANY),
                  pl.BlockSpec((kw, Wb), lambda i: (0, 0)), vec, vec, vec],
        out_specs=[pl.BlockSpec((tm, Wb), lambda i: (i, W // Wb)), pl.BlockSpec((tm, Wb), lambda i: (i, 0))],
        scratch_shapes=[pltpu.VMEM((tm + 2 * HALO, Wb), F32), pltpu.VMEM((7, tm + 2 * HALO, Wb), F32)],
        aliases={3: 0}, sem=("parallel",), name=name, ins=(z, z, z, y, conv_w, conv_b, ln_g, ln_b))
    return (out, carried) if carry else out


def _conf_bwd1(hc, dy, ln_g, ln_b, W, Wb, seg_t, name):
    M = hc.shape[0]
    tm = _row_tile(seg_t, M)

    def body(hc_ref, dy_ref, lg_ref, lb_ref, dhc_ref, dlg_ref, dlb_ref, dcb_ref):
        i = pl.program_id(0)

        @pl.when(i == 0)
        def _():
            for r in (dlg_ref, dlb_ref, dcb_ref):
                r[...] = jnp.zeros_like(r)

        hc = hc_ref[...]
        mu = jnp.mean(hc, axis=-1, keepdims=True)
        c = hc - mu
        rstd = lax.rsqrt(jnp.mean(c * c, axis=-1, keepdims=True) + EPS)
        hh = c * rstd
        lg = lg_ref[...]
        dhn = dy_ref[...] * _dsilu(hh * lg + lb_ref[...])
        dlg_ref[...] += _colsum(dhn * hh)
        dlb_ref[...] += _colsum(dhn)
        dhh = dhn * lg
        dhc = rstd * (dhh - jnp.mean(dhh, axis=-1, keepdims=True) - hh * jnp.mean(dhh * hh, axis=-1, keepdims=True))
        dcb_ref[...] += _colsum(dhc)
        dhc_ref[...] = dhc

    vec = pl.BlockSpec((1, Wb), lambda i: (0, 0))
    return pl.pallas_call(
        body, out_shape=[SDS((M, Wb), F32), SDS((1, Wb), F32), SDS((1, Wb), F32), SDS((1, Wb), F32)], grid=(M // tm,),
        in_specs=[pl.BlockSpec((tm, Wb), lambda i: (i, 0)), pl.BlockSpec((tm, Wb), lambda i: (i, W // Wb)), vec, vec],
        out_specs=[pl.BlockSpec((tm, Wb), lambda i: (i, 0)), vec, vec, vec],
        compiler_params=_params(("arbitrary",)), name=name)(hc, dy, ln_g, ln_b)


def _conf_bwd2(z, dhc, dz, conv_w, W, Wb, seg_t, name, carry=None):
    M = z.shape[0]
    tm = _row_tile(seg_t, M)
    kw = conv_w.shape[0]
    pad = (kw - 1) // 2
    col = (2 * W) // (2 * Wb)

    def body(zc_ref, zp_ref, zn_ref, dc_ref, dp_ref, dn_ref, dz_hbm, cw_ref, dz_ref, dcw_ref, dbin_ref, hs_ref, ds_ref,
             hsh_ref, dsh_ref):
        del dz_hbm
        i = pl.program_id(0)

        @pl.when(i == 0)
        def _():
            dcw_ref[...] = jnp.zeros_like(dcw_ref)
            dbin_ref[...] = jnp.zeros_like(dbin_ref)

        hp, hn = _edge_flags(i, tm, seg_t, M)
        zc = zc_ref[...]
        hs_ref[pl.ds(0, HALO), :] = _glu(zp_ref[...], Wb) * hp
        hs_ref[pl.ds(HALO, tm), :] = _glu(zc, Wb)
        hs_ref[pl.ds(HALO + tm, HALO), :] = _glu(zn_ref[...], Wb) * hn
        dcur = dc_ref[...]
        ds_ref[pl.ds(0, HALO), :] = dp_ref[...] * hp
        ds_ref[pl.ds(HALO, tm), :] = dcur
        ds_ref[pl.ds(HALO + tm, HALO), :] = dn_ref[...] * hn
        _build_shifts(ds_ref, dsh_ref)
        _build_shifts(hs_ref, hsh_ref)
        dh = _conv_taps(ds_ref, dsh_ref, cw_ref, HALO - pad, tm, kw, flip=True)
        for k in range(kw):
            dcw_ref[pl.ds(k, 1), :] += _colsum(dcur * _shifted(hs_ref, hsh_ref, HALO - pad + k, 0, tm))
        a, gt = zc[:, :Wb], zc[:, Wb:]
        s = _sigmoid(gt)
        dz = jnp.concatenate([dh * s, dh * a * s * (1.0 - s)], axis=1)
        dbin_ref[...] += _colsum(dz)
        dz_ref[...] = dz.astype(BF16)

    prev, nxt = _halo_specs(tm, 2 * Wb, col, M)
    dprev, dnxt = _halo_specs(tm, Wb, 0, M)
    out, carried = _carry_call(
        body, carry, out_shape=[SDS(dz.shape, BF16), SDS((kw, Wb), F32), SDS((1, 2 * Wb), F32)], grid=(M // tm,),
        in_specs=[pl.BlockSpec((tm, 2 * Wb), lambda i: (i, col)), prev, nxt,
                  pl.BlockSpec((tm, Wb), lambda i: (i, 0)), dprev, dnxt, pl.BlockSpec(memory_space=pl.ANY),
                  pl.BlockSpec((kw, Wb), lambda i: (0, 0))],
        out_specs=[pl.BlockSpec((tm, 2 * Wb), lambda i: (i, col)), pl.BlockSpec((kw, Wb), lambda i: (0, 0)),
                   pl.BlockSpec((1, 2 * Wb), lambda i: (0, 0))],
        scratch_shapes=[pltpu.VMEM((tm + 2 * HALO, Wb), F32), pltpu.VMEM((tm + 2 * HALO, Wb), F32),
                        pltpu.VMEM((7, tm + 2 * HALO, Wb), F32), pltpu.VMEM((7, tm + 2 * HALO, Wb), F32)],
        aliases={6: 0}, sem=("arbitrary",), name=name, ins=(z, z, z, dhc, dhc, dhc, dz, conv_w))
    return (out, carried) if carry else out


_TF = (1408, 512, 256, 128)
_RC = 16
_CG = 256


def _col_groups(width):
    return [(c0, min(_CG, width - c0)) for c0 in range(0, width, _CG)]


def _ffn_act_fwd(z, conv_w, conv_b, seg_t, name):
    M, F2 = z.shape
    Fd = F2 // 2
    tm = _row_tile(seg_t, M)
    tf = _tile(Fd, _TF)
    nf = Fd // tf
    per, last = tm // HALO, M // HALO - 1

    def body(g_ref, gp_ref, gn_ref, u_ref, cw_ref, cb_ref, o_ref, gc_ref, gs_ref):
        hp, hn = _edge_flags(pl.program_id(0), tm, seg_t, M)
        gs_ref[pl.ds(0, HALO), :] = gp_ref[...].astype(F32) * hp
        gs_ref[pl.ds(HALO, tm), :] = g_ref[...].astype(F32)
        gs_ref[pl.ds(HALO + tm, HALO), :] = gn_ref[...].astype(F32) * hn
        for c0, cw in _col_groups(tf):
            cs = pl.ds(c0, cw)
            w0, w1, w2, cb = cw_ref[pl.ds(0, 1), cs], cw_ref[pl.ds(1, 1), cs], cw_ref[pl.ds(2, 1), cs], cb_ref[:, cs]
            for r0 in range(0, tm, _RC):
                gc = (gs_ref[pl.ds(HALO - 1 + r0, _RC), cs] * w0 + gs_ref[pl.ds(HALO + r0, _RC), cs] * w1
                      + gs_ref[pl.ds(HALO + 1 + r0, _RC), cs] * w2 + cb)
                o_ref[pl.ds(r0, _RC), cs] = (_silu(gc) * u_ref[pl.ds(r0, _RC), cs].astype(F32)).astype(BF16)
                gc_ref[pl.ds(r0, _RC), cs] = gc.astype(BF16)

    return pl.pallas_call(
        body, out_shape=[SDS((M, Fd), BF16), SDS((M, Fd), BF16)], grid=(M // tm, nf),
        in_specs=[pl.BlockSpec((tm, tf), lambda i, j: (i, j)),
                  pl.BlockSpec((HALO, tf), lambda i, j: (jnp.maximum(i * per - 1, 0), j)),
                  pl.BlockSpec((HALO, tf), lambda i, j: (jnp.minimum((i + 1) * per, last), j)),
                  pl.BlockSpec((tm, tf), lambda i, j: (i, nf + j)),
                  pl.BlockSpec((3, tf), lambda i, j: (0, j)), pl.BlockSpec((1, tf), lambda i, j: (0, j))],
        out_specs=[pl.BlockSpec((tm, tf), lambda i, j: (i, j)), pl.BlockSpec((tm, tf), lambda i, j: (i, j))],
        scratch_shapes=[pltpu.VMEM((tm + 2 * HALO, tf), F32)],
        compiler_params=_params(("parallel", "parallel")), name=name)(z, z, z, z, conv_w, conv_b)


def _ffn_act_bwd(z, gc, da, conv_w, seg_t, name):
    M, F2 = z.shape
    Fd = F2 // 2
    tm = _row_tile(seg_t, M)
    tf = _tile(Fd, _TF)
    nf = Fd // tf
    per, last = tm // HALO, M // HALO - 1
    n_piece = tm // _RC

    def body(g_ref, c_ref, cp_ref, cn_ref, u_ref, up_ref, un_ref, a_ref, ap_ref, an_ref, cw_ref,
             dz_ref, dcw_ref, dcb_ref, ds_ref, du_ref):
        i, p = pl.program_id(1), pl.program_id(2)

        @pl.when(p == 0)
        def _():
            hp, hn = _edge_flags(i, tm, seg_t, M)

            @pl.when(i == 0)
            def _():
                dcw_ref[...] = jnp.zeros_like(dcw_ref)
                dcb_ref[...] = jnp.zeros_like(dcb_ref)

            def fold(v):
                return v[:8] + v[8:]

            for c0, cw in _col_groups(tf):
                cs = pl.ds(c0, cw)
                w0, w1, w2 = cw_ref[pl.ds(0, 1), cs], cw_ref[pl.ds(1, 1), cs], cw_ref[pl.ds(2, 1), cs]
                for ci in range(-1, n_piece + 1):
                    r0 = ci * _RC
                    if ci < 0:
                        gcv, ue, ae = cp_ref[:, cs], up_ref[:, cs], ap_ref[:, cs].astype(F32) * hp
                    elif ci == n_piece:
                        gcv, ue, ae = cn_ref[:, cs], un_ref[:, cs], an_ref[:, cs].astype(F32) * hn
                    else:
                        rows = pl.ds(r0, _RC)
                        gcv, ue, ae = c_ref[rows, cs], u_ref[rows, cs], a_ref[rows, cs].astype(F32)
                    gcv, ue = gcv.astype(F32), ue.astype(F32)
                    sg = _sigmoid(gcv)
                    t = ae * sg
                    ds_ref[pl.ds(HALO + r0, _RC), cs] = t * ue * (1.0 + gcv * (1.0 - sg))
                    if 0 <= ci < n_piece:
                        du_ref[pl.ds(r0, _RC), cs] = (t * gcv).astype(BF16)
                acc = [jnp.zeros((8, cw), F32) for _ in range(4)]
                for r0 in range(0, tm, _RC):
                    b = HALO + r0
                    d = [ds_ref[pl.ds(b + 1 - k, _RC), cs] for k in range(3)]
                    dz_ref[pl.ds(r0, _RC), cs] = (d[0] * w0 + d[1] * w1 + d[2] * w2).astype(BF16)
                    gv = g_ref[pl.ds(r0, _RC), cs].astype(F32)
                    for k in range(3):
                        acc[k] = acc[k] + fold(d[k] * gv)
                    acc[3] = acc[3] + fold(d[1])
                for k in range(3):
                    dcw_ref[pl.ds(k, 1), cs] += _colsum(acc[k])
                dcb_ref[:, cs] += _colsum(acc[3])

        @pl.when(p == 1)
        def _():
            dz_ref[...] = du_ref[...]

    def cur(off):
        return pl.BlockSpec((tm, tf), lambda j, i, p: (i, off + j))

    def prv(off):
        return pl.BlockSpec((HALO, tf), lambda j, i, p: (jnp.maximum(i * per - 1, 0), off + j))

    def nxt(off):
        return pl.BlockSpec((HALO, tf), lambda j, i, p: (jnp.minimum((i + 1) * per, last), off + j))

    return pl.pallas_call(
        body, out_shape=[SDS((M, F2), BF16), SDS((3, Fd), F32), SDS((1, Fd), F32)], grid=(nf, M // tm, 2),
        in_specs=[cur(0), cur(0), prv(0), nxt(0), cur(nf), prv(nf), nxt(nf), cur(0), prv(0), nxt(0),
                  pl.BlockSpec((3, tf), lambda j, i, p: (0, j))],
        out_specs=[pl.BlockSpec((tm, tf), lambda j, i, p: (i, p * nf + j)),
                   pl.BlockSpec((3, tf), lambda j, i, p: (0, j)), pl.BlockSpec((1, tf), lambda j, i, p: (0, j))],
        scratch_shapes=[pltpu.VMEM((tm + 2 * HALO, tf), F32), pltpu.VMEM((tm, tf), BF16)],
        compiler_params=_params(("parallel", "arbitrary", "arbitrary")), name=name)(
            z, gc, gc, gc, z, z, z, da, da, da, conv_w)


_LN2 = math.log(2.0)
_QSCALE = (NOPE + ROPE) ** -0.5 / _LN2


def _swap32(x):
    lane = lax.broadcasted_iota(jnp.int32, x.shape, 1)
    return jnp.where((lane % 64) < 32, pltpu.roll(x, 96, axis=1), pltpu.roll(x, 32, axis=1))


def _rms(x, g):
    r = lax.rsqrt(jnp.mean(x * x, axis=-1, keepdims=True) + EPS)
    return x * r * g


def _rms_bwd(x, g, dy):
    r = lax.rsqrt(jnp.mean(x * x, axis=-1, keepdims=True) + EPS)
    xh = x * r
    dxh = dy * g
    return r * (dxh - xh * jnp.mean(dxh * xh, axis=-1, keepdims=True)), _colsum(dy * xh)


def _mla_prep_fwd(z, gq, gkv, cos, sin, QL, KL, name):
    M, NZ = z.shape
    tm = _tile(M, (256, 128))

    def body(z_ref, gq_ref, gkv_ref, cos_ref, sin_ref, cq_ref, ckv_ref, kpe_ref):
        zv = z_ref[...]
        cq_ref[...] = _rms(zv[:, :QL], gq_ref[...]).astype(BF16)
        ckv_ref[...] = _rms(zv[:, QL:QL + KL], gkv_ref[...]).astype(BF16)
        kp = zv[:, QL + KL:]
        r = kp * cos_ref[...] + _swap32(kp) * sin_ref[...]
        lane = lax.broadcasted_iota(jnp.int32, r.shape, 1)
        kpe_ref[0] = jnp.where(lane < ROPE, r, 0.0).astype(BF16)
        kpe_ref[1] = jnp.where(lane >= ROPE, r, 0.0).astype(BF16)

    tab = pl.BlockSpec((tm, 128), lambda i: (i, 0))
    return pl.pallas_call(
        body, out_shape=[SDS((M, QL), BF16), SDS((M, KL), BF16), SDS((2, M, 128), BF16)], grid=(M // tm,),
        in_specs=[pl.BlockSpec((tm, NZ), lambda i: (i, 0)), pl.BlockSpec((1, QL), lambda i: (0, 0)),
                  pl.BlockSpec((1, KL), lambda i: (0, 0)), tab, tab],
        out_specs=[pl.BlockSpec((tm, QL), lambda i: (i, 0)), pl.BlockSpec((tm, KL), lambda i: (i, 0)),
                   pl.BlockSpec((2, tm, 128), lambda i: (0, i, 0))],
        compiler_params=_params(("parallel",)), name=name)(z, gq, gkv, cos, sin)


def _mla_prep_bwd(z, dcq, dckv, dkpe, gq, gkv, cos, sin, QL, KL, seg_t, name):
    M, NZ = z.shape
    H = dkpe.shape[0]
    tm = _row_tile(seg_t, M)
    nt = seg_t // tm

    def body(z_ref, dcq_ref, dckv_ref, dkpe_ref, gq_ref, gkv_ref, cos_ref, sin_ref, dz_ref, dgq_ref, dgkv_ref):
        i = pl.program_id(0)

        @pl.when(i == 0)
        def _():
            dgq_ref[...] = jnp.zeros_like(dgq_ref)
            dgkv_ref[...] = jnp.zeros_like(dgkv_ref)

        zv = z_ref[...]
        dyq = jnp.where(i < nt, dcq_ref[...], 0.0)
        dxq, dgq = _rms_bwd(zv[:, :QL], gq_ref[...], dyq)
        dxkv, dgkv = _rms_bwd(zv[:, QL:QL + KL], gkv_ref[...], dckv_ref[...])
        dgq_ref[...] += dgq
        dgkv_ref[...] += dgkv
        even = dkpe_ref[0]
        odd = dkpe_ref[1]
        for h in range(2, H, 2):
            even = even + dkpe_ref[h]
            odd = odd + dkpe_ref[h + 1]
        lane = lax.broadcasted_iota(jnp.int32, even.shape, 1)
        dr = jnp.where(lane < ROPE, even, odd)
        dkp = dr * cos_ref[...] - _swap32(dr) * sin_ref[...]
        dz_ref[...] = jnp.concatenate([dxq, dxkv, dkp], axis=1).astype(BF16)

    tab = pl.BlockSpec((tm, 128), lambda i: (i, 0))
    return pl.pallas_call(
        body, out_shape=[SDS((M, NZ), BF16), SDS((1, QL), F32), SDS((1, KL), F32)], grid=(M // tm,),
        in_specs=[pl.BlockSpec((tm, NZ), lambda i: (i, 0)),
                  pl.BlockSpec((tm, QL), lambda i: (jnp.minimum(i, nt - 1), 0)),
                  pl.BlockSpec((tm, KL), lambda i: (i, 0)), pl.BlockSpec((H, tm, 128), lambda i: (0, i, 0)),
                  pl.BlockSpec((1, QL), lambda i: (0, 0)), pl.BlockSpec((1, KL), lambda i: (0, 0)), tab, tab],
        out_specs=[pl.BlockSpec((tm, NZ), lambda i: (i, 0)), pl.BlockSpec((1, QL), lambda i: (0, 0)),
                   pl.BlockSpec((1, KL), lambda i: (0, 0))],
        compiler_params=_params(("arbitrary",)), name=name)(z, dcq, dckv, dkpe, gq, gkv, cos, sin)


def _qrope_fwd(q, cos, sin, HN, name):
    T, NQ = q.shape
    tm = _tile(T, (256, 128))

    def body(q_ref, cos_ref, sin_ref, o_ref):
        o_ref[:, :HN] = (q_ref[:, :HN] * _QSCALE).astype(BF16)
        for cb in range((NQ - HN) // 128):
            sl = slice(HN + cb * 128, HN + (cb + 1) * 128)
            xv = q_ref[:, sl]
            o_ref[:, sl] = ((xv * cos_ref[...] + _swap32(xv) * sin_ref[...]) * _QSCALE).astype(BF16)

    tab = pl.BlockSpec((tm, 128), lambda i: (i, 0))
    return pl.pallas_call(
        body, out_shape=SDS((T, NQ), BF16), grid=(T // tm,),
        in_specs=[pl.BlockSpec((tm, NQ), lambda i: (i, 0)), tab, tab],
        out_specs=pl.BlockSpec((tm, NQ), lambda i: (i, 0)),
        compiler_params=_params(("parallel",)), name=name)(q, cos, sin)


def _qrope_bwd(dqpe, dqa, cos, sin, HN, name):
    T, HW = dqpe.shape
    HR = HW // 2
    tm = _tile(T, (256, 128))

    def body(d_ref, dqa_hbm, cos_ref, sin_ref, o_ref):
        del dqa_hbm
        for pr in range(HR // 128):
            dr = d_ref[:, 2 * pr * 128:(2 * pr + 1) * 128] + d_ref[:, (2 * pr + 1) * 128:(2 * pr + 2) * 128]
            o_ref[:, pr * 128:(pr + 1) * 128] = (dr * cos_ref[...] - _swap32(dr) * sin_ref[...]).astype(BF16)

    tab = pl.BlockSpec((tm, 128), lambda i: (i, 0))
    return pl.pallas_call(
        body, out_shape=SDS(dqa.shape, BF16), grid=(T // tm,),
        in_specs=[pl.BlockSpec((tm, HW), lambda i: (i, 0)), pl.BlockSpec(memory_space=pl.ANY), tab, tab],
        out_specs=pl.BlockSpec((tm, HR), lambda i: (i, HN // HR)),
        input_output_aliases={1: 0}, compiler_params=_params(("parallel",)), name=name)(dqpe, dqa, cos, sin)


_ATT_SUB = 4
_ATT_SUB_B = 4
_NT = (((1,), (1,)), ((), ()))
_TN = (((0,), (0,)), ((), ()))


def _attn_fwd(qa, kv, kpe, T, H, name, carry=None):
    M = kv.shape[0]
    tq = _tile(T, (1024, 512, 256, 128))
    scale = (NOPE + ROPE) ** -0.5

    def body(qn_ref, qp_ref, kv_ref, kpe_ref, o_ref, lse_ref, kc_ref):
        @pl.when(pl.program_id(1) == 0)
        def _():
            kc_ref[:, :NOPE] = kv_ref[:, :NOPE]
            kc_ref[:, NOPE:] = kpe_ref[0]

        rs = tq // _ATT_SUB
        outs, lses = [], []
        for u in range(_ATT_SUB):
            rows = pl.ds(u * rs, rs)
            qc = jnp.concatenate([qn_ref[rows, :], qp_ref[rows, :]], axis=1)
            s = lax.dot_general(qc, kc_ref[...], _NT, preferred_element_type=F32)
            m = jnp.max(s, axis=-1, keepdims=True)
            p = jnp.exp2(s - m)
            l = jnp.sum(p, axis=-1, keepdims=True)
            o = jnp.dot(p.astype(BF16), kv_ref[:, NOPE:], preferred_element_type=F32)
            outs.append((o / l).astype(BF16))
            lses.append(jnp.broadcast_to(m + jnp.log2(l), (rs, 128)))
        o_ref[...] = jnp.concatenate(outs, axis=0)
        lse_ref[...] = jnp.concatenate(lses, axis=0)

    return _carry_call(
        body, carry, out_shape=[SDS((T, H * VDIM), BF16), SDS((T, H * 128), F32)], grid=(H, T // tq),
        in_specs=[pl.BlockSpec((tq, NOPE), lambda h, i: (i, h)), pl.BlockSpec((tq, 128), lambda h, i: (i, H + h // 2)),
                  pl.BlockSpec((M, NOPE + VDIM), lambda h, i: (0, h)), pl.BlockSpec((1, M, 128), lambda h, i: (h % 2, 0, 0))],
        out_specs=[pl.BlockSpec((tq, VDIM), lambda h, i: (i, h)), pl.BlockSpec((tq, 128), lambda h, i: (i, h))],
        scratch_shapes=[pltpu.VMEM((M, NOPE + 128), BF16)],
        sem=("parallel", "arbitrary"), name=name, ins=(qa, qa, kv, kpe))


def _attn_bwd(qa, kv, kpe, do, o, lse, T, H, name, carry=None):
    M = kv.shape[0]
    tq = _tile(T, (1024, 512, 256, 128))
    nq = T // tq
    scale = (NOPE + ROPE) ** -0.5

    def body(qn_ref, qp_ref, kv_ref, kpe_ref, do_ref, o_ref, lse_ref, dqa_ref, dqpe_ref, dkv_ref, dkpe_ref, kc_ref,
             dk_acc, dv_acc):
        i = pl.program_id(1)

        @pl.when(i == 0)
        def _():
            kc_ref[:, :NOPE] = kv_ref[:, :NOPE]
            kc_ref[:, NOPE:] = kpe_ref[0]
            dk_acc[...] = jnp.zeros_like(dk_acc)
            dv_acc[...] = jnp.zeros_like(dv_acc)

        rs = tq // _ATT_SUB_B
        p16s, ds16s = [], []
        for u in range(_ATT_SUB_B):
            rows = pl.ds(u * rs, rs)
            qc = jnp.concatenate([qn_ref[rows, :], qp_ref[rows, :]], axis=1)
            dov = do_ref[rows, :]
            s = lax.dot_general(qc, kc_ref[...], _NT, preferred_element_type=F32)
            p = jnp.exp2(s - lse_ref[rows, 0:1])
            dp = lax.dot_general(dov, kv_ref[:, NOPE:], _NT, preferred_element_type=F32)
            delta = jnp.sum(dov.astype(F32) * o_ref[rows, :].astype(F32), axis=-1, keepdims=True)
            ds16s.append((p * (dp - delta)).astype(BF16))
            p16s.append(p.astype(BF16))
        p16 = jnp.concatenate(p16s, axis=0)
        ds16 = jnp.concatenate(ds16s, axis=0)
        qc = jnp.concatenate([qn_ref[...], qp_ref[...]], axis=1)
        dq = jnp.dot(ds16, kc_ref[...], preferred_element_type=F32) * scale
        dqa_ref[...] = dq[:, :NOPE].astype(BF16)
        dqpe_ref[...] = dq[:, NOPE:]
        dv_acc[...] += lax.dot_general(p16, do_ref[...], _TN, preferred_element_type=F32)
        dk_acc[...] += lax.dot_general(ds16, qc, _TN, preferred_element_type=F32)

        @pl.when(i == nq - 1)
        def _():
            dkv_ref[:, :NOPE] = (dk_acc[:, :NOPE] * _LN2).astype(BF16)
            dkv_ref[:, NOPE:] = dv_acc[...].astype(BF16)
            dkpe_ref[0] = dk_acc[:, NOPE:] * _LN2

    return _carry_call(
        body, carry,
        out_shape=[SDS((T, H * (NOPE + ROPE)), BF16), SDS((T, H * 128), F32), SDS((M, H * (NOPE + VDIM)), BF16),
                   SDS((H, M, 128), F32)],
        grid=(H, nq),
        in_specs=[pl.BlockSpec((tq, NOPE), lambda h, i: (i, h)), pl.BlockSpec((tq, 128), lambda h, i: (i, H + h // 2)),
                  pl.BlockSpec((M, NOPE + VDIM), lambda h, i: (0, h)), pl.BlockSpec((1, M, 128), lambda h, i: (h % 2, 0, 0)),
                  pl.BlockSpec((tq, VDIM), lambda h, i: (i, h)), pl.BlockSpec((tq, VDIM), lambda h, i: (i, h)),
                  pl.BlockSpec((tq, 128), lambda h, i: (i, h))],
        out_specs=[pl.BlockSpec((tq, NOPE), lambda h, i: (i, h)), pl.BlockSpec((tq, 128), lambda h, i: (i, h)),
                   pl.BlockSpec((M, NOPE + VDIM), lambda h, i: (0, h)), pl.BlockSpec((1, M, 128), lambda h, i: (h, 0, 0))],
        scratch_shapes=[pltpu.VMEM((M, NOPE + 128), BF16), pltpu.VMEM((M, NOPE + 128), F32), pltpu.VMEM((M, VDIM), F32)],
        sem=("parallel", "arbitrary"), name=name, ins=(qa, qa, kv, kpe, do, o, lse))


def _adamw(w, m, v, name, g=None, recv=None, carry=None):
    R, C = w.shape
    summed = recv is not None
    n_recv = len(recv) if summed else 1
    runs = [r.shape[1] for r in recv] if summed else [R]
    tr = math.gcd(*runs)
    for cand in (1024, 512, 256, 128, 64, 32, 16, 8):
        if tr % cand == 0 and cand * C <= 131072:
            tr = cand
            break
    first = [sum(runs[:r]) // tr for r in range(n_recv + 1)]
    c1 = 1.0 - ADAM_B1 ** ADAM_STEP
    c2 = 1.0 - ADAM_B2 ** ADAM_STEP

    def update(gv, w_ref, m_ref, v_ref, d_ref, nm_ref, nv_ref):
        mn = ADAM_B1 * m_ref[...] + (1.0 - ADAM_B1) * gv
        vn = ADAM_B2 * v_ref[...] + (1.0 - ADAM_B2) * (gv * gv)
        nm_ref[...] = mn
        nv_ref[...] = vn
        d_ref[...] = -ADAM_LR * ((mn / c1) / (jnp.sqrt(vn / c2) + ADAM_EPS) + ADAM_WD * w_ref[...])

    def body(*refs):
        w_ref, m_ref, v_ref = refs[:3]
        g_refs = refs[3:3 + n_recv]
        outs = refs[3 + n_recv:]
        if not summed:
            update(g_refs[0][...], w_ref, m_ref, v_ref, *outs)
            return
        i = pl.program_id(0)
        for r in range(n_recv):
            @pl.when((i >= first[r]) & (i < first[r + 1]))
            def _():
                gv = g_refs[r][0].astype(F32)
                for d in range(1, N_DEV):
                    gv = gv + g_refs[r][d].astype(F32)
                outs[0][...] = gv
                update(gv, w_ref, m_ref, v_ref, *outs[1:])

    blk = pl.BlockSpec((tr, C), lambda i: (i, 0))
    if summed:
        g_specs = [pl.BlockSpec((N_DEV, tr, C), functools.partial(
            lambda i, lo, n: (0, jnp.clip(i - lo, 0, n - 1), 0), lo=first[r], n=first[r + 1] - first[r]))
                   for r in range(n_recv)]
    else:
        g_specs = [blk]
    n_out = 4 if summed else 3
    out, carried = _carry_call(
        body, carry, out_shape=[SDS((R, C), F32)] * n_out, grid=(R // tr,), in_specs=[blk, blk, blk] + g_specs,
        out_specs=[blk] * n_out, scratch_shapes=[], sem=("parallel",), name=name,
        ins=(w, m, v, *(recv if summed else [g])))
    return (out, carried) if carry else out


WEIGHTS = ['c_ctx', 'norm1_g', 'norm2_g', 'w_ada', 'b_ada', 'ab_w_in', 'ab_b_in', 'a_ln_g', 'a_ln_b', 'a_w_s', 'a_b_s',
           'b_conv_w', 'b_conv_b', 'b_ln_g', 'b_ln_b', 'ab_w_out', 'mla_w_in', 'mla_q_norm_g', 'mla_w_uq',
           'mla_kv_norm_g', 'mla_w_ukv', 'mla_w_o', 'ffn_w_up', 'ffn_conv_w', 'ffn_conv_b', 'ffn_w_down', 'final_norm_g']


def _pack(parts):
    flat = jnp.concatenate([p.reshape(-1).astype(F32) for p in parts])
    n = flat.shape[0]
    unit = 65536 if n > 65536 else 1024
    n_pad = -(-n // unit) * unit
    return jnp.pad(flat, (0, n_pad - n)).reshape(n_pad // 128, 128)


def _unpack(flat, like):
    out, off = [], 0
    for shp in like:
        n = math.prod(shp)
        out.append(flat[..., off:off + n].reshape(flat.shape[:-1] + tuple(shp)))
        off += n
    return out


def _rope_tables(T, Tc):
    rows = T // GRID_W
    row = jnp.repeat(jnp.arange(rows, dtype=F32), GRID_W)
    col = jnp.tile(jnp.arange(GRID_W, dtype=F32), rows)
    n_freq = ROPE // 4
    inv = ROPE_THETA ** (-jnp.arange(n_freq, dtype=F32) / n_freq)
    ang = jnp.concatenate([row[:, None] * inv, col[:, None] * inv], axis=-1)
    cos, sin = jnp.cos(ang), jnp.sin(ang)
    cos = jnp.tile(cos, (1, 128 // (ROPE // 2)))
    sin = jnp.tile(jnp.concatenate([-sin, sin], axis=1), (1, 128 // ROPE))
    return (jnp.concatenate([cos, jnp.ones((Tc, 128), F32)], axis=0),
            jnp.concatenate([sin, jnp.zeros((Tc, 128), F32)], axis=0))


def _step(a):
    ax, ay, ac = lax.axis_index("x"), lax.axis_index("y"), lax.axis_index("c")
    me = 4 * ax + 2 * ay + ac
    T, D = a['x'].shape[1:]
    Tc = a['ctx'].shape[1]
    M = T + Tc
    W, Wb = a['a_ln_g'].shape[1], a['b_ln_g'].shape[1]
    assert W == Wb and T % Tc == 0
    Fd = a['ffn_conv_b'].shape[1]
    QL, KL = a['mla_q_norm_g'].shape[1] * N_DEV, a['mla_kv_norm_g'].shape[1] * N_DEV
    H = a['mla_w_ukv'].shape[2] * N_DEV // (NOPE + VDIM)
    HN, HR = H * NOPE, H * ROPE
    kw = a['b_conv_w'].shape[1]
    NA = a['w_ada'].shape[2]
    bf = lambda t: t.astype(BF16)

    small_shapes = [(D,), (kw, Wb // N_DEV), (2, 3, Fd // N_DEV), (QL // N_DEV,), (KL // N_DEV,)]
    g_small = _all_gather(_pack([a['c'][0], a['b_conv_w'][0], a['ffn_conv_w'], a['mla_q_norm_g'][0], a['mla_kv_norm_g'][0]]),
                          "ag_small")
    c_all, bcw, fcw, gq, gkv = _unpack(g_small.reshape(N_DEV, -1), small_shapes)
    bcw = jnp.transpose(bcw, (1, 0, 2)).reshape(kw, Wb)
    fcw = jnp.transpose(fcw, (1, 2, 0, 3)).reshape(2, 3, Fd)
    gq, gkv = gq.reshape(1, QL), gkv.reshape(1, KL)

    a16 = jnp.concatenate([c_all, a['c_ctx'][None], jnp.zeros((N_DEV - 1, D), F32)], axis=0)
    b_loc = lax.dynamic_slice(a['b_ada'], (0, me * NA), (2, NA))
    mods = [_mm(a16, a['w_ada'][l], mode="nn", out_dtype=F32, name=f"ada_fwd{l}", bias=b_loc[l:l + 1], a_silu=True)
            for l in range(2)]
    gm = _all_gather(jnp.concatenate(mods, axis=0), "ag_mod").reshape(N_DEV, 2, 2 * N_DEV, NA)
    gm = jnp.transpose(gm, (1, 2, 0, 3)).reshape(2, 2 * N_DEV, 6 * D)
    mod_lat = [lax.dynamic_slice(gm[l], (me, 0), (1, 6 * D)).reshape(6, 1, 1, D) for l in range(2)]
    mod_ctx = [gm[l][N_DEV].reshape(6, 1, 1, D) for l in range(2)]

    def mod(l, k, both):
        return jnp.concatenate([mod_lat[l][k], mod_ctx[l][k]], axis=0) if both else mod_lat[l][k]

    def from_cols(g):
        return jnp.transpose(g, (1, 0, 2)).reshape(g.shape[1], -1)

    def from_rows(g):
        return g.reshape(-1, g.shape[2])

    def ag(x):
        return (x, False)

    def a2a(x):
        return (x, True)

    cos, sin = _rope_tables(T, Tc)
    n1g, n2g = a['norm1_g'], a['norm2_g']
    a_bst = a['a_b_s'][0].T
    mm = functools.partial(_mm)
    up_sh, dn_sh = bf(a['ffn_w_up']), bf(a['ffn_w_down'])

    w_abin = _all_gather(bf(a['ab_w_in'][0]), "ag_ab_w_in")
    x0 = jnp.concatenate([a['x'][0], a['ctx'][0]], axis=0)
    h1 = _normmod_fwd(x0, n1g[0:1], mod(0, 0, True), mod(0, 1, True), T, "l0_norm1")
    z, (g_about,) = mm(h1, w_abin, mode="nn", out_dtype=F32, name="l0_ab_in", bias=a['ab_b_in'], b_dev=True,
                       carry=[ag(bf(a['ab_w_out'][0]))])
    w_about = from_rows(g_about)
    y = _gmlp_fwd(z, a['a_ln_g'], a['a_ln_b'], a['a_w_s'][0], a_bst, W, "l0_gmlp")
    (y, hc_b), (g_up0a,) = _conf_fwd(z, y, bcw, a['b_conv_b'], a['b_ln_g'], a['b_ln_b'], W, Wb, T, "l0_conf",
                                     carry=[ag(up_sh[0][:D // 2])])
    (x1, o1), (g_up0b,) = mm(y, w_about, mode="nn", out_dtype=F32, name="l0_ab_out", res=x0, gate=mod(0, 2, True),
                             seg_t=T, carry=[ag(up_sh[0][D // 2:])])
    w_up = [jnp.concatenate([g_up0a, g_up0b], axis=1), None]
    h2 = _normmod_fwd(x1, n2g[0:1], mod(0, 3, True), mod(0, 4, True), T, "l0_norm2")
    z2, (g_dn0,) = mm(h2, w_up[0], mode="nn", out_dtype=BF16, name="l0_up", b_dev=True, carry=[ag(dn_sh[0])])
    w_dn = [from_rows(g_dn0), None]
    a2, gc2 = _ffn_act_fwd(z2, fcw[0], a['ffn_conv_b'][0:1], T, "l0_act")
    (x2, o2), (g_in, g_uq) = mm(a2, w_dn[0], mode="nn", out_dtype=F32, name="l0_down", res=x1, gate=mod(0, 5, True),
                                seg_t=T, carry=[ag(bf(a['mla_w_in'][0])), ag(bf(a['mla_w_uq'][0]))])
    w_in = from_rows(g_in)
    w_in = jnp.concatenate([w_in, w_in[:, QL + KL:]], axis=1)
    w_uq = from_cols(g_uq).reshape(QL, H, NOPE + ROPE)
    w_uq = jnp.concatenate([w_uq[:, :, :NOPE].reshape(QL, HN), w_uq[:, :, NOPE:].reshape(QL, HR)], axis=1)

    h3 = _normmod_fwd(x2, n1g[1:2], mod(1, 0, True), mod(1, 1, True), T, "l1_norm1")
    z3, (g_ukv,) = mm(h3, w_in, mode="nn", out_dtype=F32, name="l1_mla_in", carry=[ag(bf(a['mla_w_ukv'][0]))])
    w_ukv = g_ukv
    cqn, ckvn, kpe = _mla_prep_fwd(z3, gq, gkv, cos, sin, QL, KL, "l1_prep")
    q, (g_wo,) = mm(cqn, w_uq, mode="nn", out_dtype=F32, name="l1_uq", rows=T, carry=[ag(bf(a['mla_w_o'][0]))])
    w_o = from_rows(g_wo)
    kv = mm(ckvn, w_ukv, mode="nn", out_dtype=BF16, name="l1_ukv", b_dev=True)
    qa = _qrope_fwd(q, cos, sin, HN, "l1_qrope")
    (o_att, lse), (g_up1,) = _attn_fwd(qa, kv, kpe, T, H, "l1_attn", carry=[ag(up_sh[1])])
    w_up[1] = g_up1
    x3, o3 = mm(o_att, w_o, mode="nn", out_dtype=F32, name="l1_wo", res=x2, gate=mod(1, 2, False), seg_t=T)
    h4 = _normmod_fwd(x3, n2g[1:2], mod(1, 3, False), mod(1, 4, False), T, "l1_norm2")
    z4, (g_dn1,) = mm(h4, w_up[1], mode="nn", out_dtype=BF16, name="l1_up", b_dev=True, carry=[ag(dn_sh[1])])
    w_dn[1] = from_rows(g_dn1)
    a4, gc4 = _ffn_act_fwd(z4, fcw[1], a['ffn_conv_b'][1:2], T, "l1_act")
    x4, o4 = mm(a4, w_dn[1], mode="nn", out_dtype=F32, name="l1_down", res=x3, gate=mod(1, 5, False), seg_t=T)

    dx4, loss_cols, d_fng, do4, dg2_1 = _final(x4, a['final_norm_g'][None], a['loss_target'][0], o4, mod(1, 5, False),
                                               "final")
    loss = lax.psum(jnp.sum(loss_cols), ("x", "y", "c"))

    def cols(dw):
        k, n = dw.shape
        return jnp.transpose(dw.reshape(k, N_DEV, n // N_DEV), (1, 0, 2))

    def rows(dw):
        return dw.reshape(N_DEV, dw.shape[0] // N_DEV, dw.shape[1])

    da4 = mm(do4, w_dn[1], mode="nt", out_dtype=BF16, name="l1_down_dx")
    dw_dn1 = mm(a4, do4, mode="tn", out_dtype=BF16, name="l1_down_dw")
    dz4, dfcw1, dfcb1 = _ffn_act_bwd(z4, gc4, da4, fcw[1], T, "l1_act_bwd")
    dw_up1, (r_dn1,) = mm(h4, dz4, mode="tn", out_dtype=BF16, name="l1_up_dw", out_dev=True,
                          carry=[a2a(rows(dw_dn1))])
    dh4 = mm(dz4, w_up[1], mode="nt", out_dtype=F32, name="l1_up_dx", b_dev=True)
    dx3, dn2g1, dsh2_1, dsc2_1, do3, dg1_1 = _normmod_bwd(x3, n2g[1:2], mod(1, 4, False), dh4, dx4, T, "l1_norm2_bwd",
                                                         o_prev=o3, gate_prev=mod(1, 2, False))
    d_oatt = mm(do3, w_o, mode="nt", out_dtype=BF16, name="l1_wo_dx")
    dw_o = mm(o_att, do3, mode="tn", out_dtype=BF16, name="l1_wo_dw")
    (dqa, dqpe, dkv, dkpe), (r_up1, r_wo) = _attn_bwd(qa, kv, kpe, d_oatt, o_att, lse, T, H, "l1_attn_bwd",
                                                      carry=[a2a(dw_up1), a2a(rows(dw_o))])
    dqa = _qrope_bwd(dqpe, dqa, cos, sin, HN, "l1_qrope_bwd")
    dcq = mm(dqa, w_uq, mode="nt", out_dtype=F32, name="l1_uq_dx")
    dw_uq = mm(cqn, dqa, mode="tn", out_dtype=BF16, name="l1_uq_dw", rows=T)
    dw_uq = jnp.concatenate([dw_uq[:, :HN].reshape(QL, H, NOPE), dw_uq[:, HN:].reshape(QL, H, ROPE)], axis=2)
    dw_uq = dw_uq.reshape(QL, H * (NOPE + ROPE))
    dckv = mm(dkv, w_ukv, mode="nt", out_dtype=F32, name="l1_ukv_dx", b_dev=True)
    dw_ukv = mm(ckvn, dkv, mode="tn", out_dtype=BF16, name="l1_ukv_dw", out_dev=True)
    dz3, dgq, dgkv = _mla_prep_bwd(z3, dcq, dckv, dkpe, gq, gkv, cos, sin, QL, KL, T, "l1_prep_bwd")
    dh3 = mm(dz3, w_in, mode="nt", out_dtype=F32, name="l1_mla_in_dx")
    dw_in = mm(h3, dz3, mode="tn", out_dtype=BF16, name="l1_mla_in_dw").astype(F32)
    dw_in = jnp.concatenate([dw_in[:, :QL + KL], dw_in[:, QL + KL:QL + KL + ROPE] + dw_in[:, QL + KL + ROPE:QL + KL + 2 * ROPE]],
                            axis=1).astype(BF16)
    dx2, dn1g1, dsh1_1, dsc1_1, do2, dg2_0 = _normmod_bwd(x2, n1g[1:2], mod(1, 1, True), dh3, dx3, T, "l1_norm1_bwd",
                                                         o_prev=o2, gate_prev=mod(0, 5, True))
    da2, (r_uq, r_ukv) = mm(do2, w_dn[0], mode="nt", out_dtype=BF16, name="l0_down_dx",
                            carry=[a2a(cols(dw_uq)), a2a(dw_ukv)])
    dw_dn0, (r_in,) = mm(a2, do2, mode="tn", out_dtype=BF16, name="l0_down_dw", carry=[a2a(rows(dw_in))])
    dz2, dfcw0, dfcb0 = _ffn_act_bwd(z2, gc2, da2, fcw[0], T, "l0_act_bwd")
    dw_up0, (r_dn0,) = mm(h2, dz2, mode="tn", out_dtype=BF16, name="l0_up_dw", out_dev=True,
                          carry=[a2a(rows(dw_dn0))])
    dh2, (r_up0a,) = mm(dz2, w_up[0], mode="nt", out_dtype=F32, name="l0_up_dx", b_dev=True,
                        carry=[a2a(dw_up0[:, :D // 2])])
    dx1, dn2g0, dsh2_0, dsc2_0, do1, dg1_0 = _normmod_bwd(x1, n2g[0:1], mod(0, 4, True), dh2, dx2, T, "l0_norm2_bwd",
                                                         o_prev=o1, gate_prev=mod(0, 2, True))
    dy = mm(do1, w_about, mode="nt", out_dtype=F32, name="l0_ab_out_dx")
    dw_about = mm(y, do1, mode="tn", out_dtype=BF16, name="l0_ab_out_dw")
    dz, dlag, dlab, dws, dbs, dbin_a = _gmlp_bwd(z, dy, a['a_ln_g'], a['a_ln_b'], a['a_w_s'][0], a_bst, W, "l0_gmlp_bwd")
    dhc, dlbg, dlbb, dbcb = _conf_bwd1(hc_b, dy, a['b_ln_g'], a['b_ln_b'], W, Wb, T, "l0_conf_bwd1")
    (dz, dbcw, dbin_b), (r_up0b,) = _conf_bwd2(z, dhc, dz, bcw, W, Wb, T, "l0_conf_bwd2",
                                               carry=[a2a(dw_up0[:, D // 2:])])
    dh1, (r_about,) = mm(dz, w_abin, mode="nt", out_dtype=F32, name="l0_ab_in_dx", b_dev=True,
                         carry=[a2a(rows(dw_about))])
    dw_abin_a = mm(h1, dz, mode="tn", out_dtype=BF16, name="l0_ab_in_dw_a", out_dev=True, p_range=(0, D // 2))
    dw_abin_b, (r_abin_a,) = mm(h1, dz, mode="tn", out_dtype=BF16, name="l0_ab_in_dw_b", out_dev=True,
                                p_range=(D // 2, D // 2), carry=[a2a(dw_abin_a)])
    dx0, dn1g0, dsh1_0, dsc1_0 = _normmod_bwd(x0, n1g[0:1], mod(0, 1, True), dh1, dx1, T, "l0_norm1_bwd")

    zero = jnp.zeros((D,), F32)
    dmod = jnp.stack([
        jnp.stack([jnp.stack([dsh1_0[0, 0], dsc1_0[0, 0], dg1_0[0, 0], dsh2_0[0, 0], dsc2_0[0, 0], dg2_0[0, 0]]),
                   jnp.stack([dsh1_0[1, 0], dsc1_0[1, 0], dg1_0[1, 0], dsh2_0[1, 0], dsc2_0[1, 0], dg2_0[1, 0]])]),
        jnp.stack([jnp.stack([dsh1_1[0, 0], dsc1_1[0, 0], dg1_1[0, 0], dsh2_1[0, 0], dsc2_1[0, 0], dg2_1[0, 0]]),
                   jnp.stack([dsh1_1[1, 0], dsc1_1[1, 0], zero, zero, zero, zero])])])
    small = {
        'norm1_g': jnp.concatenate([dn1g0, dn1g1], axis=0), 'norm2_g': jnp.concatenate([dn2g0, dn2g1], axis=0),
        'ab_b_in': jnp.concatenate([dbin_a, dbin_b], axis=1), 'a_ln_g': dlag, 'a_ln_b': dlab, 'a_w_s': dws[None],
        'a_b_s': jnp.sum(dbs, axis=-1)[None], 'b_conv_w': dbcw, 'b_conv_b': dbcb, 'b_ln_g': dlbg, 'b_ln_b': dlbb,
        'mla_q_norm_g': dgq, 'mla_kv_norm_g': dgkv, 'ffn_conv_w': jnp.stack([dfcw0, dfcw1]),
        'ffn_conv_b': jnp.concatenate([dfcb0, dfcb1], axis=0), 'final_norm_g': d_fng[0],
    }
    names = list(small)
    g2 = _all_gather(_pack([dmod] + [small[n] for n in names]), "ag_small_grads")
    red = _sum_lead(g2, "sum_small_grads").reshape(-1)
    red = dict(zip(names, _unpack(red, [(2, 2, 6, D)] + [small[n].shape for n in names])[1:]))
    dmod_all = g2.reshape(N_DEV, -1)[:, :2 * 2 * 6 * D].reshape(N_DEV, 2, 2, 6 * D)

    a16g = jnp.concatenate([c_all, jnp.tile(a['c_ctx'][None], (N_DEV, 1))], axis=0)
    dm_loc = lax.dynamic_slice(dmod_all, (0, 0, 0, me * NA), (N_DEV, 2, 2, NA))
    g_wada, cpart = [], []
    for l in range(2):
        dm16 = jnp.concatenate([dm_loc[:, l, 0], dm_loc[:, l, 1]], axis=0)
        g_wada.append(mm(a16g, dm16, mode="tn", out_dtype=F32, name=f"ada_dw{l}", a_silu=True))
        cpart.append(mm(dm_loc[:, l, 1], a['w_ada'][l], mode="nt", out_dtype=F32, name=f"ada_dc{l}"))
    g_bada = _sum_lead(jnp.transpose(dmod_all, (0, 2, 1, 3)).reshape(2 * N_DEV, 2 * 6 * D // 128, 128), "sum_b_ada")
    g_cc = _all_gather(jnp.concatenate(cpart, axis=0), "ag_c_ctx")
    g_cc = _sum_lead(g_cc.reshape(2 * N_DEV * N_DEV, D // 128, 128), "sum_c_ctx").reshape(D)
    grads = {
        'c_ctx': g_cc * _dsilu(a['c_ctx']), 'w_ada': jnp.stack(g_wada), 'b_ada': g_bada.reshape(2, 6 * D),
        'b_conv_w': lax.dynamic_slice(red['b_conv_w'], (0, me * (Wb // N_DEV)), (kw, Wb // N_DEV))[None],
        'ffn_conv_w': lax.dynamic_slice(red['ffn_conv_w'], (0, 0, me * (Fd // N_DEV)), (2, 3, Fd // N_DEV)),
        'mla_q_norm_g': lax.dynamic_slice(red['mla_q_norm_g'], (0, me * (QL // N_DEV)), (1, QL // N_DEV)),
        'mla_kv_norm_g': lax.dynamic_slice(red['mla_kv_norm_g'], (0, me * (KL // N_DEV)), (1, KL // N_DEV)),
    }
    for n in names:
        if n not in grads:
            grads[n] = red[n].reshape(a[n].shape)

    recvs = {'ab_w_out': [r_about], 'mla_w_in': [r_in], 'mla_w_uq': [r_uq], 'mla_w_ukv': [r_ukv],
             'mla_w_o': [r_wo], 'ffn_w_up': [r_up0a, r_up0b, r_up1], 'ffn_w_down': [r_dn0, r_dn1]}
    out = {}
    for n in WEIGHTS:
        shp = a[n].shape
        w2 = a[n].reshape(-1, shp[-1])
        m2, v2 = a['m_' + n].reshape(w2.shape), a['v_' + n].reshape(w2.shape)
        if n in recvs:
            res = _adamw(w2, m2, v2, "adamw_" + n, recv=recvs[n])
        elif n == 'w_ada':
            g2d = grads[n].reshape(w2.shape)
            res, (r_abin_b,) = _adamw(w2, m2, v2, "adamw_" + n, g=g2d, carry=[a2a(dw_abin_b)])
            res = (g2d,) + tuple(res)
            recvs['ab_w_in'] = [r_abin_a, r_abin_b]
        else:
            g2d = grads[n].reshape(w2.shape)
            res = (g2d,) + tuple(_adamw(w2, m2, v2, "adamw_" + n, g=g2d))
        out[n] = [r.reshape(shp) for r in res]
    return (loss, dx0[:T][None], *[out[n][0] for n in WEIGHTS], *[out[n][1] for n in WEIGHTS],
            *[out[n][2] for n in WEIGHTS], *[out[n][3] for n in WEIGHTS])


def kernel(x, c, ctx, c_ctx, norm1_g, norm2_g, w_ada, b_ada, ab_w_in, ab_b_in, a_ln_g, a_ln_b, a_w_s, a_b_s, b_conv_w, b_conv_b, b_ln_g, b_ln_b, ab_w_out, mla_w_in, mla_q_norm_g, mla_w_uq, mla_kv_norm_g, mla_w_ukv, mla_w_o, ffn_w_up, ffn_conv_w, ffn_conv_b, ffn_w_down, final_norm_g, loss_target, m_c_ctx, m_norm1_g, m_norm2_g, m_w_ada, m_b_ada, m_ab_w_in, m_ab_b_in, m_a_ln_g, m_a_ln_b, m_a_w_s, m_a_b_s, m_b_conv_w, m_b_conv_b, m_b_ln_g, m_b_ln_b, m_ab_w_out, m_mla_w_in, m_mla_q_norm_g, m_mla_w_uq, m_mla_kv_norm_g, m_mla_w_ukv, m_mla_w_o, m_ffn_w_up, m_ffn_conv_w, m_ffn_conv_b, m_ffn_w_down, m_final_norm_g, v_c_ctx, v_norm1_g, v_norm2_g, v_w_ada, v_b_ada, v_ab_w_in, v_ab_b_in, v_a_ln_g, v_a_ln_b, v_a_w_s, v_a_b_s, v_b_conv_w, v_b_conv_b, v_b_ln_g, v_b_ln_b, v_ab_w_out, v_mla_w_in, v_mla_q_norm_g, v_mla_w_uq, v_mla_kv_norm_g, v_mla_w_ukv, v_mla_w_o, v_ffn_w_up, v_ffn_conv_w, v_ffn_conv_b, v_ffn_w_down, v_final_norm_g):
    return _step(dict(locals()))
```

```python
import functools
import math

import jax
import jax.numpy as jnp
from jax import lax
from jax.experimental import pallas as pl
from jax.experimental.pallas import tpu as pltpu

F32 = jnp.float32
BF16 = jnp.bfloat16
SDS = jax.ShapeDtypeStruct

N_DEV = 8
EPS = 1e-6
CHUNK = 128
NOPE = 128
ROPE = 64
VDIM = 128
GRID_W = 64
ROPE_THETA = 10000.0
HALO = 16
ADAM_LR, ADAM_B1, ADAM_B2, ADAM_EPS, ADAM_WD, ADAM_STEP = 0.001, 0.9, 0.999, 1e-08, 0.01, 10
VMEM_LIMIT = 56 * 1024 * 1024


def _tile(n, prefs):
    for p in prefs:
        if n % p == 0:
            return p
    return n


def _params(sem, vmem=VMEM_LIMIT):
    return pltpu.CompilerParams(dimension_semantics=sem, vmem_limit_bytes=vmem)


def _sigmoid(x):
    return 0.5 * jnp.tanh(0.5 * x) + 0.5


def _silu(x):
    return x * _sigmoid(x)


def _dsilu(x):
    s = _sigmoid(x)
    return s * (1.0 + x * (1.0 - s))


_GELU_C = math.sqrt(2.0 / math.pi)


def _gelu(x):
    return 0.5 * x * (1.0 + jnp.tanh(_GELU_C * (x + 0.044715 * x * x * x)))


def _dgelu(x):
    t = jnp.tanh(_GELU_C * (x + 0.044715 * x * x * x))
    return 0.5 * (1.0 + t) + 0.5 * x * (1.0 - t * t) * _GELU_C * (1.0 + 3.0 * 0.044715 * x * x)


def _colsum(v):
    return jnp.sum(v, axis=0, keepdims=True)


_SIBLING = 1
_CHIPS = (2, 4, 6)


def _xchg(x_ref, o_ref, send_sems, recv_sems, local_sem, scatter):
    ax, ay, ac = lax.axis_index("x"), lax.axis_index("y"), lax.axis_index("c")
    me = 4 * ax + 2 * ay + ac

    def dev(k):
        return ax ^ (k >> 2), ay ^ ((k >> 1) & 1), ac ^ (k & 1)

    def idx(k):
        px, py, pc = dev(k)
        return 4 * px + 2 * py + pc

    def copy(k, src, dst, to):
        return pltpu.make_async_remote_copy(src_ref=src, dst_ref=dst, send_sem=send_sems.at[k - 1],
                                            recv_sem=recv_sems.at[k - 1], device_id=dev(to),
                                            device_id_type=pl.DeviceIdType.MESH)

    def own():
        return pltpu.make_async_copy(x_ref.at[me] if scatter else x_ref, o_ref.at[me], local_sem)

    def sends():
        if scatter:
            return [copy(k, x_ref.at[idx(k)], o_ref.at[me], k) for k in range(1, N_DEV)]
        return [copy(k, x_ref, o_ref.at[me], k) for k in (_SIBLING,) + _CHIPS]

    def forwards():
        return [] if scatter else [copy(j + 1, o_ref.at[idx(j)], o_ref.at[idx(j)], _SIBLING) for j in _CHIPS]

    def arrival(k):
        return copy(k, o_ref.at[idx(k)], o_ref.at[idx(k)], k)

    return own, sends, forwards, arrival


def _xchg_start(*refs, scatter):
    own, sends, _, _ = _xchg(*refs, scatter)
    own().start()
    for cp in sends():
        cp.start()


def _xchg_forward(*refs, scatter):
    _, _, forwards, arrival = _xchg(*refs, scatter)
    if not scatter:
        for j, fw in zip(_CHIPS, forwards()):
            arrival(j).wait_recv()
            fw.start()


def _xchg_finish(*refs, scatter):
    own, sends, forwards, arrival = _xchg(*refs, scatter)
    for k in range(1, N_DEV):
        if scatter or k not in _CHIPS:
            arrival(k).wait_recv()
    for cp in sends() + forwards():
        cp.wait_send()
    own().wait()


_XCHG_SEMS = [pltpu.SemaphoreType.DMA((N_DEV - 1,)), pltpu.SemaphoreType.DMA((N_DEV - 1,)), pltpu.SemaphoreType.DMA]


def _xchg_shape(x, scatter):
    return SDS((N_DEV,) + tuple(x.shape[1:] if scatter else x.shape), x.dtype)


def _exchange(x, *, scatter, name):
    def body(*refs):
        _xchg_start(*refs, scatter=scatter)
        _xchg_forward(*refs, scatter=scatter)
        _xchg_finish(*refs, scatter=scatter)

    return pl.pallas_call(
        body, out_shape=_xchg_shape(x, scatter),
        in_specs=[pl.BlockSpec(memory_space=pl.ANY)], out_specs=pl.BlockSpec(memory_space=pl.ANY),
        scratch_shapes=list(_XCHG_SEMS), name=name)(x)


def _carried(body, carry, n_in, n_out, n_scratch, grid):
    nc = len(carry)
    total = math.prod(grid)
    mid = (3 * total) // 4

    def wrapped(*refs):
        ins, cin = refs[:n_in], refs[n_in:n_in + nc]
        o0 = n_in + nc
        outs, cout = refs[o0:o0 + n_out], refs[o0 + n_out:o0 + n_out + nc]
        scr = refs[o0 + n_out + nc:]
        sems = scr[n_scratch:]
        step = pl.program_id(0)
        for ax in range(1, len(grid)):
            step = step * grid[ax] + pl.program_id(ax)

        def each(fn):
            for c in range(nc):
                fn(cin[c], cout[c], *sems[3 * c:3 * c + 3], scatter=carry[c][1])

        @pl.when(step == 0)
        def _():
            each(_xchg_start)

        body(*ins, *outs, *scr[:n_scratch])

        if mid < total - 1:
            @pl.when(step == mid)
            def _():
                each(_xchg_forward)

        @pl.when(step == total - 1)
        def _():
            if mid >= total - 1:
                each(_xchg_forward)
            each(_xchg_finish)

    return wrapped


def _carry_call(body, carry, *, grid, out_shape, in_specs, out_specs, scratch_shapes, sem, name, ins, aliases=None):
    carry = carry or []
    nc = len(carry)
    if nc:
        body = _carried(body, carry, len(in_specs), len(out_shape), len(scratch_shapes), grid)
        anyspec = pl.BlockSpec(memory_space=pl.ANY)
        in_specs = list(in_specs) + [anyspec] * nc
        out_specs = list(out_specs) + [anyspec] * nc
        out_shape = list(out_shape) + [_xchg_shape(x, sc) for x, sc in carry]
        scratch_shapes = list(scratch_shapes) + list(_XCHG_SEMS) * nc
        ins = list(ins) + [x for x, _ in carry]
        sem = ("arbitrary",) * len(grid)
    out = pl.pallas_call(body, out_shape=out_shape, grid=grid, in_specs=in_specs, out_specs=out_specs,
                         scratch_shapes=scratch_shapes, compiler_params=_params(sem), name=name,
                         input_output_aliases=aliases or {})(*ins)
    n_main = len(out) - nc
    return list(out[:n_main]), list(out[n_main:])


def _all_gather(x, name):
    return _exchange(x, scatter=False, name=name)


def _all_to_all(x, name):
    return _exchange(x, scatter=True, name=name)


def _sum_lead(x, name):
    n, R, C = x.shape
    tr = _tile(R, (512, 256, 128, 64, 32, 16, 8))

    def body(x_ref, o_ref):
        acc = x_ref[0]
        for d in range(1, n):
            acc = acc + x_ref[d]
        o_ref[...] = acc

    return pl.pallas_call(
        body, out_shape=SDS((R, C), F32), grid=(R // tr,),
        in_specs=[pl.BlockSpec((n, tr, C), lambda i: (0, i, 0))], out_specs=pl.BlockSpec((tr, C), lambda i: (i, 0)),
        compiler_params=_params(("parallel",)), name=name)(x)


_TP = (1408, 1088, 1024, 768, 512, 256, 128)
_TQ = (1408, 1024, 768, 512, 256, 128)
_TR = (2048, 1408, 1024, 768, 512, 256, 128)
_TR_TN = (1088, 1024, 512, 256, 128)


def _mm(a, b, *, mode, out_dtype, name, rows=None, bias=None, res=None, gate=None, seg_t=None, a_silu=False, carry=None,
        b_dev=False, out_dev=False, p_range=None, halves=False):
    if mode == "nn":
        P, R, Q = rows or a.shape[0], a.shape[1], (b.shape[0] * b.shape[2] if b_dev else b.shape[1])
    elif mode == "nt":
        P, R, Q = rows or a.shape[-2], (2 * a.shape[2] if halves else a.shape[1]), (b.shape[1] if b_dev else b.shape[0])
    else:
        R, P, Q = rows or a.shape[0], a.shape[1], (2 * b.shape[2] if halves else b.shape[1])
    p0 = 0
    if p_range is not None:
        p0, P = p_range
    tp = _tile(P, _TP)
    tq = _tile(Q // N_DEV if (out_dev or (b_dev and mode == "nn")) else Q, _TQ)
    tr = _tile(R // N_DEV if (b_dev and mode == "nt") else R, _TR if mode != "tn" else _TR_TN)
    nk = R // tr
    qd = (Q // N_DEV) // tq
    rd = (R // N_DEV) // tr
    if mode == "nn":
        a_spec = pl.BlockSpec((tp, tr), lambda i, j, k: (i, k))
        b_spec = (pl.BlockSpec((None, tr, tq), lambda i, j, k: (j // qd, k, j % qd)) if b_dev
                  else pl.BlockSpec((tr, tq), lambda i, j, k: (k, j)))
        dims = (((1,), (0,)), ((), ()))
    elif mode == "nt":
        kh = (R // 2) // tr
        a_spec = (pl.BlockSpec((None, tp, tr), lambda i, j, k: (k // kh, i, k % kh)) if halves
                  else pl.BlockSpec((tp, tr), lambda i, j, k: (i, k)))
        b_spec = (pl.BlockSpec((None, tq, tr), lambda i, j, k: (k // rd, j, k % rd)) if b_dev
                  else pl.BlockSpec((tq, tr), lambda i, j, k: (j, k)))
        dims = (((1,), (1,)), ((), ()))
    else:
        pb = p0 // tp
        qh = (Q // 2) // tq
        a_spec = pl.BlockSpec((tr, tp), lambda i, j, k: (k, i + pb))
        b_spec = (pl.BlockSpec((None, tr, tq), lambda i, j, k: (j // qh, k, j % qh)) if halves
                  else pl.BlockSpec((tr, tq), lambda i, j, k: (k, j)))
        dims = (((0,), (0,)), ((), ()))
    ins, in_specs = [a, b], [a_spec, b_spec]
    if bias is not None:
        ins.append(bias)
        in_specs.append(pl.BlockSpec((1, tq), lambda i, j, k: (0, j)))
    gated = res is not None
    if gated:
        n_seg = gate.shape[0]
        ins += [res, gate]
        in_specs += [pl.BlockSpec((tp, tq), lambda i, j, k: (i, j)),
                     pl.BlockSpec((n_seg, 1, tq), lambda i, j, k: (0, 0, j))]
    if out_dev:
        out_shape = [SDS((N_DEV, P, Q // N_DEV), out_dtype)]
        out_specs = [pl.BlockSpec((None, tp, tq), lambda i, j, k: (j // qd, i, j % qd))]
    else:
        out_shape = [SDS((P, Q), out_dtype)]
        out_specs = [pl.BlockSpec((tp, tq), lambda i, j, k: (i, j))]
    if gated:
        out_shape.append(SDS((P, Q), BF16))
        out_specs.append(pl.BlockSpec((tp, tq), lambda i, j, k: (i, j)))

    def body(*refs):
        a_ref, b_ref = refs[0], refs[1]
        pos = 2
        bias_ref = res_ref = gate_ref = o2_ref = None
        if bias is not None:
            bias_ref = refs[pos]
            pos += 1
        if gated:
            res_ref, gate_ref = refs[pos], refs[pos + 1]
            pos += 2
        o_ref = refs[pos]
        pos += 1
        if gated:
            o2_ref = refs[pos]
            pos += 1
        acc_ref = refs[pos] if nk > 1 else None
        k = pl.program_id(2)
        av = a_ref[...]
        if a_silu:
            av = _silu(av.astype(F32))
        part = lax.dot_general(av.astype(BF16), b_ref[...].astype(BF16), dims, preferred_element_type=F32)
        if nk > 1:
            @pl.when(k == 0)
            def _():
                acc_ref[...] = part

            @pl.when(k > 0)
            def _():
                acc_ref[...] += part

        @pl.when(k == nk - 1)
        def _():
            acc = acc_ref[...] if nk > 1 else part
            if bias_ref is not None:
                acc = acc + bias_ref[...]
            if gated:
                if n_seg == 1:
                    g = gate_ref[0]
                else:
                    row = pl.program_id(0) * tp + lax.broadcasted_iota(jnp.int32, (tp, 1), 0)
                    g = jnp.where(row < seg_t, gate_ref[0], gate_ref[1])
                o_ref[...] = (res_ref[...] + g * acc).astype(o_ref.dtype)
                o2_ref[...] = acc.astype(BF16)
            else:
                o_ref[...] = acc.astype(o_ref.dtype)

    out, carried = _carry_call(
        body, carry, grid=(P // tp, Q // tq, nk), out_shape=out_shape, in_specs=in_specs, out_specs=out_specs,
        scratch_shapes=[pltpu.VMEM((tp, tq), F32)] if nk > 1 else [], sem=("parallel", "parallel", "arbitrary"),
        name=name, ins=ins)
    res_out = tuple(out) if gated else out[0]
    return (res_out, carried) if carry else res_out


def _row_tile(seg_t, m):
    return 256 if (seg_t % 256 == 0 and m % 256 == 0) else 128


def _normmod_fwd(x, g, sh, sc, seg_t, name):
    M, D = x.shape
    tm = _row_tile(seg_t, M)
    n_seg = sh.shape[0]
    nt = seg_t // tm

    def seg(i):
        return ((i >= nt).astype(jnp.int32) if n_seg == 2 else 0, 0, 0)

    def body(x_ref, g_ref, sh_ref, sc_ref, o_ref):
        xv = x_ref[...]
        r = lax.rsqrt(jnp.mean(xv * xv, axis=-1, keepdims=True) + EPS)
        y = xv * r * g_ref[...]
        o_ref[...] = (y * (1.0 + sc_ref[0]) + sh_ref[0]).astype(BF16)

    return pl.pallas_call(
        body, out_shape=SDS((M, D), BF16), grid=(M // tm,),
        in_specs=[pl.BlockSpec((tm, D), lambda i: (i, 0)), pl.BlockSpec((1, D), lambda i: (0, 0)),
                  pl.BlockSpec((1, 1, D), seg), pl.BlockSpec((1, 1, D), seg)],
        out_specs=pl.BlockSpec((tm, D), lambda i: (i, 0)),
        compiler_params=_params(("parallel",)), name=name)(x, g, sh, sc)


def _normmod_bwd(x, g, sc, dh, dx_in, seg_t, name, o_prev=None, gate_prev=None):
    M, D = x.shape
    tm = _row_tile(seg_t, M)
    n_seg = sc.shape[0]
    nt = seg_t // tm
    n_in = dx_in.shape[0] // tm
    with_prev = o_prev is not None
    n_segp = gate_prev.shape[0] if with_prev else 0

    def seg(i):
        return ((i >= nt).astype(jnp.int32) if n_seg == 2 else 0, 0, 0)

    def segp(i):
        return ((i >= nt).astype(jnp.int32) if n_segp == 2 else 0, 0, 0)

    def body(*refs):
        x_ref, g_ref, sc_ref, dh_ref, dxin_ref = refs[:5]
        pos = 5
        if with_prev:
            op_ref, gp_ref = refs[5], refs[6]
            pos = 7
        dx_ref, dg_ref, dsh_ref, dsc_ref = refs[pos:pos + 4]
        if with_prev:
            dop_ref, dgp_ref = refs[pos + 4], refs[pos + 5]
        i = pl.program_id(0)
        xv = x_ref[...]
        r = lax.rsqrt(jnp.mean(xv * xv, axis=-1, keepdims=True) + EPS)
        xh = xv * r
        gv = g_ref[...]
        dhv = dh_ref[...].astype(F32)
        dy = dhv * (1.0 + sc_ref[0])
        dxh = dy * gv
        dxv = r * (dxh - xh * jnp.mean(dxh * xh, axis=-1, keepdims=True))
        if n_in * tm < M:
            dxv = dxv + jnp.where(i < n_in, dxin_ref[...], 0.0)
        else:
            dxv = dxv + dxin_ref[...]
        dx_ref[...] = dxv

        @pl.when(i == 0)
        def _():
            dg_ref[...] = jnp.zeros_like(dg_ref)

        first_of_seg = (i == 0) | (i == nt) if n_seg == 2 else (i == 0)

        @pl.when(first_of_seg)
        def _():
            dsh_ref[...] = jnp.zeros_like(dsh_ref)
            dsc_ref[...] = jnp.zeros_like(dsc_ref)

        dg_ref[...] += _colsum(dy * xh)
        dsh_ref[0] += _colsum(dhv)
        dsc_ref[0] += _colsum(dhv * xh * gv)
        if with_prev:
            first_of_segp = (i == 0) | (i == nt) if n_segp == 2 else (i == 0)

            @pl.when(first_of_segp)
            def _():
                dgp_ref[...] = jnp.zeros_like(dgp_ref)

            dop_ref[...] = (gp_ref[0] * dxv).astype(BF16)
            dgp_ref[0] += _colsum(dxv * op_ref[...].astype(F32))

    row = pl.BlockSpec((tm, D), lambda i: (i, 0))
    ins = [x, g, sc, dh, dx_in]
    in_specs = [row, pl.BlockSpec((1, D), lambda i: (0, 0)), pl.BlockSpec((1, 1, D), seg), row,
                pl.BlockSpec((tm, D), lambda i: (jnp.minimum(i, n_in - 1), 0))]
    out_shape = [SDS((M, D), F32), SDS((1, D), F32), SDS((n_seg, 1, D), F32), SDS((n_seg, 1, D), F32)]
    out_specs = [row, pl.BlockSpec((1, D), lambda i: (0, 0)), pl.BlockSpec((1, 1, D), seg), pl.BlockSpec((1, 1, D), seg)]
    if with_prev:
        ins += [o_prev, gate_prev]
        in_specs += [row, pl.BlockSpec((1, 1, D), segp)]
        out_shape += [SDS((M, D), BF16), SDS((n_segp, 1, D), F32)]
        out_specs += [row, pl.BlockSpec((1, 1, D), segp)]
    return pl.pallas_call(
        body, out_shape=out_shape, grid=(M // tm,), in_specs=in_specs, out_specs=out_specs,
        compiler_params=_params(("arbitrary",)), name=name)(*ins)


def _final(x, g, target, o_prev, gate_prev, name):
    T, D = x.shape
    tm = _tile(T, (256, 128))

    def body(x_ref, g_ref, t_ref, op_ref, gp_ref, dx_ref, loss_ref, dg_ref, dop_ref, dgp_ref):
        i = pl.program_id(0)
        xv = x_ref[...]
        r = lax.rsqrt(jnp.mean(xv * xv, axis=-1, keepdims=True) + EPS)
        xh = xv * r
        gv = g_ref[...]
        e = xh * gv - t_ref[...]
        dout = e * (1.0 / D)
        dxh = dout * gv
        dxv = r * (dxh - xh * jnp.mean(dxh * xh, axis=-1, keepdims=True))
        dx_ref[...] = dxv
        dop_ref[...] = (gp_ref[0] * dxv).astype(BF16)

        @pl.when(i == 0)
        def _():
            loss_ref[...] = jnp.zeros_like(loss_ref)
            dg_ref[...] = jnp.zeros_like(dg_ref)
            dgp_ref[...] = jnp.zeros_like(dgp_ref)

        loss_ref[...] += _colsum(e * e) * (0.5 / D)
        dg_ref[...] += _colsum(dout * xh)
        dgp_ref[0] += _colsum(dxv * op_ref[...].astype(F32))

    row = pl.BlockSpec((tm, D), lambda i: (i, 0))
    vec = pl.BlockSpec((1, D), lambda i: (0, 0))
    vec3 = pl.BlockSpec((1, 1, D), lambda i: (0, 0, 0))
    return pl.pallas_call(
        body, out_shape=[SDS((T, D), F32), SDS((1, D), F32), SDS((1, D), F32), SDS((T, D), BF16), SDS((1, 1, D), F32)],
        grid=(T // tm,), in_specs=[row, vec, row, row, vec3], out_specs=[row, vec, vec, row, vec3],
        compiler_params=_params(("arbitrary",)), name=name)(x, g, target, o_prev, gate_prev)


def _gmlp_core(z, lg, lb, ws_ref, bst):
    W = z.shape[1] // 2
    t = _gelu(z)
    u, v = t[:, :W], t[:, W:]
    mu = jnp.mean(v, axis=-1, keepdims=True)
    vc = v - mu
    rstd = lax.rsqrt(jnp.mean(vc * vc, axis=-1, keepdims=True) + EPS)
    vhat = vc * rstd
    vn = vhat * lg + lb
    vp = []
    for h in range(W // CHUNK):
        blk = vn[:, h * CHUNK:(h + 1) * CHUNK].astype(BF16)
        vp.append(jnp.dot(ws_ref[h].astype(BF16), blk, preferred_element_type=F32) + bst[:, h:h + 1])
    return u, vhat, rstd, vp


def _gmlp_fwd(z, ln_g, ln_b, w_s, b_st, W, name):
    M = z.shape[0]
    H = W // CHUNK

    def body(z_ref, lg_ref, lb_ref, ws_ref, bst_ref, o_ref):
        u, _, _, vp = _gmlp_core(z_ref[...], lg_ref[...], lb_ref[...], ws_ref, bst_ref[...])
        for h in range(H):
            o_ref[:, h * CHUNK:(h + 1) * CHUNK] = (u[:, h * CHUNK:(h + 1) * CHUNK] * vp[h]).astype(BF16)

    vec = pl.BlockSpec((1, W), lambda i: (0, 0))
    return pl.pallas_call(
        body, out_shape=SDS((M, 2 * W), BF16), grid=(M // CHUNK,),
        in_specs=[pl.BlockSpec((CHUNK, 2 * W), lambda i: (i, 0)), vec, vec,
                  pl.BlockSpec((H, CHUNK, CHUNK), lambda i: (0, 0, 0)), pl.BlockSpec((CHUNK, H), lambda i: (0, 0))],
        out_specs=pl.BlockSpec((CHUNK, W), lambda i: (i, 0)),
        compiler_params=_params(("parallel",)), name=name)(z, ln_g, ln_b, w_s, b_st)


def _gmlp_bwd(z, dy, ln_g, ln_b, w_s, b_st, W, name):
    M = z.shape[0]
    H = W // CHUNK
    ZW = z.shape[1]

    def body(z_ref, dy_ref, lg_ref, lb_ref, ws_ref, bst_ref, dz_ref, dlg_ref, dlb_ref, dws_ref, dbs_ref, dbin_ref):
        i = pl.program_id(0)

        @pl.when(i == 0)
        def _():
            for r in (dlg_ref, dlb_ref, dws_ref, dbs_ref, dbin_ref):
                r[...] = jnp.zeros_like(r)

        zv = z_ref[...]
        lg = lg_ref[...]
        u, vhat, rstd, vp = _gmlp_core(zv, lg, lb_ref[...], ws_ref, bst_ref[...])
        vn = vhat * lg + lb_ref[...]
        dya = dy_ref[...]
        du_parts, dvn_parts = [], []
        for h in range(H):
            sl = slice(h * CHUNK, (h + 1) * CHUNK)
            dya_h = dya[:, sl]
            du_parts.append(dya_h * vp[h])
            dvp = dya_h * u[:, sl]
            dbs_ref[h] += dvp
            dvp16 = dvp.astype(BF16)
            dws_ref[h] += lax.dot_general(dvp16, vn[:, sl].astype(BF16), (((1,), (1,)), ((), ())),
                                          preferred_element_type=F32)
            dvn_parts.append(lax.dot_general(ws_ref[h].astype(BF16), dvp16, (((0,), (0,)), ((), ())),
                                             preferred_element_type=F32))
        du = jnp.concatenate(du_parts, axis=1)
        dvn = jnp.concatenate(dvn_parts, axis=1)
        dlg_ref[...] += _colsum(dvn * vhat)
        dlb_ref[...] += _colsum(dvn)
        dvh = dvn * lg
        dv = rstd * (dvh - jnp.mean(dvh, axis=-1, keepdims=True) - vhat * jnp.mean(dvh * vhat, axis=-1, keepdims=True))
        dz = jnp.concatenate([du, dv], axis=1) * _dgelu(zv)
        dbin_ref[...] += _colsum(dz)
        dz_ref[...] = dz.astype(BF16)

    vec = pl.BlockSpec((1, W), lambda i: (0, 0))
    mat = pl.BlockSpec((H, CHUNK, CHUNK), lambda i: (0, 0, 0))
    return pl.pallas_call(
        body,
        out_shape=[SDS((M, ZW), BF16), SDS((1, W), F32), SDS((1, W), F32), SDS((H, CHUNK, CHUNK), F32),
                   SDS((H, CHUNK, CHUNK), F32), SDS((1, 2 * W), F32)],
        grid=(M // CHUNK,),
        in_specs=[pl.BlockSpec((CHUNK, 2 * W), lambda i: (i, 0)), pl.BlockSpec((CHUNK, W), lambda i: (i, 0)), vec, vec,
                  mat, pl.BlockSpec((CHUNK, H), lambda i: (0, 0))],
        out_specs=[pl.BlockSpec((CHUNK, 2 * W), lambda i: (i, 0)), vec, vec, mat, mat,
                   pl.BlockSpec((1, 2 * W), lambda i: (0, 0))],
        compiler_params=_params(("arbitrary",)), name=name)(z, dy, ln_g, ln_b, w_s, b_st)


def _halo_specs(tm, width, col, n_rows):
    per = tm // HALO
    last = n_rows // HALO - 1
    prev = pl.BlockSpec((HALO, width), lambda i: (jnp.maximum(i * per - 1, 0), col))
    nxt = pl.BlockSpec((HALO, width), lambda i: (jnp.minimum((i + 1) * per, last), col))
    return prev, nxt


def _edge_flags(i, tm, seg_t, m):
    r0 = i * tm
    has_prev = jnp.where((r0 == 0) | (r0 == seg_t), 0.0, 1.0)
    has_next = jnp.where((r0 + tm == seg_t) | (r0 + tm == m), 0.0, 1.0)
    return has_prev, has_next


def _glu(zz, wb):
    return zz[:, :wb] * _sigmoid(zz[:, wb:])


def _build_shifts(src_ref, sh_ref):
    n = src_ref.shape[0] - 8
    for r in range(1, 8):
        sh_ref[r - 1, pl.ds(0, n), :] = src_ref[pl.ds(r, n), :]


def _shifted(src_ref, sh_ref, off, r0, rc):
    a, r = divmod(off, 8)
    if r == 0:
        return src_ref[pl.ds(8 * a + r0, rc), :]
    return sh_ref[r - 1, pl.ds(8 * a + r0, rc), :]


def _conv_taps(src_ref, sh_ref, w_ref, first, tm, kw, flip=False):
    rc = 32
    parts = []
    for c in range(tm // rc):
        acc = None
        for k in range(kw):
            wk = w_ref[pl.ds(kw - 1 - k if flip else k, 1), :]
            term = _shifted(src_ref, sh_ref, first + k, c * rc, rc) * wk
            acc = term if acc is None else acc + term
        parts.append(acc)
    return jnp.concatenate(parts, axis=0)


def _conf_fwd(z, y, conv_w, conv_b, ln_g, ln_b, W, Wb, seg_t, name, carry=None):
    M = z.shape[0]
    tm = _row_tile(seg_t, M)
    kw = conv_w.shape[0]
    pad = (kw - 1) // 2
    col = (2 * W) // (2 * Wb)

    def body(zc_ref, zp_ref, zn_ref, y_hbm, cw_ref, cb_ref, lg_ref, lb_ref, o_ref, hc_ref, hs_ref, sh_ref):
        del y_hbm
        hp, hn = _edge_flags(pl.program_id(0), tm, seg_t, M)
        hs_ref[pl.ds(0, HALO), :] = _glu(zp_ref[...], Wb) * hp
        hs_ref[pl.ds(HALO, tm), :] = _glu(zc_ref[...], Wb)
        hs_ref[pl.ds(HALO + tm, HALO), :] = _glu(zn_ref[...], Wb) * hn
        _build_shifts(hs_ref, sh_ref)
        hc = _conv_taps(hs_ref, sh_ref, cw_ref, HALO - pad, tm, kw) + cb_ref[...]
        hc_ref[...] = hc
        mu = jnp.mean(hc, axis=-1, keepdims=True)
        c = hc - mu
        rstd = lax.rsqrt(jnp.mean(c * c, axis=-1, keepdims=True) + EPS)
        o_ref[...] = _silu(c * rstd * lg_ref[...] + lb_ref[...]).astype(BF16)

    prev, nxt = _halo_specs(tm, 2 * Wb, col, M)
    vec = pl.BlockSpec((1, Wb), lambda i: (0, 0))
    out, carried = _carry_call(
        body, carry, out_shape=[SDS(y.shape, BF16), SDS((M, Wb), F32)], grid=(M // tm,),
        in_specs=[pl.BlockSpec((tm, 2 * Wb), lambda i: (i, col)), prev, nxt, pl.BlockSpec(memory_space=pl.ANY),
                  pl.BlockSpec((kw, Wb), lambda i: (0, 0)), vec, vec, vec],
        out_specs=[pl.BlockSpec((tm, Wb), lambda i: (i, W // Wb)), pl.BlockSpec((tm, Wb), lambda i: (i, 0))],
        scratch_shapes=[pltpu.VMEM((tm + 2 * HALO, Wb), F32), pltpu.VMEM((7, tm + 2 * HALO, Wb), F32)],
        aliases={3: 0}, sem=("parallel",), name=name, ins=(z, z, z, y, conv_w, conv_b, ln_g, ln_b))
    return (out, carried) if carry else out


def _conf_bwd1(hc, dy, ln_g, ln_b, W, Wb, seg_t, name):
    M = hc.shape[0]
    tm = _row_tile(seg_t, M)

    def body(hc_ref, dy_ref, lg_ref, lb_ref, dhc_ref, dlg_ref, dlb_ref, dcb_ref):
        i = pl.program_id(0)

        @pl.when(i == 0)
        def _():
            for r in (dlg_ref, dlb_ref, dcb_ref):
                r[...] = jnp.zeros_like(r)

        hc = hc_ref[...]
        mu = jnp.mean(hc, axis=-1, keepdims=True)
        c = hc - mu
        rstd = lax.rsqrt(jnp.mean(c * c, axis=-1, keepdims=True) + EPS)
        hh = c * rstd
        lg = lg_ref[...]
        dhn = dy_ref[...] * _dsilu(hh * lg + lb_ref[...])
        dlg_ref[...] += _colsum(dhn * hh)
        dlb_ref[...] += _colsum(dhn)
        dhh = dhn * lg
        dhc = rstd * (dhh - jnp.mean(dhh, axis=-1, keepdims=True) - hh * jnp.mean(dhh * hh, axis=-1, keepdims=True))
        dcb_ref[...] += _colsum(dhc)
        dhc_ref[...] = dhc

    vec = pl.BlockSpec((1, Wb), lambda i: (0, 0))
    return pl.pallas_call(
        body, out_shape=[SDS((M, Wb), F32), SDS((1, Wb), F32), SDS((1, Wb), F32), SDS((1, Wb), F32)], grid=(M // tm,),
        in_specs=[pl.BlockSpec((tm, Wb), lambda i: (i, 0)), pl.BlockSpec((tm, Wb), lambda i: (i, W // Wb)), vec, vec],
        out_specs=[pl.BlockSpec((tm, Wb), lambda i: (i, 0)), vec, vec, vec],
        compiler_params=_params(("arbitrary",)), name=name)(hc, dy, ln_g, ln_b)


def _conf_bwd2(z, dhc, dz, conv_w, W, Wb, seg_t, name, carry=None):
    M = z.shape[0]
    tm = _row_tile(seg_t, M)
    kw = conv_w.shape[0]
    pad = (kw - 1) // 2
    col = (2 * W) // (2 * Wb)

    def body(zc_ref, zp_ref, zn_ref, dc_ref, dp_ref, dn_ref, dz_hbm, cw_ref, dz_ref, dcw_ref, dbin_ref, hs_ref, ds_ref,
             hsh_ref, dsh_ref):
        del dz_hbm
        i = pl.program_id(0)

        @pl.when(i == 0)
        def _():
            dcw_ref[...] = jnp.zeros_like(dcw_ref)
            dbin_ref[...] = jnp.zeros_like(dbin_ref)

        hp, hn = _edge_flags(i, tm, seg_t, M)
        zc = zc_ref[...]
        hs_ref[pl.ds(0, HALO), :] = _glu(zp_ref[...], Wb) * hp
        hs_ref[pl.ds(HALO, tm), :] = _glu(zc, Wb)
        hs_ref[pl.ds(HALO + tm, HALO), :] = _glu(zn_ref[...], Wb) * hn
        dcur = dc_ref[...]
        ds_ref[pl.ds(0, HALO), :] = dp_ref[...] * hp
        ds_ref[pl.ds(HALO, tm), :] = dcur
        ds_ref[pl.ds(HALO + tm, HALO), :] = dn_ref[...] * hn
        _build_shifts(ds_ref, dsh_ref)
        _build_shifts(hs_ref, hsh_ref)
        dh = _conv_taps(ds_ref, dsh_ref, cw_ref, HALO - pad, tm, kw, flip=True)
        for k in range(kw):
            dcw_ref[pl.ds(k, 1), :] += _colsum(dcur * _shifted(hs_ref, hsh_ref, HALO - pad + k, 0, tm))
        a, gt = zc[:, :Wb], zc[:, Wb:]
        s = _sigmoid(gt)
        dz = jnp.concatenate([dh * s, dh * a * s * (1.0 - s)], axis=1)
        dbin_ref[...] += _colsum(dz)
        dz_ref[...] = dz.astype(BF16)

    prev, nxt = _halo_specs(tm, 2 * Wb, col, M)
    dprev, dnxt = _halo_specs(tm, Wb, 0, M)
    out, carried = _carry_call(
        body, carry, out_shape=[SDS(dz.shape, BF16), SDS((kw, Wb), F32), SDS((1, 2 * Wb), F32)], grid=(M // tm,),
        in_specs=[pl.BlockSpec((tm, 2 * Wb), lambda i: (i, col)), prev, nxt,
                  pl.BlockSpec((tm, Wb), lambda i: (i, 0)), dprev, dnxt, pl.BlockSpec(memory_space=pl.ANY),
                  pl.BlockSpec((kw, Wb), lambda i: (0, 0))],
        out_specs=[pl.BlockSpec((tm, 2 * Wb), lambda i: (i, col)), pl.BlockSpec((kw, Wb), lambda i: (0, 0)),
                   pl.BlockSpec((1, 2 * Wb), lambda i: (0, 0))],
        scratch_shapes=[pltpu.VMEM((tm + 2 * HALO, Wb), F32), pltpu.VMEM((tm + 2 * HALO, Wb), F32),
                        pltpu.VMEM((7, tm + 2 * HALO, Wb), F32), pltpu.VMEM((7, tm + 2 * HALO, Wb), F32)],
        aliases={6: 0}, sem=("arbitrary",), name=name, ins=(z, z, z, dhc, dhc, dhc, dz, conv_w))
    return (out, carried) if carry else out


_TF = (1408, 512, 256, 128)
_RC = 16
_CG = 256


def _col_groups(width):
    return [(c0, min(_CG, width - c0)) for c0 in range(0, width, _CG)]


def _ffn_act_fwd(z, conv_w, conv_b, seg_t, name):
    M, F2 = z.shape
    Fd = F2 // 2
    tm = _row_tile(seg_t, M)
    tf = _tile(Fd, _TF)
    nf = Fd // tf
    per, last = tm // HALO, M // HALO - 1

    def body(g_ref, gp_ref, gn_ref, u_ref, cw_ref, cb_ref, o_ref, gc_ref, gs_ref):
        hp, hn = _edge_flags(pl.program_id(0), tm, seg_t, M)
        gs_ref[pl.ds(0, HALO), :] = gp_ref[...].astype(F32) * hp
        gs_ref[pl.ds(HALO, tm), :] = g_ref[...].astype(F32)
        gs_ref[pl.ds(HALO + tm, HALO), :] = gn_ref[...].astype(F32) * hn
        for c0, cw in _col_groups(tf):
            cs = pl.ds(c0, cw)
            w0, w1, w2, cb = cw_ref[pl.ds(0, 1), cs], cw_ref[pl.ds(1, 1), cs], cw_ref[pl.ds(2, 1), cs], cb_ref[:, cs]
            for r0 in range(0, tm, _RC):
                gc = (gs_ref[pl.ds(HALO - 1 + r0, _RC), cs] * w0 + gs_ref[pl.ds(HALO + r0, _RC), cs] * w1
                      + gs_ref[pl.ds(HALO + 1 + r0, _RC), cs] * w2 + cb)
                o_ref[pl.ds(r0, _RC), cs] = (_silu(gc) * u_ref[pl.ds(r0, _RC), cs].astype(F32)).astype(BF16)
                gc_ref[pl.ds(r0, _RC), cs] = gc.astype(BF16)

    return pl.pallas_call(
        body, out_shape=[SDS((M, Fd), BF16), SDS((M, Fd), BF16)], grid=(M // tm, nf),
        in_specs=[pl.BlockSpec((tm, tf), lambda i, j: (i, j)),
                  pl.BlockSpec((HALO, tf), lambda i, j: (jnp.maximum(i * per - 1, 0), j)),
                  pl.BlockSpec((HALO, tf), lambda i, j: (jnp.minimum((i + 1) * per, last), j)),
                  pl.BlockSpec((tm, tf), lambda i, j: (i, nf + j)),
                  pl.BlockSpec((3, tf), lambda i, j: (0, j)), pl.BlockSpec((1, tf), lambda i, j: (0, j))],
        out_specs=[pl.BlockSpec((tm, tf), lambda i, j: (i, j)), pl.BlockSpec((tm, tf), lambda i, j: (i, j))],
        scratch_shapes=[pltpu.VMEM((tm + 2 * HALO, tf), F32)],
        compiler_params=_params(("parallel", "parallel")), name=name)(z, z, z, z, conv_w, conv_b)


def _ffn_act_bwd(z, gc, da, conv_w, seg_t, name):
    M, F2 = z.shape
    Fd = F2 // 2
    tm = _row_tile(seg_t, M)
    tf = _tile(Fd, _TF)
    nf = Fd // tf
    per, last = tm // HALO, M // HALO - 1
    n_piece = tm // _RC

    def body(g_ref, c_ref, cp_ref, cn_ref, u_ref, up_ref, un_ref, a_ref, ap_ref, an_ref, cw_ref,
             dz_ref, dcw_ref, dcb_ref, ds_ref):
        i = pl.program_id(1)

        def tile():
            hp, hn = _edge_flags(i, tm, seg_t, M)

            @pl.when(i == 0)
            def _():
                dcw_ref[...] = jnp.zeros_like(dcw_ref)
                dcb_ref[...] = jnp.zeros_like(dcb_ref)

            def fold(v):
                return v[:8] + v[8:]

            for c0, cw in _col_groups(tf):
                cs = pl.ds(c0, cw)
                w0, w1, w2 = cw_ref[pl.ds(0, 1), cs], cw_ref[pl.ds(1, 1), cs], cw_ref[pl.ds(2, 1), cs]
                for ci in range(-1, n_piece + 1):
                    r0 = ci * _RC
                    if ci < 0:
                        gcv, ue, ae = cp_ref[:, cs], up_ref[:, cs], ap_ref[:, cs].astype(F32) * hp
                    elif ci == n_piece:
                        gcv, ue, ae = cn_ref[:, cs], un_ref[:, cs], an_ref[:, cs].astype(F32) * hn
                    else:
                        rows = pl.ds(r0, _RC)
                        gcv, ue, ae = c_ref[rows, cs], u_ref[rows, cs], a_ref[rows, cs].astype(F32)
                    gcv, ue = gcv.astype(F32), ue.astype(F32)
                    sg = _sigmoid(gcv)
                    t = ae * sg
                    ds_ref[pl.ds(HALO + r0, _RC), cs] = t * ue * (1.0 + gcv * (1.0 - sg))
                    if 0 <= ci < n_piece:
                        dz_ref[1, pl.ds(r0, _RC), cs] = (t * gcv).astype(BF16)
                acc = [jnp.zeros((8, cw), F32) for _ in range(4)]
                for r0 in range(0, tm, _RC):
                    b = HALO + r0
                    d = [ds_ref[pl.ds(b + 1 - k, _RC), cs] for k in range(3)]
                    dz_ref[0, pl.ds(r0, _RC), cs] = (d[0] * w0 + d[1] * w1 + d[2] * w2).astype(BF16)
                    gv = g_ref[pl.ds(r0, _RC), cs].astype(F32)
                    for k in range(3):
                        acc[k] = acc[k] + fold(d[k] * gv)
                    acc[3] = acc[3] + fold(d[1])
                for k in range(3):
                    dcw_ref[pl.ds(k, 1), cs] += _colsum(acc[k])
                dcb_ref[:, cs] += _colsum(acc[3])

        tile()

    def cur(off):
        return pl.BlockSpec((tm, tf), lambda j, i: (i, off + j))

    def prv(off):
        return pl.BlockSpec((HALO, tf), lambda j, i: (jnp.maximum(i * per - 1, 0), off + j))

    def nxt(off):
        return pl.BlockSpec((HALO, tf), lambda j, i: (jnp.minimum((i + 1) * per, last), off + j))

    return pl.pallas_call(
        body, out_shape=[SDS((2, M, Fd), BF16), SDS((3, Fd), F32), SDS((1, Fd), F32)], grid=(nf, M // tm),
        in_specs=[cur(0), cur(0), prv(0), nxt(0), cur(nf), prv(nf), nxt(nf), cur(0), prv(0), nxt(0),
                  pl.BlockSpec((3, tf), lambda j, i: (0, j))],
        out_specs=[pl.BlockSpec((2, tm, tf), lambda j, i: (0, i, j)),
                   pl.BlockSpec((3, tf), lambda j, i: (0, j)), pl.BlockSpec((1, tf), lambda j, i: (0, j))],
        scratch_shapes=[pltpu.VMEM((tm + 2 * HALO, tf), F32)],
        compiler_params=_params(("parallel", "arbitrary")), name=name)(
            z, gc, gc, gc, z, z, z, da, da, da, conv_w)


_LN2 = math.log(2.0)
_QSCALE = (NOPE + ROPE) ** -0.5 / _LN2


def _swap32(x):
    lane = lax.broadcasted_iota(jnp.int32, x.shape, 1)
    return jnp.where((lane % 64) < 32, pltpu.roll(x, 96, axis=1), pltpu.roll(x, 32, axis=1))


def _rms(x, g):
    r = lax.rsqrt(jnp.mean(x * x, axis=-1, keepdims=True) + EPS)
    return x * r * g


def _rms_bwd(x, g, dy):
    r = lax.rsqrt(jnp.mean(x * x, axis=-1, keepdims=True) + EPS)
    xh = x * r
    dxh = dy * g
    return r * (dxh - xh * jnp.mean(dxh * xh, axis=-1, keepdims=True)), _colsum(dy * xh)


def _mla_prep_fwd(z, gq, gkv, cos, sin, QL, KL, name):
    M, NZ = z.shape
    tm = _tile(M, (256, 128))

    def body(z_ref, gq_ref, gkv_ref, cos_ref, sin_ref, cq_ref, ckv_ref, kpe_ref):
        zv = z_ref[...]
        cq_ref[...] = _rms(zv[:, :QL], gq_ref[...]).astype(BF16)
        ckv_ref[...] = _rms(zv[:, QL:QL + KL], gkv_ref[...]).astype(BF16)
        kp = zv[:, QL + KL:]
        r = kp * cos_ref[...] + _swap32(kp) * sin_ref[...]
        lane = lax.broadcasted_iota(jnp.int32, r.shape, 1)
        kpe_ref[0] = jnp.where(lane < ROPE, r, 0.0).astype(BF16)
        kpe_ref[1] = jnp.where(lane >= ROPE, r, 0.0).astype(BF16)

    tab = pl.BlockSpec((tm, 128), lambda i: (i, 0))
    return pl.pallas_call(
        body, out_shape=[SDS((M, QL), BF16), SDS((M, KL), BF16), SDS((2, M, 128), BF16)], grid=(M // tm,),
        in_specs=[pl.BlockSpec((tm, NZ), lambda i: (i, 0)), pl.BlockSpec((1, QL), lambda i: (0, 0)),
                  pl.BlockSpec((1, KL), lambda i: (0, 0)), tab, tab],
        out_specs=[pl.BlockSpec((tm, QL), lambda i: (i, 0)), pl.BlockSpec((tm, KL), lambda i: (i, 0)),
                   pl.BlockSpec((2, tm, 128), lambda i: (0, i, 0))],
        compiler_params=_params(("parallel",)), name=name)(z, gq, gkv, cos, sin)


def _mla_prep_bwd(z, dcq, dckv, dkpe, gq, gkv, cos, sin, QL, KL, seg_t, name):
    M, NZ = z.shape
    H = dkpe.shape[0]
    tm = _row_tile(seg_t, M)
    nt = seg_t // tm

    def body(z_ref, dcq_ref, dckv_ref, dkpe_ref, gq_ref, gkv_ref, cos_ref, sin_ref, dz_ref, dgq_ref, dgkv_ref):
        i = pl.program_id(0)

        @pl.when(i == 0)
        def _():
            dgq_ref[...] = jnp.zeros_like(dgq_ref)
            dgkv_ref[...] = jnp.zeros_like(dgkv_ref)

        zv = z_ref[...]
        dyq = jnp.where(i < nt, dcq_ref[...], 0.0)
        dxq, dgq = _rms_bwd(zv[:, :QL], gq_ref[...], dyq)
        dxkv, dgkv = _rms_bwd(zv[:, QL:QL + KL], gkv_ref[...], dckv_ref[...])
        dgq_ref[...] += dgq
        dgkv_ref[...] += dgkv
        even = dkpe_ref[0]
        odd = dkpe_ref[1]
        for h in range(2, H, 2):
            even = even + dkpe_ref[h]
            odd = odd + dkpe_ref[h + 1]
        lane = lax.broadcasted_iota(jnp.int32, even.shape, 1)
        dr = jnp.where(lane < ROPE, even, odd)
        dkp = dr * cos_ref[...] - _swap32(dr) * sin_ref[...]
        dz_ref[...] = jnp.concatenate([dxq, dxkv, dkp], axis=1).astype(BF16)

    tab = pl.BlockSpec((tm, 128), lambda i: (i, 0))
    return pl.pallas_call(
        body, out_shape=[SDS((M, NZ), BF16), SDS((1, QL), F32), SDS((1, KL), F32)], grid=(M // tm,),
        in_specs=[pl.BlockSpec((tm, NZ), lambda i: (i, 0)),
                  pl.BlockSpec((tm, QL), lambda i: (jnp.minimum(i, nt - 1), 0)),
                  pl.BlockSpec((tm, KL), lambda i: (i, 0)), pl.BlockSpec((H, tm, 128), lambda i: (0, i, 0)),
                  pl.BlockSpec((1, QL), lambda i: (0, 0)), pl.BlockSpec((1, KL), lambda i: (0, 0)), tab, tab],
        out_specs=[pl.BlockSpec((tm, NZ), lambda i: (i, 0)), pl.BlockSpec((1, QL), lambda i: (0, 0)),
                   pl.BlockSpec((1, KL), lambda i: (0, 0))],
        compiler_params=_params(("arbitrary",)), name=name)(z, dcq, dckv, dkpe, gq, gkv, cos, sin)


def _qrope_fwd(q, cos, sin, HN, name):
    T, NQ = q.shape
    tm = _tile(T, (256, 128))

    def body(q_ref, cos_ref, sin_ref, o_ref):
        o_ref[:, :HN] = (q_ref[:, :HN] * _QSCALE).astype(BF16)
        for cb in range((NQ - HN) // 128):
            sl = slice(HN + cb * 128, HN + (cb + 1) * 128)
            xv = q_ref[:, sl]
            o_ref[:, sl] = ((xv * cos_ref[...] + _swap32(xv) * sin_ref[...]) * _QSCALE).astype(BF16)

    tab = pl.BlockSpec((tm, 128), lambda i: (i, 0))
    return pl.pallas_call(
        body, out_shape=SDS((T, NQ), BF16), grid=(T // tm,),
        in_specs=[pl.BlockSpec((tm, NQ), lambda i: (i, 0)), tab, tab],
        out_specs=pl.BlockSpec((tm, NQ), lambda i: (i, 0)),
        compiler_params=_params(("parallel",)), name=name)(q, cos, sin)


def _qrope_bwd(dqpe, dqa, cos, sin, HN, name):
    T, HW = dqpe.shape
    HR = HW // 2
    tm = _tile(T, (256, 128))

    def body(d_ref, dqa_hbm, cos_ref, sin_ref, o_ref):
        del dqa_hbm
        for pr in range(HR // 128):
            dr = d_ref[:, 2 * pr * 128:(2 * pr + 1) * 128] + d_ref[:, (2 * pr + 1) * 128:(2 * pr + 2) * 128]
            o_ref[:, pr * 128:(pr + 1) * 128] = (dr * cos_ref[...] - _swap32(dr) * sin_ref[...]).astype(BF16)

    tab = pl.BlockSpec((tm, 128), lambda i: (i, 0))
    return pl.pallas_call(
        body, out_shape=SDS(dqa.shape, BF16), grid=(T // tm,),
        in_specs=[pl.BlockSpec((tm, HW), lambda i: (i, 0)), pl.BlockSpec(memory_space=pl.ANY), tab, tab],
        out_specs=pl.BlockSpec((tm, HR), lambda i: (i, HN // HR)),
        input_output_aliases={1: 0}, compiler_params=_params(("parallel",)), name=name)(dqpe, dqa, cos, sin)


_ATT_SUB = 4
_ATT_SUB_B = 4
_NT = (((1,), (1,)), ((), ()))
_TN = (((0,), (0,)), ((), ()))


def _attn_fwd(qa, kv, kpe, T, H, name, carry=None):
    M = kv.shape[0]
    tq = _tile(T, (1024, 512, 256, 128))
    scale = (NOPE + ROPE) ** -0.5

    def body(qn_ref, qp_ref, kv_ref, kpe_ref, o_ref, lse_ref, kc_ref):
        @pl.when(pl.program_id(1) == 0)
        def _():
            kc_ref[:, :NOPE] = kv_ref[:, :NOPE]
            kc_ref[:, NOPE:] = kpe_ref[0]

        rs = tq // _ATT_SUB
        outs, lses = [], []
        for u in range(_ATT_SUB):
            rows = pl.ds(u * rs, rs)
            qc = jnp.concatenate([qn_ref[rows, :], qp_ref[rows, :]], axis=1)
            s = lax.dot_general(qc, kc_ref[...], _NT, preferred_element_type=F32)
            m = jnp.max(s, axis=-1, keepdims=True)
            p = jnp.exp2(s - m)
            l = jnp.sum(p, axis=-1, keepdims=True)
            o = jnp.dot(p.astype(BF16), kv_ref[:, NOPE:], preferred_element_type=F32)
            outs.append((o / l).astype(BF16))
            lses.append(jnp.broadcast_to(m + jnp.log2(l), (rs, 128)))
        o_ref[...] = jnp.concatenate(outs, axis=0)
        lse_ref[...] = jnp.concatenate(lses, axis=0)

    return _carry_call(
        body, carry, out_shape=[SDS((T, H * VDIM), BF16), SDS((T, H * 128), F32)], grid=(H, T // tq),
        in_specs=[pl.BlockSpec((tq, NOPE), lambda h, i: (i, h)), pl.BlockSpec((tq, 128), lambda h, i: (i, H + h // 2)),
                  pl.BlockSpec((M, NOPE + VDIM), lambda h, i: (0, h)), pl.BlockSpec((1, M, 128), lambda h, i: (h % 2, 0, 0))],
        out_specs=[pl.BlockSpec((tq, VDIM), lambda h, i: (i, h)), pl.BlockSpec((tq, 128), lambda h, i: (i, h))],
        scratch_shapes=[pltpu.VMEM((M, NOPE + 128), BF16)],
        sem=("parallel", "arbitrary"), name=name, ins=(qa, qa, kv, kpe))


def _attn_bwd(qa, kv, kpe, do, o, lse, T, H, name, carry=None):
    M = kv.shape[0]
    tq = _tile(T, (1024, 512, 256, 128))
    nq = T // tq
    scale = (NOPE + ROPE) ** -0.5

    def body(qn_ref, qp_ref, kv_ref, kpe_ref, do_ref, o_ref, lse_ref, dqa_ref, dqpe_ref, dkv_ref, dkpe_ref, kc_ref,
             dk_acc, dv_acc):
        i = pl.program_id(1)

        @pl.when(i == 0)
        def _():
            kc_ref[:, :NOPE] = kv_ref[:, :NOPE]
            kc_ref[:, NOPE:] = kpe_ref[0]
            dk_acc[...] = jnp.zeros_like(dk_acc)
            dv_acc[...] = jnp.zeros_like(dv_acc)

        rs = tq // _ATT_SUB_B
        p16s, ds16s = [], []
        for u in range(_ATT_SUB_B):
            rows = pl.ds(u * rs, rs)
            qc = jnp.concatenate([qn_ref[rows, :], qp_ref[rows, :]], axis=1)
            dov = do_ref[rows, :]
            s = lax.dot_general(qc, kc_ref[...], _NT, preferred_element_type=F32)
            p = jnp.exp2(s - lse_ref[rows, 0:1])
            dp = lax.dot_general(dov, kv_ref[:, NOPE:], _NT, preferred_element_type=F32)
            delta = jnp.sum(dov.astype(F32) * o_ref[rows, :].astype(F32), axis=-1, keepdims=True)
            ds16s.append((p * (dp - delta)).astype(BF16))
            p16s.append(p.astype(BF16))
        p16 = jnp.concatenate(p16s, axis=0)
        ds16 = jnp.concatenate(ds16s, axis=0)
        qc = jnp.concatenate([qn_ref[...], qp_ref[...]], axis=1)
        dq = jnp.dot(ds16, kc_ref[...], preferred_element_type=F32) * scale
        dqa_ref[...] = dq[:, :NOPE].astype(BF16)
        dqpe_ref[...] = dq[:, NOPE:]
        dv_acc[...] += lax.dot_general(p16, do_ref[...], _TN, preferred_element_type=F32)
        dk_acc[...] += lax.dot_general(ds16, qc, _TN, preferred_element_type=F32)

        @pl.when(i == nq - 1)
        def _():
            dkv_ref[:, :NOPE] = (dk_acc[:, :NOPE] * _LN2).astype(BF16)
            dkv_ref[:, NOPE:] = dv_acc[...].astype(BF16)
            dkpe_ref[0] = dk_acc[:, NOPE:] * _LN2

    return _carry_call(
        body, carry,
        out_shape=[SDS((T, H * (NOPE + ROPE)), BF16), SDS((T, H * 128), F32), SDS((M, H * (NOPE + VDIM)), BF16),
                   SDS((H, M, 128), F32)],
        grid=(H, nq),
        in_specs=[pl.BlockSpec((tq, NOPE), lambda h, i: (i, h)), pl.BlockSpec((tq, 128), lambda h, i: (i, H + h // 2)),
                  pl.BlockSpec((M, NOPE + VDIM), lambda h, i: (0, h)), pl.BlockSpec((1, M, 128), lambda h, i: (h % 2, 0, 0)),
                  pl.BlockSpec((tq, VDIM), lambda h, i: (i, h)), pl.BlockSpec((tq, VDIM), lambda h, i: (i, h)),
                  pl.BlockSpec((tq, 128), lambda h, i: (i, h))],
        out_specs=[pl.BlockSpec((tq, NOPE), lambda h, i: (i, h)), pl.BlockSpec((tq, 128), lambda h, i: (i, h)),
                   pl.BlockSpec((M, NOPE + VDIM), lambda h, i: (0, h)), pl.BlockSpec((1, M, 128), lambda h, i: (h, 0, 0))],
        scratch_shapes=[pltpu.VMEM((M, NOPE + 128), BF16), pltpu.VMEM((M, NOPE + 128), F32), pltpu.VMEM((M, VDIM), F32)],
        sem=("parallel", "arbitrary"), name=name, ins=(qa, qa, kv, kpe, do, o, lse))


def _adamw(w, m, v, name, g=None, recv=None, carry=None):
    R, C = w.shape
    summed = recv is not None
    n_recv = len(recv) if summed else 1
    runs = [r.shape[1] for r in recv] if summed else [R]
    tr = math.gcd(*runs)
    for cand in (1024, 512, 256, 128, 64, 32, 16, 8):
        if tr % cand == 0 and cand * C <= 131072:
            tr = cand
            break
    first = [sum(runs[:r]) // tr for r in range(n_recv + 1)]
    c1 = 1.0 - ADAM_B1 ** ADAM_STEP
    c2 = 1.0 - ADAM_B2 ** ADAM_STEP

    def update(gv, w_ref, m_ref, v_ref, d_ref, nm_ref, nv_ref):
        mn = ADAM_B1 * m_ref[...] + (1.0 - ADAM_B1) * gv
        vn = ADAM_B2 * v_ref[...] + (1.0 - ADAM_B2) * (gv * gv)
        nm_ref[...] = mn
        nv_ref[...] = vn
        d_ref[...] = -ADAM_LR * ((mn / c1) / (jnp.sqrt(vn / c2) + ADAM_EPS) + ADAM_WD * w_ref[...])

    def body(*refs):
        w_ref, m_ref, v_ref = refs[:3]
        g_refs = refs[3:3 + n_recv]
        outs = refs[3 + n_recv:]
        if not summed:
            update(g_refs[0][...], w_ref, m_ref, v_ref, *outs)
            return
        i = pl.program_id(0)
        for r in range(n_recv):
            @pl.when((i >= first[r]) & (i < first[r + 1]))
            def _():
                gv = g_refs[r][0].astype(F32)
                for d in range(1, N_DEV):
                    gv = gv + g_refs[r][d].astype(F32)
                outs[0][...] = gv
                update(gv, w_ref, m_ref, v_ref, *outs[1:])

    blk = pl.BlockSpec((tr, C), lambda i: (i, 0))
    if summed:
        g_specs = [pl.BlockSpec((N_DEV, tr, C), functools.partial(
            lambda i, lo, n: (0, jnp.clip(i - lo, 0, n - 1), 0), lo=first[r], n=first[r + 1] - first[r]))
                   for r in range(n_recv)]
    else:
        g_specs = [blk]
    n_out = 4 if summed else 3
    out, carried = _carry_call(
        body, carry, out_shape=[SDS((R, C), F32)] * n_out, grid=(R // tr,), in_specs=[blk, blk, blk] + g_specs,
        out_specs=[blk] * n_out, scratch_shapes=[], sem=("parallel",), name=name,
        ins=(w, m, v, *(recv if summed else [g])))
    return (out, carried) if carry else out


WEIGHTS = ['c_ctx', 'norm1_g', 'norm2_g', 'w_ada', 'b_ada', 'ab_w_in', 'ab_b_in', 'a_ln_g', 'a_ln_b', 'a_w_s', 'a_b_s',
           'b_conv_w', 'b_conv_b', 'b_ln_g', 'b_ln_b', 'ab_w_out', 'mla_w_in', 'mla_q_norm_g', 'mla_w_uq',
           'mla_kv_norm_g', 'mla_w_ukv', 'mla_w_o', 'ffn_w_up', 'ffn_conv_w', 'ffn_conv_b', 'ffn_w_down', 'final_norm_g']


def _pack(parts):
    flat = jnp.concatenate([p.reshape(-1).astype(F32) for p in parts])
    n = flat.shape[0]
    unit = 65536 if n > 65536 else 1024
    n_pad = -(-n // unit) * unit
    return jnp.pad(flat, (0, n_pad - n)).reshape(n_pad // 128, 128)


def _unpack(flat, like):
    out, off = [], 0
    for shp in like:
        n = math.prod(shp)
        out.append(flat[..., off:off + n].reshape(flat.shape[:-1] + tuple(shp)))
        off += n
    return out


def _rope_tables(T, Tc):
    rows = T // GRID_W
    row = jnp.repeat(jnp.arange(rows, dtype=F32), GRID_W)
    col = jnp.tile(jnp.arange(GRID_W, dtype=F32), rows)
    n_freq = ROPE // 4
    inv = ROPE_THETA ** (-jnp.arange(n_freq, dtype=F32) / n_freq)
    ang = jnp.concatenate([row[:, None] * inv, col[:, None] * inv], axis=-1)
    cos, sin = jnp.cos(ang), jnp.sin(ang)
    cos = jnp.tile(cos, (1, 128 // (ROPE // 2)))
    sin = jnp.tile(jnp.concatenate([-sin, sin], axis=1), (1, 128 // ROPE))
    return (jnp.concatenate([cos, jnp.ones((Tc, 128), F32)], axis=0),
            jnp.concatenate([sin, jnp.zeros((Tc, 128), F32)], axis=0))


def _step(a):
    ax, ay, ac = lax.axis_index("x"), lax.axis_index("y"), lax.axis_index("c")
    me = 4 * ax + 2 * ay + ac
    T, D = a['x'].shape[1:]
    Tc = a['ctx'].shape[1]
    M = T + Tc
    W, Wb = a['a_ln_g'].shape[1], a['b_ln_g'].shape[1]
    assert W == Wb and T % Tc == 0
    Fd = a['ffn_conv_b'].shape[1]
    QL, KL = a['mla_q_norm_g'].shape[1] * N_DEV, a['mla_kv_norm_g'].shape[1] * N_DEV
    H = a['mla_w_ukv'].shape[2] * N_DEV // (NOPE + VDIM)
    HN, HR = H * NOPE, H * ROPE
    kw = a['b_conv_w'].shape[1]
    NA = a['w_ada'].shape[2]
    bf = lambda t: t.astype(BF16)

    small_shapes = [(D,), (kw, Wb // N_DEV), (2, 3, Fd // N_DEV), (QL // N_DEV,), (KL // N_DEV,)]
    g_small = _all_gather(_pack([a['c'][0], a['b_conv_w'][0], a['ffn_conv_w'], a['mla_q_norm_g'][0], a['mla_kv_norm_g'][0]]),
                          "ag_small")
    c_all, bcw, fcw, gq, gkv = _unpack(g_small.reshape(N_DEV, -1), small_shapes)
    bcw = jnp.transpose(bcw, (1, 0, 2)).reshape(kw, Wb)
    fcw = jnp.transpose(fcw, (1, 2, 0, 3)).reshape(2, 3, Fd)
    gq, gkv = gq.reshape(1, QL), gkv.reshape(1, KL)

    a16 = jnp.concatenate([c_all, a['c_ctx'][None], jnp.zeros((N_DEV - 1, D), F32)], axis=0)
    b_loc = lax.dynamic_slice(a['b_ada'], (0, me * NA), (2, NA))
    mods = [_mm(a16, a['w_ada'][l], mode="nn", out_dtype=F32, name=f"ada_fwd{l}", bias=b_loc[l:l + 1], a_silu=True)
            for l in range(2)]
    gm = _all_gather(jnp.concatenate(mods, axis=0), "ag_mod").reshape(N_DEV, 2, 2 * N_DEV, NA)
    gm = jnp.transpose(gm, (1, 2, 0, 3)).reshape(2, 2 * N_DEV, 6 * D)
    mod_lat = [lax.dynamic_slice(gm[l], (me, 0), (1, 6 * D)).reshape(6, 1, 1, D) for l in range(2)]
    mod_ctx = [gm[l][N_DEV].reshape(6, 1, 1, D) for l in range(2)]

    def mod(l, k, both):
        return jnp.concatenate([mod_lat[l][k], mod_ctx[l][k]], axis=0) if both else mod_lat[l][k]

    def from_cols(g):
        return jnp.transpose(g, (1, 0, 2)).reshape(g.shape[1], -1)

    def from_rows(g):
        return g.reshape(-1, g.shape[2])

    def ag(x):
        return (x, False)

    def a2a(x):
        return (x, True)

    cos, sin = _rope_tables(T, Tc)
    n1g, n2g = a['norm1_g'], a['norm2_g']
    a_bst = a['a_b_s'][0].T
    mm = functools.partial(_mm)
    up_sh, dn_sh = bf(a['ffn_w_up']), bf(a['ffn_w_down'])

    w_abin = _all_gather(bf(a['ab_w_in'][0]), "ag_ab_w_in")
    x0 = jnp.concatenate([a['x'][0], a['ctx'][0]], axis=0)
    h1 = _normmod_fwd(x0, n1g[0:1], mod(0, 0, True), mod(0, 1, True), T, "l0_norm1")
    s1, s2 = D // 4, 5 * D // 8
    z, (g_about, g_up0a) = mm(h1, w_abin, mode="nn", out_dtype=F32, name="l0_ab_in", bias=a['ab_b_in'], b_dev=True,
                              carry=[ag(bf(a['ab_w_out'][0])), ag(up_sh[0][:s1])])
    w_about = from_rows(g_about)
    y = _gmlp_fwd(z, a['a_ln_g'], a['a_ln_b'], a['a_w_s'][0], a_bst, W, "l0_gmlp")
    (y, hc_b), (g_up0b,) = _conf_fwd(z, y, bcw, a['b_conv_b'], a['b_ln_g'], a['b_ln_b'], W, Wb, T, "l0_conf",
                                     carry=[ag(up_sh[0][s1:s2])])
    (x1, o1), (g_up0c,) = mm(y, w_about, mode="nn", out_dtype=F32, name="l0_ab_out", res=x0, gate=mod(0, 2, True),
                             seg_t=T, carry=[ag(up_sh[0][s2:])])
    w_up = [jnp.concatenate([g_up0a, g_up0b, g_up0c], axis=1), None]
    h2 = _normmod_fwd(x1, n2g[0:1], mod(0, 3, True), mod(0, 4, True), T, "l0_norm2")
    z2, (g_dn0,) = mm(h2, w_up[0], mode="nn", out_dtype=BF16, name="l0_up", b_dev=True, carry=[ag(dn_sh[0])])
    w_dn = [from_rows(g_dn0), None]
    a2, gc2 = _ffn_act_fwd(z2, fcw[0], a['ffn_conv_b'][0:1], T, "l0_act")
    (x2, o2), (g_in, g_uq) = mm(a2, w_dn[0], mode="nn", out_dtype=F32, name="l0_down", res=x1, gate=mod(0, 5, True),
                                seg_t=T, carry=[ag(bf(a['mla_w_in'][0])), ag(bf(a['mla_w_uq'][0]))])
    w_in = from_rows(g_in)
    w_in = jnp.concatenate([w_in, w_in[:, QL + KL:]], axis=1)
    w_uq = from_cols(g_uq).reshape(QL, H, NOPE + ROPE)
    w_uq = jnp.concatenate([w_uq[:, :, :NOPE].reshape(QL, HN), w_uq[:, :, NOPE:].reshape(QL, HR)], axis=1)

    h3 = _normmod_fwd(x2, n1g[1:2], mod(1, 0, True), mod(1, 1, True), T, "l1_norm1")
    z3, (g_ukv,) = mm(h3, w_in, mode="nn", out_dtype=F32, name="l1_mla_in", carry=[ag(bf(a['mla_w_ukv'][0]))])
    w_ukv = g_ukv
    cqn, ckvn, kpe = _mla_prep_fwd(z3, gq, gkv, cos, sin, QL, KL, "l1_prep")
    q, (g_wo,) = mm(cqn, w_uq, mode="nn", out_dtype=F32, name="l1_uq", rows=T, carry=[ag(bf(a['mla_w_o'][0]))])
    w_o = from_rows(g_wo)
    kv = mm(ckvn, w_ukv, mode="nn", out_dtype=BF16, name="l1_ukv", b_dev=True)
    qa = _qrope_fwd(q, cos, sin, HN, "l1_qrope")
    (o_att, lse), (g_up1,) = _attn_fwd(qa, kv, kpe, T, H, "l1_attn", carry=[ag(up_sh[1])])
    w_up[1] = g_up1
    x3, o3 = mm(o_att, w_o, mode="nn", out_dtype=F32, name="l1_wo", res=x2, gate=mod(1, 2, False), seg_t=T)
    h4 = _normmod_fwd(x3, n2g[1:2], mod(1, 3, False), mod(1, 4, False), T, "l1_norm2")
    z4, (g_dn1,) = mm(h4, w_up[1], mode="nn", out_dtype=BF16, name="l1_up", b_dev=True, carry=[ag(dn_sh[1])])
    w_dn[1] = from_rows(g_dn1)
    a4, gc4 = _ffn_act_fwd(z4, fcw[1], a['ffn_conv_b'][1:2], T, "l1_act")
    x4, o4 = mm(a4, w_dn[1], mode="nn", out_dtype=F32, name="l1_down", res=x3, gate=mod(1, 5, False), seg_t=T)

    dx4, loss_cols, d_fng, do4, dg2_1 = _final(x4, a['final_norm_g'][None], a['loss_target'][0], o4, mod(1, 5, False),
                                               "final")
    loss = lax.psum(jnp.sum(loss_cols), ("x", "y", "c"))

    def cols(dw):
        k, n = dw.shape
        return jnp.transpose(dw.reshape(k, N_DEV, n // N_DEV), (1, 0, 2))

    def rows(dw):
        return dw.reshape(N_DEV, dw.shape[0] // N_DEV, dw.shape[1])

    da4 = mm(do4, w_dn[1], mode="nt", out_dtype=BF16, name="l1_down_dx")
    dw_dn1 = mm(a4, do4, mode="tn", out_dtype=BF16, name="l1_down_dw")
    dz4, dfcw1, dfcb1 = _ffn_act_bwd(z4, gc4, da4, fcw[1], T, "l1_act_bwd")
    dw_up1, (r_dn1,) = mm(h4, dz4, mode="tn", out_dtype=BF16, name="l1_up_dw", out_dev=True, halves=True,
                          carry=[a2a(rows(dw_dn1))])
    dh4 = mm(dz4, w_up[1], mode="nt", out_dtype=F32, name="l1_up_dx", b_dev=True, halves=True)
    dx3, dn2g1, dsh2_1, dsc2_1, do3, dg1_1 = _normmod_bwd(x3, n2g[1:2], mod(1, 4, False), dh4, dx4, T, "l1_norm2_bwd",
                                                         o_prev=o3, gate_prev=mod(1, 2, False))
    d_oatt = mm(do3, w_o, mode="nt", out_dtype=BF16, name="l1_wo_dx")
    dw_o = mm(o_att, do3, mode="tn", out_dtype=BF16, name="l1_wo_dw")
    (dqa, dqpe, dkv, dkpe), (r_up1, r_wo) = _attn_bwd(qa, kv, kpe, d_oatt, o_att, lse, T, H, "l1_attn_bwd",
                                                      carry=[a2a(dw_up1), a2a(rows(dw_o))])
    dqa = _qrope_bwd(dqpe, dqa, cos, sin, HN, "l1_qrope_bwd")
    dcq = mm(dqa, w_uq, mode="nt", out_dtype=F32, name="l1_uq_dx")
    dw_uq = mm(cqn, dqa, mode="tn", out_dtype=BF16, name="l1_uq_dw", rows=T)
    dw_uq = jnp.concatenate([dw_uq[:, :HN].reshape(QL, H, NOPE), dw_uq[:, HN:].reshape(QL, H, ROPE)], axis=2)
    dw_uq = dw_uq.reshape(QL, H * (NOPE + ROPE))
    dckv = mm(dkv, w_ukv, mode="nt", out_dtype=F32, name="l1_ukv_dx", b_dev=True)
    dw_ukv = mm(ckvn, dkv, mode="tn", out_dtype=BF16, name="l1_ukv_dw", out_dev=True)
    dz3, dgq, dgkv = _mla_prep_bwd(z3, dcq, dckv, dkpe, gq, gkv, cos, sin, QL, KL, T, "l1_prep_bwd")
    dh3 = mm(dz3, w_in, mode="nt", out_dtype=F32, name="l1_mla_in_dx")
    dw_in = mm(h3, dz3, mode="tn", out_dtype=BF16, name="l1_mla_in_dw").astype(F32)
    dw_in = jnp.concatenate([dw_in[:, :QL + KL], dw_in[:, QL + KL:QL + KL + ROPE] + dw_in[:, QL + KL + ROPE:QL + KL + 2 * ROPE]],
                            axis=1).astype(BF16)
    dx2, dn1g1, dsh1_1, dsc1_1, do2, dg2_0 = _normmod_bwd(x2, n1g[1:2], mod(1, 1, True), dh3, dx3, T, "l1_norm1_bwd",
                                                         o_prev=o2, gate_prev=mod(0, 5, True))
    da2, (r_uq, r_ukv) = mm(do2, w_dn[0], mode="nt", out_dtype=BF16, name="l0_down_dx",
                            carry=[a2a(cols(dw_uq)), a2a(dw_ukv)])
    dw_dn0, (r_in,) = mm(a2, do2, mode="tn", out_dtype=BF16, name="l0_down_dw", carry=[a2a(rows(dw_in))])
    dz2, dfcw0, dfcb0 = _ffn_act_bwd(z2, gc2, da2, fcw[0], T, "l0_act_bwd")
    dw_up0, (r_dn0,) = mm(h2, dz2, mode="tn", out_dtype=BF16, name="l0_up_dw", out_dev=True, halves=True,
                          carry=[a2a(rows(dw_dn0))])
    dh2, (r_up0a,) = mm(dz2, w_up[0], mode="nt", out_dtype=F32, name="l0_up_dx", b_dev=True, halves=True,
                        carry=[a2a(dw_up0[:, :D // 2])])
    dx1, dn2g0, dsh2_0, dsc2_0, do1, dg1_0 = _normmod_bwd(x1, n2g[0:1], mod(0, 4, True), dh2, dx2, T, "l0_norm2_bwd",
                                                         o_prev=o1, gate_prev=mod(0, 2, True))
    dy = mm(do1, w_about, mode="nt", out_dtype=F32, name="l0_ab_out_dx")
    dw_about = mm(y, do1, mode="tn", out_dtype=BF16, name="l0_ab_out_dw")
    dz, dlag, dlab, dws, dbs, dbin_a = _gmlp_bwd(z, dy, a['a_ln_g'], a['a_ln_b'], a['a_w_s'][0], a_bst, W, "l0_gmlp_bwd")
    dhc, dlbg, dlbb, dbcb = _conf_bwd1(hc_b, dy, a['b_ln_g'], a['b_ln_b'], W, Wb, T, "l0_conf_bwd1")
    (dz, dbcw, dbin_b), (r_up0b,) = _conf_bwd2(z, dhc, dz, bcw, W, Wb, T, "l0_conf_bwd2",
                                               carry=[a2a(dw_up0[:, D // 2:])])
    dh1, (r_about,) = mm(dz, w_abin, mode="nt", out_dtype=F32, name="l0_ab_in_dx", b_dev=True,
                         carry=[a2a(rows(dw_about))])
    dw_abin_a = mm(h1, dz, mode="tn", out_dtype=BF16, name="l0_ab_in_dw_a", out_dev=True, p_range=(0, D // 2))
    dw_abin_b, (r_abin_a,) = mm(h1, dz, mode="tn", out_dtype=BF16, name="l0_ab_in_dw_b", out_dev=True,
                                p_range=(D // 2, D // 2), carry=[a2a(dw_abin_a)])
    dx0, dn1g0, dsh1_0, dsc1_0 = _normmod_bwd(x0, n1g[0:1], mod(0, 1, True), dh1, dx1, T, "l0_norm1_bwd")

    zero = jnp.zeros((D,), F32)
    dmod = jnp.stack([
        jnp.stack([jnp.stack([dsh1_0[0, 0], dsc1_0[0, 0], dg1_0[0, 0], dsh2_0[0, 0], dsc2_0[0, 0], dg2_0[0, 0]]),
                   jnp.stack([dsh1_0[1, 0], dsc1_0[1, 0], dg1_0[1, 0], dsh2_0[1, 0], dsc2_0[1, 0], dg2_0[1, 0]])]),
        jnp.stack([jnp.stack([dsh1_1[0, 0], dsc1_1[0, 0], dg1_1[0, 0], dsh2_1[0, 0], dsc2_1[0, 0], dg2_1[0, 0]]),
                   jnp.stack([dsh1_1[1, 0], dsc1_1[1, 0], zero, zero, zero, zero])])])
    small = {
        'norm1_g': jnp.concatenate([dn1g0, dn1g1], axis=0), 'norm2_g': jnp.concatenate([dn2g0, dn2g1], axis=0),
        'ab_b_in': jnp.concatenate([dbin_a, dbin_b], axis=1), 'a_ln_g': dlag, 'a_ln_b': dlab, 'a_w_s': dws[None],
        'a_b_s': jnp.sum(dbs, axis=-1)[None], 'b_conv_w': dbcw, 'b_conv_b': dbcb, 'b_ln_g': dlbg, 'b_ln_b': dlbb,
        'mla_q_norm_g': dgq, 'mla_kv_norm_g': dgkv, 'ffn_conv_w': jnp.stack([dfcw0, dfcw1]),
        'ffn_conv_b': jnp.concatenate([dfcb0, dfcb1], axis=0), 'final_norm_g': d_fng[0],
    }
    names = list(small)
    g2 = _all_gather(_pack([dmod] + [small[n] for n in names]), "ag_small_grads")
    red = _sum_lead(g2, "sum_small_grads").reshape(-1)
    red = dict(zip(names, _unpack(red, [(2, 2, 6, D)] + [small[n].shape for n in names])[1:]))
    dmod_all = g2.reshape(N_DEV, -1)[:, :2 * 2 * 6 * D].reshape(N_DEV, 2, 2, 6 * D)

    a16g = jnp.concatenate([c_all, jnp.tile(a['c_ctx'][None], (N_DEV, 1))], axis=0)
    dm_loc = lax.dynamic_slice(dmod_all, (0, 0, 0, me * NA), (N_DEV, 2, 2, NA))
    g_wada, cpart = [], []
    for l in range(2):
        dm16 = jnp.concatenate([dm_loc[:, l, 0], dm_loc[:, l, 1]], axis=0)
        g_wada.append(mm(a16g, dm16, mode="tn", out_dtype=F32, name=f"ada_dw{l}", a_silu=True))
        cpart.append(mm(dm_loc[:, l, 1], a['w_ada'][l], mode="nt", out_dtype=F32, name=f"ada_dc{l}"))
    g_bada = _sum_lead(jnp.transpose(dmod_all, (0, 2, 1, 3)).reshape(2 * N_DEV, 2 * 6 * D // 128, 128), "sum_b_ada")
    g_cc = _all_gather(jnp.concatenate(cpart, axis=0), "ag_c_ctx")
    g_cc = _sum_lead(g_cc.reshape(2 * N_DEV * N_DEV, D // 128, 128), "sum_c_ctx").reshape(D)
    grads = {
        'c_ctx': g_cc * _dsilu(a['c_ctx']), 'w_ada': jnp.stack(g_wada), 'b_ada': g_bada.reshape(2, 6 * D),
        'b_conv_w': lax.dynamic_slice(red['b_conv_w'], (0, me * (Wb // N_DEV)), (kw, Wb // N_DEV))[None],
        'ffn_conv_w': lax.dynamic_slice(red['ffn_conv_w'], (0, 0, me * (Fd // N_DEV)), (2, 3, Fd // N_DEV)),
        'mla_q_norm_g': lax.dynamic_slice(red['mla_q_norm_g'], (0, me * (QL // N_DEV)), (1, QL // N_DEV)),
        'mla_kv_norm_g': lax.dynamic_slice(red['mla_kv_norm_g'], (0, me * (KL // N_DEV)), (1, KL // N_DEV)),
    }
    for n in names:
        if n not in grads:
            grads[n] = red[n].reshape(a[n].shape)

    recvs = {'ab_w_out': [r_about], 'mla_w_in': [r_in], 'mla_w_uq': [r_uq], 'mla_w_ukv': [r_ukv],
             'mla_w_o': [r_wo], 'ffn_w_up': [r_up0a, r_up0b, r_up1], 'ffn_w_down': [r_dn0, r_dn1]}
    out = {}
    for n in WEIGHTS:
        shp = a[n].shape
        w2 = a[n].reshape(-1, shp[-1])
        m2, v2 = a['m_' + n].reshape(w2.shape), a['v_' + n].reshape(w2.shape)
        if n in recvs:
            res = _adamw(w2, m2, v2, "adamw_" + n, recv=recvs[n])
        elif n == 'w_ada':
            g2d = grads[n].reshape(w2.shape)
            res, (r_abin_b,) = _adamw(w2, m2, v2, "adamw_" + n, g=g2d, carry=[a2a(dw_abin_b)])
            res = (g2d,) + tuple(res)
            recvs['ab_w_in'] = [r_abin_a, r_abin_b]
        else:
            g2d = grads[n].reshape(w2.shape)
            res = (g2d,) + tuple(_adamw(w2, m2, v2, "adamw_" + n, g=g2d))
        out[n] = [r.reshape(shp) for r in res]
    return (loss, dx0[:T][None], *[out[n][0] for n in WEIGHTS], *[out[n][1] for n in WEIGHTS],
            *[out[n][2] for n in WEIGHTS], *[out[n][3] for n in WEIGHTS])


def kernel(x, c, ctx, c_ctx, norm1_g, norm2_g, w_ada, b_ada, ab_w_in, ab_b_in, a_ln_g, a_ln_b, a_w_s, a_b_s, b_conv_w, b_conv_b, b_ln_g, b_ln_b, ab_w_out, mla_w_in, mla_q_norm_g, mla_w_uq, mla_kv_norm_g, mla_w_ukv, mla_w_o, ffn_w_up, ffn_conv_w, ffn_conv_b, ffn_w_down, final_norm_g, loss_target, m_c_ctx, m_norm1_g, m_norm2_g, m_w_ada, m_b_ada, m_ab_w_in, m_ab_b_in, m_a_ln_g, m_a_ln_b, m_a_w_s, m_a_b_s, m_b_conv_w, m_b_conv_b, m_b_ln_g, m_b_ln_b, m_ab_w_out, m_mla_w_in, m_mla_q_norm_g, m_mla_w_uq, m_mla_kv_norm_g, m_mla_w_ukv, m_mla_w_o, m_ffn_w_up, m_ffn_conv_w, m_ffn_conv_b, m_ffn_w_down, m_final_norm_g, v_c_ctx, v_norm1_g, v_norm2_g, v_w_ada, v_b_ada, v_ab_w_in, v_ab_b_in, v_a_ln_g, v_a_ln_b, v_a_w_s, v_a_b_s, v_b_conv_w, v_b_conv_b, v_b_ln_g, v_b_ln_b, v_ab_w_out, v_mla_w_in, v_mla_q_norm_g, v_mla_w_uq, v_mla_kv_norm_g, v_mla_w_ukv, v_mla_w_o, v_ffn_w_up, v_ffn_conv_w, v_ffn_conv_b, v_ffn_w_down, v_final_norm_g):
    return _step(dict(locals()))
```

```python
import functools
import math

import jax
import jax.numpy as jnp
from jax import lax
from jax.experimental import pallas as pl
from jax.experimental.pallas import tpu as pltpu

F32 = jnp.float32
BF16 = jnp.bfloat16
SDS = jax.ShapeDtypeStruct

N_DEV = 8
EPS = 1e-6
CHUNK = 128
NOPE = 128
ROPE = 64
VDIM = 128
GRID_W = 64
ROPE_THETA = 10000.0
HALO = 16
ADAM_LR, ADAM_B1, ADAM_B2, ADAM_EPS, ADAM_WD, ADAM_STEP = 0.001, 0.9, 0.999, 1e-08, 0.01, 10
VMEM_LIMIT = 56 * 1024 * 1024


def _tile(n, prefs):
    for p in prefs:
        if n % p == 0:
            return p
    return n


def _params(sem, vmem=VMEM_LIMIT):
    return pltpu.CompilerParams(dimension_semantics=sem, vmem_limit_bytes=vmem)


def _sigmoid(x):
    return 0.5 * jnp.tanh(0.5 * x) + 0.5


def _silu(x):
    return x * _sigmoid(x)


def _dsilu(x):
    s = _sigmoid(x)
    return s * (1.0 + x * (1.0 - s))


_GELU_C = math.sqrt(2.0 / math.pi)


def _gelu(x):
    return 0.5 * x * (1.0 + jnp.tanh(_GELU_C * (x + 0.044715 * x * x * x)))


def _dgelu(x):
    t = jnp.tanh(_GELU_C * (x + 0.044715 * x * x * x))
    return 0.5 * (1.0 + t) + 0.5 * x * (1.0 - t * t) * _GELU_C * (1.0 + 3.0 * 0.044715 * x * x)


def _colsum(v):
    return jnp.sum(v, axis=0, keepdims=True)


_SIBLING = 1
_CHIPS = (2, 4, 6)


def _xchg(x_ref, o_ref, send_sems, recv_sems, local_sem, scatter):
    ax, ay, ac = lax.axis_index("x"), lax.axis_index("y"), lax.axis_index("c")
    me = 4 * ax + 2 * ay + ac

    def dev(k):
        return ax ^ (k >> 2), ay ^ ((k >> 1) & 1), ac ^ (k & 1)

    def idx(k):
        px, py, pc = dev(k)
        return 4 * px + 2 * py + pc

    def copy(k, src, dst, to):
        return pltpu.make_async_remote_copy(src_ref=src, dst_ref=dst, send_sem=send_sems.at[k - 1],
                                            recv_sem=recv_sems.at[k - 1], device_id=dev(to),
                                            device_id_type=pl.DeviceIdType.MESH)

    def own():
        return pltpu.make_async_copy(x_ref.at[me] if scatter else x_ref, o_ref.at[me], local_sem)

    def sends():
        if scatter:
            return [copy(k, x_ref.at[idx(k)], o_ref.at[me], k) for k in range(1, N_DEV)]
        return [copy(k, x_ref, o_ref.at[me], k) for k in (_SIBLING,) + _CHIPS]

    def forwards():
        return [] if scatter else [copy(j + 1, o_ref.at[idx(j)], o_ref.at[idx(j)], _SIBLING) for j in _CHIPS]

    def arrival(k):
        return copy(k, o_ref.at[idx(k)], o_ref.at[idx(k)], k)

    return own, sends, forwards, arrival


def _xchg_start(*refs, scatter):
    own, sends, _, _ = _xchg(*refs, scatter)
    own().start()
    for cp in sends():
        cp.start()


def _xchg_forward(*refs, scatter):
    _, _, forwards, arrival = _xchg(*refs, scatter)
    if not scatter:
        for j, fw in zip(_CHIPS, forwards()):
            arrival(j).wait_recv()
            fw.start()


def _xchg_finish(*refs, scatter):
    own, sends, forwards, arrival = _xchg(*refs, scatter)
    for k in range(1, N_DEV):
        if scatter or k not in _CHIPS:
            arrival(k).wait_recv()
    for cp in sends() + forwards():
        cp.wait_send()
    own().wait()


_XCHG_SEMS = [pltpu.SemaphoreType.DMA((N_DEV - 1,)), pltpu.SemaphoreType.DMA((N_DEV - 1,)), pltpu.SemaphoreType.DMA]


def _xchg_shape(x, scatter):
    return SDS((N_DEV,) + tuple(x.shape[1:] if scatter else x.shape), x.dtype)


def _exchange(x, *, scatter, name):
    def body(*refs):
        _xchg_start(*refs, scatter=scatter)
        _xchg_forward(*refs, scatter=scatter)
        _xchg_finish(*refs, scatter=scatter)

    return pl.pallas_call(
        body, out_shape=_xchg_shape(x, scatter),
        in_specs=[pl.BlockSpec(memory_space=pl.ANY)], out_specs=pl.BlockSpec(memory_space=pl.ANY),
        scratch_shapes=list(_XCHG_SEMS), name=name)(x)


def _carried(body, carry, n_in, n_out, n_scratch, grid):
    nc = len(carry)
    total = math.prod(grid)
    mid = (3 * total) // 4

    def wrapped(*refs):
        ins, cin = refs[:n_in], refs[n_in:n_in + nc]
        o0 = n_in + nc
        outs, cout = refs[o0:o0 + n_out], refs[o0 + n_out:o0 + n_out + nc]
        scr = refs[o0 + n_out + nc:]
        sems = scr[n_scratch:]
        step = pl.program_id(0)
        for ax in range(1, len(grid)):
            step = step * grid[ax] + pl.program_id(ax)

        def each(fn):
            for c in range(nc):
                fn(cin[c], cout[c], *sems[3 * c:3 * c + 3], scatter=carry[c][1])

        @pl.when(step == 0)
        def _():
            each(_xchg_start)

        body(*ins, *outs, *scr[:n_scratch])

        if mid < total - 1:
            @pl.when(step == mid)
            def _():
                each(_xchg_forward)

        @pl.when(step == total - 1)
        def _():
            if mid >= total - 1:
                each(_xchg_forward)
            each(_xchg_finish)

    return wrapped


def _carry_call(body, carry, *, grid, out_shape, in_specs, out_specs, scratch_shapes, sem, name, ins, aliases=None):
    carry = carry or []
    nc = len(carry)
    if nc:
        body = _carried(body, carry, len(in_specs), len(out_shape), len(scratch_shapes), grid)
        anyspec = pl.BlockSpec(memory_space=pl.ANY)
        in_specs = list(in_specs) + [anyspec] * nc
        out_specs = list(out_specs) + [anyspec] * nc
        out_shape = list(out_shape) + [_xchg_shape(x, sc) for x, sc in carry]
        scratch_shapes = list(scratch_shapes) + list(_XCHG_SEMS) * nc
        ins = list(ins) + [x for x, _ in carry]
        sem = ("arbitrary",) * len(grid)
    out = pl.pallas_call(body, out_shape=out_shape, grid=grid, in_specs=in_specs, out_specs=out_specs,
                         scratch_shapes=scratch_shapes, compiler_params=_params(sem), name=name,
                         input_output_aliases=aliases or {})(*ins)
    n_main = len(out) - nc
    return list(out[:n_main]), list(out[n_main:])


def _all_gather(x, name):
    return _exchange(x, scatter=False, name=name)


def _all_to_all(x, name):
    return _exchange(x, scatter=True, name=name)


def _sum_lead(x, name):
    n, R, C = x.shape
    tr = _tile(R, (512, 256, 128, 64, 32, 16, 8))

    def body(x_ref, o_ref):
        acc = x_ref[0]
        for d in range(1, n):
            acc = acc + x_ref[d]
        o_ref[...] = acc

    return pl.pallas_call(
        body, out_shape=SDS((R, C), F32), grid=(R // tr,),
        in_specs=[pl.BlockSpec((n, tr, C), lambda i: (0, i, 0))], out_specs=pl.BlockSpec((tr, C), lambda i: (i, 0)),
        compiler_params=_params(("parallel",)), name=name)(x)


_TP = (1408, 1088, 1024, 768, 512, 256, 128)
_TQ = (1408, 1024, 768, 512, 256, 128)
_TR = (2048, 1408, 1024, 768, 512, 256, 128)
_TR_TN = (2176, 2048, 1088, 1024, 512, 256, 128)


def _mm(a, b, *, mode, out_dtype, name, rows=None, bias=None, res=None, gate=None, seg_t=None, a_silu=False, carry=None,
        b_dev=False, out_dev=False, p_range=None, halves=False):
    if mode == "nn":
        P, R, Q = rows or a.shape[0], a.shape[1], (b.shape[0] * b.shape[2] if b_dev else b.shape[1])
    elif mode == "nt":
        P, R, Q = rows or a.shape[-2], (2 * a.shape[2] if halves else a.shape[1]), (b.shape[1] if b_dev else b.shape[0])
    else:
        R, P, Q = rows or a.shape[0], a.shape[1], (2 * b.shape[2] if halves else b.shape[1])
    p0 = 0
    if p_range is not None:
        p0, P = p_range
    tp = _tile(P, _TP)
    tq = _tile(Q // N_DEV if (out_dev or (b_dev and mode == "nn")) else Q, _TQ)
    tr = _tile(R // N_DEV if (b_dev and mode == "nt") else R, _TR if mode != "tn" else _TR_TN)
    nk = R // tr
    qd = (Q // N_DEV) // tq
    rd = (R // N_DEV) // tr
    if mode == "nn":
        a_spec = pl.BlockSpec((tp, tr), lambda i, j, k: (i, k))
        b_spec = (pl.BlockSpec((None, tr, tq), lambda i, j, k: (j // qd, k, j % qd)) if b_dev
                  else pl.BlockSpec((tr, tq), lambda i, j, k: (k, j)))
        dims = (((1,), (0,)), ((), ()))
    elif mode == "nt":
        kh = (R // 2) // tr
        a_spec = (pl.BlockSpec((None, tp, tr), lambda i, j, k: (k // kh, i, k % kh)) if halves
                  else pl.BlockSpec((tp, tr), lambda i, j, k: (i, k)))
        b_spec = (pl.BlockSpec((None, tq, tr), lambda i, j, k: (k // rd, j, k % rd)) if b_dev
                  else pl.BlockSpec((tq, tr), lambda i, j, k: (j, k)))
        dims = (((1,), (1,)), ((), ()))
    else:
        pb = p0 // tp
        qh = (Q // 2) // tq
        a_spec = pl.BlockSpec((tr, tp), lambda i, j, k: (k, i + pb))
        b_spec = (pl.BlockSpec((None, tr, tq), lambda i, j, k: (j // qh, k, j % qh)) if halves
                  else pl.BlockSpec((tr, tq), lambda i, j, k: (k, j)))
        dims = (((0,), (0,)), ((), ()))
    ins, in_specs = [a, b], [a_spec, b_spec]
    if bias is not None:
        ins.append(bias)
        in_specs.append(pl.BlockSpec((1, tq), lambda i, j, k: (0, j)))
    gated = res is not None
    if gated:
        n_seg = gate.shape[0]
        ins += [res, gate]
        in_specs += [pl.BlockSpec((tp, tq), lambda i, j, k: (i, j)),
                     pl.BlockSpec((n_seg, 1, tq), lambda i, j, k: (0, 0, j))]
    if out_dev:
        out_shape = [SDS((N_DEV, P, Q // N_DEV), out_dtype)]
        out_specs = [pl.BlockSpec((None, tp, tq), lambda i, j, k: (j // qd, i, j % qd))]
    else:
        out_shape = [SDS((P, Q), out_dtype)]
        out_specs = [pl.BlockSpec((tp, tq), lambda i, j, k: (i, j))]
    if gated:
        out_shape.append(SDS((P, Q), BF16))
        out_specs.append(pl.BlockSpec((tp, tq), lambda i, j, k: (i, j)))

    def body(*refs):
        a_ref, b_ref = refs[0], refs[1]
        pos = 2
        bias_ref = res_ref = gate_ref = o2_ref = None
        if bias is not None:
            bias_ref = refs[pos]
            pos += 1
        if gated:
            res_ref, gate_ref = refs[pos], refs[pos + 1]
            pos += 2
        o_ref = refs[pos]
        pos += 1
        if gated:
            o2_ref = refs[pos]
            pos += 1
        acc_ref = refs[pos] if nk > 1 else None
        k = pl.program_id(2)
        av = a_ref[...]
        if a_silu:
            av = _silu(av.astype(F32))
        part = lax.dot_general(av.astype(BF16), b_ref[...].astype(BF16), dims, preferred_element_type=F32)
        if nk > 1:
            @pl.when(k == 0)
            def _():
                acc_ref[...] = part

            @pl.when(k > 0)
            def _():
                acc_ref[...] += part

        @pl.when(k == nk - 1)
        def _():
            acc = acc_ref[...] if nk > 1 else part
            if bias_ref is not None:
                acc = acc + bias_ref[...]
            if gated:
                if n_seg == 1:
                    g = gate_ref[0]
                else:
                    row = pl.program_id(0) * tp + lax.broadcasted_iota(jnp.int32, (tp, 1), 0)
                    g = jnp.where(row < seg_t, gate_ref[0], gate_ref[1])
                o_ref[...] = (res_ref[...] + g * acc).astype(o_ref.dtype)
                o2_ref[...] = acc.astype(BF16)
            else:
                o_ref[...] = acc.astype(o_ref.dtype)

    out, carried = _carry_call(
        body, carry, grid=(P // tp, Q // tq, nk), out_shape=out_shape, in_specs=in_specs, out_specs=out_specs,
        scratch_shapes=[pltpu.VMEM((tp, tq), F32)] if nk > 1 else [], sem=("parallel", "parallel", "arbitrary"),
        name=name, ins=ins)
    res_out = tuple(out) if gated else out[0]
    return (res_out, carried) if carry else res_out


def _row_tile(seg_t, m):
    return 256 if (seg_t % 256 == 0 and m % 256 == 0) else 128


def _normmod_fwd(x, g, sh, sc, seg_t, name):
    M, D = x.shape
    tm = _row_tile(seg_t, M)
    n_seg = sh.shape[0]
    nt = seg_t // tm

    def seg(i):
        return ((i >= nt).astype(jnp.int32) if n_seg == 2 else 0, 0, 0)

    def body(x_ref, g_ref, sh_ref, sc_ref, o_ref):
        xv = x_ref[...]
        r = lax.rsqrt(jnp.mean(xv * xv, axis=-1, keepdims=True) + EPS)
        y = xv * r * g_ref[...]
        o_ref[...] = (y * (1.0 + sc_ref[0]) + sh_ref[0]).astype(BF16)

    return pl.pallas_call(
        body, out_shape=SDS((M, D), BF16), grid=(M // tm,),
        in_specs=[pl.BlockSpec((tm, D), lambda i: (i, 0)), pl.BlockSpec((1, D), lambda i: (0, 0)),
                  pl.BlockSpec((1, 1, D), seg), pl.BlockSpec((1, 1, D), seg)],
        out_specs=pl.BlockSpec((tm, D), lambda i: (i, 0)),
        compiler_params=_params(("parallel",)), name=name)(x, g, sh, sc)


def _normmod_bwd(x, g, sc, dh, dx_in, seg_t, name, o_prev=None, gate_prev=None):
    M, D = x.shape
    tm = _row_tile(seg_t, M)
    n_seg = sc.shape[0]
    nt = seg_t // tm
    n_in = dx_in.shape[0] // tm
    with_prev = o_prev is not None
    n_segp = gate_prev.shape[0] if with_prev else 0

    def seg(i):
        return ((i >= nt).astype(jnp.int32) if n_seg == 2 else 0, 0, 0)

    def segp(i):
        return ((i >= nt).astype(jnp.int32) if n_segp == 2 else 0, 0, 0)

    def body(*refs):
        x_ref, g_ref, sc_ref, dh_ref, dxin_ref = refs[:5]
        pos = 5
        if with_prev:
            op_ref, gp_ref = refs[5], refs[6]
            pos = 7
        dx_ref, dg_ref, dsh_ref, dsc_ref = refs[pos:pos + 4]
        if with_prev:
            dop_ref, dgp_ref = refs[pos + 4], refs[pos + 5]
        i = pl.program_id(0)
        xv = x_ref[...]
        r = lax.rsqrt(jnp.mean(xv * xv, axis=-1, keepdims=True) + EPS)
        xh = xv * r
        gv = g_ref[...]
        dhv = dh_ref[...].astype(F32)
        dy = dhv * (1.0 + sc_ref[0])
        dxh = dy * gv
        dxv = r * (dxh - xh * jnp.mean(dxh * xh, axis=-1, keepdims=True))
        if n_in * tm < M:
            dxv = dxv + jnp.where(i < n_in, dxin_ref[...], 0.0)
        else:
            dxv = dxv + dxin_ref[...]
        dx_ref[...] = dxv

        @pl.when(i == 0)
        def _():
            dg_ref[...] = jnp.zeros_like(dg_ref)

        first_of_seg = (i == 0) | (i == nt) if n_seg == 2 else (i == 0)

        @pl.when(first_of_seg)
        def _():
            dsh_ref[...] = jnp.zeros_like(dsh_ref)
            dsc_ref[...] = jnp.zeros_like(dsc_ref)

        dg_ref[...] += _colsum(dy * xh)
        dsh_ref[0] += _colsum(dhv)
        dsc_ref[0] += _colsum(dhv * xh * gv)
        if with_prev:
            first_of_segp = (i == 0) | (i == nt) if n_segp == 2 else (i == 0)

            @pl.when(first_of_segp)
            def _():
                dgp_ref[...] = jnp.zeros_like(dgp_ref)

            dop_ref[...] = (gp_ref[0] * dxv).astype(BF16)
            dgp_ref[0] += _colsum(dxv * op_ref[...].astype(F32))

    row = pl.BlockSpec((tm, D), lambda i: (i, 0))
    ins = [x, g, sc, dh, dx_in]
    in_specs = [row, pl.BlockSpec((1, D), lambda i: (0, 0)), pl.BlockSpec((1, 1, D), seg), row,
                pl.BlockSpec((tm, D), lambda i: (jnp.minimum(i, n_in - 1), 0))]
    out_shape = [SDS((M, D), F32), SDS((1, D), F32), SDS((n_seg, 1, D), F32), SDS((n_seg, 1, D), F32)]
    out_specs = [row, pl.BlockSpec((1, D), lambda i: (0, 0)), pl.BlockSpec((1, 1, D), seg), pl.BlockSpec((1, 1, D), seg)]
    if with_prev:
        ins += [o_prev, gate_prev]
        in_specs += [row, pl.BlockSpec((1, 1, D), segp)]
        out_shape += [SDS((M, D), BF16), SDS((n_segp, 1, D), F32)]
        out_specs += [row, pl.BlockSpec((1, 1, D), segp)]
    return pl.pallas_call(
        body, out_shape=out_shape, grid=(M // tm,), in_specs=in_specs, out_specs=out_specs,
        compiler_params=_params(("arbitrary",)), name=name)(*ins)


def _final(x, g, target, o_prev, gate_prev, name):
    T, D = x.shape
    tm = _tile(T, (256, 128))

    def body(x_ref, g_ref, t_ref, op_ref, gp_ref, dx_ref, loss_ref, dg_ref, dop_ref, dgp_ref):
        i = pl.program_id(0)
        xv = x_ref[...]
        r = lax.rsqrt(jnp.mean(xv * xv, axis=-1, keepdims=True) + EPS)
        xh = xv * r
        gv = g_ref[...]
        e = xh * gv - t_ref[...]
        dout = e * (1.0 / D)
        dxh = dout * gv
        dxv = r * (dxh - xh * jnp.mean(dxh * xh, axis=-1, keepdims=True))
        dx_ref[...] = dxv
        dop_ref[...] = (gp_ref[0] * dxv).astype(BF16)

        @pl.when(i == 0)
        def _():
            loss_ref[...] = jnp.zeros_like(loss_ref)
            dg_ref[...] = jnp.zeros_like(dg_ref)
            dgp_ref[...] = jnp.zeros_like(dgp_ref)

        loss_ref[...] += _colsum(e * e) * (0.5 / D)
        dg_ref[...] += _colsum(dout * xh)
        dgp_ref[0] += _colsum(dxv * op_ref[...].astype(F32))

    row = pl.BlockSpec((tm, D), lambda i: (i, 0))
    vec = pl.BlockSpec((1, D), lambda i: (0, 0))
    vec3 = pl.BlockSpec((1, 1, D), lambda i: (0, 0, 0))
    return pl.pallas_call(
        body, out_shape=[SDS((T, D), F32), SDS((1, D), F32), SDS((1, D), F32), SDS((T, D), BF16), SDS((1, 1, D), F32)],
        grid=(T // tm,), in_specs=[row, vec, row, row, vec3], out_specs=[row, vec, vec, row, vec3],
        compiler_params=_params(("arbitrary",)), name=name)(x, g, target, o_prev, gate_prev)


def _gmlp_core(z, lg, lb, ws_ref, bst):
    W = z.shape[1] // 2
    t = _gelu(z)
    u, v = t[:, :W], t[:, W:]
    mu = jnp.mean(v, axis=-1, keepdims=True)
    vc = v - mu
    rstd = lax.rsqrt(jnp.mean(vc * vc, axis=-1, keepdims=True) + EPS)
    vhat = vc * rstd
    vn = vhat * lg + lb
    vp = []
    for h in range(W // CHUNK):
        blk = vn[:, h * CHUNK:(h + 1) * CHUNK].astype(BF16)
        vp.append(jnp.dot(ws_ref[h].astype(BF16), blk, preferred_element_type=F32) + bst[:, h:h + 1])
    return u, vhat, rstd, vp


def _gmlp_fwd(z, ln_g, ln_b, w_s, b_st, W, name):
    M = z.shape[0]
    H = W // CHUNK

    def body(z_ref, lg_ref, lb_ref, ws_ref, bst_ref, o_ref):
        u, _, _, vp = _gmlp_core(z_ref[...], lg_ref[...], lb_ref[...], ws_ref, bst_ref[...])
        for h in range(H):
            o_ref[:, h * CHUNK:(h + 1) * CHUNK] = (u[:, h * CHUNK:(h + 1) * CHUNK] * vp[h]).astype(BF16)

    vec = pl.BlockSpec((1, W), lambda i: (0, 0))
    return pl.pallas_call(
        body, out_shape=SDS((M, 2 * W), BF16), grid=(M // CHUNK,),
        in_specs=[pl.BlockSpec((CHUNK, 2 * W), lambda i: (i, 0)), vec, vec,
                  pl.BlockSpec((H, CHUNK, CHUNK), lambda i: (0, 0, 0)), pl.BlockSpec((CHUNK, H), lambda i: (0, 0))],
        out_specs=pl.BlockSpec((CHUNK, W), lambda i: (i, 0)),
        compiler_params=_params(("parallel",)), name=name)(z, ln_g, ln_b, w_s, b_st)


def _gmlp_bwd(z, dy, ln_g, ln_b, w_s, b_st, W, name):
    M = z.shape[0]
    H = W // CHUNK
    ZW = z.shape[1]

    def body(z_ref, dy_ref, lg_ref, lb_ref, ws_ref, bst_ref, dz_ref, dlg_ref, dlb_ref, dws_ref, dbs_ref, dbin_ref):
        i = pl.program_id(0)

        @pl.when(i == 0)
        def _():
            for r in (dlg_ref, dlb_ref, dws_ref, dbs_ref, dbin_ref):
                r[...] = jnp.zeros_like(r)

        zv = z_ref[...]
        lg = lg_ref[...]
        u, vhat, rstd, vp = _gmlp_core(zv, lg, lb_ref[...], ws_ref, bst_ref[...])
        vn = vhat * lg + lb_ref[...]
        dya = dy_ref[...]
        du_parts, dvn_parts = [], []
        for h in range(H):
            sl = slice(h * CHUNK, (h + 1) * CHUNK)
            dya_h = dya[:, sl]
            du_parts.append(dya_h * vp[h])
            dvp = dya_h * u[:, sl]
            dbs_ref[h] += dvp
            dvp16 = dvp.astype(BF16)
            dws_ref[h] += lax.dot_general(dvp16, vn[:, sl].astype(BF16), (((1,), (1,)), ((), ())),
                                          preferred_element_type=F32)
            dvn_parts.append(lax.dot_general(ws_ref[h].astype(BF16), dvp16, (((0,), (0,)), ((), ())),
                                             preferred_element_type=F32))
        du = jnp.concatenate(du_parts, axis=1)
        dvn = jnp.concatenate(dvn_parts, axis=1)
        dlg_ref[...] += _colsum(dvn * vhat)
        dlb_ref[...] += _colsum(dvn)
        dvh = dvn * lg
        dv = rstd * (dvh - jnp.mean(dvh, axis=-1, keepdims=True) - vhat * jnp.mean(dvh * vhat, axis=-1, keepdims=True))
        dz = jnp.concatenate([du, dv], axis=1) * _dgelu(zv)
        dbin_ref[...] += _colsum(dz)
        dz_ref[...] = dz.astype(BF16)

    vec = pl.BlockSpec((1, W), lambda i: (0, 0))
    mat = pl.BlockSpec((H, CHUNK, CHUNK), lambda i: (0, 0, 0))
    return pl.pallas_call(
        body,
        out_shape=[SDS((M, ZW), BF16), SDS((1, W), F32), SDS((1, W), F32), SDS((H, CHUNK, CHUNK), F32),
                   SDS((H, CHUNK, CHUNK), F32), SDS((1, 2 * W), F32)],
        grid=(M // CHUNK,),
        in_specs=[pl.BlockSpec((CHUNK, 2 * W), lambda i: (i, 0)), pl.BlockSpec((CHUNK, W), lambda i: (i, 0)), vec, vec,
                  mat, pl.BlockSpec((CHUNK, H), lambda i: (0, 0))],
        out_specs=[pl.BlockSpec((CHUNK, 2 * W), lambda i: (i, 0)), vec, vec, mat, mat,
                   pl.BlockSpec((1, 2 * W), lambda i: (0, 0))],
        compiler_params=_params(("arbitrary",)), name=name)(z, dy, ln_g, ln_b, w_s, b_st)


def _halo_specs(tm, width, col, n_rows):
    per = tm // HALO
    last = n_rows // HALO - 1
    prev = pl.BlockSpec((HALO, width), lambda i: (jnp.maximum(i * per - 1, 0), col))
    nxt = pl.BlockSpec((HALO, width), lambda i: (jnp.minimum((i + 1) * per, last), col))
    return prev, nxt


def _edge_flags(i, tm, seg_t, m):
    r0 = i * tm
    has_prev = jnp.where((r0 == 0) | (r0 == seg_t), 0.0, 1.0)
    has_next = jnp.where((r0 + tm == seg_t) | (r0 + tm == m), 0.0, 1.0)
    return has_prev, has_next


def _glu(zz, wb):
    return zz[:, :wb] * _sigmoid(zz[:, wb:])


def _build_shifts(src_ref, sh_ref):
    n = src_ref.shape[0] - 8
    for r in range(1, 8):
        sh_ref[r - 1, pl.ds(0, n), :] = src_ref[pl.ds(r, n), :]


def _shifted(src_ref, sh_ref, off, r0, rc):
    a, r = divmod(off, 8)
    if r == 0:
        return src_ref[pl.ds(8 * a + r0, rc), :]
    return sh_ref[r - 1, pl.ds(8 * a + r0, rc), :]


def _conv_taps(src_ref, sh_ref, w_ref, first, tm, kw, flip=False):
    rc = 32
    parts = []
    for c in range(tm // rc):
        acc = None
        for k in range(kw):
            wk = w_ref[pl.ds(kw - 1 - k if flip else k, 1), :]
            term = _shifted(src_ref, sh_ref, first + k, c * rc, rc) * wk
            acc = term if acc is None else acc + term
        parts.append(acc)
    return jnp.concatenate(parts, axis=0)


def _conf_fwd(z, y, conv_w, conv_b, ln_g, ln_b, W, Wb, seg_t, name, carry=None):
    M = z.shape[0]
    tm = _row_tile(seg_t, M)
    kw = conv_w.shape[0]
    pad = (kw - 1) // 2
    col = (2 * W) // (2 * Wb)

    def body(zc_ref, zp_ref, zn_ref, y_hbm, cw_ref, cb_ref, lg_ref, lb_ref, o_ref, hc_ref, hs_ref, sh_ref):
        del y_hbm
        hp, hn = _edge_flags(pl.program_id(0), tm, seg_t, M)
        hs_ref[pl.ds(0, HALO), :] = _glu(zp_ref[...], Wb) * hp
        hs_ref[pl.ds(HALO, tm), :] = _glu(zc_ref[...], Wb)
        hs_ref[pl.ds(HALO + tm, HALO), :] = _glu(zn_ref[...], Wb) * hn
        _build_shifts(hs_ref, sh_ref)
        hc = _conv_taps(hs_ref, sh_ref, cw_ref, HALO - pad, tm, kw) + cb_ref[...]
        hc_ref[...] = hc
        mu = jnp.mean(hc, axis=-1, keepdims=True)
        c = hc - mu
        rstd = lax.rsqrt(jnp.mean(c * c, axis=-1, keepdims=True) + EPS)
        o_ref[...] = _silu(c * rstd * lg_ref[...] + lb_ref[...]).astype(BF16)

    prev, nxt = _halo_specs(tm, 2 * Wb, col, M)
    vec = pl.BlockSpec((1, Wb), lambda i: (0, 0))
    out, carried = _carry_call(
        body, carry, out_shape=[SDS(y.shape, BF16), SDS((M, Wb), F32)], grid=(M // tm,),
        in_specs=[pl.BlockSpec((tm, 2 * Wb), lambda i: (i, col)), prev, nxt, pl.BlockSpec(memory_space=pl.ANY),
                  pl.BlockSpec((kw, Wb), lambda i: (0, 0)), vec, vec, vec],
        out_specs=[pl.BlockSpec((tm, Wb), lambda i: (i, W // Wb)), pl.BlockSpec((tm, Wb), lambda i: (i, 0))],
        scratch_shapes=[pltpu.VMEM((tm + 2 * HALO, Wb), F32), pltpu.VMEM((7, tm + 2 * HALO, Wb), F32)],
        aliases={3: 0}, sem=("parallel",), name=name, ins=(z, z, z, y, conv_w, conv_b, ln_g, ln_b))
    return (out, carried) if carry else out


def _conf_bwd1(hc, dy, ln_g, ln_b, W, Wb, seg_t, name):
    M = hc.shape[0]
    tm = _row_tile(seg_t, M)

    def body(hc_ref, dy_ref, lg_ref, lb_ref, dhc_ref, dlg_ref, dlb_ref, dcb_ref):
        i = pl.program_id(0)

        @pl.when(i == 0)
        def _():
            for r in (dlg_ref, dlb_ref, dcb_ref):
                r[...] = jnp.zeros_like(r)

        hc = hc_ref[...]
        mu = jnp.mean(hc, axis=-1, keepdims=True)
        c = hc - mu
        rstd = lax.rsqrt(jnp.mean(c * c, axis=-1, keepdims=True) + EPS)
        hh = c * rstd
        lg = lg_ref[...]
        dhn = dy_ref[...] * _dsilu(hh * lg + lb_ref[...])
        dlg_ref[...] += _colsum(dhn * hh)
        dlb_ref[...] += _colsum(dhn)
        dhh = dhn * lg
        dhc = rstd * (dhh - jnp.mean(dhh, axis=-1, keepdims=True) - hh * jnp.mean(dhh * hh, axis=-1, keepdims=True))
        dcb_ref[...] += _colsum(dhc)
        dhc_ref[...] = dhc

    vec = pl.BlockSpec((1, Wb), lambda i: (0, 0))
    return pl.pallas_call(
        body, out_shape=[SDS((M, Wb), F32), SDS((1, Wb), F32), SDS((1, Wb), F32), SDS((1, Wb), F32)], grid=(M // tm,),
        in_specs=[pl.BlockSpec((tm, Wb), lambda i: (i, 0)), pl.BlockSpec((tm, Wb), lambda i: (i, W // Wb)), vec, vec],
        out_specs=[pl.BlockSpec((tm, Wb), lambda i: (i, 0)), vec, vec, vec],
        compiler_params=_params(("arbitrary",)), name=name)(hc, dy, ln_g, ln_b)


def _conf_bwd2(z, dhc, dz, conv_w, W, Wb, seg_t, name, carry=None):
    M = z.shape[0]
    tm = _row_tile(seg_t, M)
    kw = conv_w.shape[0]
    pad = (kw - 1) // 2
    col = (2 * W) // (2 * Wb)

    def body(zc_ref, zp_ref, zn_ref, dc_ref, dp_ref, dn_ref, dz_hbm, cw_ref, dz_ref, dcw_ref, dbin_ref, hs_ref, ds_ref,
             hsh_ref, dsh_ref):
        del dz_hbm
        i = pl.program_id(0)

        @pl.when(i == 0)
        def _():
            dcw_ref[...] = jnp.zeros_like(dcw_ref)
            dbin_ref[...] = jnp.zeros_like(dbin_ref)

        hp, hn = _edge_flags(i, tm, seg_t, M)
        zc = zc_ref[...]
        hs_ref[pl.ds(0, HALO), :] = _glu(zp_ref[...], Wb) * hp
        hs_ref[pl.ds(HALO, tm), :] = _glu(zc, Wb)
        hs_ref[pl.ds(HALO + tm, HALO), :] = _glu(zn_ref[...], Wb) * hn
        dcur = dc_ref[...]
        ds_ref[pl.ds(0, HALO), :] = dp_ref[...] * hp
        ds_ref[pl.ds(HALO, tm), :] = dcur
        ds_ref[pl.ds(HALO + tm, HALO), :] = dn_ref[...] * hn
        _build_shifts(ds_ref, dsh_ref)
        _build_shifts(hs_ref, hsh_ref)
        dh = _conv_taps(ds_ref, dsh_ref, cw_ref, HALO - pad, tm, kw, flip=True)
        for k in range(kw):
            dcw_ref[pl.ds(k, 1), :] += _colsum(dcur * _shifted(hs_ref, hsh_ref, HALO - pad + k, 0, tm))
        a, gt = zc[:, :Wb], zc[:, Wb:]
        s = _sigmoid(gt)
        dz = jnp.concatenate([dh * s, dh * a * s * (1.0 - s)], axis=1)
        dbin_ref[...] += _colsum(dz)
        dz_ref[...] = dz.astype(BF16)

    prev, nxt = _halo_specs(tm, 2 * Wb, col, M)
    dprev, dnxt = _halo_specs(tm, Wb, 0, M)
    out, carried = _carry_call(
        body, carry, out_shape=[SDS(dz.shape, BF16), SDS((kw, Wb), F32), SDS((1, 2 * Wb), F32)], grid=(M // tm,),
        in_specs=[pl.BlockSpec((tm, 2 * Wb), lambda i: (i, col)), prev, nxt,
                  pl.BlockSpec((tm, Wb), lambda i: (i, 0)), dprev, dnxt, pl.BlockSpec(memory_space=pl.ANY),
                  pl.BlockSpec((kw, Wb), lambda i: (0, 0))],
        out_specs=[pl.BlockSpec((tm, 2 * Wb), lambda i: (i, col)), pl.BlockSpec((kw, Wb), lambda i: (0, 0)),
                   pl.BlockSpec((1, 2 * Wb), lambda i: (0, 0))],
        scratch_shapes=[pltpu.VMEM((tm + 2 * HALO, Wb), F32), pltpu.VMEM((tm + 2 * HALO, Wb), F32),
                        pltpu.VMEM((7, tm + 2 * HALO, Wb), F32), pltpu.VMEM((7, tm + 2 * HALO, Wb), F32)],
        aliases={6: 0}, sem=("arbitrary",), name=name, ins=(z, z, z, dhc, dhc, dhc, dz, conv_w))
    return (out, carried) if carry else out


_TF = (1408, 512, 256, 128)
_RC = 16
_CG = 256


def _col_groups(width):
    return [(c0, min(_CG, width - c0)) for c0 in range(0, width, _CG)]


def _ffn_act_fwd(z, conv_w, conv_b, seg_t, name):
    M, F2 = z.shape
    Fd = F2 // 2
    tm = _row_tile(seg_t, M)
    tf = _tile(Fd, _TF)
    nf = Fd // tf
    per, last = tm // HALO, M // HALO - 1

    def body(g_ref, gp_ref, gn_ref, u_ref, cw_ref, cb_ref, o_ref, gc_ref, gs_ref):
        hp, hn = _edge_flags(pl.program_id(0), tm, seg_t, M)
        gs_ref[pl.ds(0, HALO), :] = gp_ref[...].astype(F32) * hp
        gs_ref[pl.ds(HALO, tm), :] = g_ref[...].astype(F32)
        gs_ref[pl.ds(HALO + tm, HALO), :] = gn_ref[...].astype(F32) * hn
        for c0, cw in _col_groups(tf):
            cs = pl.ds(c0, cw)
            w0, w1, w2, cb = cw_ref[pl.ds(0, 1), cs], cw_ref[pl.ds(1, 1), cs], cw_ref[pl.ds(2, 1), cs], cb_ref[:, cs]
            for r0 in range(0, tm, _RC):
                gc = (gs_ref[pl.ds(HALO - 1 + r0, _RC), cs] * w0 + gs_ref[pl.ds(HALO + r0, _RC), cs] * w1
                      + gs_ref[pl.ds(HALO + 1 + r0, _RC), cs] * w2 + cb)
                o_ref[pl.ds(r0, _RC), cs] = (_silu(gc) * u_ref[pl.ds(r0, _RC), cs].astype(F32)).astype(BF16)
                gc_ref[pl.ds(r0, _RC), cs] = gc.astype(BF16)

    return pl.pallas_call(
        body, out_shape=[SDS((M, Fd), BF16), SDS((M, Fd), BF16)], grid=(M // tm, nf),
        in_specs=[pl.BlockSpec((tm, tf), lambda i, j: (i, j)),
                  pl.BlockSpec((HALO, tf), lambda i, j: (jnp.maximum(i * per - 1, 0), j)),
                  pl.BlockSpec((HALO, tf), lambda i, j: (jnp.minimum((i + 1) * per, last), j)),
                  pl.BlockSpec((tm, tf), lambda i, j: (i, nf + j)),
                  pl.BlockSpec((3, tf), lambda i, j: (0, j)), pl.BlockSpec((1, tf), lambda i, j: (0, j))],
        out_specs=[pl.BlockSpec((tm, tf), lambda i, j: (i, j)), pl.BlockSpec((tm, tf), lambda i, j: (i, j))],
        scratch_shapes=[pltpu.VMEM((tm + 2 * HALO, tf), F32)],
        compiler_params=_params(("parallel", "parallel")), name=name)(z, z, z, z, conv_w, conv_b)


def _ffn_act_bwd(z, gc, da, conv_w, seg_t, name):
    M, F2 = z.shape
    Fd = F2 // 2
    tm = _row_tile(seg_t, M)
    tf = _tile(Fd, _TF)
    nf = Fd // tf
    per, last = tm // HALO, M // HALO - 1
    n_piece = tm // _RC

    def body(g_ref, c_ref, cp_ref, cn_ref, u_ref, up_ref, un_ref, a_ref, ap_ref, an_ref, cw_ref,
             dz_ref, dcw_ref, dcb_ref, ds_ref):
        i = pl.program_id(1)

        def tile():
            hp, hn = _edge_flags(i, tm, seg_t, M)

            @pl.when(i == 0)
            def _():
                dcw_ref[...] = jnp.zeros_like(dcw_ref)
                dcb_ref[...] = jnp.zeros_like(dcb_ref)

            def fold(v):
                return v[:8] + v[8:]

            for c0, cw in _col_groups(tf):
                cs = pl.ds(c0, cw)
                w0, w1, w2 = cw_ref[pl.ds(0, 1), cs], cw_ref[pl.ds(1, 1), cs], cw_ref[pl.ds(2, 1), cs]
                for ci in range(-1, n_piece + 1):
                    r0 = ci * _RC
                    if ci < 0:
                        gcv, ue, ae = cp_ref[:, cs], up_ref[:, cs], ap_ref[:, cs].astype(F32) * hp
                    elif ci == n_piece:
                        gcv, ue, ae = cn_ref[:, cs], un_ref[:, cs], an_ref[:, cs].astype(F32) * hn
                    else:
                        rows = pl.ds(r0, _RC)
                        gcv, ue, ae = c_ref[rows, cs], u_ref[rows, cs], a_ref[rows, cs].astype(F32)
                    gcv, ue = gcv.astype(F32), ue.astype(F32)
                    sg = _sigmoid(gcv)
                    t = ae * sg
                    ds_ref[pl.ds(HALO + r0, _RC), cs] = t * ue * (1.0 + gcv * (1.0 - sg))
                    if 0 <= ci < n_piece:
                        dz_ref[1, pl.ds(r0, _RC), cs] = (t * gcv).astype(BF16)
                acc = [jnp.zeros((8, cw), F32) for _ in range(4)]
                for r0 in range(0, tm, _RC):
                    b = HALO + r0
                    d = [ds_ref[pl.ds(b + 1 - k, _RC), cs] for k in range(3)]
                    dz_ref[0, pl.ds(r0, _RC), cs] = (d[0] * w0 + d[1] * w1 + d[2] * w2).astype(BF16)
                    gv = g_ref[pl.ds(r0, _RC), cs].astype(F32)
                    for k in range(3):
                        acc[k] = acc[k] + fold(d[k] * gv)
                    acc[3] = acc[3] + fold(d[1])
                for k in range(3):
                    dcw_ref[pl.ds(k, 1), cs] += _colsum(acc[k])
                dcb_ref[:, cs] += _colsum(acc[3])

        tile()

    def cur(off):
        return pl.BlockSpec((tm, tf), lambda j, i: (i, off + j))

    def prv(off):
        return pl.BlockSpec((HALO, tf), lambda j, i: (jnp.maximum(i * per - 1, 0), off + j))

    def nxt(off):
        return pl.BlockSpec((HALO, tf), lambda j, i: (jnp.minimum((i + 1) * per, last), off + j))

    return pl.pallas_call(
        body, out_shape=[SDS((2, M, Fd), BF16), SDS((3, Fd), F32), SDS((1, Fd), F32)], grid=(nf, M // tm),
        in_specs=[cur(0), cur(0), prv(0), nxt(0), cur(nf), prv(nf), nxt(nf), cur(0), prv(0), nxt(0),
                  pl.BlockSpec((3, tf), lambda j, i: (0, j))],
        out_specs=[pl.BlockSpec((2, tm, tf), lambda j, i: (0, i, j)),
                   pl.BlockSpec((3, tf), lambda j, i: (0, j)), pl.BlockSpec((1, tf), lambda j, i: (0, j))],
        scratch_shapes=[pltpu.VMEM((tm + 2 * HALO, tf), F32)],
        compiler_params=_params(("parallel", "arbitrary")), name=name)(
            z, gc, gc, gc, z, z, z, da, da, da, conv_w)


_LN2 = math.log(2.0)
_QSCALE = (NOPE + ROPE) ** -0.5 / _LN2


def _swap32(x):
    lane = lax.broadcasted_iota(jnp.int32, x.shape, 1)
    return jnp.where((lane % 64) < 32, pltpu.roll(x, 96, axis=1), pltpu.roll(x, 32, axis=1))


def _rms(x, g):
    r = lax.rsqrt(jnp.mean(x * x, axis=-1, keepdims=True) + EPS)
    return x * r * g


def _rms_bwd(x, g, dy):
    r = lax.rsqrt(jnp.mean(x * x, axis=-1, keepdims=True) + EPS)
    xh = x * r
    dxh = dy * g
    return r * (dxh - xh * jnp.mean(dxh * xh, axis=-1, keepdims=True)), _colsum(dy * xh)


def _mla_prep_fwd(z, gq, gkv, cos, sin, QL, KL, name):
    M, NZ = z.shape
    tm = _tile(M, (256, 128))

    def body(z_ref, gq_ref, gkv_ref, cos_ref, sin_ref, cq_ref, ckv_ref, kpe_ref):
        zv = z_ref[...]
        cq_ref[...] = _rms(zv[:, :QL], gq_ref[...]).astype(BF16)
        ckv_ref[...] = _rms(zv[:, QL:QL + KL], gkv_ref[...]).astype(BF16)
        kp = zv[:, QL + KL:]
        r = kp * cos_ref[...] + _swap32(kp) * sin_ref[...]
        lane = lax.broadcasted_iota(jnp.int32, r.shape, 1)
        kpe_ref[0] = jnp.where(lane < ROPE, r, 0.0).astype(BF16)
        kpe_ref[1] = jnp.where(lane >= ROPE, r, 0.0).astype(BF16)

    tab = pl.BlockSpec((tm, 128), lambda i: (i, 0))
    return pl.pallas_call(
        body, out_shape=[SDS((M, QL), BF16), SDS((M, KL), BF16), SDS((2, M, 128), BF16)], grid=(M // tm,),
        in_specs=[pl.BlockSpec((tm, NZ), lambda i: (i, 0)), pl.BlockSpec((1, QL), lambda i: (0, 0)),
                  pl.BlockSpec((1, KL), lambda i: (0, 0)), tab, tab],
        out_specs=[pl.BlockSpec((tm, QL), lambda i: (i, 0)), pl.BlockSpec((tm, KL), lambda i: (i, 0)),
                   pl.BlockSpec((2, tm, 128), lambda i: (0, i, 0))],
        compiler_params=_params(("parallel",)), name=name)(z, gq, gkv, cos, sin)


def _mla_prep_bwd(z, dcq, dckv, dkpe, gq, gkv, cos, sin, QL, KL, seg_t, name):
    M, NZ = z.shape
    H = dkpe.shape[0]
    tm = _row_tile(seg_t, M)
    nt = seg_t // tm

    def body(z_ref, dcq_ref, dckv_ref, dkpe_ref, gq_ref, gkv_ref, cos_ref, sin_ref, dz_ref, dgq_ref, dgkv_ref):
        i = pl.program_id(0)

        @pl.when(i == 0)
        def _():
            dgq_ref[...] = jnp.zeros_like(dgq_ref)
            dgkv_ref[...] = jnp.zeros_like(dgkv_ref)

        zv = z_ref[...]
        dyq = jnp.where(i < nt, dcq_ref[...], 0.0)
        dxq, dgq = _rms_bwd(zv[:, :QL], gq_ref[...], dyq)
        dxkv, dgkv = _rms_bwd(zv[:, QL:QL + KL], gkv_ref[...], dckv_ref[...])
        dgq_ref[...] += dgq
        dgkv_ref[...] += dgkv
        even = dkpe_ref[0]
        odd = dkpe_ref[1]
        for h in range(2, H, 2):
            even = even + dkpe_ref[h]
            odd = odd + dkpe_ref[h + 1]
        lane = lax.broadcasted_iota(jnp.int32, even.shape, 1)
        dr = jnp.where(lane < ROPE, even, odd)
        dkp = dr * cos_ref[...] - _swap32(dr) * sin_ref[...]
        dz_ref[...] = jnp.concatenate([dxq, dxkv, dkp], axis=1).astype(BF16)

    tab = pl.BlockSpec((tm, 128), lambda i: (i, 0))
    return pl.pallas_call(
        body, out_shape=[SDS((M, NZ), BF16), SDS((1, QL), F32), SDS((1, KL), F32)], grid=(M // tm,),
        in_specs=[pl.BlockSpec((tm, NZ), lambda i: (i, 0)),
                  pl.BlockSpec((tm, QL), lambda i: (jnp.minimum(i, nt - 1), 0)),
                  pl.BlockSpec((tm, KL), lambda i: (i, 0)), pl.BlockSpec((H, tm, 128), lambda i: (0, i, 0)),
                  pl.BlockSpec((1, QL), lambda i: (0, 0)), pl.BlockSpec((1, KL), lambda i: (0, 0)), tab, tab],
        out_specs=[pl.BlockSpec((tm, NZ), lambda i: (i, 0)), pl.BlockSpec((1, QL), lambda i: (0, 0)),
                   pl.BlockSpec((1, KL), lambda i: (0, 0))],
        compiler_params=_params(("arbitrary",)), name=name)(z, dcq, dckv, dkpe, gq, gkv, cos, sin)


def _qrope_fwd(q, cos, sin, HN, name):
    T, NQ = q.shape
    tm = _tile(T, (256, 128))

    def body(q_ref, cos_ref, sin_ref, o_ref):
        o_ref[:, :HN] = (q_ref[:, :HN] * _QSCALE).astype(BF16)
        for cb in range((NQ - HN) // 128):
            sl = slice(HN + cb * 128, HN + (cb + 1) * 128)
            xv = q_ref[:, sl]
            o_ref[:, sl] = ((xv * cos_ref[...] + _swap32(xv) * sin_ref[...]) * _QSCALE).astype(BF16)

    tab = pl.BlockSpec((tm, 128), lambda i: (i, 0))
    return pl.pallas_call(
        body, out_shape=SDS((T, NQ), BF16), grid=(T // tm,),
        in_specs=[pl.BlockSpec((tm, NQ), lambda i: (i, 0)), tab, tab],
        out_specs=pl.BlockSpec((tm, NQ), lambda i: (i, 0)),
        compiler_params=_params(("parallel",)), name=name)(q, cos, sin)


def _qrope_bwd(dqpe, dqa, cos, sin, HN, name):
    T, HW = dqpe.shape
    HR = HW // 2
    tm = _tile(T, (256, 128))

    def body(d_ref, dqa_hbm, cos_ref, sin_ref, o_ref):
        del dqa_hbm
        for pr in range(HR // 128):
            dr = d_ref[:, 2 * pr * 128:(2 * pr + 1) * 128] + d_ref[:, (2 * pr + 1) * 128:(2 * pr + 2) * 128]
            o_ref[:, pr * 128:(pr + 1) * 128] = (dr * cos_ref[...] - _swap32(dr) * sin_ref[...]).astype(BF16)

    tab = pl.BlockSpec((tm, 128), lambda i: (i, 0))
    return pl.pallas_call(
        body, out_shape=SDS(dqa.shape, BF16), grid=(T // tm,),
        in_specs=[pl.BlockSpec((tm, HW), lambda i: (i, 0)), pl.BlockSpec(memory_space=pl.ANY), tab, tab],
        out_specs=pl.BlockSpec((tm, HR), lambda i: (i, HN // HR)),
        input_output_aliases={1: 0}, compiler_params=_params(("parallel",)), name=name)(dqpe, dqa, cos, sin)


_ATT_SUB = 4
_ATT_SUB_B = 4
_NT = (((1,), (1,)), ((), ()))
_TN = (((0,), (0,)), ((), ()))


def _attn_fwd(qa, kv, kpe, T, H, name, carry=None):
    M = kv.shape[0]
    tq = _tile(T, (1024, 512, 256, 128))
    scale = (NOPE + ROPE) ** -0.5

    def body(qn_ref, qp_ref, kv_ref, kpe_ref, o_ref, lse_ref, kc_ref):
        @pl.when(pl.program_id(1) == 0)
        def _():
            kc_ref[:, :NOPE] = kv_ref[:, :NOPE]
            kc_ref[:, NOPE:] = kpe_ref[0]

        rs = tq // _ATT_SUB
        outs, lses = [], []
        for u in range(_ATT_SUB):
            rows = pl.ds(u * rs, rs)
            qc = jnp.concatenate([qn_ref[rows, :], qp_ref[rows, :]], axis=1)
            s = lax.dot_general(qc, kc_ref[...], _NT, preferred_element_type=F32)
            m = jnp.max(s, axis=-1, keepdims=True)
            p = jnp.exp2(s - m)
            l = jnp.sum(p, axis=-1, keepdims=True)
            o = jnp.dot(p.astype(BF16), kv_ref[:, NOPE:], preferred_element_type=F32)
            outs.append((o / l).astype(BF16))
            lses.append(jnp.broadcast_to(m + jnp.log2(l), (rs, 128)))
        o_ref[...] = jnp.concatenate(outs, axis=0)
        lse_ref[...] = jnp.concatenate(lses, axis=0)

    return _carry_call(
        body, carry, out_shape=[SDS((T, H * VDIM), BF16), SDS((T, H * 128), F32)], grid=(H, T // tq),
        in_specs=[pl.BlockSpec((tq, NOPE), lambda h, i: (i, h)), pl.BlockSpec((tq, 128), lambda h, i: (i, H + h // 2)),
                  pl.BlockSpec((M, NOPE + VDIM), lambda h, i: (0, h)), pl.BlockSpec((1, M, 128), lambda h, i: (h % 2, 0, 0))],
        out_specs=[pl.BlockSpec((tq, VDIM), lambda h, i: (i, h)), pl.BlockSpec((tq, 128), lambda h, i: (i, h))],
        scratch_shapes=[pltpu.VMEM((M, NOPE + 128), BF16)],
        sem=("parallel", "arbitrary"), name=name, ins=(qa, qa, kv, kpe))


def _attn_bwd(qa, kv, kpe, do, o, lse, T, H, name, carry=None):
    M = kv.shape[0]
    tq = _tile(T, (1024, 512, 256, 128))
    nq = T // tq
    scale = (NOPE + ROPE) ** -0.5

    def body(qn_ref, qp_ref, kv_ref, kpe_ref, do_ref, o_ref, lse_ref, dqa_ref, dqpe_ref, dkv_ref, dkpe_ref, kc_ref,
             dk_acc, dv_acc):
        i = pl.program_id(1)

        @pl.when(i == 0)
        def _():
            kc_ref[:, :NOPE] = kv_ref[:, :NOPE]
            kc_ref[:, NOPE:] = kpe_ref[0]
            dk_acc[...] = jnp.zeros_like(dk_acc)
            dv_acc[...] = jnp.zeros_like(dv_acc)

        rs = tq // _ATT_SUB_B
        p16s, ds16s = [], []
        for u in range(_ATT_SUB_B):
            rows = pl.ds(u * rs, rs)
            qc = jnp.concatenate([qn_ref[rows, :], qp_ref[rows, :]], axis=1)
            dov = do_ref[rows, :]
            s = lax.dot_general(qc, kc_ref[...], _NT, preferred_element_type=F32)
            p = jnp.exp2(s - lse_ref[rows, 0:1])
            dp = lax.dot_general(dov, kv_ref[:, NOPE:], _NT, preferred_element_type=F32)
            delta = jnp.sum(dov.astype(F32) * o_ref[rows, :].astype(F32), axis=-1, keepdims=True)
            ds16s.append((p * (dp - delta)).astype(BF16))
            p16s.append(p.astype(BF16))
        p16 = jnp.concatenate(p16s, axis=0)
        ds16 = jnp.concatenate(ds16s, axis=0)
        qc = jnp.concatenate([qn_ref[...], qp_ref[...]], axis=1)
        dq = jnp.dot(ds16, kc_ref[...], preferred_element_type=F32) * scale
        dqa_ref[...] = dq[:, :NOPE].astype(BF16)
        dqpe_ref[...] = dq[:, NOPE:]
        dv_acc[...] += lax.dot_general(p16, do_ref[...], _TN, preferred_element_type=F32)
        dk_acc[...] += lax.dot_general(ds16, qc, _TN, preferred_element_type=F32)

        @pl.when(i == nq - 1)
        def _():
            dkv_ref[:, :NOPE] = (dk_acc[:, :NOPE] * _LN2).astype(BF16)
            dkv_ref[:, NOPE:] = dv_acc[...].astype(BF16)
            dkpe_ref[0] = dk_acc[:, NOPE:] * _LN2

    return _carry_call(
        body, carry,
        out_shape=[SDS((T, H * (NOPE + ROPE)), BF16), SDS((T, H * 128), F32), SDS((M, H * (NOPE + VDIM)), BF16),
                   SDS((H, M, 128), F32)],
        grid=(H, nq),
        in_specs=[pl.BlockSpec((tq, NOPE), lambda h, i: (i, h)), pl.BlockSpec((tq, 128), lambda h, i: (i, H + h // 2)),
                  pl.BlockSpec((M, NOPE + VDIM), lambda h, i: (0, h)), pl.BlockSpec((1, M, 128), lambda h, i: (h % 2, 0, 0)),
                  pl.BlockSpec((tq, VDIM), lambda h, i: (i, h)), pl.BlockSpec((tq, VDIM), lambda h, i: (i, h)),
                  pl.BlockSpec((tq, 128), lambda h, i: (i, h))],
        out_specs=[pl.BlockSpec((tq, NOPE), lambda h, i: (i, h)), pl.BlockSpec((tq, 128), lambda h, i: (i, h)),
                   pl.BlockSpec((M, NOPE + VDIM), lambda h, i: (0, h)), pl.BlockSpec((1, M, 128), lambda h, i: (h, 0, 0))],
        scratch_shapes=[pltpu.VMEM((M, NOPE + 128), BF16), pltpu.VMEM((M, NOPE + 128), F32), pltpu.VMEM((M, VDIM), F32)],
        sem=("parallel", "arbitrary"), name=name, ins=(qa, qa, kv, kpe, do, o, lse))


def _adamw(w, m, v, name, g=None, recv=None, carry=None):
    R, C = w.shape
    summed = recv is not None
    n_recv = len(recv) if summed else 1
    runs = [r.shape[1] for r in recv] if summed else [R]
    tr = math.gcd(*runs)
    for cand in (1024, 512, 256, 128, 64, 32, 16, 8):
        if tr % cand == 0 and cand * C <= 131072:
            tr = cand
            break
    first = [sum(runs[:r]) // tr for r in range(n_recv + 1)]
    c1 = 1.0 - ADAM_B1 ** ADAM_STEP
    c2 = 1.0 - ADAM_B2 ** ADAM_STEP

    def update(gv, w_ref, m_ref, v_ref, d_ref, nm_ref, nv_ref):
        mn = ADAM_B1 * m_ref[...] + (1.0 - ADAM_B1) * gv
        vn = ADAM_B2 * v_ref[...] + (1.0 - ADAM_B2) * (gv * gv)
        nm_ref[...] = mn
        nv_ref[...] = vn
        d_ref[...] = -ADAM_LR * ((mn / c1) / (jnp.sqrt(vn / c2) + ADAM_EPS) + ADAM_WD * w_ref[...])

    def body(*refs):
        w_ref, m_ref, v_ref = refs[:3]
        g_refs = refs[3:3 + n_recv]
        outs = refs[3 + n_recv:]
        if not summed:
            update(g_refs[0][...], w_ref, m_ref, v_ref, *outs)
            return
        i = pl.program_id(0)
        for r in range(n_recv):
            @pl.when((i >= first[r]) & (i < first[r + 1]))
            def _():
                gv = g_refs[r][0].astype(F32)
                for d in range(1, N_DEV):
                    gv = gv + g_refs[r][d].astype(F32)
                outs[0][...] = gv
                update(gv, w_ref, m_ref, v_ref, *outs[1:])

    blk = pl.BlockSpec((tr, C), lambda i: (i, 0))
    if summed:
        g_specs = [pl.BlockSpec((N_DEV, tr, C), functools.partial(
            lambda i, lo, n: (0, jnp.clip(i - lo, 0, n - 1), 0), lo=first[r], n=first[r + 1] - first[r]))
                   for r in range(n_recv)]
    else:
        g_specs = [blk]
    n_out = 4 if summed else 3
    out, carried = _carry_call(
        body, carry, out_shape=[SDS((R, C), F32)] * n_out, grid=(R // tr,), in_specs=[blk, blk, blk] + g_specs,
        out_specs=[blk] * n_out, scratch_shapes=[], sem=("parallel",), name=name,
        ins=(w, m, v, *(recv if summed else [g])))
    return (out, carried) if carry else out


WEIGHTS = ['c_ctx', 'norm1_g', 'norm2_g', 'w_ada', 'b_ada', 'ab_w_in', 'ab_b_in', 'a_ln_g', 'a_ln_b', 'a_w_s', 'a_b_s',
           'b_conv_w', 'b_conv_b', 'b_ln_g', 'b_ln_b', 'ab_w_out', 'mla_w_in', 'mla_q_norm_g', 'mla_w_uq',
           'mla_kv_norm_g', 'mla_w_ukv', 'mla_w_o', 'ffn_w_up', 'ffn_conv_w', 'ffn_conv_b', 'ffn_w_down', 'final_norm_g']


def _pack(parts):
    flat = jnp.concatenate([p.reshape(-1).astype(F32) for p in parts])
    n = flat.shape[0]
    unit = 65536 if n > 65536 else 1024
    n_pad = -(-n // unit) * unit
    return jnp.pad(flat, (0, n_pad - n)).reshape(n_pad // 128, 128)


def _unpack(flat, like):
    out, off = [], 0
    for shp in like:
        n = math.prod(shp)
        out.append(flat[..., off:off + n].reshape(flat.shape[:-1] + tuple(shp)))
        off += n
    return out


def _rope_tables(T, Tc):
    rows = T // GRID_W
    row = jnp.repeat(jnp.arange(rows, dtype=F32), GRID_W)
    col = jnp.tile(jnp.arange(GRID_W, dtype=F32), rows)
    n_freq = ROPE // 4
    inv = ROPE_THETA ** (-jnp.arange(n_freq, dtype=F32) / n_freq)
    ang = jnp.concatenate([row[:, None] * inv, col[:, None] * inv], axis=-1)
    cos, sin = jnp.cos(ang), jnp.sin(ang)
    cos = jnp.tile(cos, (1, 128 // (ROPE // 2)))
    sin = jnp.tile(jnp.concatenate([-sin, sin], axis=1), (1, 128 // ROPE))
    return (jnp.concatenate([cos, jnp.ones((Tc, 128), F32)], axis=0),
            jnp.concatenate([sin, jnp.zeros((Tc, 128), F32)], axis=0))


def _step(a):
    ax, ay, ac = lax.axis_index("x"), lax.axis_index("y"), lax.axis_index("c")
    me = 4 * ax + 2 * ay + ac
    T, D = a['x'].shape[1:]
    Tc = a['ctx'].shape[1]
    M = T + Tc
    W, Wb = a['a_ln_g'].shape[1], a['b_ln_g'].shape[1]
    assert W == Wb and T % Tc == 0
    Fd = a['ffn_conv_b'].shape[1]
    QL, KL = a['mla_q_norm_g'].shape[1] * N_DEV, a['mla_kv_norm_g'].shape[1] * N_DEV
    H = a['mla_w_ukv'].shape[2] * N_DEV // (NOPE + VDIM)
    HN, HR = H * NOPE, H * ROPE
    kw = a['b_conv_w'].shape[1]
    NA = a['w_ada'].shape[2]
    bf = lambda t: t.astype(BF16)

    small_shapes = [(D,), (kw, Wb // N_DEV), (2, 3, Fd // N_DEV), (QL // N_DEV,), (KL // N_DEV,)]
    g_small = _all_gather(_pack([a['c'][0], a['b_conv_w'][0], a['ffn_conv_w'], a['mla_q_norm_g'][0], a['mla_kv_norm_g'][0]]),
                          "ag_small")
    c_all, bcw, fcw, gq, gkv = _unpack(g_small.reshape(N_DEV, -1), small_shapes)
    bcw = jnp.transpose(bcw, (1, 0, 2)).reshape(kw, Wb)
    fcw = jnp.transpose(fcw, (1, 2, 0, 3)).reshape(2, 3, Fd)
    gq, gkv = gq.reshape(1, QL), gkv.reshape(1, KL)

    a16 = jnp.concatenate([c_all, a['c_ctx'][None], jnp.zeros((N_DEV - 1, D), F32)], axis=0)
    b_loc = lax.dynamic_slice(a['b_ada'], (0, me * NA), (2, NA))
    mods = [_mm(a16, a['w_ada'][l], mode="nn", out_dtype=F32, name=f"ada_fwd{l}", bias=b_loc[l:l + 1], a_silu=True)
            for l in range(2)]
    gm = _all_gather(jnp.concatenate(mods, axis=0), "ag_mod").reshape(N_DEV, 2, 2 * N_DEV, NA)
    gm = jnp.transpose(gm, (1, 2, 0, 3)).reshape(2, 2 * N_DEV, 6 * D)
    mod_lat = [lax.dynamic_slice(gm[l], (me, 0), (1, 6 * D)).reshape(6, 1, 1, D) for l in range(2)]
    mod_ctx = [gm[l][N_DEV].reshape(6, 1, 1, D) for l in range(2)]

    def mod(l, k, both):
        return jnp.concatenate([mod_lat[l][k], mod_ctx[l][k]], axis=0) if both else mod_lat[l][k]

    def from_cols(g):
        return jnp.transpose(g, (1, 0, 2)).reshape(g.shape[1], -1)

    def from_rows(g):
        return g.reshape(-1, g.shape[2])

    def ag(x):
        return (x, False)

    def a2a(x):
        return (x, True)

    cos, sin = _rope_tables(T, Tc)
    n1g, n2g = a['norm1_g'], a['norm2_g']
    a_bst = a['a_b_s'][0].T
    mm = functools.partial(_mm)
    up_sh, dn_sh = bf(a['ffn_w_up']), bf(a['ffn_w_down'])

    w_abin = _all_gather(bf(a['ab_w_in'][0]), "ag_ab_w_in")
    x0 = jnp.concatenate([a['x'][0], a['ctx'][0]], axis=0)
    h1 = _normmod_fwd(x0, n1g[0:1], mod(0, 0, True), mod(0, 1, True), T, "l0_norm1")
    s1, s2 = D // 4, 5 * D // 8
    z, (g_about, g_up0a) = mm(h1, w_abin, mode="nn", out_dtype=F32, name="l0_ab_in", bias=a['ab_b_in'], b_dev=True,
                              carry=[ag(bf(a['ab_w_out'][0])), ag(up_sh[0][:s1])])
    w_about = from_rows(g_about)
    y = _gmlp_fwd(z, a['a_ln_g'], a['a_ln_b'], a['a_w_s'][0], a_bst, W, "l0_gmlp")
    (y, hc_b), (g_up0b,) = _conf_fwd(z, y, bcw, a['b_conv_b'], a['b_ln_g'], a['b_ln_b'], W, Wb, T, "l0_conf",
                                     carry=[ag(up_sh[0][s1:s2])])
    (x1, o1), (g_up0c,) = mm(y, w_about, mode="nn", out_dtype=F32, name="l0_ab_out", res=x0, gate=mod(0, 2, True),
                             seg_t=T, carry=[ag(up_sh[0][s2:])])
    w_up = [jnp.concatenate([g_up0a, g_up0b, g_up0c], axis=1), None]
    h2 = _normmod_fwd(x1, n2g[0:1], mod(0, 3, True), mod(0, 4, True), T, "l0_norm2")
    z2, (g_dn0,) = mm(h2, w_up[0], mode="nn", out_dtype=BF16, name="l0_up", b_dev=True, carry=[ag(dn_sh[0])])
    w_dn = [from_rows(g_dn0), None]
    a2, gc2 = _ffn_act_fwd(z2, fcw[0], a['ffn_conv_b'][0:1], T, "l0_act")
    (x2, o2), (g_in, g_uq) = mm(a2, w_dn[0], mode="nn", out_dtype=F32, name="l0_down", res=x1, gate=mod(0, 5, True),
                                seg_t=T, carry=[ag(bf(a['mla_w_in'][0])), ag(bf(a['mla_w_uq'][0]))])
    w_in = from_rows(g_in)
    w_in = jnp.concatenate([w_in, w_in[:, QL + KL:]], axis=1)
    w_uq = from_cols(g_uq).reshape(QL, H, NOPE + ROPE)
    w_uq = jnp.concatenate([w_uq[:, :, :NOPE].reshape(QL, HN), w_uq[:, :, NOPE:].reshape(QL, HR)], axis=1)

    h3 = _normmod_fwd(x2, n1g[1:2], mod(1, 0, True), mod(1, 1, True), T, "l1_norm1")
    z3, (g_ukv,) = mm(h3, w_in, mode="nn", out_dtype=F32, name="l1_mla_in", carry=[ag(bf(a['mla_w_ukv'][0]))])
    w_ukv = g_ukv
    cqn, ckvn, kpe = _mla_prep_fwd(z3, gq, gkv, cos, sin, QL, KL, "l1_prep")
    q, (g_wo,) = mm(cqn, w_uq, mode="nn", out_dtype=F32, name="l1_uq", rows=T, carry=[ag(bf(a['mla_w_o'][0]))])
    w_o = from_rows(g_wo)
    kv = mm(ckvn, w_ukv, mode="nn", out_dtype=BF16, name="l1_ukv", b_dev=True)
    qa = _qrope_fwd(q, cos, sin, HN, "l1_qrope")
    (o_att, lse), (g_up1,) = _attn_fwd(qa, kv, kpe, T, H, "l1_attn", carry=[ag(up_sh[1])])
    w_up[1] = g_up1
    x3, o3 = mm(o_att, w_o, mode="nn", out_dtype=F32, name="l1_wo", res=x2, gate=mod(1, 2, False), seg_t=T)
    h4 = _normmod_fwd(x3, n2g[1:2], mod(1, 3, False), mod(1, 4, False), T, "l1_norm2")
    z4, (g_dn1,) = mm(h4, w_up[1], mode="nn", out_dtype=BF16, name="l1_up", b_dev=True, carry=[ag(dn_sh[1])])
    w_dn[1] = from_rows(g_dn1)
    a4, gc4 = _ffn_act_fwd(z4, fcw[1], a['ffn_conv_b'][1:2], T, "l1_act")
    x4, o4 = mm(a4, w_dn[1], mode="nn", out_dtype=F32, name="l1_down", res=x3, gate=mod(1, 5, False), seg_t=T)

    dx4, loss_cols, d_fng, do4, dg2_1 = _final(x4, a['final_norm_g'][None], a['loss_target'][0], o4, mod(1, 5, False),
                                               "final")
    loss = lax.psum(jnp.sum(loss_cols), ("x", "y", "c"))

    def cols(dw):
        k, n = dw.shape
        return jnp.transpose(dw.reshape(k, N_DEV, n // N_DEV), (1, 0, 2))

    def rows(dw):
        return dw.reshape(N_DEV, dw.shape[0] // N_DEV, dw.shape[1])

    da4 = mm(do4, w_dn[1], mode="nt", out_dtype=BF16, name="l1_down_dx")
    dw_dn1 = mm(a4, do4, mode="tn", out_dtype=BF16, name="l1_down_dw")
    dz4, dfcw1, dfcb1 = _ffn_act_bwd(z4, gc4, da4, fcw[1], T, "l1_act_bwd")
    dw_up1, (r_dn1,) = mm(h4, dz4, mode="tn", out_dtype=BF16, name="l1_up_dw", out_dev=True, halves=True,
                          carry=[a2a(rows(dw_dn1))])
    dh4 = mm(dz4, w_up[1], mode="nt", out_dtype=F32, name="l1_up_dx", b_dev=True, halves=True)
    dx3, dn2g1, dsh2_1, dsc2_1, do3, dg1_1 = _normmod_bwd(x3, n2g[1:2], mod(1, 4, False), dh4, dx4, T, "l1_norm2_bwd",
                                                         o_prev=o3, gate_prev=mod(1, 2, False))
    d_oatt = mm(do3, w_o, mode="nt", out_dtype=BF16, name="l1_wo_dx")
    dw_o = mm(o_att, do3, mode="tn", out_dtype=BF16, name="l1_wo_dw")
    (dqa, dqpe, dkv, dkpe), (r_up1, r_wo) = _attn_bwd(qa, kv, kpe, d_oatt, o_att, lse, T, H, "l1_attn_bwd",
                                                      carry=[a2a(dw_up1), a2a(rows(dw_o))])
    dqa = _qrope_bwd(dqpe, dqa, cos, sin, HN, "l1_qrope_bwd")
    dcq = mm(dqa, w_uq, mode="nt", out_dtype=F32, name="l1_uq_dx")
    dw_uq = mm(cqn, dqa, mode="tn", out_dtype=BF16, name="l1_uq_dw", rows=T)
    dw_uq = jnp.concatenate([dw_uq[:, :HN].reshape(QL, H, NOPE), dw_uq[:, HN:].reshape(QL, H, ROPE)], axis=2)
    dw_uq = dw_uq.reshape(QL, H * (NOPE + ROPE))
    dckv = mm(dkv, w_ukv, mode="nt", out_dtype=F32, name="l1_ukv_dx", b_dev=True)
    dw_ukv = mm(ckvn, dkv, mode="tn", out_dtype=BF16, name="l1_ukv_dw", out_dev=True)
    dz3, dgq, dgkv = _mla_prep_bwd(z3, dcq, dckv, dkpe, gq, gkv, cos, sin, QL, KL, T, "l1_prep_bwd")
    dh3 = mm(dz3, w_in, mode="nt", out_dtype=F32, name="l1_mla_in_dx")
    dw_in = mm(h3, dz3, mode="tn", out_dtype=BF16, name="l1_mla_in_dw").astype(F32)
    dw_in = jnp.concatenate([dw_in[:, :QL + KL], dw_in[:, QL + KL:QL + KL + ROPE] + dw_in[:, QL + KL + ROPE:QL + KL + 2 * ROPE]],
                            axis=1).astype(BF16)
    dx2, dn1g1, dsh1_1, dsc1_1, do2, dg2_0 = _normmod_bwd(x2, n1g[1:2], mod(1, 1, True), dh3, dx3, T, "l1_norm1_bwd",
                                                         o_prev=o2, gate_prev=mod(0, 5, True))
    da2, (r_uq, r_ukv) = mm(do2, w_dn[0], mode="nt", out_dtype=BF16, name="l0_down_dx",
                            carry=[a2a(cols(dw_uq)), a2a(dw_ukv)])
    dw_dn0, (r_in,) = mm(a2, do2, mode="tn", out_dtype=BF16, name="l0_down_dw", carry=[a2a(rows(dw_in))])
    dz2, dfcw0, dfcb0 = _ffn_act_bwd(z2, gc2, da2, fcw[0], T, "l0_act_bwd")
    dw_up0, (r_dn0,) = mm(h2, dz2, mode="tn", out_dtype=BF16, name="l0_up_dw", out_dev=True, halves=True,
                          carry=[a2a(rows(dw_dn0))])
    dh2, (r_up0a,) = mm(dz2, w_up[0], mode="nt", out_dtype=F32, name="l0_up_dx", b_dev=True, halves=True,
                        carry=[a2a(dw_up0[:, :D // 2])])
    dx1, dn2g0, dsh2_0, dsc2_0, do1, dg1_0 = _normmod_bwd(x1, n2g[0:1], mod(0, 4, True), dh2, dx2, T, "l0_norm2_bwd",
                                                         o_prev=o1, gate_prev=mod(0, 2, True))
    dy = mm(do1, w_about, mode="nt", out_dtype=F32, name="l0_ab_out_dx")
    dw_about = mm(y, do1, mode="tn", out_dtype=BF16, name="l0_ab_out_dw")
    dz, dlag, dlab, dws, dbs, dbin_a = _gmlp_bwd(z, dy, a['a_ln_g'], a['a_ln_b'], a['a_w_s'][0], a_bst, W, "l0_gmlp_bwd")
    dhc, dlbg, dlbb, dbcb = _conf_bwd1(hc_b, dy, a['b_ln_g'], a['b_ln_b'], W, Wb, T, "l0_conf_bwd1")
    (dz, dbcw, dbin_b), (r_up0b,) = _conf_bwd2(z, dhc, dz, bcw, W, Wb, T, "l0_conf_bwd2",
                                               carry=[a2a(dw_up0[:, D // 2:])])
    dh1, (r_about,) = mm(dz, w_abin, mode="nt", out_dtype=F32, name="l0_ab_in_dx", b_dev=True,
                         carry=[a2a(rows(dw_about))])
    dw_abin_a = mm(h1, dz, mode="tn", out_dtype=BF16, name="l0_ab_in_dw_a", out_dev=True, p_range=(0, D // 2))
    dw_abin_b, (r_abin_a,) = mm(h1, dz, mode="tn", out_dtype=BF16, name="l0_ab_in_dw_b", out_dev=True,
                                p_range=(D // 2, D // 2), carry=[a2a(dw_abin_a)])
    dx0, dn1g0, dsh1_0, dsc1_0 = _normmod_bwd(x0, n1g[0:1], mod(0, 1, True), dh1, dx1, T, "l0_norm1_bwd")

    zero = jnp.zeros((D,), F32)
    dmod = jnp.stack([
        jnp.stack([jnp.stack([dsh1_0[0, 0], dsc1_0[0, 0], dg1_0[0, 0], dsh2_0[0, 0], dsc2_0[0, 0], dg2_0[0, 0]]),
                   jnp.stack([dsh1_0[1, 0], dsc1_0[1, 0], dg1_0[1, 0], dsh2_0[1, 0], dsc2_0[1, 0], dg2_0[1, 0]])]),
        jnp.stack([jnp.stack([dsh1_1[0, 0], dsc1_1[0, 0], dg1_1[0, 0], dsh2_1[0, 0], dsc2_1[0, 0], dg2_1[0, 0]]),
                   jnp.stack([dsh1_1[1, 0], dsc1_1[1, 0], zero, zero, zero, zero])])])
    small = {
        'norm1_g': jnp.concatenate([dn1g0, dn1g1], axis=0), 'norm2_g': jnp.concatenate([dn2g0, dn2g1], axis=0),
        'ab_b_in': jnp.concatenate([dbin_a, dbin_b], axis=1), 'a_ln_g': dlag, 'a_ln_b': dlab, 'a_w_s': dws[None],
        'a_b_s': jnp.sum(dbs, axis=-1)[None], 'b_conv_w': dbcw, 'b_conv_b': dbcb, 'b_ln_g': dlbg, 'b_ln_b': dlbb,
        'mla_q_norm_g': dgq, 'mla_kv_norm_g': dgkv, 'ffn_conv_w': jnp.stack([dfcw0, dfcw1]),
        'ffn_conv_b': jnp.concatenate([dfcb0, dfcb1], axis=0), 'final_norm_g': d_fng[0],
    }
    names = list(small)
    g2 = _all_gather(_pack([dmod] + [small[n] for n in names]), "ag_small_grads")
    red = _sum_lead(g2, "sum_small_grads").reshape(-1)
    red = dict(zip(names, _unpack(red, [(2, 2, 6, D)] + [small[n].shape for n in names])[1:]))
    dmod_all = g2.reshape(N_DEV, -1)[:, :2 * 2 * 6 * D].reshape(N_DEV, 2, 2, 6 * D)

    a16g = jnp.concatenate([c_all, jnp.tile(a['c_ctx'][None], (N_DEV, 1))], axis=0)
    dm_loc = lax.dynamic_slice(dmod_all, (0, 0, 0, me * NA), (N_DEV, 2, 2, NA))
    g_wada, cpart = [], []
    for l in range(2):
        dm16 = jnp.concatenate([dm_loc[:, l, 0], dm_loc[:, l, 1]], axis=0)
        g_wada.append(mm(a16g, dm16, mode="tn", out_dtype=F32, name=f"ada_dw{l}", a_silu=True))
        cpart.append(mm(dm_loc[:, l, 1], a['w_ada'][l], mode="nt", out_dtype=F32, name=f"ada_dc{l}"))
    g_bada = _sum_lead(jnp.transpose(dmod_all, (0, 2, 1, 3)).reshape(2 * N_DEV, 2 * 6 * D // 128, 128), "sum_b_ada")
    g_cc = _all_gather(jnp.concatenate(cpart, axis=0), "ag_c_ctx")
    g_cc = _sum_lead(g_cc.reshape(2 * N_DEV * N_DEV, D // 128, 128), "sum_c_ctx").reshape(D)
    grads = {
        'c_ctx': g_cc * _dsilu(a['c_ctx']), 'w_ada': jnp.stack(g_wada), 'b_ada': g_bada.reshape(2, 6 * D),
        'b_conv_w': lax.dynamic_slice(red['b_conv_w'], (0, me * (Wb // N_DEV)), (kw, Wb // N_DEV))[None],
        'ffn_conv_w': lax.dynamic_slice(red['ffn_conv_w'], (0, 0, me * (Fd // N_DEV)), (2, 3, Fd // N_DEV)),
        'mla_q_norm_g': lax.dynamic_slice(red['mla_q_norm_g'], (0, me * (QL // N_DEV)), (1, QL // N_DEV)),
        'mla_kv_norm_g': lax.dynamic_slice(red['mla_kv_norm_g'], (0, me * (KL // N_DEV)), (1, KL // N_DEV)),
    }
    for n in names:
        if n not in grads:
            grads[n] = red[n].reshape(a[n].shape)

    recvs = {'ab_w_out': [r_about], 'mla_w_in': [r_in], 'mla_w_uq': [r_uq], 'mla_w_ukv': [r_ukv],
             'mla_w_o': [r_wo], 'ffn_w_up': [r_up0a, r_up0b, r_up1], 'ffn_w_down': [r_dn0, r_dn1]}
    out = {}
    for n in WEIGHTS:
        shp = a[n].shape
        w2 = a[n].reshape(-1, shp[-1])
        m2, v2 = a['m_' + n].reshape(w2.shape), a['v_' + n].reshape(w2.shape)
        if n in recvs:
            res = _adamw(w2, m2, v2, "adamw_" + n, recv=recvs[n])
        elif n == 'w_ada':
            g2d = grads[n].reshape(w2.shape)
            res, (r_abin_b,) = _adamw(w2, m2, v2, "adamw_" + n, g=g2d, carry=[a2a(dw_abin_b)])
            res = (g2d,) + tuple(res)
            recvs['ab_w_in'] = [r_abin_a, r_abin_b]
        else:
            g2d = grads[n].reshape(w2.shape)
            res = (g2d,) + tuple(_adamw(w2, m2, v2, "adamw_" + n, g=g2d))
        out[n] = [r.reshape(shp) for r in res]
    return (loss, dx0[:T][None], *[out[n][0] for n in WEIGHTS], *[out[n][1] for n in WEIGHTS],
            *[out[n][2] for n in WEIGHTS], *[out[n][3] for n in WEIGHTS])


def kernel(x, c, ctx, c_ctx, norm1_g, norm2_g, w_ada, b_ada, ab_w_in, ab_b_in, a_ln_g, a_ln_b, a_w_s, a_b_s, b_conv_w, b_conv_b, b_ln_g, b_ln_b, ab_w_out, mla_w_in, mla_q_norm_g, mla_w_uq, mla_kv_norm_g, mla_w_ukv, mla_w_o, ffn_w_up, ffn_conv_w, ffn_conv_b, ffn_w_down, final_norm_g, loss_target, m_c_ctx, m_norm1_g, m_norm2_g, m_w_ada, m_b_ada, m_ab_w_in, m_ab_b_in, m_a_ln_g, m_a_ln_b, m_a_w_s, m_a_b_s, m_b_conv_w, m_b_conv_b, m_b_ln_g, m_b_ln_b, m_ab_w_out, m_mla_w_in, m_mla_q_norm_g, m_mla_w_uq, m_mla_kv_norm_g, m_mla_w_ukv, m_mla_w_o, m_ffn_w_up, m_ffn_conv_w, m_ffn_conv_b, m_ffn_w_down, m_final_norm_g, v_c_ctx, v_norm1_g, v_norm2_g, v_w_ada, v_b_ada, v_ab_w_in, v_ab_b_in, v_a_ln_g, v_a_ln_b, v_a_w_s, v_a_b_s, v_b_conv_w, v_b_conv_b, v_b_ln_g, v_b_ln_b, v_ab_w_out, v_mla_w_in, v_mla_q_norm_g, v_mla_w_uq, v_mla_kv_norm_g, v_mla_w_ukv, v_mla_w_o, v_ffn_w_up, v_ffn_conv_w, v_ffn_conv_b, v_ffn_w_down, v_final_norm_g):
    return _step(dict(locals()))
```

```python
import functools
import math

import jax
import jax.numpy as jnp
from jax import lax
from jax.experimental import pallas as pl
from jax.experimental.pallas import tpu as pltpu

F32 = jnp.float32
BF16 = jnp.bfloat16
SDS = jax.ShapeDtypeStruct

N_DEV = 8
EPS = 1e-6
CHUNK = 128
NOPE = 128
ROPE = 64
VDIM = 128
GRID_W = 64
ROPE_THETA = 10000.0
HALO = 16
ADAM_LR, ADAM_B1, ADAM_B2, ADAM_EPS, ADAM_WD, ADAM_STEP = 0.001, 0.9, 0.999, 1e-08, 0.01, 10
VMEM_LIMIT = 56 * 1024 * 1024


def _tile(n, prefs):
    for p in prefs:
        if n % p == 0:
            return p
    return n


def _params(sem, vmem=VMEM_LIMIT):
    return pltpu.CompilerParams(dimension_semantics=sem, vmem_limit_bytes=vmem)


def _sigmoid(x):
    return 0.5 * jnp.tanh(0.5 * x) + 0.5


def _silu(x):
    return x * _sigmoid(x)


def _dsilu(x):
    s = _sigmoid(x)
    return s * (1.0 + x * (1.0 - s))


_GELU_C = math.sqrt(2.0 / math.pi)


def _gelu(x):
    return 0.5 * x * (1.0 + jnp.tanh(_GELU_C * (x + 0.044715 * x * x * x)))


def _dgelu(x):
    t = jnp.tanh(_GELU_C * (x + 0.044715 * x * x * x))
    return 0.5 * (1.0 + t) + 0.5 * x * (1.0 - t * t) * _GELU_C * (1.0 + 3.0 * 0.044715 * x * x)


def _colsum(v):
    return jnp.sum(v, axis=0, keepdims=True)


_SIBLING = 1
_CHIPS = (2, 4, 6)


def _xchg(x_ref, o_ref, send_sems, recv_sems, local_sem, scatter):
    ax, ay, ac = lax.axis_index("x"), lax.axis_index("y"), lax.axis_index("c")
    me = 4 * ax + 2 * ay + ac

    def dev(k):
        return ax ^ (k >> 2), ay ^ ((k >> 1) & 1), ac ^ (k & 1)

    def idx(k):
        px, py, pc = dev(k)
        return 4 * px + 2 * py + pc

    def copy(k, src, dst, to):
        return pltpu.make_async_remote_copy(src_ref=src, dst_ref=dst, send_sem=send_sems.at[k - 1],
                                            recv_sem=recv_sems.at[k - 1], device_id=dev(to),
                                            device_id_type=pl.DeviceIdType.MESH)

    def own():
        return pltpu.make_async_copy(x_ref.at[me] if scatter else x_ref, o_ref.at[me], local_sem)

    def sends():
        if scatter:
            return [copy(k, x_ref.at[idx(k)], o_ref.at[me], k) for k in range(1, N_DEV)]
        return [copy(k, x_ref, o_ref.at[me], k) for k in (_SIBLING,) + _CHIPS]

    def forwards():
        return [] if scatter else [copy(j + 1, o_ref.at[idx(j)], o_ref.at[idx(j)], _SIBLING) for j in _CHIPS]

    def arrival(k):
        return copy(k, o_ref.at[idx(k)], o_ref.at[idx(k)], k)

    return own, sends, forwards, arrival


def _xchg_start(*refs, scatter):
    own, sends, _, _ = _xchg(*refs, scatter)
    own().start()
    for cp in sends():
        cp.start()


def _xchg_forward(*refs, scatter):
    _, _, forwards, arrival = _xchg(*refs, scatter)
    if not scatter:
        for j, fw in zip(_CHIPS, forwards()):
            arrival(j).wait_recv()
            fw.start()


def _xchg_finish(*refs, scatter):
    own, sends, forwards, arrival = _xchg(*refs, scatter)
    for k in range(1, N_DEV):
        if scatter or k not in _CHIPS:
            arrival(k).wait_recv()
    for cp in sends() + forwards():
        cp.wait_send()
    own().wait()


_XCHG_SEMS = [pltpu.SemaphoreType.DMA((N_DEV - 1,)), pltpu.SemaphoreType.DMA((N_DEV - 1,)), pltpu.SemaphoreType.DMA]


def _xchg_shape(x, scatter):
    return SDS((N_DEV,) + tuple(x.shape[1:] if scatter else x.shape), x.dtype)


def _exchange(x, *, scatter, name):
    def body(*refs):
        _xchg_start(*refs, scatter=scatter)
        _xchg_forward(*refs, scatter=scatter)
        _xchg_finish(*refs, scatter=scatter)

    return pl.pallas_call(
        body, out_shape=_xchg_shape(x, scatter),
        in_specs=[pl.BlockSpec(memory_space=pl.ANY)], out_specs=pl.BlockSpec(memory_space=pl.ANY),
        scratch_shapes=list(_XCHG_SEMS), name=name)(x)


def _carried(body, carry, n_in, n_out, n_scratch, grid):
    nc = len(carry)
    total = math.prod(grid)
    mid = (3 * total) // 4

    def wrapped(*refs):
        ins, cin = refs[:n_in], refs[n_in:n_in + nc]
        o0 = n_in + nc
        outs, cout = refs[o0:o0 + n_out], refs[o0 + n_out:o0 + n_out + nc]
        scr = refs[o0 + n_out + nc:]
        sems = scr[n_scratch:]
        step = pl.program_id(0)
        for ax in range(1, len(grid)):
            step = step * grid[ax] + pl.program_id(ax)

        def each(fn):
            for c in range(nc):
                fn(cin[c], cout[c], *sems[3 * c:3 * c + 3], scatter=carry[c][1])

        @pl.when(step == 0)
        def _():
            each(_xchg_start)

        body(*ins, *outs, *scr[:n_scratch])

        if mid < total - 1:
            @pl.when(step == mid)
            def _():
                each(_xchg_forward)

        @pl.when(step == total - 1)
        def _():
            if mid >= total - 1:
                each(_xchg_forward)
            each(_xchg_finish)

    return wrapped


def _carry_call(body, carry, *, grid, out_shape, in_specs, out_specs, scratch_shapes, sem, name, ins, aliases=None):
    carry = carry or []
    nc = len(carry)
    if nc:
        body = _carried(body, carry, len(in_specs), len(out_shape), len(scratch_shapes), grid)
        anyspec = pl.BlockSpec(memory_space=pl.ANY)
        in_specs = list(in_specs) + [anyspec] * nc
        out_specs = list(out_specs) + [anyspec] * nc
        out_shape = list(out_shape) + [_xchg_shape(x, sc) for x, sc in carry]
        scratch_shapes = list(scratch_shapes) + list(_XCHG_SEMS) * nc
        ins = list(ins) + [x for x, _ in carry]
        sem = ("arbitrary",) * len(grid)
    out = pl.pallas_call(body, out_shape=out_shape, grid=grid, in_specs=in_specs, out_specs=out_specs,
                         scratch_shapes=scratch_shapes, compiler_params=_params(sem), name=name,
                         input_output_aliases=aliases or {})(*ins)
    n_main = len(out) - nc
    return list(out[:n_main]), list(out[n_main:])


def _all_gather(x, name):
    return _exchange(x, scatter=False, name=name)


def _all_to_all(x, name):
    return _exchange(x, scatter=True, name=name)


def _sum_lead(x, name):
    n, R, C = x.shape
    tr = _tile(R, (512, 256, 128, 64, 32, 16, 8))

    def body(x_ref, o_ref):
        acc = x_ref[0]
        for d in range(1, n):
            acc = acc + x_ref[d]
        o_ref[...] = acc

    return pl.pallas_call(
        body, out_shape=SDS((R, C), F32), grid=(R // tr,),
        in_specs=[pl.BlockSpec((n, tr, C), lambda i: (0, i, 0))], out_specs=pl.BlockSpec((tr, C), lambda i: (i, 0)),
        compiler_params=_params(("parallel",)), name=name)(x)


_TP = (1408, 1088, 1024, 768, 512, 256, 128)
_TQ = (1408, 1024, 768, 512, 256, 128)
_TR = (2048, 1408, 1024, 768, 512, 256, 128)
_TR_TN = (2176, 2048, 1088, 1024, 512, 256, 128)


def _mm(a, b, *, mode, out_dtype, name, rows=None, bias=None, res=None, gate=None, seg_t=None, a_silu=False, carry=None,
        b_dev=False, out_dev=False, p_range=None, halves=False, tiles=None, pair=False):
    if mode == "nn":
        P, R, Q = rows or a.shape[0], a.shape[1], (b.shape[0] * b.shape[2] if b_dev else b.shape[1])
    elif mode == "nt":
        P, R, Q = rows or a.shape[-2], (2 * a.shape[2] if halves else a.shape[1]), (b.shape[1] if b_dev else b.shape[0])
    else:
        R, P, Q = rows or a.shape[0], a.shape[1], (2 * b.shape[2] if halves else b.shape[1])
    p0 = 0
    if p_range is not None:
        p0, P = p_range
    tp = _tile(P, _TP)
    tq = _tile(Q // N_DEV if (out_dev or (b_dev and mode == "nn")) else Q, _TQ)
    tr = _tile(R // N_DEV if (b_dev and mode == "nt") else R, _TR if mode != "tn" else _TR_TN)
    if tiles is not None:
        tp, tq, tr = (o or d for o, d in zip(tiles, (tp, tq, tr)))
    if pair:
        assert mode == "nt" and b_dev and tr == R // N_DEV
        tr = 2 * tr
    nk = R // tr
    qd = (Q // N_DEV) // tq
    rd = (R // N_DEV) // tr
    if mode == "nn":
        a_spec = pl.BlockSpec((tp, tr), lambda i, j, k: (i, k))
        b_spec = (pl.BlockSpec((None, tr, tq), lambda i, j, k: (j // qd, k, j % qd)) if b_dev
                  else pl.BlockSpec((tr, tq), lambda i, j, k: (k, j)))
        dims = (((1,), (0,)), ((), ()))
    elif mode == "nt":
        kh = (R // 2) // tr
        a_spec = (pl.BlockSpec((None, tp, tr), lambda i, j, k: (k // kh, i, k % kh)) if halves
                  else pl.BlockSpec((tp, tr), lambda i, j, k: (i, k)))
        if pair:
            b_spec = pl.BlockSpec((2, tq, tr // 2), lambda i, j, k: (k, j, 0))
        else:
            b_spec = (pl.BlockSpec((None, tq, tr), lambda i, j, k: (k // rd, j, k % rd)) if b_dev
                      else pl.BlockSpec((tq, tr), lambda i, j, k: (j, k)))
        dims = (((1,), (1,)), ((), ()))
    else:
        pb = p0 // tp
        qh = (Q // 2) // tq
        a_spec = pl.BlockSpec((tr, tp), lambda i, j, k: (k, i + pb))
        b_spec = (pl.BlockSpec((None, tr, tq), lambda i, j, k: (j // qh, k, j % qh)) if halves
                  else pl.BlockSpec((tr, tq), lambda i, j, k: (k, j)))
        dims = (((0,), (0,)), ((), ()))
    ins, in_specs = [a, b], [a_spec, b_spec]
    if bias is not None:
        ins.append(bias)
        in_specs.append(pl.BlockSpec((1, tq), lambda i, j, k: (0, j)))
    gated = res is not None
    if gated:
        n_seg = gate.shape[0]
        ins += [res, gate]
        in_specs += [pl.BlockSpec((tp, tq), lambda i, j, k: (i, j)),
                     pl.BlockSpec((n_seg, 1, tq), lambda i, j, k: (0, 0, j))]
    if out_dev:
        out_shape = [SDS((N_DEV, P, Q // N_DEV), out_dtype)]
        out_specs = [pl.BlockSpec((None, tp, tq), lambda i, j, k: (j // qd, i, j % qd))]
    else:
        out_shape = [SDS((P, Q), out_dtype)]
        out_specs = [pl.BlockSpec((tp, tq), lambda i, j, k: (i, j))]
    if gated:
        out_shape.append(SDS((P, Q), BF16))
        out_specs.append(pl.BlockSpec((tp, tq), lambda i, j, k: (i, j)))

    def body(*refs):
        a_ref, b_ref = refs[0], refs[1]
        pos = 2
        bias_ref = res_ref = gate_ref = o2_ref = None
        if bias is not None:
            bias_ref = refs[pos]
            pos += 1
        if gated:
            res_ref, gate_ref = refs[pos], refs[pos + 1]
            pos += 2
        o_ref = refs[pos]
        pos += 1
        if gated:
            o2_ref = refs[pos]
            pos += 1
        acc_ref = refs[pos] if nk > 1 else None
        k = pl.program_id(2)
        av = a_ref[...]
        if a_silu:
            av = _silu(av.astype(F32))
        bv = jnp.concatenate([b_ref[0], b_ref[1]], axis=1) if pair else b_ref[...]
        part = lax.dot_general(av.astype(BF16), bv.astype(BF16), dims, preferred_element_type=F32)
        if nk > 1:
            @pl.when(k == 0)
            def _():
                acc_ref[...] = part

            @pl.when(k > 0)
            def _():
                acc_ref[...] += part

        @pl.when(k == nk - 1)
        def _():
            acc = acc_ref[...] if nk > 1 else part
            if bias_ref is not None:
                acc = acc + bias_ref[...]
            if gated:
                if n_seg == 1:
                    g = gate_ref[0]
                else:
                    row = pl.program_id(0) * tp + lax.broadcasted_iota(jnp.int32, (tp, 1), 0)
                    g = jnp.where(row < seg_t, gate_ref[0], gate_ref[1])
                o_ref[...] = (res_ref[...] + g * acc).astype(o_ref.dtype)
                o2_ref[...] = acc.astype(BF16)
            else:
                o_ref[...] = acc.astype(o_ref.dtype)

    out, carried = _carry_call(
        body, carry, grid=(P // tp, Q // tq, nk), out_shape=out_shape, in_specs=in_specs, out_specs=out_specs,
        scratch_shapes=[pltpu.VMEM((tp, tq), F32)] if nk > 1 else [], sem=("parallel", "parallel", "arbitrary"),
        name=name, ins=ins)
    res_out = tuple(out) if gated else out[0]
    return (res_out, carried) if carry else res_out


def _row_tile(seg_t, m):
    return 256 if (seg_t % 256 == 0 and m % 256 == 0) else 128


def _normmod_fwd(x, g, sh, sc, seg_t, name, carry=None):
    M, D = x.shape
    tm = _row_tile(seg_t, M)
    n_seg = sh.shape[0]
    nt = seg_t // tm

    def seg(i):
        return ((i >= nt).astype(jnp.int32) if n_seg == 2 else 0, 0, 0)

    def body(x_ref, g_ref, sh_ref, sc_ref, o_ref):
        xv = x_ref[...]
        r = lax.rsqrt(jnp.mean(xv * xv, axis=-1, keepdims=True) + EPS)
        y = xv * r * g_ref[...]
        o_ref[...] = (y * (1.0 + sc_ref[0]) + sh_ref[0]).astype(BF16)

    out, carried = _carry_call(
        body, carry, out_shape=[SDS((M, D), BF16)], grid=(M // tm,),
        in_specs=[pl.BlockSpec((tm, D), lambda i: (i, 0)), pl.BlockSpec((1, D), lambda i: (0, 0)),
                  pl.BlockSpec((1, 1, D), seg), pl.BlockSpec((1, 1, D), seg)],
        out_specs=[pl.BlockSpec((tm, D), lambda i: (i, 0))], scratch_shapes=[],
        sem=("parallel",), name=name, ins=(x, g, sh, sc))
    return (out[0], carried) if carry else out[0]


def _normmod_bwd(x, g, sc, dh, dx_in, seg_t, name, o_prev=None, gate_prev=None):
    M, D = x.shape
    tm = _row_tile(seg_t, M)
    n_seg = sc.shape[0]
    nt = seg_t // tm
    n_in = dx_in.shape[0] // tm
    with_prev = o_prev is not None
    n_segp = gate_prev.shape[0] if with_prev else 0

    def seg(i):
        return ((i >= nt).astype(jnp.int32) if n_seg == 2 else 0, 0, 0)

    def segp(i):
        return ((i >= nt).astype(jnp.int32) if n_segp == 2 else 0, 0, 0)

    def body(*refs):
        x_ref, g_ref, sc_ref, dh_ref, dxin_ref = refs[:5]
        pos = 5
        if with_prev:
            op_ref, gp_ref = refs[5], refs[6]
            pos = 7
        dx_ref, dg_ref, dsh_ref, dsc_ref = refs[pos:pos + 4]
        if with_prev:
            dop_ref, dgp_ref = refs[pos + 4], refs[pos + 5]
        i = pl.program_id(0)
        xv = x_ref[...]
        r = lax.rsqrt(jnp.mean(xv * xv, axis=-1, keepdims=True) + EPS)
        xh = xv * r
        gv = g_ref[...]
        dhv = dh_ref[...].astype(F32)
        dy = dhv * (1.0 + sc_ref[0])
        dxh = dy * gv
        dxv = r * (dxh - xh * jnp.mean(dxh * xh, axis=-1, keepdims=True))
        if n_in * tm < M:
            dxv = dxv + jnp.where(i < n_in, dxin_ref[...], 0.0)
        else:
            dxv = dxv + dxin_ref[...]
        dx_ref[...] = dxv

        @pl.when(i == 0)
        def _():
            dg_ref[...] = jnp.zeros_like(dg_ref)

        first_of_seg = (i == 0) | (i == nt) if n_seg == 2 else (i == 0)

        @pl.when(first_of_seg)
        def _():
            dsh_ref[...] = jnp.zeros_like(dsh_ref)
            dsc_ref[...] = jnp.zeros_like(dsc_ref)

        dg_ref[...] += _colsum(dy * xh)
        dsh_ref[0] += _colsum(dhv)
        dsc_ref[0] += _colsum(dhv * xh * gv)
        if with_prev:
            first_of_segp = (i == 0) | (i == nt) if n_segp == 2 else (i == 0)

            @pl.when(first_of_segp)
            def _():
                dgp_ref[...] = jnp.zeros_like(dgp_ref)

            dop_ref[...] = (gp_ref[0] * dxv).astype(BF16)
            dgp_ref[0] += _colsum(dxv * op_ref[...].astype(F32))

    row = pl.BlockSpec((tm, D), lambda i: (i, 0))
    ins = [x, g, sc, dh, dx_in]
    in_specs = [row, pl.BlockSpec((1, D), lambda i: (0, 0)), pl.BlockSpec((1, 1, D), seg), row,
                pl.BlockSpec((tm, D), lambda i: (jnp.minimum(i, n_in - 1), 0))]
    out_shape = [SDS((M, D), F32), SDS((1, D), F32), SDS((n_seg, 1, D), F32), SDS((n_seg, 1, D), F32)]
    out_specs = [row, pl.BlockSpec((1, D), lambda i: (0, 0)), pl.BlockSpec((1, 1, D), seg), pl.BlockSpec((1, 1, D), seg)]
    if with_prev:
        ins += [o_prev, gate_prev]
        in_specs += [row, pl.BlockSpec((1, 1, D), segp)]
        out_shape += [SDS((M, D), BF16), SDS((n_segp, 1, D), F32)]
        out_specs += [row, pl.BlockSpec((1, 1, D), segp)]
    return pl.pallas_call(
        body, out_shape=out_shape, grid=(M // tm,), in_specs=in_specs, out_specs=out_specs,
        compiler_params=_params(("arbitrary",)), name=name)(*ins)


def _final(x, g, target, o_prev, gate_prev, name):
    T, D = x.shape
    tm = _tile(T, (256, 128))

    def body(x_ref, g_ref, t_ref, op_ref, gp_ref, dx_ref, loss_ref, dg_ref, dop_ref, dgp_ref):
        i = pl.program_id(0)
        xv = x_ref[...]
        r = lax.rsqrt(jnp.mean(xv * xv, axis=-1, keepdims=True) + EPS)
        xh = xv * r
        gv = g_ref[...]
        e = xh * gv - t_ref[...]
        dout = e * (1.0 / D)
        dxh = dout * gv
        dxv = r * (dxh - xh * jnp.mean(dxh * xh, axis=-1, keepdims=True))
        dx_ref[...] = dxv
        dop_ref[...] = (gp_ref[0] * dxv).astype(BF16)

        @pl.when(i == 0)
        def _():
            loss_ref[...] = jnp.zeros_like(loss_ref)
            dg_ref[...] = jnp.zeros_like(dg_ref)
            dgp_ref[...] = jnp.zeros_like(dgp_ref)

        loss_ref[...] += _colsum(e * e) * (0.5 / D)
        dg_ref[...] += _colsum(dout * xh)
        dgp_ref[0] += _colsum(dxv * op_ref[...].astype(F32))

    row = pl.BlockSpec((tm, D), lambda i: (i, 0))
    vec = pl.BlockSpec((1, D), lambda i: (0, 0))
    vec3 = pl.BlockSpec((1, 1, D), lambda i: (0, 0, 0))
    return pl.pallas_call(
        body, out_shape=[SDS((T, D), F32), SDS((1, D), F32), SDS((1, D), F32), SDS((T, D), BF16), SDS((1, 1, D), F32)],
        grid=(T // tm,), in_specs=[row, vec, row, row, vec3], out_specs=[row, vec, vec, row, vec3],
        compiler_params=_params(("arbitrary",)), name=name)(x, g, target, o_prev, gate_prev)


def _gmlp_core(z, lg, lb, ws_ref, bst):
    W = z.shape[1] // 2
    t = _gelu(z)
    u, v = t[:, :W], t[:, W:]
    mu = jnp.mean(v, axis=-1, keepdims=True)
    vc = v - mu
    rstd = lax.rsqrt(jnp.mean(vc * vc, axis=-1, keepdims=True) + EPS)
    vhat = vc * rstd
    vn = vhat * lg + lb
    vp = []
    for h in range(W // CHUNK):
        blk = vn[:, h * CHUNK:(h + 1) * CHUNK].astype(BF16)
        vp.append(jnp.dot(ws_ref[h].astype(BF16), blk, preferred_element_type=F32) + bst[:, h:h + 1])
    return u, vhat, rstd, vp


def _gmlp_fwd(z, ln_g, ln_b, w_s, b_st, W, name):
    M = z.shape[0]
    H = W // CHUNK

    def body(z_ref, lg_ref, lb_ref, ws_ref, bst_ref, o_ref):
        u, _, _, vp = _gmlp_core(z_ref[...], lg_ref[...], lb_ref[...], ws_ref, bst_ref[...])
        for h in range(H):
            o_ref[:, h * CHUNK:(h + 1) * CHUNK] = (u[:, h * CHUNK:(h + 1) * CHUNK] * vp[h]).astype(BF16)

    vec = pl.BlockSpec((1, W), lambda i: (0, 0))
    return pl.pallas_call(
        body, out_shape=SDS((M, 2 * W), BF16), grid=(M // CHUNK,),
        in_specs=[pl.BlockSpec((CHUNK, 2 * W), lambda i: (i, 0)), vec, vec,
                  pl.BlockSpec((H, CHUNK, CHUNK), lambda i: (0, 0, 0)), pl.BlockSpec((CHUNK, H), lambda i: (0, 0))],
        out_specs=pl.BlockSpec((CHUNK, W), lambda i: (i, 0)),
        compiler_params=_params(("parallel",)), name=name)(z, ln_g, ln_b, w_s, b_st)


def _gmlp_bwd(z, dy, ln_g, ln_b, w_s, b_st, W, name):
    M = z.shape[0]
    H = W // CHUNK
    ZW = z.shape[1]

    def body(z_ref, dy_ref, lg_ref, lb_ref, ws_ref, bst_ref, dz_ref, dlg_ref, dlb_ref, dws_ref, dbs_ref, dbin_ref):
        i = pl.program_id(0)

        @pl.when(i == 0)
        def _():
            for r in (dlg_ref, dlb_ref, dws_ref, dbs_ref, dbin_ref):
                r[...] = jnp.zeros_like(r)

        zv = z_ref[...]
        lg = lg_ref[...]
        u, vhat, rstd, vp = _gmlp_core(zv, lg, lb_ref[...], ws_ref, bst_ref[...])
        vn = vhat * lg + lb_ref[...]
        dya = dy_ref[...]
        du_parts, dvn_parts = [], []
        for h in range(H):
            sl = slice(h * CHUNK, (h + 1) * CHUNK)
            dya_h = dya[:, sl]
            du_parts.append(dya_h * vp[h])
            dvp = dya_h * u[:, sl]
            dbs_ref[h] += dvp
            dvp16 = dvp.astype(BF16)
            dws_ref[h] += lax.dot_general(dvp16, vn[:, sl].astype(BF16), (((1,), (1,)), ((), ())),
                                          preferred_element_type=F32)
            dvn_parts.append(lax.dot_general(ws_ref[h].astype(BF16), dvp16, (((0,), (0,)), ((), ())),
                                             preferred_element_type=F32))
        du = jnp.concatenate(du_parts, axis=1)
        dvn = jnp.concatenate(dvn_parts, axis=1)
        dlg_ref[...] += _colsum(dvn * vhat)
        dlb_ref[...] += _colsum(dvn)
        dvh = dvn * lg
        dv = rstd * (dvh - jnp.mean(dvh, axis=-1, keepdims=True) - vhat * jnp.mean(dvh * vhat, axis=-1, keepdims=True))
        dz = jnp.concatenate([du, dv], axis=1) * _dgelu(zv)
        dbin_ref[...] += _colsum(dz)
        dz_ref[...] = dz.astype(BF16)

    vec = pl.BlockSpec((1, W), lambda i: (0, 0))
    mat = pl.BlockSpec((H, CHUNK, CHUNK), lambda i: (0, 0, 0))
    return pl.pallas_call(
        body,
        out_shape=[SDS((M, ZW), BF16), SDS((1, W), F32), SDS((1, W), F32), SDS((H, CHUNK, CHUNK), F32),
                   SDS((H, CHUNK, CHUNK), F32), SDS((1, 2 * W), F32)],
        grid=(M // CHUNK,),
        in_specs=[pl.BlockSpec((CHUNK, 2 * W), lambda i: (i, 0)), pl.BlockSpec((CHUNK, W), lambda i: (i, 0)), vec, vec,
                  mat, pl.BlockSpec((CHUNK, H), lambda i: (0, 0))],
        out_specs=[pl.BlockSpec((CHUNK, 2 * W), lambda i: (i, 0)), vec, vec, mat, mat,
                   pl.BlockSpec((1, 2 * W), lambda i: (0, 0))],
        compiler_params=_params(("arbitrary",)), name=name)(z, dy, ln_g, ln_b, w_s, b_st)


def _halo_specs(tm, width, col, n_rows):
    per = tm // HALO
    last = n_rows // HALO - 1
    prev = pl.BlockSpec((HALO, width), lambda i: (jnp.maximum(i * per - 1, 0), col))
    nxt = pl.BlockSpec((HALO, width), lambda i: (jnp.minimum((i + 1) * per, last), col))
    return prev, nxt


def _edge_flags(i, tm, seg_t, m):
    r0 = i * tm
    has_prev = jnp.where((r0 == 0) | (r0 == seg_t), 0.0, 1.0)
    has_next = jnp.where((r0 + tm == seg_t) | (r0 + tm == m), 0.0, 1.0)
    return has_prev, has_next


def _glu(zz, wb):
    return zz[:, :wb] * _sigmoid(zz[:, wb:])


def _build_shifts(src_ref, sh_ref):
    n = src_ref.shape[0] - 8
    for r in range(1, 8):
        sh_ref[r - 1, pl.ds(0, n), :] = src_ref[pl.ds(r, n), :]


def _shifted(src_ref, sh_ref, off, r0, rc):
    a, r = divmod(off, 8)
    if r == 0:
        return src_ref[pl.ds(8 * a + r0, rc), :]
    return sh_ref[r - 1, pl.ds(8 * a + r0, rc), :]


def _conv_taps(src_ref, sh_ref, w_ref, first, tm, kw, flip=False):
    rc = 32
    parts = []
    for c in range(tm // rc):
        acc = None
        for k in range(kw):
            wk = w_ref[pl.ds(kw - 1 - k if flip else k, 1), :]
            term = _shifted(src_ref, sh_ref, first + k, c * rc, rc) * wk
            acc = term if acc is None else acc + term
        parts.append(acc)
    return jnp.concatenate(parts, axis=0)


def _conf_fwd(z, y, conv_w, conv_b, ln_g, ln_b, W, Wb, seg_t, name, carry=None):
    M = z.shape[0]
    tm = _row_tile(seg_t, M)
    kw = conv_w.shape[0]
    pad = (kw - 1) // 2
    col = (2 * W) // (2 * Wb)

    def body(zc_ref, zp_ref, zn_ref, y_hbm, cw_ref, cb_ref, lg_ref, lb_ref, o_ref, hc_ref, hs_ref, sh_ref):
        del y_hbm
        hp, hn = _edge_flags(pl.program_id(0), tm, seg_t, M)
        hs_ref[pl.ds(0, HALO), :] = _glu(zp_ref[...], Wb) * hp
        hs_ref[pl.ds(HALO, tm), :] = _glu(zc_ref[...], Wb)
        hs_ref[pl.ds(HALO + tm, HALO), :] = _glu(zn_ref[...], Wb) * hn
        _build_shifts(hs_ref, sh_ref)
        hc = _conv_taps(hs_ref, sh_ref, cw_ref, HALO - pad, tm, kw) + cb_ref[...]
        hc_ref[...] = hc
        mu = jnp.mean(hc, axis=-1, keepdims=True)
        c = hc - mu
        rstd = lax.rsqrt(jnp.mean(c * c, axis=-1, keepdims=True) + EPS)
        o_ref[...] = _silu(c * rstd * lg_ref[...] + lb_ref[...]).astype(BF16)

    prev, nxt = _halo_specs(tm, 2 * Wb, col, M)
    vec = pl.BlockSpec((1, Wb), lambda i: (0, 0))
    out, carried = _carry_call(
        body, carry, out_shape=[SDS(y.shape, BF16), SDS((M, Wb), F32)], grid=(M // tm,),
        in_specs=[pl.BlockSpec((tm, 2 * Wb), lambda i: (i, col)), prev, nxt, pl.BlockSpec(memory_space=pl.ANY),
                  pl.BlockSpec((kw, Wb), lambda i: (0, 0)), vec, vec, vec],
        out_specs=[pl.BlockSpec((tm, Wb), lambda i: (i, W // Wb)), pl.BlockSpec((tm, Wb), lambda i: (i, 0))],
        scratch_shapes=[pltpu.VMEM((tm + 2 * HALO, Wb), F32), pltpu.VMEM((7, tm + 2 * HALO, Wb), F32)],
        aliases={3: 0}, sem=("parallel",), name=name, ins=(z, z, z, y, conv_w, conv_b, ln_g, ln_b))
    return (out, carried) if carry else out


def _conf_bwd1(hc, dy, ln_g, ln_b, W, Wb, seg_t, name):
    M = hc.shape[0]
    tm = _row_tile(seg_t, M)

    def body(hc_ref, dy_ref, lg_ref, lb_ref, dhc_ref, dlg_ref, dlb_ref, dcb_ref):
        i = pl.program_id(0)

        @pl.when(i == 0)
        def _():
            for r in (dlg_ref, dlb_ref, dcb_ref):
                r[...] = jnp.zeros_like(r)

        hc = hc_ref[...]
        mu = jnp.mean(hc, axis=-1, keepdims=True)
        c = hc - mu
        rstd = lax.rsqrt(jnp.mean(c * c, axis=-1, keepdims=True) + EPS)
        hh = c * rstd
        lg = lg_ref[...]
        dhn = dy_ref[...] * _dsilu(hh * lg + lb_ref[...])
        dlg_ref[...] += _colsum(dhn * hh)
        dlb_ref[...] += _colsum(dhn)
        dhh = dhn * lg
        dhc = rstd * (dhh - jnp.mean(dhh, axis=-1, keepdims=True) - hh * jnp.mean(dhh * hh, axis=-1, keepdims=True))
        dcb_ref[...] += _colsum(dhc)
        dhc_ref[...] = dhc

    vec = pl.BlockSpec((1, Wb), lambda i: (0, 0))
    return pl.pallas_call(
        body, out_shape=[SDS((M, Wb), F32), SDS((1, Wb), F32), SDS((1, Wb), F32), SDS((1, Wb), F32)], grid=(M // tm,),
        in_specs=[pl.BlockSpec((tm, Wb), lambda i: (i, 0)), pl.BlockSpec((tm, Wb), lambda i: (i, W // Wb)), vec, vec],
        out_specs=[pl.BlockSpec((tm, Wb), lambda i: (i, 0)), vec, vec, vec],
        compiler_params=_params(("arbitrary",)), name=name)(hc, dy, ln_g, ln_b)


def _conf_bwd2(z, dhc, dz, conv_w, W, Wb, seg_t, name, carry=None):
    M = z.shape[0]
    tm = _row_tile(seg_t, M)
    kw = conv_w.shape[0]
    pad = (kw - 1) // 2
    col = (2 * W) // (2 * Wb)

    def body(zc_ref, zp_ref, zn_ref, dc_ref, dp_ref, dn_ref, dz_hbm, cw_ref, dz_ref, dcw_ref, dbin_ref, hs_ref, ds_ref,
             hsh_ref, dsh_ref):
        del dz_hbm
        i = pl.program_id(0)

        @pl.when(i == 0)
        def _():
            dcw_ref[...] = jnp.zeros_like(dcw_ref)
            dbin_ref[...] = jnp.zeros_like(dbin_ref)

        hp, hn = _edge_flags(i, tm, seg_t, M)
        zc = zc_ref[...]
        hs_ref[pl.ds(0, HALO), :] = _glu(zp_ref[...], Wb) * hp
        hs_ref[pl.ds(HALO, tm), :] = _glu(zc, Wb)
        hs_ref[pl.ds(HALO + tm, HALO), :] = _glu(zn_ref[...], Wb) * hn
        dcur = dc_ref[...]
        ds_ref[pl.ds(0, HALO), :] = dp_ref[...] * hp
        ds_ref[pl.ds(HALO, tm), :] = dcur
        ds_ref[pl.ds(HALO + tm, HALO), :] = dn_ref[...] * hn
        _build_shifts(ds_ref, dsh_ref)
        _build_shifts(hs_ref, hsh_ref)
        dh = _conv_taps(ds_ref, dsh_ref, cw_ref, HALO - pad, tm, kw, flip=True)
        for k in range(kw):
            dcw_ref[pl.ds(k, 1), :] += _colsum(dcur * _shifted(hs_ref, hsh_ref, HALO - pad + k, 0, tm))
        a, gt = zc[:, :Wb], zc[:, Wb:]
        s = _sigmoid(gt)
        dz = jnp.concatenate([dh * s, dh * a * s * (1.0 - s)], axis=1)
        dbin_ref[...] += _colsum(dz)
        dz_ref[...] = dz.astype(BF16)

    prev, nxt = _halo_specs(tm, 2 * Wb, col, M)
    dprev, dnxt = _halo_specs(tm, Wb, 0, M)
    out, carried = _carry_call(
        body, carry, out_shape=[SDS(dz.shape, BF16), SDS((kw, Wb), F32), SDS((1, 2 * Wb), F32)], grid=(M // tm,),
        in_specs=[pl.BlockSpec((tm, 2 * Wb), lambda i: (i, col)), prev, nxt,
                  pl.BlockSpec((tm, Wb), lambda i: (i, 0)), dprev, dnxt, pl.BlockSpec(memory_space=pl.ANY),
                  pl.BlockSpec((kw, Wb), lambda i: (0, 0))],
        out_specs=[pl.BlockSpec((tm, 2 * Wb), lambda i: (i, col)), pl.BlockSpec((kw, Wb), lambda i: (0, 0)),
                   pl.BlockSpec((1, 2 * Wb), lambda i: (0, 0))],
        scratch_shapes=[pltpu.VMEM((tm + 2 * HALO, Wb), F32), pltpu.VMEM((tm + 2 * HALO, Wb), F32),
                        pltpu.VMEM((7, tm + 2 * HALO, Wb), F32), pltpu.VMEM((7, tm + 2 * HALO, Wb), F32)],
        aliases={6: 0}, sem=("arbitrary",), name=name, ins=(z, z, z, dhc, dhc, dhc, dz, conv_w))
    return (out, carried) if carry else out


_TF = (1408, 512, 256, 128)
_RC = 16
_CG = 256


def _col_groups(width):
    return [(c0, min(_CG, width - c0)) for c0 in range(0, width, _CG)]


def _ffn_act_fwd(z, conv_w, conv_b, seg_t, name):
    M, F2 = z.shape
    Fd = F2 // 2
    tm = _row_tile(seg_t, M)
    tf = _tile(Fd, _TF)
    nf = Fd // tf
    per, last = tm // HALO, M // HALO - 1

    def body(g_ref, gp_ref, gn_ref, u_ref, cw_ref, cb_ref, o_ref, gc_ref, gs_ref):
        hp, hn = _edge_flags(pl.program_id(0), tm, seg_t, M)
        gs_ref[pl.ds(0, HALO), :] = gp_ref[...].astype(F32) * hp
        gs_ref[pl.ds(HALO, tm), :] = g_ref[...].astype(F32)
        gs_ref[pl.ds(HALO + tm, HALO), :] = gn_ref[...].astype(F32) * hn
        for c0, cw in _col_groups(tf):
            cs = pl.ds(c0, cw)
            w0, w1, w2, cb = cw_ref[pl.ds(0, 1), cs], cw_ref[pl.ds(1, 1), cs], cw_ref[pl.ds(2, 1), cs], cb_ref[:, cs]
            for r0 in range(0, tm, _RC):
                gc = (gs_ref[pl.ds(HALO - 1 + r0, _RC), cs] * w0 + gs_ref[pl.ds(HALO + r0, _RC), cs] * w1
                      + gs_ref[pl.ds(HALO + 1 + r0, _RC), cs] * w2 + cb)
                o_ref[pl.ds(r0, _RC), cs] = (_silu(gc) * u_ref[pl.ds(r0, _RC), cs].astype(F32)).astype(BF16)
                gc_ref[pl.ds(r0, _RC), cs] = gc.astype(BF16)

    return pl.pallas_call(
        body, out_shape=[SDS((M, Fd), BF16), SDS((M, Fd), BF16)], grid=(M // tm, nf),
        in_specs=[pl.BlockSpec((tm, tf), lambda i, j: (i, j)),
                  pl.BlockSpec((HALO, tf), lambda i, j: (jnp.maximum(i * per - 1, 0), j)),
                  pl.BlockSpec((HALO, tf), lambda i, j: (jnp.minimum((i + 1) * per, last), j)),
                  pl.BlockSpec((tm, tf), lambda i, j: (i, nf + j)),
                  pl.BlockSpec((3, tf), lambda i, j: (0, j)), pl.BlockSpec((1, tf), lambda i, j: (0, j))],
        out_specs=[pl.BlockSpec((tm, tf), lambda i, j: (i, j)), pl.BlockSpec((tm, tf), lambda i, j: (i, j))],
        scratch_shapes=[pltpu.VMEM((tm + 2 * HALO, tf), F32)],
        compiler_params=_params(("parallel", "parallel")), name=name)(z, z, z, z, conv_w, conv_b)


def _ffn_act_bwd(z, gc, da, conv_w, seg_t, name):
    M, F2 = z.shape
    Fd = F2 // 2
    tm = _row_tile(seg_t, M)
    tf = _tile(Fd, _TF)
    nf = Fd // tf
    per, last = tm // HALO, M // HALO - 1
    n_piece = tm // _RC

    def body(g_ref, c_ref, cp_ref, cn_ref, u_ref, up_ref, un_ref, a_ref, ap_ref, an_ref, cw_ref,
             dz_ref, dcw_ref, dcb_ref, ds_ref):
        i = pl.program_id(1)

        def tile():
            hp, hn = _edge_flags(i, tm, seg_t, M)

            @pl.when(i == 0)
            def _():
                dcw_ref[...] = jnp.zeros_like(dcw_ref)
                dcb_ref[...] = jnp.zeros_like(dcb_ref)

            def fold(v):
                return v[:8] + v[8:]

            for c0, cw in _col_groups(tf):
                cs = pl.ds(c0, cw)
                w0, w1, w2 = cw_ref[pl.ds(0, 1), cs], cw_ref[pl.ds(1, 1), cs], cw_ref[pl.ds(2, 1), cs]
                for ci in range(-1, n_piece + 1):
                    r0 = ci * _RC
                    if ci < 0:
                        gcv, ue, ae = cp_ref[:, cs], up_ref[:, cs], ap_ref[:, cs].astype(F32) * hp
                    elif ci == n_piece:
                        gcv, ue, ae = cn_ref[:, cs], un_ref[:, cs], an_ref[:, cs].astype(F32) * hn
                    else:
                        rows = pl.ds(r0, _RC)
                        gcv, ue, ae = c_ref[rows, cs], u_ref[rows, cs], a_ref[rows, cs].astype(F32)
                    gcv, ue = gcv.astype(F32), ue.astype(F32)
                    sg = _sigmoid(gcv)
                    t = ae * sg
                    ds_ref[pl.ds(HALO + r0, _RC), cs] = t * ue * (1.0 + gcv * (1.0 - sg))
                    if 0 <= ci < n_piece:
                        dz_ref[1, pl.ds(r0, _RC), cs] = (t * gcv).astype(BF16)
                acc = [jnp.zeros((8, cw), F32) for _ in range(4)]
                for r0 in range(0, tm, _RC):
                    b = HALO + r0
                    d = [ds_ref[pl.ds(b + 1 - k, _RC), cs] for k in range(3)]
                    dz_ref[0, pl.ds(r0, _RC), cs] = (d[0] * w0 + d[1] * w1 + d[2] * w2).astype(BF16)
                    gv = g_ref[pl.ds(r0, _RC), cs].astype(F32)
                    for k in range(3):
                        acc[k] = acc[k] + fold(d[k] * gv)
                    acc[3] = acc[3] + fold(d[1])
                for k in range(3):
                    dcw_ref[pl.ds(k, 1), cs] += _colsum(acc[k])
                dcb_ref[:, cs] += _colsum(acc[3])

        tile()

    def cur(off):
        return pl.BlockSpec((tm, tf), lambda j, i: (i, off + j))

    def prv(off):
        return pl.BlockSpec((HALO, tf), lambda j, i: (jnp.maximum(i * per - 1, 0), off + j))

    def nxt(off):
        return pl.BlockSpec((HALO, tf), lambda j, i: (jnp.minimum((i + 1) * per, last), off + j))

    return pl.pallas_call(
        body, out_shape=[SDS((2, M, Fd), BF16), SDS((3, Fd), F32), SDS((1, Fd), F32)], grid=(nf, M // tm),
        in_specs=[cur(0), cur(0), prv(0), nxt(0), cur(nf), prv(nf), nxt(nf), cur(0), prv(0), nxt(0),
                  pl.BlockSpec((3, tf), lambda j, i: (0, j))],
        out_specs=[pl.BlockSpec((2, tm, tf), lambda j, i: (0, i, j)),
                   pl.BlockSpec((3, tf), lambda j, i: (0, j)), pl.BlockSpec((1, tf), lambda j, i: (0, j))],
        scratch_shapes=[pltpu.VMEM((tm + 2 * HALO, tf), F32)],
        compiler_params=_params(("parallel", "arbitrary")), name=name)(
            z, gc, gc, gc, z, z, z, da, da, da, conv_w)


_LN2 = math.log(2.0)
_QSCALE = (NOPE + ROPE) ** -0.5 / _LN2


def _swap32(x):
    lane = lax.broadcasted_iota(jnp.int32, x.shape, 1)
    return jnp.where((lane % 64) < 32, pltpu.roll(x, 96, axis=1), pltpu.roll(x, 32, axis=1))


def _rms(x, g):
    r = lax.rsqrt(jnp.mean(x * x, axis=-1, keepdims=True) + EPS)
    return x * r * g


def _rms_bwd(x, g, dy):
    r = lax.rsqrt(jnp.mean(x * x, axis=-1, keepdims=True) + EPS)
    xh = x * r
    dxh = dy * g
    return r * (dxh - xh * jnp.mean(dxh * xh, axis=-1, keepdims=True)), _colsum(dy * xh)


def _mla_prep_fwd(z, gq, gkv, cos, sin, QL, KL, name):
    M, NZ = z.shape
    tm = _tile(M, (256, 128))

    def body(z_ref, gq_ref, gkv_ref, cos_ref, sin_ref, cq_ref, ckv_ref, kpe_ref):
        zv = z_ref[...]
        cq_ref[...] = _rms(zv[:, :QL], gq_ref[...]).astype(BF16)
        ckv_ref[...] = _rms(zv[:, QL:QL + KL], gkv_ref[...]).astype(BF16)
        kp = zv[:, QL + KL:]
        r = kp * cos_ref[...] + _swap32(kp) * sin_ref[...]
        lane = lax.broadcasted_iota(jnp.int32, r.shape, 1)
        kpe_ref[0] = jnp.where(lane < ROPE, r, 0.0).astype(BF16)
        kpe_ref[1] = jnp.where(lane >= ROPE, r, 0.0).astype(BF16)

    tab = pl.BlockSpec((tm, 128), lambda i: (i, 0))
    return pl.pallas_call(
        body, out_shape=[SDS((M, QL), BF16), SDS((M, KL), BF16), SDS((2, M, 128), BF16)], grid=(M // tm,),
        in_specs=[pl.BlockSpec((tm, NZ), lambda i: (i, 0)), pl.BlockSpec((1, QL), lambda i: (0, 0)),
                  pl.BlockSpec((1, KL), lambda i: (0, 0)), tab, tab],
        out_specs=[pl.BlockSpec((tm, QL), lambda i: (i, 0)), pl.BlockSpec((tm, KL), lambda i: (i, 0)),
                   pl.BlockSpec((2, tm, 128), lambda i: (0, i, 0))],
        compiler_params=_params(("parallel",)), name=name)(z, gq, gkv, cos, sin)


def _mla_prep_bwd(z, dcq, dckv, dkpe, gq, gkv, cos, sin, QL, KL, seg_t, name):
    M, NZ = z.shape
    H = dkpe.shape[0]
    tm = _row_tile(seg_t, M)
    nt = seg_t // tm

    def body(z_ref, dcq_ref, dckv_ref, dkpe_ref, gq_ref, gkv_ref, cos_ref, sin_ref, dz_ref, dgq_ref, dgkv_ref):
        i = pl.program_id(0)

        @pl.when(i == 0)
        def _():
            dgq_ref[...] = jnp.zeros_like(dgq_ref)
            dgkv_ref[...] = jnp.zeros_like(dgkv_ref)

        zv = z_ref[...]
        dyq = jnp.where(i < nt, dcq_ref[...], 0.0)
        dxq, dgq = _rms_bwd(zv[:, :QL], gq_ref[...], dyq)
        dxkv, dgkv = _rms_bwd(zv[:, QL:QL + KL], gkv_ref[...], dckv_ref[...])
        dgq_ref[...] += dgq
        dgkv_ref[...] += dgkv
        even = dkpe_ref[0]
        odd = dkpe_ref[1]
        for h in range(2, H, 2):
            even = even + dkpe_ref[h]
            odd = odd + dkpe_ref[h + 1]
        lane = lax.broadcasted_iota(jnp.int32, even.shape, 1)
        dr = jnp.where(lane < ROPE, even, odd)
        dkp = dr * cos_ref[...] - _swap32(dr) * sin_ref[...]
        dz_ref[...] = jnp.concatenate([dxq, dxkv, dkp], axis=1).astype(BF16)

    tab = pl.BlockSpec((tm, 128), lambda i: (i, 0))
    return pl.pallas_call(
        body, out_shape=[SDS((M, NZ), BF16), SDS((1, QL), F32), SDS((1, KL), F32)], grid=(M // tm,),
        in_specs=[pl.BlockSpec((tm, NZ), lambda i: (i, 0)),
                  pl.BlockSpec((tm, QL), lambda i: (jnp.minimum(i, nt - 1), 0)),
                  pl.BlockSpec((tm, KL), lambda i: (i, 0)), pl.BlockSpec((H, tm, 128), lambda i: (0, i, 0)),
                  pl.BlockSpec((1, QL), lambda i: (0, 0)), pl.BlockSpec((1, KL), lambda i: (0, 0)), tab, tab],
        out_specs=[pl.BlockSpec((tm, NZ), lambda i: (i, 0)), pl.BlockSpec((1, QL), lambda i: (0, 0)),
                   pl.BlockSpec((1, KL), lambda i: (0, 0))],
        compiler_params=_params(("arbitrary",)), name=name)(z, dcq, dckv, dkpe, gq, gkv, cos, sin)


def _qrope_fwd(q, cos, sin, HN, name):
    T, NQ = q.shape
    tm = _tile(T, (256, 128))

    def body(q_ref, cos_ref, sin_ref, o_ref):
        o_ref[:, :HN] = (q_ref[:, :HN] * _QSCALE).astype(BF16)
        for cb in range((NQ - HN) // 128):
            sl = slice(HN + cb * 128, HN + (cb + 1) * 128)
            xv = q_ref[:, sl]
            o_ref[:, sl] = ((xv * cos_ref[...] + _swap32(xv) * sin_ref[...]) * _QSCALE).astype(BF16)

    tab = pl.BlockSpec((tm, 128), lambda i: (i, 0))
    return pl.pallas_call(
        body, out_shape=SDS((T, NQ), BF16), grid=(T // tm,),
        in_specs=[pl.BlockSpec((tm, NQ), lambda i: (i, 0)), tab, tab],
        out_specs=pl.BlockSpec((tm, NQ), lambda i: (i, 0)),
        compiler_params=_params(("parallel",)), name=name)(q, cos, sin)


def _qrope_bwd(dqpe, dqa, cos, sin, HN, name):
    T, HW = dqpe.shape
    HR = HW // 2
    tm = _tile(T, (256, 128))

    def body(d_ref, dqa_hbm, cos_ref, sin_ref, o_ref):
        del dqa_hbm
        for pr in range(HR // 128):
            dr = d_ref[:, 2 * pr * 128:(2 * pr + 1) * 128] + d_ref[:, (2 * pr + 1) * 128:(2 * pr + 2) * 128]
            o_ref[:, pr * 128:(pr + 1) * 128] = (dr * cos_ref[...] - _swap32(dr) * sin_ref[...]).astype(BF16)

    tab = pl.BlockSpec((tm, 128), lambda i: (i, 0))
    return pl.pallas_call(
        body, out_shape=SDS(dqa.shape, BF16), grid=(T // tm,),
        in_specs=[pl.BlockSpec((tm, HW), lambda i: (i, 0)), pl.BlockSpec(memory_space=pl.ANY), tab, tab],
        out_specs=pl.BlockSpec((tm, HR), lambda i: (i, HN // HR)),
        input_output_aliases={1: 0}, compiler_params=_params(("parallel",)), name=name)(dqpe, dqa, cos, sin)


_ATT_SUB = 4
_ATT_SUB_B = 4
_NT = (((1,), (1,)), ((), ()))
_TN = (((0,), (0,)), ((), ()))


def _attn_fwd(qa, kv, kpe, T, H, name, carry=None):
    M = kv.shape[0]
    tq = _tile(T, (1024, 512, 256, 128))
    scale = (NOPE + ROPE) ** -0.5

    def body(qn_ref, qp_ref, kv_ref, kpe_ref, o_ref, lse_ref, kc_ref):
        @pl.when(pl.program_id(1) == 0)
        def _():
            kc_ref[:, :NOPE] = kv_ref[:, :NOPE]
            kc_ref[:, NOPE:] = kpe_ref[0]

        rs = tq // _ATT_SUB
        outs, lses = [], []
        for u in range(_ATT_SUB):
            rows = pl.ds(u * rs, rs)
            qc = jnp.concatenate([qn_ref[rows, :], qp_ref[rows, :]], axis=1)
            s = lax.dot_general(qc, kc_ref[...], _NT, preferred_element_type=F32)
            m = jnp.max(s, axis=-1, keepdims=True)
            p = jnp.exp2(s - m)
            l = jnp.sum(p, axis=-1, keepdims=True)
            o = jnp.dot(p.astype(BF16), kv_ref[:, NOPE:], preferred_element_type=F32)
            outs.append((o / l).astype(BF16))
            lses.append(jnp.broadcast_to(m + jnp.log2(l), (rs, 128)))
        o_ref[...] = jnp.concatenate(outs, axis=0)
        lse_ref[...] = jnp.concatenate(lses, axis=0)

    return _carry_call(
        body, carry, out_shape=[SDS((T, H * VDIM), BF16), SDS((T, H * 128), F32)], grid=(H, T // tq),
        in_specs=[pl.BlockSpec((tq, NOPE), lambda h, i: (i, h)), pl.BlockSpec((tq, 128), lambda h, i: (i, H + h // 2)),
                  pl.BlockSpec((M, NOPE + VDIM), lambda h, i: (0, h)), pl.BlockSpec((1, M, 128), lambda h, i: (h % 2, 0, 0))],
        out_specs=[pl.BlockSpec((tq, VDIM), lambda h, i: (i, h)), pl.BlockSpec((tq, 128), lambda h, i: (i, h))],
        scratch_shapes=[pltpu.VMEM((M, NOPE + 128), BF16)],
        sem=("parallel", "arbitrary"), name=name, ins=(qa, qa, kv, kpe))


def _attn_bwd(qa, kv, kpe, do, o, lse, T, H, name, carry=None):
    M = kv.shape[0]
    tq = _tile(T, (1024, 512, 256, 128))
    nq = T // tq
    scale = (NOPE + ROPE) ** -0.5

    def body(qn_ref, qp_ref, kv_ref, kpe_ref, do_ref, o_ref, lse_ref, dqa_ref, dqpe_ref, dkv_ref, dkpe_ref, kc_ref,
             dk_acc, dv_acc):
        i = pl.program_id(1)

        @pl.when(i == 0)
        def _():
            kc_ref[:, :NOPE] = kv_ref[:, :NOPE]
            kc_ref[:, NOPE:] = kpe_ref[0]
            dk_acc[...] = jnp.zeros_like(dk_acc)
            dv_acc[...] = jnp.zeros_like(dv_acc)

        rs = tq // _ATT_SUB_B
        p16s, ds16s = [], []
        for u in range(_ATT_SUB_B):
            rows = pl.ds(u * rs, rs)
            qc = jnp.concatenate([qn_ref[rows, :], qp_ref[rows, :]], axis=1)
            dov = do_ref[rows, :]
            s = lax.dot_general(qc, kc_ref[...], _NT, preferred_element_type=F32)
            p = jnp.exp2(s - lse_ref[rows, 0:1])
            dp = lax.dot_general(dov, kv_ref[:, NOPE:], _NT, preferred_element_type=F32)
            delta = jnp.sum(dov.astype(F32) * o_ref[rows, :].astype(F32), axis=-1, keepdims=True)
            ds16s.append((p * (dp - delta)).astype(BF16))
            p16s.append(p.astype(BF16))
        p16 = jnp.concatenate(p16s, axis=0)
        ds16 = jnp.concatenate(ds16s, axis=0)
        qc = jnp.concatenate([qn_ref[...], qp_ref[...]], axis=1)
        dq = jnp.dot(ds16, kc_ref[...], preferred_element_type=F32) * scale
        dqa_ref[...] = dq[:, :NOPE].astype(BF16)
        dqpe_ref[...] = dq[:, NOPE:]
        dv_acc[...] += lax.dot_general(p16, do_ref[...], _TN, preferred_element_type=F32)
        dk_acc[...] += lax.dot_general(ds16, qc, _TN, preferred_element_type=F32)

        @pl.when(i == nq - 1)
        def _():
            dkv_ref[:, :NOPE] = (dk_acc[:, :NOPE] * _LN2).astype(BF16)
            dkv_ref[:, NOPE:] = dv_acc[...].astype(BF16)
            dkpe_ref[0] = dk_acc[:, NOPE:] * _LN2

    return _carry_call(
        body, carry,
        out_shape=[SDS((T, H * (NOPE + ROPE)), BF16), SDS((T, H * 128), F32), SDS((M, H * (NOPE + VDIM)), BF16),
                   SDS((H, M, 128), F32)],
        grid=(H, nq),
        in_specs=[pl.BlockSpec((tq, NOPE), lambda h, i: (i, h)), pl.BlockSpec((tq, 128), lambda h, i: (i, H + h // 2)),
                  pl.BlockSpec((M, NOPE + VDIM), lambda h, i: (0, h)), pl.BlockSpec((1, M, 128), lambda h, i: (h % 2, 0, 0)),
                  pl.BlockSpec((tq, VDIM), lambda h, i: (i, h)), pl.BlockSpec((tq, VDIM), lambda h, i: (i, h)),
                  pl.BlockSpec((tq, 128), lambda h, i: (i, h))],
        out_specs=[pl.BlockSpec((tq, NOPE), lambda h, i: (i, h)), pl.BlockSpec((tq, 128), lambda h, i: (i, h)),
                   pl.BlockSpec((M, NOPE + VDIM), lambda h, i: (0, h)), pl.BlockSpec((1, M, 128), lambda h, i: (h, 0, 0))],
        scratch_shapes=[pltpu.VMEM((M, NOPE + 128), BF16), pltpu.VMEM((M, NOPE + 128), F32), pltpu.VMEM((M, VDIM), F32)],
        sem=("parallel", "arbitrary"), name=name, ins=(qa, qa, kv, kpe, do, o, lse))


def _adamw(w, m, v, name, g=None, recv=None, carry=None):
    R, C = w.shape
    summed = recv is not None
    n_recv = len(recv) if summed else 1
    runs = [r.shape[1] for r in recv] if summed else [R]
    tr = math.gcd(*runs)
    for cand in (1024, 512, 256, 128, 64, 32, 16, 8):
        if tr % cand == 0 and cand * C <= 131072:
            tr = cand
            break
    first = [sum(runs[:r]) // tr for r in range(n_recv + 1)]
    c1 = 1.0 - ADAM_B1 ** ADAM_STEP
    c2 = 1.0 - ADAM_B2 ** ADAM_STEP

    def update(gv, w_ref, m_ref, v_ref, d_ref, nm_ref, nv_ref):
        mn = ADAM_B1 * m_ref[...] + (1.0 - ADAM_B1) * gv
        vn = ADAM_B2 * v_ref[...] + (1.0 - ADAM_B2) * (gv * gv)
        nm_ref[...] = mn
        nv_ref[...] = vn
        d_ref[...] = -ADAM_LR * ((mn / c1) / (jnp.sqrt(vn / c2) + ADAM_EPS) + ADAM_WD * w_ref[...])

    def body(*refs):
        w_ref, m_ref, v_ref = refs[:3]
        g_refs = refs[3:3 + n_recv]
        outs = refs[3 + n_recv:]
        if not summed:
            update(g_refs[0][...], w_ref, m_ref, v_ref, *outs)
            return
        i = pl.program_id(0)
        for r in range(n_recv):
            @pl.when((i >= first[r]) & (i < first[r + 1]))
            def _():
                gv = g_refs[r][0].astype(F32)
                for d in range(1, N_DEV):
                    gv = gv + g_refs[r][d].astype(F32)
                outs[0][...] = gv
                update(gv, w_ref, m_ref, v_ref, *outs[1:])

    blk = pl.BlockSpec((tr, C), lambda i: (i, 0))
    if summed:
        g_specs = [pl.BlockSpec((N_DEV, tr, C), functools.partial(
            lambda i, lo, n: (0, jnp.clip(i - lo, 0, n - 1), 0), lo=first[r], n=first[r + 1] - first[r]))
                   for r in range(n_recv)]
    else:
        g_specs = [blk]
    n_out = 4 if summed else 3
    out, carried = _carry_call(
        body, carry, out_shape=[SDS((R, C), F32)] * n_out, grid=(R // tr,), in_specs=[blk, blk, blk] + g_specs,
        out_specs=[blk] * n_out, scratch_shapes=[], sem=("parallel",), name=name,
        ins=(w, m, v, *(recv if summed else [g])))
    return (out, carried) if carry else out


WEIGHTS = ['c_ctx', 'norm1_g', 'norm2_g', 'w_ada', 'b_ada', 'ab_w_in', 'ab_b_in', 'a_ln_g', 'a_ln_b', 'a_w_s', 'a_b_s',
           'b_conv_w', 'b_conv_b', 'b_ln_g', 'b_ln_b', 'ab_w_out', 'mla_w_in', 'mla_q_norm_g', 'mla_w_uq',
           'mla_kv_norm_g', 'mla_w_ukv', 'mla_w_o', 'ffn_w_up', 'ffn_conv_w', 'ffn_conv_b', 'ffn_w_down', 'final_norm_g']


def _pack(parts):
    flat = jnp.concatenate([p.reshape(-1).astype(F32) for p in parts])
    n = flat.shape[0]
    unit = 65536 if n > 65536 else 1024
    n_pad = -(-n // unit) * unit
    return jnp.pad(flat, (0, n_pad - n)).reshape(n_pad // 128, 128)


def _unpack(flat, like):
    out, off = [], 0
    for shp in like:
        n = math.prod(shp)
        out.append(flat[..., off:off + n].reshape(flat.shape[:-1] + tuple(shp)))
        off += n
    return out


def _rope_tables(T, Tc):
    rows = T // GRID_W
    row = jnp.repeat(jnp.arange(rows, dtype=F32), GRID_W)
    col = jnp.tile(jnp.arange(GRID_W, dtype=F32), rows)
    n_freq = ROPE // 4
    inv = ROPE_THETA ** (-jnp.arange(n_freq, dtype=F32) / n_freq)
    ang = jnp.concatenate([row[:, None] * inv, col[:, None] * inv], axis=-1)
    cos, sin = jnp.cos(ang), jnp.sin(ang)
    cos = jnp.tile(cos, (1, 128 // (ROPE // 2)))
    sin = jnp.tile(jnp.concatenate([-sin, sin], axis=1), (1, 128 // ROPE))
    return (jnp.concatenate([cos, jnp.ones((Tc, 128), F32)], axis=0),
            jnp.concatenate([sin, jnp.zeros((Tc, 128), F32)], axis=0))


def _step(a):
    ax, ay, ac = lax.axis_index("x"), lax.axis_index("y"), lax.axis_index("c")
    me = 4 * ax + 2 * ay + ac
    T, D = a['x'].shape[1:]
    Tc = a['ctx'].shape[1]
    M = T + Tc
    W, Wb = a['a_ln_g'].shape[1], a['b_ln_g'].shape[1]
    assert W == Wb and T % Tc == 0
    Fd = a['ffn_conv_b'].shape[1]
    QL, KL = a['mla_q_norm_g'].shape[1] * N_DEV, a['mla_kv_norm_g'].shape[1] * N_DEV
    H = a['mla_w_ukv'].shape[2] * N_DEV // (NOPE + VDIM)
    HN, HR = H * NOPE, H * ROPE
    kw = a['b_conv_w'].shape[1]
    NA = a['w_ada'].shape[2]
    bf = lambda t: t.astype(BF16)

    small_shapes = [(D,), (kw, Wb // N_DEV), (2, 3, Fd // N_DEV), (QL // N_DEV,), (KL // N_DEV,)]
    g_small = _all_gather(_pack([a['c'][0], a['b_conv_w'][0], a['ffn_conv_w'], a['mla_q_norm_g'][0], a['mla_kv_norm_g'][0]]),
                          "ag_small")
    c_all, bcw, fcw, gq, gkv = _unpack(g_small.reshape(N_DEV, -1), small_shapes)
    bcw = jnp.transpose(bcw, (1, 0, 2)).reshape(kw, Wb)
    fcw = jnp.transpose(fcw, (1, 2, 0, 3)).reshape(2, 3, Fd)
    gq, gkv = gq.reshape(1, QL), gkv.reshape(1, KL)

    a16 = jnp.concatenate([c_all, a['c_ctx'][None], jnp.zeros((N_DEV - 1, D), F32)], axis=0)
    b_loc = lax.dynamic_slice(a['b_ada'], (0, me * NA), (2, NA))
    mods = [_mm(a16, a['w_ada'][l], mode="nn", out_dtype=F32, name=f"ada_fwd{l}", bias=b_loc[l:l + 1], a_silu=True)
            for l in range(2)]
    gm = _all_gather(jnp.concatenate(mods, axis=0), "ag_mod").reshape(N_DEV, 2, 2 * N_DEV, NA)
    gm = jnp.transpose(gm, (1, 2, 0, 3)).reshape(2, 2 * N_DEV, 6 * D)
    mod_lat = [lax.dynamic_slice(gm[l], (me, 0), (1, 6 * D)).reshape(6, 1, 1, D) for l in range(2)]
    mod_ctx = [gm[l][N_DEV].reshape(6, 1, 1, D) for l in range(2)]

    def mod(l, k, both):
        return jnp.concatenate([mod_lat[l][k], mod_ctx[l][k]], axis=0) if both else mod_lat[l][k]

    def from_cols(g):
        return jnp.transpose(g, (1, 0, 2)).reshape(g.shape[1], -1)

    def from_rows(g):
        return g.reshape(-1, g.shape[2])

    def ag(x):
        return (x, False)

    def a2a(x):
        return (x, True)

    cos, sin = _rope_tables(T, Tc)
    n1g, n2g = a['norm1_g'], a['norm2_g']
    a_bst = a['a_b_s'][0].T
    mm = functools.partial(_mm)
    up_sh, dn_sh = bf(a['ffn_w_up']), bf(a['ffn_w_down'])

    x0 = jnp.concatenate([a['x'][0], a['ctx'][0]], axis=0)
    h1, (w_abin,) = _normmod_fwd(x0, n1g[0:1], mod(0, 0, True), mod(0, 1, True), T, "l0_norm1",
                                 carry=[ag(bf(a['ab_w_in'][0]))])
    s1, s2 = D // 4, 5 * D // 8
    z, (g_about, g_up0a) = mm(h1, w_abin, mode="nn", out_dtype=F32, name="l0_ab_in", bias=a['ab_b_in'], b_dev=True,
                              carry=[ag(bf(a['ab_w_out'][0])), ag(up_sh[0][:s1])])
    w_about = from_rows(g_about)
    y = _gmlp_fwd(z, a['a_ln_g'], a['a_ln_b'], a['a_w_s'][0], a_bst, W, "l0_gmlp")
    (y, hc_b), (g_up0b,) = _conf_fwd(z, y, bcw, a['b_conv_b'], a['b_ln_g'], a['b_ln_b'], W, Wb, T, "l0_conf",
                                     carry=[ag(up_sh[0][s1:s2])])
    (x1, o1), (g_up0c,) = mm(y, w_about, mode="nn", out_dtype=F32, name="l0_ab_out", res=x0, gate=mod(0, 2, True),
                             seg_t=T, carry=[ag(up_sh[0][s2:])])
    w_up = [jnp.concatenate([g_up0a, g_up0b, g_up0c], axis=1), None]
    h2 = _normmod_fwd(x1, n2g[0:1], mod(0, 3, True), mod(0, 4, True), T, "l0_norm2")
    z2, (g_dn0,) = mm(h2, w_up[0], mode="nn", out_dtype=BF16, name="l0_up", b_dev=True, carry=[ag(dn_sh[0])])
    w_dn = [from_rows(g_dn0), None]
    a2, gc2 = _ffn_act_fwd(z2, fcw[0], a['ffn_conv_b'][0:1], T, "l0_act")
    (x2, o2), (g_in, g_uq) = mm(a2, w_dn[0], mode="nn", out_dtype=F32, name="l0_down", res=x1, gate=mod(0, 5, True),
                                seg_t=T, tiles=(M // 8, None, Fd // 2),
                                carry=[ag(bf(a['mla_w_in'][0])), ag(bf(a['mla_w_uq'][0]))])
    w_in = from_rows(g_in)
    w_in = jnp.concatenate([w_in, w_in[:, QL + KL:]], axis=1)
    w_uq = from_cols(g_uq).reshape(QL, H, NOPE + ROPE)
    w_uq = jnp.concatenate([w_uq[:, :, :NOPE].reshape(QL, HN), w_uq[:, :, NOPE:].reshape(QL, HR)], axis=1)

    h3 = _normmod_fwd(x2, n1g[1:2], mod(1, 0, True), mod(1, 1, True), T, "l1_norm1")
    z3, (g_ukv,) = mm(h3, w_in, mode="nn", out_dtype=F32, name="l1_mla_in", carry=[ag(bf(a['mla_w_ukv'][0]))])
    w_ukv = g_ukv
    cqn, ckvn, kpe = _mla_prep_fwd(z3, gq, gkv, cos, sin, QL, KL, "l1_prep")
    q, (g_wo,) = mm(cqn, w_uq, mode="nn", out_dtype=F32, name="l1_uq", rows=T, carry=[ag(bf(a['mla_w_o'][0]))])
    w_o = from_rows(g_wo)
    kv = mm(ckvn, w_ukv, mode="nn", out_dtype=BF16, name="l1_ukv", b_dev=True)
    qa = _qrope_fwd(q, cos, sin, HN, "l1_qrope")
    (o_att, lse), (g_up1,) = _attn_fwd(qa, kv, kpe, T, H, "l1_attn", carry=[ag(up_sh[1])])
    w_up[1] = g_up1
    x3, o3 = mm(o_att, w_o, mode="nn", out_dtype=F32, name="l1_wo", res=x2, gate=mod(1, 2, False), seg_t=T)
    h4 = _normmod_fwd(x3, n2g[1:2], mod(1, 3, False), mod(1, 4, False), T, "l1_norm2")
    z4, (g_dn1,) = mm(h4, w_up[1], mode="nn", out_dtype=BF16, name="l1_up", b_dev=True, carry=[ag(dn_sh[1])])
    w_dn[1] = from_rows(g_dn1)
    a4, gc4 = _ffn_act_fwd(z4, fcw[1], a['ffn_conv_b'][1:2], T, "l1_act")
    x4, o4 = mm(a4, w_dn[1], mode="nn", out_dtype=F32, name="l1_down", res=x3, gate=mod(1, 5, False), seg_t=T,
                tiles=(T // 8, None, Fd // 2))

    dx4, loss_cols, d_fng, do4, dg2_1 = _final(x4, a['final_norm_g'][None], a['loss_target'][0], o4, mod(1, 5, False),
                                               "final")
    loss = lax.psum(jnp.sum(loss_cols), ("x", "y", "c"))

    def cols(dw):
        k, n = dw.shape
        return jnp.transpose(dw.reshape(k, N_DEV, n // N_DEV), (1, 0, 2))

    def rows(dw):
        return dw.reshape(N_DEV, dw.shape[0] // N_DEV, dw.shape[1])

    da4 = mm(do4, w_dn[1], mode="nt", out_dtype=BF16, name="l1_down_dx")
    dw_dn1 = mm(a4, do4, mode="tn", out_dtype=BF16, name="l1_down_dw")
    dz4, dfcw1, dfcb1 = _ffn_act_bwd(z4, gc4, da4, fcw[1], T, "l1_act_bwd")
    dw_up1, (r_dn1,) = mm(h4, dz4, mode="tn", out_dtype=BF16, name="l1_up_dw", out_dev=True, halves=True,
                          carry=[a2a(rows(dw_dn1))])
    dh4 = mm(dz4, w_up[1], mode="nt", out_dtype=F32, name="l1_up_dx", b_dev=True, halves=True, pair=True)
    dx3, dn2g1, dsh2_1, dsc2_1, do3, dg1_1 = _normmod_bwd(x3, n2g[1:2], mod(1, 4, False), dh4, dx4, T, "l1_norm2_bwd",
                                                         o_prev=o3, gate_prev=mod(1, 2, False))
    d_oatt = mm(do3, w_o, mode="nt", out_dtype=BF16, name="l1_wo_dx")
    dw_o = mm(o_att, do3, mode="tn", out_dtype=BF16, name="l1_wo_dw")
    (dqa, dqpe, dkv, dkpe), (r_up1, r_wo) = _attn_bwd(qa, kv, kpe, d_oatt, o_att, lse, T, H, "l1_attn_bwd",
                                                      carry=[a2a(dw_up1), a2a(rows(dw_o))])
    dqa = _qrope_bwd(dqpe, dqa, cos, sin, HN, "l1_qrope_bwd")
    dcq = mm(dqa, w_uq, mode="nt", out_dtype=F32, name="l1_uq_dx")
    dw_uq = mm(cqn, dqa, mode="tn", out_dtype=BF16, name="l1_uq_dw", rows=T)
    dw_uq = jnp.concatenate([dw_uq[:, :HN].reshape(QL, H, NOPE), dw_uq[:, HN:].reshape(QL, H, ROPE)], axis=2)
    dw_uq = dw_uq.reshape(QL, H * (NOPE + ROPE))
    dckv = mm(dkv, w_ukv, mode="nt", out_dtype=F32, name="l1_ukv_dx", b_dev=True)
    dw_ukv = mm(ckvn, dkv, mode="tn", out_dtype=BF16, name="l1_ukv_dw", out_dev=True)
    dz3, dgq, dgkv = _mla_prep_bwd(z3, dcq, dckv, dkpe, gq, gkv, cos, sin, QL, KL, T, "l1_prep_bwd")
    dh3 = mm(dz3, w_in, mode="nt", out_dtype=F32, name="l1_mla_in_dx")
    dw_in = mm(h3, dz3, mode="tn", out_dtype=BF16, name="l1_mla_in_dw").astype(F32)
    dw_in = jnp.concatenate([dw_in[:, :QL + KL], dw_in[:, QL + KL:QL + KL + ROPE] + dw_in[:, QL + KL + ROPE:QL + KL + 2 * ROPE]],
                            axis=1).astype(BF16)
    dx2, dn1g1, dsh1_1, dsc1_1, do2, dg2_0 = _normmod_bwd(x2, n1g[1:2], mod(1, 1, True), dh3, dx3, T, "l1_norm1_bwd",
                                                         o_prev=o2, gate_prev=mod(0, 5, True))
    da2, (r_uq, r_ukv) = mm(do2, w_dn[0], mode="nt", out_dtype=BF16, name="l0_down_dx",
                            carry=[a2a(cols(dw_uq)), a2a(dw_ukv)])
    dw_dn0, (r_in,) = mm(a2, do2, mode="tn", out_dtype=BF16, name="l0_down_dw", carry=[a2a(rows(dw_in))])
    dz2, dfcw0, dfcb0 = _ffn_act_bwd(z2, gc2, da2, fcw[0], T, "l0_act_bwd")
    dw_up0, (r_dn0,) = mm(h2, dz2, mode="tn", out_dtype=BF16, name="l0_up_dw", out_dev=True, halves=True,
                          carry=[a2a(rows(dw_dn0))])
    dh2, (r_up0a,) = mm(dz2, w_up[0], mode="nt", out_dtype=F32, name="l0_up_dx", b_dev=True, halves=True, pair=True,
                        carry=[a2a(dw_up0[:, :D // 2])])
    dx1, dn2g0, dsh2_0, dsc2_0, do1, dg1_0 = _normmod_bwd(x1, n2g[0:1], mod(0, 4, True), dh2, dx2, T, "l0_norm2_bwd",
                                                         o_prev=o1, gate_prev=mod(0, 2, True))
    dy = mm(do1, w_about, mode="nt", out_dtype=F32, name="l0_ab_out_dx")
    dw_about = mm(y, do1, mode="tn", out_dtype=BF16, name="l0_ab_out_dw")
    dz, dlag, dlab, dws, dbs, dbin_a = _gmlp_bwd(z, dy, a['a_ln_g'], a['a_ln_b'], a['a_w_s'][0], a_bst, W, "l0_gmlp_bwd")
    dhc, dlbg, dlbb, dbcb = _conf_bwd1(hc_b, dy, a['b_ln_g'], a['b_ln_b'], W, Wb, T, "l0_conf_bwd1")
    (dz, dbcw, dbin_b), (r_up0b,) = _conf_bwd2(z, dhc, dz, bcw, W, Wb, T, "l0_conf_bwd2",
                                               carry=[a2a(dw_up0[:, D // 2:])])
    dh1, (r_about,) = mm(dz, w_abin, mode="nt", out_dtype=F32, name="l0_ab_in_dx", b_dev=True,
                         carry=[a2a(rows(dw_about))])
    dw_abin_a = mm(h1, dz, mode="tn", out_dtype=BF16, name="l0_ab_in_dw_a", out_dev=True, p_range=(0, D // 2))
    dw_abin_b, (r_abin_a,) = mm(h1, dz, mode="tn", out_dtype=BF16, name="l0_ab_in_dw_b", out_dev=True,
                                p_range=(D // 2, D // 2), carry=[a2a(dw_abin_a)])
    dx0, dn1g0, dsh1_0, dsc1_0 = _normmod_bwd(x0, n1g[0:1], mod(0, 1, True), dh1, dx1, T, "l0_norm1_bwd")

    zero = jnp.zeros((D,), F32)
    dmod = jnp.stack([
        jnp.stack([jnp.stack([dsh1_0[0, 0], dsc1_0[0, 0], dg1_0[0, 0], dsh2_0[0, 0], dsc2_0[0, 0], dg2_0[0, 0]]),
                   jnp.stack([dsh1_0[1, 0], dsc1_0[1, 0], dg1_0[1, 0], dsh2_0[1, 0], dsc2_0[1, 0], dg2_0[1, 0]])]),
        jnp.stack([jnp.stack([dsh1_1[0, 0], dsc1_1[0, 0], dg1_1[0, 0], dsh2_1[0, 0], dsc2_1[0, 0], dg2_1[0, 0]]),
                   jnp.stack([dsh1_1[1, 0], dsc1_1[1, 0], zero, zero, zero, zero])])])
    small = {
        'norm1_g': jnp.concatenate([dn1g0, dn1g1], axis=0), 'norm2_g': jnp.concatenate([dn2g0, dn2g1], axis=0),
        'ab_b_in': jnp.concatenate([dbin_a, dbin_b], axis=1), 'a_ln_g': dlag, 'a_ln_b': dlab, 'a_w_s': dws[None],
        'a_b_s': jnp.sum(dbs, axis=-1)[None], 'b_conv_w': dbcw, 'b_conv_b': dbcb, 'b_ln_g': dlbg, 'b_ln_b': dlbb,
        'mla_q_norm_g': dgq, 'mla_kv_norm_g': dgkv, 'ffn_conv_w': jnp.stack([dfcw0, dfcw1]),
        'ffn_conv_b': jnp.concatenate([dfcb0, dfcb1], axis=0), 'final_norm_g': d_fng[0],
    }
    names = list(small)
    up2d = (2 * D, a['ffn_w_up'].shape[2])
    res_up, (g2,) = _adamw(a['ffn_w_up'].reshape(up2d), a['m_ffn_w_up'].reshape(up2d), a['v_ffn_w_up'].reshape(up2d),
                           "adamw_ffn_w_up", recv=[r_up0a, r_up0b, r_up1],
                           carry=[ag(_pack([dmod] + [small[n] for n in names]))])
    red = _sum_lead(g2, "sum_small_grads").reshape(-1)
    red = dict(zip(names, _unpack(red, [(2, 2, 6, D)] + [small[n].shape for n in names])[1:]))
    dmod_all = g2.reshape(N_DEV, -1)[:, :2 * 2 * 6 * D].reshape(N_DEV, 2, 2, 6 * D)

    a16g = jnp.concatenate([c_all, jnp.tile(a['c_ctx'][None], (N_DEV, 1))], axis=0)
    dm_loc = lax.dynamic_slice(dmod_all, (0, 0, 0, me * NA), (N_DEV, 2, 2, NA))
    g_wada, cpart = [], []
    for l in range(2):
        dm16 = jnp.concatenate([dm_loc[:, l, 0], dm_loc[:, l, 1]], axis=0)
        g_wada.append(mm(a16g, dm16, mode="tn", out_dtype=F32, name=f"ada_dw{l}", a_silu=True))
        cpart.append(mm(dm_loc[:, l, 1], a['w_ada'][l], mode="nt", out_dtype=F32, name=f"ada_dc{l}"))
    g_bada = _sum_lead(jnp.transpose(dmod_all, (0, 2, 1, 3)).reshape(2 * N_DEV, 2 * 6 * D // 128, 128), "sum_b_ada")
    g_cc = _all_gather(jnp.concatenate(cpart, axis=0), "ag_c_ctx")
    g_cc = _sum_lead(g_cc.reshape(2 * N_DEV * N_DEV, D // 128, 128), "sum_c_ctx").reshape(D)
    grads = {
        'c_ctx': g_cc * _dsilu(a['c_ctx']), 'w_ada': jnp.stack(g_wada), 'b_ada': g_bada.reshape(2, 6 * D),
        'b_conv_w': lax.dynamic_slice(red['b_conv_w'], (0, me * (Wb // N_DEV)), (kw, Wb // N_DEV))[None],
        'ffn_conv_w': lax.dynamic_slice(red['ffn_conv_w'], (0, 0, me * (Fd // N_DEV)), (2, 3, Fd // N_DEV)),
        'mla_q_norm_g': lax.dynamic_slice(red['mla_q_norm_g'], (0, me * (QL // N_DEV)), (1, QL // N_DEV)),
        'mla_kv_norm_g': lax.dynamic_slice(red['mla_kv_norm_g'], (0, me * (KL // N_DEV)), (1, KL // N_DEV)),
    }
    for n in names:
        if n not in grads:
            grads[n] = red[n].reshape(a[n].shape)

    recvs = {'ab_w_out': [r_about], 'mla_w_in': [r_in], 'mla_w_uq': [r_uq], 'mla_w_ukv': [r_ukv],
             'mla_w_o': [r_wo], 'ffn_w_down': [r_dn0, r_dn1]}
    out = {}
    for n in WEIGHTS:
        shp = a[n].shape
        w2 = a[n].reshape(-1, shp[-1])
        m2, v2 = a['m_' + n].reshape(w2.shape), a['v_' + n].reshape(w2.shape)
        if n == 'ffn_w_up':
            res = res_up
        elif n in recvs:
            res = _adamw(w2, m2, v2, "adamw_" + n, recv=recvs[n])
        elif n == 'w_ada':
            g2d = grads[n].reshape(w2.shape)
            res, (r_abin_b,) = _adamw(w2, m2, v2, "adamw_" + n, g=g2d, carry=[a2a(dw_abin_b)])
            res = (g2d,) + tuple(res)
            recvs['ab_w_in'] = [r_abin_a, r_abin_b]
        else:
            g2d = grads[n].reshape(w2.shape)
            res = (g2d,) + tuple(_adamw(w2, m2, v2, "adamw_" + n, g=g2d))
        out[n] = [r.reshape(shp) for r in res]
    return (loss, dx0[:T][None], *[out[n][0] for n in WEIGHTS], *[out[n][1] for n in WEIGHTS],
            *[out[n][2] for n in WEIGHTS], *[out[n][3] for n in WEIGHTS])


def kernel(x, c, ctx, c_ctx, norm1_g, norm2_g, w_ada, b_ada, ab_w_in, ab_b_in, a_ln_g, a_ln_b, a_w_s, a_b_s, b_conv_w, b_conv_b, b_ln_g, b_ln_b, ab_w_out, mla_w_in, mla_q_norm_g, mla_w_uq, mla_kv_norm_g, mla_w_ukv, mla_w_o, ffn_w_up, ffn_conv_w, ffn_conv_b, ffn_w_down, final_norm_g, loss_target, m_c_ctx, m_norm1_g, m_norm2_g, m_w_ada, m_b_ada, m_ab_w_in, m_ab_b_in, m_a_ln_g, m_a_ln_b, m_a_w_s, m_a_b_s, m_b_conv_w, m_b_conv_b, m_b_ln_g, m_b_ln_b, m_ab_w_out, m_mla_w_in, m_mla_q_norm_g, m_mla_w_uq, m_mla_kv_norm_g, m_mla_w_ukv, m_mla_w_o, m_ffn_w_up, m_ffn_conv_w, m_ffn_conv_b, m_ffn_w_down, m_final_norm_g, v_c_ctx, v_norm1_g, v_norm2_g, v_w_ada, v_b_ada, v_ab_w_in, v_ab_b_in, v_a_ln_g, v_a_ln_b, v_a_w_s, v_a_b_s, v_b_conv_w, v_b_conv_b, v_b_ln_g, v_b_ln_b, v_ab_w_out, v_mla_w_in, v_mla_q_norm_g, v_mla_w_uq, v_mla_kv_norm_g, v_mla_w_ukv, v_mla_w_o, v_ffn_w_up, v_ffn_conv_w, v_ffn_conv_b, v_ffn_w_down, v_final_norm_g):
    return _step(dict(locals()))
```

```python
import functools
import math

import jax
import jax.numpy as jnp
from jax import lax
from jax.experimental import pallas as pl
from jax.experimental.pallas import tpu as pltpu

F32 = jnp.float32
BF16 = jnp.bfloat16
SDS = jax.ShapeDtypeStruct

N_DEV = 8
EPS = 1e-6
CHUNK = 128
NOPE = 128
ROPE = 64
VDIM = 128
GRID_W = 64
ROPE_THETA = 10000.0
HALO = 16
ADAM_LR, ADAM_B1, ADAM_B2, ADAM_EPS, ADAM_WD, ADAM_STEP = 0.001, 0.9, 0.999, 1e-08, 0.01, 10
VMEM_LIMIT = 56 * 1024 * 1024


def _tile(n, prefs):
    for p in prefs:
        if n % p == 0:
            return p
    return n


def _params(sem, vmem=VMEM_LIMIT):
    return pltpu.CompilerParams(dimension_semantics=sem, vmem_limit_bytes=vmem)


def _sigmoid(x):
    return 0.5 * jnp.tanh(0.5 * x) + 0.5


def _silu(x):
    return x * _sigmoid(x)


def _dsilu(x):
    s = _sigmoid(x)
    return s * (1.0 + x * (1.0 - s))


_GELU_C = math.sqrt(2.0 / math.pi)


def _gelu(x):
    return 0.5 * x * (1.0 + jnp.tanh(_GELU_C * (x + 0.044715 * x * x * x)))


def _dgelu(x):
    t = jnp.tanh(_GELU_C * (x + 0.044715 * x * x * x))
    return 0.5 * (1.0 + t) + 0.5 * x * (1.0 - t * t) * _GELU_C * (1.0 + 3.0 * 0.044715 * x * x)


def _colsum(v):
    return jnp.sum(v, axis=0, keepdims=True)


_SIBLING = 1
_CHIPS = (2, 4, 6)


def _xchg(x_ref, o_ref, send_sems, recv_sems, local_sem, scatter):
    ax, ay, ac = lax.axis_index("x"), lax.axis_index("y"), lax.axis_index("c")
    me = 4 * ax + 2 * ay + ac

    def dev(k):
        return ax ^ (k >> 2), ay ^ ((k >> 1) & 1), ac ^ (k & 1)

    def idx(k):
        px, py, pc = dev(k)
        return 4 * px + 2 * py + pc

    def copy(k, src, dst, to):
        return pltpu.make_async_remote_copy(src_ref=src, dst_ref=dst, send_sem=send_sems.at[k - 1],
                                            recv_sem=recv_sems.at[k - 1], device_id=dev(to),
                                            device_id_type=pl.DeviceIdType.MESH)

    def own():
        return pltpu.make_async_copy(x_ref.at[me] if scatter else x_ref, o_ref.at[me], local_sem)

    def sends():
        if scatter:
            return [copy(k, x_ref.at[idx(k)], o_ref.at[me], k) for k in range(1, N_DEV)]
        return [copy(k, x_ref, o_ref.at[me], k) for k in (_SIBLING,) + _CHIPS]

    def forwards():
        return [] if scatter else [copy(j + 1, o_ref.at[idx(j)], o_ref.at[idx(j)], _SIBLING) for j in _CHIPS]

    def arrival(k):
        return copy(k, o_ref.at[idx(k)], o_ref.at[idx(k)], k)

    return own, sends, forwards, arrival


def _xchg_start(*refs, scatter):
    own, sends, _, _ = _xchg(*refs, scatter)
    own().start()
    for cp in sends():
        cp.start()


def _xchg_forward(*refs, scatter):
    _, _, forwards, arrival = _xchg(*refs, scatter)
    if not scatter:
        for j, fw in zip(_CHIPS, forwards()):
            arrival(j).wait_recv()
            fw.start()


def _xchg_finish(*refs, scatter):
    own, sends, forwards, arrival = _xchg(*refs, scatter)
    for k in range(1, N_DEV):
        if scatter or k not in _CHIPS:
            arrival(k).wait_recv()
    for cp in sends() + forwards():
        cp.wait_send()
    own().wait()


_XCHG_SEMS = [pltpu.SemaphoreType.DMA((N_DEV - 1,)), pltpu.SemaphoreType.DMA((N_DEV - 1,)), pltpu.SemaphoreType.DMA]


def _xchg_shape(x, scatter):
    return SDS((N_DEV,) + tuple(x.shape[1:] if scatter else x.shape), x.dtype)


def _exchange(x, *, scatter, name):
    def body(*refs):
        _xchg_start(*refs, scatter=scatter)
        _xchg_forward(*refs, scatter=scatter)
        _xchg_finish(*refs, scatter=scatter)

    return pl.pallas_call(
        body, out_shape=_xchg_shape(x, scatter),
        in_specs=[pl.BlockSpec(memory_space=pl.ANY)], out_specs=pl.BlockSpec(memory_space=pl.ANY),
        scratch_shapes=list(_XCHG_SEMS), name=name)(x)


def _carried(body, carry, n_in, n_out, n_scratch, grid):
    nc = len(carry)
    total = math.prod(grid)
    mid = (3 * total) // 4

    def wrapped(*refs):
        ins, cin = refs[:n_in], refs[n_in:n_in + nc]
        o0 = n_in + nc
        outs, cout = refs[o0:o0 + n_out], refs[o0 + n_out:o0 + n_out + nc]
        scr = refs[o0 + n_out + nc:]
        sems = scr[n_scratch:]
        step = pl.program_id(0)
        for ax in range(1, len(grid)):
            step = step * grid[ax] + pl.program_id(ax)

        def each(fn):
            for c in range(nc):
                fn(cin[c], cout[c], *sems[3 * c:3 * c + 3], scatter=carry[c][1])

        @pl.when(step == 0)
        def _():
            each(_xchg_start)

        body(*ins, *outs, *scr[:n_scratch])

        if mid < total - 1:
            @pl.when(step == mid)
            def _():
                each(_xchg_forward)

        @pl.when(step == total - 1)
        def _():
            if mid >= total - 1:
                each(_xchg_forward)
            each(_xchg_finish)

    return wrapped


def _carry_call(body, carry, *, grid, out_shape, in_specs, out_specs, scratch_shapes, sem, name, ins, aliases=None):
    carry = carry or []
    nc = len(carry)
    if nc:
        body = _carried(body, carry, len(in_specs), len(out_shape), len(scratch_shapes), grid)
        anyspec = pl.BlockSpec(memory_space=pl.ANY)
        in_specs = list(in_specs) + [anyspec] * nc
        out_specs = list(out_specs) + [anyspec] * nc
        out_shape = list(out_shape) + [_xchg_shape(x, sc) for x, sc in carry]
        scratch_shapes = list(scratch_shapes) + list(_XCHG_SEMS) * nc
        ins = list(ins) + [x for x, _ in carry]
        sem = ("arbitrary",) * len(grid)
    out = pl.pallas_call(body, out_shape=out_shape, grid=grid, in_specs=in_specs, out_specs=out_specs,
                         scratch_shapes=scratch_shapes, compiler_params=_params(sem), name=name,
                         input_output_aliases=aliases or {})(*ins)
    n_main = len(out) - nc
    return list(out[:n_main]), list(out[n_main:])


def _all_gather(x, name):
    return _exchange(x, scatter=False, name=name)


def _all_to_all(x, name):
    return _exchange(x, scatter=True, name=name)


def _sum_lead(x, name):
    n, R, C = x.shape
    tr = _tile(R, (512, 256, 128, 64, 32, 16, 8))

    def body(x_ref, o_ref):
        acc = x_ref[0]
        for d in range(1, n):
            acc = acc + x_ref[d]
        o_ref[...] = acc

    return pl.pallas_call(
        body, out_shape=SDS((R, C), F32), grid=(R // tr,),
        in_specs=[pl.BlockSpec((n, tr, C), lambda i: (0, i, 0))], out_specs=pl.BlockSpec((tr, C), lambda i: (i, 0)),
        compiler_params=_params(("parallel",)), name=name)(x)


_TP = (1408, 1088, 1024, 768, 512, 256, 128)
_TQ = (1408, 1024, 768, 512, 256, 128)
_TR = (2048, 1408, 1024, 768, 512, 256, 128)
_TR_TN = (2176, 2048, 1088, 1024, 512, 256, 128)


def _mm(a, b, *, mode, out_dtype, name, rows=None, bias=None, res=None, gate=None, seg_t=None, a_silu=False, carry=None,
        b_dev=False, out_dev=False, p_range=None, halves=False, tiles=None, pair=False):
    if mode == "nn":
        P, R, Q = rows or a.shape[0], a.shape[1], (b.shape[0] * b.shape[2] if b_dev else b.shape[1])
    elif mode == "nt":
        P, R, Q = rows or a.shape[-2], (2 * a.shape[2] if halves else a.shape[1]), (b.shape[1] if b_dev else b.shape[0])
    else:
        R, P, Q = rows or a.shape[0], a.shape[1], (2 * b.shape[2] if halves else b.shape[1])
    p0 = 0
    if p_range is not None:
        p0, P = p_range
    tp = _tile(P, _TP)
    tq = _tile(Q // N_DEV if (out_dev or (b_dev and mode == "nn")) else Q, _TQ)
    tr = _tile(R // N_DEV if (b_dev and mode == "nt") else R, _TR if mode != "tn" else _TR_TN)
    if tiles is not None:
        tp, tq, tr = (o or d for o, d in zip(tiles, (tp, tq, tr)))
    if pair:
        assert mode == "nt" and b_dev and tr == R // N_DEV
        tr = 2 * tr
    nk = R // tr
    qd = (Q // N_DEV) // tq
    rd = (R // N_DEV) // tr
    if mode == "nn":
        a_spec = pl.BlockSpec((tp, tr), lambda i, j, k: (i, k))
        b_spec = (pl.BlockSpec((None, tr, tq), lambda i, j, k: (j // qd, k, j % qd)) if b_dev
                  else pl.BlockSpec((tr, tq), lambda i, j, k: (k, j)))
        dims = (((1,), (0,)), ((), ()))
    elif mode == "nt":
        kh = (R // 2) // tr
        a_spec = (pl.BlockSpec((None, tp, tr), lambda i, j, k: (k // kh, i, k % kh)) if halves
                  else pl.BlockSpec((tp, tr), lambda i, j, k: (i, k)))
        if pair:
            b_spec = pl.BlockSpec((2, tq, tr // 2), lambda i, j, k: (k, j, 0))
        else:
            b_spec = (pl.BlockSpec((None, tq, tr), lambda i, j, k: (k // rd, j, k % rd)) if b_dev
                      else pl.BlockSpec((tq, tr), lambda i, j, k: (j, k)))
        dims = (((1,), (1,)), ((), ()))
    else:
        pb = p0 // tp
        qh = (Q // 2) // tq
        a_spec = pl.BlockSpec((tr, tp), lambda i, j, k: (k, i + pb))
        b_spec = (pl.BlockSpec((None, tr, tq), lambda i, j, k: (j // qh, k, j % qh)) if halves
                  else pl.BlockSpec((tr, tq), lambda i, j, k: (k, j)))
        dims = (((0,), (0,)), ((), ()))
    ins, in_specs = [a, b], [a_spec, b_spec]
    if bias is not None:
        ins.append(bias)
        in_specs.append(pl.BlockSpec((1, tq), lambda i, j, k: (0, j)))
    gated = res is not None
    if gated:
        n_seg = gate.shape[0]
        ins += [res, gate]
        in_specs += [pl.BlockSpec((tp, tq), lambda i, j, k: (i, j)),
                     pl.BlockSpec((n_seg, 1, tq), lambda i, j, k: (0, 0, j))]
    if out_dev:
        out_shape = [SDS((N_DEV, P, Q // N_DEV), out_dtype)]
        out_specs = [pl.BlockSpec((None, tp, tq), lambda i, j, k: (j // qd, i, j % qd))]
    else:
        out_shape = [SDS((P, Q), out_dtype)]
        out_specs = [pl.BlockSpec((tp, tq), lambda i, j, k: (i, j))]
    if gated:
        out_shape.append(SDS((P, Q), BF16))
        out_specs.append(pl.BlockSpec((tp, tq), lambda i, j, k: (i, j)))

    def body(*refs):
        a_ref, b_ref = refs[0], refs[1]
        pos = 2
        bias_ref = res_ref = gate_ref = o2_ref = None
        if bias is not None:
            bias_ref = refs[pos]
            pos += 1
        if gated:
            res_ref, gate_ref = refs[pos], refs[pos + 1]
            pos += 2
        o_ref = refs[pos]
        pos += 1
        if gated:
            o2_ref = refs[pos]
            pos += 1
        acc_ref = refs[pos] if nk > 1 else None
        k = pl.program_id(2)
        av = a_ref[...]
        if a_silu:
            av = _silu(av.astype(F32))
        bv = jnp.concatenate([b_ref[0], b_ref[1]], axis=1) if pair else b_ref[...]
        part = lax.dot_general(av.astype(BF16), bv.astype(BF16), dims, preferred_element_type=F32)
        if nk > 1:
            @pl.when(k == 0)
            def _():
                acc_ref[...] = part

            @pl.when(k > 0)
            def _():
                acc_ref[...] += part

        @pl.when(k == nk - 1)
        def _():
            acc = acc_ref[...] if nk > 1 else part
            if bias_ref is not None:
                acc = acc + bias_ref[...]
            if gated:
                if n_seg == 1:
                    g = gate_ref[0]
                else:
                    row = pl.program_id(0) * tp + lax.broadcasted_iota(jnp.int32, (tp, 1), 0)
                    g = jnp.where(row < seg_t, gate_ref[0], gate_ref[1])
                o_ref[...] = (res_ref[...] + g * acc).astype(o_ref.dtype)
                o2_ref[...] = acc.astype(BF16)
            else:
                o_ref[...] = acc.astype(o_ref.dtype)

    out, carried = _carry_call(
        body, carry, grid=(P // tp, Q // tq, nk), out_shape=out_shape, in_specs=in_specs, out_specs=out_specs,
        scratch_shapes=[pltpu.VMEM((tp, tq), F32)] if nk > 1 else [], sem=("parallel", "parallel", "arbitrary"),
        name=name, ins=ins)
    res_out = tuple(out) if gated else out[0]
    return (res_out, carried) if carry else res_out


def _row_tile(seg_t, m):
    return 256 if (seg_t % 256 == 0 and m % 256 == 0) else 128


def _normmod_fwd(x, g, sh, sc, seg_t, name, carry=None):
    M, D = x.shape
    tm = _row_tile(seg_t, M)
    n_seg = sh.shape[0]
    nt = seg_t // tm

    def seg(i):
        return ((i >= nt).astype(jnp.int32) if n_seg == 2 else 0, 0, 0)

    def body(x_ref, g_ref, sh_ref, sc_ref, o_ref):
        xv = x_ref[...]
        r = lax.rsqrt(jnp.mean(xv * xv, axis=-1, keepdims=True) + EPS)
        y = xv * r * g_ref[...]
        o_ref[...] = (y * (1.0 + sc_ref[0]) + sh_ref[0]).astype(BF16)

    out, carried = _carry_call(
        body, carry, out_shape=[SDS((M, D), BF16)], grid=(M // tm,),
        in_specs=[pl.BlockSpec((tm, D), lambda i: (i, 0)), pl.BlockSpec((1, D), lambda i: (0, 0)),
                  pl.BlockSpec((1, 1, D), seg), pl.BlockSpec((1, 1, D), seg)],
        out_specs=[pl.BlockSpec((tm, D), lambda i: (i, 0))], scratch_shapes=[],
        sem=("parallel",), name=name, ins=(x, g, sh, sc))
    return (out[0], carried) if carry else out[0]


def _normmod_bwd(x, g, sc, dh, dx_in, seg_t, name, o_prev=None, gate_prev=None):
    M, D = x.shape
    tm = _row_tile(seg_t, M)
    n_seg = sc.shape[0]
    nt = seg_t // tm
    n_in = dx_in.shape[0] // tm
    with_prev = o_prev is not None
    n_segp = gate_prev.shape[0] if with_prev else 0

    def seg(i):
        return ((i >= nt).astype(jnp.int32) if n_seg == 2 else 0, 0, 0)

    def segp(i):
        return ((i >= nt).astype(jnp.int32) if n_segp == 2 else 0, 0, 0)

    def body(*refs):
        x_ref, g_ref, sc_ref, dh_ref, dxin_ref = refs[:5]
        pos = 5
        if with_prev:
            op_ref, gp_ref = refs[5], refs[6]
            pos = 7
        dx_ref, dg_ref, dsh_ref, dsc_ref = refs[pos:pos + 4]
        if with_prev:
            dop_ref, dgp_ref = refs[pos + 4], refs[pos + 5]
        i = pl.program_id(0)
        xv = x_ref[...]
        r = lax.rsqrt(jnp.mean(xv * xv, axis=-1, keepdims=True) + EPS)
        xh = xv * r
        gv = g_ref[...]
        dhv = dh_ref[...].astype(F32)
        dy = dhv * (1.0 + sc_ref[0])
        dxh = dy * gv
        dxv = r * (dxh - xh * jnp.mean(dxh * xh, axis=-1, keepdims=True))
        if n_in * tm < M:
            dxv = dxv + jnp.where(i < n_in, dxin_ref[...], 0.0)
        else:
            dxv = dxv + dxin_ref[...]
        dx_ref[...] = dxv

        @pl.when(i == 0)
        def _():
            dg_ref[...] = jnp.zeros_like(dg_ref)

        first_of_seg = (i == 0) | (i == nt) if n_seg == 2 else (i == 0)

        @pl.when(first_of_seg)
        def _():
            dsh_ref[...] = jnp.zeros_like(dsh_ref)
            dsc_ref[...] = jnp.zeros_like(dsc_ref)

        dg_ref[...] += _colsum(dy * xh)
        dsh_ref[0] += _colsum(dhv)
        dsc_ref[0] += _colsum(dhv * xh * gv)
        if with_prev:
            first_of_segp = (i == 0) | (i == nt) if n_segp == 2 else (i == 0)

            @pl.when(first_of_segp)
            def _():
                dgp_ref[...] = jnp.zeros_like(dgp_ref)

            dop_ref[...] = (gp_ref[0] * dxv).astype(BF16)
            dgp_ref[0] += _colsum(dxv * op_ref[...].astype(F32))

    row = pl.BlockSpec((tm, D), lambda i: (i, 0))
    ins = [x, g, sc, dh, dx_in]
    in_specs = [row, pl.BlockSpec((1, D), lambda i: (0, 0)), pl.BlockSpec((1, 1, D), seg), row,
                pl.BlockSpec((tm, D), lambda i: (jnp.minimum(i, n_in - 1), 0))]
    out_shape = [SDS((M, D), F32), SDS((1, D), F32), SDS((n_seg, 1, D), F32), SDS((n_seg, 1, D), F32)]
    out_specs = [row, pl.BlockSpec((1, D), lambda i: (0, 0)), pl.BlockSpec((1, 1, D), seg), pl.BlockSpec((1, 1, D), seg)]
    if with_prev:
        ins += [o_prev, gate_prev]
        in_specs += [row, pl.BlockSpec((1, 1, D), segp)]
        out_shape += [SDS((M, D), BF16), SDS((n_segp, 1, D), F32)]
        out_specs += [row, pl.BlockSpec((1, 1, D), segp)]
    return pl.pallas_call(
        body, out_shape=out_shape, grid=(M // tm,), in_specs=in_specs, out_specs=out_specs,
        compiler_params=_params(("arbitrary",)), name=name)(*ins)


def _final(x, g, target, o_prev, gate_prev, name):
    T, D = x.shape
    tm = _tile(T, (256, 128))

    def body(x_ref, g_ref, t_ref, op_ref, gp_ref, dx_ref, loss_ref, dg_ref, dop_ref, dgp_ref):
        i = pl.program_id(0)
        xv = x_ref[...]
        r = lax.rsqrt(jnp.mean(xv * xv, axis=-1, keepdims=True) + EPS)
        xh = xv * r
        gv = g_ref[...]
        e = xh * gv - t_ref[...]
        dout = e * (1.0 / D)
        dxh = dout * gv
        dxv = r * (dxh - xh * jnp.mean(dxh * xh, axis=-1, keepdims=True))
        dx_ref[...] = dxv
        dop_ref[...] = (gp_ref[0] * dxv).astype(BF16)

        @pl.when(i == 0)
        def _():
            loss_ref[...] = jnp.zeros_like(loss_ref)
            dg_ref[...] = jnp.zeros_like(dg_ref)
            dgp_ref[...] = jnp.zeros_like(dgp_ref)

        loss_ref[...] += _colsum(e * e) * (0.5 / D)
        dg_ref[...] += _colsum(dout * xh)
        dgp_ref[0] += _colsum(dxv * op_ref[...].astype(F32))

    row = pl.BlockSpec((tm, D), lambda i: (i, 0))
    vec = pl.BlockSpec((1, D), lambda i: (0, 0))
    vec3 = pl.BlockSpec((1, 1, D), lambda i: (0, 0, 0))
    return pl.pallas_call(
        body, out_shape=[SDS((T, D), F32), SDS((1, D), F32), SDS((1, D), F32), SDS((T, D), BF16), SDS((1, 1, D), F32)],
        grid=(T // tm,), in_specs=[row, vec, row, row, vec3], out_specs=[row, vec, vec, row, vec3],
        compiler_params=_params(("arbitrary",)), name=name)(x, g, target, o_prev, gate_prev)


def _gmlp_core(z, lg, lb, ws_ref, bst):
    W = z.shape[1] // 2
    t = _gelu(z)
    u, v = t[:, :W], t[:, W:]
    mu = jnp.mean(v, axis=-1, keepdims=True)
    vc = v - mu
    rstd = lax.rsqrt(jnp.mean(vc * vc, axis=-1, keepdims=True) + EPS)
    vhat = vc * rstd
    vn = vhat * lg + lb
    vp = []
    for h in range(W // CHUNK):
        blk = vn[:, h * CHUNK:(h + 1) * CHUNK].astype(BF16)
        vp.append(jnp.dot(ws_ref[h].astype(BF16), blk, preferred_element_type=F32) + bst[:, h:h + 1])
    return u, vhat, rstd, vp


def _gmlp_fwd(z, ln_g, ln_b, w_s, b_st, W, name, carry=None):
    M = z.shape[0]
    H = W // CHUNK

    def body(z_ref, lg_ref, lb_ref, ws_ref, bst_ref, o_ref):
        u, _, _, vp = _gmlp_core(z_ref[...], lg_ref[...], lb_ref[...], ws_ref, bst_ref[...])
        for h in range(H):
            o_ref[:, h * CHUNK:(h + 1) * CHUNK] = (u[:, h * CHUNK:(h + 1) * CHUNK] * vp[h]).astype(BF16)

    vec = pl.BlockSpec((1, W), lambda i: (0, 0))
    out, carried = _carry_call(
        body, carry, out_shape=[SDS((M, 2 * W), BF16)], grid=(M // CHUNK,),
        in_specs=[pl.BlockSpec((CHUNK, 2 * W), lambda i: (i, 0)), vec, vec,
                  pl.BlockSpec((H, CHUNK, CHUNK), lambda i: (0, 0, 0)), pl.BlockSpec((CHUNK, H), lambda i: (0, 0))],
        out_specs=[pl.BlockSpec((CHUNK, W), lambda i: (i, 0))], scratch_shapes=[],
        sem=("parallel",), name=name, ins=(z, ln_g, ln_b, w_s, b_st))
    return (out[0], carried) if carry else out[0]


def _gmlp_bwd(z, dy, ln_g, ln_b, w_s, b_st, W, name):
    M = z.shape[0]
    H = W // CHUNK
    ZW = z.shape[1]

    def body(z_ref, dy_ref, lg_ref, lb_ref, ws_ref, bst_ref, dz_ref, dlg_ref, dlb_ref, dws_ref, dbs_ref, dbin_ref):
        i = pl.program_id(0)

        @pl.when(i == 0)
        def _():
            for r in (dlg_ref, dlb_ref, dws_ref, dbs_ref, dbin_ref):
                r[...] = jnp.zeros_like(r)

        zv = z_ref[...]
        lg = lg_ref[...]
        u, vhat, rstd, vp = _gmlp_core(zv, lg, lb_ref[...], ws_ref, bst_ref[...])
        vn = vhat * lg + lb_ref[...]
        dya = dy_ref[...]
        du_parts, dvn_parts = [], []
        for h in range(H):
            sl = slice(h * CHUNK, (h + 1) * CHUNK)
            dya_h = dya[:, sl]
            du_parts.append(dya_h * vp[h])
            dvp = dya_h * u[:, sl]
            dbs_ref[h] += dvp
            dvp16 = dvp.astype(BF16)
            dws_ref[h] += lax.dot_general(dvp16, vn[:, sl].astype(BF16), (((1,), (1,)), ((), ())),
                                          preferred_element_type=F32)
            dvn_parts.append(lax.dot_general(ws_ref[h].astype(BF16), dvp16, (((0,), (0,)), ((), ())),
                                             preferred_element_type=F32))
        du = jnp.concatenate(du_parts, axis=1)
        dvn = jnp.concatenate(dvn_parts, axis=1)
        dlg_ref[...] += _colsum(dvn * vhat)
        dlb_ref[...] += _colsum(dvn)
        dvh = dvn * lg
        dv = rstd * (dvh - jnp.mean(dvh, axis=-1, keepdims=True) - vhat * jnp.mean(dvh * vhat, axis=-1, keepdims=True))
        dz = jnp.concatenate([du, dv], axis=1) * _dgelu(zv)
        dbin_ref[...] += _colsum(dz)
        dz_ref[...] = dz.astype(BF16)

    vec = pl.BlockSpec((1, W), lambda i: (0, 0))
    mat = pl.BlockSpec((H, CHUNK, CHUNK), lambda i: (0, 0, 0))
    return pl.pallas_call(
        body,
        out_shape=[SDS((M, ZW), BF16), SDS((1, W), F32), SDS((1, W), F32), SDS((H, CHUNK, CHUNK), F32),
                   SDS((H, CHUNK, CHUNK), F32), SDS((1, 2 * W), F32)],
        grid=(M // CHUNK,),
        in_specs=[pl.BlockSpec((CHUNK, 2 * W), lambda i: (i, 0)), pl.BlockSpec((CHUNK, W), lambda i: (i, 0)), vec, vec,
                  mat, pl.BlockSpec((CHUNK, H), lambda i: (0, 0))],
        out_specs=[pl.BlockSpec((CHUNK, 2 * W), lambda i: (i, 0)), vec, vec, mat, mat,
                   pl.BlockSpec((1, 2 * W), lambda i: (0, 0))],
        compiler_params=_params(("arbitrary",)), name=name)(z, dy, ln_g, ln_b, w_s, b_st)


def _halo_specs(tm, width, col, n_rows):
    per = tm // HALO
    last = n_rows // HALO - 1
    prev = pl.BlockSpec((HALO, width), lambda i: (jnp.maximum(i * per - 1, 0), col))
    nxt = pl.BlockSpec((HALO, width), lambda i: (jnp.minimum((i + 1) * per, last), col))
    return prev, nxt


def _edge_flags(i, tm, seg_t, m):
    r0 = i * tm
    has_prev = jnp.where((r0 == 0) | (r0 == seg_t), 0.0, 1.0)
    has_next = jnp.where((r0 + tm == seg_t) | (r0 + tm == m), 0.0, 1.0)
    return has_prev, has_next


def _glu(zz, wb):
    return zz[:, :wb] * _sigmoid(zz[:, wb:])


def _build_shifts(src_ref, sh_ref):
    n = src_ref.shape[0] - 8
    for r in range(1, 8):
        sh_ref[r - 1, pl.ds(0, n), :] = src_ref[pl.ds(r, n), :]


def _shifted(src_ref, sh_ref, off, r0, rc):
    a, r = divmod(off, 8)
    if r == 0:
        return src_ref[pl.ds(8 * a + r0, rc), :]
    return sh_ref[r - 1, pl.ds(8 * a + r0, rc), :]


def _conv_taps(src_ref, sh_ref, w_ref, first, tm, kw, flip=False):
    rc = 32
    parts = []
    for c in range(tm // rc):
        acc = None
        for k in range(kw):
            wk = w_ref[pl.ds(kw - 1 - k if flip else k, 1), :]
            term = _shifted(src_ref, sh_ref, first + k, c * rc, rc) * wk
            acc = term if acc is None else acc + term
        parts.append(acc)
    return jnp.concatenate(parts, axis=0)


def _conf_fwd(z, y, conv_w, conv_b, ln_g, ln_b, W, Wb, seg_t, name, carry=None):
    M = z.shape[0]
    tm = _row_tile(seg_t, M)
    kw = conv_w.shape[0]
    pad = (kw - 1) // 2
    col = (2 * W) // (2 * Wb)

    def body(zc_ref, zp_ref, zn_ref, y_hbm, cw_ref, cb_ref, lg_ref, lb_ref, o_ref, hc_ref, hs_ref, sh_ref):
        del y_hbm
        hp, hn = _edge_flags(pl.program_id(0), tm, seg_t, M)
        hs_ref[pl.ds(0, HALO), :] = _glu(zp_ref[...], Wb) * hp
        hs_ref[pl.ds(HALO, tm), :] = _glu(zc_ref[...], Wb)
        hs_ref[pl.ds(HALO + tm, HALO), :] = _glu(zn_ref[...], Wb) * hn
        _build_shifts(hs_ref, sh_ref)
        hc = _conv_taps(hs_ref, sh_ref, cw_ref, HALO - pad, tm, kw) + cb_ref[...]
        hc_ref[...] = hc
        mu = jnp.mean(hc, axis=-1, keepdims=True)
        c = hc - mu
        rstd = lax.rsqrt(jnp.mean(c * c, axis=-1, keepdims=True) + EPS)
        o_ref[...] = _silu(c * rstd * lg_ref[...] + lb_ref[...]).astype(BF16)

    prev, nxt = _halo_specs(tm, 2 * Wb, col, M)
    vec = pl.BlockSpec((1, Wb), lambda i: (0, 0))
    out, carried = _carry_call(
        body, carry, out_shape=[SDS(y.shape, BF16), SDS((M, Wb), F32)], grid=(M // tm,),
        in_specs=[pl.BlockSpec((tm, 2 * Wb), lambda i: (i, col)), prev, nxt, pl.BlockSpec(memory_space=pl.ANY),
                  pl.BlockSpec((kw, Wb), lambda i: (0, 0)), vec, vec, vec],
        out_specs=[pl.BlockSpec((tm, Wb), lambda i: (i, W // Wb)), pl.BlockSpec((tm, Wb), lambda i: (i, 0))],
        scratch_shapes=[pltpu.VMEM((tm + 2 * HALO, Wb), F32), pltpu.VMEM((7, tm + 2 * HALO, Wb), F32)],
        aliases={3: 0}, sem=("parallel",), name=name, ins=(z, z, z, y, conv_w, conv_b, ln_g, ln_b))
    return (out, carried) if carry else out


def _conf_bwd1(hc, dy, ln_g, ln_b, W, Wb, seg_t, name):
    M = hc.shape[0]
    tm = _row_tile(seg_t, M)

    def body(hc_ref, dy_ref, lg_ref, lb_ref, dhc_ref, dlg_ref, dlb_ref, dcb_ref):
        i = pl.program_id(0)

        @pl.when(i == 0)
        def _():
            for r in (dlg_ref, dlb_ref, dcb_ref):
                r[...] = jnp.zeros_like(r)

        hc = hc_ref[...]
        mu = jnp.mean(hc, axis=-1, keepdims=True)
        c = hc - mu
        rstd = lax.rsqrt(jnp.mean(c * c, axis=-1, keepdims=True) + EPS)
        hh = c * rstd
        lg = lg_ref[...]
        dhn = dy_ref[...] * _dsilu(hh * lg + lb_ref[...])
        dlg_ref[...] += _colsum(dhn * hh)
        dlb_ref[...] += _colsum(dhn)
        dhh = dhn * lg
        dhc = rstd * (dhh - jnp.mean(dhh, axis=-1, keepdims=True) - hh * jnp.mean(dhh * hh, axis=-1, keepdims=True))
        dcb_ref[...] += _colsum(dhc)
        dhc_ref[...] = dhc

    vec = pl.BlockSpec((1, Wb), lambda i: (0, 0))
    return pl.pallas_call(
        body, out_shape=[SDS((M, Wb), F32), SDS((1, Wb), F32), SDS((1, Wb), F32), SDS((1, Wb), F32)], grid=(M // tm,),
        in_specs=[pl.BlockSpec((tm, Wb), lambda i: (i, 0)), pl.BlockSpec((tm, Wb), lambda i: (i, W // Wb)), vec, vec],
        out_specs=[pl.BlockSpec((tm, Wb), lambda i: (i, 0)), vec, vec, vec],
        compiler_params=_params(("arbitrary",)), name=name)(hc, dy, ln_g, ln_b)


def _conf_bwd2(z, dhc, dz, conv_w, W, Wb, seg_t, name, carry=None):
    M = z.shape[0]
    tm = _row_tile(seg_t, M)
    kw = conv_w.shape[0]
    pad = (kw - 1) // 2
    col = (2 * W) // (2 * Wb)

    def body(zc_ref, zp_ref, zn_ref, dc_ref, dp_ref, dn_ref, dz_hbm, cw_ref, dz_ref, dcw_ref, dbin_ref, hs_ref, ds_ref,
             hsh_ref, dsh_ref):
        del dz_hbm
        i = pl.program_id(0)

        @pl.when(i == 0)
        def _():
            dcw_ref[...] = jnp.zeros_like(dcw_ref)
            dbin_ref[...] = jnp.zeros_like(dbin_ref)

        hp, hn = _edge_flags(i, tm, seg_t, M)
        zc = zc_ref[...]
        hs_ref[pl.ds(0, HALO), :] = _glu(zp_ref[...], Wb) * hp
        hs_ref[pl.ds(HALO, tm), :] = _glu(zc, Wb)
        hs_ref[pl.ds(HALO + tm, HALO), :] = _glu(zn_ref[...], Wb) * hn
        dcur = dc_ref[...]
        ds_ref[pl.ds(0, HALO), :] = dp_ref[...] * hp
        ds_ref[pl.ds(HALO, tm), :] = dcur
        ds_ref[pl.ds(HALO + tm, HALO), :] = dn_ref[...] * hn
        _build_shifts(ds_ref, dsh_ref)
        _build_shifts(hs_ref, hsh_ref)
        dh = _conv_taps(ds_ref, dsh_ref, cw_ref, HALO - pad, tm, kw, flip=True)
        for k in range(kw):
            dcw_ref[pl.ds(k, 1), :] += _colsum(dcur * _shifted(hs_ref, hsh_ref, HALO - pad + k, 0, tm))
        a, gt = zc[:, :Wb], zc[:, Wb:]
        s = _sigmoid(gt)
        dz = jnp.concatenate([dh * s, dh * a * s * (1.0 - s)], axis=1)
        dbin_ref[...] += _colsum(dz)
        dz_ref[...] = dz.astype(BF16)

    prev, nxt = _halo_specs(tm, 2 * Wb, col, M)
    dprev, dnxt = _halo_specs(tm, Wb, 0, M)
    out, carried = _carry_call(
        body, carry, out_shape=[SDS(dz.shape, BF16), SDS((kw, Wb), F32), SDS((1, 2 * Wb), F32)], grid=(M // tm,),
        in_specs=[pl.BlockSpec((tm, 2 * Wb), lambda i: (i, col)), prev, nxt,
                  pl.BlockSpec((tm, Wb), lambda i: (i, 0)), dprev, dnxt, pl.BlockSpec(memory_space=pl.ANY),
                  pl.BlockSpec((kw, Wb), lambda i: (0, 0))],
        out_specs=[pl.BlockSpec((tm, 2 * Wb), lambda i: (i, col)), pl.BlockSpec((kw, Wb), lambda i: (0, 0)),
                   pl.BlockSpec((1, 2 * Wb), lambda i: (0, 0))],
        scratch_shapes=[pltpu.VMEM((tm + 2 * HALO, Wb), F32), pltpu.VMEM((tm + 2 * HALO, Wb), F32),
                        pltpu.VMEM((7, tm + 2 * HALO, Wb), F32), pltpu.VMEM((7, tm + 2 * HALO, Wb), F32)],
        aliases={6: 0}, sem=("arbitrary",), name=name, ins=(z, z, z, dhc, dhc, dhc, dz, conv_w))
    return (out, carried) if carry else out


_TF = (1408, 512, 256, 128)
_RC = 16
_CG = 256


def _col_groups(width):
    return [(c0, min(_CG, width - c0)) for c0 in range(0, width, _CG)]


def _ffn_act_fwd(z, conv_w, conv_b, seg_t, name):
    M, F2 = z.shape
    Fd = F2 // 2
    tm = _row_tile(seg_t, M)
    tf = _tile(Fd, _TF)
    nf = Fd // tf
    per, last = tm // HALO, M // HALO - 1

    def body(g_ref, gp_ref, gn_ref, u_ref, cw_ref, cb_ref, o_ref, gc_ref, gs_ref):
        hp, hn = _edge_flags(pl.program_id(0), tm, seg_t, M)
        gs_ref[pl.ds(0, HALO), :] = gp_ref[...].astype(F32) * hp
        gs_ref[pl.ds(HALO, tm), :] = g_ref[...].astype(F32)
        gs_ref[pl.ds(HALO + tm, HALO), :] = gn_ref[...].astype(F32) * hn
        for c0, cw in _col_groups(tf):
            cs = pl.ds(c0, cw)
            w0, w1, w2, cb = cw_ref[pl.ds(0, 1), cs], cw_ref[pl.ds(1, 1), cs], cw_ref[pl.ds(2, 1), cs], cb_ref[:, cs]
            for r0 in range(0, tm, _RC):
                gc = (gs_ref[pl.ds(HALO - 1 + r0, _RC), cs] * w0 + gs_ref[pl.ds(HALO + r0, _RC), cs] * w1
                      + gs_ref[pl.ds(HALO + 1 + r0, _RC), cs] * w2 + cb)
                o_ref[pl.ds(r0, _RC), cs] = (_silu(gc) * u_ref[pl.ds(r0, _RC), cs].astype(F32)).astype(BF16)
                gc_ref[pl.ds(r0, _RC), cs] = gc.astype(BF16)

    return pl.pallas_call(
        body, out_shape=[SDS((M, Fd), BF16), SDS((M, Fd), BF16)], grid=(M // tm, nf),
        in_specs=[pl.BlockSpec((tm, tf), lambda i, j: (i, j)),
                  pl.BlockSpec((HALO, tf), lambda i, j: (jnp.maximum(i * per - 1, 0), j)),
                  pl.BlockSpec((HALO, tf), lambda i, j: (jnp.minimum((i + 1) * per, last), j)),
                  pl.BlockSpec((tm, tf), lambda i, j: (i, nf + j)),
                  pl.BlockSpec((3, tf), lambda i, j: (0, j)), pl.BlockSpec((1, tf), lambda i, j: (0, j))],
        out_specs=[pl.BlockSpec((tm, tf), lambda i, j: (i, j)), pl.BlockSpec((tm, tf), lambda i, j: (i, j))],
        scratch_shapes=[pltpu.VMEM((tm + 2 * HALO, tf), F32)],
        compiler_params=_params(("parallel", "parallel")), name=name)(z, z, z, z, conv_w, conv_b)


def _ffn_act_bwd(z, gc, da, conv_w, seg_t, name):
    M, F2 = z.shape
    Fd = F2 // 2
    tm = _row_tile(seg_t, M)
    tf = _tile(Fd, _TF)
    nf = Fd // tf
    per, last = tm // HALO, M // HALO - 1
    n_piece = tm // _RC

    def body(g_ref, c_ref, cp_ref, cn_ref, u_ref, up_ref, un_ref, a_ref, ap_ref, an_ref, cw_ref,
             dz_ref, dcw_ref, dcb_ref, ds_ref):
        i = pl.program_id(1)

        def tile():
            hp, hn = _edge_flags(i, tm, seg_t, M)

            @pl.when(i == 0)
            def _():
                dcw_ref[...] = jnp.zeros_like(dcw_ref)
                dcb_ref[...] = jnp.zeros_like(dcb_ref)

            def fold(v):
                return v[:8] + v[8:]

            for c0, cw in _col_groups(tf):
                cs = pl.ds(c0, cw)
                w0, w1, w2 = cw_ref[pl.ds(0, 1), cs], cw_ref[pl.ds(1, 1), cs], cw_ref[pl.ds(2, 1), cs]
                for ci in range(-1, n_piece + 1):
                    r0 = ci * _RC
                    if ci < 0:
                        gcv, ue, ae = cp_ref[:, cs], up_ref[:, cs], ap_ref[:, cs].astype(F32) * hp
                    elif ci == n_piece:
                        gcv, ue, ae = cn_ref[:, cs], un_ref[:, cs], an_ref[:, cs].astype(F32) * hn
                    else:
                        rows = pl.ds(r0, _RC)
                        gcv, ue, ae = c_ref[rows, cs], u_ref[rows, cs], a_ref[rows, cs].astype(F32)
                    gcv, ue = gcv.astype(F32), ue.astype(F32)
                    sg = _sigmoid(gcv)
                    t = ae * sg
                    ds_ref[pl.ds(HALO + r0, _RC), cs] = t * ue * (1.0 + gcv * (1.0 - sg))
                    if 0 <= ci < n_piece:
                        dz_ref[1, pl.ds(r0, _RC), cs] = (t * gcv).astype(BF16)
                acc = [jnp.zeros((8, cw), F32) for _ in range(4)]
                for r0 in range(0, tm, _RC):
                    b = HALO + r0
                    d = [ds_ref[pl.ds(b + 1 - k, _RC), cs] for k in range(3)]
                    dz_ref[0, pl.ds(r0, _RC), cs] = (d[0] * w0 + d[1] * w1 + d[2] * w2).astype(BF16)
                    gv = g_ref[pl.ds(r0, _RC), cs].astype(F32)
                    for k in range(3):
                        acc[k] = acc[k] + fold(d[k] * gv)
                    acc[3] = acc[3] + fold(d[1])
                for k in range(3):
                    dcw_ref[pl.ds(k, 1), cs] += _colsum(acc[k])
                dcb_ref[:, cs] += _colsum(acc[3])

        tile()

    def cur(off):
        return pl.BlockSpec((tm, tf), lambda j, i: (i, off + j))

    def prv(off):
        return pl.BlockSpec((HALO, tf), lambda j, i: (jnp.maximum(i * per - 1, 0), off + j))

    def nxt(off):
        return pl.BlockSpec((HALO, tf), lambda j, i: (jnp.minimum((i + 1) * per, last), off + j))

    return pl.pallas_call(
        body, out_shape=[SDS((2, M, Fd), BF16), SDS((3, Fd), F32), SDS((1, Fd), F32)], grid=(nf, M // tm),
        in_specs=[cur(0), cur(0), prv(0), nxt(0), cur(nf), prv(nf), nxt(nf), cur(0), prv(0), nxt(0),
                  pl.BlockSpec((3, tf), lambda j, i: (0, j))],
        out_specs=[pl.BlockSpec((2, tm, tf), lambda j, i: (0, i, j)),
                   pl.BlockSpec((3, tf), lambda j, i: (0, j)), pl.BlockSpec((1, tf), lambda j, i: (0, j))],
        scratch_shapes=[pltpu.VMEM((tm + 2 * HALO, tf), F32)],
        compiler_params=_params(("parallel", "arbitrary")), name=name)(
            z, gc, gc, gc, z, z, z, da, da, da, conv_w)


_LN2 = math.log(2.0)
_QSCALE = (NOPE + ROPE) ** -0.5 / _LN2


def _swap32(x):
    lane = lax.broadcasted_iota(jnp.int32, x.shape, 1)
    return jnp.where((lane % 64) < 32, pltpu.roll(x, 96, axis=1), pltpu.roll(x, 32, axis=1))


def _rms(x, g):
    r = lax.rsqrt(jnp.mean(x * x, axis=-1, keepdims=True) + EPS)
    return x * r * g


def _rms_bwd(x, g, dy):
    r = lax.rsqrt(jnp.mean(x * x, axis=-1, keepdims=True) + EPS)
    xh = x * r
    dxh = dy * g
    return r * (dxh - xh * jnp.mean(dxh * xh, axis=-1, keepdims=True)), _colsum(dy * xh)


def _mla_prep_fwd(z, gq, gkv, cos, sin, QL, KL, name):
    M, NZ = z.shape
    tm = _tile(M, (256, 128))

    def body(z_ref, gq_ref, gkv_ref, cos_ref, sin_ref, cq_ref, ckv_ref, kpe_ref):
        zv = z_ref[...]
        cq_ref[...] = _rms(zv[:, :QL], gq_ref[...]).astype(BF16)
        ckv_ref[...] = _rms(zv[:, QL:QL + KL], gkv_ref[...]).astype(BF16)
        kp = zv[:, QL + KL:]
        r = kp * cos_ref[...] + _swap32(kp) * sin_ref[...]
        lane = lax.broadcasted_iota(jnp.int32, r.shape, 1)
        kpe_ref[0] = jnp.where(lane < ROPE, r, 0.0).astype(BF16)
        kpe_ref[1] = jnp.where(lane >= ROPE, r, 0.0).astype(BF16)

    tab = pl.BlockSpec((tm, 128), lambda i: (i, 0))
    return pl.pallas_call(
        body, out_shape=[SDS((M, QL), BF16), SDS((M, KL), BF16), SDS((2, M, 128), BF16)], grid=(M // tm,),
        in_specs=[pl.BlockSpec((tm, NZ), lambda i: (i, 0)), pl.BlockSpec((1, QL), lambda i: (0, 0)),
                  pl.BlockSpec((1, KL), lambda i: (0, 0)), tab, tab],
        out_specs=[pl.BlockSpec((tm, QL), lambda i: (i, 0)), pl.BlockSpec((tm, KL), lambda i: (i, 0)),
                   pl.BlockSpec((2, tm, 128), lambda i: (0, i, 0))],
        compiler_params=_params(("parallel",)), name=name)(z, gq, gkv, cos, sin)


def _mla_prep_bwd(z, dcq, dckv, dkpe, gq, gkv, cos, sin, QL, KL, seg_t, name):
    M, NZ = z.shape
    H = dkpe.shape[0]
    tm = _row_tile(seg_t, M)
    nt = seg_t // tm

    def body(z_ref, dcq_ref, dckv_ref, dkpe_ref, gq_ref, gkv_ref, cos_ref, sin_ref, dz_ref, dgq_ref, dgkv_ref):
        i = pl.program_id(0)

        @pl.when(i == 0)
        def _():
            dgq_ref[...] = jnp.zeros_like(dgq_ref)
            dgkv_ref[...] = jnp.zeros_like(dgkv_ref)

        zv = z_ref[...]
        dyq = jnp.where(i < nt, dcq_ref[...], 0.0)
        dxq, dgq = _rms_bwd(zv[:, :QL], gq_ref[...], dyq)
        dxkv, dgkv = _rms_bwd(zv[:, QL:QL + KL], gkv_ref[...], dckv_ref[...])
        dgq_ref[...] += dgq
        dgkv_ref[...] += dgkv
        even = dkpe_ref[0]
        odd = dkpe_ref[1]
        for h in range(2, H, 2):
            even = even + dkpe_ref[h]
            odd = odd + dkpe_ref[h + 1]
        lane = lax.broadcasted_iota(jnp.int32, even.shape, 1)
        dr = jnp.where(lane < ROPE, even, odd)
        dkp = dr * cos_ref[...] - _swap32(dr) * sin_ref[...]
        dz_ref[...] = jnp.concatenate([dxq, dxkv, dkp], axis=1).astype(BF16)

    tab = pl.BlockSpec((tm, 128), lambda i: (i, 0))
    return pl.pallas_call(
        body, out_shape=[SDS((M, NZ), BF16), SDS((1, QL), F32), SDS((1, KL), F32)], grid=(M // tm,),
        in_specs=[pl.BlockSpec((tm, NZ), lambda i: (i, 0)),
                  pl.BlockSpec((tm, QL), lambda i: (jnp.minimum(i, nt - 1), 0)),
                  pl.BlockSpec((tm, KL), lambda i: (i, 0)), pl.BlockSpec((H, tm, 128), lambda i: (0, i, 0)),
                  pl.BlockSpec((1, QL), lambda i: (0, 0)), pl.BlockSpec((1, KL), lambda i: (0, 0)), tab, tab],
        out_specs=[pl.BlockSpec((tm, NZ), lambda i: (i, 0)), pl.BlockSpec((1, QL), lambda i: (0, 0)),
                   pl.BlockSpec((1, KL), lambda i: (0, 0))],
        compiler_params=_params(("arbitrary",)), name=name)(z, dcq, dckv, dkpe, gq, gkv, cos, sin)


def _qrope_fwd(q, cos, sin, HN, name):
    T, NQ = q.shape
    tm = _tile(T, (256, 128))

    def body(q_ref, cos_ref, sin_ref, o_ref):
        o_ref[:, :HN] = (q_ref[:, :HN] * _QSCALE).astype(BF16)
        for cb in range((NQ - HN) // 128):
            sl = slice(HN + cb * 128, HN + (cb + 1) * 128)
            xv = q_ref[:, sl]
            o_ref[:, sl] = ((xv * cos_ref[...] + _swap32(xv) * sin_ref[...]) * _QSCALE).astype(BF16)

    tab = pl.BlockSpec((tm, 128), lambda i: (i, 0))
    return pl.pallas_call(
        body, out_shape=SDS((T, NQ), BF16), grid=(T // tm,),
        in_specs=[pl.BlockSpec((tm, NQ), lambda i: (i, 0)), tab, tab],
        out_specs=pl.BlockSpec((tm, NQ), lambda i: (i, 0)),
        compiler_params=_params(("parallel",)), name=name)(q, cos, sin)


def _qrope_bwd(dqpe, dqa, cos, sin, HN, name):
    T, HW = dqpe.shape
    HR = HW // 2
    tm = _tile(T, (256, 128))

    def body(d_ref, dqa_hbm, cos_ref, sin_ref, o_ref):
        del dqa_hbm
        for pr in range(HR // 128):
            dr = d_ref[:, 2 * pr * 128:(2 * pr + 1) * 128] + d_ref[:, (2 * pr + 1) * 128:(2 * pr + 2) * 128]
            o_ref[:, pr * 128:(pr + 1) * 128] = (dr * cos_ref[...] - _swap32(dr) * sin_ref[...]).astype(BF16)

    tab = pl.BlockSpec((tm, 128), lambda i: (i, 0))
    return pl.pallas_call(
        body, out_shape=SDS(dqa.shape, BF16), grid=(T // tm,),
        in_specs=[pl.BlockSpec((tm, HW), lambda i: (i, 0)), pl.BlockSpec(memory_space=pl.ANY), tab, tab],
        out_specs=pl.BlockSpec((tm, HR), lambda i: (i, HN // HR)),
        input_output_aliases={1: 0}, compiler_params=_params(("parallel",)), name=name)(dqpe, dqa, cos, sin)


_ATT_SUB = 4
_ATT_SUB_B = 4
_NT = (((1,), (1,)), ((), ()))
_TN = (((0,), (0,)), ((), ()))


def _attn_fwd(qa, kv, kpe, T, H, name, carry=None):
    M = kv.shape[0]
    tq = _tile(T, (1024, 512, 256, 128))
    scale = (NOPE + ROPE) ** -0.5

    def body(qn_ref, qp_ref, kv_ref, kpe_ref, o_ref, lse_ref, kc_ref):
        @pl.when(pl.program_id(1) == 0)
        def _():
            kc_ref[:, :NOPE] = kv_ref[:, :NOPE]
            kc_ref[:, NOPE:] = kpe_ref[0]

        rs = tq // _ATT_SUB
        outs, lses = [], []
        for u in range(_ATT_SUB):
            rows = pl.ds(u * rs, rs)
            qc = jnp.concatenate([qn_ref[rows, :], qp_ref[rows, :]], axis=1)
            s = lax.dot_general(qc, kc_ref[...], _NT, preferred_element_type=F32)
            m = jnp.max(s, axis=-1, keepdims=True)
            p = jnp.exp2(s - m)
            l = jnp.sum(p, axis=-1, keepdims=True)
            o = jnp.dot(p.astype(BF16), kv_ref[:, NOPE:], preferred_element_type=F32)
            outs.append((o / l).astype(BF16))
            lses.append(jnp.broadcast_to(m + jnp.log2(l), (rs, 128)))
        o_ref[...] = jnp.concatenate(outs, axis=0)
        lse_ref[...] = jnp.concatenate(lses, axis=0)

    return _carry_call(
        body, carry, out_shape=[SDS((T, H * VDIM), BF16), SDS((T, H * 128), F32)], grid=(H, T // tq),
        in_specs=[pl.BlockSpec((tq, NOPE), lambda h, i: (i, h)), pl.BlockSpec((tq, 128), lambda h, i: (i, H + h // 2)),
                  pl.BlockSpec((M, NOPE + VDIM), lambda h, i: (0, h)), pl.BlockSpec((1, M, 128), lambda h, i: (h % 2, 0, 0))],
        out_specs=[pl.BlockSpec((tq, VDIM), lambda h, i: (i, h)), pl.BlockSpec((tq, 128), lambda h, i: (i, h))],
        scratch_shapes=[pltpu.VMEM((M, NOPE + 128), BF16)],
        sem=("parallel", "arbitrary"), name=name, ins=(qa, qa, kv, kpe))


def _attn_bwd(qa, kv, kpe, do, o, lse, T, H, name, carry=None):
    M = kv.shape[0]
    tq = _tile(T, (1024, 512, 256, 128))
    nq = T // tq
    scale = (NOPE + ROPE) ** -0.5

    def body(qn_ref, qp_ref, kv_ref, kpe_ref, do_ref, o_ref, lse_ref, dqa_ref, dqpe_ref, dkv_ref, dkpe_ref, kc_ref,
             dk_acc, dv_acc):
        i = pl.program_id(1)

        @pl.when(i == 0)
        def _():
            kc_ref[:, :NOPE] = kv_ref[:, :NOPE]
            kc_ref[:, NOPE:] = kpe_ref[0]
            dk_acc[...] = jnp.zeros_like(dk_acc)
            dv_acc[...] = jnp.zeros_like(dv_acc)

        rs = tq // _ATT_SUB_B
        p16s, ds16s = [], []
        for u in range(_ATT_SUB_B):
            rows = pl.ds(u * rs, rs)
            qc = jnp.concatenate([qn_ref[rows, :], qp_ref[rows, :]], axis=1)
            dov = do_ref[rows, :]
            s = lax.dot_general(qc, kc_ref[...], _NT, preferred_element_type=F32)
            p = jnp.exp2(s - lse_ref[rows, 0:1])
            dp = lax.dot_general(dov, kv_ref[:, NOPE:], _NT, preferred_element_type=F32)
            delta = jnp.sum(dov.astype(F32) * o_ref[rows, :].astype(F32), axis=-1, keepdims=True)
            ds16s.append((p * (dp - delta)).astype(BF16))
            p16s.append(p.astype(BF16))
        p16 = jnp.concatenate(p16s, axis=0)
        ds16 = jnp.concatenate(ds16s, axis=0)
        qc = jnp.concatenate([qn_ref[...], qp_ref[...]], axis=1)
        dq = jnp.dot(ds16, kc_ref[...], preferred_element_type=F32) * scale
        dqa_ref[...] = dq[:, :NOPE].astype(BF16)
        dqpe_ref[...] = dq[:, NOPE:]
        dv_acc[...] += lax.dot_general(p16, do_ref[...], _TN, preferred_element_type=F32)
        dk_acc[...] += lax.dot_general(ds16, qc, _TN, preferred_element_type=F32)

        @pl.when(i == nq - 1)
        def _():
            dkv_ref[:, :NOPE] = (dk_acc[:, :NOPE] * _LN2).astype(BF16)
            dkv_ref[:, NOPE:] = dv_acc[...].astype(BF16)
            dkpe_ref[0] = dk_acc[:, NOPE:] * _LN2

    return _carry_call(
        body, carry,
        out_shape=[SDS((T, H * (NOPE + ROPE)), BF16), SDS((T, H * 128), F32), SDS((M, H * (NOPE + VDIM)), BF16),
                   SDS((H, M, 128), F32)],
        grid=(H, nq),
        in_specs=[pl.BlockSpec((tq, NOPE), lambda h, i: (i, h)), pl.BlockSpec((tq, 128), lambda h, i: (i, H + h // 2)),
                  pl.BlockSpec((M, NOPE + VDIM), lambda h, i: (0, h)), pl.BlockSpec((1, M, 128), lambda h, i: (h % 2, 0, 0)),
                  pl.BlockSpec((tq, VDIM), lambda h, i: (i, h)), pl.BlockSpec((tq, VDIM), lambda h, i: (i, h)),
                  pl.BlockSpec((tq, 128), lambda h, i: (i, h))],
        out_specs=[pl.BlockSpec((tq, NOPE), lambda h, i: (i, h)), pl.BlockSpec((tq, 128), lambda h, i: (i, h)),
                   pl.BlockSpec((M, NOPE + VDIM), lambda h, i: (0, h)), pl.BlockSpec((1, M, 128), lambda h, i: (h, 0, 0))],
        scratch_shapes=[pltpu.VMEM((M, NOPE + 128), BF16), pltpu.VMEM((M, NOPE + 128), F32), pltpu.VMEM((M, VDIM), F32)],
        sem=("parallel", "arbitrary"), name=name, ins=(qa, qa, kv, kpe, do, o, lse))


def _adamw(w, m, v, name, g=None, recv=None, carry=None):
    R, C = w.shape
    summed = recv is not None
    n_recv = len(recv) if summed else 1
    runs = [r.shape[1] for r in recv] if summed else [R]
    tr = math.gcd(*runs)
    for cand in (1024, 512, 256, 128, 64, 32, 16, 8):
        if tr % cand == 0 and cand * C <= 131072:
            tr = cand
            break
    first = [sum(runs[:r]) // tr for r in range(n_recv + 1)]
    c1 = 1.0 - ADAM_B1 ** ADAM_STEP
    c2 = 1.0 - ADAM_B2 ** ADAM_STEP

    def update(gv, w_ref, m_ref, v_ref, d_ref, nm_ref, nv_ref):
        mn = ADAM_B1 * m_ref[...] + (1.0 - ADAM_B1) * gv
        vn = ADAM_B2 * v_ref[...] + (1.0 - ADAM_B2) * (gv * gv)
        nm_ref[...] = mn
        nv_ref[...] = vn
        d_ref[...] = -ADAM_LR * ((mn / c1) / (jnp.sqrt(vn / c2) + ADAM_EPS) + ADAM_WD * w_ref[...])

    def body(*refs):
        w_ref, m_ref, v_ref = refs[:3]
        g_refs = refs[3:3 + n_recv]
        outs = refs[3 + n_recv:]
        if not summed:
            update(g_refs[0][...], w_ref, m_ref, v_ref, *outs)
            return
        i = pl.program_id(0)
        for r in range(n_recv):
            @pl.when((i >= first[r]) & (i < first[r + 1]))
            def _():
                gv = g_refs[r][0].astype(F32)
                for d in range(1, N_DEV):
                    gv = gv + g_refs[r][d].astype(F32)
                outs[0][...] = gv
                update(gv, w_ref, m_ref, v_ref, *outs[1:])

    blk = pl.BlockSpec((tr, C), lambda i: (i, 0))
    if summed:
        g_specs = [pl.BlockSpec((N_DEV, tr, C), functools.partial(
            lambda i, lo, n: (0, jnp.clip(i - lo, 0, n - 1), 0), lo=first[r], n=first[r + 1] - first[r]))
                   for r in range(n_recv)]
    else:
        g_specs = [blk]
    n_out = 4 if summed else 3
    out, carried = _carry_call(
        body, carry, out_shape=[SDS((R, C), F32)] * n_out, grid=(R // tr,), in_specs=[blk, blk, blk] + g_specs,
        out_specs=[blk] * n_out, scratch_shapes=[], sem=("parallel",), name=name,
        ins=(w, m, v, *(recv if summed else [g])))
    return (out, carried) if carry else out


WEIGHTS = ['c_ctx', 'norm1_g', 'norm2_g', 'w_ada', 'b_ada', 'ab_w_in', 'ab_b_in', 'a_ln_g', 'a_ln_b', 'a_w_s', 'a_b_s',
           'b_conv_w', 'b_conv_b', 'b_ln_g', 'b_ln_b', 'ab_w_out', 'mla_w_in', 'mla_q_norm_g', 'mla_w_uq',
           'mla_kv_norm_g', 'mla_w_ukv', 'mla_w_o', 'ffn_w_up', 'ffn_conv_w', 'ffn_conv_b', 'ffn_w_down', 'final_norm_g']


def _pack(parts):
    flat = jnp.concatenate([p.reshape(-1).astype(F32) for p in parts])
    n = flat.shape[0]
    unit = 65536 if n > 65536 else 1024
    n_pad = -(-n // unit) * unit
    return jnp.pad(flat, (0, n_pad - n)).reshape(n_pad // 128, 128)


def _unpack(flat, like):
    out, off = [], 0
    for shp in like:
        n = math.prod(shp)
        out.append(flat[..., off:off + n].reshape(flat.shape[:-1] + tuple(shp)))
        off += n
    return out


def _rope_tables(T, Tc):
    rows = T // GRID_W
    row = jnp.repeat(jnp.arange(rows, dtype=F32), GRID_W)
    col = jnp.tile(jnp.arange(GRID_W, dtype=F32), rows)
    n_freq = ROPE // 4
    inv = ROPE_THETA ** (-jnp.arange(n_freq, dtype=F32) / n_freq)
    ang = jnp.concatenate([row[:, None] * inv, col[:, None] * inv], axis=-1)
    cos, sin = jnp.cos(ang), jnp.sin(ang)
    cos = jnp.tile(cos, (1, 128 // (ROPE // 2)))
    sin = jnp.tile(jnp.concatenate([-sin, sin], axis=1), (1, 128 // ROPE))
    return (jnp.concatenate([cos, jnp.ones((Tc, 128), F32)], axis=0),
            jnp.concatenate([sin, jnp.zeros((Tc, 128), F32)], axis=0))


def _step(a):
    ax, ay, ac = lax.axis_index("x"), lax.axis_index("y"), lax.axis_index("c")
    me = 4 * ax + 2 * ay + ac
    T, D = a['x'].shape[1:]
    Tc = a['ctx'].shape[1]
    M = T + Tc
    W, Wb = a['a_ln_g'].shape[1], a['b_ln_g'].shape[1]
    assert W == Wb and T % Tc == 0
    Fd = a['ffn_conv_b'].shape[1]
    QL, KL = a['mla_q_norm_g'].shape[1] * N_DEV, a['mla_kv_norm_g'].shape[1] * N_DEV
    H = a['mla_w_ukv'].shape[2] * N_DEV // (NOPE + VDIM)
    HN, HR = H * NOPE, H * ROPE
    kw = a['b_conv_w'].shape[1]
    NA = a['w_ada'].shape[2]
    bf = lambda t: t.astype(BF16)

    small_shapes = [(D,), (kw, Wb // N_DEV), (2, 3, Fd // N_DEV), (QL // N_DEV,), (KL // N_DEV,)]
    g_small = _all_gather(_pack([a['c'][0], a['b_conv_w'][0], a['ffn_conv_w'], a['mla_q_norm_g'][0], a['mla_kv_norm_g'][0]]),
                          "ag_small")
    c_all, bcw, fcw, gq, gkv = _unpack(g_small.reshape(N_DEV, -1), small_shapes)
    bcw = jnp.transpose(bcw, (1, 0, 2)).reshape(kw, Wb)
    fcw = jnp.transpose(fcw, (1, 2, 0, 3)).reshape(2, 3, Fd)
    gq, gkv = gq.reshape(1, QL), gkv.reshape(1, KL)

    a16 = jnp.concatenate([c_all, a['c_ctx'][None], jnp.zeros((N_DEV - 1, D), F32)], axis=0)
    b_loc = lax.dynamic_slice(a['b_ada'], (0, me * NA), (2, NA))
    mods = [_mm(a16, a['w_ada'][l], mode="nn", out_dtype=F32, name=f"ada_fwd{l}", bias=b_loc[l:l + 1], a_silu=True)
            for l in range(2)]
    gm = _all_gather(jnp.concatenate(mods, axis=0), "ag_mod").reshape(N_DEV, 2, 2 * N_DEV, NA)
    gm = jnp.transpose(gm, (1, 2, 0, 3)).reshape(2, 2 * N_DEV, 6 * D)
    mod_lat = [lax.dynamic_slice(gm[l], (me, 0), (1, 6 * D)).reshape(6, 1, 1, D) for l in range(2)]
    mod_ctx = [gm[l][N_DEV].reshape(6, 1, 1, D) for l in range(2)]

    def mod(l, k, both):
        return jnp.concatenate([mod_lat[l][k], mod_ctx[l][k]], axis=0) if both else mod_lat[l][k]

    def from_cols(g):
        return jnp.transpose(g, (1, 0, 2)).reshape(g.shape[1], -1)

    def from_rows(g):
        return g.reshape(-1, g.shape[2])

    def ag(x):
        return (x, False)

    def a2a(x):
        return (x, True)

    cos, sin = _rope_tables(T, Tc)
    n1g, n2g = a['norm1_g'], a['norm2_g']
    a_bst = a['a_b_s'][0].T
    mm = functools.partial(_mm)
    up_sh, dn_sh = bf(a['ffn_w_up']), bf(a['ffn_w_down'])

    x0 = jnp.concatenate([a['x'][0], a['ctx'][0]], axis=0)
    h1, (w_abin,) = _normmod_fwd(x0, n1g[0:1], mod(0, 0, True), mod(0, 1, True), T, "l0_norm1",
                                 carry=[ag(bf(a['ab_w_in'][0]))])
    s1, s2, s3 = 3 * D // 16, 6 * D // 16, 11 * D // 16
    z, (g_about, g_up0a) = mm(h1, w_abin, mode="nn", out_dtype=F32, name="l0_ab_in", bias=a['ab_b_in'], b_dev=True,
                              carry=[ag(bf(a['ab_w_out'][0])), ag(up_sh[0][:s1])])
    w_about = from_rows(g_about)
    y, (g_up0b,) = _gmlp_fwd(z, a['a_ln_g'], a['a_ln_b'], a['a_w_s'][0], a_bst, W, "l0_gmlp",
                             carry=[ag(up_sh[0][s1:s2])])
    (y, hc_b), (g_up0c,) = _conf_fwd(z, y, bcw, a['b_conv_b'], a['b_ln_g'], a['b_ln_b'], W, Wb, T, "l0_conf",
                                     carry=[ag(up_sh[0][s2:s3])])
    (x1, o1), (g_up0d,) = mm(y, w_about, mode="nn", out_dtype=F32, name="l0_ab_out", res=x0, gate=mod(0, 2, True),
                             seg_t=T, carry=[ag(up_sh[0][s3:])])
    w_up = [jnp.concatenate([g_up0a, g_up0b, g_up0c, g_up0d], axis=1), None]
    h2 = _normmod_fwd(x1, n2g[0:1], mod(0, 3, True), mod(0, 4, True), T, "l0_norm2")
    z2, (g_dn0,) = mm(h2, w_up[0], mode="nn", out_dtype=BF16, name="l0_up", b_dev=True, carry=[ag(dn_sh[0])])
    w_dn = [from_rows(g_dn0), None]
    a2, gc2 = _ffn_act_fwd(z2, fcw[0], a['ffn_conv_b'][0:1], T, "l0_act")
    (x2, o2), (g_in, g_uq) = mm(a2, w_dn[0], mode="nn", out_dtype=F32, name="l0_down", res=x1, gate=mod(0, 5, True),
                                seg_t=T, tiles=(M // 8, None, Fd // 2),
                                carry=[ag(bf(a['mla_w_in'][0])), ag(bf(a['mla_w_uq'][0]))])
    w_in = from_rows(g_in)
    w_in = jnp.concatenate([w_in, w_in[:, QL + KL:]], axis=1)
    w_uq = from_cols(g_uq).reshape(QL, H, NOPE + ROPE)
    w_uq = jnp.concatenate([w_uq[:, :, :NOPE].reshape(QL, HN), w_uq[:, :, NOPE:].reshape(QL, HR)], axis=1)

    h3 = _normmod_fwd(x2, n1g[1:2], mod(1, 0, True), mod(1, 1, True), T, "l1_norm1")
    z3, (g_ukv,) = mm(h3, w_in, mode="nn", out_dtype=F32, name="l1_mla_in", carry=[ag(bf(a['mla_w_ukv'][0]))])
    w_ukv = g_ukv
    cqn, ckvn, kpe = _mla_prep_fwd(z3, gq, gkv, cos, sin, QL, KL, "l1_prep")
    q, (g_wo,) = mm(cqn, w_uq, mode="nn", out_dtype=F32, name="l1_uq", rows=T, carry=[ag(bf(a['mla_w_o'][0]))])
    w_o = from_rows(g_wo)
    kv = mm(ckvn, w_ukv, mode="nn", out_dtype=BF16, name="l1_ukv", b_dev=True)
    qa = _qrope_fwd(q, cos, sin, HN, "l1_qrope")
    (o_att, lse), (g_up1,) = _attn_fwd(qa, kv, kpe, T, H, "l1_attn", carry=[ag(up_sh[1])])
    w_up[1] = g_up1
    x3, o3 = mm(o_att, w_o, mode="nn", out_dtype=F32, name="l1_wo", res=x2, gate=mod(1, 2, False), seg_t=T)
    h4 = _normmod_fwd(x3, n2g[1:2], mod(1, 3, False), mod(1, 4, False), T, "l1_norm2")
    z4, (g_dn1,) = mm(h4, w_up[1], mode="nn", out_dtype=BF16, name="l1_up", b_dev=True, carry=[ag(dn_sh[1])])
    w_dn[1] = from_rows(g_dn1)
    a4, gc4 = _ffn_act_fwd(z4, fcw[1], a['ffn_conv_b'][1:2], T, "l1_act")
    x4, o4 = mm(a4, w_dn[1], mode="nn", out_dtype=F32, name="l1_down", res=x3, gate=mod(1, 5, False), seg_t=T,
                tiles=(T // 8, None, Fd // 2))

    dx4, loss_cols, d_fng, do4, dg2_1 = _final(x4, a['final_norm_g'][None], a['loss_target'][0], o4, mod(1, 5, False),
                                               "final")
    loss = lax.psum(jnp.sum(loss_cols), ("x", "y", "c"))

    def cols(dw):
        k, n = dw.shape
        return jnp.transpose(dw.reshape(k, N_DEV, n // N_DEV), (1, 0, 2))

    def rows(dw):
        return dw.reshape(N_DEV, dw.shape[0] // N_DEV, dw.shape[1])

    da4 = mm(do4, w_dn[1], mode="nt", out_dtype=BF16, name="l1_down_dx")
    dw_dn1 = mm(a4, do4, mode="tn", out_dtype=BF16, name="l1_down_dw")
    dz4, dfcw1, dfcb1 = _ffn_act_bwd(z4, gc4, da4, fcw[1], T, "l1_act_bwd")
    dw_up1, (r_dn1,) = mm(h4, dz4, mode="tn", out_dtype=BF16, name="l1_up_dw", out_dev=True, halves=True,
                          carry=[a2a(rows(dw_dn1))])
    dh4 = mm(dz4, w_up[1], mode="nt", out_dtype=F32, name="l1_up_dx", b_dev=True, halves=True, pair=True)
    dx3, dn2g1, dsh2_1, dsc2_1, do3, dg1_1 = _normmod_bwd(x3, n2g[1:2], mod(1, 4, False), dh4, dx4, T, "l1_norm2_bwd",
                                                         o_prev=o3, gate_prev=mod(1, 2, False))
    d_oatt = mm(do3, w_o, mode="nt", out_dtype=BF16, name="l1_wo_dx")
    dw_o = mm(o_att, do3, mode="tn", out_dtype=BF16, name="l1_wo_dw")
    (dqa, dqpe, dkv, dkpe), (r_up1, r_wo) = _attn_bwd(qa, kv, kpe, d_oatt, o_att, lse, T, H, "l1_attn_bwd",
                                                      carry=[a2a(dw_up1), a2a(rows(dw_o))])
    dqa = _qrope_bwd(dqpe, dqa, cos, sin, HN, "l1_qrope_bwd")
    dcq = mm(dqa, w_uq, mode="nt", out_dtype=F32, name="l1_uq_dx")
    dw_uq = mm(cqn, dqa, mode="tn", out_dtype=BF16, name="l1_uq_dw", rows=T)
    dw_uq = jnp.concatenate([dw_uq[:, :HN].reshape(QL, H, NOPE), dw_uq[:, HN:].reshape(QL, H, ROPE)], axis=2)
    dw_uq = dw_uq.reshape(QL, H * (NOPE + ROPE))
    dckv = mm(dkv, w_ukv, mode="nt", out_dtype=F32, name="l1_ukv_dx", b_dev=True)
    dw_ukv = mm(ckvn, dkv, mode="tn", out_dtype=BF16, name="l1_ukv_dw", out_dev=True)
    dz3, dgq, dgkv = _mla_prep_bwd(z3, dcq, dckv, dkpe, gq, gkv, cos, sin, QL, KL, T, "l1_prep_bwd")
    dh3 = mm(dz3, w_in, mode="nt", out_dtype=F32, name="l1_mla_in_dx")
    dw_in = mm(h3, dz3, mode="tn", out_dtype=BF16, name="l1_mla_in_dw").astype(F32)
    dw_in = jnp.concatenate([dw_in[:, :QL + KL], dw_in[:, QL + KL:QL + KL + ROPE] + dw_in[:, QL + KL + ROPE:QL + KL + 2 * ROPE]],
                            axis=1).astype(BF16)
    dx2, dn1g1, dsh1_1, dsc1_1, do2, dg2_0 = _normmod_bwd(x2, n1g[1:2], mod(1, 1, True), dh3, dx3, T, "l1_norm1_bwd",
                                                         o_prev=o2, gate_prev=mod(0, 5, True))
    da2, (r_uq, r_ukv) = mm(do2, w_dn[0], mode="nt", out_dtype=BF16, name="l0_down_dx",
                            carry=[a2a(cols(dw_uq)), a2a(dw_ukv)])
    dw_dn0, (r_in,) = mm(a2, do2, mode="tn", out_dtype=BF16, name="l0_down_dw", carry=[a2a(rows(dw_in))])
    dz2, dfcw0, dfcb0 = _ffn_act_bwd(z2, gc2, da2, fcw[0], T, "l0_act_bwd")
    dw_up0, (r_dn0,) = mm(h2, dz2, mode="tn", out_dtype=BF16, name="l0_up_dw", out_dev=True, halves=True,
                          carry=[a2a(rows(dw_dn0))])
    dh2, (r_up0a,) = mm(dz2, w_up[0], mode="nt", out_dtype=F32, name="l0_up_dx", b_dev=True, halves=True, pair=True,
                        carry=[a2a(dw_up0[:, :D // 2])])
    dx1, dn2g0, dsh2_0, dsc2_0, do1, dg1_0 = _normmod_bwd(x1, n2g[0:1], mod(0, 4, True), dh2, dx2, T, "l0_norm2_bwd",
                                                         o_prev=o1, gate_prev=mod(0, 2, True))
    dy = mm(do1, w_about, mode="nt", out_dtype=F32, name="l0_ab_out_dx")
    s_up = 13 * D // 16
    dw_about, (r_up0c,) = mm(y, do1, mode="tn", out_dtype=BF16, name="l0_ab_out_dw", carry=[a2a(dw_up0[:, s_up:])])
    dz, dlag, dlab, dws, dbs, dbin_a = _gmlp_bwd(z, dy, a['a_ln_g'], a['a_ln_b'], a['a_w_s'][0], a_bst, W, "l0_gmlp_bwd")
    dhc, dlbg, dlbb, dbcb = _conf_bwd1(hc_b, dy, a['b_ln_g'], a['b_ln_b'], W, Wb, T, "l0_conf_bwd1")
    (dz, dbcw, dbin_b), (r_up0b,) = _conf_bwd2(z, dhc, dz, bcw, W, Wb, T, "l0_conf_bwd2",
                                               carry=[a2a(dw_up0[:, D // 2:s_up])])
    dh1, (r_about,) = mm(dz, w_abin, mode="nt", out_dtype=F32, name="l0_ab_in_dx", b_dev=True,
                         carry=[a2a(rows(dw_about))])
    dw_abin_a = mm(h1, dz, mode="tn", out_dtype=BF16, name="l0_ab_in_dw_a", out_dev=True, p_range=(0, D // 2))
    dw_abin_b, (r_abin_a,) = mm(h1, dz, mode="tn", out_dtype=BF16, name="l0_ab_in_dw_b", out_dev=True,
                                p_range=(D // 2, D // 2), carry=[a2a(dw_abin_a)])
    dx0, dn1g0, dsh1_0, dsc1_0 = _normmod_bwd(x0, n1g[0:1], mod(0, 1, True), dh1, dx1, T, "l0_norm1_bwd")

    zero = jnp.zeros((D,), F32)
    dmod = jnp.stack([
        jnp.stack([jnp.stack([dsh1_0[0, 0], dsc1_0[0, 0], dg1_0[0, 0], dsh2_0[0, 0], dsc2_0[0, 0], dg2_0[0, 0]]),
                   jnp.stack([dsh1_0[1, 0], dsc1_0[1, 0], dg1_0[1, 0], dsh2_0[1, 0], dsc2_0[1, 0], dg2_0[1, 0]])]),
        jnp.stack([jnp.stack([dsh1_1[0, 0], dsc1_1[0, 0], dg1_1[0, 0], dsh2_1[0, 0], dsc2_1[0, 0], dg2_1[0, 0]]),
                   jnp.stack([dsh1_1[1, 0], dsc1_1[1, 0], zero, zero, zero, zero])])])
    small = {
        'norm1_g': jnp.concatenate([dn1g0, dn1g1], axis=0), 'norm2_g': jnp.concatenate([dn2g0, dn2g1], axis=0),
        'ab_b_in': jnp.concatenate([dbin_a, dbin_b], axis=1), 'a_ln_g': dlag, 'a_ln_b': dlab, 'a_w_s': dws[None],
        'a_b_s': jnp.sum(dbs, axis=-1)[None], 'b_conv_w': dbcw, 'b_conv_b': dbcb, 'b_ln_g': dlbg, 'b_ln_b': dlbb,
        'mla_q_norm_g': dgq, 'mla_kv_norm_g': dgkv, 'ffn_conv_w': jnp.stack([dfcw0, dfcw1]),
        'ffn_conv_b': jnp.concatenate([dfcb0, dfcb1], axis=0), 'final_norm_g': d_fng[0],
    }
    names = list(small)
    up2d = (2 * D, a['ffn_w_up'].shape[2])
    res_up, (g2,) = _adamw(a['ffn_w_up'].reshape(up2d), a['m_ffn_w_up'].reshape(up2d), a['v_ffn_w_up'].reshape(up2d),
                           "adamw_ffn_w_up", recv=[r_up0a, r_up0b, r_up0c, r_up1],
                           carry=[ag(_pack([dmod] + [small[n] for n in names]))])
    red = _sum_lead(g2, "sum_small_grads").reshape(-1)
    red = dict(zip(names, _unpack(red, [(2, 2, 6, D)] + [small[n].shape for n in names])[1:]))
    dmod_all = g2.reshape(N_DEV, -1)[:, :2 * 2 * 6 * D].reshape(N_DEV, 2, 2, 6 * D)

    a16g = jnp.concatenate([c_all, jnp.tile(a['c_ctx'][None], (N_DEV, 1))], axis=0)
    dm_loc = lax.dynamic_slice(dmod_all, (0, 0, 0, me * NA), (N_DEV, 2, 2, NA))
    g_wada, cpart = [], []
    for l in range(2):
        dm16 = jnp.concatenate([dm_loc[:, l, 0], dm_loc[:, l, 1]], axis=0)
        g_wada.append(mm(a16g, dm16, mode="tn", out_dtype=F32, name=f"ada_dw{l}", a_silu=True))
        cpart.append(mm(dm_loc[:, l, 1], a['w_ada'][l], mode="nt", out_dtype=F32, name=f"ada_dc{l}"))
    g_bada = _sum_lead(jnp.transpose(dmod_all, (0, 2, 1, 3)).reshape(2 * N_DEV, 2 * 6 * D // 128, 128), "sum_b_ada")
    g_cc = _all_gather(jnp.concatenate(cpart, axis=0), "ag_c_ctx")
    g_cc = _sum_lead(g_cc.reshape(2 * N_DEV * N_DEV, D // 128, 128), "sum_c_ctx").reshape(D)
    grads = {
        'c_ctx': g_cc * _dsilu(a['c_ctx']), 'w_ada': jnp.stack(g_wada), 'b_ada': g_bada.reshape(2, 6 * D),
        'b_conv_w': lax.dynamic_slice(red['b_conv_w'], (0, me * (Wb // N_DEV)), (kw, Wb // N_DEV))[None],
        'ffn_conv_w': lax.dynamic_slice(red['ffn_conv_w'], (0, 0, me * (Fd // N_DEV)), (2, 3, Fd // N_DEV)),
        'mla_q_norm_g': lax.dynamic_slice(red['mla_q_norm_g'], (0, me * (QL // N_DEV)), (1, QL // N_DEV)),
        'mla_kv_norm_g': lax.dynamic_slice(red['mla_kv_norm_g'], (0, me * (KL // N_DEV)), (1, KL // N_DEV)),
    }
    for n in names:
        if n not in grads:
            grads[n] = red[n].reshape(a[n].shape)

    recvs = {'ab_w_out': [r_about], 'mla_w_in': [r_in], 'mla_w_uq': [r_uq], 'mla_w_ukv': [r_ukv],
             'mla_w_o': [r_wo], 'ffn_w_down': [r_dn0, r_dn1]}
    out = {}
    for n in WEIGHTS:
        shp = a[n].shape
        w2 = a[n].reshape(-1, shp[-1])
        m2, v2 = a['m_' + n].reshape(w2.shape), a['v_' + n].reshape(w2.shape)
        if n == 'ffn_w_up':
            res = res_up
        elif n in recvs:
            res = _adamw(w2, m2, v2, "adamw_" + n, recv=recvs[n])
        elif n == 'w_ada':
            g2d = grads[n].reshape(w2.shape)
            res, (r_abin_b,) = _adamw(w2, m2, v2, "adamw_" + n, g=g2d, carry=[a2a(dw_abin_b)])
            res = (g2d,) + tuple(res)
            recvs['ab_w_in'] = [r_abin_a, r_abin_b]
        else:
            g2d = grads[n].reshape(w2.shape)
            res = (g2d,) + tuple(_adamw(w2, m2, v2, "adamw_" + n, g=g2d))
        out[n] = [r.reshape(shp) for r in res]
    return (loss, dx0[:T][None], *[out[n][0] for n in WEIGHTS], *[out[n][1] for n in WEIGHTS],
            *[out[n][2] for n in WEIGHTS], *[out[n][3] for n in WEIGHTS])


def kernel(x, c, ctx, c_ctx, norm1_g, norm2_g, w_ada, b_ada, ab_w_in, ab_b_in, a_ln_g, a_ln_b, a_w_s, a_b_s, b_conv_w, b_conv_b, b_ln_g, b_ln_b, ab_w_out, mla_w_in, mla_q_norm_g, mla_w_uq, mla_kv_norm_g, mla_w_ukv, mla_w_o, ffn_w_up, ffn_conv_w, ffn_conv_b, ffn_w_down, final_norm_g, loss_target, m_c_ctx, m_norm1_g, m_norm2_g, m_w_ada, m_b_ada, m_ab_w_in, m_ab_b_in, m_a_ln_g, m_a_ln_b, m_a_w_s, m_a_b_s, m_b_conv_w, m_b_conv_b, m_b_ln_g, m_b_ln_b, m_ab_w_out, m_mla_w_in, m_mla_q_norm_g, m_mla_w_uq, m_mla_kv_norm_g, m_mla_w_ukv, m_mla_w_o, m_ffn_w_up, m_ffn_conv_w, m_ffn_conv_b, m_ffn_w_down, m_final_norm_g, v_c_ctx, v_norm1_g, v_norm2_g, v_w_ada, v_b_ada, v_ab_w_in, v_ab_b_in, v_a_ln_g, v_a_ln_b, v_a_w_s, v_a_b_s, v_b_conv_w, v_b_conv_b, v_b_ln_g, v_b_ln_b, v_ab_w_out, v_mla_w_in, v_mla_q_norm_g, v_mla_w_uq, v_mla_kv_norm_g, v_mla_w_ukv, v_mla_w_o, v_ffn_w_up, v_ffn_conv_w, v_ffn_conv_b, v_ffn_w_down, v_final_norm_g):
    return _step(dict(locals()))
```

```python
import functools
import math

import jax
import jax.numpy as jnp
from jax import lax
from jax.experimental import pallas as pl
from jax.experimental.pallas import tpu as pltpu

F32 = jnp.float32
BF16 = jnp.bfloat16
SDS = jax.ShapeDtypeStruct

N_DEV = 8
EPS = 1e-6
CHUNK = 128
NOPE = 128
ROPE = 64
VDIM = 128
GRID_W = 64
ROPE_THETA = 10000.0
HALO = 16
ADAM_LR, ADAM_B1, ADAM_B2, ADAM_EPS, ADAM_WD, ADAM_STEP = 0.001, 0.9, 0.999, 1e-08, 0.01, 10
VMEM_LIMIT = 56 * 1024 * 1024


def _tile(n, prefs):
    for p in prefs:
        if n % p == 0:
            return p
    return n


def _params(sem, vmem=VMEM_LIMIT):
    return pltpu.CompilerParams(dimension_semantics=sem, vmem_limit_bytes=vmem)


def _sigmoid(x):
    return 0.5 * jnp.tanh(0.5 * x) + 0.5


def _silu(x):
    return x * _sigmoid(x)


def _dsilu(x):
    s = _sigmoid(x)
    return s * (1.0 + x * (1.0 - s))


_GELU_C = math.sqrt(2.0 / math.pi)


def _gelu(x):
    return 0.5 * x * (1.0 + jnp.tanh(_GELU_C * (x + 0.044715 * x * x * x)))


def _dgelu(x):
    t = jnp.tanh(_GELU_C * (x + 0.044715 * x * x * x))
    return 0.5 * (1.0 + t) + 0.5 * x * (1.0 - t * t) * _GELU_C * (1.0 + 3.0 * 0.044715 * x * x)


def _colsum(v):
    return jnp.sum(v, axis=0, keepdims=True)


_SIBLING = 1
_CHIPS = (2, 4, 6)


def _xchg(x_ref, o_ref, send_sems, recv_sems, local_sem, scatter):
    ax, ay, ac = lax.axis_index("x"), lax.axis_index("y"), lax.axis_index("c")
    me = 4 * ax + 2 * ay + ac

    def dev(k):
        return ax ^ (k >> 2), ay ^ ((k >> 1) & 1), ac ^ (k & 1)

    def idx(k):
        px, py, pc = dev(k)
        return 4 * px + 2 * py + pc

    def copy(k, src, dst, to):
        return pltpu.make_async_remote_copy(src_ref=src, dst_ref=dst, send_sem=send_sems.at[k - 1],
                                            recv_sem=recv_sems.at[k - 1], device_id=dev(to),
                                            device_id_type=pl.DeviceIdType.MESH)

    def own():
        return pltpu.make_async_copy(x_ref.at[me] if scatter else x_ref, o_ref.at[me], local_sem)

    def sends():
        if scatter:
            return [copy(k, x_ref.at[idx(k)], o_ref.at[me], k) for k in range(1, N_DEV)]
        return [copy(k, x_ref, o_ref.at[me], k) for k in (_SIBLING,) + _CHIPS]

    def forwards():
        return [] if scatter else [copy(j + 1, o_ref.at[idx(j)], o_ref.at[idx(j)], _SIBLING) for j in _CHIPS]

    def arrival(k):
        return copy(k, o_ref.at[idx(k)], o_ref.at[idx(k)], k)

    return own, sends, forwards, arrival


def _xchg_start(*refs, scatter):
    own, sends, _, _ = _xchg(*refs, scatter)
    own().start()
    for cp in sends():
        cp.start()


def _xchg_forward(*refs, scatter):
    _, _, forwards, arrival = _xchg(*refs, scatter)
    if not scatter:
        for j, fw in zip(_CHIPS, forwards()):
            arrival(j).wait_recv()
            fw.start()


def _xchg_finish(*refs, scatter):
    own, sends, forwards, arrival = _xchg(*refs, scatter)
    for k in range(1, N_DEV):
        if scatter or k not in _CHIPS:
            arrival(k).wait_recv()
    for cp in sends() + forwards():
        cp.wait_send()
    own().wait()


_XCHG_SEMS = [pltpu.SemaphoreType.DMA((N_DEV - 1,)), pltpu.SemaphoreType.DMA((N_DEV - 1,)), pltpu.SemaphoreType.DMA]


def _xchg_shape(x, scatter):
    return SDS((N_DEV,) + tuple(x.shape[1:] if scatter else x.shape), x.dtype)


def _exchange(x, *, scatter, name):
    def body(*refs):
        _xchg_start(*refs, scatter=scatter)
        _xchg_forward(*refs, scatter=scatter)
        _xchg_finish(*refs, scatter=scatter)

    return pl.pallas_call(
        body, out_shape=_xchg_shape(x, scatter),
        in_specs=[pl.BlockSpec(memory_space=pl.ANY)], out_specs=pl.BlockSpec(memory_space=pl.ANY),
        scratch_shapes=list(_XCHG_SEMS), name=name)(x)


def _carried(body, carry, n_in, n_out, n_scratch, grid):
    nc = len(carry)
    total = math.prod(grid)
    mid = (3 * total) // 4

    def wrapped(*refs):
        ins, cin = refs[:n_in], refs[n_in:n_in + nc]
        o0 = n_in + nc
        outs, cout = refs[o0:o0 + n_out], refs[o0 + n_out:o0 + n_out + nc]
        scr = refs[o0 + n_out + nc:]
        sems = scr[n_scratch:]
        step = pl.program_id(0)
        for ax in range(1, len(grid)):
            step = step * grid[ax] + pl.program_id(ax)

        def each(fn):
            for c in range(nc):
                fn(cin[c], cout[c], *sems[3 * c:3 * c + 3], scatter=carry[c][1])

        @pl.when(step == 0)
        def _():
            each(_xchg_start)

        body(*ins, *outs, *scr[:n_scratch])

        if mid < total - 1:
            @pl.when(step == mid)
            def _():
                each(_xchg_forward)

        @pl.when(step == total - 1)
        def _():
            if mid >= total - 1:
                each(_xchg_forward)
            each(_xchg_finish)

    return wrapped


def _carry_call(body, carry, *, grid, out_shape, in_specs, out_specs, scratch_shapes, sem, name, ins, aliases=None):
    carry = carry or []
    nc = len(carry)
    if nc:
        body = _carried(body, carry, len(in_specs), len(out_shape), len(scratch_shapes), grid)
        anyspec = pl.BlockSpec(memory_space=pl.ANY)
        in_specs = list(in_specs) + [anyspec] * nc
        out_specs = list(out_specs) + [anyspec] * nc
        out_shape = list(out_shape) + [_xchg_shape(x, sc) for x, sc in carry]
        scratch_shapes = list(scratch_shapes) + list(_XCHG_SEMS) * nc
        ins = list(ins) + [x for x, _ in carry]
        sem = ("arbitrary",) * len(grid)
    out = pl.pallas_call(body, out_shape=out_shape, grid=grid, in_specs=in_specs, out_specs=out_specs,
                         scratch_shapes=scratch_shapes, compiler_params=_params(sem), name=name,
                         input_output_aliases=aliases or {})(*ins)
    n_main = len(out) - nc
    return list(out[:n_main]), list(out[n_main:])


def _all_gather(x, name):
    return _exchange(x, scatter=False, name=name)


def _all_to_all(x, name):
    return _exchange(x, scatter=True, name=name)


def _sum_lead(x, name):
    n, R, C = x.shape
    tr = _tile(R, (512, 256, 128, 64, 32, 16, 8))

    def body(x_ref, o_ref):
        acc = x_ref[0]
        for d in range(1, n):
            acc = acc + x_ref[d]
        o_ref[...] = acc

    return pl.pallas_call(
        body, out_shape=SDS((R, C), F32), grid=(R // tr,),
        in_specs=[pl.BlockSpec((n, tr, C), lambda i: (0, i, 0))], out_specs=pl.BlockSpec((tr, C), lambda i: (i, 0)),
        compiler_params=_params(("parallel",)), name=name)(x)


_TP = (1408, 1088, 1024, 768, 512, 256, 128)
_TQ = (1408, 1024, 768, 512, 256, 128)
_TR = (2048, 1408, 1024, 768, 512, 256, 128)
_TR_TN = (2176, 2048, 1088, 1024, 512, 256, 128)


def _mm(a, b, *, mode, out_dtype, name, rows=None, bias=None, res=None, gate=None, seg_t=None, a_silu=False, carry=None,
        b_dev=False, out_dev=False, p_range=None, halves=False, tiles=None, pair=False):
    if mode == "nn":
        P, R, Q = rows or a.shape[0], a.shape[1], (b.shape[0] * b.shape[2] if b_dev else b.shape[1])
    elif mode == "nt":
        P, R, Q = rows or a.shape[-2], (2 * a.shape[2] if halves else a.shape[1]), (b.shape[1] if b_dev else b.shape[0])
    else:
        R, P, Q = rows or a.shape[0], a.shape[1], (2 * b.shape[2] if halves else b.shape[1])
    p0 = 0
    if p_range is not None:
        p0, P = p_range
    tp = _tile(P, _TP)
    tq = _tile(Q // N_DEV if (out_dev or (b_dev and mode == "nn")) else Q, _TQ)
    tr = _tile(R // N_DEV if (b_dev and mode == "nt") else R, _TR if mode != "tn" else _TR_TN)
    if tiles is not None:
        tp, tq, tr = (o or d for o, d in zip(tiles, (tp, tq, tr)))
    if pair:
        assert mode == "nt" and b_dev and tr == R // N_DEV
        tr = 2 * tr
    nk = R // tr
    qd = (Q // N_DEV) // tq
    rd = (R // N_DEV) // tr
    if mode == "nn":
        a_spec = pl.BlockSpec((tp, tr), lambda i, j, k: (i, k))
        b_spec = (pl.BlockSpec((None, tr, tq), lambda i, j, k: (j // qd, k, j % qd)) if b_dev
                  else pl.BlockSpec((tr, tq), lambda i, j, k: (k, j)))
        dims = (((1,), (0,)), ((), ()))
    elif mode == "nt":
        kh = (R // 2) // tr
        a_spec = (pl.BlockSpec((None, tp, tr), lambda i, j, k: (k // kh, i, k % kh)) if halves
                  else pl.BlockSpec((tp, tr), lambda i, j, k: (i, k)))
        if pair:
            b_spec = pl.BlockSpec((2, tq, tr // 2), lambda i, j, k: (k, j, 0))
        else:
            b_spec = (pl.BlockSpec((None, tq, tr), lambda i, j, k: (k // rd, j, k % rd)) if b_dev
                      else pl.BlockSpec((tq, tr), lambda i, j, k: (j, k)))
        dims = (((1,), (1,)), ((), ()))
    else:
        pb = p0 // tp
        qh = (Q // 2) // tq
        a_spec = pl.BlockSpec((tr, tp), lambda i, j, k: (k, i + pb))
        b_spec = (pl.BlockSpec((None, tr, tq), lambda i, j, k: (j // qh, k, j % qh)) if halves
                  else pl.BlockSpec((tr, tq), lambda i, j, k: (k, j)))
        dims = (((0,), (0,)), ((), ()))
    ins, in_specs = [a, b], [a_spec, b_spec]
    if bias is not None:
        ins.append(bias)
        in_specs.append(pl.BlockSpec((1, tq), lambda i, j, k: (0, j)))
    gated = res is not None
    if gated:
        n_seg = gate.shape[0]
        ins += [res, gate]
        in_specs += [pl.BlockSpec((tp, tq), lambda i, j, k: (i, j)),
                     pl.BlockSpec((n_seg, 1, tq), lambda i, j, k: (0, 0, j))]
    if out_dev:
        out_shape = [SDS((N_DEV, P, Q // N_DEV), out_dtype)]
        out_specs = [pl.BlockSpec((None, tp, tq), lambda i, j, k: (j // qd, i, j % qd))]
    else:
        out_shape = [SDS((P, Q), out_dtype)]
        out_specs = [pl.BlockSpec((tp, tq), lambda i, j, k: (i, j))]
    if gated:
        out_shape.append(SDS((P, Q), BF16))
        out_specs.append(pl.BlockSpec((tp, tq), lambda i, j, k: (i, j)))

    def body(*refs):
        a_ref, b_ref = refs[0], refs[1]
        pos = 2
        bias_ref = res_ref = gate_ref = o2_ref = None
        if bias is not None:
            bias_ref = refs[pos]
            pos += 1
        if gated:
            res_ref, gate_ref = refs[pos], refs[pos + 1]
            pos += 2
        o_ref = refs[pos]
        pos += 1
        if gated:
            o2_ref = refs[pos]
            pos += 1
        acc_ref = refs[pos] if nk > 1 else None
        k = pl.program_id(2)
        av = a_ref[...]
        if a_silu:
            av = _silu(av.astype(F32))
        bv = jnp.concatenate([b_ref[0], b_ref[1]], axis=1) if pair else b_ref[...]
        part = lax.dot_general(av.astype(BF16), bv.astype(BF16), dims, preferred_element_type=F32)
        if nk > 1:
            @pl.when(k == 0)
            def _():
                acc_ref[...] = part

            @pl.when(k > 0)
            def _():
                acc_ref[...] += part

        @pl.when(k == nk - 1)
        def _():
            acc = acc_ref[...] if nk > 1 else part
            if bias_ref is not None:
                acc = acc + bias_ref[...]
            if gated:
                if n_seg == 1:
                    g = gate_ref[0]
                else:
                    row = pl.program_id(0) * tp + lax.broadcasted_iota(jnp.int32, (tp, 1), 0)
                    g = jnp.where(row < seg_t, gate_ref[0], gate_ref[1])
                o_ref[...] = (res_ref[...] + g * acc).astype(o_ref.dtype)
                o2_ref[...] = acc.astype(BF16)
            else:
                o_ref[...] = acc.astype(o_ref.dtype)

    out, carried = _carry_call(
        body, carry, grid=(P // tp, Q // tq, nk), out_shape=out_shape, in_specs=in_specs, out_specs=out_specs,
        scratch_shapes=[pltpu.VMEM((tp, tq), F32)] if nk > 1 else [], sem=("parallel", "parallel", "arbitrary"),
        name=name, ins=ins)
    res_out = tuple(out) if gated else out[0]
    return (res_out, carried) if carry else res_out


def _row_tile(seg_t, m):
    return 256 if (seg_t % 256 == 0 and m % 256 == 0) else 128


def _normmod_fwd(x, g, sh, sc, seg_t, name, carry=None):
    M, D = x.shape
    tm = _row_tile(seg_t, M)
    n_seg = sh.shape[0]
    nt = seg_t // tm

    def seg(i):
        return ((i >= nt).astype(jnp.int32) if n_seg == 2 else 0, 0, 0)

    def body(x_ref, g_ref, sh_ref, sc_ref, o_ref):
        xv = x_ref[...]
        r = lax.rsqrt(jnp.mean(xv * xv, axis=-1, keepdims=True) + EPS)
        y = xv * r * g_ref[...]
        o_ref[...] = (y * (1.0 + sc_ref[0]) + sh_ref[0]).astype(BF16)

    out, carried = _carry_call(
        body, carry, out_shape=[SDS((M, D), BF16)], grid=(M // tm,),
        in_specs=[pl.BlockSpec((tm, D), lambda i: (i, 0)), pl.BlockSpec((1, D), lambda i: (0, 0)),
                  pl.BlockSpec((1, 1, D), seg), pl.BlockSpec((1, 1, D), seg)],
        out_specs=[pl.BlockSpec((tm, D), lambda i: (i, 0))], scratch_shapes=[],
        sem=("parallel",), name=name, ins=(x, g, sh, sc))
    return (out[0], carried) if carry else out[0]


def _normmod_bwd(x, g, sc, dh, dx_in, seg_t, name, o_prev=None, gate_prev=None):
    M, D = x.shape
    tm = _row_tile(seg_t, M)
    n_seg = sc.shape[0]
    nt = seg_t // tm
    n_in = dx_in.shape[0] // tm
    with_prev = o_prev is not None
    n_segp = gate_prev.shape[0] if with_prev else 0

    def seg(i):
        return ((i >= nt).astype(jnp.int32) if n_seg == 2 else 0, 0, 0)

    def segp(i):
        return ((i >= nt).astype(jnp.int32) if n_segp == 2 else 0, 0, 0)

    def body(*refs):
        x_ref, g_ref, sc_ref, dh_ref, dxin_ref = refs[:5]
        pos = 5
        if with_prev:
            op_ref, gp_ref = refs[5], refs[6]
            pos = 7
        dx_ref, dg_ref, dsh_ref, dsc_ref = refs[pos:pos + 4]
        if with_prev:
            dop_ref, dgp_ref = refs[pos + 4], refs[pos + 5]
        i = pl.program_id(0)
        xv = x_ref[...]
        r = lax.rsqrt(jnp.mean(xv * xv, axis=-1, keepdims=True) + EPS)
        xh = xv * r
        gv = g_ref[...]
        dhv = dh_ref[...].astype(F32)
        dy = dhv * (1.0 + sc_ref[0])
        dxh = dy * gv
        dxv = r * (dxh - xh * jnp.mean(dxh * xh, axis=-1, keepdims=True))
        if n_in * tm < M:
            dxv = dxv + jnp.where(i < n_in, dxin_ref[...], 0.0)
        else:
            dxv = dxv + dxin_ref[...]
        dx_ref[...] = dxv

        @pl.when(i == 0)
        def _():
            dg_ref[...] = jnp.zeros_like(dg_ref)

        first_of_seg = (i == 0) | (i == nt) if n_seg == 2 else (i == 0)

        @pl.when(first_of_seg)
        def _():
            dsh_ref[...] = jnp.zeros_like(dsh_ref)
            dsc_ref[...] = jnp.zeros_like(dsc_ref)

        dg_ref[...] += _colsum(dy * xh)
        dsh_ref[0] += _colsum(dhv)
        dsc_ref[0] += _colsum(dhv * xh * gv)
        if with_prev:
            first_of_segp = (i == 0) | (i == nt) if n_segp == 2 else (i == 0)

            @pl.when(first_of_segp)
            def _():
                dgp_ref[...] = jnp.zeros_like(dgp_ref)

            dop_ref[...] = (gp_ref[0] * dxv).astype(BF16)
            dgp_ref[0] += _colsum(dxv * op_ref[...].astype(F32))

    row = pl.BlockSpec((tm, D), lambda i: (i, 0))
    ins = [x, g, sc, dh, dx_in]
    in_specs = [row, pl.BlockSpec((1, D), lambda i: (0, 0)), pl.BlockSpec((1, 1, D), seg), row,
                pl.BlockSpec((tm, D), lambda i: (jnp.minimum(i, n_in - 1), 0))]
    out_shape = [SDS((M, D), F32), SDS((1, D), F32), SDS((n_seg, 1, D), F32), SDS((n_seg, 1, D), F32)]
    out_specs = [row, pl.BlockSpec((1, D), lambda i: (0, 0)), pl.BlockSpec((1, 1, D), seg), pl.BlockSpec((1, 1, D), seg)]
    if with_prev:
        ins += [o_prev, gate_prev]
        in_specs += [row, pl.BlockSpec((1, 1, D), segp)]
        out_shape += [SDS((M, D), BF16), SDS((n_segp, 1, D), F32)]
        out_specs += [row, pl.BlockSpec((1, 1, D), segp)]
    return pl.pallas_call(
        body, out_shape=out_shape, grid=(M // tm,), in_specs=in_specs, out_specs=out_specs,
        compiler_params=_params(("arbitrary",)), name=name)(*ins)


def _final(x, g, target, o_prev, gate_prev, name):
    T, D = x.shape
    tm = _tile(T, (256, 128))

    def body(x_ref, g_ref, t_ref, op_ref, gp_ref, dx_ref, loss_ref, dg_ref, dop_ref, dgp_ref):
        i = pl.program_id(0)
        xv = x_ref[...]
        r = lax.rsqrt(jnp.mean(xv * xv, axis=-1, keepdims=True) + EPS)
        xh = xv * r
        gv = g_ref[...]
        e = xh * gv - t_ref[...]
        dout = e * (1.0 / D)
        dxh = dout * gv
        dxv = r * (dxh - xh * jnp.mean(dxh * xh, axis=-1, keepdims=True))
        dx_ref[...] = dxv
        dop_ref[...] = (gp_ref[0] * dxv).astype(BF16)

        @pl.when(i == 0)
        def _():
            loss_ref[...] = jnp.zeros_like(loss_ref)
            dg_ref[...] = jnp.zeros_like(dg_ref)
            dgp_ref[...] = jnp.zeros_like(dgp_ref)

        loss_ref[...] += _colsum(e * e) * (0.5 / D)
        dg_ref[...] += _colsum(dout * xh)
        dgp_ref[0] += _colsum(dxv * op_ref[...].astype(F32))

    row = pl.BlockSpec((tm, D), lambda i: (i, 0))
    vec = pl.BlockSpec((1, D), lambda i: (0, 0))
    vec3 = pl.BlockSpec((1, 1, D), lambda i: (0, 0, 0))
    return pl.pallas_call(
        body, out_shape=[SDS((T, D), F32), SDS((1, D), F32), SDS((1, D), F32), SDS((T, D), BF16), SDS((1, 1, D), F32)],
        grid=(T // tm,), in_specs=[row, vec, row, row, vec3], out_specs=[row, vec, vec, row, vec3],
        compiler_params=_params(("arbitrary",)), name=name)(x, g, target, o_prev, gate_prev)


def _gmlp_core(z, lg, lb, ws_ref, bst):
    W = z.shape[1] // 2
    t = _gelu(z)
    u, v = t[:, :W], t[:, W:]
    mu = jnp.mean(v, axis=-1, keepdims=True)
    vc = v - mu
    rstd = lax.rsqrt(jnp.mean(vc * vc, axis=-1, keepdims=True) + EPS)
    vhat = vc * rstd
    vn = vhat * lg + lb
    vp = []
    for h in range(W // CHUNK):
        blk = vn[:, h * CHUNK:(h + 1) * CHUNK].astype(BF16)
        vp.append(jnp.dot(ws_ref[h].astype(BF16), blk, preferred_element_type=F32) + bst[:, h:h + 1])
    return u, vhat, rstd, vp


def _gmlp_fwd(z, ln_g, ln_b, w_s, b_st, W, name, carry=None):
    M = z.shape[0]
    H = W // CHUNK

    def body(z_ref, lg_ref, lb_ref, ws_ref, bst_ref, o_ref):
        u, _, _, vp = _gmlp_core(z_ref[...], lg_ref[...], lb_ref[...], ws_ref, bst_ref[...])
        for h in range(H):
            o_ref[:, h * CHUNK:(h + 1) * CHUNK] = (u[:, h * CHUNK:(h + 1) * CHUNK] * vp[h]).astype(BF16)

    vec = pl.BlockSpec((1, W), lambda i: (0, 0))
    out, carried = _carry_call(
        body, carry, out_shape=[SDS((M, 2 * W), BF16)], grid=(M // CHUNK,),
        in_specs=[pl.BlockSpec((CHUNK, 2 * W), lambda i: (i, 0)), vec, vec,
                  pl.BlockSpec((H, CHUNK, CHUNK), lambda i: (0, 0, 0)), pl.BlockSpec((CHUNK, H), lambda i: (0, 0))],
        out_specs=[pl.BlockSpec((CHUNK, W), lambda i: (i, 0))], scratch_shapes=[],
        sem=("parallel",), name=name, ins=(z, ln_g, ln_b, w_s, b_st))
    return (out[0], carried) if carry else out[0]


def _gmlp_bwd(z, dy, ln_g, ln_b, w_s, b_st, W, name):
    M = z.shape[0]
    H = W // CHUNK
    ZW = z.shape[1]

    def body(z_ref, dy_ref, lg_ref, lb_ref, ws_ref, bst_ref, dz_ref, dlg_ref, dlb_ref, dws_ref, dbs_ref, dbin_ref):
        i = pl.program_id(0)

        @pl.when(i == 0)
        def _():
            for r in (dlg_ref, dlb_ref, dws_ref, dbs_ref, dbin_ref):
                r[...] = jnp.zeros_like(r)

        zv = z_ref[...]
        lg = lg_ref[...]
        u, vhat, rstd, vp = _gmlp_core(zv, lg, lb_ref[...], ws_ref, bst_ref[...])
        vn = vhat * lg + lb_ref[...]
        dya = dy_ref[...]
        du_parts, dvn_parts = [], []
        for h in range(H):
            sl = slice(h * CHUNK, (h + 1) * CHUNK)
            dya_h = dya[:, sl]
            du_parts.append(dya_h * vp[h])
            dvp = dya_h * u[:, sl]
            dbs_ref[h] += dvp
            dvp16 = dvp.astype(BF16)
            dws_ref[h] += lax.dot_general(dvp16, vn[:, sl].astype(BF16), (((1,), (1,)), ((), ())),
                                          preferred_element_type=F32)
            dvn_parts.append(lax.dot_general(ws_ref[h].astype(BF16), dvp16, (((0,), (0,)), ((), ())),
                                             preferred_element_type=F32))
        du = jnp.concatenate(du_parts, axis=1)
        dvn = jnp.concatenate(dvn_parts, axis=1)
        dlg_ref[...] += _colsum(dvn * vhat)
        dlb_ref[...] += _colsum(dvn)
        dvh = dvn * lg
        dv = rstd * (dvh - jnp.mean(dvh, axis=-1, keepdims=True) - vhat * jnp.mean(dvh * vhat, axis=-1, keepdims=True))
        dz = jnp.concatenate([du, dv], axis=1) * _dgelu(zv)
        dbin_ref[...] += _colsum(dz)
        dz_ref[...] = dz.astype(BF16)

    vec = pl.BlockSpec((1, W), lambda i: (0, 0))
    mat = pl.BlockSpec((H, CHUNK, CHUNK), lambda i: (0, 0, 0))
    return pl.pallas_call(
        body,
        out_shape=[SDS((M, ZW), BF16), SDS((1, W), F32), SDS((1, W), F32), SDS((H, CHUNK, CHUNK), F32),
                   SDS((H, CHUNK, CHUNK), F32), SDS((1, 2 * W), F32)],
        grid=(M // CHUNK,),
        in_specs=[pl.BlockSpec((CHUNK, 2 * W), lambda i: (i, 0)), pl.BlockSpec((CHUNK, W), lambda i: (i, 0)), vec, vec,
                  mat, pl.BlockSpec((CHUNK, H), lambda i: (0, 0))],
        out_specs=[pl.BlockSpec((CHUNK, 2 * W), lambda i: (i, 0)), vec, vec, mat, mat,
                   pl.BlockSpec((1, 2 * W), lambda i: (0, 0))],
        compiler_params=_params(("arbitrary",)), name=name)(z, dy, ln_g, ln_b, w_s, b_st)


def _halo_specs(tm, width, col, n_rows):
    per = tm // HALO
    last = n_rows // HALO - 1
    prev = pl.BlockSpec((HALO, width), lambda i: (jnp.maximum(i * per - 1, 0), col))
    nxt = pl.BlockSpec((HALO, width), lambda i: (jnp.minimum((i + 1) * per, last), col))
    return prev, nxt


def _edge_flags(i, tm, seg_t, m):
    r0 = i * tm
    has_prev = jnp.where((r0 == 0) | (r0 == seg_t), 0.0, 1.0)
    has_next = jnp.where((r0 + tm == seg_t) | (r0 + tm == m), 0.0, 1.0)
    return has_prev, has_next


def _glu(zz, wb):
    return zz[:, :wb] * _sigmoid(zz[:, wb:])


def _build_shifts(src_ref, sh_ref):
    n = src_ref.shape[0] - 8
    for r in range(1, 8):
        sh_ref[r - 1, pl.ds(0, n), :] = src_ref[pl.ds(r, n), :]


def _shifted(src_ref, sh_ref, off, r0, rc):
    a, r = divmod(off, 8)
    if r == 0:
        return src_ref[pl.ds(8 * a + r0, rc), :]
    return sh_ref[r - 1, pl.ds(8 * a + r0, rc), :]


def _conv_taps(src_ref, sh_ref, w_ref, first, tm, kw, flip=False):
    rc = 32
    parts = []
    for c in range(tm // rc):
        acc = None
        for k in range(kw):
            wk = w_ref[pl.ds(kw - 1 - k if flip else k, 1), :]
            term = _shifted(src_ref, sh_ref, first + k, c * rc, rc) * wk
            acc = term if acc is None else acc + term
        parts.append(acc)
    return jnp.concatenate(parts, axis=0)


def _conf_fwd(z, y, conv_w, conv_b, ln_g, ln_b, W, Wb, seg_t, name, carry=None):
    M = z.shape[0]
    tm = _row_tile(seg_t, M)
    kw = conv_w.shape[0]
    pad = (kw - 1) // 2
    col = (2 * W) // (2 * Wb)

    def body(zc_ref, zp_ref, zn_ref, y_hbm, cw_ref, cb_ref, lg_ref, lb_ref, o_ref, hc_ref, hs_ref, sh_ref):
        del y_hbm
        hp, hn = _edge_flags(pl.program_id(0), tm, seg_t, M)
        hs_ref[pl.ds(0, HALO), :] = _glu(zp_ref[...], Wb) * hp
        hs_ref[pl.ds(HALO, tm), :] = _glu(zc_ref[...], Wb)
        hs_ref[pl.ds(HALO + tm, HALO), :] = _glu(zn_ref[...], Wb) * hn
        _build_shifts(hs_ref, sh_ref)
        hc = _conv_taps(hs_ref, sh_ref, cw_ref, HALO - pad, tm, kw) + cb_ref[...]
        hc_ref[...] = hc
        mu = jnp.mean(hc, axis=-1, keepdims=True)
        c = hc - mu
        rstd = lax.rsqrt(jnp.mean(c * c, axis=-1, keepdims=True) + EPS)
        o_ref[...] = _silu(c * rstd * lg_ref[...] + lb_ref[...]).astype(BF16)

    prev, nxt = _halo_specs(tm, 2 * Wb, col, M)
    vec = pl.BlockSpec((1, Wb), lambda i: (0, 0))
    out, carried = _carry_call(
        body, carry, out_shape=[SDS(y.shape, BF16), SDS((M, Wb), F32)], grid=(M // tm,),
        in_specs=[pl.BlockSpec((tm, 2 * Wb), lambda i: (i, col)), prev, nxt, pl.BlockSpec(memory_space=pl.ANY),
                  pl.BlockSpec((kw, Wb), lambda i: (0, 0)), vec, vec, vec],
        out_specs=[pl.BlockSpec((tm, Wb), lambda i: (i, W // Wb)), pl.BlockSpec((tm, Wb), lambda i: (i, 0))],
        scratch_shapes=[pltpu.VMEM((tm + 2 * HALO, Wb), F32), pltpu.VMEM((7, tm + 2 * HALO, Wb), F32)],
        aliases={3: 0}, sem=("parallel",), name=name, ins=(z, z, z, y, conv_w, conv_b, ln_g, ln_b))
    return (out, carried) if carry else out


def _conf_bwd1(hc, dy, ln_g, ln_b, W, Wb, seg_t, name):
    M = hc.shape[0]
    tm = _row_tile(seg_t, M)

    def body(hc_ref, dy_ref, lg_ref, lb_ref, dhc_ref, dlg_ref, dlb_ref, dcb_ref):
        i = pl.program_id(0)

        @pl.when(i == 0)
        def _():
            for r in (dlg_ref, dlb_ref, dcb_ref):
                r[...] = jnp.zeros_like(r)

        hc = hc_ref[...]
        mu = jnp.mean(hc, axis=-1, keepdims=True)
        c = hc - mu
        rstd = lax.rsqrt(jnp.mean(c * c, axis=-1, keepdims=True) + EPS)
        hh = c * rstd
        lg = lg_ref[...]
        dhn = dy_ref[...] * _dsilu(hh * lg + lb_ref[...])
        dlg_ref[...] += _colsum(dhn * hh)
        dlb_ref[...] += _colsum(dhn)
        dhh = dhn * lg
        dhc = rstd * (dhh - jnp.mean(dhh, axis=-1, keepdims=True) - hh * jnp.mean(dhh * hh, axis=-1, keepdims=True))
        dcb_ref[...] += _colsum(dhc)
        dhc_ref[...] = dhc

    vec = pl.BlockSpec((1, Wb), lambda i: (0, 0))
    return pl.pallas_call(
        body, out_shape=[SDS((M, Wb), F32), SDS((1, Wb), F32), SDS((1, Wb), F32), SDS((1, Wb), F32)], grid=(M // tm,),
        in_specs=[pl.BlockSpec((tm, Wb), lambda i: (i, 0)), pl.BlockSpec((tm, Wb), lambda i: (i, W // Wb)), vec, vec],
        out_specs=[pl.BlockSpec((tm, Wb), lambda i: (i, 0)), vec, vec, vec],
        compiler_params=_params(("arbitrary",)), name=name)(hc, dy, ln_g, ln_b)


def _conf_bwd2(z, dhc, dz, conv_w, W, Wb, seg_t, name, carry=None):
    M = z.shape[0]
    tm = _row_tile(seg_t, M)
    kw = conv_w.shape[0]
    pad = (kw - 1) // 2
    col = (2 * W) // (2 * Wb)

    def body(zc_ref, zp_ref, zn_ref, dc_ref, dp_ref, dn_ref, dz_hbm, cw_ref, dz_ref, dcw_ref, dbin_ref, hs_ref, ds_ref,
             hsh_ref, dsh_ref):
        del dz_hbm
        i = pl.program_id(0)

        @pl.when(i == 0)
        def _():
            dcw_ref[...] = jnp.zeros_like(dcw_ref)
            dbin_ref[...] = jnp.zeros_like(dbin_ref)

        hp, hn = _edge_flags(i, tm, seg_t, M)
        zc = zc_ref[...]
        hs_ref[pl.ds(0, HALO), :] = _glu(zp_ref[...], Wb) * hp
        hs_ref[pl.ds(HALO, tm), :] = _glu(zc, Wb)
        hs_ref[pl.ds(HALO + tm, HALO), :] = _glu(zn_ref[...], Wb) * hn
        dcur = dc_ref[...]
        ds_ref[pl.ds(0, HALO), :] = dp_ref[...] * hp
        ds_ref[pl.ds(HALO, tm), :] = dcur
        ds_ref[pl.ds(HALO + tm, HALO), :] = dn_ref[...] * hn
        _build_shifts(ds_ref, dsh_ref)
        _build_shifts(hs_ref, hsh_ref)
        dh = _conv_taps(ds_ref, dsh_ref, cw_ref, HALO - pad, tm, kw, flip=True)
        for k in range(kw):
            dcw_ref[pl.ds(k, 1), :] += _colsum(dcur * _shifted(hs_ref, hsh_ref, HALO - pad + k, 0, tm))
        a, gt = zc[:, :Wb], zc[:, Wb:]
        s = _sigmoid(gt)
        dz = jnp.concatenate([dh * s, dh * a * s * (1.0 - s)], axis=1)
        dbin_ref[...] += _colsum(dz)
        dz_ref[...] = dz.astype(BF16)

    prev, nxt = _halo_specs(tm, 2 * Wb, col, M)
    dprev, dnxt = _halo_specs(tm, Wb, 0, M)
    out, carried = _carry_call(
        body, carry, out_shape=[SDS(dz.shape, BF16), SDS((kw, Wb), F32), SDS((1, 2 * Wb), F32)], grid=(M // tm,),
        in_specs=[pl.BlockSpec((tm, 2 * Wb), lambda i: (i, col)), prev, nxt,
                  pl.BlockSpec((tm, Wb), lambda i: (i, 0)), dprev, dnxt, pl.BlockSpec(memory_space=pl.ANY),
                  pl.BlockSpec((kw, Wb), lambda i: (0, 0))],
        out_specs=[pl.BlockSpec((tm, 2 * Wb), lambda i: (i, col)), pl.BlockSpec((kw, Wb), lambda i: (0, 0)),
                   pl.BlockSpec((1, 2 * Wb), lambda i: (0, 0))],
        scratch_shapes=[pltpu.VMEM((tm + 2 * HALO, Wb), F32), pltpu.VMEM((tm + 2 * HALO, Wb), F32),
                        pltpu.VMEM((7, tm + 2 * HALO, Wb), F32), pltpu.VMEM((7, tm + 2 * HALO, Wb), F32)],
        aliases={6: 0}, sem=("arbitrary",), name=name, ins=(z, z, z, dhc, dhc, dhc, dz, conv_w))
    return (out, carried) if carry else out


_TF = (1408, 512, 256, 128)
_RC = 16
_CG = 256


def _col_groups(width):
    return [(c0, min(_CG, width - c0)) for c0 in range(0, width, _CG)]


def _ffn_act_fwd(z, conv_w, conv_b, seg_t, name):
    M, F2 = z.shape
    Fd = F2 // 2
    tm = _row_tile(seg_t, M)
    tf = _tile(Fd, _TF)
    nf = Fd // tf
    per, last = tm // HALO, M // HALO - 1

    def body(g_ref, gp_ref, gn_ref, u_ref, cw_ref, cb_ref, o_ref, gc_ref, gs_ref):
        hp, hn = _edge_flags(pl.program_id(0), tm, seg_t, M)
        gs_ref[pl.ds(0, HALO), :] = gp_ref[...].astype(F32) * hp
        gs_ref[pl.ds(HALO, tm), :] = g_ref[...].astype(F32)
        gs_ref[pl.ds(HALO + tm, HALO), :] = gn_ref[...].astype(F32) * hn
        for c0, cw in _col_groups(tf):
            cs = pl.ds(c0, cw)
            w0, w1, w2, cb = cw_ref[pl.ds(0, 1), cs], cw_ref[pl.ds(1, 1), cs], cw_ref[pl.ds(2, 1), cs], cb_ref[:, cs]
            for r0 in range(0, tm, _RC):
                gc = (gs_ref[pl.ds(HALO - 1 + r0, _RC), cs] * w0 + gs_ref[pl.ds(HALO + r0, _RC), cs] * w1
                      + gs_ref[pl.ds(HALO + 1 + r0, _RC), cs] * w2 + cb)
                o_ref[pl.ds(r0, _RC), cs] = (_silu(gc) * u_ref[pl.ds(r0, _RC), cs].astype(F32)).astype(BF16)
                gc_ref[pl.ds(r0, _RC), cs] = gc.astype(BF16)

    return pl.pallas_call(
        body, out_shape=[SDS((M, Fd), BF16), SDS((M, Fd), BF16)], grid=(M // tm, nf),
        in_specs=[pl.BlockSpec((tm, tf), lambda i, j: (i, j)),
                  pl.BlockSpec((HALO, tf), lambda i, j: (jnp.maximum(i * per - 1, 0), j)),
                  pl.BlockSpec((HALO, tf), lambda i, j: (jnp.minimum((i + 1) * per, last), j)),
                  pl.BlockSpec((tm, tf), lambda i, j: (i, nf + j)),
                  pl.BlockSpec((3, tf), lambda i, j: (0, j)), pl.BlockSpec((1, tf), lambda i, j: (0, j))],
        out_specs=[pl.BlockSpec((tm, tf), lambda i, j: (i, j)), pl.BlockSpec((tm, tf), lambda i, j: (i, j))],
        scratch_shapes=[pltpu.VMEM((tm + 2 * HALO, tf), F32)],
        compiler_params=_params(("parallel", "parallel")), name=name)(z, z, z, z, conv_w, conv_b)


def _ffn_act_bwd(z, gc, da, conv_w, seg_t, name):
    M, F2 = z.shape
    Fd = F2 // 2
    tm = _row_tile(seg_t, M)
    tf = _tile(Fd, _TF)
    nf = Fd // tf
    per, last = tm // HALO, M // HALO - 1
    n_piece = tm // _RC

    def body(g_ref, c_ref, cp_ref, cn_ref, u_ref, up_ref, un_ref, a_ref, ap_ref, an_ref, cw_ref,
             dz_ref, dcw_ref, dcb_ref, ds_ref):
        i = pl.program_id(1)

        def tile():
            hp, hn = _edge_flags(i, tm, seg_t, M)

            @pl.when(i == 0)
            def _():
                dcw_ref[...] = jnp.zeros_like(dcw_ref)
                dcb_ref[...] = jnp.zeros_like(dcb_ref)

            def fold(v):
                return v[:8] + v[8:]

            for c0, cw in _col_groups(tf):
                cs = pl.ds(c0, cw)
                w0, w1, w2 = cw_ref[pl.ds(0, 1), cs], cw_ref[pl.ds(1, 1), cs], cw_ref[pl.ds(2, 1), cs]
                for ci in range(-1, n_piece + 1):
                    r0 = ci * _RC
                    if ci < 0:
                        gcv, ue, ae = cp_ref[:, cs], up_ref[:, cs], ap_ref[:, cs].astype(F32) * hp
                    elif ci == n_piece:
                        gcv, ue, ae = cn_ref[:, cs], un_ref[:, cs], an_ref[:, cs].astype(F32) * hn
                    else:
                        rows = pl.ds(r0, _RC)
                        gcv, ue, ae = c_ref[rows, cs], u_ref[rows, cs], a_ref[rows, cs].astype(F32)
                    gcv, ue = gcv.astype(F32), ue.astype(F32)
                    sg = _sigmoid(gcv)
                    t = ae * sg
                    ds_ref[pl.ds(HALO + r0, _RC), cs] = t * ue * (1.0 + gcv * (1.0 - sg))
                    if 0 <= ci < n_piece:
                        dz_ref[1, pl.ds(r0, _RC), cs] = (t * gcv).astype(BF16)
                acc = [jnp.zeros((8, cw), F32) for _ in range(4)]
                for r0 in range(0, tm, _RC):
                    b = HALO + r0
                    d = [ds_ref[pl.ds(b + 1 - k, _RC), cs] for k in range(3)]
                    dz_ref[0, pl.ds(r0, _RC), cs] = (d[0] * w0 + d[1] * w1 + d[2] * w2).astype(BF16)
                    gv = g_ref[pl.ds(r0, _RC), cs].astype(F32)
                    for k in range(3):
                        acc[k] = acc[k] + fold(d[k] * gv)
                    acc[3] = acc[3] + fold(d[1])
                for k in range(3):
                    dcw_ref[pl.ds(k, 1), cs] += _colsum(acc[k])
                dcb_ref[:, cs] += _colsum(acc[3])

        tile()

    def cur(off):
        return pl.BlockSpec((tm, tf), lambda j, i: (i, off + j))

    def prv(off):
        return pl.BlockSpec((HALO, tf), lambda j, i: (jnp.maximum(i * per - 1, 0), off + j))

    def nxt(off):
        return pl.BlockSpec((HALO, tf), lambda j, i: (jnp.minimum((i + 1) * per, last), off + j))

    return pl.pallas_call(
        body, out_shape=[SDS((2, M, Fd), BF16), SDS((3, Fd), F32), SDS((1, Fd), F32)], grid=(nf, M // tm),
        in_specs=[cur(0), cur(0), prv(0), nxt(0), cur(nf), prv(nf), nxt(nf), cur(0), prv(0), nxt(0),
                  pl.BlockSpec((3, tf), lambda j, i: (0, j))],
        out_specs=[pl.BlockSpec((2, tm, tf), lambda j, i: (0, i, j)),
                   pl.BlockSpec((3, tf), lambda j, i: (0, j)), pl.BlockSpec((1, tf), lambda j, i: (0, j))],
        scratch_shapes=[pltpu.VMEM((tm + 2 * HALO, tf), F32)],
        compiler_params=_params(("parallel", "arbitrary")), name=name)(
            z, gc, gc, gc, z, z, z, da, da, da, conv_w)


_LN2 = math.log(2.0)
_QSCALE = (NOPE + ROPE) ** -0.5 / _LN2


def _swap32(x):
    lane = lax.broadcasted_iota(jnp.int32, x.shape, 1)
    return jnp.where((lane % 64) < 32, pltpu.roll(x, 96, axis=1), pltpu.roll(x, 32, axis=1))


def _rms(x, g):
    r = lax.rsqrt(jnp.mean(x * x, axis=-1, keepdims=True) + EPS)
    return x * r * g


def _rms_bwd(x, g, dy):
    r = lax.rsqrt(jnp.mean(x * x, axis=-1, keepdims=True) + EPS)
    xh = x * r
    dxh = dy * g
    return r * (dxh - xh * jnp.mean(dxh * xh, axis=-1, keepdims=True)), _colsum(dy * xh)


def _mla_prep_fwd(z, gq, gkv, cos, sin, QL, KL, name):
    M, NZ = z.shape
    tm = _tile(M, (256, 128))

    def body(z_ref, gq_ref, gkv_ref, cos_ref, sin_ref, cq_ref, ckv_ref, kpe_ref):
        zv = z_ref[...]
        cq_ref[...] = _rms(zv[:, :QL], gq_ref[...]).astype(BF16)
        ckv_ref[...] = _rms(zv[:, QL:QL + KL], gkv_ref[...]).astype(BF16)
        kp = zv[:, QL + KL:]
        r = kp * cos_ref[...] + _swap32(kp) * sin_ref[...]
        lane = lax.broadcasted_iota(jnp.int32, r.shape, 1)
        kpe_ref[0] = jnp.where(lane < ROPE, r, 0.0).astype(BF16)
        kpe_ref[1] = jnp.where(lane >= ROPE, r, 0.0).astype(BF16)

    tab = pl.BlockSpec((tm, 128), lambda i: (i, 0))
    return pl.pallas_call(
        body, out_shape=[SDS((M, QL), BF16), SDS((M, KL), BF16), SDS((2, M, 128), BF16)], grid=(M // tm,),
        in_specs=[pl.BlockSpec((tm, NZ), lambda i: (i, 0)), pl.BlockSpec((1, QL), lambda i: (0, 0)),
                  pl.BlockSpec((1, KL), lambda i: (0, 0)), tab, tab],
        out_specs=[pl.BlockSpec((tm, QL), lambda i: (i, 0)), pl.BlockSpec((tm, KL), lambda i: (i, 0)),
                   pl.BlockSpec((2, tm, 128), lambda i: (0, i, 0))],
        compiler_params=_params(("parallel",)), name=name)(z, gq, gkv, cos, sin)


def _mla_prep_bwd(z, dcq, dckv, dkpe, gq, gkv, cos, sin, QL, KL, seg_t, name):
    M, NZ = z.shape
    H = dkpe.shape[0]
    tm = _row_tile(seg_t, M)
    nt = seg_t // tm

    def body(z_ref, dcq_ref, dckv_ref, dkpe_ref, gq_ref, gkv_ref, cos_ref, sin_ref, dz_ref, dgq_ref, dgkv_ref):
        i = pl.program_id(0)

        @pl.when(i == 0)
        def _():
            dgq_ref[...] = jnp.zeros_like(dgq_ref)
            dgkv_ref[...] = jnp.zeros_like(dgkv_ref)

        zv = z_ref[...]
        dyq = jnp.where(i < nt, dcq_ref[...], 0.0)
        dxq, dgq = _rms_bwd(zv[:, :QL], gq_ref[...], dyq)
        dxkv, dgkv = _rms_bwd(zv[:, QL:QL + KL], gkv_ref[...], dckv_ref[...])
        dgq_ref[...] += dgq
        dgkv_ref[...] += dgkv
        even = dkpe_ref[0]
        odd = dkpe_ref[1]
        for h in range(2, H, 2):
            even = even + dkpe_ref[h]
            odd = odd + dkpe_ref[h + 1]
        lane = lax.broadcasted_iota(jnp.int32, even.shape, 1)
        dr = jnp.where(lane < ROPE, even, odd)
        dkp = dr * cos_ref[...] - _swap32(dr) * sin_ref[...]
        dz_ref[...] = jnp.concatenate([dxq, dxkv, dkp], axis=1).astype(BF16)

    tab = pl.BlockSpec((tm, 128), lambda i: (i, 0))
    return pl.pallas_call(
        body, out_shape=[SDS((M, NZ), BF16), SDS((1, QL), F32), SDS((1, KL), F32)], grid=(M // tm,),
        in_specs=[pl.BlockSpec((tm, NZ), lambda i: (i, 0)),
                  pl.BlockSpec((tm, QL), lambda i: (jnp.minimum(i, nt - 1), 0)),
                  pl.BlockSpec((tm, KL), lambda i: (i, 0)), pl.BlockSpec((H, tm, 128), lambda i: (0, i, 0)),
                  pl.BlockSpec((1, QL), lambda i: (0, 0)), pl.BlockSpec((1, KL), lambda i: (0, 0)), tab, tab],
        out_specs=[pl.BlockSpec((tm, NZ), lambda i: (i, 0)), pl.BlockSpec((1, QL), lambda i: (0, 0)),
                   pl.BlockSpec((1, KL), lambda i: (0, 0))],
        compiler_params=_params(("arbitrary",)), name=name)(z, dcq, dckv, dkpe, gq, gkv, cos, sin)


def _qrope_fwd(q, cos, sin, HN, name):
    T, NQ = q.shape
    tm = _tile(T, (256, 128))

    def body(q_ref, cos_ref, sin_ref, o_ref):
        o_ref[:, :HN] = (q_ref[:, :HN] * _QSCALE).astype(BF16)
        for cb in range((NQ - HN) // 128):
            sl = slice(HN + cb * 128, HN + (cb + 1) * 128)
            xv = q_ref[:, sl]
            o_ref[:, sl] = ((xv * cos_ref[...] + _swap32(xv) * sin_ref[...]) * _QSCALE).astype(BF16)

    tab = pl.BlockSpec((tm, 128), lambda i: (i, 0))
    return pl.pallas_call(
        body, out_shape=SDS((T, NQ), BF16), grid=(T // tm,),
        in_specs=[pl.BlockSpec((tm, NQ), lambda i: (i, 0)), tab, tab],
        out_specs=pl.BlockSpec((tm, NQ), lambda i: (i, 0)),
        compiler_params=_params(("parallel",)), name=name)(q, cos, sin)


def _qrope_bwd(dqpe, dqa, cos, sin, HN, name):
    T, HW = dqpe.shape
    HR = HW // 2
    tm = _tile(T, (256, 128))

    def body(d_ref, dqa_hbm, cos_ref, sin_ref, o_ref):
        del dqa_hbm
        for pr in range(HR // 128):
            dr = d_ref[:, 2 * pr * 128:(2 * pr + 1) * 128] + d_ref[:, (2 * pr + 1) * 128:(2 * pr + 2) * 128]
            o_ref[:, pr * 128:(pr + 1) * 128] = (dr * cos_ref[...] - _swap32(dr) * sin_ref[...]).astype(BF16)

    tab = pl.BlockSpec((tm, 128), lambda i: (i, 0))
    return pl.pallas_call(
        body, out_shape=SDS(dqa.shape, BF16), grid=(T // tm,),
        in_specs=[pl.BlockSpec((tm, HW), lambda i: (i, 0)), pl.BlockSpec(memory_space=pl.ANY), tab, tab],
        out_specs=pl.BlockSpec((tm, HR), lambda i: (i, HN // HR)),
        input_output_aliases={1: 0}, compiler_params=_params(("parallel",)), name=name)(dqpe, dqa, cos, sin)


_ATT_SUB = 8
_ATT_SUB_B = 4
_NT = (((1,), (1,)), ((), ()))
_TN = (((0,), (0,)), ((), ()))


def _attn_fwd(qa, kv, kpe, T, H, name, carry=None):
    M = kv.shape[0]
    tq = _tile(T, (2048, 1024, 512, 256, 128))

    def body(qn_ref, qp_ref, kv_ref, kpe_ref, o_ref, lse_ref, kc_ref):
        @pl.when(pl.program_id(1) == 0)
        def _():
            kc_ref[:, :NOPE] = kv_ref[:, :NOPE]
            kc_ref[:, NOPE:] = kpe_ref[0]

        rs = tq // _ATT_SUB
        outs, lses = [], []
        for u in range(_ATT_SUB):
            rows = pl.ds(u * rs, rs)
            qc = jnp.concatenate([qn_ref[rows, :], qp_ref[rows, :]], axis=1)
            s = lax.dot_general(qc, kc_ref[...], _NT, preferred_element_type=F32)
            m = jnp.max(s, axis=-1, keepdims=True)
            p = jnp.exp2(s - m)
            l = jnp.sum(p, axis=-1, keepdims=True)
            o = jnp.dot(p.astype(BF16), kv_ref[:, NOPE:], preferred_element_type=F32)
            outs.append((o / l).astype(BF16))
            lses.append(jnp.broadcast_to(m + jnp.log2(l), (rs, 128)))
        o_ref[...] = jnp.concatenate(outs, axis=0)
        lse_ref[...] = jnp.concatenate(lses, axis=0)

    return _carry_call(
        body, carry, out_shape=[SDS((T, H * VDIM), BF16), SDS((T, H * 128), F32)], grid=(H, T // tq),
        in_specs=[pl.BlockSpec((tq, NOPE), lambda h, i: (i, h)), pl.BlockSpec((tq, 128), lambda h, i: (i, H + h // 2)),
                  pl.BlockSpec((M, NOPE + VDIM), lambda h, i: (0, h)), pl.BlockSpec((1, M, 128), lambda h, i: (h % 2, 0, 0))],
        out_specs=[pl.BlockSpec((tq, VDIM), lambda h, i: (i, h)), pl.BlockSpec((tq, 128), lambda h, i: (i, h))],
        scratch_shapes=[pltpu.VMEM((M, NOPE + 128), BF16)],
        sem=("parallel", "arbitrary"), name=name, ins=(qa, qa, kv, kpe))


def _attn_bwd(qa, kv, kpe, do, o, lse, T, H, name, carry=None):
    M = kv.shape[0]
    tq = _tile(T, (1024, 512, 256, 128))
    nq = T // tq
    scale = (NOPE + ROPE) ** -0.5

    def body(qn_ref, qp_ref, kv_ref, kpe_ref, do_ref, o_ref, lse_ref, dqa_ref, dqpe_ref, dkv_ref, dkpe_ref, kc_ref,
             dk_acc, dv_acc):
        i = pl.program_id(1)

        @pl.when(i == 0)
        def _():
            kc_ref[:, :NOPE] = kv_ref[:, :NOPE]
            kc_ref[:, NOPE:] = kpe_ref[0]
            dk_acc[...] = jnp.zeros_like(dk_acc)
            dv_acc[...] = jnp.zeros_like(dv_acc)

        rs = tq // _ATT_SUB_B
        p16s, ds16s = [], []
        for u in range(_ATT_SUB_B):
            rows = pl.ds(u * rs, rs)
            qc = jnp.concatenate([qn_ref[rows, :], qp_ref[rows, :]], axis=1)
            dov = do_ref[rows, :]
            s = lax.dot_general(qc, kc_ref[...], _NT, preferred_element_type=F32)
            p = jnp.exp2(s - lse_ref[rows, 0:1])
            dp = lax.dot_general(dov, kv_ref[:, NOPE:], _NT, preferred_element_type=F32)
            delta = jnp.sum(dov.astype(F32) * o_ref[rows, :].astype(F32), axis=-1, keepdims=True)
            ds16s.append((p * (dp - delta)).astype(BF16))
            p16s.append(p.astype(BF16))
        p16 = jnp.concatenate(p16s, axis=0)
        ds16 = jnp.concatenate(ds16s, axis=0)
        qc = jnp.concatenate([qn_ref[...], qp_ref[...]], axis=1)
        dq = jnp.dot(ds16, kc_ref[...], preferred_element_type=F32) * scale
        dqa_ref[...] = dq[:, :NOPE].astype(BF16)
        dqpe_ref[...] = dq[:, NOPE:]
        dv_acc[...] += lax.dot_general(p16, do_ref[...], _TN, preferred_element_type=F32)
        dk_acc[...] += lax.dot_general(ds16, qc, _TN, preferred_element_type=F32)

        @pl.when(i == nq - 1)
        def _():
            dkv_ref[:, :NOPE] = (dk_acc[:, :NOPE] * _LN2).astype(BF16)
            dkv_ref[:, NOPE:] = dv_acc[...].astype(BF16)
            dkpe_ref[0] = dk_acc[:, NOPE:] * _LN2

    return _carry_call(
        body, carry,
        out_shape=[SDS((T, H * (NOPE + ROPE)), BF16), SDS((T, H * 128), F32), SDS((M, H * (NOPE + VDIM)), BF16),
                   SDS((H, M, 128), F32)],
        grid=(H, nq),
        in_specs=[pl.BlockSpec((tq, NOPE), lambda h, i: (i, h)), pl.BlockSpec((tq, 128), lambda h, i: (i, H + h // 2)),
                  pl.BlockSpec((M, NOPE + VDIM), lambda h, i: (0, h)), pl.BlockSpec((1, M, 128), lambda h, i: (h % 2, 0, 0)),
                  pl.BlockSpec((tq, VDIM), lambda h, i: (i, h)), pl.BlockSpec((tq, VDIM), lambda h, i: (i, h)),
                  pl.BlockSpec((tq, 128), lambda h, i: (i, h))],
        out_specs=[pl.BlockSpec((tq, NOPE), lambda h, i: (i, h)), pl.BlockSpec((tq, 128), lambda h, i: (i, h)),
                   pl.BlockSpec((M, NOPE + VDIM), lambda h, i: (0, h)), pl.BlockSpec((1, M, 128), lambda h, i: (h, 0, 0))],
        scratch_shapes=[pltpu.VMEM((M, NOPE + 128), BF16), pltpu.VMEM((M, NOPE + 128), F32), pltpu.VMEM((M, VDIM), F32)],
        sem=("parallel", "arbitrary"), name=name, ins=(qa, qa, kv, kpe, do, o, lse))


def _adamw(w, m, v, name, g=None, recv=None, carry=None):
    R, C = w.shape
    summed = recv is not None
    n_recv = len(recv) if summed else 1
    runs = [r.shape[1] for r in recv] if summed else [R]
    tr = math.gcd(*runs)
    for cand in (1024, 512, 256, 128, 64, 32, 16, 8):
        if tr % cand == 0 and cand * C <= 131072:
            tr = cand
            break
    first = [sum(runs[:r]) // tr for r in range(n_recv + 1)]
    c1 = 1.0 - ADAM_B1 ** ADAM_STEP
    c2 = 1.0 - ADAM_B2 ** ADAM_STEP

    def update(gv, w_ref, m_ref, v_ref, d_ref, nm_ref, nv_ref):
        mn = ADAM_B1 * m_ref[...] + (1.0 - ADAM_B1) * gv
        vn = ADAM_B2 * v_ref[...] + (1.0 - ADAM_B2) * (gv * gv)
        nm_ref[...] = mn
        nv_ref[...] = vn
        d_ref[...] = -ADAM_LR * ((mn / c1) / (jnp.sqrt(vn / c2) + ADAM_EPS) + ADAM_WD * w_ref[...])

    def body(*refs):
        w_ref, m_ref, v_ref = refs[:3]
        g_refs = refs[3:3 + n_recv]
        outs = refs[3 + n_recv:]
        if not summed:
            update(g_refs[0][...], w_ref, m_ref, v_ref, *outs)
            return
        i = pl.program_id(0)
        for r in range(n_recv):
            @pl.when((i >= first[r]) & (i < first[r + 1]))
            def _():
                gv = g_refs[r][0].astype(F32)
                for d in range(1, N_DEV):
                    gv = gv + g_refs[r][d].astype(F32)
                outs[0][...] = gv
                update(gv, w_ref, m_ref, v_ref, *outs[1:])

    blk = pl.BlockSpec((tr, C), lambda i: (i, 0))
    if summed:
        g_specs = [pl.BlockSpec((N_DEV, tr, C), functools.partial(
            lambda i, lo, n: (0, jnp.clip(i - lo, 0, n - 1), 0), lo=first[r], n=first[r + 1] - first[r]))
                   for r in range(n_recv)]
    else:
        g_specs = [blk]
    n_out = 4 if summed else 3
    out, carried = _carry_call(
        body, carry, out_shape=[SDS((R, C), F32)] * n_out, grid=(R // tr,), in_specs=[blk, blk, blk] + g_specs,
        out_specs=[blk] * n_out, scratch_shapes=[], sem=("parallel",), name=name,
        ins=(w, m, v, *(recv if summed else [g])))
    return (out, carried) if carry else out


WEIGHTS = ['c_ctx', 'norm1_g', 'norm2_g', 'w_ada', 'b_ada', 'ab_w_in', 'ab_b_in', 'a_ln_g', 'a_ln_b', 'a_w_s', 'a_b_s',
           'b_conv_w', 'b_conv_b', 'b_ln_g', 'b_ln_b', 'ab_w_out', 'mla_w_in', 'mla_q_norm_g', 'mla_w_uq',
           'mla_kv_norm_g', 'mla_w_ukv', 'mla_w_o', 'ffn_w_up', 'ffn_conv_w', 'ffn_conv_b', 'ffn_w_down', 'final_norm_g']


def _pack(parts):
    flat = jnp.concatenate([p.reshape(-1).astype(F32) for p in parts])
    n = flat.shape[0]
    unit = 65536 if n > 65536 else 1024
    n_pad = -(-n // unit) * unit
    return jnp.pad(flat, (0, n_pad - n)).reshape(n_pad // 128, 128)


def _unpack(flat, like):
    out, off = [], 0
    for shp in like:
        n = math.prod(shp)
        out.append(flat[..., off:off + n].reshape(flat.shape[:-1] + tuple(shp)))
        off += n
    return out


def _rope_tables(T, Tc):
    rows = T // GRID_W
    row = jnp.repeat(jnp.arange(rows, dtype=F32), GRID_W)
    col = jnp.tile(jnp.arange(GRID_W, dtype=F32), rows)
    n_freq = ROPE // 4
    inv = ROPE_THETA ** (-jnp.arange(n_freq, dtype=F32) / n_freq)
    ang = jnp.concatenate([row[:, None] * inv, col[:, None] * inv], axis=-1)
    cos, sin = jnp.cos(ang), jnp.sin(ang)
    cos = jnp.tile(cos, (1, 128 // (ROPE // 2)))
    sin = jnp.tile(jnp.concatenate([-sin, sin], axis=1), (1, 128 // ROPE))
    return (jnp.concatenate([cos, jnp.ones((Tc, 128), F32)], axis=0),
            jnp.concatenate([sin, jnp.zeros((Tc, 128), F32)], axis=0))


def _step(a):
    ax, ay, ac = lax.axis_index("x"), lax.axis_index("y"), lax.axis_index("c")
    me = 4 * ax + 2 * ay + ac
    T, D = a['x'].shape[1:]
    Tc = a['ctx'].shape[1]
    M = T + Tc
    W, Wb = a['a_ln_g'].shape[1], a['b_ln_g'].shape[1]
    assert W == Wb and T % Tc == 0
    Fd = a['ffn_conv_b'].shape[1]
    QL, KL = a['mla_q_norm_g'].shape[1] * N_DEV, a['mla_kv_norm_g'].shape[1] * N_DEV
    H = a['mla_w_ukv'].shape[2] * N_DEV // (NOPE + VDIM)
    HN, HR = H * NOPE, H * ROPE
    kw = a['b_conv_w'].shape[1]
    NA = a['w_ada'].shape[2]
    bf = lambda t: t.astype(BF16)

    small_shapes = [(D,), (kw, Wb // N_DEV), (2, 3, Fd // N_DEV), (QL // N_DEV,), (KL // N_DEV,)]
    g_small = _all_gather(_pack([a['c'][0], a['b_conv_w'][0], a['ffn_conv_w'], a['mla_q_norm_g'][0], a['mla_kv_norm_g'][0]]),
                          "ag_small")
    c_all, bcw, fcw, gq, gkv = _unpack(g_small.reshape(N_DEV, -1), small_shapes)
    bcw = jnp.transpose(bcw, (1, 0, 2)).reshape(kw, Wb)
    fcw = jnp.transpose(fcw, (1, 2, 0, 3)).reshape(2, 3, Fd)
    gq, gkv = gq.reshape(1, QL), gkv.reshape(1, KL)

    a16 = jnp.concatenate([c_all, a['c_ctx'][None], jnp.zeros((N_DEV - 1, D), F32)], axis=0)
    b_loc = lax.dynamic_slice(a['b_ada'], (0, me * NA), (2, NA))
    mods = [_mm(a16, a['w_ada'][l], mode="nn", out_dtype=F32, name=f"ada_fwd{l}", bias=b_loc[l:l + 1], a_silu=True)
            for l in range(2)]
    gm = _all_gather(jnp.concatenate(mods, axis=0), "ag_mod").reshape(N_DEV, 2, 2 * N_DEV, NA)
    gm = jnp.transpose(gm, (1, 2, 0, 3)).reshape(2, 2 * N_DEV, 6 * D)
    mod_lat = [lax.dynamic_slice(gm[l], (me, 0), (1, 6 * D)).reshape(6, 1, 1, D) for l in range(2)]
    mod_ctx = [gm[l][N_DEV].reshape(6, 1, 1, D) for l in range(2)]

    def mod(l, k, both):
        return jnp.concatenate([mod_lat[l][k], mod_ctx[l][k]], axis=0) if both else mod_lat[l][k]

    def from_cols(g):
        return jnp.transpose(g, (1, 0, 2)).reshape(g.shape[1], -1)

    def from_rows(g):
        return g.reshape(-1, g.shape[2])

    def ag(x):
        return (x, False)

    def a2a(x):
        return (x, True)

    cos, sin = _rope_tables(T, Tc)
    n1g, n2g = a['norm1_g'], a['norm2_g']
    a_bst = a['a_b_s'][0].T
    mm = functools.partial(_mm)
    up_sh, dn_sh = bf(a['ffn_w_up']), bf(a['ffn_w_down'])

    x0 = jnp.concatenate([a['x'][0], a['ctx'][0]], axis=0)
    h1, (w_abin,) = _normmod_fwd(x0, n1g[0:1], mod(0, 0, True), mod(0, 1, True), T, "l0_norm1",
                                 carry=[ag(bf(a['ab_w_in'][0]))])
    s1, s2, s3 = 3 * D // 16, 6 * D // 16, 11 * D // 16
    z, (g_about, g_up0a) = mm(h1, w_abin, mode="nn", out_dtype=F32, name="l0_ab_in", bias=a['ab_b_in'], b_dev=True,
                              carry=[ag(bf(a['ab_w_out'][0])), ag(up_sh[0][:s1])])
    w_about = from_rows(g_about)
    y, (g_up0b,) = _gmlp_fwd(z, a['a_ln_g'], a['a_ln_b'], a['a_w_s'][0], a_bst, W, "l0_gmlp",
                             carry=[ag(up_sh[0][s1:s2])])
    (y, hc_b), (g_up0c,) = _conf_fwd(z, y, bcw, a['b_conv_b'], a['b_ln_g'], a['b_ln_b'], W, Wb, T, "l0_conf",
                                     carry=[ag(up_sh[0][s2:s3])])
    (x1, o1), (g_up0d,) = mm(y, w_about, mode="nn", out_dtype=F32, name="l0_ab_out", res=x0, gate=mod(0, 2, True),
                             seg_t=T, carry=[ag(up_sh[0][s3:])])
    w_up = [jnp.concatenate([g_up0a, g_up0b, g_up0c, g_up0d], axis=1), None]
    h2 = _normmod_fwd(x1, n2g[0:1], mod(0, 3, True), mod(0, 4, True), T, "l0_norm2")
    z2, (g_dn0,) = mm(h2, w_up[0], mode="nn", out_dtype=BF16, name="l0_up", b_dev=True, carry=[ag(dn_sh[0])])
    w_dn = [from_rows(g_dn0), None]
    a2, gc2 = _ffn_act_fwd(z2, fcw[0], a['ffn_conv_b'][0:1], T, "l0_act")
    (x2, o2), (g_in, g_uq) = mm(a2, w_dn[0], mode="nn", out_dtype=F32, name="l0_down", res=x1, gate=mod(0, 5, True),
                                seg_t=T, tiles=(M // 8, None, Fd // 2),
                                carry=[ag(bf(a['mla_w_in'][0])), ag(bf(a['mla_w_uq'][0]))])
    w_in = from_rows(g_in)
    w_in = jnp.concatenate([w_in, w_in[:, QL + KL:]], axis=1)
    w_uq = from_cols(g_uq).reshape(QL, H, NOPE + ROPE)
    w_uq = jnp.concatenate([w_uq[:, :, :NOPE].reshape(QL, HN), w_uq[:, :, NOPE:].reshape(QL, HR)], axis=1)

    h3 = _normmod_fwd(x2, n1g[1:2], mod(1, 0, True), mod(1, 1, True), T, "l1_norm1")
    z3, (g_ukv,) = mm(h3, w_in, mode="nn", out_dtype=F32, name="l1_mla_in", carry=[ag(bf(a['mla_w_ukv'][0]))])
    w_ukv = g_ukv
    cqn, ckvn, kpe = _mla_prep_fwd(z3, gq, gkv, cos, sin, QL, KL, "l1_prep")
    q, (g_wo,) = mm(cqn, w_uq, mode="nn", out_dtype=F32, name="l1_uq", rows=T, carry=[ag(bf(a['mla_w_o'][0]))])
    w_o = from_rows(g_wo)
    kv = mm(ckvn, w_ukv, mode="nn", out_dtype=BF16, name="l1_ukv", b_dev=True)
    qa = _qrope_fwd(q, cos, sin, HN, "l1_qrope")
    (o_att, lse), (g_up1,) = _attn_fwd(qa, kv, kpe, T, H, "l1_attn", carry=[ag(up_sh[1])])
    w_up[1] = g_up1
    x3, o3 = mm(o_att, w_o, mode="nn", out_dtype=F32, name="l1_wo", res=x2, gate=mod(1, 2, False), seg_t=T)
    h4 = _normmod_fwd(x3, n2g[1:2], mod(1, 3, False), mod(1, 4, False), T, "l1_norm2")
    z4, (g_dn1,) = mm(h4, w_up[1], mode="nn", out_dtype=BF16, name="l1_up", b_dev=True, carry=[ag(dn_sh[1])])
    w_dn[1] = from_rows(g_dn1)
    a4, gc4 = _ffn_act_fwd(z4, fcw[1], a['ffn_conv_b'][1:2], T, "l1_act")
    x4, o4 = mm(a4, w_dn[1], mode="nn", out_dtype=F32, name="l1_down", res=x3, gate=mod(1, 5, False), seg_t=T,
                tiles=(T // 8, None, Fd // 2))

    dx4, loss_cols, d_fng, do4, dg2_1 = _final(x4, a['final_norm_g'][None], a['loss_target'][0], o4, mod(1, 5, False),
                                               "final")
    loss = lax.psum(jnp.sum(loss_cols), ("x", "y", "c"))

    def cols(dw):
        k, n = dw.shape
        return jnp.transpose(dw.reshape(k, N_DEV, n // N_DEV), (1, 0, 2))

    def rows(dw):
        return dw.reshape(N_DEV, dw.shape[0] // N_DEV, dw.shape[1])

    da4 = mm(do4, w_dn[1], mode="nt", out_dtype=BF16, name="l1_down_dx")
    dw_dn1 = mm(a4, do4, mode="tn", out_dtype=BF16, name="l1_down_dw")
    dz4, dfcw1, dfcb1 = _ffn_act_bwd(z4, gc4, da4, fcw[1], T, "l1_act_bwd")
    dw_up1, (r_dn1,) = mm(h4, dz4, mode="tn", out_dtype=BF16, name="l1_up_dw", out_dev=True, halves=True,
                          carry=[a2a(rows(dw_dn1))])
    dh4 = mm(dz4, w_up[1], mode="nt", out_dtype=F32, name="l1_up_dx", b_dev=True, halves=True, pair=True)
    dx3, dn2g1, dsh2_1, dsc2_1, do3, dg1_1 = _normmod_bwd(x3, n2g[1:2], mod(1, 4, False), dh4, dx4, T, "l1_norm2_bwd",
                                                         o_prev=o3, gate_prev=mod(1, 2, False))
    d_oatt = mm(do3, w_o, mode="nt", out_dtype=BF16, name="l1_wo_dx")
    dw_o = mm(o_att, do3, mode="tn", out_dtype=BF16, name="l1_wo_dw")
    (dqa, dqpe, dkv, dkpe), (r_up1, r_wo) = _attn_bwd(qa, kv, kpe, d_oatt, o_att, lse, T, H, "l1_attn_bwd",
                                                      carry=[a2a(dw_up1), a2a(rows(dw_o))])
    dqa = _qrope_bwd(dqpe, dqa, cos, sin, HN, "l1_qrope_bwd")
    dcq = mm(dqa, w_uq, mode="nt", out_dtype=F32, name="l1_uq_dx")
    dw_uq = mm(cqn, dqa, mode="tn", out_dtype=BF16, name="l1_uq_dw", rows=T)
    dw_uq = jnp.concatenate([dw_uq[:, :HN].reshape(QL, H, NOPE), dw_uq[:, HN:].reshape(QL, H, ROPE)], axis=2)
    dw_uq = dw_uq.reshape(QL, H * (NOPE + ROPE))
    dckv = mm(dkv, w_ukv, mode="nt", out_dtype=F32, name="l1_ukv_dx", b_dev=True, pair=True)
    dw_ukv = mm(ckvn, dkv, mode="tn", out_dtype=BF16, name="l1_ukv_dw", out_dev=True)
    dz3, dgq, dgkv = _mla_prep_bwd(z3, dcq, dckv, dkpe, gq, gkv, cos, sin, QL, KL, T, "l1_prep_bwd")
    dh3 = mm(dz3, w_in, mode="nt", out_dtype=F32, name="l1_mla_in_dx")
    dw_in = mm(h3, dz3, mode="tn", out_dtype=BF16, name="l1_mla_in_dw").astype(F32)
    dw_in = jnp.concatenate([dw_in[:, :QL + KL], dw_in[:, QL + KL:QL + KL + ROPE] + dw_in[:, QL + KL + ROPE:QL + KL + 2 * ROPE]],
                            axis=1).astype(BF16)
    dx2, dn1g1, dsh1_1, dsc1_1, do2, dg2_0 = _normmod_bwd(x2, n1g[1:2], mod(1, 1, True), dh3, dx3, T, "l1_norm1_bwd",
                                                         o_prev=o2, gate_prev=mod(0, 5, True))
    da2, (r_uq, r_ukv) = mm(do2, w_dn[0], mode="nt", out_dtype=BF16, name="l0_down_dx",
                            carry=[a2a(cols(dw_uq)), a2a(dw_ukv)])
    dw_dn0, (r_in,) = mm(a2, do2, mode="tn", out_dtype=BF16, name="l0_down_dw", carry=[a2a(rows(dw_in))])
    dz2, dfcw0, dfcb0 = _ffn_act_bwd(z2, gc2, da2, fcw[0], T, "l0_act_bwd")
    dw_up0, (r_dn0,) = mm(h2, dz2, mode="tn", out_dtype=BF16, name="l0_up_dw", out_dev=True, halves=True,
                          carry=[a2a(rows(dw_dn0))])
    dh2, (r_up0a,) = mm(dz2, w_up[0], mode="nt", out_dtype=F32, name="l0_up_dx", b_dev=True, halves=True, pair=True,
                        carry=[a2a(dw_up0[:, :D // 2])])
    dx1, dn2g0, dsh2_0, dsc2_0, do1, dg1_0 = _normmod_bwd(x1, n2g[0:1], mod(0, 4, True), dh2, dx2, T, "l0_norm2_bwd",
                                                         o_prev=o1, gate_prev=mod(0, 2, True))
    dy = mm(do1, w_about, mode="nt", out_dtype=F32, name="l0_ab_out_dx")
    s_up = 13 * D // 16
    dw_about, (r_up0c,) = mm(y, do1, mode="tn", out_dtype=BF16, name="l0_ab_out_dw", carry=[a2a(dw_up0[:, s_up:])])
    dz, dlag, dlab, dws, dbs, dbin_a = _gmlp_bwd(z, dy, a['a_ln_g'], a['a_ln_b'], a['a_w_s'][0], a_bst, W, "l0_gmlp_bwd")
    dhc, dlbg, dlbb, dbcb = _conf_bwd1(hc_b, dy, a['b_ln_g'], a['b_ln_b'], W, Wb, T, "l0_conf_bwd1")
    (dz, dbcw, dbin_b), (r_up0b,) = _conf_bwd2(z, dhc, dz, bcw, W, Wb, T, "l0_conf_bwd2",
                                               carry=[a2a(dw_up0[:, D // 2:s_up])])
    dh1, (r_about,) = mm(dz, w_abin, mode="nt", out_dtype=F32, name="l0_ab_in_dx", b_dev=True, pair=True,
                         carry=[a2a(rows(dw_about))])
    dw_abin_a = mm(h1, dz, mode="tn", out_dtype=BF16, name="l0_ab_in_dw_a", out_dev=True, p_range=(0, D // 2))
    dw_abin_b, (r_abin_a,) = mm(h1, dz, mode="tn", out_dtype=BF16, name="l0_ab_in_dw_b", out_dev=True,
                                p_range=(D // 2, D // 2), carry=[a2a(dw_abin_a)])
    dx0, dn1g0, dsh1_0, dsc1_0 = _normmod_bwd(x0, n1g[0:1], mod(0, 1, True), dh1, dx1, T, "l0_norm1_bwd")

    zero = jnp.zeros((D,), F32)
    dmod = jnp.stack([
        jnp.stack([jnp.stack([dsh1_0[0, 0], dsc1_0[0, 0], dg1_0[0, 0], dsh2_0[0, 0], dsc2_0[0, 0], dg2_0[0, 0]]),
                   jnp.stack([dsh1_0[1, 0], dsc1_0[1, 0], dg1_0[1, 0], dsh2_0[1, 0], dsc2_0[1, 0], dg2_0[1, 0]])]),
        jnp.stack([jnp.stack([dsh1_1[0, 0], dsc1_1[0, 0], dg1_1[0, 0], dsh2_1[0, 0], dsc2_1[0, 0], dg2_1[0, 0]]),
                   jnp.stack([dsh1_1[1, 0], dsc1_1[1, 0], zero, zero, zero, zero])])])
    small = {
        'norm1_g': jnp.concatenate([dn1g0, dn1g1], axis=0), 'norm2_g': jnp.concatenate([dn2g0, dn2g1], axis=0),
        'ab_b_in': jnp.concatenate([dbin_a, dbin_b], axis=1), 'a_ln_g': dlag, 'a_ln_b': dlab, 'a_w_s': dws[None],
        'a_b_s': jnp.sum(dbs, axis=-1)[None], 'b_conv_w': dbcw, 'b_conv_b': dbcb, 'b_ln_g': dlbg, 'b_ln_b': dlbb,
        'mla_q_norm_g': dgq, 'mla_kv_norm_g': dgkv, 'ffn_conv_w': jnp.stack([dfcw0, dfcw1]),
        'ffn_conv_b': jnp.concatenate([dfcb0, dfcb1], axis=0), 'final_norm_g': d_fng[0],
    }
    names = list(small)
    up2d = (2 * D, a['ffn_w_up'].shape[2])
    res_up, (g2,) = _adamw(a['ffn_w_up'].reshape(up2d), a['m_ffn_w_up'].reshape(up2d), a['v_ffn_w_up'].reshape(up2d),
                           "adamw_ffn_w_up", recv=[r_up0a, r_up0b, r_up0c, r_up1],
                           carry=[ag(_pack([dmod] + [small[n] for n in names]))])
    red = _sum_lead(g2, "sum_small_grads").reshape(-1)
    red = dict(zip(names, _unpack(red, [(2, 2, 6, D)] + [small[n].shape for n in names])[1:]))
    dmod_all = g2.reshape(N_DEV, -1)[:, :2 * 2 * 6 * D].reshape(N_DEV, 2, 2, 6 * D)

    a16g = jnp.concatenate([c_all, jnp.tile(a['c_ctx'][None], (N_DEV, 1))], axis=0)
    dm_loc = lax.dynamic_slice(dmod_all, (0, 0, 0, me * NA), (N_DEV, 2, 2, NA))
    g_wada, cpart = [], []
    for l in range(2):
        dm16 = jnp.concatenate([dm_loc[:, l, 0], dm_loc[:, l, 1]], axis=0)
        g_wada.append(mm(a16g, dm16, mode="tn", out_dtype=F32, name=f"ada_dw{l}", a_silu=True))
        cpart.append(mm(dm_loc[:, l, 1], a['w_ada'][l], mode="nt", out_dtype=F32, name=f"ada_dc{l}"))
    g_bada = _sum_lead(jnp.transpose(dmod_all, (0, 2, 1, 3)).reshape(2 * N_DEV, 2 * 6 * D // 128, 128), "sum_b_ada")
    g_cc = _all_gather(jnp.concatenate(cpart, axis=0), "ag_c_ctx")
    g_cc = _sum_lead(g_cc.reshape(2 * N_DEV * N_DEV, D // 128, 128), "sum_c_ctx").reshape(D)
    grads = {
        'c_ctx': g_cc * _dsilu(a['c_ctx']), 'w_ada': jnp.stack(g_wada), 'b_ada': g_bada.reshape(2, 6 * D),
        'b_conv_w': lax.dynamic_slice(red['b_conv_w'], (0, me * (Wb // N_DEV)), (kw, Wb // N_DEV))[None],
        'ffn_conv_w': lax.dynamic_slice(red['ffn_conv_w'], (0, 0, me * (Fd // N_DEV)), (2, 3, Fd // N_DEV)),
        'mla_q_norm_g': lax.dynamic_slice(red['mla_q_norm_g'], (0, me * (QL // N_DEV)), (1, QL // N_DEV)),
        'mla_kv_norm_g': lax.dynamic_slice(red['mla_kv_norm_g'], (0, me * (KL // N_DEV)), (1, KL // N_DEV)),
    }
    for n in names:
        if n not in grads:
            grads[n] = red[n].reshape(a[n].shape)

    recvs = {'ab_w_out': [r_about], 'mla_w_in': [r_in], 'mla_w_uq': [r_uq], 'mla_w_ukv': [r_ukv],
             'mla_w_o': [r_wo], 'ffn_w_down': [r_dn0, r_dn1]}
    out = {}
    for n in WEIGHTS:
        shp = a[n].shape
        w2 = a[n].reshape(-1, shp[-1])
        m2, v2 = a['m_' + n].reshape(w2.shape), a['v_' + n].reshape(w2.shape)
        if n == 'ffn_w_up':
            res = res_up
        elif n in recvs:
            res = _adamw(w2, m2, v2, "adamw_" + n, recv=recvs[n])
        elif n == 'w_ada':
            g2d = grads[n].reshape(w2.shape)
            res, (r_abin_b,) = _adamw(w2, m2, v2, "adamw_" + n, g=g2d, carry=[a2a(dw_abin_b)])
            res = (g2d,) + tuple(res)
            recvs['ab_w_in'] = [r_abin_a, r_abin_b]
        else:
            g2d = grads[n].reshape(w2.shape)
            res = (g2d,) + tuple(_adamw(w2, m2, v2, "adamw_" + n, g=g2d))
        out[n] = [r.reshape(shp) for r in res]
    return (loss, dx0[:T][None], *[out[n][0] for n in WEIGHTS], *[out[n][1] for n in WEIGHTS],
            *[out[n][2] for n in WEIGHTS], *[out[n][3] for n in WEIGHTS])


def kernel(x, c, ctx, c_ctx, norm1_g, norm2_g, w_ada, b_ada, ab_w_in, ab_b_in, a_ln_g, a_ln_b, a_w_s, a_b_s, b_conv_w, b_conv_b, b_ln_g, b_ln_b, ab_w_out, mla_w_in, mla_q_norm_g, mla_w_uq, mla_kv_norm_g, mla_w_ukv, mla_w_o, ffn_w_up, ffn_conv_w, ffn_conv_b, ffn_w_down, final_norm_g, loss_target, m_c_ctx, m_norm1_g, m_norm2_g, m_w_ada, m_b_ada, m_ab_w_in, m_ab_b_in, m_a_ln_g, m_a_ln_b, m_a_w_s, m_a_b_s, m_b_conv_w, m_b_conv_b, m_b_ln_g, m_b_ln_b, m_ab_w_out, m_mla_w_in, m_mla_q_norm_g, m_mla_w_uq, m_mla_kv_norm_g, m_mla_w_ukv, m_mla_w_o, m_ffn_w_up, m_ffn_conv_w, m_ffn_conv_b, m_ffn_w_down, m_final_norm_g, v_c_ctx, v_norm1_g, v_norm2_g, v_w_ada, v_b_ada, v_ab_w_in, v_ab_b_in, v_a_ln_g, v_a_ln_b, v_a_w_s, v_a_b_s, v_b_conv_w, v_b_conv_b, v_b_ln_g, v_b_ln_b, v_ab_w_out, v_mla_w_in, v_mla_q_norm_g, v_mla_w_uq, v_mla_kv_norm_g, v_mla_w_ukv, v_mla_w_o, v_ffn_w_up, v_ffn_conv_w, v_ffn_conv_b, v_ffn_w_down, v_final_norm_g):
    return _step(dict(locals()))
```

```python
import functools
import math

import jax
import jax.numpy as jnp
from jax import lax
from jax.experimental import pallas as pl
from jax.experimental.pallas import tpu as pltpu

F32 = jnp.float32
BF16 = jnp.bfloat16
SDS = jax.ShapeDtypeStruct

N_DEV = 8
EPS = 1e-6
CHUNK = 128
NOPE = 128
ROPE = 64
VDIM = 128
GRID_W = 64
ROPE_THETA = 10000.0
HALO = 16
ADAM_LR, ADAM_B1, ADAM_B2, ADAM_EPS, ADAM_WD, ADAM_STEP = 0.001, 0.9, 0.999, 1e-08, 0.01, 10
VMEM_LIMIT = 56 * 1024 * 1024


def _tile(n, prefs):
    for p in prefs:
        if n % p == 0:
            return p
    return n


def _params(sem, vmem=VMEM_LIMIT):
    return pltpu.CompilerParams(dimension_semantics=sem, vmem_limit_bytes=vmem)


def _sigmoid(x):
    return 0.5 * jnp.tanh(0.5 * x) + 0.5


def _silu(x):
    return x * _sigmoid(x)


def _dsilu(x):
    s = _sigmoid(x)
    return s * (1.0 + x * (1.0 - s))


_GELU_C = math.sqrt(2.0 / math.pi)


def _gelu(x):
    return 0.5 * x * (1.0 + jnp.tanh(_GELU_C * (x + 0.044715 * x * x * x)))


def _dgelu(x):
    t = jnp.tanh(_GELU_C * (x + 0.044715 * x * x * x))
    return 0.5 * (1.0 + t) + 0.5 * x * (1.0 - t * t) * _GELU_C * (1.0 + 3.0 * 0.044715 * x * x)


def _colsum(v):
    return jnp.sum(v, axis=0, keepdims=True)


_SIBLING = 1
_CHIPS = (2, 4, 6)


def _xchg(x_ref, o_ref, send_sems, recv_sems, local_sem, scatter):
    ax, ay, ac = lax.axis_index("x"), lax.axis_index("y"), lax.axis_index("c")
    me = 4 * ax + 2 * ay + ac

    def dev(k):
        return ax ^ (k >> 2), ay ^ ((k >> 1) & 1), ac ^ (k & 1)

    def idx(k):
        px, py, pc = dev(k)
        return 4 * px + 2 * py + pc

    def copy(k, src, dst, to):
        return pltpu.make_async_remote_copy(src_ref=src, dst_ref=dst, send_sem=send_sems.at[k - 1],
                                            recv_sem=recv_sems.at[k - 1], device_id=dev(to),
                                            device_id_type=pl.DeviceIdType.MESH)

    def own():
        return pltpu.make_async_copy(x_ref.at[me] if scatter else x_ref, o_ref.at[me], local_sem)

    def sends():
        if scatter:
            return [copy(k, x_ref.at[idx(k)], o_ref.at[me], k) for k in range(1, N_DEV)]
        return [copy(k, x_ref, o_ref.at[me], k) for k in (_SIBLING,) + _CHIPS]

    def forwards():
        return [] if scatter else [copy(j + 1, o_ref.at[idx(j)], o_ref.at[idx(j)], _SIBLING) for j in _CHIPS]

    def arrival(k):
        return copy(k, o_ref.at[idx(k)], o_ref.at[idx(k)], k)

    return own, sends, forwards, arrival


def _xchg_start(*refs, scatter):
    own, sends, _, _ = _xchg(*refs, scatter)
    own().start()
    for cp in sends():
        cp.start()


def _xchg_forward(*refs, scatter):
    _, _, forwards, arrival = _xchg(*refs, scatter)
    if not scatter:
        for j, fw in zip(_CHIPS, forwards()):
            arrival(j).wait_recv()
            fw.start()


def _xchg_finish(*refs, scatter):
    own, sends, forwards, arrival = _xchg(*refs, scatter)
    for k in range(1, N_DEV):
        if scatter or k not in _CHIPS:
            arrival(k).wait_recv()
    for cp in sends() + forwards():
        cp.wait_send()
    own().wait()


_XCHG_SEMS = [pltpu.SemaphoreType.DMA((N_DEV - 1,)), pltpu.SemaphoreType.DMA((N_DEV - 1,)), pltpu.SemaphoreType.DMA]


def _xchg_shape(x, scatter):
    return SDS((N_DEV,) + tuple(x.shape[1:] if scatter else x.shape), x.dtype)


def _exchange(x, *, scatter, name):
    def body(*refs):
        _xchg_start(*refs, scatter=scatter)
        _xchg_forward(*refs, scatter=scatter)
        _xchg_finish(*refs, scatter=scatter)

    return pl.pallas_call(
        body, out_shape=_xchg_shape(x, scatter),
        in_specs=[pl.BlockSpec(memory_space=pl.ANY)], out_specs=pl.BlockSpec(memory_space=pl.ANY),
        scratch_shapes=list(_XCHG_SEMS), name=name)(x)


def _carried(body, carry, n_in, n_out, n_scratch, grid):
    nc = len(carry)
    total = math.prod(grid)
    mid = (3 * total) // 4

    def wrapped(*refs):
        ins, cin = refs[:n_in], refs[n_in:n_in + nc]
        o0 = n_in + nc
        outs, cout = refs[o0:o0 + n_out], refs[o0 + n_out:o0 + n_out + nc]
        scr = refs[o0 + n_out + nc:]
        sems = scr[n_scratch:]
        step = pl.program_id(0)
        for ax in range(1, len(grid)):
            step = step * grid[ax] + pl.program_id(ax)

        def each(fn):
            for c in range(nc):
                fn(cin[c], cout[c], *sems[3 * c:3 * c + 3], scatter=carry[c][1])

        @pl.when(step == 0)
        def _():
            each(_xchg_start)

        body(*ins, *outs, *scr[:n_scratch])

        if mid < total - 1:
            @pl.when(step == mid)
            def _():
                each(_xchg_forward)

        @pl.when(step == total - 1)
        def _():
            if mid >= total - 1:
                each(_xchg_forward)
            each(_xchg_finish)

    return wrapped


def _carry_call(body, carry, *, grid, out_shape, in_specs, out_specs, scratch_shapes, sem, name, ins, aliases=None):
    carry = carry or []
    nc = len(carry)
    if nc:
        body = _carried(body, carry, len(in_specs), len(out_shape), len(scratch_shapes), grid)
        anyspec = pl.BlockSpec(memory_space=pl.ANY)
        in_specs = list(in_specs) + [anyspec] * nc
        out_specs = list(out_specs) + [anyspec] * nc
        out_shape = list(out_shape) + [_xchg_shape(x, sc) for x, sc in carry]
        scratch_shapes = list(scratch_shapes) + list(_XCHG_SEMS) * nc
        ins = list(ins) + [x for x, _ in carry]
        sem = ("arbitrary",) * len(grid)
    out = pl.pallas_call(body, out_shape=out_shape, grid=grid, in_specs=in_specs, out_specs=out_specs,
                         scratch_shapes=scratch_shapes, compiler_params=_params(sem), name=name,
                         input_output_aliases=aliases or {})(*ins)
    n_main = len(out) - nc
    return list(out[:n_main]), list(out[n_main:])


def _all_gather(x, name):
    return _exchange(x, scatter=False, name=name)


def _all_to_all(x, name):
    return _exchange(x, scatter=True, name=name)


def _sum_lead(x, name):
    n, R, C = x.shape
    tr = _tile(R, (512, 256, 128, 64, 32, 16, 8))

    def body(x_ref, o_ref):
        acc = x_ref[0]
        for d in range(1, n):
            acc = acc + x_ref[d]
        o_ref[...] = acc

    return pl.pallas_call(
        body, out_shape=SDS((R, C), F32), grid=(R // tr,),
        in_specs=[pl.BlockSpec((n, tr, C), lambda i: (0, i, 0))], out_specs=pl.BlockSpec((tr, C), lambda i: (i, 0)),
        compiler_params=_params(("parallel",)), name=name)(x)


_TP = (1408, 1088, 1024, 768, 512, 256, 128)
_TQ = (1408, 1024, 768, 512, 256, 128)
_TR = (2048, 1408, 1024, 768, 512, 256, 128)
_TR_TN = (2176, 2048, 1088, 1024, 512, 256, 128)


def _mm(a, b, *, mode, out_dtype, name, rows=None, bias=None, res=None, gate=None, seg_t=None, a_silu=False, carry=None,
        b_dev=False, out_dev=False, p_range=None, halves=False, tiles=None, pair=False, norm=None):
    if mode == "nn":
        P, R, Q = rows or a.shape[0], a.shape[1], (b.shape[0] * b.shape[2] if b_dev else b.shape[1])
    elif mode == "nt":
        P, R, Q = rows or a.shape[-2], (2 * a.shape[2] if halves else a.shape[1]), (b.shape[1] if b_dev else b.shape[0])
    else:
        R, P, Q = rows or a.shape[0], a.shape[1], (2 * b.shape[2] if halves else b.shape[1])
    p0 = 0
    if p_range is not None:
        p0, P = p_range
    tp = _tile(P, _TP)
    tq = _tile(Q // N_DEV if (out_dev or (b_dev and mode == "nn")) else Q, _TQ)
    tr = _tile(R // N_DEV if (b_dev and mode == "nt") else R, _TR if mode != "tn" else _TR_TN)
    if tiles is not None:
        tp, tq, tr = (o or d for o, d in zip(tiles, (tp, tq, tr)))
    if pair:
        assert mode == "nt" and b_dev and tr == R // N_DEV
        tr = 2 * tr
    nk = R // tr
    qd = (Q // N_DEV) // tq
    rd = (R // N_DEV) // tr
    if mode == "nn":
        a_spec = pl.BlockSpec((tp, tr), lambda i, j, k: (i, k))
        b_spec = (pl.BlockSpec((None, tr, tq), lambda i, j, k: (j // qd, k, j % qd)) if b_dev
                  else pl.BlockSpec((tr, tq), lambda i, j, k: (k, j)))
        dims = (((1,), (0,)), ((), ()))
    elif mode == "nt":
        kh = (R // 2) // tr
        a_spec = (pl.BlockSpec((None, tp, tr), lambda i, j, k: (k // kh, i, k % kh)) if halves
                  else pl.BlockSpec((tp, tr), lambda i, j, k: (i, k)))
        if pair:
            b_spec = pl.BlockSpec((2, tq, tr // 2), lambda i, j, k: (k, j, 0))
        else:
            b_spec = (pl.BlockSpec((None, tq, tr), lambda i, j, k: (k // rd, j, k % rd)) if b_dev
                      else pl.BlockSpec((tq, tr), lambda i, j, k: (j, k)))
        dims = (((1,), (1,)), ((), ()))
    else:
        pb = p0 // tp
        qh = (Q // 2) // tq
        a_spec = pl.BlockSpec((tr, tp), lambda i, j, k: (k, i + pb))
        b_spec = (pl.BlockSpec((None, tr, tq), lambda i, j, k: (j // qh, k, j % qh)) if halves
                  else pl.BlockSpec((tr, tq), lambda i, j, k: (k, j)))
        dims = (((0,), (0,)), ((), ()))
    ins, in_specs = [a, b], [a_spec, b_spec]
    if bias is not None:
        ins.append(bias)
        in_specs.append(pl.BlockSpec((1, tq), lambda i, j, k: (0, j)))
    gated = res is not None
    if gated:
        n_seg = gate.shape[0]
        ins += [res, gate]
        in_specs += [pl.BlockSpec((tp, tq), lambda i, j, k: (i, j)),
                     pl.BlockSpec((n_seg, 1, tq), lambda i, j, k: (0, 0, j))]
    if out_dev:
        out_shape = [SDS((N_DEV, P, Q // N_DEV), out_dtype)]
        out_specs = [pl.BlockSpec((None, tp, tq), lambda i, j, k: (j // qd, i, j % qd))]
    else:
        out_shape = [SDS((P, Q), out_dtype)]
        out_specs = [pl.BlockSpec((tp, tq), lambda i, j, k: (i, j))]
    if gated:
        out_shape.append(SDS((P, Q), BF16))
        out_specs.append(pl.BlockSpec((tp, tq), lambda i, j, k: (i, j)))
    if norm is not None:
        assert gated and tq == Q
        n_segn = norm[1].shape[0]
        ins += list(norm)
        in_specs += [pl.BlockSpec((1, tq), lambda i, j, k: (0, j)),
                     pl.BlockSpec((n_segn, 1, tq), lambda i, j, k: (0, 0, j)),
                     pl.BlockSpec((n_segn, 1, tq), lambda i, j, k: (0, 0, j))]
        out_shape.append(SDS((P, Q), BF16))
        out_specs.append(pl.BlockSpec((tp, tq), lambda i, j, k: (i, j)))

    def body(*refs):
        a_ref, b_ref = refs[0], refs[1]
        pos = 2
        bias_ref = res_ref = gate_ref = o2_ref = None
        if bias is not None:
            bias_ref = refs[pos]
            pos += 1
        if gated:
            res_ref, gate_ref = refs[pos], refs[pos + 1]
            pos += 2
        if norm is not None:
            ng_ref, nsh_ref, nsc_ref = refs[pos:pos + 3]
            pos += 3
        o_ref = refs[pos]
        pos += 1
        if gated:
            o2_ref = refs[pos]
            pos += 1
        if norm is not None:
            h_ref = refs[pos]
            pos += 1
        acc_ref = refs[pos] if nk > 1 else None
        k = pl.program_id(2)
        av = a_ref[...]
        if a_silu:
            av = _silu(av.astype(F32))
        bv = jnp.concatenate([b_ref[0], b_ref[1]], axis=1) if pair else b_ref[...]
        part = lax.dot_general(av.astype(BF16), bv.astype(BF16), dims, preferred_element_type=F32)
        if nk > 1:
            @pl.when(k == 0)
            def _():
                acc_ref[...] = part

            @pl.when(k > 0)
            def _():
                acc_ref[...] += part

        @pl.when(k == nk - 1)
        def _():
            acc = acc_ref[...] if nk > 1 else part
            if bias_ref is not None:
                acc = acc + bias_ref[...]
            if gated:
                if n_seg == 1:
                    g = gate_ref[0]
                else:
                    row = pl.program_id(0) * tp + lax.broadcasted_iota(jnp.int32, (tp, 1), 0)
                    g = jnp.where(row < seg_t, gate_ref[0], gate_ref[1])
                xn = res_ref[...] + g * acc
                o_ref[...] = xn.astype(o_ref.dtype)
                o2_ref[...] = acc.astype(BF16)
                if norm is not None:
                    r = lax.rsqrt(jnp.mean(xn * xn, axis=-1, keepdims=True) + EPS)
                    y = xn * r * ng_ref[...]
                    if n_segn == 1:
                        nsh, nsc = nsh_ref[0], nsc_ref[0]
                    else:
                        row = pl.program_id(0) * tp + lax.broadcasted_iota(jnp.int32, (tp, 1), 0)
                        nsh = jnp.where(row < seg_t, nsh_ref[0], nsh_ref[1])
                        nsc = jnp.where(row < seg_t, nsc_ref[0], nsc_ref[1])
                    h_ref[...] = (y * (1.0 + nsc) + nsh).astype(BF16)
            else:
                o_ref[...] = acc.astype(o_ref.dtype)

    out, carried = _carry_call(
        body, carry, grid=(P // tp, Q // tq, nk), out_shape=out_shape, in_specs=in_specs, out_specs=out_specs,
        scratch_shapes=[pltpu.VMEM((tp, tq), F32)] if nk > 1 else [], sem=("parallel", "parallel", "arbitrary"),
        name=name, ins=ins)
    res_out = tuple(out) if gated else out[0]
    return (res_out, carried) if carry else res_out


def _row_tile(seg_t, m):
    return 256 if (seg_t % 256 == 0 and m % 256 == 0) else 128


def _normmod_fwd(x, g, sh, sc, seg_t, name, carry=None):
    M, D = x.shape
    tm = _row_tile(seg_t, M)
    n_seg = sh.shape[0]
    nt = seg_t // tm

    def seg(i):
        return ((i >= nt).astype(jnp.int32) if n_seg == 2 else 0, 0, 0)

    def body(x_ref, g_ref, sh_ref, sc_ref, o_ref):
        xv = x_ref[...]
        r = lax.rsqrt(jnp.mean(xv * xv, axis=-1, keepdims=True) + EPS)
        y = xv * r * g_ref[...]
        o_ref[...] = (y * (1.0 + sc_ref[0]) + sh_ref[0]).astype(BF16)

    out, carried = _carry_call(
        body, carry, out_shape=[SDS((M, D), BF16)], grid=(M // tm,),
        in_specs=[pl.BlockSpec((tm, D), lambda i: (i, 0)), pl.BlockSpec((1, D), lambda i: (0, 0)),
                  pl.BlockSpec((1, 1, D), seg), pl.BlockSpec((1, 1, D), seg)],
        out_specs=[pl.BlockSpec((tm, D), lambda i: (i, 0))], scratch_shapes=[],
        sem=("parallel",), name=name, ins=(x, g, sh, sc))
    return (out[0], carried) if carry else out[0]


def _normmod_bwd(x, g, sc, dh, dx_in, seg_t, name, o_prev=None, gate_prev=None):
    M, D = x.shape
    tm = _row_tile(seg_t, M)
    n_seg = sc.shape[0]
    nt = seg_t // tm
    n_in = dx_in.shape[0] // tm
    with_prev = o_prev is not None
    n_segp = gate_prev.shape[0] if with_prev else 0

    def seg(i):
        return ((i >= nt).astype(jnp.int32) if n_seg == 2 else 0, 0, 0)

    def segp(i):
        return ((i >= nt).astype(jnp.int32) if n_segp == 2 else 0, 0, 0)

    def body(*refs):
        x_ref, g_ref, sc_ref, dh_ref, dxin_ref = refs[:5]
        pos = 5
        if with_prev:
            op_ref, gp_ref = refs[5], refs[6]
            pos = 7
        dx_ref, dg_ref, dsh_ref, dsc_ref = refs[pos:pos + 4]
        if with_prev:
            dop_ref, dgp_ref = refs[pos + 4], refs[pos + 5]
        i = pl.program_id(0)
        xv = x_ref[...]
        r = lax.rsqrt(jnp.mean(xv * xv, axis=-1, keepdims=True) + EPS)
        xh = xv * r
        gv = g_ref[...]
        dhv = dh_ref[...].astype(F32)
        dy = dhv * (1.0 + sc_ref[0])
        dxh = dy * gv
        dxv = r * (dxh - xh * jnp.mean(dxh * xh, axis=-1, keepdims=True))
        if n_in * tm < M:
            dxv = dxv + jnp.where(i < n_in, dxin_ref[...], 0.0)
        else:
            dxv = dxv + dxin_ref[...]
        dx_ref[...] = dxv

        @pl.when(i == 0)
        def _():
            dg_ref[...] = jnp.zeros_like(dg_ref)

        first_of_seg = (i == 0) | (i == nt) if n_seg == 2 else (i == 0)

        @pl.when(first_of_seg)
        def _():
            dsh_ref[...] = jnp.zeros_like(dsh_ref)
            dsc_ref[...] = jnp.zeros_like(dsc_ref)

        dg_ref[...] += _colsum(dy * xh)
        dsh_ref[0] += _colsum(dhv)
        dsc_ref[0] += _colsum(dhv * xh * gv)
        if with_prev:
            first_of_segp = (i == 0) | (i == nt) if n_segp == 2 else (i == 0)

            @pl.when(first_of_segp)
            def _():
                dgp_ref[...] = jnp.zeros_like(dgp_ref)

            dop_ref[...] = (gp_ref[0] * dxv).astype(BF16)
            dgp_ref[0] += _colsum(dxv * op_ref[...].astype(F32))

    row = pl.BlockSpec((tm, D), lambda i: (i, 0))
    ins = [x, g, sc, dh, dx_in]
    in_specs = [row, pl.BlockSpec((1, D), lambda i: (0, 0)), pl.BlockSpec((1, 1, D), seg), row,
                pl.BlockSpec((tm, D), lambda i: (jnp.minimum(i, n_in - 1), 0))]
    out_shape = [SDS((M, D), F32), SDS((1, D), F32), SDS((n_seg, 1, D), F32), SDS((n_seg, 1, D), F32)]
    out_specs = [row, pl.BlockSpec((1, D), lambda i: (0, 0)), pl.BlockSpec((1, 1, D), seg), pl.BlockSpec((1, 1, D), seg)]
    if with_prev:
        ins += [o_prev, gate_prev]
        in_specs += [row, pl.BlockSpec((1, 1, D), segp)]
        out_shape += [SDS((M, D), BF16), SDS((n_segp, 1, D), F32)]
        out_specs += [row, pl.BlockSpec((1, 1, D), segp)]
    return pl.pallas_call(
        body, out_shape=out_shape, grid=(M // tm,), in_specs=in_specs, out_specs=out_specs,
        compiler_params=_params(("arbitrary",)), name=name)(*ins)


def _final(x, g, target, o_prev, gate_prev, name):
    T, D = x.shape
    tm = _tile(T, (256, 128))

    def body(x_ref, g_ref, t_ref, op_ref, gp_ref, dx_ref, loss_ref, dg_ref, dop_ref, dgp_ref):
        i = pl.program_id(0)
        xv = x_ref[...]
        r = lax.rsqrt(jnp.mean(xv * xv, axis=-1, keepdims=True) + EPS)
        xh = xv * r
        gv = g_ref[...]
        e = xh * gv - t_ref[...]
        dout = e * (1.0 / D)
        dxh = dout * gv
        dxv = r * (dxh - xh * jnp.mean(dxh * xh, axis=-1, keepdims=True))
        dx_ref[...] = dxv
        dop_ref[...] = (gp_ref[0] * dxv).astype(BF16)

        @pl.when(i == 0)
        def _():
            loss_ref[...] = jnp.zeros_like(loss_ref)
            dg_ref[...] = jnp.zeros_like(dg_ref)
            dgp_ref[...] = jnp.zeros_like(dgp_ref)

        loss_ref[...] += _colsum(e * e) * (0.5 / D)
        dg_ref[...] += _colsum(dout * xh)
        dgp_ref[0] += _colsum(dxv * op_ref[...].astype(F32))

    row = pl.BlockSpec((tm, D), lambda i: (i, 0))
    vec = pl.BlockSpec((1, D), lambda i: (0, 0))
    vec3 = pl.BlockSpec((1, 1, D), lambda i: (0, 0, 0))
    return pl.pallas_call(
        body, out_shape=[SDS((T, D), F32), SDS((1, D), F32), SDS((1, D), F32), SDS((T, D), BF16), SDS((1, 1, D), F32)],
        grid=(T // tm,), in_specs=[row, vec, row, row, vec3], out_specs=[row, vec, vec, row, vec3],
        compiler_params=_params(("arbitrary",)), name=name)(x, g, target, o_prev, gate_prev)


def _gmlp_core(z, lg, lb, ws_ref, bst):
    W = z.shape[1] // 2
    t = _gelu(z)
    u, v = t[:, :W], t[:, W:]
    mu = jnp.mean(v, axis=-1, keepdims=True)
    vc = v - mu
    rstd = lax.rsqrt(jnp.mean(vc * vc, axis=-1, keepdims=True) + EPS)
    vhat = vc * rstd
    vn = vhat * lg + lb
    vp = []
    for h in range(W // CHUNK):
        blk = vn[:, h * CHUNK:(h + 1) * CHUNK].astype(BF16)
        vp.append(jnp.dot(ws_ref[h].astype(BF16), blk, preferred_element_type=F32) + bst[:, h:h + 1])
    return u, vhat, rstd, vp


def _gmlp_fwd(z, ln_g, ln_b, w_s, b_st, W, name, carry=None):
    M = z.shape[0]
    H = W // CHUNK

    def body(z_ref, lg_ref, lb_ref, ws_ref, bst_ref, o_ref):
        u, _, _, vp = _gmlp_core(z_ref[...], lg_ref[...], lb_ref[...], ws_ref, bst_ref[...])
        for h in range(H):
            o_ref[:, h * CHUNK:(h + 1) * CHUNK] = (u[:, h * CHUNK:(h + 1) * CHUNK] * vp[h]).astype(BF16)

    vec = pl.BlockSpec((1, W), lambda i: (0, 0))
    out, carried = _carry_call(
        body, carry, out_shape=[SDS((M, 2 * W), BF16)], grid=(M // CHUNK,),
        in_specs=[pl.BlockSpec((CHUNK, 2 * W), lambda i: (i, 0)), vec, vec,
                  pl.BlockSpec((H, CHUNK, CHUNK), lambda i: (0, 0, 0)), pl.BlockSpec((CHUNK, H), lambda i: (0, 0))],
        out_specs=[pl.BlockSpec((CHUNK, W), lambda i: (i, 0))], scratch_shapes=[],
        sem=("parallel",), name=name, ins=(z, ln_g, ln_b, w_s, b_st))
    return (out[0], carried) if carry else out[0]


def _gmlp_bwd(z, dy, ln_g, ln_b, w_s, b_st, W, name):
    M = z.shape[0]
    H = W // CHUNK
    ZW = z.shape[1]

    def body(z_ref, dy_ref, lg_ref, lb_ref, ws_ref, bst_ref, dz_ref, dlg_ref, dlb_ref, dws_ref, dbs_ref, dbin_ref):
        i = pl.program_id(0)

        @pl.when(i == 0)
        def _():
            for r in (dlg_ref, dlb_ref, dws_ref, dbs_ref, dbin_ref):
                r[...] = jnp.zeros_like(r)

        zv = z_ref[...]
        lg = lg_ref[...]
        u, vhat, rstd, vp = _gmlp_core(zv, lg, lb_ref[...], ws_ref, bst_ref[...])
        vn = vhat * lg + lb_ref[...]
        dya = dy_ref[...]
        du_parts, dvn_parts = [], []
        for h in range(H):
            sl = slice(h * CHUNK, (h + 1) * CHUNK)
            dya_h = dya[:, sl]
            du_parts.append(dya_h * vp[h])
            dvp = dya_h * u[:, sl]
            dbs_ref[h] += dvp
            dvp16 = dvp.astype(BF16)
            dws_ref[h] += lax.dot_general(dvp16, vn[:, sl].astype(BF16), (((1,), (1,)), ((), ())),
                                          preferred_element_type=F32)
            dvn_parts.append(lax.dot_general(ws_ref[h].astype(BF16), dvp16, (((0,), (0,)), ((), ())),
                                             preferred_element_type=F32))
        du = jnp.concatenate(du_parts, axis=1)
        dvn = jnp.concatenate(dvn_parts, axis=1)
        dlg_ref[...] += _colsum(dvn * vhat)
        dlb_ref[...] += _colsum(dvn)
        dvh = dvn * lg
        dv = rstd * (dvh - jnp.mean(dvh, axis=-1, keepdims=True) - vhat * jnp.mean(dvh * vhat, axis=-1, keepdims=True))
        dz = jnp.concatenate([du, dv], axis=1) * _dgelu(zv)
        dbin_ref[...] += _colsum(dz)
        dz_ref[...] = dz.astype(BF16)

    vec = pl.BlockSpec((1, W), lambda i: (0, 0))
    mat = pl.BlockSpec((H, CHUNK, CHUNK), lambda i: (0, 0, 0))
    return pl.pallas_call(
        body,
        out_shape=[SDS((M, ZW), BF16), SDS((1, W), F32), SDS((1, W), F32), SDS((H, CHUNK, CHUNK), F32),
                   SDS((H, CHUNK, CHUNK), F32), SDS((1, 2 * W), F32)],
        grid=(M // CHUNK,),
        in_specs=[pl.BlockSpec((CHUNK, 2 * W), lambda i: (i, 0)), pl.BlockSpec((CHUNK, W), lambda i: (i, 0)), vec, vec,
                  mat, pl.BlockSpec((CHUNK, H), lambda i: (0, 0))],
        out_specs=[pl.BlockSpec((CHUNK, 2 * W), lambda i: (i, 0)), vec, vec, mat, mat,
                   pl.BlockSpec((1, 2 * W), lambda i: (0, 0))],
        compiler_params=_params(("arbitrary",)), name=name)(z, dy, ln_g, ln_b, w_s, b_st)


def _halo_specs(tm, width, col, n_rows):
    per = tm // HALO
    last = n_rows // HALO - 1
    prev = pl.BlockSpec((HALO, width), lambda i: (jnp.maximum(i * per - 1, 0), col))
    nxt = pl.BlockSpec((HALO, width), lambda i: (jnp.minimum((i + 1) * per, last), col))
    return prev, nxt


def _edge_flags(i, tm, seg_t, m):
    r0 = i * tm
    has_prev = jnp.where((r0 == 0) | (r0 == seg_t), 0.0, 1.0)
    has_next = jnp.where((r0 + tm == seg_t) | (r0 + tm == m), 0.0, 1.0)
    return has_prev, has_next


def _glu(zz, wb):
    return zz[:, :wb] * _sigmoid(zz[:, wb:])


def _build_shifts(src_ref, sh_ref):
    n = src_ref.shape[0] - 8
    for r in range(1, 8):
        sh_ref[r - 1, pl.ds(0, n), :] = src_ref[pl.ds(r, n), :]


def _shifted(src_ref, sh_ref, off, r0, rc):
    a, r = divmod(off, 8)
    if r == 0:
        return src_ref[pl.ds(8 * a + r0, rc), :]
    return sh_ref[r - 1, pl.ds(8 * a + r0, rc), :]


def _conv_taps(src_ref, sh_ref, w_ref, first, tm, kw, flip=False):
    rc = 32
    parts = []
    for c in range(tm // rc):
        acc = None
        for k in range(kw):
            wk = w_ref[pl.ds(kw - 1 - k if flip else k, 1), :]
            term = _shifted(src_ref, sh_ref, first + k, c * rc, rc) * wk
            acc = term if acc is None else acc + term
        parts.append(acc)
    return jnp.concatenate(parts, axis=0)


def _conf_fwd(z, y, conv_w, conv_b, ln_g, ln_b, W, Wb, seg_t, name, carry=None):
    M = z.shape[0]
    tm = _row_tile(seg_t, M)
    kw = conv_w.shape[0]
    pad = (kw - 1) // 2
    col = (2 * W) // (2 * Wb)

    def body(zc_ref, zp_ref, zn_ref, y_hbm, cw_ref, cb_ref, lg_ref, lb_ref, o_ref, hc_ref, hs_ref, sh_ref):
        del y_hbm
        hp, hn = _edge_flags(pl.program_id(0), tm, seg_t, M)
        hs_ref[pl.ds(0, HALO), :] = _glu(zp_ref[...], Wb) * hp
        hs_ref[pl.ds(HALO, tm), :] = _glu(zc_ref[...], Wb)
        hs_ref[pl.ds(HALO + tm, HALO), :] = _glu(zn_ref[...], Wb) * hn
        _build_shifts(hs_ref, sh_ref)
        hc = _conv_taps(hs_ref, sh_ref, cw_ref, HALO - pad, tm, kw) + cb_ref[...]
        hc_ref[...] = hc
        mu = jnp.mean(hc, axis=-1, keepdims=True)
        c = hc - mu
        rstd = lax.rsqrt(jnp.mean(c * c, axis=-1, keepdims=True) + EPS)
        o_ref[...] = _silu(c * rstd * lg_ref[...] + lb_ref[...]).astype(BF16)

    prev, nxt = _halo_specs(tm, 2 * Wb, col, M)
    vec = pl.BlockSpec((1, Wb), lambda i: (0, 0))
    out, carried = _carry_call(
        body, carry, out_shape=[SDS(y.shape, BF16), SDS((M, Wb), F32)], grid=(M // tm,),
        in_specs=[pl.BlockSpec((tm, 2 * Wb), lambda i: (i, col)), prev, nxt, pl.BlockSpec(memory_space=pl.ANY),
                  pl.BlockSpec((kw, Wb), lambda i: (0, 0)), vec, vec, vec],
        out_specs=[pl.BlockSpec((tm, Wb), lambda i: (i, W // Wb)), pl.BlockSpec((tm, Wb), lambda i: (i, 0))],
        scratch_shapes=[pltpu.VMEM((tm + 2 * HALO, Wb), F32), pltpu.VMEM((7, tm + 2 * HALO, Wb), F32)],
        aliases={3: 0}, sem=("parallel",), name=name, ins=(z, z, z, y, conv_w, conv_b, ln_g, ln_b))
    return (out, carried) if carry else out


def _conf_bwd1(hc, dy, ln_g, ln_b, W, Wb, seg_t, name):
    M = hc.shape[0]
    tm = _row_tile(seg_t, M)

    def body(hc_ref, dy_ref, lg_ref, lb_ref, dhc_ref, dlg_ref, dlb_ref, dcb_ref):
        i = pl.program_id(0)

        @pl.when(i == 0)
        def _():
            for r in (dlg_ref, dlb_ref, dcb_ref):
                r[...] = jnp.zeros_like(r)

        hc = hc_ref[...]
        mu = jnp.mean(hc, axis=-1, keepdims=True)
        c = hc - mu
        rstd = lax.rsqrt(jnp.mean(c * c, axis=-1, keepdims=True) + EPS)
        hh = c * rstd
        lg = lg_ref[...]
        dhn = dy_ref[...] * _dsilu(hh * lg + lb_ref[...])
        dlg_ref[...] += _colsum(dhn * hh)
        dlb_ref[...] += _colsum(dhn)
        dhh = dhn * lg
        dhc = rstd * (dhh - jnp.mean(dhh, axis=-1, keepdims=True) - hh * jnp.mean(dhh * hh, axis=-1, keepdims=True))
        dcb_ref[...] += _colsum(dhc)
        dhc_ref[...] = dhc

    vec = pl.BlockSpec((1, Wb), lambda i: (0, 0))
    return pl.pallas_call(
        body, out_shape=[SDS((M, Wb), F32), SDS((1, Wb), F32), SDS((1, Wb), F32), SDS((1, Wb), F32)], grid=(M // tm,),
        in_specs=[pl.BlockSpec((tm, Wb), lambda i: (i, 0)), pl.BlockSpec((tm, Wb), lambda i: (i, W // Wb)), vec, vec],
        out_specs=[pl.BlockSpec((tm, Wb), lambda i: (i, 0)), vec, vec, vec],
        compiler_params=_params(("arbitrary",)), name=name)(hc, dy, ln_g, ln_b)


def _conf_bwd2(z, dhc, dz, conv_w, W, Wb, seg_t, name, carry=None):
    M = z.shape[0]
    tm = _row_tile(seg_t, M)
    kw = conv_w.shape[0]
    pad = (kw - 1) // 2
    col = (2 * W) // (2 * Wb)

    def body(zc_ref, zp_ref, zn_ref, dc_ref, dp_ref, dn_ref, dz_hbm, cw_ref, dz_ref, dcw_ref, dbin_ref, hs_ref, ds_ref,
             hsh_ref, dsh_ref):
        del dz_hbm
        i = pl.program_id(0)

        @pl.when(i == 0)
        def _():
            dcw_ref[...] = jnp.zeros_like(dcw_ref)
            dbin_ref[...] = jnp.zeros_like(dbin_ref)

        hp, hn = _edge_flags(i, tm, seg_t, M)
        zc = zc_ref[...]
        hs_ref[pl.ds(0, HALO), :] = _glu(zp_ref[...], Wb) * hp
        hs_ref[pl.ds(HALO, tm), :] = _glu(zc, Wb)
        hs_ref[pl.ds(HALO + tm, HALO), :] = _glu(zn_ref[...], Wb) * hn
        dcur = dc_ref[...]
        ds_ref[pl.ds(0, HALO), :] = dp_ref[...] * hp
        ds_ref[pl.ds(HALO, tm), :] = dcur
        ds_ref[pl.ds(HALO + tm, HALO), :] = dn_ref[...] * hn
        _build_shifts(ds_ref, dsh_ref)
        _build_shifts(hs_ref, hsh_ref)
        dh = _conv_taps(ds_ref, dsh_ref, cw_ref, HALO - pad, tm, kw, flip=True)
        for k in range(kw):
            dcw_ref[pl.ds(k, 1), :] += _colsum(dcur * _shifted(hs_ref, hsh_ref, HALO - pad + k, 0, tm))
        a, gt = zc[:, :Wb], zc[:, Wb:]
        s = _sigmoid(gt)
        dz = jnp.concatenate([dh * s, dh * a * s * (1.0 - s)], axis=1)
        dbin_ref[...] += _colsum(dz)
        dz_ref[...] = dz.astype(BF16)

    prev, nxt = _halo_specs(tm, 2 * Wb, col, M)
    dprev, dnxt = _halo_specs(tm, Wb, 0, M)
    out, carried = _carry_call(
        body, carry, out_shape=[SDS(dz.shape, BF16), SDS((kw, Wb), F32), SDS((1, 2 * Wb), F32)], grid=(M // tm,),
        in_specs=[pl.BlockSpec((tm, 2 * Wb), lambda i: (i, col)), prev, nxt,
                  pl.BlockSpec((tm, Wb), lambda i: (i, 0)), dprev, dnxt, pl.BlockSpec(memory_space=pl.ANY),
                  pl.BlockSpec((kw, Wb), lambda i: (0, 0))],
        out_specs=[pl.BlockSpec((tm, 2 * Wb), lambda i: (i, col)), pl.BlockSpec((kw, Wb), lambda i: (0, 0)),
                   pl.BlockSpec((1, 2 * Wb), lambda i: (0, 0))],
        scratch_shapes=[pltpu.VMEM((tm + 2 * HALO, Wb), F32), pltpu.VMEM((tm + 2 * HALO, Wb), F32),
                        pltpu.VMEM((7, tm + 2 * HALO, Wb), F32), pltpu.VMEM((7, tm + 2 * HALO, Wb), F32)],
        aliases={6: 0}, sem=("arbitrary",), name=name, ins=(z, z, z, dhc, dhc, dhc, dz, conv_w))
    return (out, carried) if carry else out


_TF = (1408, 512, 256, 128)
_RC = 16
_CG = 256


def _col_groups(width):
    return [(c0, min(_CG, width - c0)) for c0 in range(0, width, _CG)]


def _ffn_act_fwd(z, conv_w, conv_b, seg_t, name):
    M, F2 = z.shape
    Fd = F2 // 2
    tm = _row_tile(seg_t, M)
    tf = _tile(Fd, _TF)
    nf = Fd // tf
    per, last = tm // HALO, M // HALO - 1

    def body(g_ref, gp_ref, gn_ref, u_ref, cw_ref, cb_ref, o_ref, gc_ref, gs_ref):
        hp, hn = _edge_flags(pl.program_id(0), tm, seg_t, M)
        gs_ref[pl.ds(0, HALO), :] = gp_ref[...].astype(F32) * hp
        gs_ref[pl.ds(HALO, tm), :] = g_ref[...].astype(F32)
        gs_ref[pl.ds(HALO + tm, HALO), :] = gn_ref[...].astype(F32) * hn
        for c0, cw in _col_groups(tf):
            cs = pl.ds(c0, cw)
            w0, w1, w2, cb = cw_ref[pl.ds(0, 1), cs], cw_ref[pl.ds(1, 1), cs], cw_ref[pl.ds(2, 1), cs], cb_ref[:, cs]
            for r0 in range(0, tm, _RC):
                gc = (gs_ref[pl.ds(HALO - 1 + r0, _RC), cs] * w0 + gs_ref[pl.ds(HALO + r0, _RC), cs] * w1
                      + gs_ref[pl.ds(HALO + 1 + r0, _RC), cs] * w2 + cb)
                o_ref[pl.ds(r0, _RC), cs] = (_silu(gc) * u_ref[pl.ds(r0, _RC), cs].astype(F32)).astype(BF16)
                gc_ref[pl.ds(r0, _RC), cs] = gc.astype(BF16)

    return pl.pallas_call(
        body, out_shape=[SDS((M, Fd), BF16), SDS((M, Fd), BF16)], grid=(M // tm, nf),
        in_specs=[pl.BlockSpec((tm, tf), lambda i, j: (i, j)),
                  pl.BlockSpec((HALO, tf), lambda i, j: (jnp.maximum(i * per - 1, 0), j)),
                  pl.BlockSpec((HALO, tf), lambda i, j: (jnp.minimum((i + 1) * per, last), j)),
                  pl.BlockSpec((tm, tf), lambda i, j: (i, nf + j)),
                  pl.BlockSpec((3, tf), lambda i, j: (0, j)), pl.BlockSpec((1, tf), lambda i, j: (0, j))],
        out_specs=[pl.BlockSpec((tm, tf), lambda i, j: (i, j)), pl.BlockSpec((tm, tf), lambda i, j: (i, j))],
        scratch_shapes=[pltpu.VMEM((tm + 2 * HALO, tf), F32)],
        compiler_params=_params(("parallel", "parallel")), name=name)(z, z, z, z, conv_w, conv_b)


def _ffn_act_bwd(z, gc, da, conv_w, seg_t, name):
    M, F2 = z.shape
    Fd = F2 // 2
    tm = _row_tile(seg_t, M)
    tf = _tile(Fd, _TF)
    nf = Fd // tf
    per, last = tm // HALO, M // HALO - 1
    n_piece = tm // _RC

    def body(g_ref, c_ref, cp_ref, cn_ref, u_ref, up_ref, un_ref, a_ref, ap_ref, an_ref, cw_ref,
             dz_ref, dcw_ref, dcb_ref, ds_ref):
        i = pl.program_id(1)

        def tile():
            hp, hn = _edge_flags(i, tm, seg_t, M)

            @pl.when(i == 0)
            def _():
                dcw_ref[...] = jnp.zeros_like(dcw_ref)
                dcb_ref[...] = jnp.zeros_like(dcb_ref)

            def fold(v):
                return v[:8] + v[8:]

            for c0, cw in _col_groups(tf):
                cs = pl.ds(c0, cw)
                w0, w1, w2 = cw_ref[pl.ds(0, 1), cs], cw_ref[pl.ds(1, 1), cs], cw_ref[pl.ds(2, 1), cs]
                for ci in range(-1, n_piece + 1):
                    r0 = ci * _RC
                    if ci < 0:
                        gcv, ue, ae = cp_ref[:, cs], up_ref[:, cs], ap_ref[:, cs].astype(F32) * hp
                    elif ci == n_piece:
                        gcv, ue, ae = cn_ref[:, cs], un_ref[:, cs], an_ref[:, cs].astype(F32) * hn
                    else:
                        rows = pl.ds(r0, _RC)
                        gcv, ue, ae = c_ref[rows, cs], u_ref[rows, cs], a_ref[rows, cs].astype(F32)
                    gcv, ue = gcv.astype(F32), ue.astype(F32)
                    sg = _sigmoid(gcv)
                    t = ae * sg
                    ds_ref[pl.ds(HALO + r0, _RC), cs] = t * ue * (1.0 + gcv * (1.0 - sg))
                    if 0 <= ci < n_piece:
                        dz_ref[1, pl.ds(r0, _RC), cs] = (t * gcv).astype(BF16)
                acc = [jnp.zeros((8, cw), F32) for _ in range(4)]
                for r0 in range(0, tm, _RC):
                    b = HALO + r0
                    d = [ds_ref[pl.ds(b + 1 - k, _RC), cs] for k in range(3)]
                    dz_ref[0, pl.ds(r0, _RC), cs] = (d[0] * w0 + d[1] * w1 + d[2] * w2).astype(BF16)
                    gv = g_ref[pl.ds(r0, _RC), cs].astype(F32)
                    for k in range(3):
                        acc[k] = acc[k] + fold(d[k] * gv)
                    acc[3] = acc[3] + fold(d[1])
                for k in range(3):
                    dcw_ref[pl.ds(k, 1), cs] += _colsum(acc[k])
                dcb_ref[:, cs] += _colsum(acc[3])

        tile()

    def cur(off):
        return pl.BlockSpec((tm, tf), lambda j, i: (i, off + j))

    def prv(off):
        return pl.BlockSpec((HALO, tf), lambda j, i: (jnp.maximum(i * per - 1, 0), off + j))

    def nxt(off):
        return pl.BlockSpec((HALO, tf), lambda j, i: (jnp.minimum((i + 1) * per, last), off + j))

    return pl.pallas_call(
        body, out_shape=[SDS((2, M, Fd), BF16), SDS((3, Fd), F32), SDS((1, Fd), F32)], grid=(nf, M // tm),
        in_specs=[cur(0), cur(0), prv(0), nxt(0), cur(nf), prv(nf), nxt(nf), cur(0), prv(0), nxt(0),
                  pl.BlockSpec((3, tf), lambda j, i: (0, j))],
        out_specs=[pl.BlockSpec((2, tm, tf), lambda j, i: (0, i, j)),
                   pl.BlockSpec((3, tf), lambda j, i: (0, j)), pl.BlockSpec((1, tf), lambda j, i: (0, j))],
        scratch_shapes=[pltpu.VMEM((tm + 2 * HALO, tf), F32)],
        compiler_params=_params(("parallel", "arbitrary")), name=name)(
            z, gc, gc, gc, z, z, z, da, da, da, conv_w)


_LN2 = math.log(2.0)
_QSCALE = (NOPE + ROPE) ** -0.5 / _LN2


def _swap32(x):
    lane = lax.broadcasted_iota(jnp.int32, x.shape, 1)
    return jnp.where((lane % 64) < 32, pltpu.roll(x, 96, axis=1), pltpu.roll(x, 32, axis=1))


def _rms(x, g):
    r = lax.rsqrt(jnp.mean(x * x, axis=-1, keepdims=True) + EPS)
    return x * r * g


def _rms_bwd(x, g, dy):
    r = lax.rsqrt(jnp.mean(x * x, axis=-1, keepdims=True) + EPS)
    xh = x * r
    dxh = dy * g
    return r * (dxh - xh * jnp.mean(dxh * xh, axis=-1, keepdims=True)), _colsum(dy * xh)


def _mla_prep_fwd(z, gq, gkv, cos, sin, QL, KL, name):
    M, NZ = z.shape
    tm = _tile(M, (256, 128))

    def body(z_ref, gq_ref, gkv_ref, cos_ref, sin_ref, cq_ref, ckv_ref, kpe_ref):
        zv = z_ref[...]
        cq_ref[...] = _rms(zv[:, :QL], gq_ref[...]).astype(BF16)
        ckv_ref[...] = _rms(zv[:, QL:QL + KL], gkv_ref[...]).astype(BF16)
        kp = zv[:, QL + KL:]
        r = kp * cos_ref[...] + _swap32(kp) * sin_ref[...]
        lane = lax.broadcasted_iota(jnp.int32, r.shape, 1)
        kpe_ref[0] = jnp.where(lane < ROPE, r, 0.0).astype(BF16)
        kpe_ref[1] = jnp.where(lane >= ROPE, r, 0.0).astype(BF16)

    tab = pl.BlockSpec((tm, 128), lambda i: (i, 0))
    return pl.pallas_call(
        body, out_shape=[SDS((M, QL), BF16), SDS((M, KL), BF16), SDS((2, M, 128), BF16)], grid=(M // tm,),
        in_specs=[pl.BlockSpec((tm, NZ), lambda i: (i, 0)), pl.BlockSpec((1, QL), lambda i: (0, 0)),
                  pl.BlockSpec((1, KL), lambda i: (0, 0)), tab, tab],
        out_specs=[pl.BlockSpec((tm, QL), lambda i: (i, 0)), pl.BlockSpec((tm, KL), lambda i: (i, 0)),
                   pl.BlockSpec((2, tm, 128), lambda i: (0, i, 0))],
        compiler_params=_params(("parallel",)), name=name)(z, gq, gkv, cos, sin)


def _mla_prep_bwd(z, dcq, dckv, dkpe, gq, gkv, cos, sin, QL, KL, seg_t, name):
    M, NZ = z.shape
    H = dkpe.shape[0]
    tm = _row_tile(seg_t, M)
    nt = seg_t // tm

    def body(z_ref, dcq_ref, dckv_ref, dkpe_ref, gq_ref, gkv_ref, cos_ref, sin_ref, dz_ref, dgq_ref, dgkv_ref):
        i = pl.program_id(0)

        @pl.when(i == 0)
        def _():
            dgq_ref[...] = jnp.zeros_like(dgq_ref)
            dgkv_ref[...] = jnp.zeros_like(dgkv_ref)

        zv = z_ref[...]
        dyq = jnp.where(i < nt, dcq_ref[...], 0.0)
        dxq, dgq = _rms_bwd(zv[:, :QL], gq_ref[...], dyq)
        dxkv, dgkv = _rms_bwd(zv[:, QL:QL + KL], gkv_ref[...], dckv_ref[...])
        dgq_ref[...] += dgq
        dgkv_ref[...] += dgkv
        even = dkpe_ref[0]
        odd = dkpe_ref[1]
        for h in range(2, H, 2):
            even = even + dkpe_ref[h]
            odd = odd + dkpe_ref[h + 1]
        lane = lax.broadcasted_iota(jnp.int32, even.shape, 1)
        dr = jnp.where(lane < ROPE, even, odd)
        dkp = dr * cos_ref[...] - _swap32(dr) * sin_ref[...]
        dz_ref[...] = jnp.concatenate([dxq, dxkv, dkp], axis=1).astype(BF16)

    tab = pl.BlockSpec((tm, 128), lambda i: (i, 0))
    return pl.pallas_call(
        body, out_shape=[SDS((M, NZ), BF16), SDS((1, QL), F32), SDS((1, KL), F32)], grid=(M // tm,),
        in_specs=[pl.BlockSpec((tm, NZ), lambda i: (i, 0)),
                  pl.BlockSpec((tm, QL), lambda i: (jnp.minimum(i, nt - 1), 0)),
                  pl.BlockSpec((tm, KL), lambda i: (i, 0)), pl.BlockSpec((H, tm, 128), lambda i: (0, i, 0)),
                  pl.BlockSpec((1, QL), lambda i: (0, 0)), pl.BlockSpec((1, KL), lambda i: (0, 0)), tab, tab],
        out_specs=[pl.BlockSpec((tm, NZ), lambda i: (i, 0)), pl.BlockSpec((1, QL), lambda i: (0, 0)),
                   pl.BlockSpec((1, KL), lambda i: (0, 0))],
        compiler_params=_params(("arbitrary",)), name=name)(z, dcq, dckv, dkpe, gq, gkv, cos, sin)


def _qrope_fwd(q, cos, sin, HN, name):
    T, NQ = q.shape
    tm = _tile(T, (256, 128))

    def body(q_ref, cos_ref, sin_ref, o_ref):
        o_ref[:, :HN] = (q_ref[:, :HN] * _QSCALE).astype(BF16)
        for cb in range((NQ - HN) // 128):
            sl = slice(HN + cb * 128, HN + (cb + 1) * 128)
            xv = q_ref[:, sl]
            o_ref[:, sl] = ((xv * cos_ref[...] + _swap32(xv) * sin_ref[...]) * _QSCALE).astype(BF16)

    tab = pl.BlockSpec((tm, 128), lambda i: (i, 0))
    return pl.pallas_call(
        body, out_shape=SDS((T, NQ), BF16), grid=(T // tm,),
        in_specs=[pl.BlockSpec((tm, NQ), lambda i: (i, 0)), tab, tab],
        out_specs=pl.BlockSpec((tm, NQ), lambda i: (i, 0)),
        compiler_params=_params(("parallel",)), name=name)(q, cos, sin)


def _qrope_bwd(dqpe, dqa, cos, sin, HN, name):
    T, HW = dqpe.shape
    HR = HW // 2
    tm = _tile(T, (256, 128))

    def body(d_ref, dqa_hbm, cos_ref, sin_ref, o_ref):
        del dqa_hbm
        for pr in range(HR // 128):
            dr = d_ref[:, 2 * pr * 128:(2 * pr + 1) * 128] + d_ref[:, (2 * pr + 1) * 128:(2 * pr + 2) * 128]
            o_ref[:, pr * 128:(pr + 1) * 128] = (dr * cos_ref[...] - _swap32(dr) * sin_ref[...]).astype(BF16)

    tab = pl.BlockSpec((tm, 128), lambda i: (i, 0))
    return pl.pallas_call(
        body, out_shape=SDS(dqa.shape, BF16), grid=(T // tm,),
        in_specs=[pl.BlockSpec((tm, HW), lambda i: (i, 0)), pl.BlockSpec(memory_space=pl.ANY), tab, tab],
        out_specs=pl.BlockSpec((tm, HR), lambda i: (i, HN // HR)),
        input_output_aliases={1: 0}, compiler_params=_params(("parallel",)), name=name)(dqpe, dqa, cos, sin)


_ATT_SUB = 8
_ATT_SUB_B = 4
_NT = (((1,), (1,)), ((), ()))
_TN = (((0,), (0,)), ((), ()))


def _attn_fwd(qa, kv, kpe, T, H, name, carry=None):
    M = kv.shape[0]
    tq = _tile(T, (2048, 1024, 512, 256, 128))

    def body(qn_ref, qp_ref, kv_ref, kpe_ref, o_ref, lse_ref, kc_ref):
        @pl.when(pl.program_id(1) == 0)
        def _():
            kc_ref[:, :NOPE] = kv_ref[:, :NOPE]
            kc_ref[:, NOPE:] = kpe_ref[0]

        rs = tq // _ATT_SUB
        outs, lses = [], []
        for u in range(_ATT_SUB):
            rows = pl.ds(u * rs, rs)
            qc = jnp.concatenate([qn_ref[rows, :], qp_ref[rows, :]], axis=1)
            s = lax.dot_general(qc, kc_ref[...], _NT, preferred_element_type=F32)
            m = jnp.max(s, axis=-1, keepdims=True)
            p = jnp.exp2(s - m)
            l = jnp.sum(p, axis=-1, keepdims=True)
            o = jnp.dot(p.astype(BF16), kv_ref[:, NOPE:], preferred_element_type=F32)
            outs.append((o / l).astype(BF16))
            lses.append(jnp.broadcast_to(m + jnp.log2(l), (rs, 128)))
        o_ref[...] = jnp.concatenate(outs, axis=0)
        lse_ref[...] = jnp.concatenate(lses, axis=0)

    return _carry_call(
        body, carry, out_shape=[SDS((T, H * VDIM), BF16), SDS((T, H * 128), F32)], grid=(H, T // tq),
        in_specs=[pl.BlockSpec((tq, NOPE), lambda h, i: (i, h)), pl.BlockSpec((tq, 128), lambda h, i: (i, H + h // 2)),
                  pl.BlockSpec((M, NOPE + VDIM), lambda h, i: (0, h)), pl.BlockSpec((1, M, 128), lambda h, i: (h % 2, 0, 0))],
        out_specs=[pl.BlockSpec((tq, VDIM), lambda h, i: (i, h)), pl.BlockSpec((tq, 128), lambda h, i: (i, h))],
        scratch_shapes=[pltpu.VMEM((M, NOPE + 128), BF16)],
        sem=("parallel", "arbitrary"), name=name, ins=(qa, qa, kv, kpe))


def _attn_bwd(qa, kv, kpe, do, o, lse, T, H, name, carry=None):
    M = kv.shape[0]
    tq = _tile(T, (1024, 512, 256, 128))
    nq = T // tq
    scale = (NOPE + ROPE) ** -0.5

    def body(qn_ref, qp_ref, kv_ref, kpe_ref, do_ref, o_ref, lse_ref, dqa_ref, dqpe_ref, dkv_ref, dkpe_ref, kc_ref,
             dk_acc, dv_acc):
        i = pl.program_id(1)

        @pl.when(i == 0)
        def _():
            kc_ref[:, :NOPE] = kv_ref[:, :NOPE]
            kc_ref[:, NOPE:] = kpe_ref[0]
            dk_acc[...] = jnp.zeros_like(dk_acc)
            dv_acc[...] = jnp.zeros_like(dv_acc)

        rs = tq // _ATT_SUB_B
        p16s, ds16s = [], []
        for u in range(_ATT_SUB_B):
            rows = pl.ds(u * rs, rs)
            qc = jnp.concatenate([qn_ref[rows, :], qp_ref[rows, :]], axis=1)
            dov = do_ref[rows, :]
            s = lax.dot_general(qc, kc_ref[...], _NT, preferred_element_type=F32)
            p = jnp.exp2(s - lse_ref[rows, 0:1])
            dp = lax.dot_general(dov, kv_ref[:, NOPE:], _NT, preferred_element_type=F32)
            delta = jnp.sum(dov.astype(F32) * o_ref[rows, :].astype(F32), axis=-1, keepdims=True)
            ds16s.append((p * (dp - delta)).astype(BF16))
            p16s.append(p.astype(BF16))
        p16 = jnp.concatenate(p16s, axis=0)
        ds16 = jnp.concatenate(ds16s, axis=0)
        qc = jnp.concatenate([qn_ref[...], qp_ref[...]], axis=1)
        dq = jnp.dot(ds16, kc_ref[...], preferred_element_type=F32) * scale
        dqa_ref[...] = dq[:, :NOPE].astype(BF16)
        dqpe_ref[...] = dq[:, NOPE:]
        dv_acc[...] += lax.dot_general(p16, do_ref[...], _TN, preferred_element_type=F32)
        dk_acc[...] += lax.dot_general(ds16, qc, _TN, preferred_element_type=F32)

        @pl.when(i == nq - 1)
        def _():
            dkv_ref[:, :NOPE] = (dk_acc[:, :NOPE] * _LN2).astype(BF16)
            dkv_ref[:, NOPE:] = dv_acc[...].astype(BF16)
            dkpe_ref[0] = dk_acc[:, NOPE:] * _LN2

    return _carry_call(
        body, carry,
        out_shape=[SDS((T, H * (NOPE + ROPE)), BF16), SDS((T, H * 128), F32), SDS((M, H * (NOPE + VDIM)), BF16),
                   SDS((H, M, 128), F32)],
        grid=(H, nq),
        in_specs=[pl.BlockSpec((tq, NOPE), lambda h, i: (i, h)), pl.BlockSpec((tq, 128), lambda h, i: (i, H + h // 2)),
                  pl.BlockSpec((M, NOPE + VDIM), lambda h, i: (0, h)), pl.BlockSpec((1, M, 128), lambda h, i: (h % 2, 0, 0)),
                  pl.BlockSpec((tq, VDIM), lambda h, i: (i, h)), pl.BlockSpec((tq, VDIM), lambda h, i: (i, h)),
                  pl.BlockSpec((tq, 128), lambda h, i: (i, h))],
        out_specs=[pl.BlockSpec((tq, NOPE), lambda h, i: (i, h)), pl.BlockSpec((tq, 128), lambda h, i: (i, h)),
                   pl.BlockSpec((M, NOPE + VDIM), lambda h, i: (0, h)), pl.BlockSpec((1, M, 128), lambda h, i: (h, 0, 0))],
        scratch_shapes=[pltpu.VMEM((M, NOPE + 128), BF16), pltpu.VMEM((M, NOPE + 128), F32), pltpu.VMEM((M, VDIM), F32)],
        sem=("parallel", "arbitrary"), name=name, ins=(qa, qa, kv, kpe, do, o, lse))


def _adamw(w, m, v, name, g=None, recv=None, carry=None):
    R, C = w.shape
    summed = recv is not None
    n_recv = len(recv) if summed else 1
    runs = [r.shape[1] for r in recv] if summed else [R]
    tr = math.gcd(*runs)
    for cand in (1024, 512, 256, 128, 64, 32, 16, 8):
        if tr % cand == 0 and cand * C <= 131072:
            tr = cand
            break
    first = [sum(runs[:r]) // tr for r in range(n_recv + 1)]
    c1 = 1.0 - ADAM_B1 ** ADAM_STEP
    c2 = 1.0 - ADAM_B2 ** ADAM_STEP

    def update(gv, w_ref, m_ref, v_ref, d_ref, nm_ref, nv_ref):
        mn = ADAM_B1 * m_ref[...] + (1.0 - ADAM_B1) * gv
        vn = ADAM_B2 * v_ref[...] + (1.0 - ADAM_B2) * (gv * gv)
        nm_ref[...] = mn
        nv_ref[...] = vn
        d_ref[...] = -ADAM_LR * ((mn / c1) / (jnp.sqrt(vn / c2) + ADAM_EPS) + ADAM_WD * w_ref[...])

    def body(*refs):
        w_ref, m_ref, v_ref = refs[:3]
        g_refs = refs[3:3 + n_recv]
        outs = refs[3 + n_recv:]
        if not summed:
            update(g_refs[0][...], w_ref, m_ref, v_ref, *outs)
            return
        i = pl.program_id(0)
        for r in range(n_recv):
            @pl.when((i >= first[r]) & (i < first[r + 1]))
            def _():
                gv = g_refs[r][0].astype(F32)
                for d in range(1, N_DEV):
                    gv = gv + g_refs[r][d].astype(F32)
                outs[0][...] = gv
                update(gv, w_ref, m_ref, v_ref, *outs[1:])

    blk = pl.BlockSpec((tr, C), lambda i: (i, 0))
    if summed:
        g_specs = [pl.BlockSpec((N_DEV, tr, C), functools.partial(
            lambda i, lo, n: (0, jnp.clip(i - lo, 0, n - 1), 0), lo=first[r], n=first[r + 1] - first[r]))
                   for r in range(n_recv)]
    else:
        g_specs = [blk]
    n_out = 4 if summed else 3
    out, carried = _carry_call(
        body, carry, out_shape=[SDS((R, C), F32)] * n_out, grid=(R // tr,), in_specs=[blk, blk, blk] + g_specs,
        out_specs=[blk] * n_out, scratch_shapes=[], sem=("parallel",), name=name,
        ins=(w, m, v, *(recv if summed else [g])))
    return (out, carried) if carry else out


WEIGHTS = ['c_ctx', 'norm1_g', 'norm2_g', 'w_ada', 'b_ada', 'ab_w_in', 'ab_b_in', 'a_ln_g', 'a_ln_b', 'a_w_s', 'a_b_s',
           'b_conv_w', 'b_conv_b', 'b_ln_g', 'b_ln_b', 'ab_w_out', 'mla_w_in', 'mla_q_norm_g', 'mla_w_uq',
           'mla_kv_norm_g', 'mla_w_ukv', 'mla_w_o', 'ffn_w_up', 'ffn_conv_w', 'ffn_conv_b', 'ffn_w_down', 'final_norm_g']


def _pack(parts):
    flat = jnp.concatenate([p.reshape(-1).astype(F32) for p in parts])
    n = flat.shape[0]
    unit = 65536 if n > 65536 else 1024
    n_pad = -(-n // unit) * unit
    return jnp.pad(flat, (0, n_pad - n)).reshape(n_pad // 128, 128)


def _unpack(flat, like):
    out, off = [], 0
    for shp in like:
        n = math.prod(shp)
        out.append(flat[..., off:off + n].reshape(flat.shape[:-1] + tuple(shp)))
        off += n
    return out


def _rope_tables(T, Tc):
    rows = T // GRID_W
    row = jnp.repeat(jnp.arange(rows, dtype=F32), GRID_W)
    col = jnp.tile(jnp.arange(GRID_W, dtype=F32), rows)
    n_freq = ROPE // 4
    inv = ROPE_THETA ** (-jnp.arange(n_freq, dtype=F32) / n_freq)
    ang = jnp.concatenate([row[:, None] * inv, col[:, None] * inv], axis=-1)
    cos, sin = jnp.cos(ang), jnp.sin(ang)
    cos = jnp.tile(cos, (1, 128 // (ROPE // 2)))
    sin = jnp.tile(jnp.concatenate([-sin, sin], axis=1), (1, 128 // ROPE))
    return (jnp.concatenate([cos, jnp.ones((Tc, 128), F32)], axis=0),
            jnp.concatenate([sin, jnp.zeros((Tc, 128), F32)], axis=0))


def _step(a):
    ax, ay, ac = lax.axis_index("x"), lax.axis_index("y"), lax.axis_index("c")
    me = 4 * ax + 2 * ay + ac
    T, D = a['x'].shape[1:]
    Tc = a['ctx'].shape[1]
    M = T + Tc
    W, Wb = a['a_ln_g'].shape[1], a['b_ln_g'].shape[1]
    assert W == Wb and T % Tc == 0
    Fd = a['ffn_conv_b'].shape[1]
    QL, KL = a['mla_q_norm_g'].shape[1] * N_DEV, a['mla_kv_norm_g'].shape[1] * N_DEV
    H = a['mla_w_ukv'].shape[2] * N_DEV // (NOPE + VDIM)
    HN, HR = H * NOPE, H * ROPE
    kw = a['b_conv_w'].shape[1]
    NA = a['w_ada'].shape[2]
    bf = lambda t: t.astype(BF16)

    small_shapes = [(D,), (kw, Wb // N_DEV), (2, 3, Fd // N_DEV), (QL // N_DEV,), (KL // N_DEV,)]
    g_small = _all_gather(_pack([a['c'][0], a['b_conv_w'][0], a['ffn_conv_w'], a['mla_q_norm_g'][0], a['mla_kv_norm_g'][0]]),
                          "ag_small")
    c_all, bcw, fcw, gq, gkv = _unpack(g_small.reshape(N_DEV, -1), small_shapes)
    bcw = jnp.transpose(bcw, (1, 0, 2)).reshape(kw, Wb)
    fcw = jnp.transpose(fcw, (1, 2, 0, 3)).reshape(2, 3, Fd)
    gq, gkv = gq.reshape(1, QL), gkv.reshape(1, KL)

    a16 = jnp.concatenate([c_all, a['c_ctx'][None], jnp.zeros((N_DEV - 1, D), F32)], axis=0)
    b_loc = lax.dynamic_slice(a['b_ada'], (0, me * NA), (2, NA))
    mods = [_mm(a16, a['w_ada'][l], mode="nn", out_dtype=F32, name=f"ada_fwd{l}", bias=b_loc[l:l + 1], a_silu=True)
            for l in range(2)]
    gm = _all_gather(jnp.concatenate(mods, axis=0), "ag_mod").reshape(N_DEV, 2, 2 * N_DEV, NA)
    gm = jnp.transpose(gm, (1, 2, 0, 3)).reshape(2, 2 * N_DEV, 6 * D)
    mod_lat = [lax.dynamic_slice(gm[l], (me, 0), (1, 6 * D)).reshape(6, 1, 1, D) for l in range(2)]
    mod_ctx = [gm[l][N_DEV].reshape(6, 1, 1, D) for l in range(2)]

    def mod(l, k, both):
        return jnp.concatenate([mod_lat[l][k], mod_ctx[l][k]], axis=0) if both else mod_lat[l][k]

    def from_cols(g):
        return jnp.transpose(g, (1, 0, 2)).reshape(g.shape[1], -1)

    def from_rows(g):
        return g.reshape(-1, g.shape[2])

    def ag(x):
        return (x, False)

    def a2a(x):
        return (x, True)

    cos, sin = _rope_tables(T, Tc)
    n1g, n2g = a['norm1_g'], a['norm2_g']
    a_bst = a['a_b_s'][0].T
    mm = functools.partial(_mm)
    up_sh, dn_sh = bf(a['ffn_w_up']), bf(a['ffn_w_down'])

    x0 = jnp.concatenate([a['x'][0], a['ctx'][0]], axis=0)
    h1, (w_abin,) = _normmod_fwd(x0, n1g[0:1], mod(0, 0, True), mod(0, 1, True), T, "l0_norm1",
                                 carry=[ag(bf(a['ab_w_in'][0]))])
    s1, s2, s3 = 3 * D // 16, 6 * D // 16, 11 * D // 16
    z, (g_about, g_up0a) = mm(h1, w_abin, mode="nn", out_dtype=F32, name="l0_ab_in", bias=a['ab_b_in'], b_dev=True,
                              carry=[ag(bf(a['ab_w_out'][0])), ag(up_sh[0][:s1])])
    w_about = from_rows(g_about)
    y, (g_up0b,) = _gmlp_fwd(z, a['a_ln_g'], a['a_ln_b'], a['a_w_s'][0], a_bst, W, "l0_gmlp",
                             carry=[ag(up_sh[0][s1:s2])])
    (y, hc_b), (g_up0c,) = _conf_fwd(z, y, bcw, a['b_conv_b'], a['b_ln_g'], a['b_ln_b'], W, Wb, T, "l0_conf",
                                     carry=[ag(up_sh[0][s2:s3])])
    (x1, o1, h2), (g_up0d,) = mm(y, w_about, mode="nn", out_dtype=F32, name="l0_ab_out", res=x0, gate=mod(0, 2, True),
                                 seg_t=T, tiles=(M // 8, D, None), norm=(n2g[0:1], mod(0, 3, True), mod(0, 4, True)),
                                 carry=[ag(up_sh[0][s3:])])
    w_up = [jnp.concatenate([g_up0a, g_up0b, g_up0c, g_up0d], axis=1), None]
    z2, (g_dn0,) = mm(h2, w_up[0], mode="nn", out_dtype=BF16, name="l0_up", b_dev=True, carry=[ag(dn_sh[0])])
    w_dn = [from_rows(g_dn0), None]
    a2, gc2 = _ffn_act_fwd(z2, fcw[0], a['ffn_conv_b'][0:1], T, "l0_act")
    (x2, o2), (g_in, g_uq) = mm(a2, w_dn[0], mode="nn", out_dtype=F32, name="l0_down", res=x1, gate=mod(0, 5, True),
                                seg_t=T, tiles=(M // 8, None, Fd // 2),
                                carry=[ag(bf(a['mla_w_in'][0])), ag(bf(a['mla_w_uq'][0]))])
    w_in = from_rows(g_in)
    w_in = jnp.concatenate([w_in, w_in[:, QL + KL:]], axis=1)
    w_uq = from_cols(g_uq).reshape(QL, H, NOPE + ROPE)
    w_uq = jnp.concatenate([w_uq[:, :, :NOPE].reshape(QL, HN), w_uq[:, :, NOPE:].reshape(QL, HR)], axis=1)

    h3 = _normmod_fwd(x2, n1g[1:2], mod(1, 0, True), mod(1, 1, True), T, "l1_norm1")
    z3, (g_ukv,) = mm(h3, w_in, mode="nn", out_dtype=F32, name="l1_mla_in", carry=[ag(bf(a['mla_w_ukv'][0]))])
    w_ukv = g_ukv
    cqn, ckvn, kpe = _mla_prep_fwd(z3, gq, gkv, cos, sin, QL, KL, "l1_prep")
    q, (g_wo,) = mm(cqn, w_uq, mode="nn", out_dtype=F32, name="l1_uq", rows=T, carry=[ag(bf(a['mla_w_o'][0]))])
    w_o = from_rows(g_wo)
    kv = mm(ckvn, w_ukv, mode="nn", out_dtype=BF16, name="l1_ukv", b_dev=True)
    qa = _qrope_fwd(q, cos, sin, HN, "l1_qrope")
    (o_att, lse), (g_up1,) = _attn_fwd(qa, kv, kpe, T, H, "l1_attn", carry=[ag(up_sh[1])])
    w_up[1] = g_up1
    x3, o3, h4 = mm(o_att, w_o, mode="nn", out_dtype=F32, name="l1_wo", res=x2, gate=mod(1, 2, False), seg_t=T,
                    tiles=(T // 8, D, None), norm=(n2g[1:2], mod(1, 3, False), mod(1, 4, False)))
    z4, (g_dn1,) = mm(h4, w_up[1], mode="nn", out_dtype=BF16, name="l1_up", b_dev=True, carry=[ag(dn_sh[1])])
    w_dn[1] = from_rows(g_dn1)
    a4, gc4 = _ffn_act_fwd(z4, fcw[1], a['ffn_conv_b'][1:2], T, "l1_act")
    x4, o4 = mm(a4, w_dn[1], mode="nn", out_dtype=F32, name="l1_down", res=x3, gate=mod(1, 5, False), seg_t=T,
                tiles=(T // 8, None, Fd // 2))

    dx4, loss_cols, d_fng, do4, dg2_1 = _final(x4, a['final_norm_g'][None], a['loss_target'][0], o4, mod(1, 5, False),
                                               "final")
    loss = lax.psum(jnp.sum(loss_cols), ("x", "y", "c"))

    def cols(dw):
        k, n = dw.shape
        return jnp.transpose(dw.reshape(k, N_DEV, n // N_DEV), (1, 0, 2))

    def rows(dw):
        return dw.reshape(N_DEV, dw.shape[0] // N_DEV, dw.shape[1])

    da4 = mm(do4, w_dn[1], mode="nt", out_dtype=BF16, name="l1_down_dx")
    dw_dn1 = mm(a4, do4, mode="tn", out_dtype=BF16, name="l1_down_dw")
    dz4, dfcw1, dfcb1 = _ffn_act_bwd(z4, gc4, da4, fcw[1], T, "l1_act_bwd")
    dw_up1, (r_dn1,) = mm(h4, dz4, mode="tn", out_dtype=BF16, name="l1_up_dw", out_dev=True, halves=True,
                          carry=[a2a(rows(dw_dn1))])
    dh4 = mm(dz4, w_up[1], mode="nt", out_dtype=F32, name="l1_up_dx", b_dev=True, halves=True, pair=True)
    dx3, dn2g1, dsh2_1, dsc2_1, do3, dg1_1 = _normmod_bwd(x3, n2g[1:2], mod(1, 4, False), dh4, dx4, T, "l1_norm2_bwd",
                                                         o_prev=o3, gate_prev=mod(1, 2, False))
    d_oatt = mm(do3, w_o, mode="nt", out_dtype=BF16, name="l1_wo_dx")
    dw_o = mm(o_att, do3, mode="tn", out_dtype=BF16, name="l1_wo_dw")
    (dqa, dqpe, dkv, dkpe), (r_up1, r_wo) = _attn_bwd(qa, kv, kpe, d_oatt, o_att, lse, T, H, "l1_attn_bwd",
                                                      carry=[a2a(dw_up1), a2a(rows(dw_o))])
    dqa = _qrope_bwd(dqpe, dqa, cos, sin, HN, "l1_qrope_bwd")
    dcq = mm(dqa, w_uq, mode="nt", out_dtype=F32, name="l1_uq_dx")
    dw_uq = mm(cqn, dqa, mode="tn", out_dtype=BF16, name="l1_uq_dw", rows=T)
    dw_uq = jnp.concatenate([dw_uq[:, :HN].reshape(QL, H, NOPE), dw_uq[:, HN:].reshape(QL, H, ROPE)], axis=2)
    dw_uq = dw_uq.reshape(QL, H * (NOPE + ROPE))
    dckv = mm(dkv, w_ukv, mode="nt", out_dtype=F32, name="l1_ukv_dx", b_dev=True, pair=True)
    dw_ukv = mm(ckvn, dkv, mode="tn", out_dtype=BF16, name="l1_ukv_dw", out_dev=True)
    dz3, dgq, dgkv = _mla_prep_bwd(z3, dcq, dckv, dkpe, gq, gkv, cos, sin, QL, KL, T, "l1_prep_bwd")
    dh3 = mm(dz3, w_in, mode="nt", out_dtype=F32, name="l1_mla_in_dx")
    dw_in = mm(h3, dz3, mode="tn", out_dtype=BF16, name="l1_mla_in_dw").astype(F32)
    dw_in = jnp.concatenate([dw_in[:, :QL + KL], dw_in[:, QL + KL:QL + KL + ROPE] + dw_in[:, QL + KL + ROPE:QL + KL + 2 * ROPE]],
                            axis=1).astype(BF16)
    dx2, dn1g1, dsh1_1, dsc1_1, do2, dg2_0 = _normmod_bwd(x2, n1g[1:2], mod(1, 1, True), dh3, dx3, T, "l1_norm1_bwd",
                                                         o_prev=o2, gate_prev=mod(0, 5, True))
    da2, (r_uq, r_ukv) = mm(do2, w_dn[0], mode="nt", out_dtype=BF16, name="l0_down_dx",
                            carry=[a2a(cols(dw_uq)), a2a(dw_ukv)])
    dw_dn0, (r_in,) = mm(a2, do2, mode="tn", out_dtype=BF16, name="l0_down_dw", carry=[a2a(rows(dw_in))])
    dz2, dfcw0, dfcb0 = _ffn_act_bwd(z2, gc2, da2, fcw[0], T, "l0_act_bwd")
    dw_up0, (r_dn0,) = mm(h2, dz2, mode="tn", out_dtype=BF16, name="l0_up_dw", out_dev=True, halves=True,
                          carry=[a2a(rows(dw_dn0))])
    dh2, (r_up0a,) = mm(dz2, w_up[0], mode="nt", out_dtype=F32, name="l0_up_dx", b_dev=True, halves=True, pair=True,
                        carry=[a2a(dw_up0[:, :D // 2])])
    dx1, dn2g0, dsh2_0, dsc2_0, do1, dg1_0 = _normmod_bwd(x1, n2g[0:1], mod(0, 4, True), dh2, dx2, T, "l0_norm2_bwd",
                                                         o_prev=o1, gate_prev=mod(0, 2, True))
    dy = mm(do1, w_about, mode="nt", out_dtype=F32, name="l0_ab_out_dx")
    s_up = 13 * D // 16
    dw_about, (r_up0c,) = mm(y, do1, mode="tn", out_dtype=BF16, name="l0_ab_out_dw", carry=[a2a(dw_up0[:, s_up:])])
    dz, dlag, dlab, dws, dbs, dbin_a = _gmlp_bwd(z, dy, a['a_ln_g'], a['a_ln_b'], a['a_w_s'][0], a_bst, W, "l0_gmlp_bwd")
    dhc, dlbg, dlbb, dbcb = _conf_bwd1(hc_b, dy, a['b_ln_g'], a['b_ln_b'], W, Wb, T, "l0_conf_bwd1")
    (dz, dbcw, dbin_b), (r_up0b,) = _conf_bwd2(z, dhc, dz, bcw, W, Wb, T, "l0_conf_bwd2",
                                               carry=[a2a(dw_up0[:, D // 2:s_up])])
    dh1, (r_about,) = mm(dz, w_abin, mode="nt", out_dtype=F32, name="l0_ab_in_dx", b_dev=True, pair=True,
                         carry=[a2a(rows(dw_about))])
    dw_abin_a = mm(h1, dz, mode="tn", out_dtype=BF16, name="l0_ab_in_dw_a", out_dev=True, p_range=(0, D // 2))
    dw_abin_b, (r_abin_a,) = mm(h1, dz, mode="tn", out_dtype=BF16, name="l0_ab_in_dw_b", out_dev=True,
                                p_range=(D // 2, D // 2), carry=[a2a(dw_abin_a)])
    dx0, dn1g0, dsh1_0, dsc1_0 = _normmod_bwd(x0, n1g[0:1], mod(0, 1, True), dh1, dx1, T, "l0_norm1_bwd")

    zero = jnp.zeros((D,), F32)
    dmod = jnp.stack([
        jnp.stack([jnp.stack([dsh1_0[0, 0], dsc1_0[0, 0], dg1_0[0, 0], dsh2_0[0, 0], dsc2_0[0, 0], dg2_0[0, 0]]),
                   jnp.stack([dsh1_0[1, 0], dsc1_0[1, 0], dg1_0[1, 0], dsh2_0[1, 0], dsc2_0[1, 0], dg2_0[1, 0]])]),
        jnp.stack([jnp.stack([dsh1_1[0, 0], dsc1_1[0, 0], dg1_1[0, 0], dsh2_1[0, 0], dsc2_1[0, 0], dg2_1[0, 0]]),
                   jnp.stack([dsh1_1[1, 0], dsc1_1[1, 0], zero, zero, zero, zero])])])
    small = {
        'norm1_g': jnp.concatenate([dn1g0, dn1g1], axis=0), 'norm2_g': jnp.concatenate([dn2g0, dn2g1], axis=0),
        'ab_b_in': jnp.concatenate([dbin_a, dbin_b], axis=1), 'a_ln_g': dlag, 'a_ln_b': dlab, 'a_w_s': dws[None],
        'a_b_s': jnp.sum(dbs, axis=-1)[None], 'b_conv_w': dbcw, 'b_conv_b': dbcb, 'b_ln_g': dlbg, 'b_ln_b': dlbb,
        'mla_q_norm_g': dgq, 'mla_kv_norm_g': dgkv, 'ffn_conv_w': jnp.stack([dfcw0, dfcw1]),
        'ffn_conv_b': jnp.concatenate([dfcb0, dfcb1], axis=0), 'final_norm_g': d_fng[0],
    }
    names = list(small)
    up2d = (2 * D, a['ffn_w_up'].shape[2])
    res_up, (g2,) = _adamw(a['ffn_w_up'].reshape(up2d), a['m_ffn_w_up'].reshape(up2d), a['v_ffn_w_up'].reshape(up2d),
                           "adamw_ffn_w_up", recv=[r_up0a, r_up0b, r_up0c, r_up1],
                           carry=[ag(_pack([dmod] + [small[n] for n in names]))])
    red = _sum_lead(g2, "sum_small_grads").reshape(-1)
    red = dict(zip(names, _unpack(red, [(2, 2, 6, D)] + [small[n].shape for n in names])[1:]))
    dmod_all = g2.reshape(N_DEV, -1)[:, :2 * 2 * 6 * D].reshape(N_DEV, 2, 2, 6 * D)

    a16g = jnp.concatenate([c_all, jnp.tile(a['c_ctx'][None], (N_DEV, 1))], axis=0)
    dm_loc = lax.dynamic_slice(dmod_all, (0, 0, 0, me * NA), (N_DEV, 2, 2, NA))
    g_wada, cpart = [], []
    for l in range(2):
        dm16 = jnp.concatenate([dm_loc[:, l, 0], dm_loc[:, l, 1]], axis=0)
        g_wada.append(mm(a16g, dm16, mode="tn", out_dtype=F32, name=f"ada_dw{l}", a_silu=True))
        cpart.append(mm(dm_loc[:, l, 1], a['w_ada'][l], mode="nt", out_dtype=F32, name=f"ada_dc{l}"))
    g_bada = _sum_lead(jnp.transpose(dmod_all, (0, 2, 1, 3)).reshape(2 * N_DEV, 2 * 6 * D // 128, 128), "sum_b_ada")
    g_cc = _all_gather(jnp.concatenate(cpart, axis=0), "ag_c_ctx")
    g_cc = _sum_lead(g_cc.reshape(2 * N_DEV * N_DEV, D // 128, 128), "sum_c_ctx").reshape(D)
    grads = {
        'c_ctx': g_cc * _dsilu(a['c_ctx']), 'w_ada': jnp.stack(g_wada), 'b_ada': g_bada.reshape(2, 6 * D),
        'b_conv_w': lax.dynamic_slice(red['b_conv_w'], (0, me * (Wb // N_DEV)), (kw, Wb // N_DEV))[None],
        'ffn_conv_w': lax.dynamic_slice(red['ffn_conv_w'], (0, 0, me * (Fd // N_DEV)), (2, 3, Fd // N_DEV)),
        'mla_q_norm_g': lax.dynamic_slice(red['mla_q_norm_g'], (0, me * (QL // N_DEV)), (1, QL // N_DEV)),
        'mla_kv_norm_g': lax.dynamic_slice(red['mla_kv_norm_g'], (0, me * (KL // N_DEV)), (1, KL // N_DEV)),
    }
    for n in names:
        if n not in grads:
            grads[n] = red[n].reshape(a[n].shape)

    recvs = {'ab_w_out': [r_about], 'mla_w_in': [r_in], 'mla_w_uq': [r_uq], 'mla_w_ukv': [r_ukv],
             'mla_w_o': [r_wo], 'ffn_w_down': [r_dn0, r_dn1]}
    out = {}
    for n in WEIGHTS:
        shp = a[n].shape
        w2 = a[n].reshape(-1, shp[-1])
        m2, v2 = a['m_' + n].reshape(w2.shape), a['v_' + n].reshape(w2.shape)
        if n == 'ffn_w_up':
            res = res_up
        elif n in recvs:
            res = _adamw(w2, m2, v2, "adamw_" + n, recv=recvs[n])
        elif n == 'w_ada':
            g2d = grads[n].reshape(w2.shape)
            res, (r_abin_b,) = _adamw(w2, m2, v2, "adamw_" + n, g=g2d, carry=[a2a(dw_abin_b)])
            res = (g2d,) + tuple(res)
            recvs['ab_w_in'] = [r_abin_a, r_abin_b]
        else:
            g2d = grads[n].reshape(w2.shape)
            res = (g2d,) + tuple(_adamw(w2, m2, v2, "adamw_" + n, g=g2d))
        out[n] = [r.reshape(shp) for r in res]
    return (loss, dx0[:T][None], *[out[n][0] for n in WEIGHTS], *[out[n][1] for n in WEIGHTS],
            *[out[n][2] for n in WEIGHTS], *[out[n][3] for n in WEIGHTS])


def kernel(x, c, ctx, c_ctx, norm1_g, norm2_g, w_ada, b_ada, ab_w_in, ab_b_in, a_ln_g, a_ln_b, a_w_s, a_b_s, b_conv_w, b_conv_b, b_ln_g, b_ln_b, ab_w_out, mla_w_in, mla_q_norm_g, mla_w_uq, mla_kv_norm_g, mla_w_ukv, mla_w_o, ffn_w_up, ffn_conv_w, ffn_conv_b, ffn_w_down, final_norm_g, loss_target, m_c_ctx, m_norm1_g, m_norm2_g, m_w_ada, m_b_ada, m_ab_w_in, m_ab_b_in, m_a_ln_g, m_a_ln_b, m_a_w_s, m_a_b_s, m_b_conv_w, m_b_conv_b, m_b_ln_g, m_b_ln_b, m_ab_w_out, m_mla_w_in, m_mla_q_norm_g, m_mla_w_uq, m_mla_kv_norm_g, m_mla_w_ukv, m_mla_w_o, m_ffn_w_up, m_ffn_conv_w, m_ffn_conv_b, m_ffn_w_down, m_final_norm_g, v_c_ctx, v_norm1_g, v_norm2_g, v_w_ada, v_b_ada, v_ab_w_in, v_ab_b_in, v_a_ln_g, v_a_ln_b, v_a_w_s, v_a_b_s, v_b_conv_w, v_b_conv_b, v_b_ln_g, v_b_ln_b, v_ab_w_out, v_mla_w_in, v_mla_q_norm_g, v_mla_w_uq, v_mla_kv_norm_g, v_mla_w_ukv, v_mla_w_o, v_ffn_w_up, v_ffn_conv_w, v_ffn_conv_b, v_ffn_w_down, v_final_norm_g):
    return _step(dict(locals()))
```
